```python
import math
import jax
import jax.numpy as jnp
from jax import lax
import numpy as np

D_MODEL = 2048
BATCH = 4
SEQ = 2048
DEPTH = 2
DEC_BATCH = 8
DEC_SEQ = 1
PAST_LEN = 16384
PAGE_SIZE = 128

D_MIX = D_MODEL
A_HEADS = 4
A_HALF = D_MODEL // 32
A_DHEAD = 2 * A_HALF
A_WIDTH = A_HEADS * A_DHEAD
B_WIDTH = D_MIX // 2
B_HEADDIM = 64
B_HEADS = B_WIDTH // B_HEADDIM
B_GROUPS = 4
B_STATE = 128
CONV_W = 4
B_CONV_DIM = B_WIDTH + 2 * B_GROUPS * B_STATE
SSD_CHUNK = 128
C_HEADS = 4
C_DHEAD = D_MIX // 16
C_WIDTH = C_HEADS * C_DHEAD
CMP_LEN = 32
CMP_STRIDE = 16
CMP_HID = C_DHEAD
SEL_BLOCK = 64
SEL_TOPK = 16
WINDOW = 512
PEER_HEADS = 8
PEER_NKEYS = 128
PEER_TOPK = 16
PEER_DQ = 256
PEER_EXPERTS = PEER_NKEYS * PEER_NKEYS
Q_BLOCK = 128
TOK_BLOCK = 128
NORM_EPS = 1e-6
NEG_INF = -1e30
FORCE_SCORE = 1e4
IN_DIM = 3 * A_WIDTH + B_WIDTH + B_CONV_DIM + B_HEADS + C_WIDTH + 6 * C_DHEAD + 3 * C_HEADS

kernel_name = 'hybrid_diffattn_ssd_nsa_peer_step'


def rmsnorm(x, g):
    xf = x.astype(jnp.float32)
    y = xf * lax.rsqrt(jnp.mean(xf * xf, axis=-1, keepdims=True) + NORM_EPS)
    return (y * g.astype(jnp.float32)).astype(x.dtype)


def alibi_slopes(n_heads):
    return jnp.asarray([2.0 ** (-8.0 * (h + 1) / n_heads) for h in range(n_heads)], dtype=jnp.float32)


def query_block(n):
    return Q_BLOCK if n % Q_BLOCK == 0 else n


def gather_pages(pool, page_table):
    rows = pool[page_table]
    return rows.reshape((rows.shape[0], rows.shape[1] * rows.shape[2]) + rows.shape[3:])


def diff_attention(q, k, v, q_pos, k_pos, lam, lam_init, subln_g, slopes):
    bsz, tq = q.shape[:2]
    blk = query_block(tq)
    nblk = tq // blk
    qb = jnp.moveaxis(q.reshape(bsz, nblk, blk, A_HEADS, 2, A_HALF), 1, 0)
    pb = q_pos.reshape(nblk, blk)
    scale = A_HALF ** -0.5

    def one_block(args):
        qi, pi = args
        s = jnp.einsum('bqhcd,bkhcd->bhcqk', qi, k).astype(jnp.float32) * scale
        dist = pi[:, None] - k_pos[None, :]
        s = s - slopes[None, :, None, None, None] * dist.astype(jnp.float32)[None, None, None]
        s = jnp.where((dist >= 0)[None, None, None], s, NEG_INF)
        p = jax.nn.softmax(s, axis=-1)
        pd = p[:, :, 0] - lam * p[:, :, 1]
        return jnp.einsum('bhqk,bkhd->bqhd', pd.astype(v.dtype), v)

    o = lax.map(one_block, (qb, pb))
    o = jnp.moveaxis(o, 0, 1).reshape(bsz, tq, A_HEADS, A_DHEAD)
    o = rmsnorm(o, subln_g) * (1.0 - lam_init)
    return o.reshape(bsz, tq, A_WIDTH)


def ssd_chunked(x, dt, a_neg, bm, cm, h0):
    bsz, seqlen = x.shape[:2]
    cs = SSD_CHUNK if seqlen % SSD_CHUNK == 0 else seqlen
    nc = seqlen // cs

    def chunks(t):
        return jnp.moveaxis(t.reshape((bsz, nc, cs) + t.shape[2:]), 1, 0)

    causal = jnp.tril(jnp.ones((cs, cs), dtype=bool))

    def step(h, inp):
        xc, dtc, bc, cc = inp
        acum = jnp.cumsum(dtc * a_neg, axis=1)
        seg = acum[:, :, None, :] - acum[:, None, :, :]
        decay = jnp.exp(jnp.where(causal[None, :, :, None], seg, NEG_INF))
        cb = jnp.einsum('bthn,bshn->btsh', cc, bc)
        y = jnp.einsum('btsh,bshp->bthp', cb * decay * dtc[:, None, :, :], xc)
        y = y + jnp.einsum('bthn,bhpn->bthp', cc, h) * jnp.exp(acum)[..., None]
        w = jnp.exp(acum[:, -1:, :] - acum) * dtc
        h_new = jnp.exp(acum[:, -1, :])[:, :, None, None] * h + jnp.einsum('bsh,bshp,bshn->bhpn', w, xc, bc)
        return h_new, y

    h, ys = lax.scan(step, h0, (chunks(x), chunks(dt), chunks(bm), chunks(cm)))
    return jnp.moveaxis(ys, 0, 1).reshape(x.shape), h


def ssd_mixer(z, xbc, dt_raw, conv_buf, conv_w, conv_b, dt_bias, a_log, d_skip, norm_g, h0):
    bsz, seqlen = z.shape[:2]
    full = jnp.concatenate([conv_buf.astype(xbc.dtype), xbc], axis=1)
    conv = conv_b + sum(full[:, i:i + seqlen] * conv_w[i] for i in range(CONV_W))
    xbc_c = jax.nn.silu(conv.astype(jnp.float32))
    new_buf = full[:, seqlen:]
    gn = B_GROUPS * B_STATE
    xs = xbc_c[..., :B_WIDTH].reshape(bsz, seqlen, B_HEADS, B_HEADDIM)
    bm = xbc_c[..., B_WIDTH:B_WIDTH + gn].reshape(bsz, seqlen, B_GROUPS, B_STATE)
    cm = xbc_c[..., B_WIDTH + gn:].reshape(bsz, seqlen, B_GROUPS, B_STATE)
    rep = B_HEADS // B_GROUPS
    bm = jnp.repeat(bm, rep, axis=2)
    cm = jnp.repeat(cm, rep, axis=2)
    dt = jax.nn.softplus(dt_raw.astype(jnp.float32) + dt_bias.astype(jnp.float32))
    a_neg = -jnp.exp(a_log.astype(jnp.float32))
    y, h = ssd_chunked(xs, dt, a_neg, bm, cm, h0.astype(jnp.float32))
    y = y + d_skip.astype(jnp.float32)[:, None] * xs
    y = y.reshape(bsz, seqlen, B_WIDTH) * jax.nn.silu(z.astype(jnp.float32))
    return rmsnorm(y, norm_g).astype(z.dtype), h, new_buf


def compress_rows(rows, pe, w1, w2):
    bsz, seqlen = rows.shape[:2]
    n_cmp = (seqlen - CMP_LEN) // CMP_STRIDE + 1
    idx = np.arange(n_cmp)[:, None] * CMP_STRIDE + np.arange(CMP_LEN)[None, :]
    blocks = rows[:, idx] + pe
    hid = jax.nn.gelu(blocks.reshape(bsz, n_cmp, CMP_LEN * C_DHEAD) @ w1)
    return hid @ w2


def cmp_to_sel_matrix(n_cmp, n_sel):
    start = np.arange(n_cmp)[:, None] * CMP_STRIDE
    jb = np.arange(n_sel)[None, :]
    lo = np.maximum(start, jb * SEL_BLOCK)
    hi = np.minimum(start + CMP_LEN, (jb + 1) * SEL_BLOCK)
    return jnp.asarray(np.clip(hi - lo, 0, None) / CMP_LEN, dtype=jnp.float32)


def nsa_mixer(q, kv_rows, win_rows, gate_raw, q_pos, win_pos0, pe, w1, w2, out_g, slopes):
    bsz, tq = q.shape[:2]
    seqlen = kv_rows.shape[1]
    scale = C_DHEAD ** -0.5
    q_start = int(q_pos[0])

    k_cmp = compress_rows(kv_rows[:, :, 0], pe[0], w1[0], w2[0])
    v_cmp = compress_rows(kv_rows[:, :, 1], pe[1], w1[1], w2[1])
    n_cmp = k_cmp.shape[1]
    cmp_end = np.arange(n_cmp) * CMP_STRIDE + CMP_LEN - 1
    dist_c = q_pos[:, None] - cmp_end[None, :]
    valid_c = (dist_c >= 0)[None, None]
    s = jnp.einsum('bqhd,bnd->bhqn', q, k_cmp).astype(jnp.float32) * scale
    s = s - slopes[None, :, None, None] * jnp.asarray(dist_c, jnp.float32)[None, None]
    p_cmp = jnp.where(valid_c, jax.nn.softmax(jnp.where(valid_c, s, NEG_INF), axis=-1), 0.0)
    o_cmp = jnp.einsum('bhqn,bnd->bqhd', p_cmp.astype(q.dtype), v_cmp)

    n_sel = -(-seqlen // SEL_BLOCK)
    imp = jnp.einsum('bhqn,nj->bqj', p_cmp, cmp_to_sel_matrix(n_cmp, n_sel))
    blk_id = np.arange(n_sel)[None, :]
    qblk = (q_pos // SEL_BLOCK)[:, None]
    sel_valid = blk_id <= qblk
    forced = (blk_id == 0) | (blk_id == qblk) | (blk_id == qblk - 1)
    score = jnp.where(sel_valid, imp + jnp.where(forced, FORCE_SCORE, 0.0), NEG_INF)
    k_eff = min(SEL_TOPK, n_sel)
    top_s, top_i = lax.top_k(score, k_eff)
    top_ok = top_s > NEG_INF / 2
    sel_rows = jnp.pad(kv_rows[:, :, 2:], ((0, 0), (0, n_sel * SEL_BLOCK - seqlen), (0, 0), (0, 0)))
    sel_rows = sel_rows.reshape(bsz, n_sel, SEL_BLOCK, 2, C_DHEAD)
    blk = query_block(tq)
    nblk = tq // blk
    qb = jnp.moveaxis(q.reshape(bsz, nblk, blk, C_HEADS, C_DHEAD), 1, 0)
    ib = jnp.moveaxis(top_i.reshape(bsz, nblk, blk, k_eff), 1, 0)
    okb = jnp.moveaxis(top_ok.reshape(bsz, nblk, blk, k_eff), 1, 0)
    pb = q_pos.reshape(nblk, blk)
    bidx = jnp.arange(bsz)[:, None, None]

    def sel_block(args):
        qi, ii, oki, pi = args
        g = sel_rows[bidx, ii]
        kpos = ii[..., None] * SEL_BLOCK + jnp.arange(SEL_BLOCK)
        dist = pi[None, :, None, None] - kpos
        ok = (dist >= 0) & oki[..., None]
        s_sel = jnp.einsum('bqhd,bqksd->bhqks', qi, g[..., 0, :]).astype(jnp.float32) * scale
        s_sel = s_sel - slopes[None, :, None, None, None] * dist[:, None].astype(jnp.float32)
        s_sel = jnp.where(ok[:, None], s_sel, NEG_INF).reshape(bsz, C_HEADS, blk, k_eff * SEL_BLOCK)
        p = jax.nn.softmax(s_sel, axis=-1)
        vg = g[..., 1, :].reshape(bsz, blk, k_eff * SEL_BLOCK, C_DHEAD)
        return jnp.einsum('bhqn,bqnd->bqhd', p.astype(qi.dtype), vg)

    o_sel = jnp.moveaxis(lax.map(sel_block, (qb, ib, okb, pb)), 0, 1).reshape(bsz, tq, C_HEADS, C_DHEAD)

    win_pad = jnp.pad(win_rows, ((0, 0), (WINDOW, 0), (0, 0), (0, 0)))
    band = WINDOW + blk - 1
    starts = q_start + np.arange(nblk) * blk - win_pos0 + 1
    idx = starts[:, None] + np.arange(band)[None, :]
    kpos = idx - WINDOW + win_pos0
    wk = win_pad[:, idx]
    dist_w = pb[:, :, None] - kpos[:, None, :]
    ok_w = ((dist_w >= 0) & (dist_w < WINDOW) & (kpos[:, None, :] >= win_pos0))[None, :, None]
    qw = q.reshape(bsz, nblk, blk, C_HEADS, C_DHEAD)
    s_w = jnp.einsum('bnqhd,bnkd->bnhqk', qw, wk[..., 0, :]).astype(jnp.float32) * scale
    s_w = s_w - slopes[None, None, :, None, None] * jnp.asarray(dist_w, jnp.float32)[None, :, None]
    p_w = jax.nn.softmax(jnp.where(ok_w, s_w, NEG_INF), axis=-1)
    o_win = jnp.einsum('bnhqk,bnkd->bnqhd', p_w.astype(q.dtype), wk[..., 1, :]).reshape(bsz, tq, C_HEADS, C_DHEAD)

    g = jax.nn.sigmoid(gate_raw.astype(jnp.float32)).reshape(bsz, tq, 3, C_HEADS)
    o = g[:, :, 0, :, None] * o_cmp + g[:, :, 1, :, None] * o_sel + g[:, :, 2, :, None] * o_win
    return rmsnorm(o.reshape(bsz, tq, C_WIDTH), out_g).astype(q.dtype)


def peer_ffn(x, wq, subkeys, u_tab, v_tab):
    bsz, t, d = x.shape
    n = bsz * t
    xt = x.reshape(n, d)
    q = (xt @ wq).reshape(n, PEER_HEADS, 2, PEER_DQ // 2)
    s = jnp.einsum('nhcd,hckd->nhck', q, subkeys).astype(jnp.float32)
    s1, i1 = lax.top_k(s[:, :, 0], PEER_TOPK)
    s2, i2 = lax.top_k(s[:, :, 1], PEER_TOPK)
    cand = (s1[..., :, None] + s2[..., None, :]).reshape(n, PEER_HEADS, PEER_TOPK * PEER_TOPK)
    cidx = (i1[..., :, None] * PEER_NKEYS + i2[..., None, :]).reshape(n, PEER_HEADS, PEER_TOPK * PEER_TOPK)
    top_s, pos = lax.top_k(cand, PEER_TOPK)
    eidx = jnp.take_along_axis(cidx, pos, axis=-1).reshape(n, PEER_HEADS * PEER_TOPK)
    gate = jax.nn.softmax(top_s, axis=-1).reshape(n, PEER_HEADS * PEER_TOPK)
    blk = TOK_BLOCK if n % TOK_BLOCK == 0 else n

    def expert_block(args):
        xb, eb, gb = args
        hid = jnp.einsum('nd,nkd->nk', xb, u_tab[eb]).astype(jnp.float32)
        act = gb * jax.nn.gelu(hid)
        return jnp.einsum('nk,nkd->nd', act.astype(xb.dtype), v_tab[eb])

    out = lax.map(expert_block, (xt.reshape(n // blk, blk, d), eidx.reshape(n // blk, blk, -1), gate.reshape(n // blk, blk, -1)))
    return out.reshape(bsz, t, d)


def hybrid_layer(x, q_start, diff_past, nsa_past, win_buf, conv_buf, h0, layer_idx, p):
    bsz, t = x.shape[:2]
    q_pos = q_start + np.arange(t)
    k_pos = np.arange(q_start + t)
    hn = rmsnorm(x, p['norm_mix_g'])
    proj = hn @ p['w_in']
    sizes = [A_WIDTH, A_WIDTH, A_WIDTH, B_WIDTH, B_CONV_DIM, B_HEADS, C_WIDTH, 6 * C_DHEAD, 3 * C_HEADS]
    cuts = [int(c) for c in np.cumsum(sizes)[:-1]]
    a_q, a_k, a_v, b_z, b_xbc, b_dt, c_q, c_kv, c_gate = jnp.split(proj, cuts, axis=-1)

    diff_new = jnp.stack([a_k.reshape(bsz, t, A_HEADS, A_DHEAD), a_v.reshape(bsz, t, A_HEADS, A_DHEAD)], axis=2)
    diff_all = jnp.concatenate([diff_past.astype(diff_new.dtype), diff_new], axis=1)
    lam_init = 0.8 - 0.6 * math.exp(-0.3 * layer_idx)
    dl = p['diff_lam'].astype(jnp.float32)
    lam = jnp.exp(jnp.sum(dl[0] * dl[1])) - jnp.exp(jnp.sum(dl[2] * dl[3])) + lam_init
    o_a = diff_attention(a_q.reshape(bsz, t, A_HEADS, 2, A_HALF),
                         diff_all[:, :, 0].reshape(bsz, -1, A_HEADS, 2, A_HALF), diff_all[:, :, 1],
                         q_pos, k_pos, lam, lam_init, p['diff_subln_g'], alibi_slopes(A_HEADS))

    o_b, h_new, conv_new = ssd_mixer(b_z, b_xbc, b_dt, conv_buf, p['ssm_conv_w'], p['ssm_conv_b'], p['ssm_dt_bias'],
                                     p['ssm_a_log'], p['ssm_d'], p['ssm_norm_g'], h0)

    nsa_new = c_kv[..., :4 * C_DHEAD].reshape(bsz, t, 4, C_DHEAD)
    win_new = c_kv[..., 4 * C_DHEAD:].reshape(bsz, t, 2, C_DHEAD)
    nsa_all = jnp.concatenate([nsa_past.astype(nsa_new.dtype), nsa_new], axis=1)
    win_all = jnp.concatenate([win_buf.astype(win_new.dtype), win_new], axis=1)
    o_c = nsa_mixer(c_q.reshape(bsz, t, C_HEADS, C_DHEAD), nsa_all, win_all, c_gate, q_pos,
                    q_start - win_buf.shape[1], p['nsa_pe'], p['nsa_cmp_w1'], p['nsa_cmp_w2'], p['nsa_out_g'],
                    alibi_slopes(C_HEADS))

    mixed = jnp.concatenate([o_a, o_b.astype(o_a.dtype), o_c.astype(o_a.dtype)], axis=-1) @ p['w_out']
    h = x + mixed
    y = h + peer_ffn(rmsnorm(h, p['norm_ffn_g']), p['peer_wq'], p['peer_subkeys'], p['peer_u'], p['peer_v'])
    keep = min(WINDOW, win_all.shape[1])
    return y, (diff_new, nsa_new, win_all[:, win_all.shape[1] - keep:], h_new, conv_new)


def setup_inputs(seed: int = 0) -> dict:
    key = jax.random.key(seed)
    ks = jax.random.split(key, 32)
    f32 = jnp.float32
    n_pages = PAST_LEN // PAGE_SIZE
    n_pool = (DEC_BATCH * n_pages * 5) // 4
    win_buf = min(WINDOW, PAST_LEN)

    def nrm(k, shape, scale):
        return jax.random.normal(k, shape, f32) * scale

    dt0 = jnp.exp(jax.random.uniform(ks[15], (DEPTH, B_HEADS), f32, math.log(1e-3), math.log(1e-1)))
    return {
        'x_prompt': nrm(ks[0], (BATCH, SEQ, D_MODEL), 1.0),
        'x_sample': nrm(ks[1], (DEC_BATCH, DEC_SEQ, D_MODEL), 1.0),
        'cache_diff_kv': nrm(ks[2], (DEPTH, n_pool, PAGE_SIZE, 2, A_HEADS, A_DHEAD), 1.0),
        'cache_nsa_kv': nrm(ks[3], (DEPTH, n_pool, PAGE_SIZE, 4, C_DHEAD), 1.0),
        'cache_nsa_win': nrm(ks[4], (DEPTH, DEC_BATCH, win_buf, 2, C_DHEAD), 1.0),
        'state_ssm': nrm(ks[5], (DEPTH, DEC_BATCH, B_HEADS, B_HEADDIM, B_STATE), 0.1),
        'state_conv': nrm(ks[6], (DEPTH, DEC_BATCH, CONV_W - 1, B_CONV_DIM), 1.0),
        'page_table': jax.random.permutation(ks[7], n_pool)[:DEC_BATCH * n_pages].reshape(DEC_BATCH, n_pages).astype(jnp.int32),
        'norm_mix_g': 1.0 + nrm(ks[8], (DEPTH, D_MODEL), 0.01),
        'w_in': nrm(ks[9], (DEPTH, D_MODEL, IN_DIM), D_MODEL ** -0.5),
        'w_out': nrm(ks[10], (DEPTH, D_MIX, D_MODEL), D_MIX ** -0.5),
        'diff_lam': nrm(ks[11], (DEPTH, 4, A_HALF), 0.1),
        'diff_subln_g': 1.0 + nrm(ks[12], (DEPTH, A_DHEAD), 0.01),
        'ssm_conv_w': nrm(ks[13], (DEPTH, CONV_W, B_CONV_DIM), CONV_W ** -0.5),
        'ssm_conv_b': nrm(ks[14], (DEPTH, B_CONV_DIM), 0.01),
        'ssm_dt_bias': dt0 + jnp.log(-jnp.expm1(-dt0)),
        'ssm_a_log': jnp.log(jax.random.uniform(ks[16], (DEPTH, B_HEADS), f32, 1.0, 16.0)),
        'ssm_d': 1.0 + nrm(ks[17], (DEPTH, B_HEADS), 0.01),
        'ssm_norm_g': 1.0 + nrm(ks[18], (DEPTH, B_WIDTH), 0.01),
        'nsa_pe': nrm(ks[19], (DEPTH, 2, CMP_LEN, C_DHEAD), 0.02),
        'nsa_cmp_w1': nrm(ks[20], (DEPTH, 2, CMP_LEN * C_DHEAD, CMP_HID), (CMP_LEN * C_DHEAD) ** -0.5),
        'nsa_cmp_w2': nrm(ks[21], (DEPTH, 2, CMP_HID, C_DHEAD), CMP_HID ** -0.5),
        'nsa_out_g': 1.0 + nrm(ks[22], (DEPTH, C_WIDTH), 0.01),
        'norm_ffn_g': 1.0 + nrm(ks[23], (DEPTH, D_MODEL), 0.01),
        'peer_wq': nrm(ks[24], (DEPTH, D_MODEL, PEER_HEADS * PEER_DQ), D_MODEL ** -0.5),
        'peer_subkeys': nrm(ks[25], (DEPTH, PEER_HEADS, 2, PEER_NKEYS, PEER_DQ // 2), (PEER_DQ // 2) ** -0.5),
        'peer_u': nrm(ks[26], (DEPTH, PEER_EXPERTS, D_MODEL), D_MODEL ** -0.5),
        'peer_v': nrm(ks[27], (DEPTH, PEER_EXPERTS, D_MODEL), PEER_HEADS ** -0.5),
        'norm_final_g': 1.0 + nrm(ks[28], (D_MODEL,), 0.01),
    }


def reference(x_prompt, x_sample, cache_diff_kv, cache_nsa_kv, cache_nsa_win, state_ssm, state_conv, page_table,
              norm_mix_g, w_in, w_out, diff_lam, diff_subln_g, ssm_conv_w, ssm_conv_b, ssm_dt_bias, ssm_a_log,
              ssm_d, ssm_norm_g, nsa_pe, nsa_cmp_w1, nsa_cmp_w2, nsa_out_g, norm_ffn_g, peer_wq, peer_subkeys,
              peer_u, peer_v, norm_final_g):
    past_len = page_table.shape[1] * PAGE_SIZE
    xp = x_prompt
    xs = x_sample
    bp = x_prompt.shape[0]
    st_prompt = [[], [], [], [], []]
    st_sample = [[], [], [], [], []]
    for l in range(DEPTH):
        p = {'norm_mix_g': norm_mix_g[l], 'w_in': w_in[l], 'w_out': w_out[l], 'diff_lam': diff_lam[l],
             'diff_subln_g': diff_subln_g[l], 'ssm_conv_w': ssm_conv_w[l], 'ssm_conv_b': ssm_conv_b[l],
             'ssm_dt_bias': ssm_dt_bias[l], 'ssm_a_log': ssm_a_log[l], 'ssm_d': ssm_d[l],
             'ssm_norm_g': ssm_norm_g[l], 'nsa_pe': nsa_pe[l], 'nsa_cmp_w1': nsa_cmp_w1[l],
             'nsa_cmp_w2': nsa_cmp_w2[l], 'nsa_out_g': nsa_out_g[l], 'norm_ffn_g': norm_ffn_g[l],
             'peer_wq': peer_wq[l], 'peer_subkeys': peer_subkeys[l], 'peer_u': peer_u[l], 'peer_v': peer_v[l]}
        xp, sp = hybrid_layer(xp, 0,
                              jnp.zeros((bp, 0, 2, A_HEADS, A_DHEAD), xp.dtype),
                              jnp.zeros((bp, 0, 4, C_DHEAD), xp.dtype),
                              jnp.zeros((bp, 0, 2, C_DHEAD), xp.dtype),
                              jnp.zeros((bp, CONV_W - 1, B_CONV_DIM), xp.dtype),
                              jnp.zeros((bp, B_HEADS, B_HEADDIM, B_STATE), jnp.float32), l, p)
        xs, ss = hybrid_layer(xs, past_len,
                              gather_pages(cache_diff_kv[l], page_table),
                              gather_pages(cache_nsa_kv[l], page_table),
                              cache_nsa_win[l], state_conv[l], state_ssm[l], l, p)
        for i in range(5):
            st_prompt[i].append(sp[i])
            st_sample[i].append(ss[i])
    y_prompt = rmsnorm(xp, norm_final_g)
    y_sample = rmsnorm(xs, norm_final_g)
    diff_kv_prompt = jnp.stack(st_prompt[0])
    nsa_kv_prompt = jnp.stack(st_prompt[1])
    nsa_win_prompt = jnp.stack(st_prompt[2])
    ssm_prompt = jnp.stack(st_prompt[3])
    conv_prompt = jnp.stack(st_prompt[4])
    diff_kv_sample = jnp.stack(st_sample[0])
    nsa_kv_sample = jnp.stack(st_sample[1])
    nsa_win_sample = jnp.stack(st_sample[2])
    ssm_sample = jnp.stack(st_sample[3])
    conv_sample = jnp.stack(st_sample[4])
    return (y_prompt, y_sample, diff_kv_prompt, nsa_kv_prompt, nsa_win_prompt, ssm_prompt, conv_prompt,
            diff_kv_sample, nsa_kv_sample, nsa_win_sample, ssm_sample, conv_sample)
```

```python
import functools
import math

import jax
import jax.numpy as jnp
import numpy as np
from jax import lax
from jax.experimental import pallas as pl
from jax.experimental.pallas import tpu as pltpu

F32 = jnp.float32
BF16 = jnp.bfloat16

D_MODEL = 2048
A_HEADS = 4
A_HALF = 64
A_DHEAD = 128
A_WIDTH = 512
B_WIDTH = 1024
B_HEADDIM = 64
B_HEADS = 16
B_GROUPS = 4
B_STATE = 128
CONV_W = 4
B_CONV_DIM = 2048
SSD_CHUNK = 128
C_HEADS = 4
C_DHEAD = 128
C_WIDTH = 512
CMP_LEN = 32
CMP_STRIDE = 16
SEL_BLOCK = 64
SEL_TOPK = 16
WINDOW = 512
PEER_HEADS = 8
PEER_NKEYS = 128
PEER_TOPK = 16
PEER_DQ = 256
PAGE_SIZE = 128
NORM_EPS = 1e-6
NEG_INF = -1e30
FORCE_SCORE = 1e4

LANES = 128
VMEM_LIMIT = 56 * 1024 * 1024

COL_XBC = 0
COL_Z = 2048
COL_AQ = 3072
COL_AK = 3584
COL_AV = 4096
COL_CQ = 4608
COL_CKV = 5120
COL_WIN = 5632
COL_SMALL = 5888
PROJ_W = 6144
GATE_LANE0 = B_HEADS

NT_DIMS = (((1,), (1,)), ((), ()))


def _cparams(sem, vmem=VMEM_LIMIT):
    return pltpu.CompilerParams(dimension_semantics=sem, vmem_limit_bytes=vmem)


def _gelu(x):
    return 0.5 * x * (1.0 + jnp.tanh(math.sqrt(2.0 / math.pi) * (x + 0.044715 * (x * x * x))))


def _silu(x):
    return x * (1.0 / (1.0 + jnp.exp(-x)))


def _softplus(x):
    return jnp.maximum(x, 0.0) + jnp.log(1.0 + jnp.exp(-jnp.abs(x)))


def _alibi_slope(h):
    if isinstance(h, int):
        return 2.0 ** (-2.0 * (h + 1))
    return jnp.exp2(jnp.full((1, 1), -2.0, F32) * (h + 1).astype(F32))


def _proj_kernel(*refs, two):
    if two:
        xa_ref, xb_ref, g_ref, w_ref, o_ref, xn_ref = refs
    else:
        xa_ref, g_ref, w_ref, o_ref, xn_ref = refs

    @pl.when(pl.program_id(1) == 0)
    def _():
        x = xa_ref[...]
        if two:
            x = x + xb_ref[...]
        ms = jnp.mean(x * x, axis=-1, keepdims=True)
        xn_ref[...] = (x * lax.rsqrt(ms + NORM_EPS) * g_ref[...]).astype(BF16)

    o_ref[...] = jnp.dot(xn_ref[...], w_ref[...], preferred_element_type=F32)


def _proj_call(xa, xb, g, w_r, tm, tc=1536):
    n = xa.shape[0]
    two = xb is not None
    xspec = pl.BlockSpec((tm, D_MODEL), lambda i, j: (i, 0))
    ins = [xa] + ([xb] if two else []) + [g.reshape(1, D_MODEL), w_r]
    specs = [xspec] + ([xspec] if two else []) + [
        pl.BlockSpec((1, D_MODEL), lambda i, j: (0, 0)),
        pl.BlockSpec((D_MODEL, tc), lambda i, j: (0, j)),
    ]
    return pl.pallas_call(
        functools.partial(_proj_kernel, two=two),
        grid=(n // tm, PROJ_W // tc),
        in_specs=specs,
        out_specs=pl.BlockSpec((tm, tc), lambda i, j: (i, j)),
        out_shape=jax.ShapeDtypeStruct((n, PROJ_W), F32),
        scratch_shapes=[pltpu.VMEM((tm, D_MODEL), BF16)],
        compiler_params=_cparams(("parallel", "arbitrary")),
        name="in_proj",
    )(*ins)


def _diff_lambda(dl, lam_init):
    a = jnp.sum(dl[0:1] * dl[1:2], axis=-1, keepdims=True)
    b = jnp.sum(dl[2:3] * dl[3:4], axis=-1, keepdims=True)
    return jnp.exp(a) - jnp.exp(b) + lam_init


def _diff_prompt_kernel(q_ref, k_ref, v_ref, dl_ref, g_ref, o_ref, *, tq, lam_init):
    h = pl.program_id(1)
    i = pl.program_id(2)
    t = k_ref.shape[0]
    scale = A_HALF ** -0.5
    lam = _diff_lambda(dl_ref[...], lam_init)
    q = q_ref[...]
    lane = lax.broadcasted_iota(jnp.int32, (1, A_DHEAD), 1)
    kb = k_ref[...].astype(BF16)
    vb = v_ref[...].astype(BF16)
    qpos = i * tq + lax.broadcasted_iota(jnp.int32, (tq, 1), 0)
    kpos = lax.broadcasted_iota(jnp.int32, (1, t), 1)
    dist = qpos - kpos
    ok = dist >= 0
    bias = _alibi_slope(h) * dist.astype(F32)

    def half_softmax(c):
        qc = jnp.where((lane >= c * A_HALF) & (lane < (c + 1) * A_HALF), q, 0.0).astype(BF16)
        s = lax.dot_general(qc, kb, NT_DIMS, preferred_element_type=F32) * scale
        s = jnp.where(ok, s - bias, NEG_INF)
        m = jnp.max(s, axis=-1, keepdims=True)
        e = jnp.exp(s - m)
        return e / jnp.sum(e, axis=-1, keepdims=True)

    pd = half_softmax(0) - lam * half_softmax(1)
    o = jnp.dot(pd.astype(BF16), vb, preferred_element_type=F32)
    ms = jnp.mean(o * o, axis=-1, keepdims=True)
    o_ref[...] = o * lax.rsqrt(ms + NORM_EPS) * g_ref[...] * (1.0 - lam_init)


def _diff_prompt_call(proj, bsz, t, dl, subln_g, lam_init, tq=256):
    nq = t // tq
    cq, ck, cv = COL_AQ // A_DHEAD, COL_AK // A_DHEAD, COL_AV // A_DHEAD
    return pl.pallas_call(
        functools.partial(_diff_prompt_kernel, tq=tq, lam_init=lam_init),
        grid=(bsz, A_HEADS, nq),
        in_specs=[
            pl.BlockSpec((tq, A_DHEAD), lambda b, h, i: (b * nq + i, cq + h)),
            pl.BlockSpec((t, A_DHEAD), lambda b, h, i: (b, ck + h)),
            pl.BlockSpec((t, A_DHEAD), lambda b, h, i: (b, cv + h)),
            pl.BlockSpec((4, A_HALF), lambda b, h, i: (0, 0)),
            pl.BlockSpec((1, A_DHEAD), lambda b, h, i: (0, 0)),
        ],
        out_specs=pl.BlockSpec((tq, A_DHEAD), lambda b, h, i: (b * nq + i, h)),
        out_shape=jax.ShapeDtypeStruct((bsz * t, A_WIDTH), F32),
        compiler_params=_cparams(("parallel", "parallel", "arbitrary")),
        name="diff_attn_prompt",
    )(proj, proj, proj, dl, subln_g.reshape(1, A_DHEAD))


def _ssd_prompt_kernel(xbc_ref, z_ref, sm_ref, cw_ref, cb_ref, dtb_ref, alog_ref, dsk_ref, g_ref,
                       o_ref, hout_ref, buf_ref, h_ref):
    c = pl.program_id(1)
    cs = SSD_CHUNK

    @pl.when(c == 0)
    def _():
        buf_ref[0:8, :] = jnp.zeros((8, B_CONV_DIM), F32)
        h_ref[...] = jnp.zeros_like(h_ref)

    xbc = xbc_ref[...]
    buf_ref[8:8 + cs, :] = xbc
    cw = cw_ref[...]
    conv = cb_ref[...] + cw[3:4] * xbc
    for j in range(1, CONV_W):
        conv = conv + cw[3 - j:4 - j] * buf_ref[8 - j:8 - j + cs, :]
    buf_ref[0:8, :] = xbc[cs - 8:cs, :]
    xc = _silu(conv)
    xs = xc[:, :B_WIDTH]

    dt = _softplus(sm_ref[...] + dtb_ref[...])
    a_neg = -jnp.exp(alog_ref[...])
    dta = dt * a_neg
    row = lax.broadcasted_iota(jnp.int32, (cs, cs), 0)
    col = lax.broadcasted_iota(jnp.int32, (cs, cs), 1)
    causal = row >= col
    acum = jnp.dot(causal.astype(F32), dta, preferred_element_type=F32, precision=lax.Precision.HIGHEST)
    acum_t = acum.T
    dt_t = dt.T
    lane = lax.broadcasted_iota(jnp.int32, (1, LANES), 1)
    lo = lane < B_HEADDIM

    ys = []
    for g in range(B_GROUPS):
        bg = xc[:, B_WIDTH + g * B_STATE:B_WIDTH + (g + 1) * B_STATE]
        cg = xc[:, B_WIDTH + B_GROUPS * B_STATE + g * B_STATE:B_WIDTH + B_GROUPS * B_STATE + (g + 1) * B_STATE]
        bgb = bg.astype(BF16)
        cgb = cg.astype(BF16)
        cb = lax.dot_general(cgb, bgb, NT_DIMS, preferred_element_type=F32)
        for pr in range(2):
            h0 = g * 4 + pr * 2
            xpair = xs[:, h0 * B_HEADDIM:(h0 + 2) * B_HEADDIM]
            xpb = xpair.astype(BF16)
            ydiag = []
            ecol = []
            wcol = []
            elast = []
            for hh in (h0, h0 + 1):
                a_col = acum[:, hh:hh + 1]
                a_row = acum_t[hh:hh + 1, :]
                decay = jnp.exp(jnp.where(causal, a_col - a_row, NEG_INF))
                lm = cb * decay * dt_t[hh:hh + 1, :]
                ydiag.append(jnp.dot(lm.astype(BF16), xpb, preferred_element_type=F32))
                a_last = acum[cs - 1:cs, hh:hh + 1]
                ecol.append(jnp.exp(a_col))
                wcol.append(jnp.exp(a_last - a_col) * dt[:, hh:hh + 1])
                elast.append(jnp.exp(a_last))
            hp = h_ref[h0 * B_HEADDIM:(h0 + 2) * B_HEADDIM, :]
            yoff = lax.dot_general(cgb, hp.astype(BF16), NT_DIMS, preferred_element_type=F32)
            y = jnp.where(lo, ydiag[0], ydiag[1]) + yoff * jnp.where(lo, ecol[0], ecol[1])
            ys.append(y)
            wx = xpair * jnp.where(lo, wcol[0], wcol[1])
            upd = jnp.dot(wx.T.astype(BF16), bgb, preferred_element_type=F32)
            prow = lax.broadcasted_iota(jnp.int32, (LANES, 1), 0) < B_HEADDIM
            h_ref[h0 * B_HEADDIM:(h0 + 2) * B_HEADDIM, :] = jnp.where(prow, elast[0], elast[1]) * hp + upd

    y = jnp.concatenate(ys, axis=1)
    y = y + dsk_ref[...] * xs
    y = y * _silu(z_ref[...])
    ms = jnp.mean(y * y, axis=-1, keepdims=True)
    o_ref[...] = y * lax.rsqrt(ms + NORM_EPS) * g_ref[...]

    @pl.when(c == pl.num_programs(1) - 1)
    def _():
        hout_ref[0] = h_ref[...]


def _pad_lanes(v, fill=0.0):
    v = v.reshape(1, -1).astype(F32)
    return jnp.pad(v, ((0, 0), (0, LANES - v.shape[1])), constant_values=fill)


def _ssd_prompt_call(proj, bsz, t, conv_w, conv_b, dt_bias, a_log, d_skip, norm_g):
    nc = t // SSD_CHUNK
    cs = SSD_CHUNK
    const = lambda b, c: (0, 0)
    o, hout = pl.pallas_call(
        _ssd_prompt_kernel,
        grid=(bsz, nc),
        in_specs=[
            pl.BlockSpec((cs, B_CONV_DIM), lambda b, c: (b * nc + c, COL_XBC // B_CONV_DIM)),
            pl.BlockSpec((cs, B_WIDTH), lambda b, c: (b * nc + c, COL_Z // B_WIDTH)),
            pl.BlockSpec((cs, LANES), lambda b, c: (b * nc + c, COL_SMALL // LANES)),
            pl.BlockSpec((CONV_W, B_CONV_DIM), const),
            pl.BlockSpec((1, B_CONV_DIM), const),
            pl.BlockSpec((1, LANES), const),
            pl.BlockSpec((1, LANES), const),
            pl.BlockSpec((1, B_WIDTH), const),
            pl.BlockSpec((1, B_WIDTH), const),
        ],
        out_specs=[
            pl.BlockSpec((cs, B_WIDTH), lambda b, c: (b * nc + c, 0)),
            pl.BlockSpec((1, B_HEADS * B_HEADDIM, B_STATE), lambda b, c: (b, 0, 0)),
        ],
        out_shape=[
            jax.ShapeDtypeStruct((bsz * t, B_WIDTH), F32),
            jax.ShapeDtypeStruct((bsz, B_HEADS * B_HEADDIM, B_STATE), F32),
        ],
        scratch_shapes=[pltpu.VMEM((8 + cs, B_CONV_DIM), F32), pltpu.VMEM((B_HEADS * B_HEADDIM, B_STATE), F32)],
        compiler_params=_cparams(("parallel", "arbitrary")),
        name="ssd_prompt",
    )(proj, proj, proj, conv_w, conv_b.reshape(1, -1), _pad_lanes(dt_bias), _pad_lanes(a_log),
      jnp.repeat(d_skip, B_HEADDIM).reshape(1, B_WIDTH), norm_g.reshape(1, B_WIDTH))
    return o, hout.reshape(bsz, B_HEADS, B_HEADDIM, B_STATE)


def _compress_kernel(k_ref, v_ref, pe_ref, w1_ref, w2_ref, kc_ref, vc_ref, *, nchunk):
    outs = []
    for kv, rows_ref in enumerate((k_ref, v_ref)):
        acc_lo = jnp.zeros((nchunk, C_DHEAD), F32)
        acc_hi = jnp.zeros((nchunk, C_DHEAD), F32)
        for r in range(CMP_STRIDE):
            x = rows_ref[pl.ds(r, nchunk, stride=CMP_STRIDE), :]
            x_lo = (x + pe_ref[kv, r:r + 1, :]).astype(BF16)
            x_hi = (x + pe_ref[kv, CMP_STRIDE + r:CMP_STRIDE + r + 1, :]).astype(BF16)
            acc_lo = acc_lo + jnp.dot(x_lo, w1_ref[kv, r].astype(BF16), preferred_element_type=F32)
            acc_hi = acc_hi + jnp.dot(x_hi, w1_ref[kv, CMP_STRIDE + r].astype(BF16), preferred_element_type=F32)
        hid = acc_lo + pltpu.roll(acc_hi, nchunk - 1, 0)
        outs.append(jnp.dot(_gelu(hid).astype(BF16), w2_ref[kv].astype(BF16), preferred_element_type=F32))
    kc_ref[0] = outs[0]
    vc_ref[0] = outs[1]


def _compress_prompt_call(proj, bsz, t, pe, w1, w2):
    nchunk = t // CMP_STRIDE
    shp = jax.ShapeDtypeStruct((bsz, nchunk, C_DHEAD), F32)
    return pl.pallas_call(
        functools.partial(_compress_kernel, nchunk=nchunk),
        grid=(bsz,),
        in_specs=[
            pl.BlockSpec((t, C_DHEAD), lambda b: (b, COL_CKV // C_DHEAD)),
            pl.BlockSpec((t, C_DHEAD), lambda b: (b, COL_CKV // C_DHEAD + 1)),
            pl.BlockSpec((2, CMP_LEN, C_DHEAD), lambda b: (0, 0, 0)),
            pl.BlockSpec((2, CMP_LEN, C_DHEAD, C_DHEAD), lambda b: (0, 0, 0, 0)),
            pl.BlockSpec((2, C_DHEAD, C_DHEAD), lambda b: (0, 0, 0)),
        ],
        out_specs=[pl.BlockSpec((1, nchunk, C_DHEAD), lambda b: (b, 0, 0))] * 2,
        out_shape=[shp, shp],
        compiler_params=_cparams(("parallel",)),
        name="nsa_compress_prompt",
    )(proj, proj, pe, w1.reshape(2, CMP_LEN, C_DHEAD, C_DHEAD), w2)


def _masked_softmax(s, ok):
    s = jnp.where(ok, s, NEG_INF)
    m = jnp.max(s, axis=-1, keepdims=True)
    e = jnp.where(ok, jnp.exp(s - m), 0.0)
    return e, jnp.sum(e, axis=-1, keepdims=True)


def _topk_mask_lanes(score, k, n):
    lane = lax.broadcasted_iota(jnp.int32, (1, LANES), 1)
    rank = jnp.zeros(score.shape, F32)
    for i in range(n):
        ci = score[:, i:i + 1]
        beats = (ci > score) | ((ci == score) & (lane > i))
        rank = rank + jnp.where(beats, 1.0, 0.0)
    return (rank < k) & (lane < n)


def _nsa_prompt_kernel(q_ref, ks_ref, vs_ref, kw_ref, vw_ref, kc_ref, vc_ref, sm_ref, g_ref, o_ref, *, tq, n_cmp):
    i = pl.program_id(1)
    t = ks_ref.shape[0]
    n_sel = t // SEL_BLOCK
    scale = C_DHEAD ** -0.5
    qpos = i * tq + lax.broadcasted_iota(jnp.int32, (tq, 1), 0)
    kpos = lax.broadcasted_iota(jnp.int32, (1, t), 1)
    dist = qpos - kpos
    dist_f = dist.astype(F32)
    lane = lax.broadcasted_iota(jnp.int32, (1, LANES), 1)

    cmp_end = lane * CMP_STRIDE + (CMP_LEN - 1)
    dist_c = qpos - cmp_end
    ok_c = (dist_c >= 0) & (lane < n_cmp)
    dist_cf = dist_c.astype(F32)
    kcb = kc_ref[0].astype(BF16)
    vcb = vc_ref[0].astype(BF16)
    qs = [q_ref[:, h * C_DHEAD:(h + 1) * C_DHEAD].astype(BF16) for h in range(C_HEADS)]
    o_cmp = []
    psum = jnp.zeros((tq, LANES), F32)
    for h in range(C_HEADS):
        s = lax.dot_general(qs[h], kcb, NT_DIMS, preferred_element_type=F32) * scale
        s = s - _alibi_slope(h) * dist_cf
        e, den = _masked_softmax(s, ok_c)
        p = e / jnp.maximum(den, 1e-30)
        psum = psum + p
        o_cmp.append(jnp.dot(p.astype(BF16), vcb, preferred_element_type=F32))

    n_i = lax.broadcasted_iota(jnp.int32, (LANES, LANES), 0)
    j_i = lax.broadcasted_iota(jnp.int32, (LANES, LANES), 1)
    lo_ = jnp.maximum(n_i * CMP_STRIDE, j_i * SEL_BLOCK)
    hi_ = jnp.minimum(n_i * CMP_STRIDE + CMP_LEN, (j_i + 1) * SEL_BLOCK)
    ovl = jnp.maximum(hi_ - lo_, 0).astype(F32) * (1.0 / CMP_LEN)
    ovl = jnp.where((n_i < n_cmp) & (j_i < n_sel), ovl, 0.0)
    imp = jnp.dot(psum, ovl, preferred_element_type=F32, precision=lax.Precision.HIGHEST)
    sel_shift = SEL_BLOCK.bit_length() - 1
    qblk = qpos >> sel_shift
    sel_valid = lane <= qblk
    forced = (lane == 0) | (lane == qblk) | (lane == qblk - 1)
    score = jnp.where(sel_valid, imp + jnp.where(forced, FORCE_SCORE, 0.0), NEG_INF)
    score = jnp.where(lane < n_sel, score, -jnp.inf)
    chosen = _topk_mask_lanes(score, min(SEL_TOPK, n_sel), n_sel) & sel_valid
    e_j = lax.broadcasted_iota(jnp.int32, (LANES, t), 0)
    e_k = lax.broadcasted_iota(jnp.int32, (LANES, t), 1)
    expand = jnp.where((e_k >> sel_shift) == e_j, 1.0, 0.0).astype(BF16)
    key_sel = jnp.dot(jnp.where(chosen, 1.0, 0.0).astype(BF16), expand, preferred_element_type=F32) > 0.5
    ok_s = key_sel & (dist >= 0)
    ok_w = (dist >= 0) & (dist < WINDOW)

    ksb = ks_ref[...].astype(BF16)
    vsb = vs_ref[...].astype(BF16)
    kwb = kw_ref[...].astype(BF16)
    vwb = vw_ref[...].astype(BF16)
    gate = 1.0 / (1.0 + jnp.exp(-sm_ref[...]))
    outs = []
    for h in range(C_HEADS):
        bias = _alibi_slope(h) * dist_f
        s = lax.dot_general(qs[h], ksb, NT_DIMS, preferred_element_type=F32) * scale - bias
        e, den = _masked_softmax(s, ok_s)
        o_sel = jnp.dot((e / den).astype(BF16), vsb, preferred_element_type=F32)
        s = lax.dot_general(qs[h], kwb, NT_DIMS, preferred_element_type=F32) * scale - bias
        e, den = _masked_softmax(s, ok_w)
        o_win = jnp.dot((e / den).astype(BF16), vwb, preferred_element_type=F32)
        g0 = gate[:, GATE_LANE0 + h:GATE_LANE0 + h + 1]
        g1 = gate[:, GATE_LANE0 + C_HEADS + h:GATE_LANE0 + C_HEADS + h + 1]
        g2 = gate[:, GATE_LANE0 + 2 * C_HEADS + h:GATE_LANE0 + 2 * C_HEADS + h + 1]
        outs.append(g0 * o_cmp[h] + g1 * o_sel + g2 * o_win)
    o = jnp.concatenate(outs, axis=1)
    ms = jnp.mean(o * o, axis=-1, keepdims=True)
    o_ref[...] = o * lax.rsqrt(ms + NORM_EPS) * g_ref[...]


def _nsa_prompt_call(proj, kc, vc, bsz, t, out_g, tq=256):
    nq = t // tq
    n_cmp = (t - CMP_LEN) // CMP_STRIDE + 1
    c0 = COL_CKV // C_DHEAD
    w0 = COL_WIN // C_DHEAD
    kvspec = lambda col: pl.BlockSpec((t, C_DHEAD), lambda b, i: (b, col))
    return pl.pallas_call(
        functools.partial(_nsa_prompt_kernel, tq=tq, n_cmp=n_cmp),
        grid=(bsz, nq),
        in_specs=[
            pl.BlockSpec((tq, C_WIDTH), lambda b, i: (b * nq + i, COL_CQ // C_WIDTH)),
            kvspec(c0 + 2), kvspec(c0 + 3), kvspec(w0), kvspec(w0 + 1),
            pl.BlockSpec((1, kc.shape[1], C_DHEAD), lambda b, i: (b, 0, 0)),
            pl.BlockSpec((1, kc.shape[1], C_DHEAD), lambda b, i: (b, 0, 0)),
            pl.BlockSpec((tq, LANES), lambda b, i: (b * nq + i, COL_SMALL // LANES)),
            pl.BlockSpec((1, C_WIDTH), lambda b, i: (0, 0)),
        ],
        out_specs=pl.BlockSpec((tq, C_WIDTH), lambda b, i: (b * nq + i, 0)),
        out_shape=jax.ShapeDtypeStruct((bsz * t, C_WIDTH), F32),
        compiler_params=_cparams(("parallel", "arbitrary")),
        name="nsa_attn_prompt",
    )(proj, proj, proj, proj, proj, kc, vc, proj, out_g.reshape(1, C_WIDTH))


def _outproj_kernel(*refs, two):
    if two:
        xa_ref, xb_ref, oa_ref, ob_ref, oc_ref, w_ref, g_ref, h_ref, hn_ref = refs
    else:
        xa_ref, oa_ref, ob_ref, oc_ref, w_ref, g_ref, h_ref, hn_ref = refs
    x = xa_ref[...]
    if two:
        x = x + xb_ref[...]
    mixed = jnp.dot(oa_ref[...].astype(BF16), w_ref[0:A_WIDTH, :], preferred_element_type=F32)
    mixed = mixed + jnp.dot(ob_ref[...].astype(BF16), w_ref[A_WIDTH:A_WIDTH + B_WIDTH, :], preferred_element_type=F32)
    mixed = mixed + jnp.dot(oc_ref[...].astype(BF16), w_ref[A_WIDTH + B_WIDTH:, :], preferred_element_type=F32)
    h = x + mixed
    h_ref[...] = h
    ms = jnp.mean(h * h, axis=-1, keepdims=True)
    hn_ref[...] = (h * lax.rsqrt(ms + NORM_EPS) * g_ref[...]).astype(BF16)


def _outproj_call(xa, xb, oa, ob, oc, w_out_b, g, tm):
    n = xa.shape[0]
    two = xb is not None
    row = lambda w: pl.BlockSpec((tm, w), lambda i: (i, 0))
    ins = [xa] + ([xb] if two else []) + [oa, ob, oc, w_out_b, g.reshape(1, D_MODEL)]
    specs = [row(D_MODEL)] + ([row(D_MODEL)] if two else []) + [
        row(A_WIDTH), row(B_WIDTH), row(C_WIDTH),
        pl.BlockSpec((D_MODEL, D_MODEL), lambda i: (0, 0)),
        pl.BlockSpec((1, D_MODEL), lambda i: (0, 0)),
    ]
    return pl.pallas_call(
        functools.partial(_outproj_kernel, two=two),
        grid=(n // tm,),
        in_specs=specs,
        out_specs=[row(D_MODEL), row(D_MODEL)],
        out_shape=[jax.ShapeDtypeStruct((n, D_MODEL), F32), jax.ShapeDtypeStruct((n, D_MODEL), BF16)],
        compiler_params=_cparams(("parallel",)),
        name="out_proj",
    )(*ins)


def _peer_q_kernel(wqt_ref, hn_ref, qt_ref):
    qt_ref[...] = lax.dot_general(wqt_ref[...], hn_ref[...], NT_DIMS, preferred_element_type=F32)


def _peer_q_call(wq_t, hn, tm):
    n = hn.shape[0]
    dq = wq_t.shape[0]
    return pl.pallas_call(
        _peer_q_kernel,
        grid=(n // tm,),
        in_specs=[pl.BlockSpec((dq, D_MODEL), lambda i: (0, 0)), pl.BlockSpec((tm, D_MODEL), lambda i: (i, 0))],
        out_specs=pl.BlockSpec((dq, tm), lambda i: (0, i)),
        out_shape=jax.ShapeDtypeStruct((dq, n), F32),
        compiler_params=_cparams(("parallel",)),
        name="peer_query",
    )(wq_t, hn)


NOT_RANKED = 99.0


def _top_rows(s, k):
    r = s.shape[0]
    iota = lax.broadcasted_iota(jnp.int32, s.shape, 0).astype(F32)
    rank = jnp.full(s.shape, NOT_RANKED, F32)
    vals, rows = [], []
    for j in range(k):
        m = jnp.max(s, axis=0, keepdims=True)
        idx = jnp.min(jnp.where(s == m, iota, float(r)), axis=0, keepdims=True)
        hit = iota == idx
        rank = jnp.where(hit, float(j), rank)
        s = jnp.where(hit, -jnp.inf, s)
        vals.append(m)
        rows.append(idx)
    return jnp.concatenate(vals, axis=0), jnp.concatenate(rows, axis=0), rank


def _peer_route_kernel(qt_ref, sk_ref, lim_ref, coef_ref, rank2_ref, e2_ref):
    half = PEER_DQ // 2
    s1 = jnp.dot(sk_ref[0, 0], qt_ref[0:half, :], preferred_element_type=F32, precision=lax.Precision.HIGHEST)
    s2 = jnp.dot(sk_ref[0, 1], qt_ref[half:, :], preferred_element_type=F32, precision=lax.Precision.HIGHEST)
    v1, _, rank1 = _top_rows(s1, PEER_TOPK)
    v2, _, rank2 = _top_rows(s2, PEER_TOPK)
    cand = jnp.concatenate([v1[a:a + 1, :] + v2 for a in range(PEER_TOPK)], axis=0)
    top, pos, _ = _top_rows(cand, PEER_TOPK)
    z = jnp.sum(jnp.exp(top - top[0:1, :]), axis=0, keepdims=True)
    a_of = jnp.floor(pos * (1.0 / PEER_TOPK))
    lim = jnp.zeros(s1.shape, F32)
    for a in range(PEER_TOPK):
        cnt = jnp.sum(jnp.where(a_of == float(a), 1.0, 0.0), axis=0, keepdims=True)
        lim = jnp.where(rank1 == float(a), cnt, lim)
    lim_ref[0] = lim
    coef_ref[0] = jnp.exp(s1 - v1[0:1, :]) / z
    rank2_ref[0] = rank2
    e2_ref[0] = jnp.exp(s2 - v2[0:1, :])


def _peer_route_call(q_t, subkeys, tn=LANES):
    n = q_t.shape[1]
    shp = jax.ShapeDtypeStruct((PEER_HEADS, PEER_NKEYS, n), F32)
    ospec = pl.BlockSpec((1, PEER_NKEYS, tn), lambda j, h: (h, 0, j))
    return pl.pallas_call(
        _peer_route_kernel,
        grid=(n // tn, PEER_HEADS),
        in_specs=[
            pl.BlockSpec((PEER_DQ, tn), lambda j, h: (h, j)),
            pl.BlockSpec((1, 2, PEER_NKEYS, PEER_DQ // 2), lambda j, h: (h, 0, 0, 0)),
        ],
        out_specs=[ospec] * 4,
        out_shape=[shp] * 4,
        compiler_params=_cparams(("parallel", "arbitrary")),
        name="peer_route",
    )(q_t, subkeys)


def _peer_expert_kernel(xn_ref, lim_ref, coef_ref, rank2_ref, e2_ref, u_ref, v_ref, o_ref, *, et):
    t = pl.program_id(1)

    @pl.when(t == 0)
    def _():
        o_ref[...] = jnp.zeros_like(o_ref)

    hid = lax.dot_general(u_ref[...].astype(BF16), xn_ref[...], NT_DIMS, preferred_element_type=F32)
    acts = []
    for ii in range(et // PEER_NKEYS):
        i1 = t * (et // PEER_NKEYS) + ii
        gate = jnp.zeros((PEER_NKEYS, hid.shape[1]), F32)
        for h in range(PEER_HEADS):
            lim = lim_ref[h, pl.ds(i1, 1), :]
            coef = coef_ref[h, pl.ds(i1, 1), :]
            gate = gate + jnp.where(rank2_ref[h] < lim, e2_ref[h], 0.0) * coef
        acts.append(gate * _gelu(hid[ii * PEER_NKEYS:(ii + 1) * PEER_NKEYS, :]))
    act = jnp.concatenate(acts, axis=0) if len(acts) > 1 else acts[0]
    o_ref[...] += jnp.dot(act.T.astype(BF16), v_ref[...].astype(BF16), preferred_element_type=F32)


def _peer_expert_call(hn, route, u_tab, v_tab, tn, et=256):
    n = hn.shape[0]
    n_exp = u_tab.shape[0]
    rspec = pl.BlockSpec((PEER_HEADS, PEER_NKEYS, tn), lambda j, t: (0, 0, j))
    tspec = pl.BlockSpec((et, D_MODEL), lambda j, t: (t, 0))
    return pl.pallas_call(
        functools.partial(_peer_expert_kernel, et=et),
        grid=(n // tn, n_exp // et),
        in_specs=[pl.BlockSpec((tn, D_MODEL), lambda j, t: (j, 0)), rspec, rspec, rspec, rspec, tspec, tspec],
        out_specs=pl.BlockSpec((tn, D_MODEL), lambda j, t: (j, 0)),
        out_shape=jax.ShapeDtypeStruct((n, D_MODEL), F32),
        compiler_params=_cparams(("parallel", "arbitrary")),
        name="peer_experts",
    )(hn, *route, u_tab, v_tab)


def _final_norm_kernel(xa_ref, xb_ref, g_ref, o_ref):
    x = xa_ref[...] + xb_ref[...]
    ms = jnp.mean(x * x, axis=-1, keepdims=True)
    o_ref[...] = x * lax.rsqrt(ms + NORM_EPS) * g_ref[...]


def _final_norm_call(xa, xb, g, tm):
    n = xa.shape[0]
    row = pl.BlockSpec((tm, D_MODEL), lambda i: (i, 0))
    return pl.pallas_call(
        _final_norm_kernel,
        grid=(n // tm,),
        in_specs=[row, row, pl.BlockSpec((1, D_MODEL), lambda i: (0, 0))],
        out_specs=row,
        out_shape=jax.ShapeDtypeStruct((n, D_MODEL), F32),
        compiler_params=_cparams(("parallel",)),
        name="final_norm",
    )(xa, xb, g.reshape(1, D_MODEL))


def _rmsnorm(x, g):
    return x * lax.rsqrt(jnp.mean(x * x, axis=-1, keepdims=True) + NORM_EPS) * g


def _gather_pages(pool, page_table):
    rows = pool[page_table]
    return rows.reshape((rows.shape[0], rows.shape[1] * rows.shape[2]) + rows.shape[3:])


def _slopes(n):
    return jnp.asarray([2.0 ** (-8.0 * (h + 1) / n) for h in range(n)], F32)


def _diff_sample_jnp(a_q, a_k, a_v, past, q_start, dl, lam_init, subln_g):
    bsz = a_q.shape[0]
    new = jnp.stack([a_k.reshape(bsz, 1, A_HEADS, A_DHEAD), a_v.reshape(bsz, 1, A_HEADS, A_DHEAD)], axis=2)
    allkv = jnp.concatenate([past, new], axis=1)
    q = a_q.reshape(bsz, 1, A_HEADS, 2, A_HALF)
    k = allkv[:, :, 0].reshape(bsz, -1, A_HEADS, 2, A_HALF)
    v = allkv[:, :, 1]
    lam = jnp.exp(jnp.sum(dl[0] * dl[1])) - jnp.exp(jnp.sum(dl[2] * dl[3])) + lam_init
    k_pos = np.arange(q_start + 1)
    s = jnp.einsum('bqhcd,bkhcd->bhcqk', q, k) * (A_HALF ** -0.5)
    dist = jnp.asarray(q_start - k_pos, F32)[None, :]
    s = s - _slopes(A_HEADS)[None, :, None, None, None] * dist[None, None, None]
    p = jax.nn.softmax(s, axis=-1)
    pd = p[:, :, 0] - lam * p[:, :, 1]
    o = jnp.einsum('bhqk,bkhd->bqhd', pd, v)
    o = _rmsnorm(o, subln_g) * (1.0 - lam_init)
    return o.reshape(bsz, A_WIDTH)


def _ssd_sample_jnp(z, xbc, dt_raw, conv_buf, conv_w, conv_b, dt_bias, a_log, d_skip, norm_g, h0):
    bsz = z.shape[0]
    full = jnp.concatenate([conv_buf, xbc[:, None, :]], axis=1)
    conv = conv_b + sum(full[:, i] * conv_w[i] for i in range(CONV_W))
    xc = jax.nn.silu(conv)
    new_buf = full[:, 1:]
    gn = B_GROUPS * B_STATE
    xs = xc[:, :B_WIDTH].reshape(bsz, B_HEADS, B_HEADDIM)
    bm = jnp.repeat(xc[:, B_WIDTH:B_WIDTH + gn].reshape(bsz, B_GROUPS, B_STATE), B_HEADS // B_GROUPS, axis=1)
    cm = jnp.repeat(xc[:, B_WIDTH + gn:].reshape(bsz, B_GROUPS, B_STATE), B_HEADS // B_GROUPS, axis=1)
    dt = jax.nn.softplus(dt_raw + dt_bias)
    a = dt * (-jnp.exp(a_log))
    cb = jnp.einsum('bhn,bhn->bh', cm, bm)
    y = (cb * dt)[..., None] * xs + jnp.einsum('bhn,bhpn->bhp', cm, h0) * jnp.exp(a)[..., None]
    h_new = jnp.exp(a)[:, :, None, None] * h0 + jnp.einsum('bh,bhp,bhn->bhpn', dt, xs, bm)
    y = y + d_skip[:, None] * xs
    y = y.reshape(bsz, B_WIDTH) * jax.nn.silu(z)
    return _rmsnorm(y, norm_g), h_new, new_buf


def _compress_rows_jnp(rows, pe, w1, w2):
    bsz, seqlen = rows.shape[:2]
    n_cmp = (seqlen - CMP_LEN) // CMP_STRIDE + 1
    idx = np.arange(n_cmp)[:, None] * CMP_STRIDE + np.arange(CMP_LEN)[None, :]
    blocks = rows[:, idx] + pe
    hid = jax.nn.gelu(blocks.reshape(bsz, n_cmp, CMP_LEN * C_DHEAD) @ w1)
    return hid @ w2


def _nsa_sample_jnp(c_q, kv_rows, win_rows, gate_raw, q_start, win_pos0, pe, w1, w2, out_g):
    bsz = c_q.shape[0]
    q = c_q.reshape(bsz, 1, C_HEADS, C_DHEAD)
    seqlen = kv_rows.shape[1]
    scale = C_DHEAD ** -0.5
    slopes = _slopes(C_HEADS)
    q_pos = np.asarray([q_start])
    k_cmp = _compress_rows_jnp(kv_rows[:, :, 0], pe[0], w1[0], w2[0])
    v_cmp = _compress_rows_jnp(kv_rows[:, :, 1], pe[1], w1[1], w2[1])
    n_cmp = k_cmp.shape[1]
    cmp_end = np.arange(n_cmp) * CMP_STRIDE + CMP_LEN - 1
    dist_c = q_pos[:, None] - cmp_end[None, :]
    valid_c = (dist_c >= 0)[None, None]
    s = jnp.einsum('bqhd,bnd->bhqn', q, k_cmp) * scale
    s = s - slopes[None, :, None, None] * jnp.asarray(dist_c, F32)[None, None]
    p_cmp = jnp.where(valid_c, jax.nn.softmax(jnp.where(valid_c, s, NEG_INF), axis=-1), 0.0)
    o_cmp = jnp.einsum('bhqn,bnd->bqhd', p_cmp, v_cmp)
    n_sel = -(-seqlen // SEL_BLOCK)
    start = np.arange(n_cmp)[:, None] * CMP_STRIDE
    jb = np.arange(n_sel)[None, :]
    ovl = np.clip(np.minimum(start + CMP_LEN, (jb + 1) * SEL_BLOCK) - np.maximum(start, jb * SEL_BLOCK), 0, None) / CMP_LEN
    imp = jnp.einsum('bhqn,nj->bqj', p_cmp, jnp.asarray(ovl, F32), precision=lax.Precision.HIGHEST)
    blk_id = np.arange(n_sel)[None, :]
    qblk = (q_pos // SEL_BLOCK)[:, None]
    sel_valid = blk_id <= qblk
    forced = (blk_id == 0) | (blk_id == qblk) | (blk_id == qblk - 1)
    score = jnp.where(sel_valid, imp + jnp.where(forced, FORCE_SCORE, 0.0), NEG_INF)
    k_eff = min(SEL_TOPK, n_sel)
    top_s, top_i = lax.top_k(score, k_eff)
    top_ok = top_s > NEG_INF / 2
    sel_rows = jnp.pad(kv_rows[:, :, 2:], ((0, 0), (0, n_sel * SEL_BLOCK - seqlen), (0, 0), (0, 0)))
    sel_rows = sel_rows.reshape(bsz, n_sel, SEL_BLOCK, 2, C_DHEAD)
    bidx = jnp.arange(bsz)[:, None, None]
    g = sel_rows[bidx, top_i]
    kpos = top_i[..., None] * SEL_BLOCK + jnp.arange(SEL_BLOCK)
    dist = q_start - kpos
    ok = (dist >= 0) & top_ok[..., None]
    s_sel = jnp.einsum('bqhd,bqksd->bhqks', q, g[..., 0, :]) * scale
    s_sel = s_sel - slopes[None, :, None, None, None] * dist[:, None].astype(F32)
    s_sel = jnp.where(ok[:, None], s_sel, NEG_INF).reshape(bsz, C_HEADS, 1, k_eff * SEL_BLOCK)
    p = jax.nn.softmax(s_sel, axis=-1)
    vg = g[..., 1, :].reshape(bsz, 1, k_eff * SEL_BLOCK, C_DHEAD)
    o_sel = jnp.einsum('bhqn,bqnd->bqhd', p, vg)
    lw = win_rows.shape[1]
    kpos_w = win_pos0 + np.arange(lw)
    dist_w = q_start - kpos_w
    ok_w = jnp.asarray((dist_w >= 0) & (dist_w < WINDOW))[None, None, None, :]
    s_w = jnp.einsum('bqhd,bkd->bhqk', q, win_rows[:, :, 0]) * scale
    s_w = s_w - slopes[None, :, None, None] * jnp.asarray(dist_w, F32)[None, None, None]
    p_w = jax.nn.softmax(jnp.where(ok_w, s_w, NEG_INF), axis=-1)
    o_win = jnp.einsum('bhqk,bkd->bqhd', p_w, win_rows[:, :, 1])
    gt = jax.nn.sigmoid(gate_raw).reshape(bsz, 1, 3, C_HEADS)
    o = gt[:, :, 0, :, None] * o_cmp + gt[:, :, 1, :, None] * o_sel + gt[:, :, 2, :, None] * o_win
    return _rmsnorm(o.reshape(bsz, C_WIDTH), out_g)


def _reorder_w_in(w):
    return jnp.concatenate([
        w[:, 2560:4608], w[:, 1536:2560], w[:, 0:1536], w[:, 4624:5136], w[:, 5136:5904],
        w[:, 4608:4624], w[:, 5904:5916], jnp.zeros((D_MODEL, PROJ_W - 5916), w.dtype)], axis=1).astype(BF16)


def _token_mixer_tail(xa, xb, oa, ob, oc, w_out_b, ffn_g, wq_t, subkeys, u_tab, v_tab, tm, tn):
    h, hn = _outproj_call(xa, xb, oa, ob, oc, w_out_b, ffn_g, tm)
    q_t = _peer_q_call(wq_t, hn, tm)
    route = _peer_route_call(q_t, subkeys)
    peer = _peer_expert_call(hn, route, u_tab, v_tab, tn)
    return h, peer


def kernel(x_prompt, x_sample, cache_diff_kv, cache_nsa_kv, cache_nsa_win, state_ssm, state_conv, page_table,
           norm_mix_g, w_in, w_out, diff_lam, diff_subln_g, ssm_conv_w, ssm_conv_b, ssm_dt_bias, ssm_a_log,
           ssm_d, ssm_norm_g, nsa_pe, nsa_cmp_w1, nsa_cmp_w2, nsa_out_g, norm_ffn_g, peer_wq, peer_subkeys,
           peer_u, peer_v, norm_final_g):
    depth = w_in.shape[0]
    bp, t, _ = x_prompt.shape
    bs = x_sample.shape[0]
    past_len = page_table.shape[1] * PAGE_SIZE
    n_p = bp * t
    n_s = LANES
    tm_p = 512

    xa_p, xb_p = x_prompt.reshape(n_p, D_MODEL), None
    xa_s = jnp.pad(x_sample.reshape(bs, D_MODEL), ((0, n_s - bs), (0, 0)))
    xb_s = None
    st_p = [[] for _ in range(5)]
    st_s = [[] for _ in range(5)]
    for l in range(depth):
        lam_init = 0.8 - 0.6 * math.exp(-0.3 * l)
        w_r = _reorder_w_in(w_in[l])
        w_out_b = w_out[l].astype(BF16)
        wq_t = peer_wq[l].T.astype(BF16)

        proj = _proj_call(xa_p, xb_p, norm_mix_g[l], w_r, tm_p)
        o_a = _diff_prompt_call(proj, bp, t, diff_lam[l], diff_subln_g[l], lam_init)
        o_b, h_ssm = _ssd_prompt_call(proj, bp, t, ssm_conv_w[l], ssm_conv_b[l], ssm_dt_bias[l], ssm_a_log[l],
                                      ssm_d[l], ssm_norm_g[l])
        kc, vc = _compress_prompt_call(proj, bp, t, nsa_pe[l], nsa_cmp_w1[l], nsa_cmp_w2[l])
        o_c = _nsa_prompt_call(proj, kc, vc, bp, t, nsa_out_g[l])
        proj3 = proj.reshape(bp, t, PROJ_W)
        st_p[0].append(proj3[:, :, COL_AK:COL_AK + 2 * A_WIDTH].reshape(bp, t, 2, A_HEADS, A_DHEAD))
        st_p[1].append(proj3[:, :, COL_CKV:COL_CKV + 4 * C_DHEAD].reshape(bp, t, 4, C_DHEAD))
        keep = min(WINDOW, t)
        st_p[2].append(proj3[:, t - keep:, COL_WIN:COL_WIN + 2 * C_DHEAD].reshape(bp, keep, 2, C_DHEAD))
        st_p[3].append(h_ssm)
        st_p[4].append(proj3[:, t - (CONV_W - 1):, COL_XBC:COL_XBC + B_CONV_DIM])
        xa_p, xb_p = _token_mixer_tail(xa_p, xb_p, o_a, o_b, o_c, w_out_b, norm_ffn_g[l], wq_t, peer_subkeys[l],
                                       peer_u[l], peer_v[l], tm_p, 512)

        proj_s = _proj_call(xa_s, xb_s, norm_mix_g[l], w_r, n_s)[:bs]
        a_q = proj_s[:, COL_AQ:COL_AQ + A_WIDTH]
        a_k = proj_s[:, COL_AK:COL_AK + A_WIDTH]
        a_v = proj_s[:, COL_AV:COL_AV + A_WIDTH]
        o_a = _diff_sample_jnp(a_q, a_k, a_v, _gather_pages(cache_diff_kv[l], page_table), past_len, diff_lam[l],
                               lam_init, diff_subln_g[l])
        o_b, h_new, conv_new = _ssd_sample_jnp(
            proj_s[:, COL_Z:COL_Z + B_WIDTH], proj_s[:, COL_XBC:COL_XBC + B_CONV_DIM],
            proj_s[:, COL_SMALL:COL_SMALL + B_HEADS], state_conv[l], ssm_conv_w[l], ssm_conv_b[l], ssm_dt_bias[l],
            ssm_a_log[l], ssm_d[l], ssm_norm_g[l], state_ssm[l])
        nsa_new = proj_s[:, COL_CKV:COL_CKV + 4 * C_DHEAD].reshape(bs, 1, 4, C_DHEAD)
        win_new = proj_s[:, COL_WIN:COL_WIN + 2 * C_DHEAD].reshape(bs, 1, 2, C_DHEAD)
        nsa_all = jnp.concatenate([_gather_pages(cache_nsa_kv[l], page_table), nsa_new], axis=1)
        win_all = jnp.concatenate([cache_nsa_win[l], win_new], axis=1)
        o_c = _nsa_sample_jnp(proj_s[:, COL_CQ:COL_CQ + C_WIDTH], nsa_all, win_all,
                              proj_s[:, COL_SMALL + GATE_LANE0:COL_SMALL + GATE_LANE0 + 3 * C_HEADS], past_len,
                              past_len - cache_nsa_win.shape[2], nsa_pe[l], nsa_cmp_w1[l], nsa_cmp_w2[l], nsa_out_g[l])
        st_s[0].append(jnp.stack([a_k.reshape(bs, 1, A_HEADS, A_DHEAD), a_v.reshape(bs, 1, A_HEADS, A_DHEAD)], axis=2))
        st_s[1].append(nsa_new)
        keep = min(WINDOW, win_all.shape[1])
        st_s[2].append(win_all[:, win_all.shape[1] - keep:])
        st_s[3].append(h_new)
        st_s[4].append(conv_new)
        pad = lambda a: jnp.pad(a, ((0, n_s - bs), (0, 0)))
        xa_s, xb_s = _token_mixer_tail(xa_s, xb_s, pad(o_a), pad(o_b), pad(o_c), w_out_b, norm_ffn_g[l], wq_t,
                                       peer_subkeys[l], peer_u[l], peer_v[l], n_s, n_s)

    y_p = _final_norm_call(xa_p, xb_p, norm_final_g, tm_p).reshape(bp, t, D_MODEL)
    y_s = _final_norm_call(xa_s, xb_s, norm_final_g, n_s)[:bs].reshape(bs, 1, D_MODEL)
    return (y_p, y_s) + tuple(jnp.stack(s) for s in st_p) + tuple(jnp.stack(s) for s in st_s)
```

```python
import functools
import math

import jax
import jax.numpy as jnp
import numpy as np
from jax import lax
from jax.experimental import pallas as pl
from jax.experimental.pallas import tpu as pltpu

F32 = jnp.float32
BF16 = jnp.bfloat16

D_MODEL = 2048
A_HEADS = 4
A_HALF = 64
A_DHEAD = 128
A_WIDTH = 512
B_WIDTH = 1024
B_HEADDIM = 64
B_HEADS = 16
B_GROUPS = 4
B_STATE = 128
CONV_W = 4
B_CONV_DIM = 2048
SSD_CHUNK = 128
C_HEADS = 4
C_DHEAD = 128
C_WIDTH = 512
CMP_LEN = 32
CMP_STRIDE = 16
SEL_BLOCK = 64
SEL_TOPK = 16
WINDOW = 512
PEER_HEADS = 8
PEER_NKEYS = 128
PEER_TOPK = 16
PEER_DQ = 256
PAGE_SIZE = 128
NORM_EPS = 1e-6
NEG_INF = -1e30
FORCE_SCORE = 1e4

LANES = 128
VMEM_LIMIT = 56 * 1024 * 1024

COL_XBC = 0
COL_Z = 2048
COL_AQ = 3072
COL_AK = 3584
COL_AV = 4096
COL_CQ = 4608
COL_CKV = 5120
COL_WIN = 5632
COL_SMALL = 5888
PROJ_W = 6144
GATE_LANE0 = B_HEADS

NT_DIMS = (((1,), (1,)), ((), ()))


def _cparams(sem, vmem=VMEM_LIMIT):
    return pltpu.CompilerParams(dimension_semantics=sem, vmem_limit_bytes=vmem)


def _gelu(x):
    return 0.5 * x * (1.0 + jnp.tanh(math.sqrt(2.0 / math.pi) * (x + 0.044715 * (x * x * x))))


def _silu(x):
    return x * (1.0 / (1.0 + jnp.exp(-x)))


def _softplus(x):
    return jnp.maximum(x, 0.0) + jnp.log(1.0 + jnp.exp(-jnp.abs(x)))


def _alibi_slope(h):
    if isinstance(h, int):
        return 2.0 ** (-2.0 * (h + 1))
    return jnp.exp2(jnp.full((1, 1), -2.0, F32) * (h + 1).astype(F32))


def _proj_kernel(*refs, two):
    if two:
        xa_ref, xb_ref, g_ref, w_ref, o_ref, xn_ref = refs
    else:
        xa_ref, g_ref, w_ref, o_ref, xn_ref = refs

    @pl.when(pl.program_id(1) == 0)
    def _():
        x = xa_ref[...]
        if two:
            x = x + xb_ref[...]
        ms = jnp.mean(x * x, axis=-1, keepdims=True)
        xn_ref[...] = (x * lax.rsqrt(ms + NORM_EPS) * g_ref[...]).astype(BF16)

    o_ref[...] = jnp.dot(xn_ref[...], w_ref[...], preferred_element_type=F32)


def _proj_call(xa, xb, g, w_r, tm, tc=1536):
    n = xa.shape[0]
    two = xb is not None
    xspec = pl.BlockSpec((tm, D_MODEL), lambda i, j: (i, 0))
    ins = [xa] + ([xb] if two else []) + [g.reshape(1, D_MODEL), w_r]
    specs = [xspec] + ([xspec] if two else []) + [
        pl.BlockSpec((1, D_MODEL), lambda i, j: (0, 0)),
        pl.BlockSpec((D_MODEL, tc), lambda i, j: (0, j)),
    ]
    return pl.pallas_call(
        functools.partial(_proj_kernel, two=two),
        grid=(n // tm, PROJ_W // tc),
        in_specs=specs,
        out_specs=pl.BlockSpec((tm, tc), lambda i, j: (i, j)),
        out_shape=jax.ShapeDtypeStruct((n, PROJ_W), F32),
        scratch_shapes=[pltpu.VMEM((tm, D_MODEL), BF16)],
        compiler_params=_cparams(("parallel", "arbitrary")),
        name="in_proj",
    )(*ins)


def _diff_lambda(dl, lam_init):
    a = jnp.sum(dl[0:1] * dl[1:2], axis=-1, keepdims=True)
    b = jnp.sum(dl[2:3] * dl[3:4], axis=-1, keepdims=True)
    return jnp.exp(a) - jnp.exp(b) + lam_init


def _diff_prompt_kernel(q_ref, k_ref, v_ref, dl_ref, g_ref, o_ref, *, tq, lam_init):
    h = pl.program_id(1)
    i = pl.program_id(2)
    t = k_ref.shape[0]
    scale = A_HALF ** -0.5
    lam = _diff_lambda(dl_ref[...], lam_init)
    q = q_ref[...]
    lane = lax.broadcasted_iota(jnp.int32, (1, A_DHEAD), 1)
    kb = k_ref[...].astype(BF16)
    vb = v_ref[...].astype(BF16)
    qpos = i * tq + lax.broadcasted_iota(jnp.int32, (tq, 1), 0)
    kpos = lax.broadcasted_iota(jnp.int32, (1, t), 1)
    dist = qpos - kpos
    ok = dist >= 0
    bias = _alibi_slope(h) * dist.astype(F32)

    def half_softmax(c):
        qc = jnp.where((lane >= c * A_HALF) & (lane < (c + 1) * A_HALF), q, 0.0).astype(BF16)
        s = lax.dot_general(qc, kb, NT_DIMS, preferred_element_type=F32) * scale
        s = jnp.where(ok, s - bias, NEG_INF)
        m = jnp.max(s, axis=-1, keepdims=True)
        e = jnp.exp(s - m)
        return e / jnp.sum(e, axis=-1, keepdims=True)

    pd = half_softmax(0) - lam * half_softmax(1)
    o = jnp.dot(pd.astype(BF16), vb, preferred_element_type=F32)
    ms = jnp.mean(o * o, axis=-1, keepdims=True)
    o_ref[...] = o * lax.rsqrt(ms + NORM_EPS) * g_ref[...] * (1.0 - lam_init)


def _diff_prompt_call(proj, bsz, t, dl, subln_g, lam_init, tq=256):
    nq = t // tq
    cq, ck, cv = COL_AQ // A_DHEAD, COL_AK // A_DHEAD, COL_AV // A_DHEAD
    return pl.pallas_call(
        functools.partial(_diff_prompt_kernel, tq=tq, lam_init=lam_init),
        grid=(bsz, A_HEADS, nq),
        in_specs=[
            pl.BlockSpec((tq, A_DHEAD), lambda b, h, i: (b * nq + i, cq + h)),
            pl.BlockSpec((t, A_DHEAD), lambda b, h, i: (b, ck + h)),
            pl.BlockSpec((t, A_DHEAD), lambda b, h, i: (b, cv + h)),
            pl.BlockSpec((4, A_HALF), lambda b, h, i: (0, 0)),
            pl.BlockSpec((1, A_DHEAD), lambda b, h, i: (0, 0)),
        ],
        out_specs=pl.BlockSpec((tq, A_DHEAD), lambda b, h, i: (b * nq + i, h)),
        out_shape=jax.ShapeDtypeStruct((bsz * t, A_WIDTH), F32),
        compiler_params=_cparams(("parallel", "parallel", "arbitrary")),
        name="diff_attn_prompt",
    )(proj, proj, proj, dl, subln_g.reshape(1, A_DHEAD))


def _ssd_prompt_kernel(xbc_ref, z_ref, sm_ref, cw_ref, cb_ref, dtb_ref, alog_ref, dsk_ref, g_ref,
                       o_ref, hout_ref, buf_ref, h_ref):
    c = pl.program_id(1)
    cs = SSD_CHUNK

    @pl.when(c == 0)
    def _():
        buf_ref[0:8, :] = jnp.zeros((8, B_CONV_DIM), F32)
        h_ref[...] = jnp.zeros_like(h_ref)

    xbc = xbc_ref[...]
    buf_ref[8:8 + cs, :] = xbc
    cw = cw_ref[...]
    conv = cb_ref[...] + cw[3:4] * xbc
    for j in range(1, CONV_W):
        conv = conv + cw[3 - j:4 - j] * buf_ref[8 - j:8 - j + cs, :]
    buf_ref[0:8, :] = xbc[cs - 8:cs, :]
    xc = _silu(conv)
    xs = xc[:, :B_WIDTH]

    dt = _softplus(sm_ref[...] + dtb_ref[...])
    a_neg = -jnp.exp(alog_ref[...])
    dta = dt * a_neg
    row = lax.broadcasted_iota(jnp.int32, (cs, cs), 0)
    col = lax.broadcasted_iota(jnp.int32, (cs, cs), 1)
    causal = row >= col
    acum = jnp.dot(causal.astype(F32), dta, preferred_element_type=F32, precision=lax.Precision.HIGHEST)
    acum_t = acum.T
    dt_t = dt.T
    lane = lax.broadcasted_iota(jnp.int32, (1, LANES), 1)
    lo = lane < B_HEADDIM

    ys = []
    for g in range(B_GROUPS):
        bg = xc[:, B_WIDTH + g * B_STATE:B_WIDTH + (g + 1) * B_STATE]
        cg = xc[:, B_WIDTH + B_GROUPS * B_STATE + g * B_STATE:B_WIDTH + B_GROUPS * B_STATE + (g + 1) * B_STATE]
        bgb = bg.astype(BF16)
        cgb = cg.astype(BF16)
        cb = lax.dot_general(cgb, bgb, NT_DIMS, preferred_element_type=F32)
        for pr in range(2):
            h0 = g * 4 + pr * 2
            xpair = xs[:, h0 * B_HEADDIM:(h0 + 2) * B_HEADDIM]
            xpb = xpair.astype(BF16)
            ydiag = []
            ecol = []
            wcol = []
            elast = []
            for hh in (h0, h0 + 1):
                a_col = acum[:, hh:hh + 1]
                a_row = acum_t[hh:hh + 1, :]
                decay = jnp.exp(jnp.where(causal, a_col - a_row, NEG_INF))
                lm = cb * decay * dt_t[hh:hh + 1, :]
                ydiag.append(jnp.dot(lm.astype(BF16), xpb, preferred_element_type=F32))
                a_last = acum[cs - 1:cs, hh:hh + 1]
                ecol.append(jnp.exp(a_col))
                wcol.append(jnp.exp(a_last - a_col) * dt[:, hh:hh + 1])
                elast.append(jnp.exp(a_last))
            hp = h_ref[h0 * B_HEADDIM:(h0 + 2) * B_HEADDIM, :]
            yoff = lax.dot_general(cgb, hp.astype(BF16), NT_DIMS, preferred_element_type=F32)
            y = jnp.where(lo, ydiag[0], ydiag[1]) + yoff * jnp.where(lo, ecol[0], ecol[1])
            ys.append(y)
            wx = xpair * jnp.where(lo, wcol[0], wcol[1])
            upd = jnp.dot(wx.T.astype(BF16), bgb, preferred_element_type=F32)
            prow = lax.broadcasted_iota(jnp.int32, (LANES, 1), 0) < B_HEADDIM
            h_ref[h0 * B_HEADDIM:(h0 + 2) * B_HEADDIM, :] = jnp.where(prow, elast[0], elast[1]) * hp + upd

    y = jnp.concatenate(ys, axis=1)
    y = y + dsk_ref[...] * xs
    y = y * _silu(z_ref[...])
    ms = jnp.mean(y * y, axis=-1, keepdims=True)
    o_ref[...] = y * lax.rsqrt(ms + NORM_EPS) * g_ref[...]

    @pl.when(c == pl.num_programs(1) - 1)
    def _():
        hout_ref[0] = h_ref[...]


def _pad_lanes(v, fill=0.0):
    v = v.reshape(1, -1).astype(F32)
    return jnp.pad(v, ((0, 0), (0, LANES - v.shape[1])), constant_values=fill)


def _ssd_prompt_call(proj, bsz, t, conv_w, conv_b, dt_bias, a_log, d_skip, norm_g):
    nc = t // SSD_CHUNK
    cs = SSD_CHUNK
    const = lambda b, c: (0, 0)
    o, hout = pl.pallas_call(
        _ssd_prompt_kernel,
        grid=(bsz, nc),
        in_specs=[
            pl.BlockSpec((cs, B_CONV_DIM), lambda b, c: (b * nc + c, COL_XBC // B_CONV_DIM)),
            pl.BlockSpec((cs, B_WIDTH), lambda b, c: (b * nc + c, COL_Z // B_WIDTH)),
            pl.BlockSpec((cs, LANES), lambda b, c: (b * nc + c, COL_SMALL // LANES)),
            pl.BlockSpec((CONV_W, B_CONV_DIM), const),
            pl.BlockSpec((1, B_CONV_DIM), const),
            pl.BlockSpec((1, LANES), const),
            pl.BlockSpec((1, LANES), const),
            pl.BlockSpec((1, B_WIDTH), const),
            pl.BlockSpec((1, B_WIDTH), const),
        ],
        out_specs=[
            pl.BlockSpec((cs, B_WIDTH), lambda b, c: (b * nc + c, 0)),
            pl.BlockSpec((1, B_HEADS * B_HEADDIM, B_STATE), lambda b, c: (b, 0, 0)),
        ],
        out_shape=[
            jax.ShapeDtypeStruct((bsz * t, B_WIDTH), F32),
            jax.ShapeDtypeStruct((bsz, B_HEADS * B_HEADDIM, B_STATE), F32),
        ],
        scratch_shapes=[pltpu.VMEM((8 + cs, B_CONV_DIM), F32), pltpu.VMEM((B_HEADS * B_HEADDIM, B_STATE), F32)],
        compiler_params=_cparams(("parallel", "arbitrary")),
        name="ssd_prompt",
    )(proj, proj, proj, conv_w, conv_b.reshape(1, -1), _pad_lanes(dt_bias), _pad_lanes(a_log),
      jnp.repeat(d_skip, B_HEADDIM).reshape(1, B_WIDTH), norm_g.reshape(1, B_WIDTH))
    return o, hout.reshape(bsz, B_HEADS, B_HEADDIM, B_STATE)


def _compress_kernel(k_ref, v_ref, pe_ref, w1_ref, w2_ref, kc_ref, vc_ref, *, nchunk):
    outs = []
    for kv, rows_ref in enumerate((k_ref, v_ref)):
        acc_lo = jnp.zeros((nchunk, C_DHEAD), F32)
        acc_hi = jnp.zeros((nchunk, C_DHEAD), F32)
        for r in range(CMP_STRIDE):
            x = rows_ref[pl.ds(r, nchunk, stride=CMP_STRIDE), :]
            x_lo = (x + pe_ref[kv, r:r + 1, :]).astype(BF16)
            x_hi = (x + pe_ref[kv, CMP_STRIDE + r:CMP_STRIDE + r + 1, :]).astype(BF16)
            acc_lo = acc_lo + jnp.dot(x_lo, w1_ref[kv, r].astype(BF16), preferred_element_type=F32)
            acc_hi = acc_hi + jnp.dot(x_hi, w1_ref[kv, CMP_STRIDE + r].astype(BF16), preferred_element_type=F32)
        hid = acc_lo + pltpu.roll(acc_hi, nchunk - 1, 0)
        outs.append(jnp.dot(_gelu(hid).astype(BF16), w2_ref[kv].astype(BF16), preferred_element_type=F32))
    kc_ref[0] = outs[0]
    vc_ref[0] = outs[1]


def _compress_prompt_call(proj, bsz, t, pe, w1, w2):
    nchunk = t // CMP_STRIDE
    shp = jax.ShapeDtypeStruct((bsz, nchunk, C_DHEAD), F32)
    return pl.pallas_call(
        functools.partial(_compress_kernel, nchunk=nchunk),
        grid=(bsz,),
        in_specs=[
            pl.BlockSpec((t, C_DHEAD), lambda b: (b, COL_CKV // C_DHEAD)),
            pl.BlockSpec((t, C_DHEAD), lambda b: (b, COL_CKV // C_DHEAD + 1)),
            pl.BlockSpec((2, CMP_LEN, C_DHEAD), lambda b: (0, 0, 0)),
            pl.BlockSpec((2, CMP_LEN, C_DHEAD, C_DHEAD), lambda b: (0, 0, 0, 0)),
            pl.BlockSpec((2, C_DHEAD, C_DHEAD), lambda b: (0, 0, 0)),
        ],
        out_specs=[pl.BlockSpec((1, nchunk, C_DHEAD), lambda b: (b, 0, 0))] * 2,
        out_shape=[shp, shp],
        compiler_params=_cparams(("parallel",)),
        name="nsa_compress_prompt",
    )(proj, proj, pe, w1.reshape(2, CMP_LEN, C_DHEAD, C_DHEAD), w2)


def _masked_softmax(s, ok):
    s = jnp.where(ok, s, NEG_INF)
    m = jnp.max(s, axis=-1, keepdims=True)
    e = jnp.where(ok, jnp.exp(s - m), 0.0)
    return e, jnp.sum(e, axis=-1, keepdims=True)


def _topk_mask_lanes(score, k, n):
    lane = lax.broadcasted_iota(jnp.int32, (1, LANES), 1)
    rank = jnp.zeros(score.shape, F32)
    for i in range(n):
        ci = score[:, i:i + 1]
        beats = (ci > score) | ((ci == score) & (lane > i))
        rank = rank + jnp.where(beats, 1.0, 0.0)
    return (rank < k) & (lane < n)


def _nsa_prompt_kernel(q_ref, ks_ref, vs_ref, kw_ref, vw_ref, kc_ref, vc_ref, sm_ref, g_ref, o_ref, *, tq, n_cmp):
    i = pl.program_id(1)
    t = ks_ref.shape[0]
    n_sel = t // SEL_BLOCK
    scale = C_DHEAD ** -0.5
    qpos = i * tq + lax.broadcasted_iota(jnp.int32, (tq, 1), 0)
    kpos = lax.broadcasted_iota(jnp.int32, (1, t), 1)
    dist = qpos - kpos
    dist_f = dist.astype(F32)
    lane = lax.broadcasted_iota(jnp.int32, (1, LANES), 1)

    cmp_end = lane * CMP_STRIDE + (CMP_LEN - 1)
    dist_c = qpos - cmp_end
    ok_c = (dist_c >= 0) & (lane < n_cmp)
    dist_cf = dist_c.astype(F32)
    kcb = kc_ref[0].astype(BF16)
    vcb = vc_ref[0].astype(BF16)
    qs = [q_ref[:, h * C_DHEAD:(h + 1) * C_DHEAD].astype(BF16) for h in range(C_HEADS)]
    o_cmp = []
    psum = jnp.zeros((tq, LANES), F32)
    for h in range(C_HEADS):
        s = lax.dot_general(qs[h], kcb, NT_DIMS, preferred_element_type=F32) * scale
        s = s - _alibi_slope(h) * dist_cf
        e, den = _masked_softmax(s, ok_c)
        p = e / jnp.maximum(den, 1e-30)
        psum = psum + p
        o_cmp.append(jnp.dot(p.astype(BF16), vcb, preferred_element_type=F32))

    n_i = lax.broadcasted_iota(jnp.int32, (LANES, LANES), 0)
    j_i = lax.broadcasted_iota(jnp.int32, (LANES, LANES), 1)
    lo_ = jnp.maximum(n_i * CMP_STRIDE, j_i * SEL_BLOCK)
    hi_ = jnp.minimum(n_i * CMP_STRIDE + CMP_LEN, (j_i + 1) * SEL_BLOCK)
    ovl = jnp.maximum(hi_ - lo_, 0).astype(F32) * (1.0 / CMP_LEN)
    ovl = jnp.where((n_i < n_cmp) & (j_i < n_sel), ovl, 0.0)
    imp = jnp.dot(psum, ovl, preferred_element_type=F32, precision=lax.Precision.HIGHEST)
    sel_shift = SEL_BLOCK.bit_length() - 1
    qblk = qpos >> sel_shift
    sel_valid = lane <= qblk
    forced = (lane == 0) | (lane == qblk) | (lane == qblk - 1)
    score = jnp.where(sel_valid, imp + jnp.where(forced, FORCE_SCORE, 0.0), NEG_INF)
    score = jnp.where(lane < n_sel, score, -jnp.inf)
    chosen = _topk_mask_lanes(score, min(SEL_TOPK, n_sel), n_sel) & sel_valid
    e_j = lax.broadcasted_iota(jnp.int32, (LANES, t), 0)
    e_k = lax.broadcasted_iota(jnp.int32, (LANES, t), 1)
    expand = jnp.where((e_k >> sel_shift) == e_j, 1.0, 0.0).astype(BF16)
    key_sel = jnp.dot(jnp.where(chosen, 1.0, 0.0).astype(BF16), expand, preferred_element_type=F32) > 0.5
    ok_s = key_sel & (dist >= 0)
    ok_w = (dist >= 0) & (dist < WINDOW)

    ksb = ks_ref[...].astype(BF16)
    vsb = vs_ref[...].astype(BF16)
    kwb = kw_ref[...].astype(BF16)
    vwb = vw_ref[...].astype(BF16)
    gate = 1.0 / (1.0 + jnp.exp(-sm_ref[...]))
    outs = []
    for h in range(C_HEADS):
        bias = _alibi_slope(h) * dist_f
        s = lax.dot_general(qs[h], ksb, NT_DIMS, preferred_element_type=F32) * scale - bias
        e, den = _masked_softmax(s, ok_s)
        o_sel = jnp.dot((e / den).astype(BF16), vsb, preferred_element_type=F32)
        s = lax.dot_general(qs[h], kwb, NT_DIMS, preferred_element_type=F32) * scale - bias
        e, den = _masked_softmax(s, ok_w)
        o_win = jnp.dot((e / den).astype(BF16), vwb, preferred_element_type=F32)
        g0 = gate[:, GATE_LANE0 + h:GATE_LANE0 + h + 1]
        g1 = gate[:, GATE_LANE0 + C_HEADS + h:GATE_LANE0 + C_HEADS + h + 1]
        g2 = gate[:, GATE_LANE0 + 2 * C_HEADS + h:GATE_LANE0 + 2 * C_HEADS + h + 1]
        outs.append(g0 * o_cmp[h] + g1 * o_sel + g2 * o_win)
    o = jnp.concatenate(outs, axis=1)
    ms = jnp.mean(o * o, axis=-1, keepdims=True)
    o_ref[...] = o * lax.rsqrt(ms + NORM_EPS) * g_ref[...]


def _nsa_prompt_call(proj, kc, vc, bsz, t, out_g, tq=256):
    nq = t // tq
    n_cmp = (t - CMP_LEN) // CMP_STRIDE + 1
    c0 = COL_CKV // C_DHEAD
    w0 = COL_WIN // C_DHEAD
    kvspec = lambda col: pl.BlockSpec((t, C_DHEAD), lambda b, i: (b, col))
    return pl.pallas_call(
        functools.partial(_nsa_prompt_kernel, tq=tq, n_cmp=n_cmp),
        grid=(bsz, nq),
        in_specs=[
            pl.BlockSpec((tq, C_WIDTH), lambda b, i: (b * nq + i, COL_CQ // C_WIDTH)),
            kvspec(c0 + 2), kvspec(c0 + 3), kvspec(w0), kvspec(w0 + 1),
            pl.BlockSpec((1, kc.shape[1], C_DHEAD), lambda b, i: (b, 0, 0)),
            pl.BlockSpec((1, kc.shape[1], C_DHEAD), lambda b, i: (b, 0, 0)),
            pl.BlockSpec((tq, LANES), lambda b, i: (b * nq + i, COL_SMALL // LANES)),
            pl.BlockSpec((1, C_WIDTH), lambda b, i: (0, 0)),
        ],
        out_specs=pl.BlockSpec((tq, C_WIDTH), lambda b, i: (b * nq + i, 0)),
        out_shape=jax.ShapeDtypeStruct((bsz * t, C_WIDTH), F32),
        compiler_params=_cparams(("parallel", "arbitrary")),
        name="nsa_attn_prompt",
    )(proj, proj, proj, proj, proj, kc, vc, proj, out_g.reshape(1, C_WIDTH))


def _outproj_kernel(*refs, two):
    if two:
        xa_ref, xb_ref, oa_ref, ob_ref, oc_ref, w_ref, g_ref, h_ref, hn_ref = refs
    else:
        xa_ref, oa_ref, ob_ref, oc_ref, w_ref, g_ref, h_ref, hn_ref = refs
    x = xa_ref[...]
    if two:
        x = x + xb_ref[...]
    mixed = jnp.dot(oa_ref[...].astype(BF16), w_ref[0:A_WIDTH, :], preferred_element_type=F32)
    mixed = mixed + jnp.dot(ob_ref[...].astype(BF16), w_ref[A_WIDTH:A_WIDTH + B_WIDTH, :], preferred_element_type=F32)
    mixed = mixed + jnp.dot(oc_ref[...].astype(BF16), w_ref[A_WIDTH + B_WIDTH:, :], preferred_element_type=F32)
    h = x + mixed
    h_ref[...] = h
    ms = jnp.mean(h * h, axis=-1, keepdims=True)
    hn_ref[...] = (h * lax.rsqrt(ms + NORM_EPS) * g_ref[...]).astype(BF16)


def _outproj_call(xa, xb, oa, ob, oc, w_out_b, g, tm):
    n = xa.shape[0]
    two = xb is not None
    row = lambda w: pl.BlockSpec((tm, w), lambda i: (i, 0))
    ins = [xa] + ([xb] if two else []) + [oa, ob, oc, w_out_b, g.reshape(1, D_MODEL)]
    specs = [row(D_MODEL)] + ([row(D_MODEL)] if two else []) + [
        row(A_WIDTH), row(B_WIDTH), row(C_WIDTH),
        pl.BlockSpec((D_MODEL, D_MODEL), lambda i: (0, 0)),
        pl.BlockSpec((1, D_MODEL), lambda i: (0, 0)),
    ]
    return pl.pallas_call(
        functools.partial(_outproj_kernel, two=two),
        grid=(n // tm,),
        in_specs=specs,
        out_specs=[row(D_MODEL), row(D_MODEL)],
        out_shape=[jax.ShapeDtypeStruct((n, D_MODEL), F32), jax.ShapeDtypeStruct((n, D_MODEL), BF16)],
        compiler_params=_cparams(("parallel",)),
        name="out_proj",
    )(*ins)


def _peer_q_kernel(wqt_ref, hn_ref, qt_ref):
    qt_ref[...] = lax.dot_general(wqt_ref[...], hn_ref[...], NT_DIMS, preferred_element_type=F32)


def _peer_q_call(wq_t, hn, tm):
    n = hn.shape[0]
    dq = wq_t.shape[0]
    return pl.pallas_call(
        _peer_q_kernel,
        grid=(n // tm,),
        in_specs=[pl.BlockSpec((dq, D_MODEL), lambda i: (0, 0)), pl.BlockSpec((tm, D_MODEL), lambda i: (i, 0))],
        out_specs=pl.BlockSpec((dq, tm), lambda i: (0, i)),
        out_shape=jax.ShapeDtypeStruct((dq, n), F32),
        compiler_params=_cparams(("parallel",)),
        name="peer_query",
    )(wq_t, hn)


NOT_RANKED = 99.0


def _top_rows(s, pos, k, want_rank):
    rank = jnp.full(s.shape, NOT_RANKED, F32) if want_rank else None
    vals, picks = [], []
    for j in range(k):
        m = jnp.max(s, axis=0, keepdims=True)
        idx = jnp.min(jnp.where(s == m, pos, 1e9), axis=0, keepdims=True)
        hit = pos == idx
        if want_rank:
            rank = jnp.where(hit, float(j), rank)
        s = jnp.where(hit, -jnp.inf, s)
        vals.append(m)
        picks.append(idx)
    return jnp.concatenate(vals, axis=0), jnp.concatenate(picks, axis=0), rank


PAIR_ROWS = PEER_TOPK + 7 * 8 + 8


def _pair_candidates(v1, v2):
    tn = v1.shape[1]
    parts = [v1[0:1, :] + v2] + [v1[a:a + 1, :] + v2[0:8, :] for a in range(1, 8)] + [v1[8:16, :] + v2[0:1, :]]
    r = lax.broadcasted_iota(jnp.int32, (PAIR_ROWS, tn), 0)
    mid = r - PEER_TOPK
    pos = jnp.where(r < PEER_TOPK, r,
                    jnp.where(r < PEER_TOPK + 56, ((mid >> 3) + 1) * PEER_TOPK + (mid & 7), (r - 64) * PEER_TOPK))
    return jnp.concatenate(parts, axis=0), pos.astype(F32)


def _peer_route_kernel(qt_ref, sk_ref, lim_ref, coef_ref, rank2_ref, e2_ref, *, heads):
    half = PEER_DQ // 2
    row = lax.broadcasted_iota(jnp.int32, (PEER_NKEYS, qt_ref.shape[1]), 0).astype(F32)
    for hh in range(heads):
        q = qt_ref[hh * PEER_DQ:(hh + 1) * PEER_DQ, :]
        s1 = jnp.dot(sk_ref[hh, 0], q[0:half, :], preferred_element_type=F32, precision=lax.Precision.HIGHEST)
        s2 = jnp.dot(sk_ref[hh, 1], q[half:, :], preferred_element_type=F32, precision=lax.Precision.HIGHEST)
        v1, _, rank1 = _top_rows(s1, row, PEER_TOPK, True)
        v2, _, rank2 = _top_rows(s2, row, PEER_TOPK, True)
        cand, cpos = _pair_candidates(v1, v2)
        top, pos, _ = _top_rows(cand, cpos, PEER_TOPK, False)
        z = jnp.sum(jnp.exp(top - top[0:1, :]), axis=0, keepdims=True)
        a_of = jnp.floor(pos * (1.0 / PEER_TOPK))
        lim = jnp.zeros(s1.shape, F32)
        for a in range(PEER_TOPK):
            cnt = jnp.sum(jnp.where(a_of == float(a), 1.0, 0.0), axis=0, keepdims=True)
            lim = jnp.where(rank1 == float(a), cnt, lim)
        lim_ref[hh] = lim
        coef_ref[hh] = jnp.exp(s1 - v1[0:1, :]) / z
        rank2_ref[hh] = rank2
        e2_ref[hh] = jnp.exp(s2 - v2[0:1, :])


def _peer_route_call(q_t, subkeys, tn=LANES, heads=2):
    n = q_t.shape[1]
    shp = jax.ShapeDtypeStruct((PEER_HEADS, PEER_NKEYS, n), F32)
    ospec = pl.BlockSpec((heads, PEER_NKEYS, tn), lambda j, h: (h, 0, j))
    return pl.pallas_call(
        functools.partial(_peer_route_kernel, heads=heads),
        grid=(n // tn, PEER_HEADS // heads),
        in_specs=[
            pl.BlockSpec((heads * PEER_DQ, tn), lambda j, h: (h, j)),
            pl.BlockSpec((heads, 2, PEER_NKEYS, PEER_DQ // 2), lambda j, h: (h, 0, 0, 0)),
        ],
        out_specs=[ospec] * 4,
        out_shape=[shp] * 4,
        compiler_params=_cparams(("parallel", "arbitrary")),
        name="peer_route",
    )(q_t, subkeys)


MXU_DEPTH = 256


def _peer_expert_kernel(xn_ref, lim_ref, coef_ref, rank2_ref, e2_ref, u_ref, v_ref, o_ref, *cast_refs, et):
    t = pl.program_id(1)

    @pl.when(t == 0)
    def _():
        o_ref[...] = jnp.zeros_like(o_ref)

    ub = u_ref[...].astype(BF16)
    vb = v_ref[...].astype(BF16)
    if cast_refs:
        cast_refs[0][...] = ub
        cast_refs[1][...] = vb
    tn = xn_ref.shape[0]
    hid = lax.dot_general(ub, xn_ref[...], NT_DIMS, preferred_element_type=F32)
    per = MXU_DEPTH // PEER_NKEYS
    acc = None
    for c in range(et // MXU_DEPTH):
        acts = []
        for ii in range(c * per, (c + 1) * per):
            i1 = t * (et // PEER_NKEYS) + ii
            gate = jnp.zeros((PEER_NKEYS, tn), F32)
            for h in range(PEER_HEADS):
                lim = lim_ref[h, pl.ds(i1, 1), :]
                coef = coef_ref[h, pl.ds(i1, 1), :]
                gate = gate + jnp.where(rank2_ref[h] < lim, e2_ref[h], 0.0) * coef
            acts.append(gate * _gelu(hid[ii * PEER_NKEYS:(ii + 1) * PEER_NKEYS, :]))
        act = jnp.concatenate(acts, axis=0)
        part = jnp.dot(act.T.astype(BF16), vb[c * MXU_DEPTH:(c + 1) * MXU_DEPTH, :], preferred_element_type=F32)
        acc = part if acc is None else acc + part
    o_ref[...] += acc


def _peer_expert_call(hn, route, u_tab, v_tab, tn, et, emit_bf16=False):
    n = hn.shape[0]
    n_exp = u_tab.shape[0]
    rspec = pl.BlockSpec((PEER_HEADS, PEER_NKEYS, tn), lambda j, t: (0, 0, j))
    tspec = pl.BlockSpec((et, D_MODEL), lambda j, t: (t, 0))
    ospec = pl.BlockSpec((tn, D_MODEL), lambda j, t: (j, 0))
    oshape = jax.ShapeDtypeStruct((n, D_MODEL), F32)
    if emit_bf16:
        assert n == tn
        tshape = jax.ShapeDtypeStruct((n_exp, D_MODEL), BF16)
        out_specs, out_shape = [ospec, tspec, tspec], [oshape, tshape, tshape]
    else:
        out_specs, out_shape = ospec, oshape
    return pl.pallas_call(
        functools.partial(_peer_expert_kernel, et=et),
        grid=(n // tn, n_exp // et),
        in_specs=[pl.BlockSpec((tn, D_MODEL), lambda j, t: (j, 0)), rspec, rspec, rspec, rspec, tspec, tspec],
        out_specs=out_specs,
        out_shape=out_shape,
        compiler_params=_cparams(("parallel", "arbitrary")),
        name="peer_experts",
    )(hn, *route, u_tab, v_tab)


def _final_norm_kernel(xa_ref, xb_ref, g_ref, o_ref):
    x = xa_ref[...] + xb_ref[...]
    ms = jnp.mean(x * x, axis=-1, keepdims=True)
    o_ref[...] = x * lax.rsqrt(ms + NORM_EPS) * g_ref[...]


def _final_norm_call(xa, xb, g, tm):
    n = xa.shape[0]
    row = pl.BlockSpec((tm, D_MODEL), lambda i: (i, 0))
    return pl.pallas_call(
        _final_norm_kernel,
        grid=(n // tm,),
        in_specs=[row, row, pl.BlockSpec((1, D_MODEL), lambda i: (0, 0))],
        out_specs=row,
        out_shape=jax.ShapeDtypeStruct((n, D_MODEL), F32),
        compiler_params=_cparams(("parallel",)),
        name="final_norm",
    )(xa, xb, g.reshape(1, D_MODEL))


def _rmsnorm(x, g):
    return x * lax.rsqrt(jnp.mean(x * x, axis=-1, keepdims=True) + NORM_EPS) * g


def _gather_pages(pool, page_table):
    rows = pool[page_table]
    return rows.reshape((rows.shape[0], rows.shape[1] * rows.shape[2]) + rows.shape[3:])


def _slopes(n):
    return jnp.asarray([2.0 ** (-8.0 * (h + 1) / n) for h in range(n)], F32)


def _diff_sample_jnp(a_q, a_k, a_v, past, q_start, dl, lam_init, subln_g):
    bsz = a_q.shape[0]
    new = jnp.stack([a_k.reshape(bsz, 1, A_HEADS, A_DHEAD), a_v.reshape(bsz, 1, A_HEADS, A_DHEAD)], axis=2)
    allkv = jnp.concatenate([past, new], axis=1)
    q = a_q.reshape(bsz, 1, A_HEADS, 2, A_HALF)
    k = allkv[:, :, 0].reshape(bsz, -1, A_HEADS, 2, A_HALF)
    v = allkv[:, :, 1]
    lam = jnp.exp(jnp.sum(dl[0] * dl[1])) - jnp.exp(jnp.sum(dl[2] * dl[3])) + lam_init
    k_pos = np.arange(q_start + 1)
    s = jnp.einsum('bqhcd,bkhcd->bhcqk', q, k) * (A_HALF ** -0.5)
    dist = jnp.asarray(q_start - k_pos, F32)[None, :]
    s = s - _slopes(A_HEADS)[None, :, None, None, None] * dist[None, None, None]
    p = jax.nn.softmax(s, axis=-1)
    pd = p[:, :, 0] - lam * p[:, :, 1]
    o = jnp.einsum('bhqk,bkhd->bqhd', pd, v)
    o = _rmsnorm(o, subln_g) * (1.0 - lam_init)
    return o.reshape(bsz, A_WIDTH)


def _ssd_sample_jnp(z, xbc, dt_raw, conv_buf, conv_w, conv_b, dt_bias, a_log, d_skip, norm_g, h0):
    bsz = z.shape[0]
    full = jnp.concatenate([conv_buf, xbc[:, None, :]], axis=1)
    conv = conv_b + sum(full[:, i] * conv_w[i] for i in range(CONV_W))
    xc = jax.nn.silu(conv)
    new_buf = full[:, 1:]
    gn = B_GROUPS * B_STATE
    xs = xc[:, :B_WIDTH].reshape(bsz, B_HEADS, B_HEADDIM)
    bm = jnp.repeat(xc[:, B_WIDTH:B_WIDTH + gn].reshape(bsz, B_GROUPS, B_STATE), B_HEADS // B_GROUPS, axis=1)
    cm = jnp.repeat(xc[:, B_WIDTH + gn:].reshape(bsz, B_GROUPS, B_STATE), B_HEADS // B_GROUPS, axis=1)
    dt = jax.nn.softplus(dt_raw + dt_bias)
    a = dt * (-jnp.exp(a_log))
    cb = jnp.einsum('bhn,bhn->bh', cm, bm)
    y = (cb * dt)[..., None] * xs + jnp.einsum('bhn,bhpn->bhp', cm, h0) * jnp.exp(a)[..., None]
    h_new = jnp.exp(a)[:, :, None, None] * h0 + jnp.einsum('bh,bhp,bhn->bhpn', dt, xs, bm)
    y = y + d_skip[:, None] * xs
    y = y.reshape(bsz, B_WIDTH) * jax.nn.silu(z)
    return _rmsnorm(y, norm_g), h_new, new_buf


def _compress_rows_jnp(rows, pe, w1, w2):
    bsz, seqlen = rows.shape[:2]
    n_cmp = (seqlen - CMP_LEN) // CMP_STRIDE + 1
    idx = np.arange(n_cmp)[:, None] * CMP_STRIDE + np.arange(CMP_LEN)[None, :]
    blocks = rows[:, idx] + pe
    hid = jax.nn.gelu(blocks.reshape(bsz, n_cmp, CMP_LEN * C_DHEAD) @ w1)
    return hid @ w2


def _nsa_sample_jnp(c_q, kv_rows, win_rows, gate_raw, q_start, win_pos0, pe, w1, w2, out_g):
    bsz = c_q.shape[0]
    q = c_q.reshape(bsz, 1, C_HEADS, C_DHEAD)
    seqlen = kv_rows.shape[1]
    scale = C_DHEAD ** -0.5
    slopes = _slopes(C_HEADS)
    q_pos = np.asarray([q_start])
    k_cmp = _compress_rows_jnp(kv_rows[:, :, 0], pe[0], w1[0], w2[0])
    v_cmp = _compress_rows_jnp(kv_rows[:, :, 1], pe[1], w1[1], w2[1])
    n_cmp = k_cmp.shape[1]
    cmp_end = np.arange(n_cmp) * CMP_STRIDE + CMP_LEN - 1
    dist_c = q_pos[:, None] - cmp_end[None, :]
    valid_c = (dist_c >= 0)[None, None]
    s = jnp.einsum('bqhd,bnd->bhqn', q, k_cmp) * scale
    s = s - slopes[None, :, None, None] * jnp.asarray(dist_c, F32)[None, None]
    p_cmp = jnp.where(valid_c, jax.nn.softmax(jnp.where(valid_c, s, NEG_INF), axis=-1), 0.0)
    o_cmp = jnp.einsum('bhqn,bnd->bqhd', p_cmp, v_cmp)
    n_sel = -(-seqlen // SEL_BLOCK)
    start = np.arange(n_cmp)[:, None] * CMP_STRIDE
    jb = np.arange(n_sel)[None, :]
    ovl = np.clip(np.minimum(start + CMP_LEN, (jb + 1) * SEL_BLOCK) - np.maximum(start, jb * SEL_BLOCK), 0, None) / CMP_LEN
    imp = jnp.einsum('bhqn,nj->bqj', p_cmp, jnp.asarray(ovl, F32), precision=lax.Precision.HIGHEST)
    blk_id = np.arange(n_sel)[None, :]
    qblk = (q_pos // SEL_BLOCK)[:, None]
    sel_valid = blk_id <= qblk
    forced = (blk_id == 0) | (blk_id == qblk) | (blk_id == qblk - 1)
    score = jnp.where(sel_valid, imp + jnp.where(forced, FORCE_SCORE, 0.0), NEG_INF)
    k_eff = min(SEL_TOPK, n_sel)
    top_s, top_i = lax.top_k(score, k_eff)
    top_ok = top_s > NEG_INF / 2
    sel_rows = jnp.pad(kv_rows[:, :, 2:], ((0, 0), (0, n_sel * SEL_BLOCK - seqlen), (0, 0), (0, 0)))
    sel_rows = sel_rows.reshape(bsz, n_sel, SEL_BLOCK, 2, C_DHEAD)
    bidx = jnp.arange(bsz)[:, None, None]
    g = sel_rows[bidx, top_i]
    kpos = top_i[..., None] * SEL_BLOCK + jnp.arange(SEL_BLOCK)
    dist = q_start - kpos
    ok = (dist >= 0) & top_ok[..., None]
    s_sel = jnp.einsum('bqhd,bqksd->bhqks', q, g[..., 0, :]) * scale
    s_sel = s_sel - slopes[None, :, None, None, None] * dist[:, None].astype(F32)
    s_sel = jnp.where(ok[:, None], s_sel, NEG_INF).reshape(bsz, C_HEADS, 1, k_eff * SEL_BLOCK)
    p = jax.nn.softmax(s_sel, axis=-1)
    vg = g[..., 1, :].reshape(bsz, 1, k_eff * SEL_BLOCK, C_DHEAD)
    o_sel = jnp.einsum('bhqn,bqnd->bqhd', p, vg)
    lw = win_rows.shape[1]
    kpos_w = win_pos0 + np.arange(lw)
    dist_w = q_start - kpos_w
    ok_w = jnp.asarray((dist_w >= 0) & (dist_w < WINDOW))[None, None, None, :]
    s_w = jnp.einsum('bqhd,bkd->bhqk', q, win_rows[:, :, 0]) * scale
    s_w = s_w - slopes[None, :, None, None] * jnp.asarray(dist_w, F32)[None, None, None]
    p_w = jax.nn.softmax(jnp.where(ok_w, s_w, NEG_INF), axis=-1)
    o_win = jnp.einsum('bhqk,bkd->bqhd', p_w, win_rows[:, :, 1])
    gt = jax.nn.sigmoid(gate_raw).reshape(bsz, 1, 3, C_HEADS)
    o = gt[:, :, 0, :, None] * o_cmp + gt[:, :, 1, :, None] * o_sel + gt[:, :, 2, :, None] * o_win
    return _rmsnorm(o.reshape(bsz, C_WIDTH), out_g)


DIFF_PAGES = 8


def _row_block_mask(rows, width, block):
    r = lax.broadcasted_iota(jnp.int32, (rows, width), 0)
    c = lax.broadcasted_iota(jnp.int32, (rows, width), 1)
    return (c // block) == r


def _diff_sample_kernel(pt_ref, q_ref, knew_ref, vnew_ref, dl_ref, g_ref, *rest, past_len, lam_init):
    page_refs = rest[:DIFF_PAGES]
    o_ref, m_ref, l_ref, acc_ref = rest[DIFF_PAGES:]
    p = pl.program_id(1)
    scale = A_HALF ** -0.5
    nrow = 2 * A_HEADS

    @pl.when(p == 0)
    def _():
        m_ref[...] = jnp.full(m_ref.shape, NEG_INF, F32)
        l_ref[...] = jnp.zeros(l_ref.shape, F32)
        acc_ref[...] = jnp.zeros(acc_ref.shape, F32)

    q = q_ref[0]
    q8 = jnp.where(_row_block_mask(nrow, A_WIDTH, A_HALF), q, 0.0)
    q8b = q8.astype(BF16)
    row = lax.broadcasted_iota(jnp.int32, (nrow, 1), 0)
    slope = jnp.exp2(-2.0 * ((row >> 1) + 1).astype(F32))
    ss, vs = [], []
    for g in range(DIFF_PAGES):
        blk = page_refs[g][0, 0]
        ss.append(lax.dot_general(q8b, blk[:, :A_WIDTH].astype(BF16), NT_DIMS, preferred_element_type=F32))
        vs.append(blk[:, A_WIDTH:].astype(BF16))
    s = jnp.concatenate(ss, axis=1) * scale
    nk = DIFF_PAGES * PAGE_SIZE
    kpos = p * nk + lax.broadcasted_iota(jnp.int32, (1, nk), 1)
    s = s - slope * (past_len - kpos).astype(F32)
    m_old = m_ref[:, 0:1]
    m_new = jnp.maximum(m_old, jnp.max(s, axis=-1, keepdims=True))
    alpha = jnp.exp(m_old - m_new)
    e = jnp.exp(s - m_new)
    l_new = alpha * l_ref[:, 0:1] + jnp.sum(e, axis=-1, keepdims=True)
    acc = alpha * acc_ref[...] + jnp.dot(e.astype(BF16), jnp.concatenate(vs, axis=0), preferred_element_type=F32)
    m_ref[...] = jnp.broadcast_to(m_new, m_ref.shape)
    l_ref[...] = jnp.broadcast_to(l_new, l_ref.shape)
    acc_ref[...] = acc

    @pl.when(p == pl.num_programs(1) - 1)
    def _():
        s_n = jnp.sum(q8 * knew_ref[0], axis=-1, keepdims=True) * scale
        m_f = jnp.maximum(m_new, s_n)
        a_f = jnp.exp(m_new - m_f)
        e_n = jnp.exp(s_n - m_f)
        den = a_f * l_new + e_n
        o8 = (a_f * acc + e_n * vnew_ref[0]) / den
        lam = _diff_lambda(dl_ref[...], lam_init)
        coef = jnp.where((row & 1) == 0, 1.0, -lam)
        r = lax.broadcasted_iota(jnp.int32, (nrow, A_WIDTH), 0)
        c = lax.broadcasted_iota(jnp.int32, (nrow, A_WIDTH), 1)
        o = jnp.sum(jnp.where((c // A_DHEAD) == (r >> 1), o8 * coef, 0.0), axis=0, keepdims=True)
        outs = []
        for h in range(A_HEADS):
            oh = o[:, h * A_DHEAD:(h + 1) * A_DHEAD]
            ms = jnp.mean(oh * oh, axis=-1, keepdims=True)
            outs.append(oh * lax.rsqrt(ms + NORM_EPS) * g_ref[...] * (1.0 - lam_init))
        o_ref[0] = jnp.concatenate(outs, axis=1)


def _diff_sample_call(page_table, proj3, cache4, layer, dl, subln_g, lam_init):
    bs, n_pages = page_table.shape
    past_len = n_pages * PAGE_SIZE
    steps = n_pages // DIFF_PAGES

    def page_spec(g):
        return pl.BlockSpec((1, 1, PAGE_SIZE, 2 * A_WIDTH), lambda b, p, pt: (layer, pt[b, p * DIFF_PAGES + g], 0, 0))

    grid_spec = pltpu.PrefetchScalarGridSpec(
        num_scalar_prefetch=1,
        grid=(bs, steps),
        in_specs=[
            pl.BlockSpec((1, 1, A_WIDTH), lambda b, p, pt: (b, 0, COL_AQ // A_WIDTH)),
            pl.BlockSpec((1, 1, A_WIDTH), lambda b, p, pt: (b, 0, COL_AK // A_WIDTH)),
            pl.BlockSpec((1, 1, A_WIDTH), lambda b, p, pt: (b, 0, COL_AV // A_WIDTH)),
            pl.BlockSpec((4, A_HALF), lambda b, p, pt: (0, 0)),
            pl.BlockSpec((1, A_DHEAD), lambda b, p, pt: (0, 0)),
        ] + [page_spec(g) for g in range(DIFF_PAGES)],
        out_specs=pl.BlockSpec((1, 1, A_WIDTH), lambda b, p, pt: (b, 0, 0)),
        scratch_shapes=[pltpu.VMEM((2 * A_HEADS, LANES), F32), pltpu.VMEM((2 * A_HEADS, LANES), F32),
                        pltpu.VMEM((2 * A_HEADS, A_WIDTH), F32)],
    )
    return pl.pallas_call(
        functools.partial(_diff_sample_kernel, past_len=past_len, lam_init=lam_init),
        grid_spec=grid_spec,
        out_shape=jax.ShapeDtypeStruct((bs, 1, A_WIDTH), F32),
        compiler_params=_cparams(("parallel", "arbitrary")),
        name="diff_attn_sample",
    )(page_table, proj3, proj3, proj3, dl, subln_g.reshape(1, A_DHEAD), *([cache4] * DIFF_PAGES))


def _diag_rows(vec):
    n = vec.shape[1]
    r = lax.broadcasted_iota(jnp.int32, (n, n), 0)
    c = lax.broadcasted_iota(jnp.int32, (n, n), 1)
    return jnp.where(r == c, vec, 0.0)


def _ssd_sample_kernel(xbc_ref, z_ref, sm_ref, cbuf_ref, h0_ref, cw_ref, cb_ref, dtb_ref, alog_ref, dsk_ref, g_ref,
                       o_ref, hout_ref, cout_ref):
    hi = lax.Precision.HIGHEST
    new = xbc_ref[0]
    buf = cbuf_ref[0, 0]
    cw = cw_ref[...]
    conv = cb_ref[...] + cw[CONV_W - 1:CONV_W] * new
    for i in range(CONV_W - 1):
        conv = conv + cw[i:i + 1] * buf[i:i + 1]
    cout_ref[0] = jnp.concatenate([buf[1:CONV_W - 1], new], axis=0)
    xc = _silu(conv)
    xs = xc[:, :B_WIDTH]
    dt = _softplus(sm_ref[0] + dtb_ref[...])
    ea = jnp.exp(dt * (-jnp.exp(alog_ref[...])))
    hr = lax.broadcasted_iota(jnp.int32, (LANES, B_WIDTH), 0)
    hc = lax.broadcasted_iota(jnp.int32, (LANES, B_WIDTH), 1)
    rep = jnp.where((hc // B_HEADDIM) == hr, 1.0, 0.0)
    both = jnp.concatenate([dt, ea, jnp.zeros((6, LANES), F32)], axis=0)
    both_rep = jnp.dot(both, rep, preferred_element_type=F32, precision=hi)
    u = both_rep[0:1] * xs
    ea_rep = both_rep[1:2]
    gn = B_GROUPS * B_STATE
    rows = (B_HEADS // B_GROUPS) * B_HEADDIM
    ys = []
    for g in range(B_GROUPS):
        r0 = g * rows
        bg = xc[:, B_WIDTH + g * B_STATE:B_WIDTH + (g + 1) * B_STATE]
        cg = xc[:, B_WIDTH + gn + g * B_STATE:B_WIDTH + gn + (g + 1) * B_STATE]
        h0 = h0_ref[0, 0, r0:r0 + rows, :]
        hn = jnp.dot(_diag_rows(ea_rep[:, r0:r0 + rows]), h0, preferred_element_type=F32, precision=hi)
        hn = hn + jnp.dot(_diag_rows(u[:, r0:r0 + rows]), jnp.broadcast_to(bg, (rows, B_STATE)),
                          preferred_element_type=F32, precision=hi)
        hout_ref[0, r0:r0 + rows, :] = hn
        c8 = jnp.broadcast_to(cg, (8, B_STATE)).astype(BF16)
        ys.append(lax.dot_general(c8, hn.astype(BF16), NT_DIMS, preferred_element_type=F32)[0:1])
    y = jnp.concatenate(ys, axis=1) + dsk_ref[...] * xs
    y = y * _silu(z_ref[0])
    ms = jnp.mean(y * y, axis=-1, keepdims=True)
    o_ref[0] = y * lax.rsqrt(ms + NORM_EPS) * g_ref[...]


def _ssd_sample_call(proj3, state_conv, state_ssm4, layer, conv_w, conv_b, dt_bias, a_log, d_skip, norm_g):
    bs = proj3.shape[0]
    const = lambda b: (0, 0)
    nrow = B_HEADS * B_HEADDIM
    return pl.pallas_call(
        _ssd_sample_kernel,
        grid=(bs,),
        in_specs=[
            pl.BlockSpec((1, 1, B_CONV_DIM), lambda b: (b, 0, COL_XBC // B_CONV_DIM)),
            pl.BlockSpec((1, 1, B_WIDTH), lambda b: (b, 0, COL_Z // B_WIDTH)),
            pl.BlockSpec((1, 1, LANES), lambda b: (b, 0, COL_SMALL // LANES)),
            pl.BlockSpec((1, 1, CONV_W - 1, B_CONV_DIM), lambda b: (layer, b, 0, 0)),
            pl.BlockSpec((1, 1, nrow, B_STATE), lambda b: (layer, b, 0, 0)),
            pl.BlockSpec((CONV_W, B_CONV_DIM), const),
            pl.BlockSpec((1, B_CONV_DIM), const),
            pl.BlockSpec((1, LANES), const),
            pl.BlockSpec((1, LANES), const),
            pl.BlockSpec((1, B_WIDTH), const),
            pl.BlockSpec((1, B_WIDTH), const),
        ],
        out_specs=[
            pl.BlockSpec((1, 1, B_WIDTH), lambda b: (b, 0, 0)),
            pl.BlockSpec((1, nrow, B_STATE), lambda b: (b, 0, 0)),
            pl.BlockSpec((1, CONV_W - 1, B_CONV_DIM), lambda b: (b, 0, 0)),
        ],
        out_shape=[
            jax.ShapeDtypeStruct((bs, 1, B_WIDTH), F32),
            jax.ShapeDtypeStruct((bs, nrow, B_STATE), F32),
            jax.ShapeDtypeStruct((bs, CONV_W - 1, B_CONV_DIM), F32),
        ],
        compiler_params=_cparams(("parallel",)),
        name="ssd_sample",
    )(proj3, proj3, proj3, state_conv, state_ssm4, conv_w, conv_b.reshape(1, -1), _pad_lanes(dt_bias),
      _pad_lanes(a_log), jnp.repeat(d_skip, B_HEADDIM).reshape(1, B_WIDTH), norm_g.reshape(1, B_WIDTH))


CMP_PAGES = 16


def _compress_paged_kernel(pt_ref, pe_ref, w1_ref, *rest):
    page_refs = rest[:2 * CMP_PAGES]
    o_ref = rest[2 * CMP_PAGES]
    per_page = PAGE_SIZE // CMP_STRIDE
    outs = []
    for kv in range(2):
        acc_lo = jnp.zeros((CMP_PAGES * per_page, C_DHEAD), F32)
        acc_hi = jnp.zeros((CMP_PAGES * per_page, C_DHEAD), F32)
        for r in range(CMP_STRIDE):
            x = jnp.concatenate([page_refs[kv * CMP_PAGES + g][pl.ds(r, per_page, stride=CMP_STRIDE), :]
                                 for g in range(CMP_PAGES)], axis=0)
            x_lo = (x + pe_ref[kv, r:r + 1, :]).astype(BF16)
            x_hi = (x + pe_ref[kv, CMP_STRIDE + r:CMP_STRIDE + r + 1, :]).astype(BF16)
            acc_lo = acc_lo + jnp.dot(x_lo, w1_ref[kv, r].astype(BF16), preferred_element_type=F32)
            acc_hi = acc_hi + jnp.dot(x_hi, w1_ref[kv, CMP_STRIDE + r].astype(BF16), preferred_element_type=F32)
        outs += [acc_lo, acc_hi]
    o_ref[0] = jnp.concatenate(outs, axis=1)


def _compress_paged_call(page_table, cache5, layer, pe, w1):
    bs, n_pages = page_table.shape
    steps = n_pages // CMP_PAGES
    per_page = PAGE_SIZE // CMP_STRIDE

    def page_spec(kv, g):
        return pl.BlockSpec((None, None, PAGE_SIZE, C_DHEAD),
                            lambda b, p, pt: (layer, pt[b, p * CMP_PAGES + g], 0, kv))

    grid_spec = pltpu.PrefetchScalarGridSpec(
        num_scalar_prefetch=1,
        grid=(bs, steps),
        in_specs=[
            pl.BlockSpec((2, CMP_LEN, C_DHEAD), lambda b, p, pt: (0, 0, 0)),
            pl.BlockSpec((2, CMP_LEN, C_DHEAD, C_DHEAD), lambda b, p, pt: (0, 0, 0, 0)),
        ] + [page_spec(kv, g) for kv in range(2) for g in range(CMP_PAGES)],
        out_specs=pl.BlockSpec((1, CMP_PAGES * per_page, 4 * C_DHEAD), lambda b, p, pt: (b, p, 0)),
    )
    return pl.pallas_call(
        _compress_paged_kernel,
        grid_spec=grid_spec,
        out_shape=jax.ShapeDtypeStruct((bs, n_pages * per_page, 4 * C_DHEAD), F32),
        compiler_params=_cparams(("parallel", "arbitrary")),
        name="nsa_compress_sample",
    )(page_table, pe, w1.reshape(2, CMP_LEN, C_DHEAD, C_DHEAD), *([cache5] * (2 * CMP_PAGES)))


def _heads_to_rows(q):
    rows = [q[:, h * C_DHEAD:(h + 1) * C_DHEAD] for h in range(C_HEADS)]
    return jnp.concatenate(rows + [jnp.zeros((8 - C_HEADS, C_DHEAD), F32)], axis=0)


SEL_LANES = 384


def _nsa_select_kernel(part_ref, w2_ref, q_ref, ocmp_ref, sel_ref, *, q_pos):
    nchunk = part_ref.shape[1]
    n_cmp = (q_pos + 1 - CMP_LEN) // CMP_STRIDE + 1
    n_sel = -(-(q_pos + 1) // SEL_BLOCK)
    scale = C_DHEAD ** -0.5
    part = part_ref[0]
    kv_cmp = []
    for kv in range(2):
        lo = part[:, (2 * kv) * C_DHEAD:(2 * kv + 1) * C_DHEAD]
        hi = part[:, (2 * kv + 1) * C_DHEAD:(2 * kv + 2) * C_DHEAD]
        hid = lo + pltpu.roll(hi, nchunk - 1, 0)
        kv_cmp.append(jnp.dot(_gelu(hid).astype(BF16), w2_ref[kv].astype(BF16), preferred_element_type=F32).astype(BF16))
    q8 = _heads_to_rows(q_ref[0]).astype(BF16)
    row = lax.broadcasted_iota(jnp.int32, (8, 1), 0)
    slope = jnp.exp2(-2.0 * (row + 1).astype(F32))
    n_i = lax.broadcasted_iota(jnp.int32, (1, nchunk), 1)
    dist_c = q_pos - (n_i * CMP_STRIDE + CMP_LEN - 1)
    ok = (dist_c >= 0) & (n_i < n_cmp)
    s = lax.dot_general(q8, kv_cmp[0], NT_DIMS, preferred_element_type=F32) * scale - slope * dist_c.astype(F32)
    e, den = _masked_softmax(s, ok)
    p = jnp.where(row < C_HEADS, e / jnp.maximum(den, 1e-30), 0.0)
    ocmp_ref[0] = jnp.dot(p.astype(BF16), kv_cmp[1], preferred_element_type=F32)
    psum = jnp.broadcast_to(jnp.sum(p, axis=0, keepdims=True), (8, nchunk))
    c_i = lax.broadcasted_iota(jnp.int32, (nchunk, SEL_LANES), 0)
    j_i = lax.broadcasted_iota(jnp.int32, (nchunk, SEL_LANES), 1)
    lo_ = jnp.maximum(c_i * CMP_STRIDE, j_i * SEL_BLOCK)
    hi_ = jnp.minimum(c_i * CMP_STRIDE + CMP_LEN, (j_i + 1) * SEL_BLOCK)
    ovl = jnp.where((c_i < n_cmp) & (j_i < n_sel), jnp.maximum(hi_ - lo_, 0).astype(F32) * (1.0 / CMP_LEN), 0.0)
    imp = jnp.dot(psum, ovl, preferred_element_type=F32, precision=lax.Precision.HIGHEST)[0:1]
    lane = lax.broadcasted_iota(jnp.int32, (1, SEL_LANES), 1)
    qblk = q_pos // SEL_BLOCK
    forced = (lane == 0) | (lane == qblk) | (lane == qblk - 1)
    score = jnp.where(lane <= qblk, imp + jnp.where(forced, FORCE_SCORE, 0.0), NEG_INF)
    score = jnp.where(lane < n_sel, score, -jnp.inf)
    lane_f = lane.astype(F32)
    out_lane = lax.broadcasted_iota(jnp.int32, (1, LANES), 1)
    sel = jnp.full((1, LANES), -1.0, F32)
    for k in range(min(SEL_TOPK, n_sel)):
        m = jnp.max(score, axis=-1, keepdims=True)
        idx = jnp.min(jnp.where(score == m, lane_f, 1e9), axis=-1, keepdims=True)
        sel = jnp.where(out_lane == k, jnp.where(m > NEG_INF / 2, idx, -1.0), sel)
        score = jnp.where(lane_f == idx, -jnp.inf, score)
    sel_ref[0] = sel.astype(jnp.int32)


def _nsa_select_call(part, w2, q3, q_pos):
    bs, nchunk, _ = part.shape
    return pl.pallas_call(
        functools.partial(_nsa_select_kernel, q_pos=q_pos),
        grid=(bs,),
        in_specs=[
            pl.BlockSpec((1, nchunk, 4 * C_DHEAD), lambda b: (b, 0, 0)),
            pl.BlockSpec((2, C_DHEAD, C_DHEAD), lambda b: (0, 0, 0)),
            pl.BlockSpec((1, 1, C_WIDTH), lambda b: (b, 0, COL_CQ // C_WIDTH)),
        ],
        out_specs=[pl.BlockSpec((1, 8, C_DHEAD), lambda b: (b, 0, 0)), pl.BlockSpec((1, 1, LANES), lambda b: (b, 0, 0))],
        out_shape=[jax.ShapeDtypeStruct((bs, 8, C_DHEAD), F32), jax.ShapeDtypeStruct((bs, 1, LANES), jnp.int32)],
        compiler_params=_cparams(("parallel",)),
        name="nsa_select_sample",
    )(part, w2, q3)


def _nsa_attend_kernel(sel_ref, pt_ref, q_ref, new_ref, wnew_ref, sm_ref, ocmp_ref, win_ref, g_ref, *rest, q_pos):
    k_eff = SEL_TOPK
    blk_refs = rest[:k_eff]
    o_ref, wout_ref = rest[k_eff:]
    b = pl.program_id(0)
    scale = C_DHEAD ** -0.5
    n_past_blocks = q_pos // SEL_BLOCK
    q8f = _heads_to_rows(q_ref[0])
    q8 = q8f.astype(BF16)
    row = lax.broadcasted_iota(jnp.int32, (8, 1), 0)
    slope = jnp.exp2(-2.0 * (row + 1).astype(F32))
    lane64 = lax.broadcasted_iota(jnp.int32, (1, SEL_BLOCK), 1)

    ss, vs, oks = [], [], []
    new_sel = jnp.zeros((1, 1), jnp.int32)
    for k in range(k_eff):
        j = sel_ref[b, k]
        blk = blk_refs[k][0, 0, 0]
        s = lax.dot_general(q8, blk[:, 2 * C_DHEAD:3 * C_DHEAD].astype(BF16), NT_DIMS, preferred_element_type=F32)
        dist = q_pos - (j * SEL_BLOCK + lane64)
        ss.append(s * scale - slope * dist.astype(F32))
        oks.append(lane64 * 0 + jnp.where((j >= 0) & (j < n_past_blocks), 1, 0))
        vs.append(blk[:, 3 * C_DHEAD:].astype(BF16))
        new_sel = new_sel + jnp.where(j == n_past_blocks, 1, 0)
    s = jnp.concatenate(ss, axis=1)
    ok = jnp.concatenate(oks, axis=1) > 0
    new = new_ref[0]
    s_n = jnp.sum(q8f * new[:, 2 * C_DHEAD:3 * C_DHEAD], axis=-1, keepdims=True) * scale
    s_n = jnp.where(new_sel > 0, s_n, NEG_INF)
    s = jnp.where(ok, s, NEG_INF)
    m = jnp.maximum(jnp.max(s, axis=-1, keepdims=True), s_n)
    e = jnp.where(ok, jnp.exp(s - m), 0.0)
    e_n = jnp.where(new_sel > 0, jnp.exp(s_n - m), 0.0)
    den = jnp.sum(e, axis=-1, keepdims=True) + e_n
    o_sel = (jnp.dot(e.astype(BF16), jnp.concatenate(vs, axis=0), preferred_element_type=F32)
             + e_n * new[:, 3 * C_DHEAD:]) / den

    win = win_ref[0, 0]
    lw = win.shape[0]
    wnew = wnew_ref[0]
    wpos = lax.broadcasted_iota(jnp.int32, (1, lw), 1)
    dist_w = lw - wpos
    ok_w = dist_w < WINDOW
    s = lax.dot_general(q8, win[:, :C_DHEAD].astype(BF16), NT_DIMS, preferred_element_type=F32) * scale
    s = jnp.where(ok_w, s - slope * dist_w.astype(F32), NEG_INF)
    s_n = jnp.sum(q8f * wnew[:, :C_DHEAD], axis=-1, keepdims=True) * scale
    m = jnp.maximum(jnp.max(s, axis=-1, keepdims=True), s_n)
    e = jnp.where(ok_w, jnp.exp(s - m), 0.0)
    e_n = jnp.exp(s_n - m)
    den = jnp.sum(e, axis=-1, keepdims=True) + e_n
    o_win = (jnp.dot(e.astype(BF16), win[:, C_DHEAD:].astype(BF16), preferred_element_type=F32)
             + e_n * wnew[:, C_DHEAD:]) / den
    keep = min(WINDOW, lw + 1)
    wout_ref[0, 0:keep - 1, :] = win_ref[0, 0, lw + 1 - keep:lw, :]
    wout_ref[0, keep - 1:keep, :] = wnew

    gate = 1.0 / (1.0 + jnp.exp(-sm_ref[0]))
    outs = []
    for h in range(C_HEADS):
        g0 = gate[:, GATE_LANE0 + h:GATE_LANE0 + h + 1]
        g1 = gate[:, GATE_LANE0 + C_HEADS + h:GATE_LANE0 + C_HEADS + h + 1]
        g2 = gate[:, GATE_LANE0 + 2 * C_HEADS + h:GATE_LANE0 + 2 * C_HEADS + h + 1]
        outs.append(g0 * ocmp_ref[0, h:h + 1, :] + g1 * o_sel[h:h + 1, :] + g2 * o_win[h:h + 1, :])
    o = jnp.concatenate(outs, axis=1)
    ms = jnp.mean(o * o, axis=-1, keepdims=True)
    o_ref[0] = o * lax.rsqrt(ms + NORM_EPS) * g_ref[...]


def _nsa_attend_call(sel, page_table, proj3, ocmp, cache_blk, cache_win4, layer, out_g, q_pos):
    bs = proj3.shape[0]
    lw = cache_win4.shape[2]
    keep = min(WINDOW, lw + 1)
    n_pages = page_table.shape[1]
    halves = PAGE_SIZE // SEL_BLOCK

    def blk_spec(k):
        def imap(b, sel_r, pt_r):
            j = jnp.clip(sel_r[b, k], 0, n_pages * halves - 1)
            return (layer, pt_r[b, j // halves], j % halves, 0, 0)
        return pl.BlockSpec((1, 1, 1, SEL_BLOCK, 4 * C_DHEAD), imap)

    row3 = lambda w, col: pl.BlockSpec((1, 1, w), lambda b, s_, p_: (b, 0, col))
    grid_spec = pltpu.PrefetchScalarGridSpec(
        num_scalar_prefetch=2,
        grid=(bs,),
        in_specs=[
            row3(C_WIDTH, COL_CQ // C_WIDTH),
            row3(4 * C_DHEAD, COL_CKV // (4 * C_DHEAD)),
            row3(2 * C_DHEAD, COL_WIN // (2 * C_DHEAD)),
            row3(LANES, COL_SMALL // LANES),
            pl.BlockSpec((1, 8, C_DHEAD), lambda b, s_, p_: (b, 0, 0)),
            pl.BlockSpec((1, 1, lw, 2 * C_DHEAD), lambda b, s_, p_: (layer, b, 0, 0)),
            pl.BlockSpec((1, C_WIDTH), lambda b, s_, p_: (0, 0)),
        ] + [blk_spec(k) for k in range(SEL_TOPK)],
        out_specs=[
            pl.BlockSpec((1, 1, C_WIDTH), lambda b, s_, p_: (b, 0, 0)),
            pl.BlockSpec((1, keep, 2 * C_DHEAD), lambda b, s_, p_: (b, 0, 0)),
        ],
    )
    return pl.pallas_call(
        functools.partial(_nsa_attend_kernel, q_pos=q_pos),
        grid_spec=grid_spec,
        out_shape=[jax.ShapeDtypeStruct((bs, 1, C_WIDTH), F32), jax.ShapeDtypeStruct((bs, keep, 2 * C_DHEAD), F32)],
        compiler_params=_cparams(("arbitrary",)),
        name="nsa_attend_sample",
    )(sel, page_table, proj3, proj3, proj3, proj3, ocmp, cache_win4, out_g.reshape(1, C_WIDTH),
      *([cache_blk] * SEL_TOPK))


def _reorder_w_in(w):
    return jnp.concatenate([
        w[:, 2560:4608], w[:, 1536:2560], w[:, 0:1536], w[:, 4624:5136], w[:, 5136:5904],
        w[:, 4608:4624], w[:, 5904:5916], jnp.zeros((D_MODEL, PROJ_W - 5916), w.dtype)], axis=1).astype(BF16)


def _token_mixer_tail(xa, xb, oa, ob, oc, w_out_b, ffn_g, wq_t, subkeys, u_tab, v_tab, tm, tn, et, emit_bf16=False):
    h, hn = _outproj_call(xa, xb, oa, ob, oc, w_out_b, ffn_g, tm)
    q_t = _peer_q_call(wq_t, hn, tm)
    route = _peer_route_call(q_t, subkeys)
    return h, _peer_expert_call(hn, route, u_tab, v_tab, tn, et, emit_bf16)


def kernel(x_prompt, x_sample, cache_diff_kv, cache_nsa_kv, cache_nsa_win, state_ssm, state_conv, page_table,
           norm_mix_g, w_in, w_out, diff_lam, diff_subln_g, ssm_conv_w, ssm_conv_b, ssm_dt_bias, ssm_a_log,
           ssm_d, ssm_norm_g, nsa_pe, nsa_cmp_w1, nsa_cmp_w2, nsa_out_g, norm_ffn_g, peer_wq, peer_subkeys,
           peer_u, peer_v, norm_final_g):
    depth = w_in.shape[0]
    bp, t, _ = x_prompt.shape
    bs = x_sample.shape[0]
    past_len = page_table.shape[1] * PAGE_SIZE
    n_p = bp * t
    n_s = LANES
    tm_p = 512

    xa_p, xb_p = x_prompt.reshape(n_p, D_MODEL), None
    xa_s = jnp.pad(x_sample.reshape(bs, D_MODEL), ((0, n_s - bs), (0, 0)))
    xb_s = None
    st_p = [[] for _ in range(5)]
    st_s = [[] for _ in range(5)]
    n_pool = cache_diff_kv.shape[1]
    cache_diff4 = cache_diff_kv.reshape(depth, n_pool, PAGE_SIZE, 2 * A_WIDTH)
    cache_nsa4 = cache_nsa_kv.reshape(depth, n_pool, PAGE_SIZE, 4 * C_DHEAD)
    cache_nsa_blk = cache_nsa_kv.reshape(depth, n_pool, PAGE_SIZE // SEL_BLOCK, SEL_BLOCK, 4 * C_DHEAD)
    cache_win4 = cache_nsa_win.reshape(depth, bs, cache_nsa_win.shape[2], 2 * C_DHEAD)
    state_ssm4 = state_ssm.reshape(depth, bs, B_HEADS * B_HEADDIM, B_STATE)
    for l in range(depth):
        lam_init = 0.8 - 0.6 * math.exp(-0.3 * l)
        w_r = _reorder_w_in(w_in[l])
        w_out_b = w_out[l].astype(BF16)
        wq_t = peer_wq[l].T.astype(BF16)

        proj_s = _proj_call(xa_s, xb_s, norm_mix_g[l], w_r, n_s)[:bs]
        proj_s3 = proj_s.reshape(bs, 1, PROJ_W)
        o_a = _diff_sample_call(page_table, proj_s3, cache_diff4, l, diff_lam[l], diff_subln_g[l], lam_init)
        o_b, h_new, conv_new = _ssd_sample_call(proj_s3, state_conv, state_ssm4, l, ssm_conv_w[l], ssm_conv_b[l],
                                                ssm_dt_bias[l], ssm_a_log[l], ssm_d[l], ssm_norm_g[l])
        part = _compress_paged_call(page_table, cache_nsa4, l, nsa_pe[l], nsa_cmp_w1[l])
        o_cmp, sel = _nsa_select_call(part, nsa_cmp_w2[l], proj_s3, past_len)
        o_c, win_out = _nsa_attend_call(sel[:, 0, :SEL_TOPK], page_table, proj_s3, o_cmp, cache_nsa_blk, cache_win4, l,
                                        nsa_out_g[l], past_len)
        st_s[0].append(proj_s[:, COL_AK:COL_AK + 2 * A_WIDTH].reshape(bs, 1, 2, A_HEADS, A_DHEAD))
        st_s[1].append(proj_s[:, COL_CKV:COL_CKV + 4 * C_DHEAD].reshape(bs, 1, 4, C_DHEAD))
        st_s[2].append(win_out.reshape(bs, -1, 2, C_DHEAD))
        st_s[3].append(h_new.reshape(bs, B_HEADS, B_HEADDIM, B_STATE))
        st_s[4].append(conv_new)
        pad = lambda a: jnp.pad(a.reshape(bs, -1), ((0, n_s - bs), (0, 0)))
        xa_s, (xb_s, u_b, v_b) = _token_mixer_tail(xa_s, xb_s, pad(o_a), pad(o_b), pad(o_c), w_out_b, norm_ffn_g[l],
                                                   wq_t, peer_subkeys[l], peer_u[l], peer_v[l], n_s, n_s, 512, True)

        proj = _proj_call(xa_p, xb_p, norm_mix_g[l], w_r, tm_p)
        o_a = _diff_prompt_call(proj, bp, t, diff_lam[l], diff_subln_g[l], lam_init)
        o_b, h_ssm = _ssd_prompt_call(proj, bp, t, ssm_conv_w[l], ssm_conv_b[l], ssm_dt_bias[l], ssm_a_log[l],
                                      ssm_d[l], ssm_norm_g[l])
        kc, vc = _compress_prompt_call(proj, bp, t, nsa_pe[l], nsa_cmp_w1[l], nsa_cmp_w2[l])
        o_c = _nsa_prompt_call(proj, kc, vc, bp, t, nsa_out_g[l])
        proj3 = proj.reshape(bp, t, PROJ_W)
        st_p[0].append(proj3[:, :, COL_AK:COL_AK + 2 * A_WIDTH].reshape(bp, t, 2, A_HEADS, A_DHEAD))
        st_p[1].append(proj3[:, :, COL_CKV:COL_CKV + 4 * C_DHEAD].reshape(bp, t, 4, C_DHEAD))
        keep = min(WINDOW, t)
        st_p[2].append(proj3[:, t - keep:, COL_WIN:COL_WIN + 2 * C_DHEAD].reshape(bp, keep, 2, C_DHEAD))
        st_p[3].append(h_ssm)
        st_p[4].append(proj3[:, t - (CONV_W - 1):, COL_XBC:COL_XBC + B_CONV_DIM])
        xa_p, xb_p = _token_mixer_tail(xa_p, xb_p, o_a, o_b, o_c, w_out_b, norm_ffn_g[l], wq_t, peer_subkeys[l],
                                       u_b, v_b, tm_p, 512, 512)

    y_p = _final_norm_call(xa_p, xb_p, norm_final_g, tm_p).reshape(bp, t, D_MODEL)
    y_s = _final_norm_call(xa_s, xb_s, norm_final_g, n_s)[:bs].reshape(bs, 1, D_MODEL)
    return (y_p, y_s) + tuple(jnp.stack(s) for s in st_p) + tuple(jnp.stack(s) for s in st_s)
```

```python
import functools
import math

import jax
import jax.numpy as jnp
from jax import lax
from jax.experimental import pallas as pl
from jax.experimental.pallas import tpu as pltpu

F32 = jnp.float32
BF16 = jnp.bfloat16

D_MODEL = 2048
A_HEADS = 4
A_HALF = 64
A_DHEAD = 128
A_WIDTH = 512
B_WIDTH = 1024
B_HEADDIM = 64
B_HEADS = 16
B_GROUPS = 4
B_STATE = 128
CONV_W = 4
B_CONV_DIM = 2048
SSD_CHUNK = 128
C_HEADS = 4
C_DHEAD = 128
C_WIDTH = 512
CMP_LEN = 32
CMP_STRIDE = 16
SEL_BLOCK = 64
SEL_TOPK = 16
WINDOW = 512
PEER_HEADS = 8
PEER_NKEYS = 128
PEER_TOPK = 16
PEER_DQ = 256
PAGE_SIZE = 128
NORM_EPS = 1e-6
NEG_INF = -1e30
FORCE_SCORE = 1e4

LANES = 128
VMEM_LIMIT = 56 * 1024 * 1024

COL_XBC = 0
COL_Z = 2048
COL_AQ = 3072
COL_AK = 3584
COL_AV = 4096
COL_CQ = 4608
COL_CKV = 5120
COL_WIN = 5632
COL_SMALL = 5888
PROJ_W = 6144
GATE_LANE0 = B_HEADS

NT_DIMS = (((1,), (1,)), ((), ()))


def _cparams(sem, vmem=VMEM_LIMIT):
    return pltpu.CompilerParams(dimension_semantics=sem, vmem_limit_bytes=vmem)


def _gelu(x):
    return 0.5 * x * (1.0 + jnp.tanh(math.sqrt(2.0 / math.pi) * (x + 0.044715 * (x * x * x))))


def _silu(x):
    return x * (1.0 / (1.0 + jnp.exp(-x)))


def _softplus(x):
    return jnp.maximum(x, 0.0) + jnp.log(1.0 + jnp.exp(-jnp.abs(x)))


def _alibi_slope(h):
    if isinstance(h, int):
        return 2.0 ** (-2.0 * (h + 1))
    return jnp.exp2(jnp.full((1, 1), -2.0, F32) * (h + 1).astype(F32))


def _proj_kernel(*refs, two):
    if two:
        xa_ref, xb_ref, g_ref, w_ref, o_ref, xn_ref = refs
    else:
        xa_ref, g_ref, w_ref, o_ref, xn_ref = refs

    @pl.when(pl.program_id(1) == 0)
    def _():
        x = xa_ref[...]
        if two:
            x = x + xb_ref[...]
        ms = jnp.mean(x * x, axis=-1, keepdims=True)
        xn_ref[...] = (x * lax.rsqrt(ms + NORM_EPS) * g_ref[...]).astype(BF16)

    o_ref[...] = jnp.dot(xn_ref[...], w_ref[...], preferred_element_type=F32)


def _proj_call(xa, xb, g, w_r, tm, tc=1536):
    n = xa.shape[0]
    two = xb is not None
    xspec = pl.BlockSpec((tm, D_MODEL), lambda i, j: (i, 0))
    ins = [xa] + ([xb] if two else []) + [g.reshape(1, D_MODEL), w_r]
    specs = [xspec] + ([xspec] if two else []) + [
        pl.BlockSpec((1, D_MODEL), lambda i, j: (0, 0)),
        pl.BlockSpec((D_MODEL, tc), lambda i, j: (0, j)),
    ]
    return pl.pallas_call(
        functools.partial(_proj_kernel, two=two),
        grid=(n // tm, PROJ_W // tc),
        in_specs=specs,
        out_specs=pl.BlockSpec((tm, tc), lambda i, j: (i, j)),
        out_shape=jax.ShapeDtypeStruct((n, PROJ_W), F32),
        scratch_shapes=[pltpu.VMEM((tm, D_MODEL), BF16)],
        compiler_params=_cparams(("parallel", "arbitrary")),
        name="in_proj",
    )(*ins)


def _diff_lambda(dl, lam_init):
    a = jnp.sum(dl[0:1] * dl[1:2], axis=-1, keepdims=True)
    b = jnp.sum(dl[2:3] * dl[3:4], axis=-1, keepdims=True)
    return jnp.exp(a) - jnp.exp(b) + lam_init


CAUSAL_LEVELS = 4


def _causal_prefixes(i, nq, tq, body):
    levels = min(CAUSAL_LEVELS, nq)
    per = nq // levels
    for lv in range(levels):
        pl.when(i // per == lv)(functools.partial(body, (lv + 1) * per * tq))


def _diff_prompt_kernel(q_ref, k_ref, v_ref, dl_ref, g_ref, o_ref, *, tq, lam_init):
    h = pl.program_id(1)
    i = pl.program_id(2)
    t = k_ref.shape[0]
    scale = A_HALF ** -0.5

    def body(nk):
        lam = _diff_lambda(dl_ref[...], lam_init)
        q = q_ref[...]
        lane = lax.broadcasted_iota(jnp.int32, (1, A_DHEAD), 1)
        kb = k_ref[0:nk, :].astype(BF16)
        vb = v_ref[0:nk, :].astype(BF16)
        qpos = i * tq + lax.broadcasted_iota(jnp.int32, (tq, 1), 0)
        kpos = lax.broadcasted_iota(jnp.int32, (1, nk), 1)
        dist = qpos - kpos
        ok = dist >= 0
        bias = _alibi_slope(h) * dist.astype(F32)

        def half_softmax(c):
            qc = jnp.where((lane >= c * A_HALF) & (lane < (c + 1) * A_HALF), q, 0.0).astype(BF16)
            s = lax.dot_general(qc, kb, NT_DIMS, preferred_element_type=F32) * scale
            s = jnp.where(ok, s - bias, NEG_INF)
            m = jnp.max(s, axis=-1, keepdims=True)
            e = jnp.exp(s - m)
            return e / jnp.sum(e, axis=-1, keepdims=True)

        pd = half_softmax(0) - lam * half_softmax(1)
        o = jnp.dot(pd.astype(BF16), vb, preferred_element_type=F32)
        ms = jnp.mean(o * o, axis=-1, keepdims=True)
        o_ref[...] = o * lax.rsqrt(ms + NORM_EPS) * g_ref[...] * (1.0 - lam_init)

    _causal_prefixes(i, t // tq, tq, body)


def _diff_prompt_call(proj, bsz, t, dl, subln_g, lam_init, tq=256):
    nq = t // tq
    cq, ck, cv = COL_AQ // A_DHEAD, COL_AK // A_DHEAD, COL_AV // A_DHEAD
    return pl.pallas_call(
        functools.partial(_diff_prompt_kernel, tq=tq, lam_init=lam_init),
        grid=(bsz, A_HEADS, nq),
        in_specs=[
            pl.BlockSpec((tq, A_DHEAD), lambda b, h, i: (b * nq + i, cq + h)),
            pl.BlockSpec((t, A_DHEAD), lambda b, h, i: (b, ck + h)),
            pl.BlockSpec((t, A_DHEAD), lambda b, h, i: (b, cv + h)),
            pl.BlockSpec((4, A_HALF), lambda b, h, i: (0, 0)),
            pl.BlockSpec((1, A_DHEAD), lambda b, h, i: (0, 0)),
        ],
        out_specs=pl.BlockSpec((tq, A_DHEAD), lambda b, h, i: (b * nq + i, h)),
        out_shape=jax.ShapeDtypeStruct((bsz * t, A_WIDTH), F32),
        compiler_params=_cparams(("parallel", "parallel", "arbitrary")),
        name="diff_attn_prompt",
    )(proj, proj, proj, dl, subln_g.reshape(1, A_DHEAD))


def _ssd_prompt_kernel(xbc_ref, z_ref, sm_ref, cw_ref, cb_ref, dtb_ref, alog_ref, dsk_ref, g_ref,
                       o_ref, hout_ref, buf_ref, h_ref):
    c = pl.program_id(1)
    cs = SSD_CHUNK

    @pl.when(c == 0)
    def _():
        buf_ref[0:8, :] = jnp.zeros((8, B_CONV_DIM), F32)
        h_ref[...] = jnp.zeros_like(h_ref)

    xbc = xbc_ref[...]
    buf_ref[8:8 + cs, :] = xbc
    cw = cw_ref[...]
    conv = cb_ref[...] + cw[3:4] * xbc
    for j in range(1, CONV_W):
        conv = conv + cw[3 - j:4 - j] * buf_ref[8 - j:8 - j + cs, :]
    buf_ref[0:8, :] = xbc[cs - 8:cs, :]
    xc = _silu(conv)
    xs = xc[:, :B_WIDTH]

    dt = _softplus(sm_ref[...] + dtb_ref[...])
    a_neg = -jnp.exp(alog_ref[...])
    dta = dt * a_neg
    row = lax.broadcasted_iota(jnp.int32, (cs, cs), 0)
    col = lax.broadcasted_iota(jnp.int32, (cs, cs), 1)
    causal = row >= col
    acum = jnp.dot(causal.astype(F32), dta, preferred_element_type=F32, precision=lax.Precision.HIGHEST)
    acum_t = acum.T
    dt_t = dt.T
    lane = lax.broadcasted_iota(jnp.int32, (1, LANES), 1)
    lo = lane < B_HEADDIM

    ys = []
    for g in range(B_GROUPS):
        bg = xc[:, B_WIDTH + g * B_STATE:B_WIDTH + (g + 1) * B_STATE]
        cg = xc[:, B_WIDTH + B_GROUPS * B_STATE + g * B_STATE:B_WIDTH + B_GROUPS * B_STATE + (g + 1) * B_STATE]
        bgb = bg.astype(BF16)
        cgb = cg.astype(BF16)
        cb = lax.dot_general(cgb, bgb, NT_DIMS, preferred_element_type=F32)
        for pr in range(2):
            h0 = g * 4 + pr * 2
            xpair = xs[:, h0 * B_HEADDIM:(h0 + 2) * B_HEADDIM]
            xpb = xpair.astype(BF16)
            ydiag = []
            ecol = []
            wcol = []
            elast = []
            for hh in (h0, h0 + 1):
                a_col = acum[:, hh:hh + 1]
                a_row = acum_t[hh:hh + 1, :]
                decay = jnp.exp(jnp.where(causal, a_col - a_row, NEG_INF))
                lm = cb * decay * dt_t[hh:hh + 1, :]
                ydiag.append(jnp.dot(lm.astype(BF16), xpb, preferred_element_type=F32))
                a_last = acum[cs - 1:cs, hh:hh + 1]
                ecol.append(jnp.exp(a_col))
                wcol.append(jnp.exp(a_last - a_col) * dt[:, hh:hh + 1])
                elast.append(jnp.exp(a_last))
            hp = h_ref[h0 * B_HEADDIM:(h0 + 2) * B_HEADDIM, :]
            yoff = lax.dot_general(cgb, hp.astype(BF16), NT_DIMS, preferred_element_type=F32)
            y = jnp.where(lo, ydiag[0], ydiag[1]) + yoff * jnp.where(lo, ecol[0], ecol[1])
            ys.append(y)
            wx = xpair * jnp.where(lo, wcol[0], wcol[1])
            upd = jnp.dot(wx.T.astype(BF16), bgb, preferred_element_type=F32)
            prow = lax.broadcasted_iota(jnp.int32, (LANES, 1), 0) < B_HEADDIM
            h_ref[h0 * B_HEADDIM:(h0 + 2) * B_HEADDIM, :] = jnp.where(prow, elast[0], elast[1]) * hp + upd

    y = jnp.concatenate(ys, axis=1)
    y = y + dsk_ref[...] * xs
    y = y * _silu(z_ref[...])
    ms = jnp.mean(y * y, axis=-1, keepdims=True)
    o_ref[...] = y * lax.rsqrt(ms + NORM_EPS) * g_ref[...]

    @pl.when(c == pl.num_programs(1) - 1)
    def _():
        hout_ref[0] = h_ref[...]


def _pad_lanes(v, fill=0.0):
    v = v.reshape(1, -1).astype(F32)
    return jnp.pad(v, ((0, 0), (0, LANES - v.shape[1])), constant_values=fill)


def _ssd_prompt_call(proj, bsz, t, conv_w, conv_b, dt_bias, a_log, d_skip, norm_g):
    nc = t // SSD_CHUNK
    cs = SSD_CHUNK
    const = lambda b, c: (0, 0)
    o, hout = pl.pallas_call(
        _ssd_prompt_kernel,
        grid=(bsz, nc),
        in_specs=[
            pl.BlockSpec((cs, B_CONV_DIM), lambda b, c: (b * nc + c, COL_XBC // B_CONV_DIM)),
            pl.BlockSpec((cs, B_WIDTH), lambda b, c: (b * nc + c, COL_Z // B_WIDTH)),
            pl.BlockSpec((cs, LANES), lambda b, c: (b * nc + c, COL_SMALL // LANES)),
            pl.BlockSpec((CONV_W, B_CONV_DIM), const),
            pl.BlockSpec((1, B_CONV_DIM), const),
            pl.BlockSpec((1, LANES), const),
            pl.BlockSpec((1, LANES), const),
            pl.BlockSpec((1, B_WIDTH), const),
            pl.BlockSpec((1, B_WIDTH), const),
        ],
        out_specs=[
            pl.BlockSpec((cs, B_WIDTH), lambda b, c: (b * nc + c, 0)),
            pl.BlockSpec((1, B_HEADS * B_HEADDIM, B_STATE), lambda b, c: (b, 0, 0)),
        ],
        out_shape=[
            jax.ShapeDtypeStruct((bsz * t, B_WIDTH), F32),
            jax.ShapeDtypeStruct((bsz, B_HEADS * B_HEADDIM, B_STATE), F32),
        ],
        scratch_shapes=[pltpu.VMEM((8 + cs, B_CONV_DIM), F32), pltpu.VMEM((B_HEADS * B_HEADDIM, B_STATE), F32)],
        compiler_params=_cparams(("parallel", "arbitrary")),
        name="ssd_prompt",
    )(proj, proj, proj, conv_w, conv_b.reshape(1, -1), _pad_lanes(dt_bias), _pad_lanes(a_log),
      jnp.repeat(d_skip, B_HEADDIM).reshape(1, B_WIDTH), norm_g.reshape(1, B_WIDTH))
    return o, hout.reshape(bsz, B_HEADS, B_HEADDIM, B_STATE)


def _compress_kernel(k_ref, v_ref, pe_ref, w1_ref, w2_ref, kc_ref, vc_ref, *, nchunk):
    outs = []
    for kv, rows_ref in enumerate((k_ref, v_ref)):
        acc_lo = jnp.zeros((nchunk, C_DHEAD), F32)
        acc_hi = jnp.zeros((nchunk, C_DHEAD), F32)
        for r in range(CMP_STRIDE):
            x = rows_ref[pl.ds(r, nchunk, stride=CMP_STRIDE), :]
            x_lo = (x + pe_ref[kv, r:r + 1, :]).astype(BF16)
            x_hi = (x + pe_ref[kv, CMP_STRIDE + r:CMP_STRIDE + r + 1, :]).astype(BF16)
            acc_lo = acc_lo + jnp.dot(x_lo, w1_ref[kv, r].astype(BF16), preferred_element_type=F32)
            acc_hi = acc_hi + jnp.dot(x_hi, w1_ref[kv, CMP_STRIDE + r].astype(BF16), preferred_element_type=F32)
        hid = acc_lo + pltpu.roll(acc_hi, nchunk - 1, 0)
        outs.append(jnp.dot(_gelu(hid).astype(BF16), w2_ref[kv].astype(BF16), preferred_element_type=F32))
    kc_ref[0] = outs[0]
    vc_ref[0] = outs[1]


def _compress_prompt_call(proj, bsz, t, pe, w1, w2):
    nchunk = t // CMP_STRIDE
    shp = jax.ShapeDtypeStruct((bsz, nchunk, C_DHEAD), F32)
    return pl.pallas_call(
        functools.partial(_compress_kernel, nchunk=nchunk),
        grid=(bsz,),
        in_specs=[
            pl.BlockSpec((t, C_DHEAD), lambda b: (b, COL_CKV // C_DHEAD)),
            pl.BlockSpec((t, C_DHEAD), lambda b: (b, COL_CKV // C_DHEAD + 1)),
            pl.BlockSpec((2, CMP_LEN, C_DHEAD), lambda b: (0, 0, 0)),
            pl.BlockSpec((2, CMP_LEN, C_DHEAD, C_DHEAD), lambda b: (0, 0, 0, 0)),
            pl.BlockSpec((2, C_DHEAD, C_DHEAD), lambda b: (0, 0, 0)),
        ],
        out_specs=[pl.BlockSpec((1, nchunk, C_DHEAD), lambda b: (b, 0, 0))] * 2,
        out_shape=[shp, shp],
        compiler_params=_cparams(("parallel",)),
        name="nsa_compress_prompt",
    )(proj, proj, pe, w1.reshape(2, CMP_LEN, C_DHEAD, C_DHEAD), w2)


def _masked_softmax(s, ok):
    s = jnp.where(ok, s, NEG_INF)
    m = jnp.max(s, axis=-1, keepdims=True)
    e = jnp.where(ok, jnp.exp(s - m), 0.0)
    return e, jnp.sum(e, axis=-1, keepdims=True)


def _topk_mask_lanes(score, k, n):
    lane = lax.broadcasted_iota(jnp.int32, (1, LANES), 1)
    rank = jnp.zeros(score.shape, F32)
    for i in range(n):
        ci = score[:, i:i + 1]
        beats = (ci > score) | ((ci == score) & (lane > i))
        rank = rank + jnp.where(beats, 1.0, 0.0)
    return (rank < k) & (lane < n)


def _nsa_prompt_kernel(q_ref, ks_ref, vs_ref, kw_ref, vw_ref, kc_ref, vc_ref, sm_ref, g_ref, o_ref, *, tq, n_cmp):
    i = pl.program_id(1)
    t = ks_ref.shape[0]
    n_sel = t // SEL_BLOCK
    scale = C_DHEAD ** -0.5
    sel_shift = SEL_BLOCK.bit_length() - 1
    wlen = min(t, WINDOW + tq)

    def body(nk):
        qpos = i * tq + lax.broadcasted_iota(jnp.int32, (tq, 1), 0)
        lane = lax.broadcasted_iota(jnp.int32, (1, LANES), 1)

        cmp_end = lane * CMP_STRIDE + (CMP_LEN - 1)
        dist_c = qpos - cmp_end
        ok_c = (dist_c >= 0) & (lane < n_cmp)
        dist_cf = dist_c.astype(F32)
        kcb = kc_ref[0].astype(BF16)
        vcb = vc_ref[0].astype(BF16)
        qs = [q_ref[:, h * C_DHEAD:(h + 1) * C_DHEAD].astype(BF16) for h in range(C_HEADS)]
        o_cmp = []
        psum = jnp.zeros((tq, LANES), F32)
        for h in range(C_HEADS):
            s = lax.dot_general(qs[h], kcb, NT_DIMS, preferred_element_type=F32) * scale
            s = s - _alibi_slope(h) * dist_cf
            e, den = _masked_softmax(s, ok_c)
            p = e / jnp.maximum(den, 1e-30)
            psum = psum + p
            o_cmp.append(jnp.dot(p.astype(BF16), vcb, preferred_element_type=F32))

        n_i = lax.broadcasted_iota(jnp.int32, (LANES, LANES), 0)
        j_i = lax.broadcasted_iota(jnp.int32, (LANES, LANES), 1)
        lo_ = jnp.maximum(n_i * CMP_STRIDE, j_i * SEL_BLOCK)
        hi_ = jnp.minimum(n_i * CMP_STRIDE + CMP_LEN, (j_i + 1) * SEL_BLOCK)
        ovl = jnp.maximum(hi_ - lo_, 0).astype(F32) * (1.0 / CMP_LEN)
        ovl = jnp.where((n_i < n_cmp) & (j_i < n_sel), ovl, 0.0)
        imp = jnp.dot(psum, ovl, preferred_element_type=F32, precision=lax.Precision.HIGHEST)
        qblk = qpos >> sel_shift
        sel_valid = lane <= qblk
        forced = (lane == 0) | (lane == qblk) | (lane == qblk - 1)
        score = jnp.where(sel_valid, imp + jnp.where(forced, FORCE_SCORE, 0.0), NEG_INF)
        score = jnp.where(lane < n_sel, score, -jnp.inf)
        chosen = _topk_mask_lanes(score, min(SEL_TOPK, n_sel), n_sel) & sel_valid
        e_j = lax.broadcasted_iota(jnp.int32, (LANES, nk), 0)
        e_k = lax.broadcasted_iota(jnp.int32, (LANES, nk), 1)
        expand = jnp.where((e_k >> sel_shift) == e_j, 1.0, 0.0).astype(BF16)
        key_sel = jnp.dot(jnp.where(chosen, 1.0, 0.0).astype(BF16), expand, preferred_element_type=F32) > 0.5
        dist = qpos - lax.broadcasted_iota(jnp.int32, (1, nk), 1)
        dist_f = dist.astype(F32)
        ok_s = key_sel & (dist >= 0)
        w0 = pl.multiple_of(jnp.clip(i * tq - WINDOW, 0, t - wlen), 8)
        dist_w = qpos - (w0 + lax.broadcasted_iota(jnp.int32, (1, wlen), 1))
        dist_wf = dist_w.astype(F32)
        ok_w = (dist_w >= 0) & (dist_w < WINDOW)

        ksb = ks_ref[0:nk, :].astype(BF16)
        vsb = vs_ref[0:nk, :].astype(BF16)
        kwb = kw_ref[pl.ds(w0, wlen), :].astype(BF16)
        vwb = vw_ref[pl.ds(w0, wlen), :].astype(BF16)
        gate = 1.0 / (1.0 + jnp.exp(-sm_ref[...]))
        outs = []
        for h in range(C_HEADS):
            s = lax.dot_general(qs[h], ksb, NT_DIMS, preferred_element_type=F32) * scale - _alibi_slope(h) * dist_f
            e, den = _masked_softmax(s, ok_s)
            o_sel = jnp.dot((e / den).astype(BF16), vsb, preferred_element_type=F32)
            s = lax.dot_general(qs[h], kwb, NT_DIMS, preferred_element_type=F32) * scale - _alibi_slope(h) * dist_wf
            e, den = _masked_softmax(s, ok_w)
            o_win = jnp.dot((e / den).astype(BF16), vwb, preferred_element_type=F32)
            g0 = gate[:, GATE_LANE0 + h:GATE_LANE0 + h + 1]
            g1 = gate[:, GATE_LANE0 + C_HEADS + h:GATE_LANE0 + C_HEADS + h + 1]
            g2 = gate[:, GATE_LANE0 + 2 * C_HEADS + h:GATE_LANE0 + 2 * C_HEADS + h + 1]
            outs.append(g0 * o_cmp[h] + g1 * o_sel + g2 * o_win)
        o = jnp.concatenate(outs, axis=1)
        ms = jnp.mean(o * o, axis=-1, keepdims=True)
        o_ref[...] = o * lax.rsqrt(ms + NORM_EPS) * g_ref[...]

    _causal_prefixes(i, t // tq, tq, body)


def _nsa_prompt_call(proj, kc, vc, bsz, t, out_g, tq=256):
    nq = t // tq
    n_cmp = (t - CMP_LEN) // CMP_STRIDE + 1
    c0 = COL_CKV // C_DHEAD
    w0 = COL_WIN // C_DHEAD
    kvspec = lambda col: pl.BlockSpec((t, C_DHEAD), lambda b, i: (b, col))
    return pl.pallas_call(
        functools.partial(_nsa_prompt_kernel, tq=tq, n_cmp=n_cmp),
        grid=(bsz, nq),
        in_specs=[
            pl.BlockSpec((tq, C_WIDTH), lambda b, i: (b * nq + i, COL_CQ // C_WIDTH)),
            kvspec(c0 + 2), kvspec(c0 + 3), kvspec(w0), kvspec(w0 + 1),
            pl.BlockSpec((1, kc.shape[1], C_DHEAD), lambda b, i: (b, 0, 0)),
            pl.BlockSpec((1, kc.shape[1], C_DHEAD), lambda b, i: (b, 0, 0)),
            pl.BlockSpec((tq, LANES), lambda b, i: (b * nq + i, COL_SMALL // LANES)),
            pl.BlockSpec((1, C_WIDTH), lambda b, i: (0, 0)),
        ],
        out_specs=pl.BlockSpec((tq, C_WIDTH), lambda b, i: (b * nq + i, 0)),
        out_shape=jax.ShapeDtypeStruct((bsz * t, C_WIDTH), F32),
        compiler_params=_cparams(("parallel", "arbitrary")),
        name="nsa_attn_prompt",
    )(proj, proj, proj, proj, proj, kc, vc, proj, out_g.reshape(1, C_WIDTH))


def _outproj_kernel(*refs, two):
    if two:
        xa_ref, xb_ref, oa_ref, ob_ref, oc_ref, w_ref, g_ref, h_ref, hnt_ref = refs
    else:
        xa_ref, oa_ref, ob_ref, oc_ref, w_ref, g_ref, h_ref, hnt_ref = refs
    x = xa_ref[...]
    if two:
        x = x + xb_ref[...]
    mixed = jnp.dot(oa_ref[...].astype(BF16), w_ref[0:A_WIDTH, :], preferred_element_type=F32)
    mixed = mixed + jnp.dot(ob_ref[...].astype(BF16), w_ref[A_WIDTH:A_WIDTH + B_WIDTH, :], preferred_element_type=F32)
    mixed = mixed + jnp.dot(oc_ref[...].astype(BF16), w_ref[A_WIDTH + B_WIDTH:, :], preferred_element_type=F32)
    h = x + mixed
    h_ref[...] = h
    ms = jnp.mean(h * h, axis=-1, keepdims=True)
    hnt_ref[...] = (h * lax.rsqrt(ms + NORM_EPS) * g_ref[...]).T.astype(BF16)


def _outproj_call(xa, xb, oa, ob, oc, w_out_b, g, tm):
    n = xa.shape[0]
    two = xb is not None
    row = lambda w: pl.BlockSpec((tm, w), lambda i: (i, 0))
    ins = [xa] + ([xb] if two else []) + [oa, ob, oc, w_out_b, g.reshape(1, D_MODEL)]
    specs = [row(D_MODEL)] + ([row(D_MODEL)] if two else []) + [
        row(A_WIDTH), row(B_WIDTH), row(C_WIDTH),
        pl.BlockSpec((D_MODEL, D_MODEL), lambda i: (0, 0)),
        pl.BlockSpec((1, D_MODEL), lambda i: (0, 0)),
    ]
    return pl.pallas_call(
        functools.partial(_outproj_kernel, two=two),
        grid=(n // tm,),
        in_specs=specs,
        out_specs=[row(D_MODEL), pl.BlockSpec((D_MODEL, tm), lambda i: (0, i))],
        out_shape=[jax.ShapeDtypeStruct((n, D_MODEL), F32), jax.ShapeDtypeStruct((D_MODEL, n), BF16)],
        compiler_params=_cparams(("parallel",)),
        name="out_proj",
    )(*ins)


def _peer_q_kernel(wqt_ref, hnt_ref, qt_ref):
    qt_ref[...] = jnp.dot(wqt_ref[...], hnt_ref[...], preferred_element_type=F32)


def _peer_q_call(wq_t, hn_t, tm):
    n = hn_t.shape[1]
    dq = wq_t.shape[0]
    return pl.pallas_call(
        _peer_q_kernel,
        grid=(n // tm,),
        in_specs=[pl.BlockSpec((dq, D_MODEL), lambda i: (0, 0)), pl.BlockSpec((D_MODEL, tm), lambda i: (0, i))],
        out_specs=pl.BlockSpec((dq, tm), lambda i: (0, i)),
        out_shape=jax.ShapeDtypeStruct((dq, n), F32),
        compiler_params=_cparams(("parallel",)),
        name="peer_query",
    )(wq_t, hn_t)


NOT_RANKED = 99.0


def _top_rows(s, pos, k, want_rank):
    rank = jnp.full(s.shape, NOT_RANKED, F32) if want_rank else None
    vals, picks = [], []
    for j in range(k):
        m = jnp.max(s, axis=0, keepdims=True)
        idx = jnp.min(jnp.where(s == m, pos, 1e9), axis=0, keepdims=True)
        hit = pos == idx
        if want_rank:
            rank = jnp.where(hit, float(j), rank)
        s = jnp.where(hit, -jnp.inf, s)
        vals.append(m)
        picks.append(idx)
    return jnp.concatenate(vals, axis=0), jnp.concatenate(picks, axis=0), rank


PAIR_ROWS = PEER_TOPK + 7 * 8 + 8


def _pair_candidates(v1, v2):
    tn = v1.shape[1]
    parts = [v1[0:1, :] + v2] + [v1[a:a + 1, :] + v2[0:8, :] for a in range(1, 8)] + [v1[8:16, :] + v2[0:1, :]]
    r = lax.broadcasted_iota(jnp.int32, (PAIR_ROWS, tn), 0)
    mid = r - PEER_TOPK
    pos = jnp.where(r < PEER_TOPK, r,
                    jnp.where(r < PEER_TOPK + 56, ((mid >> 3) + 1) * PEER_TOPK + (mid & 7), (r - 64) * PEER_TOPK))
    return jnp.concatenate(parts, axis=0), pos.astype(F32)


def _peer_route_kernel(qt_ref, sk_ref, lim_ref, coef_ref, rank2_ref, e2_ref, *, heads):
    half = PEER_DQ // 2
    row = lax.broadcasted_iota(jnp.int32, (PEER_NKEYS, qt_ref.shape[1]), 0).astype(F32)
    for hh in range(heads):
        q = qt_ref[hh * PEER_DQ:(hh + 1) * PEER_DQ, :]
        s1 = jnp.dot(sk_ref[hh, 0], q[0:half, :], preferred_element_type=F32, precision=lax.Precision.HIGHEST)
        s2 = jnp.dot(sk_ref[hh, 1], q[half:, :], preferred_element_type=F32, precision=lax.Precision.HIGHEST)
        v1, _, rank1 = _top_rows(s1, row, PEER_TOPK, True)
        v2, _, rank2 = _top_rows(s2, row, PEER_TOPK, True)
        cand, cpos = _pair_candidates(v1, v2)
        top, pos, _ = _top_rows(cand, cpos, PEER_TOPK, False)
        z = jnp.sum(jnp.exp(top - top[0:1, :]), axis=0, keepdims=True)
        a_of = jnp.floor(pos * (1.0 / PEER_TOPK))
        lim = jnp.zeros(s1.shape, F32)
        for a in range(PEER_TOPK):
            cnt = jnp.sum(jnp.where(a_of == float(a), 1.0, 0.0), axis=0, keepdims=True)
            lim = jnp.where(rank1 == float(a), cnt, lim)
        lim_ref[hh] = lim
        coef_ref[hh] = jnp.exp(s1 - v1[0:1, :]) / z
        rank2_ref[hh] = rank2.astype(BF16)
        e2_ref[hh] = jnp.exp(s2 - v2[0:1, :]).astype(BF16)


def _peer_route_call(q_t, subkeys, tn=LANES, heads=2):
    n = q_t.shape[1]
    shp = lambda dt: jax.ShapeDtypeStruct((PEER_HEADS, PEER_NKEYS, n), dt)
    ospec = pl.BlockSpec((heads, PEER_NKEYS, tn), lambda j, h: (h, 0, j))
    return pl.pallas_call(
        functools.partial(_peer_route_kernel, heads=heads),
        grid=(n // tn, PEER_HEADS // heads),
        in_specs=[
            pl.BlockSpec((heads * PEER_DQ, tn), lambda j, h: (h, j)),
            pl.BlockSpec((heads, 2, PEER_NKEYS, PEER_DQ // 2), lambda j, h: (h, 0, 0, 0)),
        ],
        out_specs=[ospec] * 4,
        out_shape=[shp(F32), shp(F32), shp(BF16), shp(BF16)],
        compiler_params=_cparams(("parallel", "arbitrary")),
        name="peer_route",
    )(q_t, subkeys)


def _peer_expert_kernel(hnt_ref, lim_ref, coef_ref, rank2_ref, e2_ref, u_ref, v_ref, o_ref, *rest, et, emit):
    acc_ref = rest[-1]
    t = pl.program_id(1)

    @pl.when(t == 0)
    def _():
        acc_ref[...] = jnp.zeros_like(acc_ref)

    if emit:
        ub = u_ref[...].astype(BF16)
        vtb = v_ref[...].T.astype(BF16)
        rest[0][...] = ub
        rest[1][...] = vtb
    else:
        ub = u_ref[...]
        vtb = v_ref[...]
    tn = hnt_ref.shape[1]
    hid = jnp.dot(ub, hnt_ref[...], preferred_element_type=F32)
    acts = []
    for ii in range(et // PEER_NKEYS):
        i1 = t * (et // PEER_NKEYS) + ii
        gate = jnp.zeros((PEER_NKEYS, tn), BF16)
        for h in range(PEER_HEADS):
            lim = lim_ref[h, pl.ds(i1, 1), :].astype(BF16)
            coef = coef_ref[h, pl.ds(i1, 1), :].astype(BF16)
            gate = gate + jnp.where(rank2_ref[h] < lim, e2_ref[h], jnp.zeros((), BF16)) * coef
        acts.append(gate * _gelu(hid[ii * PEER_NKEYS:(ii + 1) * PEER_NKEYS, :]).astype(BF16))
    acc_ref[...] += jnp.dot(vtb, jnp.concatenate(acts, axis=0), preferred_element_type=F32)

    @pl.when(t == pl.num_programs(1) - 1)
    def _():
        o_ref[...] = acc_ref[...].T


def _peer_expert_call(hn_t, route, u_tab, v_tab, tn, et, emit=False):
    n = hn_t.shape[1]
    n_exp = u_tab.shape[0]
    rspec = pl.BlockSpec((PEER_HEADS, PEER_NKEYS, tn), lambda j, t: (0, 0, j))
    uspec = pl.BlockSpec((et, D_MODEL), lambda j, t: (t, 0))
    vtspec = pl.BlockSpec((D_MODEL, et), lambda j, t: (0, t))
    ospec = pl.BlockSpec((tn, D_MODEL), lambda j, t: (j, 0))
    oshape = jax.ShapeDtypeStruct((n, D_MODEL), F32)
    if emit:
        assert n == tn
        out_specs = [ospec, uspec, vtspec]
        out_shape = [oshape, jax.ShapeDtypeStruct((n_exp, D_MODEL), BF16), jax.ShapeDtypeStruct((D_MODEL, n_exp), BF16)]
    else:
        out_specs, out_shape = ospec, oshape
    return pl.pallas_call(
        functools.partial(_peer_expert_kernel, et=et, emit=emit),
        grid=(n // tn, n_exp // et),
        in_specs=[pl.BlockSpec((D_MODEL, tn), lambda j, t: (0, j)), rspec, rspec, rspec, rspec, uspec,
                  uspec if emit else vtspec],
        out_specs=out_specs,
        out_shape=out_shape,
        scratch_shapes=[pltpu.VMEM((D_MODEL, tn), F32)],
        compiler_params=_cparams(("parallel", "arbitrary")),
        name="peer_experts",
    )(hn_t, *route, u_tab, v_tab)


def _final_norm_kernel(xa_ref, xb_ref, g_ref, o_ref):
    x = xa_ref[...] + xb_ref[...]
    ms = jnp.mean(x * x, axis=-1, keepdims=True)
    o_ref[...] = x * lax.rsqrt(ms + NORM_EPS) * g_ref[...]


def _final_norm_call(xa, xb, g, tm):
    n = xa.shape[0]
    row = pl.BlockSpec((tm, D_MODEL), lambda i: (i, 0))
    return pl.pallas_call(
        _final_norm_kernel,
        grid=(n // tm,),
        in_specs=[row, row, pl.BlockSpec((1, D_MODEL), lambda i: (0, 0))],
        out_specs=row,
        out_shape=jax.ShapeDtypeStruct((n, D_MODEL), F32),
        compiler_params=_cparams(("parallel",)),
        name="final_norm",
    )(xa, xb, g.reshape(1, D_MODEL))


DIFF_PAGES = 8


def _diff_sample_kernel(pt_ref, q_ref, knew_ref, vnew_ref, dl_ref, g_ref, *rest, past_len, lam_init):
    page_refs = rest[:DIFF_PAGES]
    o_ref, m_ref, l_ref, acc_ref = rest[DIFF_PAGES:]
    p = pl.program_id(1)
    scale = A_HALF ** -0.5
    nrow = 2 * A_HEADS
    per_key = 2 * A_HEADS

    @pl.when(p == 0)
    def _():
        m_ref[...] = jnp.full(m_ref.shape, NEG_INF, F32)
        l_ref[...] = jnp.zeros(l_ref.shape, F32)
        acc_ref[...] = jnp.zeros(acc_ref.shape, F32)

    row = lax.broadcasted_iota(jnp.int32, (nrow, 1), 0)
    lane = lax.broadcasted_iota(jnp.int32, (1, A_DHEAD), 1)
    slope = jnp.exp2(-2.0 * ((row >> 1) + 1).astype(F32))
    q = q_ref[0]
    q2 = [jnp.where(((row >> 1) == h) & ((lane >= A_HALF) == ((row & 1) == 1)), q[:, h * A_DHEAD:(h + 1) * A_DHEAD], 0.0)
          for h in range(A_HEADS)]
    q2b = [x.astype(BF16) for x in q2]
    ss = []
    for g in range(DIFF_PAGES):
        sg = None
        for h in range(A_HEADS):
            kh = page_refs[g][0, 0, pl.ds(h, PAGE_SIZE, stride=per_key), :].astype(BF16)
            d = lax.dot_general(q2b[h], kh, NT_DIMS, preferred_element_type=F32)
            sg = d if sg is None else sg + d
        ss.append(sg)
    s = jnp.concatenate(ss, axis=1) * scale
    nk = DIFF_PAGES * PAGE_SIZE
    kpos = p * nk + lax.broadcasted_iota(jnp.int32, (1, nk), 1)
    s = s - slope * (past_len - kpos).astype(F32)
    m_old = m_ref[:, 0:1]
    m_new = jnp.maximum(m_old, jnp.max(s, axis=-1, keepdims=True))
    alpha = jnp.exp(m_old - m_new)
    e = jnp.exp(s - m_new)
    eb = e.astype(BF16)
    l_new = alpha * l_ref[:, 0:1] + jnp.sum(e, axis=-1, keepdims=True)
    pv = jnp.zeros((nrow, A_DHEAD), F32)
    for h in range(A_HEADS):
        vh = jnp.concatenate([page_refs[g][0, 0, pl.ds(A_HEADS + h, PAGE_SIZE, stride=per_key), :].astype(BF16)
                              for g in range(DIFF_PAGES)], axis=0)
        pv = pv + jnp.where((row >> 1) == h, jnp.dot(eb, vh, preferred_element_type=F32), 0.0)
    acc = alpha * acc_ref[...] + pv
    m_ref[...] = jnp.broadcast_to(m_new, m_ref.shape)
    l_ref[...] = jnp.broadcast_to(l_new, l_ref.shape)
    acc_ref[...] = acc

    @pl.when(p == pl.num_programs(1) - 1)
    def _():
        knew = knew_ref[0]
        vnew = vnew_ref[0]
        s_n = jnp.zeros((nrow, 1), F32)
        for h in range(A_HEADS):
            s_n = s_n + jnp.sum(q2[h] * knew[:, h * A_DHEAD:(h + 1) * A_DHEAD], axis=-1, keepdims=True)
        s_n = s_n * scale
        v8 = jnp.concatenate([vnew[:, (r // 2) * A_DHEAD:(r // 2 + 1) * A_DHEAD] for r in range(nrow)], axis=0)
        m_f = jnp.maximum(m_new, s_n)
        a_f = jnp.exp(m_new - m_f)
        e_n = jnp.exp(s_n - m_f)
        o8 = (a_f * acc + e_n * v8) / (a_f * l_new + e_n)
        lam = _diff_lambda(dl_ref[...], lam_init)
        outs = []
        for h in range(A_HEADS):
            oh = o8[2 * h:2 * h + 1, :] - lam * o8[2 * h + 1:2 * h + 2, :]
            ms = jnp.mean(oh * oh, axis=-1, keepdims=True)
            outs.append(oh * lax.rsqrt(ms + NORM_EPS) * g_ref[...] * (1.0 - lam_init))
        o_ref[0] = jnp.concatenate(outs, axis=1)


def _diff_sample_call(page_table, proj3, cache4, layer, dl, subln_g, lam_init):
    bs, n_pages = page_table.shape
    past_len = n_pages * PAGE_SIZE
    steps = n_pages // DIFF_PAGES

    def page_spec(g):
        return pl.BlockSpec((1, 1, PAGE_SIZE * 2 * A_HEADS, A_DHEAD),
                            lambda b, p, pt: (layer, pt[b, p * DIFF_PAGES + g], 0, 0))

    grid_spec = pltpu.PrefetchScalarGridSpec(
        num_scalar_prefetch=1,
        grid=(bs, steps),
        in_specs=[
            pl.BlockSpec((1, 1, A_WIDTH), lambda b, p, pt: (b, 0, COL_AQ // A_WIDTH)),
            pl.BlockSpec((1, 1, A_WIDTH), lambda b, p, pt: (b, 0, COL_AK // A_WIDTH)),
            pl.BlockSpec((1, 1, A_WIDTH), lambda b, p, pt: (b, 0, COL_AV // A_WIDTH)),
            pl.BlockSpec((4, A_HALF), lambda b, p, pt: (0, 0)),
            pl.BlockSpec((1, A_DHEAD), lambda b, p, pt: (0, 0)),
        ] + [page_spec(g) for g in range(DIFF_PAGES)],
        out_specs=pl.BlockSpec((1, 1, A_WIDTH), lambda b, p, pt: (b, 0, 0)),
        scratch_shapes=[pltpu.VMEM((2 * A_HEADS, LANES), F32), pltpu.VMEM((2 * A_HEADS, LANES), F32),
                        pltpu.VMEM((2 * A_HEADS, A_DHEAD), F32)],
    )
    return pl.pallas_call(
        functools.partial(_diff_sample_kernel, past_len=past_len, lam_init=lam_init),
        grid_spec=grid_spec,
        out_shape=jax.ShapeDtypeStruct((bs, 1, A_WIDTH), F32),
        compiler_params=_cparams(("parallel", "arbitrary")),
        name="diff_attn_sample",
    )(page_table, proj3, proj3, proj3, dl, subln_g.reshape(1, A_DHEAD), *([cache4] * DIFF_PAGES))


def _diag_rows(vec):
    n = vec.shape[1]
    r = lax.broadcasted_iota(jnp.int32, (n, n), 0)
    c = lax.broadcasted_iota(jnp.int32, (n, n), 1)
    return jnp.where(r == c, vec, 0.0)


def _ssd_sample_kernel(xbc_ref, z_ref, sm_ref, cbuf_ref, h0_ref, cw_ref, cb_ref, dtb_ref, alog_ref, dsk_ref, g_ref,
                       o_ref, hout_ref, cout_ref):
    hi = lax.Precision.HIGHEST
    new = xbc_ref[0]
    buf = cbuf_ref[0, 0]
    cw = cw_ref[...]
    conv = cb_ref[...] + cw[CONV_W - 1:CONV_W] * new
    for i in range(CONV_W - 1):
        conv = conv + cw[i:i + 1] * buf[i:i + 1]
    cout_ref[0] = jnp.concatenate([buf[1:CONV_W - 1], new], axis=0)
    xc = _silu(conv)
    xs = xc[:, :B_WIDTH]
    dt = _softplus(sm_ref[0] + dtb_ref[...])
    ea = jnp.exp(dt * (-jnp.exp(alog_ref[...])))
    hr = lax.broadcasted_iota(jnp.int32, (LANES, B_WIDTH), 0)
    hc = lax.broadcasted_iota(jnp.int32, (LANES, B_WIDTH), 1)
    rep = jnp.where((hc // B_HEADDIM) == hr, 1.0, 0.0)
    both = jnp.concatenate([dt, ea, jnp.zeros((6, LANES), F32)], axis=0)
    both_rep = jnp.dot(both, rep, preferred_element_type=F32, precision=hi)
    u = both_rep[0:1] * xs
    ea_rep = both_rep[1:2]
    gn = B_GROUPS * B_STATE
    rows = (B_HEADS // B_GROUPS) * B_HEADDIM
    ys = []
    for g in range(B_GROUPS):
        r0 = g * rows
        bg = xc[:, B_WIDTH + g * B_STATE:B_WIDTH + (g + 1) * B_STATE]
        cg = xc[:, B_WIDTH + gn + g * B_STATE:B_WIDTH + gn + (g + 1) * B_STATE]
        h0 = h0_ref[0, 0, r0:r0 + rows, :]
        hn = jnp.dot(_diag_rows(ea_rep[:, r0:r0 + rows]), h0, preferred_element_type=F32, precision=hi)
        hn = hn + jnp.dot(_diag_rows(u[:, r0:r0 + rows]), jnp.broadcast_to(bg, (rows, B_STATE)),
                          preferred_element_type=F32, precision=hi)
        hout_ref[0, r0:r0 + rows, :] = hn
        c8 = jnp.broadcast_to(cg, (8, B_STATE)).astype(BF16)
        ys.append(lax.dot_general(c8, hn.astype(BF16), NT_DIMS, preferred_element_type=F32)[0:1])
    y = jnp.concatenate(ys, axis=1) + dsk_ref[...] * xs
    y = y * _silu(z_ref[0])
    ms = jnp.mean(y * y, axis=-1, keepdims=True)
    o_ref[0] = y * lax.rsqrt(ms + NORM_EPS) * g_ref[...]


def _ssd_sample_call(proj3, state_conv, state_ssm4, layer, conv_w, conv_b, dt_bias, a_log, d_skip, norm_g):
    bs = proj3.shape[0]
    const = lambda b: (0, 0)
    nrow = B_HEADS * B_HEADDIM
    return pl.pallas_call(
        _ssd_sample_kernel,
        grid=(bs,),
        in_specs=[
            pl.BlockSpec((1, 1, B_CONV_DIM), lambda b: (b, 0, COL_XBC // B_CONV_DIM)),
            pl.BlockSpec((1, 1, B_WIDTH), lambda b: (b, 0, COL_Z // B_WIDTH)),
            pl.BlockSpec((1, 1, LANES), lambda b: (b, 0, COL_SMALL // LANES)),
            pl.BlockSpec((1, 1, CONV_W - 1, B_CONV_DIM), lambda b: (layer, b, 0, 0)),
            pl.BlockSpec((1, 1, nrow, B_STATE), lambda b: (layer, b, 0, 0)),
            pl.BlockSpec((CONV_W, B_CONV_DIM), const),
            pl.BlockSpec((1, B_CONV_DIM), const),
            pl.BlockSpec((1, LANES), const),
            pl.BlockSpec((1, LANES), const),
            pl.BlockSpec((1, B_WIDTH), const),
            pl.BlockSpec((1, B_WIDTH), const),
        ],
        out_specs=[
            pl.BlockSpec((1, 1, B_WIDTH), lambda b: (b, 0, 0)),
            pl.BlockSpec((1, nrow, B_STATE), lambda b: (b, 0, 0)),
            pl.BlockSpec((1, CONV_W - 1, B_CONV_DIM), lambda b: (b, 0, 0)),
        ],
        out_shape=[
            jax.ShapeDtypeStruct((bs, 1, B_WIDTH), F32),
            jax.ShapeDtypeStruct((bs, nrow, B_STATE), F32),
            jax.ShapeDtypeStruct((bs, CONV_W - 1, B_CONV_DIM), F32),
        ],
        compiler_params=_cparams(("parallel",)),
        name="ssd_sample",
    )(proj3, proj3, proj3, state_conv, state_ssm4, conv_w, conv_b.reshape(1, -1), _pad_lanes(dt_bias),
      _pad_lanes(a_log), jnp.repeat(d_skip, B_HEADDIM).reshape(1, B_WIDTH), norm_g.reshape(1, B_WIDTH))


CMP_PAGES = 32
NSA_ROWS = 4


def _compress_paged_kernel(pt_ref, pe_ref, w1_ref, *rest):
    page_refs = rest[:CMP_PAGES]
    o_ref = rest[CMP_PAGES]
    per_page = PAGE_SIZE // CMP_STRIDE
    outs = []
    for kv in range(2):
        acc_lo = jnp.zeros((CMP_PAGES * per_page, C_DHEAD), F32)
        acc_hi = jnp.zeros((CMP_PAGES * per_page, C_DHEAD), F32)
        for r in range(CMP_STRIDE):
            x = jnp.concatenate([page_refs[g][pl.ds(NSA_ROWS * r + kv, per_page, stride=NSA_ROWS * CMP_STRIDE), :]
                                 for g in range(CMP_PAGES)], axis=0)
            x_lo = (x + pe_ref[kv, r:r + 1, :]).astype(BF16)
            x_hi = (x + pe_ref[kv, CMP_STRIDE + r:CMP_STRIDE + r + 1, :]).astype(BF16)
            acc_lo = acc_lo + jnp.dot(x_lo, w1_ref[kv, r].astype(BF16), preferred_element_type=F32)
            acc_hi = acc_hi + jnp.dot(x_hi, w1_ref[kv, CMP_STRIDE + r].astype(BF16), preferred_element_type=F32)
        outs += [acc_lo, acc_hi]
    o_ref[0] = jnp.concatenate(outs, axis=1)


def _compress_paged_call(page_table, cache4, layer, pe, w1):
    bs, n_pages = page_table.shape
    steps = n_pages // CMP_PAGES
    per_page = PAGE_SIZE // CMP_STRIDE

    def page_spec(g):
        return pl.BlockSpec((None, None, PAGE_SIZE * NSA_ROWS, C_DHEAD),
                            lambda b, p, pt: (layer, pt[b, p * CMP_PAGES + g], 0, 0))

    grid_spec = pltpu.PrefetchScalarGridSpec(
        num_scalar_prefetch=1,
        grid=(bs, steps),
        in_specs=[
            pl.BlockSpec((2, CMP_LEN, C_DHEAD), lambda b, p, pt: (0, 0, 0)),
            pl.BlockSpec((2, CMP_LEN, C_DHEAD, C_DHEAD), lambda b, p, pt: (0, 0, 0, 0)),
        ] + [page_spec(g) for g in range(CMP_PAGES)],
        out_specs=pl.BlockSpec((1, CMP_PAGES * per_page, 4 * C_DHEAD), lambda b, p, pt: (b, p, 0)),
    )
    return pl.pallas_call(
        _compress_paged_kernel,
        grid_spec=grid_spec,
        out_shape=jax.ShapeDtypeStruct((bs, n_pages * per_page, 4 * C_DHEAD), F32),
        compiler_params=_cparams(("parallel", "arbitrary")),
        name="nsa_compress_sample",
    )(page_table, pe, w1.reshape(2, CMP_LEN, C_DHEAD, C_DHEAD), *([cache4] * CMP_PAGES))


def _heads_to_rows(q):
    rows = [q[:, h * C_DHEAD:(h + 1) * C_DHEAD] for h in range(C_HEADS)]
    return jnp.concatenate(rows + [jnp.zeros((8 - C_HEADS, C_DHEAD), F32)], axis=0)


SEL_LANES = 384


def _nsa_select_kernel(part_ref, w2_ref, q_ref, ocmp_ref, sel_ref, *, q_pos):
    nchunk = part_ref.shape[1]
    n_cmp = (q_pos + 1 - CMP_LEN) // CMP_STRIDE + 1
    n_sel = -(-(q_pos + 1) // SEL_BLOCK)
    scale = C_DHEAD ** -0.5
    part = part_ref[0]
    kv_cmp = []
    for kv in range(2):
        lo = part[:, (2 * kv) * C_DHEAD:(2 * kv + 1) * C_DHEAD]
        hi = part[:, (2 * kv + 1) * C_DHEAD:(2 * kv + 2) * C_DHEAD]
        hid = lo + pltpu.roll(hi, nchunk - 1, 0)
        kv_cmp.append(jnp.dot(_gelu(hid).astype(BF16), w2_ref[kv].astype(BF16), preferred_element_type=F32).astype(BF16))
    q8 = _heads_to_rows(q_ref[0]).astype(BF16)
    row = lax.broadcasted_iota(jnp.int32, (8, 1), 0)
    slope = jnp.exp2(-2.0 * (row + 1).astype(F32))
    n_i = lax.broadcasted_iota(jnp.int32, (1, nchunk), 1)
    dist_c = q_pos - (n_i * CMP_STRIDE + CMP_LEN - 1)
    ok = (dist_c >= 0) & (n_i < n_cmp)
    s = lax.dot_general(q8, kv_cmp[0], NT_DIMS, preferred_element_type=F32) * scale - slope * dist_c.astype(F32)
    e, den = _masked_softmax(s, ok)
    p = jnp.where(row < C_HEADS, e / jnp.maximum(den, 1e-30), 0.0)
    ocmp_ref[0] = jnp.dot(p.astype(BF16), kv_cmp[1], preferred_element_type=F32)
    psum = jnp.broadcast_to(jnp.sum(p, axis=0, keepdims=True), (8, nchunk))
    c_i = lax.broadcasted_iota(jnp.int32, (nchunk, SEL_LANES), 0)
    j_i = lax.broadcasted_iota(jnp.int32, (nchunk, SEL_LANES), 1)
    lo_ = jnp.maximum(c_i * CMP_STRIDE, j_i * SEL_BLOCK)
    hi_ = jnp.minimum(c_i * CMP_STRIDE + CMP_LEN, (j_i + 1) * SEL_BLOCK)
    ovl = jnp.where((c_i < n_cmp) & (j_i < n_sel), jnp.maximum(hi_ - lo_, 0).astype(F32) * (1.0 / CMP_LEN), 0.0)
    imp = jnp.dot(psum, ovl, preferred_element_type=F32, precision=lax.Precision.HIGHEST)[0:1]
    lane = lax.broadcasted_iota(jnp.int32, (1, SEL_LANES), 1)
    qblk = q_pos // SEL_BLOCK
    forced = (lane == 0) | (lane == qblk) | (lane == qblk - 1)
    score = jnp.where(lane <= qblk, imp + jnp.where(forced, FORCE_SCORE, 0.0), NEG_INF)
    score = jnp.where(lane < n_sel, score, -jnp.inf)
    lane_f = lane.astype(F32)
    out_lane = lax.broadcasted_iota(jnp.int32, (1, LANES), 1)
    sel = jnp.full((1, LANES), -1.0, F32)
    for k in range(min(SEL_TOPK, n_sel)):
        m = jnp.max(score, axis=-1, keepdims=True)
        idx = jnp.min(jnp.where(score == m, lane_f, 1e9), axis=-1, keepdims=True)
        sel = jnp.where(out_lane == k, jnp.where(m > NEG_INF / 2, idx, -1.0), sel)
        score = jnp.where(lane_f == idx, -jnp.inf, score)
    sel_ref[0] = sel.astype(jnp.int32)


def _nsa_select_call(part, w2, q3, q_pos):
    bs, nchunk, _ = part.shape
    return pl.pallas_call(
        functools.partial(_nsa_select_kernel, q_pos=q_pos),
        grid=(bs,),
        in_specs=[
            pl.BlockSpec((1, nchunk, 4 * C_DHEAD), lambda b: (b, 0, 0)),
            pl.BlockSpec((2, C_DHEAD, C_DHEAD), lambda b: (0, 0, 0)),
            pl.BlockSpec((1, 1, C_WIDTH), lambda b: (b, 0, COL_CQ // C_WIDTH)),
        ],
        out_specs=[pl.BlockSpec((1, 8, C_DHEAD), lambda b: (b, 0, 0)), pl.BlockSpec((1, 1, LANES), lambda b: (b, 0, 0))],
        out_shape=[jax.ShapeDtypeStruct((bs, 8, C_DHEAD), F32), jax.ShapeDtypeStruct((bs, 1, LANES), jnp.int32)],
        compiler_params=_cparams(("parallel",)),
        name="nsa_select_sample",
    )(part, w2, q3)


def _nsa_attend_kernel(sel_ref, pt_ref, q_ref, new_ref, wnew_ref, sm_ref, ocmp_ref, win_ref, g_ref, *rest, q_pos):
    k_eff = SEL_TOPK
    blk_refs = rest[:k_eff]
    o_ref, wout_ref = rest[k_eff:]
    b = pl.program_id(0)
    scale = C_DHEAD ** -0.5
    n_past_blocks = q_pos // SEL_BLOCK
    q8f = _heads_to_rows(q_ref[0])
    q8 = q8f.astype(BF16)
    row = lax.broadcasted_iota(jnp.int32, (8, 1), 0)
    slope = jnp.exp2(-2.0 * (row + 1).astype(F32))
    lane64 = lax.broadcasted_iota(jnp.int32, (1, SEL_BLOCK), 1)

    ss, vs, oks = [], [], []
    new_sel = jnp.zeros((1, 1), jnp.int32)
    for k in range(k_eff):
        j = sel_ref[b, k]
        k_sel = blk_refs[k][0, 0, pl.ds(2, SEL_BLOCK, stride=NSA_ROWS), :]
        v_sel = blk_refs[k][0, 0, pl.ds(3, SEL_BLOCK, stride=NSA_ROWS), :]
        s = lax.dot_general(q8, k_sel.astype(BF16), NT_DIMS, preferred_element_type=F32)
        dist = q_pos - (j * SEL_BLOCK + lane64)
        ss.append(s * scale - slope * dist.astype(F32))
        oks.append(lane64 * 0 + jnp.where((j >= 0) & (j < n_past_blocks), 1, 0))
        vs.append(v_sel.astype(BF16))
        new_sel = new_sel + jnp.where(j == n_past_blocks, 1, 0)
    s = jnp.concatenate(ss, axis=1)
    ok = jnp.concatenate(oks, axis=1) > 0
    new = new_ref[0]
    s_n = jnp.sum(q8f * new[:, 2 * C_DHEAD:3 * C_DHEAD], axis=-1, keepdims=True) * scale
    s_n = jnp.where(new_sel > 0, s_n, NEG_INF)
    s = jnp.where(ok, s, NEG_INF)
    m = jnp.maximum(jnp.max(s, axis=-1, keepdims=True), s_n)
    e = jnp.where(ok, jnp.exp(s - m), 0.0)
    e_n = jnp.where(new_sel > 0, jnp.exp(s_n - m), 0.0)
    den = jnp.sum(e, axis=-1, keepdims=True) + e_n
    o_sel = (jnp.dot(e.astype(BF16), jnp.concatenate(vs, axis=0), preferred_element_type=F32)
             + e_n * new[:, 3 * C_DHEAD:]) / den

    lw = win_ref.shape[2] // 2
    win_k = win_ref[0, 0, pl.ds(0, lw, stride=2), :]
    win_v = win_ref[0, 0, pl.ds(1, lw, stride=2), :]
    wnew = wnew_ref[0]
    wpos = lax.broadcasted_iota(jnp.int32, (1, lw), 1)
    dist_w = lw - wpos
    ok_w = dist_w < WINDOW
    s = lax.dot_general(q8, win_k.astype(BF16), NT_DIMS, preferred_element_type=F32) * scale
    s = jnp.where(ok_w, s - slope * dist_w.astype(F32), NEG_INF)
    s_n = jnp.sum(q8f * wnew[:, :C_DHEAD], axis=-1, keepdims=True) * scale
    m = jnp.maximum(jnp.max(s, axis=-1, keepdims=True), s_n)
    e = jnp.where(ok_w, jnp.exp(s - m), 0.0)
    e_n = jnp.exp(s_n - m)
    den = jnp.sum(e, axis=-1, keepdims=True) + e_n
    o_win = (jnp.dot(e.astype(BF16), win_v.astype(BF16), preferred_element_type=F32)
             + e_n * wnew[:, C_DHEAD:]) / den
    keep = min(WINDOW, lw + 1)
    wout_ref[0, 0:2 * (keep - 1), :] = win_ref[0, 0, 2 * (lw + 1 - keep):2 * lw, :]
    wout_ref[0, 2 * keep - 2:2 * keep - 1, :] = wnew[:, :C_DHEAD]
    wout_ref[0, 2 * keep - 1:2 * keep, :] = wnew[:, C_DHEAD:]

    gate = 1.0 / (1.0 + jnp.exp(-sm_ref[0]))
    outs = []
    for h in range(C_HEADS):
        g0 = gate[:, GATE_LANE0 + h:GATE_LANE0 + h + 1]
        g1 = gate[:, GATE_LANE0 + C_HEADS + h:GATE_LANE0 + C_HEADS + h + 1]
        g2 = gate[:, GATE_LANE0 + 2 * C_HEADS + h:GATE_LANE0 + 2 * C_HEADS + h + 1]
        outs.append(g0 * ocmp_ref[0, h:h + 1, :] + g1 * o_sel[h:h + 1, :] + g2 * o_win[h:h + 1, :])
    o = jnp.concatenate(outs, axis=1)
    ms = jnp.mean(o * o, axis=-1, keepdims=True)
    o_ref[0] = o * lax.rsqrt(ms + NORM_EPS) * g_ref[...]


def _nsa_attend_call(sel, page_table, proj3, ocmp, cache4, cache_win4, layer, out_g, q_pos):
    bs = proj3.shape[0]
    lw = cache_win4.shape[2] // 2
    keep = min(WINDOW, lw + 1)
    n_pages = page_table.shape[1]
    halves = PAGE_SIZE // SEL_BLOCK

    def blk_spec(k):
        def imap(b, sel_r, pt_r):
            j = jnp.clip(sel_r[b, k], 0, n_pages * halves - 1)
            return (layer, pt_r[b, j // halves], j % halves, 0)
        return pl.BlockSpec((1, 1, SEL_BLOCK * NSA_ROWS, C_DHEAD), imap)

    row3 = lambda w, col: pl.BlockSpec((1, 1, w), lambda b, s_, p_: (b, 0, col))
    grid_spec = pltpu.PrefetchScalarGridSpec(
        num_scalar_prefetch=2,
        grid=(bs,),
        in_specs=[
            row3(C_WIDTH, COL_CQ // C_WIDTH),
            row3(4 * C_DHEAD, COL_CKV // (4 * C_DHEAD)),
            row3(2 * C_DHEAD, COL_WIN // (2 * C_DHEAD)),
            row3(LANES, COL_SMALL // LANES),
            pl.BlockSpec((1, 8, C_DHEAD), lambda b, s_, p_: (b, 0, 0)),
            pl.BlockSpec((1, 1, 2 * lw, C_DHEAD), lambda b, s_, p_: (layer, b, 0, 0)),
            pl.BlockSpec((1, C_WIDTH), lambda b, s_, p_: (0, 0)),
        ] + [blk_spec(k) for k in range(SEL_TOPK)],
        out_specs=[
            pl.BlockSpec((1, 1, C_WIDTH), lambda b, s_, p_: (b, 0, 0)),
            pl.BlockSpec((1, 2 * keep, C_DHEAD), lambda b, s_, p_: (b, 0, 0)),
        ],
    )
    return pl.pallas_call(
        functools.partial(_nsa_attend_kernel, q_pos=q_pos),
        grid_spec=grid_spec,
        out_shape=[jax.ShapeDtypeStruct((bs, 1, C_WIDTH), F32), jax.ShapeDtypeStruct((bs, 2 * keep, C_DHEAD), F32)],
        compiler_params=_cparams(("arbitrary",)),
        name="nsa_attend_sample",
    )(sel, page_table, proj3, proj3, proj3, proj3, ocmp, cache_win4, out_g.reshape(1, C_WIDTH),
      *([cache4] * SEL_TOPK))


def _reorder_w_in(w):
    return jnp.concatenate([
        w[:, 2560:4608], w[:, 1536:2560], w[:, 0:1536], w[:, 4624:5136], w[:, 5136:5904],
        w[:, 4608:4624], w[:, 5904:5916], jnp.zeros((D_MODEL, PROJ_W - 5916), w.dtype)], axis=1).astype(BF16)


def _token_mixer_tail(xa, xb, oa, ob, oc, w_out_b, ffn_g, wq_t, subkeys, u_tab, v_tab, tm, tn, et, emit_bf16=False):
    h, hn_t = _outproj_call(xa, xb, oa, ob, oc, w_out_b, ffn_g, tm)
    q_t = _peer_q_call(wq_t, hn_t, tm)
    route = _peer_route_call(q_t, subkeys)
    return h, _peer_expert_call(hn_t, route, u_tab, v_tab, tn, et, emit_bf16)


def kernel(x_prompt, x_sample, cache_diff_kv, cache_nsa_kv, cache_nsa_win, state_ssm, state_conv, page_table,
           norm_mix_g, w_in, w_out, diff_lam, diff_subln_g, ssm_conv_w, ssm_conv_b, ssm_dt_bias, ssm_a_log,
           ssm_d, ssm_norm_g, nsa_pe, nsa_cmp_w1, nsa_cmp_w2, nsa_out_g, norm_ffn_g, peer_wq, peer_subkeys,
           peer_u, peer_v, norm_final_g):
    depth = w_in.shape[0]
    bp, t, _ = x_prompt.shape
    bs = x_sample.shape[0]
    past_len = page_table.shape[1] * PAGE_SIZE
    n_p = bp * t
    n_s = LANES
    tm_p = 512

    xa_p, xb_p = x_prompt.reshape(n_p, D_MODEL), None
    xa_s = jnp.pad(x_sample.reshape(bs, D_MODEL), ((0, n_s - bs), (0, 0)))
    xb_s = None
    st_p = [[] for _ in range(5)]
    st_s = [[] for _ in range(5)]
    n_pool = cache_diff_kv.shape[1]
    cache_diff4 = cache_diff_kv.reshape(depth, n_pool, PAGE_SIZE * 2 * A_HEADS, A_DHEAD)
    cache_nsa4 = cache_nsa_kv.reshape(depth, n_pool, PAGE_SIZE * NSA_ROWS, C_DHEAD)
    cache_win4 = cache_nsa_win.reshape(depth, bs, cache_nsa_win.shape[2] * 2, C_DHEAD)
    state_ssm4 = state_ssm.reshape(depth, bs, B_HEADS * B_HEADDIM, B_STATE)
    for l in range(depth):
        lam_init = 0.8 - 0.6 * math.exp(-0.3 * l)
        w_r = _reorder_w_in(w_in[l])
        w_out_b = w_out[l].astype(BF16)
        wq_t = peer_wq[l].T.astype(BF16)

        proj_s = _proj_call(xa_s, xb_s, norm_mix_g[l], w_r, n_s)[:bs]
        proj_s3 = proj_s.reshape(bs, 1, PROJ_W)
        o_a = _diff_sample_call(page_table, proj_s3, cache_diff4, l, diff_lam[l], diff_subln_g[l], lam_init)
        o_b, h_new, conv_new = _ssd_sample_call(proj_s3, state_conv, state_ssm4, l, ssm_conv_w[l], ssm_conv_b[l],
                                                ssm_dt_bias[l], ssm_a_log[l], ssm_d[l], ssm_norm_g[l])
        part = _compress_paged_call(page_table, cache_nsa4, l, nsa_pe[l], nsa_cmp_w1[l])
        o_cmp, sel = _nsa_select_call(part, nsa_cmp_w2[l], proj_s3, past_len)
        o_c, win_out = _nsa_attend_call(sel[:, 0, :SEL_TOPK], page_table, proj_s3, o_cmp, cache_nsa4, cache_win4, l,
                                        nsa_out_g[l], past_len)
        st_s[0].append(proj_s[:, COL_AK:COL_AK + 2 * A_WIDTH].reshape(bs, 1, 2, A_HEADS, A_DHEAD))
        st_s[1].append(proj_s[:, COL_CKV:COL_CKV + 4 * C_DHEAD].reshape(bs, 1, 4, C_DHEAD))
        st_s[2].append(win_out.reshape(bs, -1, 2, C_DHEAD))
        st_s[3].append(h_new.reshape(bs, B_HEADS, B_HEADDIM, B_STATE))
        st_s[4].append(conv_new)
        pad = lambda a: jnp.pad(a.reshape(bs, -1), ((0, n_s - bs), (0, 0)))
        xa_s, (xb_s, u_b, vt_b) = _token_mixer_tail(xa_s, xb_s, pad(o_a), pad(o_b), pad(o_c), w_out_b, norm_ffn_g[l],
                                                   wq_t, peer_subkeys[l], peer_u[l], peer_v[l], n_s, n_s, 512, True)

        proj = _proj_call(xa_p, xb_p, norm_mix_g[l], w_r, tm_p)
        o_a = _diff_prompt_call(proj, bp, t, diff_lam[l], diff_subln_g[l], lam_init)
        o_b, h_ssm = _ssd_prompt_call(proj, bp, t, ssm_conv_w[l], ssm_conv_b[l], ssm_dt_bias[l], ssm_a_log[l],
                                      ssm_d[l], ssm_norm_g[l])
        kc, vc = _compress_prompt_call(proj, bp, t, nsa_pe[l], nsa_cmp_w1[l], nsa_cmp_w2[l])
        o_c = _nsa_prompt_call(proj, kc, vc, bp, t, nsa_out_g[l])
        proj3 = proj.reshape(bp, t, PROJ_W)
        st_p[0].append(proj3[:, :, COL_AK:COL_AK + 2 * A_WIDTH].reshape(bp, t, 2, A_HEADS, A_DHEAD))
        st_p[1].append(proj3[:, :, COL_CKV:COL_CKV + 4 * C_DHEAD].reshape(bp, t, 4, C_DHEAD))
        keep = min(WINDOW, t)
        st_p[2].append(proj3[:, t - keep:, COL_WIN:COL_WIN + 2 * C_DHEAD].reshape(bp, keep, 2, C_DHEAD))
        st_p[3].append(h_ssm)
        st_p[4].append(proj3[:, t - (CONV_W - 1):, COL_XBC:COL_XBC + B_CONV_DIM])
        xa_p, xb_p = _token_mixer_tail(xa_p, xb_p, o_a, o_b, o_c, w_out_b, norm_ffn_g[l], wq_t, peer_subkeys[l],
                                       u_b, vt_b, tm_p, 512, 1024)

    y_p = _final_norm_call(xa_p, xb_p, norm_final_g, tm_p).reshape(bp, t, D_MODEL)
    y_s = _final_norm_call(xa_s, xb_s, norm_final_g, n_s)[:bs].reshape(bs, 1, D_MODEL)
    return (y_p, y_s) + tuple(jnp.stack(s) for s in st_p) + tuple(jnp.stack(s) for s in st_s)
```

```python
import functools
import math

import jax
import jax.numpy as jnp
from jax import lax
from jax.experimental import pallas as pl
from jax.experimental.pallas import tpu as pltpu

F32 = jnp.float32
BF16 = jnp.bfloat16

D_MODEL = 2048
A_HEADS = 4
A_HALF = 64
A_DHEAD = 128
A_WIDTH = 512
B_WIDTH = 1024
B_HEADDIM = 64
B_HEADS = 16
B_GROUPS = 4
B_STATE = 128
CONV_W = 4
B_CONV_DIM = 2048
SSD_CHUNK = 128
C_HEADS = 4
C_DHEAD = 128
C_WIDTH = 512
CMP_LEN = 32
CMP_STRIDE = 16
SEL_BLOCK = 64
SEL_TOPK = 16
WINDOW = 512
PEER_HEADS = 8
PEER_NKEYS = 128
PEER_TOPK = 16
PEER_DQ = 256
PAGE_SIZE = 128
NSA_ROWS = 4
NORM_EPS = 1e-6
NEG_INF = -1e30
FORCE_SCORE = 1e4

LANES = 128
VMEM_LIMIT = 56 * 1024 * 1024

COL_XBC = 0
COL_Z = 2048
COL_AQ = 3072
COL_AK = 3584
COL_AV = 4096
COL_CQ = 4608
COL_CKV = 5120
COL_WIN = 5632
COL_SMALL = 5888
PROJ_W = 6144
GATE_LANE0 = B_HEADS

NT_DIMS = (((1,), (1,)), ((), ()))


def _cparams(sem, vmem=VMEM_LIMIT):
    return pltpu.CompilerParams(dimension_semantics=sem, vmem_limit_bytes=vmem)


def _gelu(x):
    return 0.5 * x * (1.0 + jnp.tanh(math.sqrt(2.0 / math.pi) * (x + 0.044715 * (x * x * x))))


def _silu(x):
    return x * (1.0 / (1.0 + jnp.exp(-x)))


def _softplus(x):
    return jnp.maximum(x, 0.0) + jnp.log(1.0 + jnp.exp(-jnp.abs(x)))


def _row_view(ref, rows):
    return ref.reshape(rows, ref.shape[-1])


def _alibi_slope(h):
    if isinstance(h, int):
        return 2.0 ** (-2.0 * (h + 1))
    return jnp.exp2(jnp.full((1, 1), -2.0, F32) * (h + 1).astype(F32))


def _proj_kernel(*refs, two, tc):
    if two:
        xa_ref, xb_ref, g_ref, w_ref, o_ref, dkv_ref, nkv_ref, xn_ref = refs
    else:
        xa_ref, g_ref, w_ref, o_ref, dkv_ref, nkv_ref, xn_ref = refs
    j = pl.program_id(1)

    @pl.when(pl.program_id(1) == 0)
    def _():
        x = xa_ref[...]
        if two:
            x = x + xb_ref[...]
        ms = jnp.mean(x * x, axis=-1, keepdims=True)
        xn_ref[...] = (x * lax.rsqrt(ms + NORM_EPS) * g_ref[...]).astype(BF16)

    res = jnp.dot(xn_ref[...], w_ref[...], preferred_element_type=F32)
    o_ref[...] = res

    @pl.when(j == COL_AK // tc)
    def _():
        for b in range(2 * A_HEADS):
            c = COL_AK % tc + b * LANES
            dkv_ref[:, b // A_HEADS, b % A_HEADS, :] = res[:, c:c + LANES]

    @pl.when(j == COL_CKV // tc)
    def _():
        for b in range(NSA_ROWS):
            c = COL_CKV % tc + b * LANES
            nkv_ref[:, b, :] = res[:, c:c + LANES]


def _proj_call(xa, xb, g, w_r, tm, tc=1536):
    assert COL_AK // tc == (COL_AK + 2 * A_WIDTH - 1) // tc and COL_CKV // tc == (COL_CKV + NSA_ROWS * C_DHEAD - 1) // tc
    n = xa.shape[0]
    two = xb is not None
    xspec = pl.BlockSpec((tm, D_MODEL), lambda i, j: (i, 0))
    ins = [xa] + ([xb] if two else []) + [g.reshape(1, D_MODEL), w_r]
    specs = [xspec] + ([xspec] if two else []) + [
        pl.BlockSpec((1, D_MODEL), lambda i, j: (0, 0)),
        pl.BlockSpec((D_MODEL, tc), lambda i, j: (0, j)),
    ]
    return pl.pallas_call(
        functools.partial(_proj_kernel, two=two, tc=tc),
        grid=(n // tm, PROJ_W // tc),
        in_specs=specs,
        out_specs=[pl.BlockSpec((tm, tc), lambda i, j: (i, j)),
                   pl.BlockSpec((tm, 2, A_HEADS, A_DHEAD), lambda i, j: (i, 0, 0, 0)),
                   pl.BlockSpec((tm, NSA_ROWS, C_DHEAD), lambda i, j: (i, 0, 0))],
        out_shape=[jax.ShapeDtypeStruct((n, PROJ_W), F32), jax.ShapeDtypeStruct((n, 2, A_HEADS, A_DHEAD), F32),
                   jax.ShapeDtypeStruct((n, NSA_ROWS, C_DHEAD), F32)],
        scratch_shapes=[pltpu.VMEM((tm, D_MODEL), BF16)],
        compiler_params=_cparams(("parallel", "arbitrary")),
        name="in_proj",
    )(*ins)


def _diff_lambda(dl, lam_init):
    a = jnp.sum(dl[0:1] * dl[1:2], axis=-1, keepdims=True)
    b = jnp.sum(dl[2:3] * dl[3:4], axis=-1, keepdims=True)
    return jnp.exp(a) - jnp.exp(b) + lam_init


CAUSAL_LEVELS = 4


def _causal_prefixes(i, nq, tq, body):
    levels = min(CAUSAL_LEVELS, nq)
    per = nq // levels
    for lv in range(levels):
        pl.when(i // per == lv)(functools.partial(body, (lv + 1) * per * tq))


def _diff_prompt_kernel(q_ref, k_ref, v_ref, dl_ref, g_ref, o_ref, *, tq, lam_init):
    h = pl.program_id(1)
    i = pl.program_id(2)
    t = k_ref.shape[0]
    scale = A_HALF ** -0.5

    def body(nk):
        lam = _diff_lambda(dl_ref[...], lam_init)
        q = q_ref[...]
        lane = lax.broadcasted_iota(jnp.int32, (1, A_DHEAD), 1)
        kb = k_ref[0:nk, :].astype(BF16)
        vb = v_ref[0:nk, :].astype(BF16)
        qpos = i * tq + lax.broadcasted_iota(jnp.int32, (tq, 1), 0)
        kpos = lax.broadcasted_iota(jnp.int32, (1, nk), 1)
        dist = qpos - kpos
        ok = dist >= 0
        bias = _alibi_slope(h) * dist.astype(F32)

        def half_softmax(c):
            qc = jnp.where((lane >= c * A_HALF) & (lane < (c + 1) * A_HALF), q, 0.0).astype(BF16)
            s = lax.dot_general(qc, kb, NT_DIMS, preferred_element_type=F32) * scale
            s = jnp.where(ok, s - bias, NEG_INF)
            m = jnp.max(s, axis=-1, keepdims=True)
            e = jnp.exp(s - m)
            return e / jnp.sum(e, axis=-1, keepdims=True)

        pd = half_softmax(0) - lam * half_softmax(1)
        o = jnp.dot(pd.astype(BF16), vb, preferred_element_type=F32)
        ms = jnp.mean(o * o, axis=-1, keepdims=True)
        o_ref[...] = o * lax.rsqrt(ms + NORM_EPS) * g_ref[...] * (1.0 - lam_init)

    _causal_prefixes(i, t // tq, tq, body)


def _diff_prompt_call(proj, bsz, t, dl, subln_g, lam_init, tq=256):
    nq = t // tq
    cq, ck, cv = COL_AQ // A_DHEAD, COL_AK // A_DHEAD, COL_AV // A_DHEAD
    return pl.pallas_call(
        functools.partial(_diff_prompt_kernel, tq=tq, lam_init=lam_init),
        grid=(bsz, A_HEADS, nq),
        in_specs=[
            pl.BlockSpec((tq, A_DHEAD), lambda b, h, i: (b * nq + i, cq + h)),
            pl.BlockSpec((t, A_DHEAD), lambda b, h, i: (b, ck + h)),
            pl.BlockSpec((t, A_DHEAD), lambda b, h, i: (b, cv + h)),
            pl.BlockSpec((4, A_HALF), lambda b, h, i: (0, 0)),
            pl.BlockSpec((1, A_DHEAD), lambda b, h, i: (0, 0)),
        ],
        out_specs=pl.BlockSpec((tq, A_DHEAD), lambda b, h, i: (b * nq + i, h)),
        out_shape=jax.ShapeDtypeStruct((bsz * t, A_WIDTH), F32),
        compiler_params=_cparams(("parallel", "parallel", "arbitrary")),
        name="diff_attn_prompt",
    )(proj, proj, proj, dl, subln_g.reshape(1, A_DHEAD))


def _ssd_prompt_kernel(xbc_ref, z_ref, sm_ref, cw_ref, cb_ref, dtb_ref, alog_ref, dsk_ref, g_ref,
                       o_ref, hout_ref, buf_ref, h_ref):
    c = pl.program_id(1)
    cs = SSD_CHUNK

    @pl.when(c == 0)
    def _():
        buf_ref[0:8, :] = jnp.zeros((8, B_CONV_DIM), F32)
        h_ref[...] = jnp.zeros_like(h_ref)

    xbc = xbc_ref[...]
    buf_ref[8:8 + cs, :] = xbc
    cw = cw_ref[...]
    conv = cb_ref[...] + cw[3:4] * xbc
    for j in range(1, CONV_W):
        conv = conv + cw[3 - j:4 - j] * buf_ref[8 - j:8 - j + cs, :]
    buf_ref[0:8, :] = xbc[cs - 8:cs, :]
    xc = _silu(conv)
    xs = xc[:, :B_WIDTH]

    dt = _softplus(sm_ref[...] + dtb_ref[...])
    a_neg = -jnp.exp(alog_ref[...])
    dta = dt * a_neg
    row = lax.broadcasted_iota(jnp.int32, (cs, cs), 0)
    col = lax.broadcasted_iota(jnp.int32, (cs, cs), 1)
    causal = row >= col
    acum = jnp.dot(causal.astype(F32), dta, preferred_element_type=F32, precision=lax.Precision.HIGHEST)
    acum_t = acum.T
    dt_t = dt.T
    lane = lax.broadcasted_iota(jnp.int32, (1, LANES), 1)
    lo = lane < B_HEADDIM

    ys = []
    for g in range(B_GROUPS):
        bg = xc[:, B_WIDTH + g * B_STATE:B_WIDTH + (g + 1) * B_STATE]
        cg = xc[:, B_WIDTH + B_GROUPS * B_STATE + g * B_STATE:B_WIDTH + B_GROUPS * B_STATE + (g + 1) * B_STATE]
        bgb = bg.astype(BF16)
        cgb = cg.astype(BF16)
        cb = lax.dot_general(cgb, bgb, NT_DIMS, preferred_element_type=F32)
        for pr in range(2):
            h0 = g * 4 + pr * 2
            xpair = xs[:, h0 * B_HEADDIM:(h0 + 2) * B_HEADDIM]
            xpb = xpair.astype(BF16)
            ydiag = []
            ecol = []
            wcol = []
            elast = []
            for hh in (h0, h0 + 1):
                a_col = acum[:, hh:hh + 1]
                a_row = acum_t[hh:hh + 1, :]
                decay = jnp.exp(jnp.where(causal, a_col - a_row, NEG_INF))
                lm = cb * decay * dt_t[hh:hh + 1, :]
                ydiag.append(jnp.dot(lm.astype(BF16), xpb, preferred_element_type=F32))
                a_last = acum[cs - 1:cs, hh:hh + 1]
                ecol.append(jnp.exp(a_col))
                wcol.append(jnp.exp(a_last - a_col) * dt[:, hh:hh + 1])
                elast.append(jnp.exp(a_last))
            hp = h_ref[h0 * B_HEADDIM:(h0 + 2) * B_HEADDIM, :]
            yoff = lax.dot_general(cgb, hp.astype(BF16), NT_DIMS, preferred_element_type=F32)
            y = jnp.where(lo, ydiag[0], ydiag[1]) + yoff * jnp.where(lo, ecol[0], ecol[1])
            ys.append(y)
            wx = xpair * jnp.where(lo, wcol[0], wcol[1])
            upd = jnp.dot(wx.T.astype(BF16), bgb, preferred_element_type=F32)
            prow = lax.broadcasted_iota(jnp.int32, (LANES, 1), 0) < B_HEADDIM
            h_ref[h0 * B_HEADDIM:(h0 + 2) * B_HEADDIM, :] = jnp.where(prow, elast[0], elast[1]) * hp + upd

    y = jnp.concatenate(ys, axis=1)
    y = y + dsk_ref[...] * xs
    y = y * _silu(z_ref[...])
    ms = jnp.mean(y * y, axis=-1, keepdims=True)
    o_ref[...] = y * lax.rsqrt(ms + NORM_EPS) * g_ref[...]

    @pl.when(c == pl.num_programs(1) - 1)
    def _():
        hout_ref[0] = h_ref[...]


def _pad_lanes(v, fill=0.0):
    v = v.reshape(1, -1).astype(F32)
    return jnp.pad(v, ((0, 0), (0, LANES - v.shape[1])), constant_values=fill)


def _ssd_prompt_call(proj, bsz, t, conv_w, conv_b, dt_bias, a_log, d_skip, norm_g):
    nc = t // SSD_CHUNK
    cs = SSD_CHUNK
    const = lambda b, c: (0, 0)
    o, hout = pl.pallas_call(
        _ssd_prompt_kernel,
        grid=(bsz, nc),
        in_specs=[
            pl.BlockSpec((cs, B_CONV_DIM), lambda b, c: (b * nc + c, COL_XBC // B_CONV_DIM)),
            pl.BlockSpec((cs, B_WIDTH), lambda b, c: (b * nc + c, COL_Z // B_WIDTH)),
            pl.BlockSpec((cs, LANES), lambda b, c: (b * nc + c, COL_SMALL // LANES)),
            pl.BlockSpec((CONV_W, B_CONV_DIM), const),
            pl.BlockSpec((1, B_CONV_DIM), const),
            pl.BlockSpec((1, LANES), const),
            pl.BlockSpec((1, LANES), const),
            pl.BlockSpec((1, B_WIDTH), const),
            pl.BlockSpec((1, B_WIDTH), const),
        ],
        out_specs=[
            pl.BlockSpec((cs, B_WIDTH), lambda b, c: (b * nc + c, 0)),
            pl.BlockSpec((1, B_HEADS * B_HEADDIM, B_STATE), lambda b, c: (b, 0, 0)),
        ],
        out_shape=[
            jax.ShapeDtypeStruct((bsz * t, B_WIDTH), F32),
            jax.ShapeDtypeStruct((bsz, B_HEADS * B_HEADDIM, B_STATE), F32),
        ],
        scratch_shapes=[pltpu.VMEM((8 + cs, B_CONV_DIM), F32), pltpu.VMEM((B_HEADS * B_HEADDIM, B_STATE), F32)],
        compiler_params=_cparams(("parallel", "arbitrary")),
        name="ssd_prompt",
    )(proj, proj, proj, conv_w, conv_b.reshape(1, -1), _pad_lanes(dt_bias), _pad_lanes(a_log),
      jnp.repeat(d_skip, B_HEADDIM).reshape(1, B_WIDTH), norm_g.reshape(1, B_WIDTH))
    return o, hout.reshape(bsz, B_HEADS, B_HEADDIM, B_STATE)


def _compress_kernel(k_ref, v_ref, pe_ref, w1_ref, w2_ref, kc_ref, vc_ref, *, nchunk):
    outs = []
    for kv, rows_ref in enumerate((k_ref, v_ref)):
        acc_lo = jnp.zeros((nchunk, C_DHEAD), F32)
        acc_hi = jnp.zeros((nchunk, C_DHEAD), F32)
        for r in range(CMP_STRIDE):
            x = rows_ref[pl.ds(r, nchunk, stride=CMP_STRIDE), :]
            x_lo = (x + pe_ref[kv, r:r + 1, :]).astype(BF16)
            x_hi = (x + pe_ref[kv, CMP_STRIDE + r:CMP_STRIDE + r + 1, :]).astype(BF16)
            acc_lo = acc_lo + jnp.dot(x_lo, w1_ref[kv, r].astype(BF16), preferred_element_type=F32)
            acc_hi = acc_hi + jnp.dot(x_hi, w1_ref[kv, CMP_STRIDE + r].astype(BF16), preferred_element_type=F32)
        hid = acc_lo + pltpu.roll(acc_hi, nchunk - 1, 0)
        outs.append(jnp.dot(_gelu(hid).astype(BF16), w2_ref[kv].astype(BF16), preferred_element_type=F32))
    kc_ref[0] = outs[0]
    vc_ref[0] = outs[1]


def _compress_prompt_call(proj, bsz, t, pe, w1, w2):
    nchunk = t // CMP_STRIDE
    shp = jax.ShapeDtypeStruct((bsz, nchunk, C_DHEAD), F32)
    return pl.pallas_call(
        functools.partial(_compress_kernel, nchunk=nchunk),
        grid=(bsz,),
        in_specs=[
            pl.BlockSpec((t, C_DHEAD), lambda b: (b, COL_CKV // C_DHEAD)),
            pl.BlockSpec((t, C_DHEAD), lambda b: (b, COL_CKV // C_DHEAD + 1)),
            pl.BlockSpec((2, CMP_LEN, C_DHEAD), lambda b: (0, 0, 0)),
            pl.BlockSpec((2, CMP_LEN, C_DHEAD, C_DHEAD), lambda b: (0, 0, 0, 0)),
            pl.BlockSpec((2, C_DHEAD, C_DHEAD), lambda b: (0, 0, 0)),
        ],
        out_specs=[pl.BlockSpec((1, nchunk, C_DHEAD), lambda b: (b, 0, 0))] * 2,
        out_shape=[shp, shp],
        compiler_params=_cparams(("parallel",)),
        name="nsa_compress_prompt",
    )(proj, proj, pe, w1.reshape(2, CMP_LEN, C_DHEAD, C_DHEAD), w2)


def _masked_softmax(s, ok):
    s = jnp.where(ok, s, NEG_INF)
    m = jnp.max(s, axis=-1, keepdims=True)
    e = jnp.where(ok, jnp.exp(s - m), 0.0)
    return e, jnp.sum(e, axis=-1, keepdims=True)


def _topk_mask_lanes(score, k, n):
    lane = lax.broadcasted_iota(jnp.int32, (1, LANES), 1)
    rank = jnp.zeros(score.shape, F32)
    for i in range(n):
        ci = score[:, i:i + 1]
        beats = (ci > score) | ((ci == score) & (lane > i))
        rank = rank + jnp.where(beats, 1.0, 0.0)
    return (rank < k) & (lane < n)


def _nsa_prompt_kernel(q_ref, ks_ref, vs_ref, kw_ref, vw_ref, kc_ref, vc_ref, sm_ref, g_ref, o_ref, *, tq, n_cmp):
    i = pl.program_id(1)
    t = ks_ref.shape[0]
    n_sel = t // SEL_BLOCK
    scale = C_DHEAD ** -0.5
    sel_shift = SEL_BLOCK.bit_length() - 1
    wlen = min(t, WINDOW + tq)

    def body(nk):
        qpos = i * tq + lax.broadcasted_iota(jnp.int32, (tq, 1), 0)
        lane = lax.broadcasted_iota(jnp.int32, (1, LANES), 1)

        cmp_end = lane * CMP_STRIDE + (CMP_LEN - 1)
        dist_c = qpos - cmp_end
        ok_c = (dist_c >= 0) & (lane < n_cmp)
        dist_cf = dist_c.astype(F32)
        kcb = kc_ref[0].astype(BF16)
        vcb = vc_ref[0].astype(BF16)
        qs = [q_ref[:, h * C_DHEAD:(h + 1) * C_DHEAD].astype(BF16) for h in range(C_HEADS)]
        o_cmp = []
        psum = jnp.zeros((tq, LANES), F32)
        for h in range(C_HEADS):
            s = lax.dot_general(qs[h], kcb, NT_DIMS, preferred_element_type=F32) * scale
            s = s - _alibi_slope(h) * dist_cf
            e, den = _masked_softmax(s, ok_c)
            p = e / jnp.maximum(den, 1e-30)
            psum = psum + p
            o_cmp.append(jnp.dot(p.astype(BF16), vcb, preferred_element_type=F32))

        n_i = lax.broadcasted_iota(jnp.int32, (LANES, LANES), 0)
        j_i = lax.broadcasted_iota(jnp.int32, (LANES, LANES), 1)
        lo_ = jnp.maximum(n_i * CMP_STRIDE, j_i * SEL_BLOCK)
        hi_ = jnp.minimum(n_i * CMP_STRIDE + CMP_LEN, (j_i + 1) * SEL_BLOCK)
        ovl = jnp.maximum(hi_ - lo_, 0).astype(F32) * (1.0 / CMP_LEN)
        ovl = jnp.where((n_i < n_cmp) & (j_i < n_sel), ovl, 0.0)
        imp = jnp.dot(psum, ovl, preferred_element_type=F32, precision=lax.Precision.HIGHEST)
        qblk = qpos >> sel_shift
        sel_valid = lane <= qblk
        forced = (lane == 0) | (lane == qblk) | (lane == qblk - 1)
        score = jnp.where(sel_valid, imp + jnp.where(forced, FORCE_SCORE, 0.0), NEG_INF)
        score = jnp.where(lane < n_sel, score, -jnp.inf)
        chosen = _topk_mask_lanes(score, min(SEL_TOPK, n_sel), n_sel) & sel_valid
        e_j = lax.broadcasted_iota(jnp.int32, (LANES, nk), 0)
        e_k = lax.broadcasted_iota(jnp.int32, (LANES, nk), 1)
        expand = jnp.where((e_k >> sel_shift) == e_j, 1.0, 0.0).astype(BF16)
        key_sel = jnp.dot(jnp.where(chosen, 1.0, 0.0).astype(BF16), expand, preferred_element_type=F32) > 0.5
        dist = qpos - lax.broadcasted_iota(jnp.int32, (1, nk), 1)
        dist_f = dist.astype(F32)
        ok_s = key_sel & (dist >= 0)
        w0 = pl.multiple_of(jnp.clip(i * tq - WINDOW, 0, t - wlen), 8)
        dist_w = qpos - (w0 + lax.broadcasted_iota(jnp.int32, (1, wlen), 1))
        dist_wf = dist_w.astype(F32)
        ok_w = (dist_w >= 0) & (dist_w < WINDOW)

        ksb = ks_ref[0:nk, :].astype(BF16)
        vsb = vs_ref[0:nk, :].astype(BF16)
        kwb = kw_ref[pl.ds(w0, wlen), :].astype(BF16)
        vwb = vw_ref[pl.ds(w0, wlen), :].astype(BF16)
        gate = 1.0 / (1.0 + jnp.exp(-sm_ref[...]))
        outs = []
        for h in range(C_HEADS):
            s = lax.dot_general(qs[h], ksb, NT_DIMS, preferred_element_type=F32) * scale - _alibi_slope(h) * dist_f
            e, den = _masked_softmax(s, ok_s)
            o_sel = jnp.dot((e / den).astype(BF16), vsb, preferred_element_type=F32)
            s = lax.dot_general(qs[h], kwb, NT_DIMS, preferred_element_type=F32) * scale - _alibi_slope(h) * dist_wf
            e, den = _masked_softmax(s, ok_w)
            o_win = jnp.dot((e / den).astype(BF16), vwb, preferred_element_type=F32)
            g0 = gate[:, GATE_LANE0 + h:GATE_LANE0 + h + 1]
            g1 = gate[:, GATE_LANE0 + C_HEADS + h:GATE_LANE0 + C_HEADS + h + 1]
            g2 = gate[:, GATE_LANE0 + 2 * C_HEADS + h:GATE_LANE0 + 2 * C_HEADS + h + 1]
            outs.append(g0 * o_cmp[h] + g1 * o_sel + g2 * o_win)
        o = jnp.concatenate(outs, axis=1)
        ms = jnp.mean(o * o, axis=-1, keepdims=True)
        o_ref[...] = o * lax.rsqrt(ms + NORM_EPS) * g_ref[...]

    body(t)


def _nsa_prompt_call(proj, kc, vc, bsz, t, out_g, tq=256):
    nq = t // tq
    n_cmp = (t - CMP_LEN) // CMP_STRIDE + 1
    c0 = COL_CKV // C_DHEAD
    w0 = COL_WIN // C_DHEAD
    kvspec = lambda col: pl.BlockSpec((t, C_DHEAD), lambda b, i: (b, col))
    return pl.pallas_call(
        functools.partial(_nsa_prompt_kernel, tq=tq, n_cmp=n_cmp),
        grid=(bsz, nq),
        in_specs=[
            pl.BlockSpec((tq, C_WIDTH), lambda b, i: (b * nq + i, COL_CQ // C_WIDTH)),
            kvspec(c0 + 2), kvspec(c0 + 3), kvspec(w0), kvspec(w0 + 1),
            pl.BlockSpec((1, kc.shape[1], C_DHEAD), lambda b, i: (b, 0, 0)),
            pl.BlockSpec((1, kc.shape[1], C_DHEAD), lambda b, i: (b, 0, 0)),
            pl.BlockSpec((tq, LANES), lambda b, i: (b * nq + i, COL_SMALL // LANES)),
            pl.BlockSpec((1, C_WIDTH), lambda b, i: (0, 0)),
        ],
        out_specs=pl.BlockSpec((tq, C_WIDTH), lambda b, i: (b * nq + i, 0)),
        out_shape=jax.ShapeDtypeStruct((bsz * t, C_WIDTH), F32),
        compiler_params=_cparams(("parallel", "arbitrary")),
        name="nsa_attn_prompt",
    )(proj, proj, proj, proj, proj, kc, vc, proj, out_g.reshape(1, C_WIDTH))


def _outproj_kernel(*refs, two):
    if two:
        xa_ref, xb_ref, oa_ref, ob_ref, oc_ref, w_ref, g_ref, h_ref, hnt_ref = refs
    else:
        xa_ref, oa_ref, ob_ref, oc_ref, w_ref, g_ref, h_ref, hnt_ref = refs
    x = xa_ref[...]
    if two:
        x = x + xb_ref[...]
    mixed = jnp.dot(oa_ref[...].astype(BF16), w_ref[0:A_WIDTH, :], preferred_element_type=F32)
    mixed = mixed + jnp.dot(ob_ref[...].astype(BF16), w_ref[A_WIDTH:A_WIDTH + B_WIDTH, :], preferred_element_type=F32)
    mixed = mixed + jnp.dot(oc_ref[...].astype(BF16), w_ref[A_WIDTH + B_WIDTH:, :], preferred_element_type=F32)
    h = x + mixed
    h_ref[...] = h
    ms = jnp.mean(h * h, axis=-1, keepdims=True)
    hnt_ref[...] = (h * lax.rsqrt(ms + NORM_EPS) * g_ref[...]).T.astype(BF16)


def _outproj_call(xa, xb, oa, ob, oc, w_out_b, g, tm):
    n = xa.shape[0]
    two = xb is not None
    row = lambda w: pl.BlockSpec((tm, w), lambda i: (i, 0))
    ins = [xa] + ([xb] if two else []) + [oa, ob, oc, w_out_b, g.reshape(1, D_MODEL)]
    specs = [row(D_MODEL)] + ([row(D_MODEL)] if two else []) + [
        row(A_WIDTH), row(B_WIDTH), row(C_WIDTH),
        pl.BlockSpec((D_MODEL, D_MODEL), lambda i: (0, 0)),
        pl.BlockSpec((1, D_MODEL), lambda i: (0, 0)),
    ]
    return pl.pallas_call(
        functools.partial(_outproj_kernel, two=two),
        grid=(n // tm,),
        in_specs=specs,
        out_specs=[row(D_MODEL), pl.BlockSpec((D_MODEL, tm), lambda i: (0, i))],
        out_shape=[jax.ShapeDtypeStruct((n, D_MODEL), F32), jax.ShapeDtypeStruct((D_MODEL, n), BF16)],
        compiler_params=_cparams(("parallel",)),
        name="out_proj",
    )(*ins)


def _peer_q_kernel(wqt_ref, hnt_ref, qt_ref):
    qt_ref[...] = jnp.dot(wqt_ref[...], hnt_ref[...], preferred_element_type=F32)


def _peer_q_call(wq_t, hn_t, tm):
    n = hn_t.shape[1]
    dq = wq_t.shape[0]
    return pl.pallas_call(
        _peer_q_kernel,
        grid=(n // tm,),
        in_specs=[pl.BlockSpec((dq, D_MODEL), lambda i: (0, 0)), pl.BlockSpec((D_MODEL, tm), lambda i: (0, i))],
        out_specs=pl.BlockSpec((dq, tm), lambda i: (0, i)),
        out_shape=jax.ShapeDtypeStruct((dq, n), F32),
        compiler_params=_cparams(("parallel",)),
        name="peer_query",
    )(wq_t, hn_t)


NOT_RANKED = 99.0


def _top_rows(s, pos, k, want_rank):
    rank = jnp.full(s.shape, NOT_RANKED, F32) if want_rank else None
    vals, picks = [], []
    for j in range(k):
        m = jnp.max(s, axis=0, keepdims=True)
        idx = jnp.min(jnp.where(s == m, pos, 1e9), axis=0, keepdims=True)
        hit = pos == idx
        if want_rank:
            rank = jnp.where(hit, float(j), rank)
        s = jnp.where(hit, -jnp.inf, s)
        vals.append(m)
        picks.append(idx)
    return jnp.concatenate(vals, axis=0), jnp.concatenate(picks, axis=0), rank


PAIR_ROWS = PEER_TOPK + 7 * 8 + 8


def _pair_candidates(v1, v2):
    tn = v1.shape[1]
    parts = [v1[0:1, :] + v2] + [v1[a:a + 1, :] + v2[0:8, :] for a in range(1, 8)] + [v1[8:16, :] + v2[0:1, :]]
    r = lax.broadcasted_iota(jnp.int32, (PAIR_ROWS, tn), 0)
    mid = r - PEER_TOPK
    pos = jnp.where(r < PEER_TOPK, r,
                    jnp.where(r < PEER_TOPK + 56, ((mid >> 3) + 1) * PEER_TOPK + (mid & 7), (r - 64) * PEER_TOPK))
    return jnp.concatenate(parts, axis=0), pos.astype(F32)


def _peer_route_kernel(qt_ref, sk_ref, lim_ref, coef_ref, rank2_ref, e2_ref, *, heads):
    half = PEER_DQ // 2
    row = lax.broadcasted_iota(jnp.int32, (PEER_NKEYS, qt_ref.shape[1]), 0).astype(F32)
    for hh in range(heads):
        q = qt_ref[hh * PEER_DQ:(hh + 1) * PEER_DQ, :]
        s1 = jnp.dot(sk_ref[hh, 0], q[0:half, :], preferred_element_type=F32, precision=lax.Precision.HIGHEST)
        s2 = jnp.dot(sk_ref[hh, 1], q[half:, :], preferred_element_type=F32, precision=lax.Precision.HIGHEST)
        v1, _, rank1 = _top_rows(s1, row, PEER_TOPK, True)
        v2, _, rank2 = _top_rows(s2, row, PEER_TOPK, True)
        cand, cpos = _pair_candidates(v1, v2)
        top, pos, _ = _top_rows(cand, cpos, PEER_TOPK, False)
        z = jnp.sum(jnp.exp(top - top[0:1, :]), axis=0, keepdims=True)
        a_of = jnp.floor(pos * (1.0 / PEER_TOPK))
        lim = jnp.zeros(s1.shape, F32)
        for a in range(PEER_TOPK):
            cnt = jnp.sum(jnp.where(a_of == float(a), 1.0, 0.0), axis=0, keepdims=True)
            lim = jnp.where(rank1 == float(a), cnt, lim)
        lim_ref[hh] = lim
        coef_ref[hh] = jnp.exp(s1 - v1[0:1, :]) / z
        rank2_ref[hh] = rank2.astype(BF16)
        e2_ref[hh] = jnp.exp(s2 - v2[0:1, :]).astype(BF16)


def _peer_route_call(q_t, subkeys, tn=LANES, heads=2):
    n = q_t.shape[1]
    shp = lambda dt: jax.ShapeDtypeStruct((PEER_HEADS, PEER_NKEYS, n), dt)
    ospec = pl.BlockSpec((heads, PEER_NKEYS, tn), lambda j, h: (h, 0, j))
    return pl.pallas_call(
        functools.partial(_peer_route_kernel, heads=heads),
        grid=(n // tn, PEER_HEADS // heads),
        in_specs=[
            pl.BlockSpec((heads * PEER_DQ, tn), lambda j, h: (h, j)),
            pl.BlockSpec((heads, 2, PEER_NKEYS, PEER_DQ // 2), lambda j, h: (h, 0, 0, 0)),
        ],
        out_specs=[ospec] * 4,
        out_shape=[shp(F32), shp(F32), shp(BF16), shp(BF16)],
        compiler_params=_cparams(("parallel", "arbitrary")),
        name="peer_route",
    )(q_t, subkeys)


def _peer_expert_kernel(hnt_ref, lim_ref, coef_ref, rank2_ref, e2_ref, u_ref, v_ref, o_ref, *rest, et, emit):
    acc_ref = rest[-1]
    t = pl.program_id(1)

    @pl.when(t == 0)
    def _():
        acc_ref[...] = jnp.zeros_like(acc_ref)

    if emit:
        ub = u_ref[...].astype(BF16)
        vtb = v_ref[...].T.astype(BF16)
        rest[0][...] = ub
        rest[1][...] = vtb
    else:
        ub = u_ref[...]
        vtb = v_ref[...]
    tn = hnt_ref.shape[1]
    hid = jnp.dot(ub, hnt_ref[...], preferred_element_type=F32)
    acts = []
    for ii in range(et // PEER_NKEYS):
        i1 = t * (et // PEER_NKEYS) + ii
        gate = jnp.zeros((PEER_NKEYS, tn), BF16)
        for h in range(PEER_HEADS):
            lim = lim_ref[h, pl.ds(i1, 1), :].astype(BF16)
            coef = coef_ref[h, pl.ds(i1, 1), :].astype(BF16)
            gate = gate + jnp.where(rank2_ref[h] < lim, e2_ref[h], jnp.zeros((), BF16)) * coef
        acts.append(gate * _gelu(hid[ii * PEER_NKEYS:(ii + 1) * PEER_NKEYS, :]).astype(BF16))
    acc_ref[...] += jnp.dot(vtb, jnp.concatenate(acts, axis=0), preferred_element_type=F32)

    @pl.when(t == pl.num_programs(1) - 1)
    def _():
        o_ref[...] = acc_ref[...].T


def _peer_expert_call(hn_t, route, u_tab, v_tab, tn, et, emit=False):
    n = hn_t.shape[1]
    n_exp = u_tab.shape[0]
    rspec = pl.BlockSpec((PEER_HEADS, PEER_NKEYS, tn), lambda j, t: (0, 0, j))
    uspec = pl.BlockSpec((et, D_MODEL), lambda j, t: (t, 0))
    vtspec = pl.BlockSpec((D_MODEL, et), lambda j, t: (0, t))
    ospec = pl.BlockSpec((tn, D_MODEL), lambda j, t: (j, 0))
    oshape = jax.ShapeDtypeStruct((n, D_MODEL), F32)
    if emit:
        assert n == tn
        out_specs = [ospec, uspec, vtspec]
        out_shape = [oshape, jax.ShapeDtypeStruct((n_exp, D_MODEL), BF16), jax.ShapeDtypeStruct((D_MODEL, n_exp), BF16)]
    else:
        out_specs, out_shape = ospec, oshape
    return pl.pallas_call(
        functools.partial(_peer_expert_kernel, et=et, emit=emit),
        grid=(n // tn, n_exp // et),
        in_specs=[pl.BlockSpec((D_MODEL, tn), lambda j, t: (0, j)), rspec, rspec, rspec, rspec, uspec,
                  uspec if emit else vtspec],
        out_specs=out_specs,
        out_shape=out_shape,
        scratch_shapes=[pltpu.VMEM((D_MODEL, tn), F32)],
        compiler_params=_cparams(("parallel", "arbitrary")),
        name="peer_experts",
    )(hn_t, *route, u_tab, v_tab)


def _final_norm_kernel(xa_ref, xb_ref, g_ref, o_ref):
    x = xa_ref[...] + xb_ref[...]
    ms = jnp.mean(x * x, axis=-1, keepdims=True)
    o_ref[...] = x * lax.rsqrt(ms + NORM_EPS) * g_ref[...]


def _final_norm_call(xa, xb, g, tm):
    n = xa.shape[0]
    row = pl.BlockSpec((tm, D_MODEL), lambda i: (i, 0))
    return pl.pallas_call(
        _final_norm_kernel,
        grid=(n // tm,),
        in_specs=[row, row, pl.BlockSpec((1, D_MODEL), lambda i: (0, 0))],
        out_specs=row,
        out_shape=jax.ShapeDtypeStruct((n, D_MODEL), F32),
        compiler_params=_cparams(("parallel",)),
        name="final_norm",
    )(xa, xb, g.reshape(1, D_MODEL))


DIFF_PAGES = 8


def _diff_sample_kernel(pt_ref, q_ref, knew_ref, vnew_ref, dl_ref, g_ref, *rest, past_len, lam_init):
    page_refs = [_row_view(r, PAGE_SIZE * 2 * A_HEADS) for r in rest[:DIFF_PAGES]]
    o_ref, m_ref, l_ref, acc_ref = rest[DIFF_PAGES:]
    p = pl.program_id(1)
    scale = A_HALF ** -0.5
    nrow = 2 * A_HEADS
    per_key = 2 * A_HEADS

    @pl.when(p == 0)
    def _():
        m_ref[...] = jnp.full(m_ref.shape, NEG_INF, F32)
        l_ref[...] = jnp.zeros(l_ref.shape, F32)
        acc_ref[...] = jnp.zeros(acc_ref.shape, F32)

    row = lax.broadcasted_iota(jnp.int32, (nrow, 1), 0)
    lane = lax.broadcasted_iota(jnp.int32, (1, A_DHEAD), 1)
    slope = jnp.exp2(-2.0 * ((row >> 1) + 1).astype(F32))
    q = q_ref[0]
    q2 = [jnp.where(((row >> 1) == h) & ((lane >= A_HALF) == ((row & 1) == 1)), q[:, h * A_DHEAD:(h + 1) * A_DHEAD], 0.0)
          for h in range(A_HEADS)]
    q2b = [x.astype(BF16) for x in q2]
    ss = []
    for g in range(DIFF_PAGES):
        sg = None
        for h in range(A_HEADS):
            kh = page_refs[g][pl.ds(h, PAGE_SIZE, stride=per_key), :].astype(BF16)
            d = lax.dot_general(q2b[h], kh, NT_DIMS, preferred_element_type=F32)
            sg = d if sg is None else sg + d
        ss.append(sg)
    s = jnp.concatenate(ss, axis=1) * scale
    nk = DIFF_PAGES * PAGE_SIZE
    kpos = p * nk + lax.broadcasted_iota(jnp.int32, (1, nk), 1)
    s = s - slope * (past_len - kpos).astype(F32)
    m_old = m_ref[:, 0:1]
    m_new = jnp.maximum(m_old, jnp.max(s, axis=-1, keepdims=True))
    alpha = jnp.exp(m_old - m_new)
    e = jnp.exp(s - m_new)
    eb = e.astype(BF16)
    l_new = alpha * l_ref[:, 0:1] + jnp.sum(e, axis=-1, keepdims=True)
    pv = jnp.zeros((nrow, A_DHEAD), F32)
    for h in range(A_HEADS):
        vh = jnp.concatenate([page_refs[g][pl.ds(A_HEADS + h, PAGE_SIZE, stride=per_key), :].astype(BF16)
                              for g in range(DIFF_PAGES)], axis=0)
        pv = pv + jnp.where((row >> 1) == h, jnp.dot(eb, vh, preferred_element_type=F32), 0.0)
    acc = alpha * acc_ref[...] + pv
    m_ref[...] = jnp.broadcast_to(m_new, m_ref.shape)
    l_ref[...] = jnp.broadcast_to(l_new, l_ref.shape)
    acc_ref[...] = acc

    @pl.when(p == pl.num_programs(1) - 1)
    def _():
        knew = knew_ref[0]
        vnew = vnew_ref[0]
        s_n = jnp.zeros((nrow, 1), F32)
        for h in range(A_HEADS):
            s_n = s_n + jnp.sum(q2[h] * knew[:, h * A_DHEAD:(h + 1) * A_DHEAD], axis=-1, keepdims=True)
        s_n = s_n * scale
        v8 = jnp.concatenate([vnew[:, (r // 2) * A_DHEAD:(r // 2 + 1) * A_DHEAD] for r in range(nrow)], axis=0)
        m_f = jnp.maximum(m_new, s_n)
        a_f = jnp.exp(m_new - m_f)
        e_n = jnp.exp(s_n - m_f)
        o8 = (a_f * acc + e_n * v8) / (a_f * l_new + e_n)
        lam = _diff_lambda(dl_ref[...], lam_init)
        outs = []
        for h in range(A_HEADS):
            oh = o8[2 * h:2 * h + 1, :] - lam * o8[2 * h + 1:2 * h + 2, :]
            ms = jnp.mean(oh * oh, axis=-1, keepdims=True)
            outs.append(oh * lax.rsqrt(ms + NORM_EPS) * g_ref[...] * (1.0 - lam_init))
        o_ref[0] = jnp.concatenate(outs, axis=1)


def _diff_sample_call(page_table, proj3, cache, layer, dl, subln_g, lam_init):
    bs, n_pages = page_table.shape
    past_len = n_pages * PAGE_SIZE
    steps = n_pages // DIFF_PAGES

    def page_spec(g):
        return pl.BlockSpec((None, None, PAGE_SIZE, 2, A_HEADS, A_DHEAD),
                            lambda b, p, pt: (layer, pt[b, p * DIFF_PAGES + g], 0, 0, 0, 0))

    grid_spec = pltpu.PrefetchScalarGridSpec(
        num_scalar_prefetch=1,
        grid=(bs, steps),
        in_specs=[
            pl.BlockSpec((1, 1, A_WIDTH), lambda b, p, pt: (b, 0, COL_AQ // A_WIDTH)),
            pl.BlockSpec((1, 1, A_WIDTH), lambda b, p, pt: (b, 0, COL_AK // A_WIDTH)),
            pl.BlockSpec((1, 1, A_WIDTH), lambda b, p, pt: (b, 0, COL_AV // A_WIDTH)),
            pl.BlockSpec((4, A_HALF), lambda b, p, pt: (0, 0)),
            pl.BlockSpec((1, A_DHEAD), lambda b, p, pt: (0, 0)),
        ] + [page_spec(g) for g in range(DIFF_PAGES)],
        out_specs=pl.BlockSpec((1, 1, A_WIDTH), lambda b, p, pt: (b, 0, 0)),
        scratch_shapes=[pltpu.VMEM((2 * A_HEADS, LANES), F32), pltpu.VMEM((2 * A_HEADS, LANES), F32),
                        pltpu.VMEM((2 * A_HEADS, A_DHEAD), F32)],
    )
    return pl.pallas_call(
        functools.partial(_diff_sample_kernel, past_len=past_len, lam_init=lam_init),
        grid_spec=grid_spec,
        out_shape=jax.ShapeDtypeStruct((bs, 1, A_WIDTH), F32),
        compiler_params=_cparams(("parallel", "arbitrary")),
        name="diff_attn_sample",
    )(page_table, proj3, proj3, proj3, dl, subln_g.reshape(1, A_DHEAD), *([cache] * DIFF_PAGES))


def _diag_rows(vec):
    n = vec.shape[1]
    r = lax.broadcasted_iota(jnp.int32, (n, n), 0)
    c = lax.broadcasted_iota(jnp.int32, (n, n), 1)
    return jnp.where(r == c, vec, 0.0)


def _ssd_sample_kernel(xbc_ref, z_ref, sm_ref, cbuf_ref, h0_ref, cw_ref, cb_ref, dtb_ref, alog_ref, dsk_ref, g_ref,
                       o_ref, hout_ref, cout_ref):
    hi = lax.Precision.HIGHEST
    new = xbc_ref[0]
    buf = cbuf_ref[0, 0]
    cw = cw_ref[...]
    conv = cb_ref[...] + cw[CONV_W - 1:CONV_W] * new
    for i in range(CONV_W - 1):
        conv = conv + cw[i:i + 1] * buf[i:i + 1]
    cout_ref[0] = jnp.concatenate([buf[1:CONV_W - 1], new], axis=0)
    xc = _silu(conv)
    xs = xc[:, :B_WIDTH]
    dt = _softplus(sm_ref[0] + dtb_ref[...])
    ea = jnp.exp(dt * (-jnp.exp(alog_ref[...])))
    hr = lax.broadcasted_iota(jnp.int32, (LANES, B_WIDTH), 0)
    hc = lax.broadcasted_iota(jnp.int32, (LANES, B_WIDTH), 1)
    rep = jnp.where((hc // B_HEADDIM) == hr, 1.0, 0.0)
    both = jnp.concatenate([dt, ea, jnp.zeros((6, LANES), F32)], axis=0)
    both_rep = jnp.dot(both, rep, preferred_element_type=F32, precision=hi)
    u = both_rep[0:1] * xs
    ea_rep = both_rep[1:2]
    gn = B_GROUPS * B_STATE
    rows = (B_HEADS // B_GROUPS) * B_HEADDIM
    ys = []
    for g in range(B_GROUPS):
        r0 = g * rows
        bg = xc[:, B_WIDTH + g * B_STATE:B_WIDTH + (g + 1) * B_STATE]
        cg = xc[:, B_WIDTH + gn + g * B_STATE:B_WIDTH + gn + (g + 1) * B_STATE]
        h0 = h0_ref[0, 0, r0:r0 + rows, :]
        hn = jnp.dot(_diag_rows(ea_rep[:, r0:r0 + rows]), h0, preferred_element_type=F32, precision=hi)
        hn = hn + jnp.dot(_diag_rows(u[:, r0:r0 + rows]), jnp.broadcast_to(bg, (rows, B_STATE)),
                          preferred_element_type=F32, precision=hi)
        hout_ref[0, r0:r0 + rows, :] = hn
        c8 = jnp.broadcast_to(cg, (8, B_STATE)).astype(BF16)
        ys.append(lax.dot_general(c8, hn.astype(BF16), NT_DIMS, preferred_element_type=F32)[0:1])
    y = jnp.concatenate(ys, axis=1) + dsk_ref[...] * xs
    y = y * _silu(z_ref[0])
    ms = jnp.mean(y * y, axis=-1, keepdims=True)
    o_ref[0] = y * lax.rsqrt(ms + NORM_EPS) * g_ref[...]


def _ssd_sample_call(proj3, state_conv, state_ssm4, layer, conv_w, conv_b, dt_bias, a_log, d_skip, norm_g):
    bs = proj3.shape[0]
    const = lambda b: (0, 0)
    nrow = B_HEADS * B_HEADDIM
    return pl.pallas_call(
        _ssd_sample_kernel,
        grid=(bs,),
        in_specs=[
            pl.BlockSpec((1, 1, B_CONV_DIM), lambda b: (b, 0, COL_XBC // B_CONV_DIM)),
            pl.BlockSpec((1, 1, B_WIDTH), lambda b: (b, 0, COL_Z // B_WIDTH)),
            pl.BlockSpec((1, 1, LANES), lambda b: (b, 0, COL_SMALL // LANES)),
            pl.BlockSpec((1, 1, CONV_W - 1, B_CONV_DIM), lambda b: (layer, b, 0, 0)),
            pl.BlockSpec((1, 1, nrow, B_STATE), lambda b: (layer, b, 0, 0)),
            pl.BlockSpec((CONV_W, B_CONV_DIM), const),
            pl.BlockSpec((1, B_CONV_DIM), const),
            pl.BlockSpec((1, LANES), const),
            pl.BlockSpec((1, LANES), const),
            pl.BlockSpec((1, B_WIDTH), const),
            pl.BlockSpec((1, B_WIDTH), const),
        ],
        out_specs=[
            pl.BlockSpec((1, 1, B_WIDTH), lambda b: (b, 0, 0)),
            pl.BlockSpec((1, nrow, B_STATE), lambda b: (b, 0, 0)),
            pl.BlockSpec((1, CONV_W - 1, B_CONV_DIM), lambda b: (b, 0, 0)),
        ],
        out_shape=[
            jax.ShapeDtypeStruct((bs, 1, B_WIDTH), F32),
            jax.ShapeDtypeStruct((bs, nrow, B_STATE), F32),
            jax.ShapeDtypeStruct((bs, CONV_W - 1, B_CONV_DIM), F32),
        ],
        compiler_params=_cparams(("parallel",)),
        name="ssd_sample",
    )(proj3, proj3, proj3, state_conv, state_ssm4, conv_w, conv_b.reshape(1, -1), _pad_lanes(dt_bias),
      _pad_lanes(a_log), jnp.repeat(d_skip, B_HEADDIM).reshape(1, B_WIDTH), norm_g.reshape(1, B_WIDTH))


CMP_PAGES = 32


def _compress_paged_kernel(pt_ref, pe_ref, w1_ref, *rest):
    page_refs = [_row_view(r, PAGE_SIZE * NSA_ROWS) for r in rest[:CMP_PAGES]]
    o_ref = rest[CMP_PAGES]
    per_page = PAGE_SIZE // CMP_STRIDE
    outs = []
    for kv in range(2):
        acc_lo = jnp.zeros((CMP_PAGES * per_page, C_DHEAD), F32)
        acc_hi = jnp.zeros((CMP_PAGES * per_page, C_DHEAD), F32)
        for r in range(CMP_STRIDE):
            x = jnp.concatenate([page_refs[g][pl.ds(NSA_ROWS * r + kv, per_page, stride=NSA_ROWS * CMP_STRIDE), :]
                                 for g in range(CMP_PAGES)], axis=0)
            x_lo = (x + pe_ref[kv, r:r + 1, :]).astype(BF16)
            x_hi = (x + pe_ref[kv, CMP_STRIDE + r:CMP_STRIDE + r + 1, :]).astype(BF16)
            acc_lo = acc_lo + jnp.dot(x_lo, w1_ref[kv, r].astype(BF16), preferred_element_type=F32)
            acc_hi = acc_hi + jnp.dot(x_hi, w1_ref[kv, CMP_STRIDE + r].astype(BF16), preferred_element_type=F32)
        outs += [acc_lo, acc_hi]
    o_ref[0] = jnp.concatenate(outs, axis=1)


def _compress_paged_call(page_table, cache, layer, pe, w1):
    bs, n_pages = page_table.shape
    steps = n_pages // CMP_PAGES
    per_page = PAGE_SIZE // CMP_STRIDE

    def page_spec(g):
        return pl.BlockSpec((None, None, PAGE_SIZE, NSA_ROWS, C_DHEAD),
                            lambda b, p, pt: (layer, pt[b, p * CMP_PAGES + g], 0, 0, 0))

    grid_spec = pltpu.PrefetchScalarGridSpec(
        num_scalar_prefetch=1,
        grid=(bs, steps),
        in_specs=[
            pl.BlockSpec((2, CMP_LEN, C_DHEAD), lambda b, p, pt: (0, 0, 0)),
            pl.BlockSpec((2, CMP_LEN, C_DHEAD, C_DHEAD), lambda b, p, pt: (0, 0, 0, 0)),
        ] + [page_spec(g) for g in range(CMP_PAGES)],
        out_specs=pl.BlockSpec((1, CMP_PAGES * per_page, 4 * C_DHEAD), lambda b, p, pt: (b, p, 0)),
    )
    return pl.pallas_call(
        _compress_paged_kernel,
        grid_spec=grid_spec,
        out_shape=jax.ShapeDtypeStruct((bs, n_pages * per_page, 4 * C_DHEAD), F32),
        compiler_params=_cparams(("parallel", "arbitrary")),
        name="nsa_compress_sample",
    )(page_table, pe, w1.reshape(2, CMP_LEN, C_DHEAD, C_DHEAD), *([cache] * CMP_PAGES))


def _heads_to_rows(q):
    rows = [q[:, h * C_DHEAD:(h + 1) * C_DHEAD] for h in range(C_HEADS)]
    return jnp.concatenate(rows + [jnp.zeros((8 - C_HEADS, C_DHEAD), F32)], axis=0)


SEL_LANES = 384


def _nsa_select_kernel(part_ref, w2_ref, q_ref, ocmp_ref, sel_ref, *, q_pos):
    nchunk = part_ref.shape[1]
    n_cmp = (q_pos + 1 - CMP_LEN) // CMP_STRIDE + 1
    n_sel = -(-(q_pos + 1) // SEL_BLOCK)
    scale = C_DHEAD ** -0.5
    part = part_ref[0]
    kv_cmp = []
    for kv in range(2):
        lo = part[:, (2 * kv) * C_DHEAD:(2 * kv + 1) * C_DHEAD]
        hi = part[:, (2 * kv + 1) * C_DHEAD:(2 * kv + 2) * C_DHEAD]
        hid = lo + pltpu.roll(hi, nchunk - 1, 0)
        kv_cmp.append(jnp.dot(_gelu(hid).astype(BF16), w2_ref[kv].astype(BF16), preferred_element_type=F32).astype(BF16))
    q8 = _heads_to_rows(q_ref[0]).astype(BF16)
    row = lax.broadcasted_iota(jnp.int32, (8, 1), 0)
    slope = jnp.exp2(-2.0 * (row + 1).astype(F32))
    n_i = lax.broadcasted_iota(jnp.int32, (1, nchunk), 1)
    dist_c = q_pos - (n_i * CMP_STRIDE + CMP_LEN - 1)
    ok = (dist_c >= 0) & (n_i < n_cmp)
    s = lax.dot_general(q8, kv_cmp[0], NT_DIMS, preferred_element_type=F32) * scale - slope * dist_c.astype(F32)
    e, den = _masked_softmax(s, ok)
    p = jnp.where(row < C_HEADS, e / jnp.maximum(den, 1e-30), 0.0)
    ocmp_ref[0] = jnp.dot(p.astype(BF16), kv_cmp[1], preferred_element_type=F32)
    psum = jnp.broadcast_to(jnp.sum(p, axis=0, keepdims=True), (8, nchunk))
    c_i = lax.broadcasted_iota(jnp.int32, (nchunk, SEL_LANES), 0)
    j_i = lax.broadcasted_iota(jnp.int32, (nchunk, SEL_LANES), 1)
    lo_ = jnp.maximum(c_i * CMP_STRIDE, j_i * SEL_BLOCK)
    hi_ = jnp.minimum(c_i * CMP_STRIDE + CMP_LEN, (j_i + 1) * SEL_BLOCK)
    ovl = jnp.where((c_i < n_cmp) & (j_i < n_sel), jnp.maximum(hi_ - lo_, 0).astype(F32) * (1.0 / CMP_LEN), 0.0)
    imp = jnp.dot(psum, ovl, preferred_element_type=F32, precision=lax.Precision.HIGHEST)[0:1]
    lane = lax.broadcasted_iota(jnp.int32, (1, SEL_LANES), 1)
    qblk = q_pos // SEL_BLOCK
    forced = (lane == 0) | (lane == qblk) | (lane == qblk - 1)
    score = jnp.where(lane <= qblk, imp + jnp.where(forced, FORCE_SCORE, 0.0), NEG_INF)
    score = jnp.where(lane < n_sel, score, -jnp.inf)
    lane_f = lane.astype(F32)
    out_lane = lax.broadcasted_iota(jnp.int32, (1, LANES), 1)
    sel = jnp.full((1, LANES), -1.0, F32)
    for k in range(min(SEL_TOPK, n_sel)):
        m = jnp.max(score, axis=-1, keepdims=True)
        idx = jnp.min(jnp.where(score == m, lane_f, 1e9), axis=-1, keepdims=True)
        sel = jnp.where(out_lane == k, jnp.where(m > NEG_INF / 2, idx, -1.0), sel)
        score = jnp.where(lane_f == idx, -jnp.inf, score)
    sel_ref[0] = sel.astype(jnp.int32)


def _nsa_select_call(part, w2, q3, q_pos):
    bs, nchunk, _ = part.shape
    return pl.pallas_call(
        functools.partial(_nsa_select_kernel, q_pos=q_pos),
        grid=(bs,),
        in_specs=[
            pl.BlockSpec((1, nchunk, 4 * C_DHEAD), lambda b: (b, 0, 0)),
            pl.BlockSpec((2, C_DHEAD, C_DHEAD), lambda b: (0, 0, 0)),
            pl.BlockSpec((1, 1, C_WIDTH), lambda b: (b, 0, COL_CQ // C_WIDTH)),
        ],
        out_specs=[pl.BlockSpec((1, 8, C_DHEAD), lambda b: (b, 0, 0)), pl.BlockSpec((1, 1, LANES), lambda b: (b, 0, 0))],
        out_shape=[jax.ShapeDtypeStruct((bs, 8, C_DHEAD), F32), jax.ShapeDtypeStruct((bs, 1, LANES), jnp.int32)],
        compiler_params=_cparams(("parallel",)),
        name="nsa_select_sample",
    )(part, w2, q3)


def _nsa_attend_kernel(sel_ref, pt_ref, q_ref, new_ref, wnew_ref, sm_ref, ocmp_ref, win_ref, g_ref, *rest, q_pos):
    k_eff = SEL_TOPK
    blk_refs = [_row_view(r, SEL_BLOCK * NSA_ROWS) for r in rest[:k_eff]]
    o_ref, wout_ref = rest[k_eff:]
    b = pl.program_id(0)
    scale = C_DHEAD ** -0.5
    n_past_blocks = q_pos // SEL_BLOCK
    q8f = _heads_to_rows(q_ref[0])
    q8 = q8f.astype(BF16)
    row = lax.broadcasted_iota(jnp.int32, (8, 1), 0)
    slope = jnp.exp2(-2.0 * (row + 1).astype(F32))
    lane64 = lax.broadcasted_iota(jnp.int32, (1, SEL_BLOCK), 1)

    ss, vs, oks = [], [], []
    new_sel = jnp.zeros((1, 1), jnp.int32)
    for k in range(k_eff):
        j = sel_ref[b, k]
        k_sel = blk_refs[k][pl.ds(2, SEL_BLOCK, stride=NSA_ROWS), :]
        v_sel = blk_refs[k][pl.ds(3, SEL_BLOCK, stride=NSA_ROWS), :]
        s = lax.dot_general(q8, k_sel.astype(BF16), NT_DIMS, preferred_element_type=F32)
        dist = q_pos - (j * SEL_BLOCK + lane64)
        ss.append(s * scale - slope * dist.astype(F32))
        oks.append(lane64 * 0 + jnp.where((j >= 0) & (j < n_past_blocks), 1, 0))
        vs.append(v_sel.astype(BF16))
        new_sel = new_sel + jnp.where(j == n_past_blocks, 1, 0)
    s = jnp.concatenate(ss, axis=1)
    ok = jnp.concatenate(oks, axis=1) > 0
    new = new_ref[0]
    s_n = jnp.sum(q8f * new[:, 2 * C_DHEAD:3 * C_DHEAD], axis=-1, keepdims=True) * scale
    s_n = jnp.where(new_sel > 0, s_n, NEG_INF)
    s = jnp.where(ok, s, NEG_INF)
    m = jnp.maximum(jnp.max(s, axis=-1, keepdims=True), s_n)
    e = jnp.where(ok, jnp.exp(s - m), 0.0)
    e_n = jnp.where(new_sel > 0, jnp.exp(s_n - m), 0.0)
    den = jnp.sum(e, axis=-1, keepdims=True) + e_n
    o_sel = (jnp.dot(e.astype(BF16), jnp.concatenate(vs, axis=0), preferred_element_type=F32)
             + e_n * new[:, 3 * C_DHEAD:]) / den

    lw = win_ref.shape[0]
    win_k = win_ref[:, 0, :]
    win_v = win_ref[:, 1, :]
    wnew = wnew_ref[0]
    wpos = lax.broadcasted_iota(jnp.int32, (1, lw), 1)
    dist_w = lw - wpos
    ok_w = dist_w < WINDOW
    s = lax.dot_general(q8, win_k.astype(BF16), NT_DIMS, preferred_element_type=F32) * scale
    s = jnp.where(ok_w, s - slope * dist_w.astype(F32), NEG_INF)
    s_n = jnp.sum(q8f * wnew[:, :C_DHEAD], axis=-1, keepdims=True) * scale
    m = jnp.maximum(jnp.max(s, axis=-1, keepdims=True), s_n)
    e = jnp.where(ok_w, jnp.exp(s - m), 0.0)
    e_n = jnp.exp(s_n - m)
    den = jnp.sum(e, axis=-1, keepdims=True) + e_n
    o_win = (jnp.dot(e.astype(BF16), win_v.astype(BF16), preferred_element_type=F32)
             + e_n * wnew[:, C_DHEAD:]) / den
    keep = min(WINDOW, lw + 1)
    wout_ref[0:keep - 1, :, :] = win_ref[lw + 1 - keep:lw, :, :]
    wout_ref[keep - 1:keep, 0, :] = wnew[:, :C_DHEAD]
    wout_ref[keep - 1:keep, 1, :] = wnew[:, C_DHEAD:]

    gate = 1.0 / (1.0 + jnp.exp(-sm_ref[0]))
    outs = []
    for h in range(C_HEADS):
        g0 = gate[:, GATE_LANE0 + h:GATE_LANE0 + h + 1]
        g1 = gate[:, GATE_LANE0 + C_HEADS + h:GATE_LANE0 + C_HEADS + h + 1]
        g2 = gate[:, GATE_LANE0 + 2 * C_HEADS + h:GATE_LANE0 + 2 * C_HEADS + h + 1]
        outs.append(g0 * ocmp_ref[0, h:h + 1, :] + g1 * o_sel[h:h + 1, :] + g2 * o_win[h:h + 1, :])
    o = jnp.concatenate(outs, axis=1)
    ms = jnp.mean(o * o, axis=-1, keepdims=True)
    o_ref[0] = o * lax.rsqrt(ms + NORM_EPS) * g_ref[...]


def _nsa_attend_call(sel, page_table, proj3, ocmp, cache, cache_win, layer, out_g, q_pos):
    bs = proj3.shape[0]
    lw = cache_win.shape[2]
    keep = min(WINDOW, lw + 1)
    n_pages = page_table.shape[1]
    halves = PAGE_SIZE // SEL_BLOCK

    def blk_spec(k):
        def imap(b, sel_r, pt_r):
            j = jnp.clip(sel_r[b, k], 0, n_pages * halves - 1)
            return (layer, pt_r[b, j // halves], j % halves, 0, 0)
        return pl.BlockSpec((None, None, SEL_BLOCK, NSA_ROWS, C_DHEAD), imap)

    row3 = lambda w, col: pl.BlockSpec((1, 1, w), lambda b, s_, p_: (b, 0, col))
    grid_spec = pltpu.PrefetchScalarGridSpec(
        num_scalar_prefetch=2,
        grid=(bs,),
        in_specs=[
            row3(C_WIDTH, COL_CQ // C_WIDTH),
            row3(4 * C_DHEAD, COL_CKV // (4 * C_DHEAD)),
            row3(2 * C_DHEAD, COL_WIN // (2 * C_DHEAD)),
            row3(LANES, COL_SMALL // LANES),
            pl.BlockSpec((1, 8, C_DHEAD), lambda b, s_, p_: (b, 0, 0)),
            pl.BlockSpec((None, None, lw, 2, C_DHEAD), lambda b, s_, p_: (layer, b, 0, 0, 0)),
            pl.BlockSpec((1, C_WIDTH), lambda b, s_, p_: (0, 0)),
        ] + [blk_spec(k) for k in range(SEL_TOPK)],
        out_specs=[
            pl.BlockSpec((1, 1, C_WIDTH), lambda b, s_, p_: (b, 0, 0)),
            pl.BlockSpec((None, keep, 2, C_DHEAD), lambda b, s_, p_: (b, 0, 0, 0)),
        ],
    )
    return pl.pallas_call(
        functools.partial(_nsa_attend_kernel, q_pos=q_pos),
        grid_spec=grid_spec,
        out_shape=[jax.ShapeDtypeStruct((bs, 1, C_WIDTH), F32), jax.ShapeDtypeStruct((bs, keep, 2, C_DHEAD), F32)],
        compiler_params=_cparams(("arbitrary",)),
        name="nsa_attend_sample",
    )(sel, page_table, proj3, proj3, proj3, proj3, ocmp, cache_win, out_g.reshape(1, C_WIDTH),
      *([cache] * SEL_TOPK))


def _reorder_w_in(w):
    return jnp.concatenate([
        w[:, 2560:4608], w[:, 1536:2560], w[:, 0:1536], w[:, 4624:5136], w[:, 5136:5904],
        w[:, 4608:4624], w[:, 5904:5916], jnp.zeros((D_MODEL, PROJ_W - 5916), w.dtype)], axis=1).astype(BF16)


def _token_mixer_tail(xa, xb, oa, ob, oc, w_out_b, ffn_g, wq_t, subkeys, u_tab, v_tab, tm, tn, et, emit_bf16=False):
    h, hn_t = _outproj_call(xa, xb, oa, ob, oc, w_out_b, ffn_g, tm)
    q_t = _peer_q_call(wq_t, hn_t, tm)
    route = _peer_route_call(q_t, subkeys)
    return h, _peer_expert_call(hn_t, route, u_tab, v_tab, tn, et, emit_bf16)


def kernel(x_prompt, x_sample, cache_diff_kv, cache_nsa_kv, cache_nsa_win, state_ssm, state_conv, page_table,
           norm_mix_g, w_in, w_out, diff_lam, diff_subln_g, ssm_conv_w, ssm_conv_b, ssm_dt_bias, ssm_a_log,
           ssm_d, ssm_norm_g, nsa_pe, nsa_cmp_w1, nsa_cmp_w2, nsa_out_g, norm_ffn_g, peer_wq, peer_subkeys,
           peer_u, peer_v, norm_final_g):
    depth = w_in.shape[0]
    bp, t, _ = x_prompt.shape
    bs = x_sample.shape[0]
    past_len = page_table.shape[1] * PAGE_SIZE
    n_p = bp * t
    n_s = LANES
    tm_p = 512

    xa_p, xb_p = x_prompt.reshape(n_p, D_MODEL), None
    xa_s = jnp.pad(x_sample.reshape(bs, D_MODEL), ((0, n_s - bs), (0, 0)))
    xb_s = None
    st_p = [[] for _ in range(5)]
    st_s = [[] for _ in range(5)]
    state_ssm4 = state_ssm.reshape(depth, bs, B_HEADS * B_HEADDIM, B_STATE)
    for l in range(depth):
        lam_init = 0.8 - 0.6 * math.exp(-0.3 * l)
        w_r = _reorder_w_in(w_in[l])
        w_out_b = w_out[l].astype(BF16)
        wq_t = peer_wq[l].T.astype(BF16)

        proj_s, dkv_s, nkv_s = _proj_call(xa_s, xb_s, norm_mix_g[l], w_r, n_s)
        proj_s = proj_s[:bs]
        proj_s3 = proj_s.reshape(bs, 1, PROJ_W)
        o_a = _diff_sample_call(page_table, proj_s3, cache_diff_kv, l, diff_lam[l], diff_subln_g[l], lam_init)
        o_b, h_new, conv_new = _ssd_sample_call(proj_s3, state_conv, state_ssm4, l, ssm_conv_w[l], ssm_conv_b[l],
                                                ssm_dt_bias[l], ssm_a_log[l], ssm_d[l], ssm_norm_g[l])
        part = _compress_paged_call(page_table, cache_nsa_kv, l, nsa_pe[l], nsa_cmp_w1[l])
        o_cmp, sel = _nsa_select_call(part, nsa_cmp_w2[l], proj_s3, past_len)
        o_c, win_out = _nsa_attend_call(sel[:, 0, :SEL_TOPK], page_table, proj_s3, o_cmp, cache_nsa_kv, cache_nsa_win, l,
                                        nsa_out_g[l], past_len)
        st_s[0].append(dkv_s[:bs].reshape(bs, 1, 2, A_HEADS, A_DHEAD))
        st_s[1].append(nkv_s[:bs].reshape(bs, 1, NSA_ROWS, C_DHEAD))
        st_s[2].append(win_out)
        st_s[3].append(h_new.reshape(bs, B_HEADS, B_HEADDIM, B_STATE))
        st_s[4].append(conv_new)
        pad = lambda a: jnp.pad(a.reshape(bs, -1), ((0, n_s - bs), (0, 0)))
        xa_s, (xb_s, u_b, vt_b) = _token_mixer_tail(xa_s, xb_s, pad(o_a), pad(o_b), pad(o_c), w_out_b, norm_ffn_g[l],
                                                   wq_t, peer_subkeys[l], peer_u[l], peer_v[l], n_s, n_s, 512, True)

        proj, dkv_p, nkv_p = _proj_call(xa_p, xb_p, norm_mix_g[l], w_r, tm_p)
        o_a = _diff_prompt_call(proj, bp, t, diff_lam[l], diff_subln_g[l], lam_init)
        o_b, h_ssm = _ssd_prompt_call(proj, bp, t, ssm_conv_w[l], ssm_conv_b[l], ssm_dt_bias[l], ssm_a_log[l],
                                      ssm_d[l], ssm_norm_g[l])
        kc, vc = _compress_prompt_call(proj, bp, t, nsa_pe[l], nsa_cmp_w1[l], nsa_cmp_w2[l])
        o_c = _nsa_prompt_call(proj, kc, vc, bp, t, nsa_out_g[l])
        proj3 = proj.reshape(bp, t, PROJ_W)
        st_p[0].append(dkv_p.reshape(bp, t, 2, A_HEADS, A_DHEAD))
        st_p[1].append(nkv_p.reshape(bp, t, NSA_ROWS, C_DHEAD))
        keep = min(WINDOW, t)
        st_p[2].append(proj3[:, t - keep:, COL_WIN:COL_WIN + 2 * C_DHEAD].reshape(bp, keep, 2, C_DHEAD))
        st_p[3].append(h_ssm)
        st_p[4].append(proj3[:, t - (CONV_W - 1):, COL_XBC:COL_XBC + B_CONV_DIM])
        xa_p, xb_p = _token_mixer_tail(xa_p, xb_p, o_a, o_b, o_c, w_out_b, norm_ffn_g[l], wq_t, peer_subkeys[l],
                                       u_b, vt_b, tm_p, 512, 1024)

    y_p = _final_norm_call(xa_p, xb_p, norm_final_g, tm_p).reshape(bp, t, D_MODEL)
    y_s = _final_norm_call(xa_s, xb_s, norm_final_g, n_s)[:bs].reshape(bs, 1, D_MODEL)
    return (y_p, y_s) + tuple(jnp.stack(s) for s in st_p) + tuple(jnp.stack(s) for s in st_s)
```

```python
import functools
import math

import jax
import jax.numpy as jnp
from jax import lax
from jax.experimental import pallas as pl
from jax.experimental.pallas import tpu as pltpu

F32 = jnp.float32
BF16 = jnp.bfloat16

D_MODEL = 2048
A_HEADS = 4
A_HALF = 64
A_DHEAD = 128
A_WIDTH = 512
B_WIDTH = 1024
B_HEADDIM = 64
B_HEADS = 16
B_GROUPS = 4
B_STATE = 128
CONV_W = 4
B_CONV_DIM = 2048
SSD_CHUNK = 128
C_HEADS = 4
C_DHEAD = 128
C_WIDTH = 512
CMP_LEN = 32
CMP_STRIDE = 16
SEL_BLOCK = 64
SEL_TOPK = 16
WINDOW = 512
PEER_HEADS = 8
PEER_NKEYS = 128
PEER_TOPK = 16
PEER_DQ = 256
PAGE_SIZE = 128
NSA_ROWS = 4
NORM_EPS = 1e-6
NEG_INF = -1e30
FORCE_SCORE = 1e4

LANES = 128
VMEM_LIMIT = 56 * 1024 * 1024

COL_AQ = 0
COL_AK = 512
COL_AV = 1024
COL_Z = 1536
COL_XBC = 2560
COL_MAIN = 4608
COL_CQ = 4608
COL_CKV = 5120
COL_WIN = 5632
COL_SMALL = 5888
PROJ_W = 6144
GATE_LANE0 = B_HEADS

NT_DIMS = (((1,), (1,)), ((), ()))


def _cparams(sem, vmem=VMEM_LIMIT):
    return pltpu.CompilerParams(dimension_semantics=sem, vmem_limit_bytes=vmem)


def _gelu(x):
    return 0.5 * x * (1.0 + jnp.tanh(math.sqrt(2.0 / math.pi) * (x + 0.044715 * (x * x * x))))


def _silu(x):
    return x * (1.0 / (1.0 + jnp.exp(-x)))


def _softplus(x):
    return jnp.maximum(x, 0.0) + jnp.log(1.0 + jnp.exp(-jnp.abs(x)))


def _row_view(ref, rows):
    return ref.reshape(rows, ref.shape[-1])


def _alibi_slope(h):
    if isinstance(h, int):
        return 2.0 ** (-2.0 * (h + 1))
    return jnp.exp2(jnp.full((1, 1), -2.0, F32) * (h + 1).astype(F32))


def _proj_kernel(*refs, two, tc):
    if two:
        xa_ref, xb_ref, g_ref, wm_ref, wt_ref, o_ref, dkv_ref, nkv_ref, xn_ref = refs
    else:
        xa_ref, g_ref, wm_ref, wt_ref, o_ref, dkv_ref, nkv_ref, xn_ref = refs
    j = pl.program_id(1)
    n_main = COL_MAIN // tc

    @pl.when(j == 0)
    def _():
        x = xa_ref[...]
        if two:
            x = x + xb_ref[...]
        ms = jnp.mean(x * x, axis=-1, keepdims=True)
        xn_ref[...] = (x * lax.rsqrt(ms + NORM_EPS) * g_ref[...]).astype(BF16)

    def store_cache_rows(res, tile):
        for b in range(2 * A_HEADS):
            col = COL_AK + b * LANES
            if col // tc == tile:
                dkv_ref[:, b // A_HEADS, b % A_HEADS, :] = res[:, col % tc:col % tc + LANES]
        for b in range(NSA_ROWS):
            col = COL_CKV + b * LANES
            if col // tc == tile:
                nkv_ref[:, b, :] = res[:, col % tc:col % tc + LANES]

    state_tiles = sorted({(COL_AK + b * LANES) // tc for b in range(2 * A_HEADS)}
                         | {(COL_CKV + b * LANES) // tc for b in range(NSA_ROWS)})

    def finish(res, tiles):
        o_ref[...] = res
        for tile in tiles:
            if tile in state_tiles:
                pl.when(j == tile)(functools.partial(store_cache_rows, res, tile))

    @pl.when(j < n_main)
    def _():
        finish(jnp.dot(xn_ref[...], wm_ref[...].astype(BF16), preferred_element_type=F32), range(n_main))

    @pl.when(j >= n_main)
    def _():
        finish(jnp.dot(xn_ref[...], wt_ref[...], preferred_element_type=F32), range(n_main, PROJ_W // tc))


def _proj_call(xa, xb, g, w_in, layer, w_tail, tm, tc=768):
    n = xa.shape[0]
    two = xb is not None
    n_main = COL_MAIN // tc
    assert COL_MAIN % tc == 0 and w_tail.shape[1] == PROJ_W - COL_MAIN
    xspec = pl.BlockSpec((tm, D_MODEL), lambda i, j: (i, 0))
    ins = [xa] + ([xb] if two else []) + [g.reshape(1, D_MODEL), w_in, w_tail]
    specs = [xspec] + ([xspec] if two else []) + [
        pl.BlockSpec((1, D_MODEL), lambda i, j: (0, 0)),
        pl.BlockSpec((None, D_MODEL, tc), lambda i, j: (layer, 0, jnp.minimum(j, n_main - 1))),
        pl.BlockSpec((D_MODEL, tc), lambda i, j: (0, jnp.maximum(j - n_main, 0))),
    ]
    return pl.pallas_call(
        functools.partial(_proj_kernel, two=two, tc=tc),
        grid=(n // tm, PROJ_W // tc),
        in_specs=specs,
        out_specs=[pl.BlockSpec((tm, tc), lambda i, j: (i, j)),
                   pl.BlockSpec((tm, 2, A_HEADS, A_DHEAD), lambda i, j: (i, 0, 0, 0)),
                   pl.BlockSpec((tm, NSA_ROWS, C_DHEAD), lambda i, j: (i, 0, 0))],
        out_shape=[jax.ShapeDtypeStruct((n, PROJ_W), F32), jax.ShapeDtypeStruct((n, 2, A_HEADS, A_DHEAD), F32),
                   jax.ShapeDtypeStruct((n, NSA_ROWS, C_DHEAD), F32)],
        scratch_shapes=[pltpu.VMEM((tm, D_MODEL), BF16)],
        compiler_params=_cparams(("parallel", "arbitrary")),
        name="in_proj",
    )(*ins)


def _diff_lambda(dl, lam_init):
    a = jnp.sum(dl[0:1] * dl[1:2], axis=-1, keepdims=True)
    b = jnp.sum(dl[2:3] * dl[3:4], axis=-1, keepdims=True)
    return jnp.exp(a) - jnp.exp(b) + lam_init


CAUSAL_LEVELS = 4


def _causal_prefixes(i, nq, tq, body):
    levels = min(CAUSAL_LEVELS, nq)
    per = nq // levels
    for lv in range(levels):
        pl.when(i // per == lv)(functools.partial(body, (lv + 1) * per * tq))


def _diff_prompt_kernel(q_ref, k_ref, v_ref, dl_ref, g_ref, o_ref, *, tq, lam_init):
    h = pl.program_id(1)
    i = pl.program_id(2)
    t = k_ref.shape[0]
    scale = A_HALF ** -0.5

    def body(nk):
        lam = _diff_lambda(dl_ref[...], lam_init)
        q = q_ref[...]
        lane = lax.broadcasted_iota(jnp.int32, (1, A_DHEAD), 1)
        kb = k_ref[0:nk, :].astype(BF16)
        vb = v_ref[0:nk, :].astype(BF16)
        qpos = i * tq + lax.broadcasted_iota(jnp.int32, (tq, 1), 0)
        kpos = lax.broadcasted_iota(jnp.int32, (1, nk), 1)
        dist = qpos - kpos
        ok = dist >= 0
        bias = _alibi_slope(h) * dist.astype(F32)

        def half_softmax(c):
            qc = jnp.where((lane >= c * A_HALF) & (lane < (c + 1) * A_HALF), q, 0.0).astype(BF16)
            s = lax.dot_general(qc, kb, NT_DIMS, preferred_element_type=F32) * scale
            s = jnp.where(ok, s - bias, NEG_INF)
            m = jnp.max(s, axis=-1, keepdims=True)
            e = jnp.exp(s - m)
            return e / jnp.sum(e, axis=-1, keepdims=True)

        pd = half_softmax(0) - lam * half_softmax(1)
        o = jnp.dot(pd.astype(BF16), vb, preferred_element_type=F32)
        ms = jnp.mean(o * o, axis=-1, keepdims=True)
        o_ref[...] = o * lax.rsqrt(ms + NORM_EPS) * g_ref[...] * (1.0 - lam_init)

    _causal_prefixes(i, t // tq, tq, body)


def _diff_prompt_call(proj, bsz, t, dl, subln_g, lam_init, tq=256):
    nq = t // tq
    cq, ck, cv = COL_AQ // A_DHEAD, COL_AK // A_DHEAD, COL_AV // A_DHEAD
    return pl.pallas_call(
        functools.partial(_diff_prompt_kernel, tq=tq, lam_init=lam_init),
        grid=(bsz, A_HEADS, nq),
        in_specs=[
            pl.BlockSpec((tq, A_DHEAD), lambda b, h, i: (b * nq + i, cq + h)),
            pl.BlockSpec((t, A_DHEAD), lambda b, h, i: (b, ck + h)),
            pl.BlockSpec((t, A_DHEAD), lambda b, h, i: (b, cv + h)),
            pl.BlockSpec((4, A_HALF), lambda b, h, i: (0, 0)),
            pl.BlockSpec((1, A_DHEAD), lambda b, h, i: (0, 0)),
        ],
        out_specs=pl.BlockSpec((tq, A_DHEAD), lambda b, h, i: (b * nq + i, h)),
        out_shape=jax.ShapeDtypeStruct((bsz * t, A_WIDTH), F32),
        compiler_params=_cparams(("parallel", "parallel", "arbitrary")),
        name="diff_attn_prompt",
    )(proj, proj, proj, dl, subln_g.reshape(1, A_DHEAD))


SSD_COLS = 512


def _ssd_prompt_kernel(*refs):
    nx, nz = B_CONV_DIM // SSD_COLS, B_WIDTH // SSD_COLS
    xbc_refs, z_refs = refs[:nx], refs[nx:nx + nz]
    sm_ref, cw_ref, cb_ref, dtb_ref, alog_ref, dsk_ref, g_ref, o_ref, hout_ref, buf_ref, h_ref = refs[nx + nz:]
    c = pl.program_id(1)
    cs = SSD_CHUNK

    @pl.when(c == 0)
    def _():
        buf_ref[0:8, :] = jnp.zeros((8, B_CONV_DIM), F32)
        h_ref[...] = jnp.zeros_like(h_ref)

    xbc = jnp.concatenate([r[...] for r in xbc_refs], axis=1)
    buf_ref[8:8 + cs, :] = xbc
    cw = cw_ref[...]
    conv = cb_ref[...] + cw[3:4] * xbc
    for j in range(1, CONV_W):
        conv = conv + cw[3 - j:4 - j] * buf_ref[8 - j:8 - j + cs, :]
    buf_ref[0:8, :] = xbc[cs - 8:cs, :]
    xc = _silu(conv)
    xs = xc[:, :B_WIDTH]

    dt = _softplus(sm_ref[...] + dtb_ref[...])
    a_neg = -jnp.exp(alog_ref[...])
    dta = dt * a_neg
    row = lax.broadcasted_iota(jnp.int32, (cs, cs), 0)
    col = lax.broadcasted_iota(jnp.int32, (cs, cs), 1)
    causal = row >= col
    acum = jnp.dot(causal.astype(F32), dta, preferred_element_type=F32, precision=lax.Precision.HIGHEST)
    acum_t = acum.T
    dt_t = dt.T
    lane = lax.broadcasted_iota(jnp.int32, (1, LANES), 1)
    lo = lane < B_HEADDIM

    ys = []
    for g in range(B_GROUPS):
        bg = xc[:, B_WIDTH + g * B_STATE:B_WIDTH + (g + 1) * B_STATE]
        cg = xc[:, B_WIDTH + B_GROUPS * B_STATE + g * B_STATE:B_WIDTH + B_GROUPS * B_STATE + (g + 1) * B_STATE]
        bgb = bg.astype(BF16)
        cgb = cg.astype(BF16)
        cb = lax.dot_general(cgb, bgb, NT_DIMS, preferred_element_type=F32)
        for pr in range(2):
            h0 = g * 4 + pr * 2
            xpair = xs[:, h0 * B_HEADDIM:(h0 + 2) * B_HEADDIM]
            xpb = xpair.astype(BF16)
            ydiag = []
            ecol = []
            wcol = []
            elast = []
            for hh in (h0, h0 + 1):
                a_col = acum[:, hh:hh + 1]
                a_row = acum_t[hh:hh + 1, :]
                decay = jnp.exp(jnp.where(causal, a_col - a_row, NEG_INF))
                lm = cb * decay * dt_t[hh:hh + 1, :]
                ydiag.append(jnp.dot(lm.astype(BF16), xpb, preferred_element_type=F32))
                a_last = acum[cs - 1:cs, hh:hh + 1]
                ecol.append(jnp.exp(a_col))
                wcol.append(jnp.exp(a_last - a_col) * dt[:, hh:hh + 1])
                elast.append(jnp.exp(a_last))
            hp = h_ref[h0 * B_HEADDIM:(h0 + 2) * B_HEADDIM, :]
            yoff = lax.dot_general(cgb, hp.astype(BF16), NT_DIMS, preferred_element_type=F32)
            y = jnp.where(lo, ydiag[0], ydiag[1]) + yoff * jnp.where(lo, ecol[0], ecol[1])
            ys.append(y)
            wx = xpair * jnp.where(lo, wcol[0], wcol[1])
            upd = jnp.dot(wx.T.astype(BF16), bgb, preferred_element_type=F32)
            prow = lax.broadcasted_iota(jnp.int32, (LANES, 1), 0) < B_HEADDIM
            h_ref[h0 * B_HEADDIM:(h0 + 2) * B_HEADDIM, :] = jnp.where(prow, elast[0], elast[1]) * hp + upd

    y = jnp.concatenate(ys, axis=1)
    y = y + dsk_ref[...] * xs
    y = y * _silu(jnp.concatenate([r[...] for r in z_refs], axis=1))
    ms = jnp.mean(y * y, axis=-1, keepdims=True)
    o_ref[...] = y * lax.rsqrt(ms + NORM_EPS) * g_ref[...]

    @pl.when(c == pl.num_programs(1) - 1)
    def _():
        hout_ref[0] = h_ref[...]


def _pad_lanes(v, fill=0.0):
    v = v.reshape(1, -1).astype(F32)
    return jnp.pad(v, ((0, 0), (0, LANES - v.shape[1])), constant_values=fill)


def _ssd_prompt_call(proj, bsz, t, conv_w, conv_b, dt_bias, a_log, d_skip, norm_g):
    nc = t // SSD_CHUNK
    cs = SSD_CHUNK
    const = lambda b, c: (0, 0)
    o, hout = pl.pallas_call(
        _ssd_prompt_kernel,
        grid=(bsz, nc),
        in_specs=[
            *[pl.BlockSpec((cs, SSD_COLS), functools.partial(lambda k, b, c: (b * nc + c, COL_XBC // SSD_COLS + k), k))
              for k in range(B_CONV_DIM // SSD_COLS)],
            *[pl.BlockSpec((cs, SSD_COLS), functools.partial(lambda k, b, c: (b * nc + c, COL_Z // SSD_COLS + k), k))
              for k in range(B_WIDTH // SSD_COLS)],
            pl.BlockSpec((cs, LANES), lambda b, c: (b * nc + c, COL_SMALL // LANES)),
            pl.BlockSpec((CONV_W, B_CONV_DIM), const),
            pl.BlockSpec((1, B_CONV_DIM), const),
            pl.BlockSpec((1, LANES), const),
            pl.BlockSpec((1, LANES), const),
            pl.BlockSpec((1, B_WIDTH), const),
            pl.BlockSpec((1, B_WIDTH), const),
        ],
        out_specs=[
            pl.BlockSpec((cs, B_WIDTH), lambda b, c: (b * nc + c, 0)),
            pl.BlockSpec((1, B_HEADS * B_HEADDIM, B_STATE), lambda b, c: (b, 0, 0)),
        ],
        out_shape=[
            jax.ShapeDtypeStruct((bsz * t, B_WIDTH), F32),
            jax.ShapeDtypeStruct((bsz, B_HEADS * B_HEADDIM, B_STATE), F32),
        ],
        scratch_shapes=[pltpu.VMEM((8 + cs, B_CONV_DIM), F32), pltpu.VMEM((B_HEADS * B_HEADDIM, B_STATE), F32)],
        compiler_params=_cparams(("parallel", "arbitrary")),
        name="ssd_prompt",
    )(*([proj] * (B_CONV_DIM // SSD_COLS + B_WIDTH // SSD_COLS + 1)), conv_w, conv_b.reshape(1, -1),
      _pad_lanes(dt_bias), _pad_lanes(a_log), jnp.repeat(d_skip, B_HEADDIM).reshape(1, B_WIDTH),
      norm_g.reshape(1, B_WIDTH))
    return o, hout.reshape(bsz, B_HEADS, B_HEADDIM, B_STATE)


def _compress_kernel(k_ref, v_ref, pe_ref, w1_ref, w2_ref, kc_ref, vc_ref, *, nchunk):
    outs = []
    for kv, rows_ref in enumerate((k_ref, v_ref)):
        acc_lo = jnp.zeros((nchunk, C_DHEAD), F32)
        acc_hi = jnp.zeros((nchunk, C_DHEAD), F32)
        for r in range(CMP_STRIDE):
            x = rows_ref[pl.ds(r, nchunk, stride=CMP_STRIDE), :]
            x_lo = (x + pe_ref[kv, r:r + 1, :]).astype(BF16)
            x_hi = (x + pe_ref[kv, CMP_STRIDE + r:CMP_STRIDE + r + 1, :]).astype(BF16)
            acc_lo = acc_lo + jnp.dot(x_lo, w1_ref[kv, r].astype(BF16), preferred_element_type=F32)
            acc_hi = acc_hi + jnp.dot(x_hi, w1_ref[kv, CMP_STRIDE + r].astype(BF16), preferred_element_type=F32)
        hid = acc_lo + pltpu.roll(acc_hi, nchunk - 1, 0)
        outs.append(jnp.dot(_gelu(hid).astype(BF16), w2_ref[kv].astype(BF16), preferred_element_type=F32))
    kc_ref[0] = outs[0]
    vc_ref[0] = outs[1]


def _compress_prompt_call(proj, bsz, t, pe, w1, w2):
    nchunk = t // CMP_STRIDE
    shp = jax.ShapeDtypeStruct((bsz, nchunk, C_DHEAD), F32)
    return pl.pallas_call(
        functools.partial(_compress_kernel, nchunk=nchunk),
        grid=(bsz,),
        in_specs=[
            pl.BlockSpec((t, C_DHEAD), lambda b: (b, COL_CKV // C_DHEAD)),
            pl.BlockSpec((t, C_DHEAD), lambda b: (b, COL_CKV // C_DHEAD + 1)),
            pl.BlockSpec((2, CMP_LEN, C_DHEAD), lambda b: (0, 0, 0)),
            pl.BlockSpec((2, CMP_LEN, C_DHEAD, C_DHEAD), lambda b: (0, 0, 0, 0)),
            pl.BlockSpec((2, C_DHEAD, C_DHEAD), lambda b: (0, 0, 0)),
        ],
        out_specs=[pl.BlockSpec((1, nchunk, C_DHEAD), lambda b: (b, 0, 0))] * 2,
        out_shape=[shp, shp],
        compiler_params=_cparams(("parallel",)),
        name="nsa_compress_prompt",
    )(proj, proj, pe, w1.reshape(2, CMP_LEN, C_DHEAD, C_DHEAD), w2)


def _masked_softmax(s, ok):
    s = jnp.where(ok, s, NEG_INF)
    m = jnp.max(s, axis=-1, keepdims=True)
    e = jnp.where(ok, jnp.exp(s - m), 0.0)
    return e, jnp.sum(e, axis=-1, keepdims=True)


def _topk_mask_lanes(score, k, n):
    lane = lax.broadcasted_iota(jnp.int32, (1, LANES), 1)
    rank = jnp.zeros(score.shape, F32)
    for i in range(n):
        ci = score[:, i:i + 1]
        beats = (ci > score) | ((ci == score) & (lane > i))
        rank = rank + jnp.where(beats, 1.0, 0.0)
    return (rank < k) & (lane < n)


def _nsa_prompt_kernel(q_ref, ks_ref, vs_ref, kw_ref, vw_ref, kc_ref, vc_ref, sm_ref, g_ref, o_ref, *, tq, n_cmp):
    i = pl.program_id(1)
    t = ks_ref.shape[0]
    n_sel = t // SEL_BLOCK
    scale = C_DHEAD ** -0.5
    sel_shift = SEL_BLOCK.bit_length() - 1
    wlen = min(t, WINDOW + tq)

    def body(nk):
        qpos = i * tq + lax.broadcasted_iota(jnp.int32, (tq, 1), 0)
        lane = lax.broadcasted_iota(jnp.int32, (1, LANES), 1)

        cmp_end = lane * CMP_STRIDE + (CMP_LEN - 1)
        dist_c = qpos - cmp_end
        ok_c = (dist_c >= 0) & (lane < n_cmp)
        dist_cf = dist_c.astype(F32)
        kcb = kc_ref[0].astype(BF16)
        vcb = vc_ref[0].astype(BF16)
        qs = [q_ref[:, h * C_DHEAD:(h + 1) * C_DHEAD].astype(BF16) for h in range(C_HEADS)]
        o_cmp = []
        psum = jnp.zeros((tq, LANES), F32)
        for h in range(C_HEADS):
            s = lax.dot_general(qs[h], kcb, NT_DIMS, preferred_element_type=F32) * scale
            s = s - _alibi_slope(h) * dist_cf
            e, den = _masked_softmax(s, ok_c)
            p = e / jnp.maximum(den, 1e-30)
            psum = psum + p
            o_cmp.append(jnp.dot(p.astype(BF16), vcb, preferred_element_type=F32))

        n_i = lax.broadcasted_iota(jnp.int32, (LANES, LANES), 0)
        j_i = lax.broadcasted_iota(jnp.int32, (LANES, LANES), 1)
        lo_ = jnp.maximum(n_i * CMP_STRIDE, j_i * SEL_BLOCK)
        hi_ = jnp.minimum(n_i * CMP_STRIDE + CMP_LEN, (j_i + 1) * SEL_BLOCK)
        ovl = jnp.maximum(hi_ - lo_, 0).astype(F32) * (1.0 / CMP_LEN)
        ovl = jnp.where((n_i < n_cmp) & (j_i < n_sel), ovl, 0.0)
        imp = jnp.dot(psum, ovl, preferred_element_type=F32, precision=lax.Precision.HIGHEST)
        qblk = qpos >> sel_shift
        sel_valid = lane <= qblk
        forced = (lane == 0) | (lane == qblk) | (lane == qblk - 1)
        score = jnp.where(sel_valid, imp + jnp.where(forced, FORCE_SCORE, 0.0), NEG_INF)
        score = jnp.where(lane < n_sel, score, -jnp.inf)
        chosen = _topk_mask_lanes(score, min(SEL_TOPK, n_sel), n_sel) & sel_valid
        e_j = lax.broadcasted_iota(jnp.int32, (LANES, nk), 0)
        e_k = lax.broadcasted_iota(jnp.int32, (LANES, nk), 1)
        expand = jnp.where((e_k >> sel_shift) == e_j, 1.0, 0.0).astype(BF16)
        key_sel = jnp.dot(jnp.where(chosen, 1.0, 0.0).astype(BF16), expand, preferred_element_type=F32) > 0.5
        dist = qpos - lax.broadcasted_iota(jnp.int32, (1, nk), 1)
        dist_f = dist.astype(F32)
        ok_s = key_sel & (dist >= 0)
        w0 = pl.multiple_of(jnp.clip(i * tq - WINDOW, 0, t - wlen), 8)
        dist_w = qpos - (w0 + lax.broadcasted_iota(jnp.int32, (1, wlen), 1))
        dist_wf = dist_w.astype(F32)
        ok_w = (dist_w >= 0) & (dist_w < WINDOW)

        ksb = ks_ref[0:nk, :].astype(BF16)
        vsb = vs_ref[0:nk, :].astype(BF16)
        kwb = kw_ref[pl.ds(w0, wlen), :].astype(BF16)
        vwb = vw_ref[pl.ds(w0, wlen), :].astype(BF16)
        gate = 1.0 / (1.0 + jnp.exp(-sm_ref[...]))
        outs = []
        for h in range(C_HEADS):
            s = lax.dot_general(qs[h], ksb, NT_DIMS, preferred_element_type=F32) * scale - _alibi_slope(h) * dist_f
            e, den = _masked_softmax(s, ok_s)
            o_sel = jnp.dot((e / den).astype(BF16), vsb, preferred_element_type=F32)
            s = lax.dot_general(qs[h], kwb, NT_DIMS, preferred_element_type=F32) * scale - _alibi_slope(h) * dist_wf
            e, den = _masked_softmax(s, ok_w)
            o_win = jnp.dot((e / den).astype(BF16), vwb, preferred_element_type=F32)
            g0 = gate[:, GATE_LANE0 + h:GATE_LANE0 + h + 1]
            g1 = gate[:, GATE_LANE0 + C_HEADS + h:GATE_LANE0 + C_HEADS + h + 1]
            g2 = gate[:, GATE_LANE0 + 2 * C_HEADS + h:GATE_LANE0 + 2 * C_HEADS + h + 1]
            outs.append(g0 * o_cmp[h] + g1 * o_sel + g2 * o_win)
        o = jnp.concatenate(outs, axis=1)
        ms = jnp.mean(o * o, axis=-1, keepdims=True)
        o_ref[...] = o * lax.rsqrt(ms + NORM_EPS) * g_ref[...]

    body(t)


def _nsa_prompt_call(proj, kc, vc, bsz, t, out_g, tq=256):
    nq = t // tq
    n_cmp = (t - CMP_LEN) // CMP_STRIDE + 1
    c0 = COL_CKV // C_DHEAD
    w0 = COL_WIN // C_DHEAD
    kvspec = lambda col: pl.BlockSpec((t, C_DHEAD), lambda b, i: (b, col))
    return pl.pallas_call(
        functools.partial(_nsa_prompt_kernel, tq=tq, n_cmp=n_cmp),
        grid=(bsz, nq),
        in_specs=[
            pl.BlockSpec((tq, C_WIDTH), lambda b, i: (b * nq + i, COL_CQ // C_WIDTH)),
            kvspec(c0 + 2), kvspec(c0 + 3), kvspec(w0), kvspec(w0 + 1),
            pl.BlockSpec((1, kc.shape[1], C_DHEAD), lambda b, i: (b, 0, 0)),
            pl.BlockSpec((1, kc.shape[1], C_DHEAD), lambda b, i: (b, 0, 0)),
            pl.BlockSpec((tq, LANES), lambda b, i: (b * nq + i, COL_SMALL // LANES)),
            pl.BlockSpec((1, C_WIDTH), lambda b, i: (0, 0)),
        ],
        out_specs=pl.BlockSpec((tq, C_WIDTH), lambda b, i: (b * nq + i, 0)),
        out_shape=jax.ShapeDtypeStruct((bsz * t, C_WIDTH), F32),
        compiler_params=_cparams(("parallel", "arbitrary")),
        name="nsa_attn_prompt",
    )(proj, proj, proj, proj, proj, kc, vc, proj, out_g.reshape(1, C_WIDTH))


def _outproj_kernel(*refs, two):
    if two:
        xa_ref, xb_ref, oa_ref, ob_ref, oc_ref, w_ref, g_ref, h_ref, hnt_ref = refs
    else:
        xa_ref, oa_ref, ob_ref, oc_ref, w_ref, g_ref, h_ref, hnt_ref = refs
    x = xa_ref[...]
    if two:
        x = x + xb_ref[...]
    mixed = jnp.dot(oa_ref[...].astype(BF16), w_ref[0:A_WIDTH, :], preferred_element_type=F32)
    mixed = mixed + jnp.dot(ob_ref[...].astype(BF16), w_ref[A_WIDTH:A_WIDTH + B_WIDTH, :], preferred_element_type=F32)
    mixed = mixed + jnp.dot(oc_ref[...].astype(BF16), w_ref[A_WIDTH + B_WIDTH:, :], preferred_element_type=F32)
    h = x + mixed
    h_ref[...] = h
    ms = jnp.mean(h * h, axis=-1, keepdims=True)
    hnt_ref[...] = (h * lax.rsqrt(ms + NORM_EPS) * g_ref[...]).T.astype(BF16)


def _outproj_call(xa, xb, oa, ob, oc, w_out_b, g, tm):
    n = xa.shape[0]
    two = xb is not None
    row = lambda w: pl.BlockSpec((tm, w), lambda i: (i, 0))
    ins = [xa] + ([xb] if two else []) + [oa, ob, oc, w_out_b, g.reshape(1, D_MODEL)]
    specs = [row(D_MODEL)] + ([row(D_MODEL)] if two else []) + [
        row(A_WIDTH), row(B_WIDTH), row(C_WIDTH),
        pl.BlockSpec((D_MODEL, D_MODEL), lambda i: (0, 0)),
        pl.BlockSpec((1, D_MODEL), lambda i: (0, 0)),
    ]
    return pl.pallas_call(
        functools.partial(_outproj_kernel, two=two),
        grid=(n // tm,),
        in_specs=specs,
        out_specs=[row(D_MODEL), pl.BlockSpec((D_MODEL, tm), lambda i: (0, i))],
        out_shape=[jax.ShapeDtypeStruct((n, D_MODEL), F32), jax.ShapeDtypeStruct((D_MODEL, n), BF16)],
        compiler_params=_cparams(("parallel",)),
        name="out_proj",
    )(*ins)


def _peer_q_kernel(wqt_ref, hnt_ref, qt_ref):
    qt_ref[...] = jnp.dot(wqt_ref[...], hnt_ref[...], preferred_element_type=F32)


def _peer_q_call(wq_t, hn_t, tm):
    n = hn_t.shape[1]
    dq = wq_t.shape[0]
    return pl.pallas_call(
        _peer_q_kernel,
        grid=(n // tm,),
        in_specs=[pl.BlockSpec((dq, D_MODEL), lambda i: (0, 0)), pl.BlockSpec((D_MODEL, tm), lambda i: (0, i))],
        out_specs=pl.BlockSpec((dq, tm), lambda i: (0, i)),
        out_shape=jax.ShapeDtypeStruct((dq, n), F32),
        compiler_params=_cparams(("parallel",)),
        name="peer_query",
    )(wq_t, hn_t)


NOT_RANKED = 99.0


def _top_rows(s, pos, k, want_rank):
    rank = jnp.full(s.shape, NOT_RANKED, F32) if want_rank else None
    vals, picks = [], []
    for j in range(k):
        m = jnp.max(s, axis=0, keepdims=True)
        idx = jnp.min(jnp.where(s == m, pos, 1e9), axis=0, keepdims=True)
        hit = pos == idx
        if want_rank:
            rank = jnp.where(hit, float(j), rank)
        s = jnp.where(hit, -jnp.inf, s)
        vals.append(m)
        picks.append(idx)
    return jnp.concatenate(vals, axis=0), jnp.concatenate(picks, axis=0), rank


PAIR_ROWS = PEER_TOPK + 7 * 8 + 8


def _pair_candidates(v1, v2):
    tn = v1.shape[1]
    parts = [v1[0:1, :] + v2] + [v1[a:a + 1, :] + v2[0:8, :] for a in range(1, 8)] + [v1[8:16, :] + v2[0:1, :]]
    r = lax.broadcasted_iota(jnp.int32, (PAIR_ROWS, tn), 0)
    mid = r - PEER_TOPK
    pos = jnp.where(r < PEER_TOPK, r,
                    jnp.where(r < PEER_TOPK + 56, ((mid >> 3) + 1) * PEER_TOPK + (mid & 7), (r - 64) * PEER_TOPK))
    return jnp.concatenate(parts, axis=0), pos.astype(F32)


def _peer_route_kernel(qt_ref, sk_ref, lim_ref, coef_ref, rank2_ref, e2_ref, *, heads):
    half = PEER_DQ // 2
    row = lax.broadcasted_iota(jnp.int32, (PEER_NKEYS, qt_ref.shape[1]), 0).astype(F32)
    for hh in range(heads):
        q = qt_ref[hh * PEER_DQ:(hh + 1) * PEER_DQ, :]
        s1 = jnp.dot(sk_ref[hh, 0], q[0:half, :], preferred_element_type=F32, precision=lax.Precision.HIGHEST)
        s2 = jnp.dot(sk_ref[hh, 1], q[half:, :], preferred_element_type=F32, precision=lax.Precision.HIGHEST)
        v1, _, rank1 = _top_rows(s1, row, PEER_TOPK, True)
        v2, _, rank2 = _top_rows(s2, row, PEER_TOPK, True)
        cand, cpos = _pair_candidates(v1, v2)
        top, pos, _ = _top_rows(cand, cpos, PEER_TOPK, False)
        z = jnp.sum(jnp.exp(top - top[0:1, :]), axis=0, keepdims=True)
        a_of = jnp.floor(pos * (1.0 / PEER_TOPK))
        lim = jnp.zeros(s1.shape, F32)
        for a in range(PEER_TOPK):
            cnt = jnp.sum(jnp.where(a_of == float(a), 1.0, 0.0), axis=0, keepdims=True)
            lim = jnp.where(rank1 == float(a), cnt, lim)
        lim_ref[hh] = lim
        coef_ref[hh] = jnp.exp(s1 - v1[0:1, :]) / z
        rank2_ref[hh] = rank2.astype(BF16)
        e2_ref[hh] = jnp.exp(s2 - v2[0:1, :]).astype(BF16)


def _peer_route_call(q_t, subkeys, tn=LANES, heads=2):
    n = q_t.shape[1]
    shp = lambda dt: jax.ShapeDtypeStruct((PEER_HEADS, PEER_NKEYS, n), dt)
    ospec = pl.BlockSpec((heads, PEER_NKEYS, tn), lambda j, h: (h, 0, j))
    return pl.pallas_call(
        functools.partial(_peer_route_kernel, heads=heads),
        grid=(n // tn, PEER_HEADS // heads),
        in_specs=[
            pl.BlockSpec((heads * PEER_DQ, tn), lambda j, h: (h, j)),
            pl.BlockSpec((heads, 2, PEER_NKEYS, PEER_DQ // 2), lambda j, h: (h, 0, 0, 0)),
        ],
        out_specs=[ospec] * 4,
        out_shape=[shp(F32), shp(F32), shp(BF16), shp(BF16)],
        compiler_params=_cparams(("parallel", "arbitrary")),
        name="peer_route",
    )(q_t, subkeys)


def _peer_expert_kernel(hnt_ref, lim_ref, coef_ref, rank2_ref, e2_ref, u_ref, v_ref, o_ref, *rest, et, emit):
    acc_ref = rest[-1]
    t = pl.program_id(1)

    @pl.when(t == 0)
    def _():
        acc_ref[...] = jnp.zeros_like(acc_ref)

    if emit:
        ub = u_ref[...].astype(BF16)
        vtb = v_ref[...].T.astype(BF16)
        rest[0][...] = ub
        rest[1][...] = vtb
    else:
        ub = u_ref[...]
        vtb = v_ref[...]
    tn = hnt_ref.shape[1]
    hid = jnp.dot(ub, hnt_ref[...], preferred_element_type=F32)
    acts = []
    for ii in range(et // PEER_NKEYS):
        i1 = t * (et // PEER_NKEYS) + ii
        gate = jnp.zeros((PEER_NKEYS, tn), BF16)
        for h in range(PEER_HEADS):
            lim = lim_ref[h, pl.ds(i1, 1), :].astype(BF16)
            coef = coef_ref[h, pl.ds(i1, 1), :].astype(BF16)
            gate = gate + jnp.where(rank2_ref[h] < lim, e2_ref[h], jnp.zeros((), BF16)) * coef
        acts.append(gate * _gelu(hid[ii * PEER_NKEYS:(ii + 1) * PEER_NKEYS, :]).astype(BF16))
    acc_ref[...] += jnp.dot(vtb, jnp.concatenate(acts, axis=0), preferred_element_type=F32)

    @pl.when(t == pl.num_programs(1) - 1)
    def _():
        o_ref[...] = acc_ref[...].T


def _peer_expert_call(hn_t, route, u_tab, v_tab, tn, et, emit=False, layer=0):
    n = hn_t.shape[1]
    n_exp = u_tab.shape[-2] if emit else u_tab.shape[0]
    rspec = pl.BlockSpec((PEER_HEADS, PEER_NKEYS, tn), lambda j, t: (0, 0, j))
    uspec = pl.BlockSpec((et, D_MODEL), lambda j, t: (t, 0))
    vtspec = pl.BlockSpec((D_MODEL, et), lambda j, t: (0, t))
    ospec = pl.BlockSpec((tn, D_MODEL), lambda j, t: (j, 0))
    oshape = jax.ShapeDtypeStruct((n, D_MODEL), F32)
    if emit:
        assert n == tn
        out_specs = [ospec, uspec, vtspec]
        out_shape = [oshape, jax.ShapeDtypeStruct((n_exp, D_MODEL), BF16), jax.ShapeDtypeStruct((D_MODEL, n_exp), BF16)]
    else:
        out_specs, out_shape = ospec, oshape
    return pl.pallas_call(
        functools.partial(_peer_expert_kernel, et=et, emit=emit),
        grid=(n // tn, n_exp // et),
        in_specs=[pl.BlockSpec((D_MODEL, tn), lambda j, t: (0, j)), rspec, rspec, rspec, rspec]
        + ([pl.BlockSpec((None, et, D_MODEL), lambda j, t: (layer, t, 0))] * 2 if emit else [uspec, vtspec]),
        out_specs=out_specs,
        out_shape=out_shape,
        scratch_shapes=[pltpu.VMEM((D_MODEL, tn), F32)],
        compiler_params=_cparams(("parallel", "arbitrary")),
        name="peer_experts",
    )(hn_t, *route, u_tab, v_tab)


def _final_norm_kernel(xa_ref, xb_ref, g_ref, o_ref):
    x = xa_ref[...] + xb_ref[...]
    ms = jnp.mean(x * x, axis=-1, keepdims=True)
    o_ref[...] = x * lax.rsqrt(ms + NORM_EPS) * g_ref[...]


def _final_norm_call(xa, xb, g, tm):
    n = xa.shape[0]
    row = pl.BlockSpec((tm, D_MODEL), lambda i: (i, 0))
    return pl.pallas_call(
        _final_norm_kernel,
        grid=(n // tm,),
        in_specs=[row, row, pl.BlockSpec((1, D_MODEL), lambda i: (0, 0))],
        out_specs=row,
        out_shape=jax.ShapeDtypeStruct((n, D_MODEL), F32),
        compiler_params=_cparams(("parallel",)),
        name="final_norm",
    )(xa, xb, g.reshape(1, D_MODEL))


DIFF_PAGES = 8


def _diff_sample_kernel(pt_ref, q_ref, knew_ref, vnew_ref, dl_ref, g_ref, *rest, past_len, lam_init):
    page_refs = [_row_view(r, PAGE_SIZE * 2 * A_HEADS) for r in rest[:DIFF_PAGES]]
    o_ref, m_ref, l_ref, acc_ref = rest[DIFF_PAGES:]
    p = pl.program_id(1)
    scale = A_HALF ** -0.5
    nrow = 2 * A_HEADS
    per_key = 2 * A_HEADS

    @pl.when(p == 0)
    def _():
        m_ref[...] = jnp.full(m_ref.shape, NEG_INF, F32)
        l_ref[...] = jnp.zeros(l_ref.shape, F32)
        acc_ref[...] = jnp.zeros(acc_ref.shape, F32)

    row = lax.broadcasted_iota(jnp.int32, (nrow, 1), 0)
    lane = lax.broadcasted_iota(jnp.int32, (1, A_DHEAD), 1)
    slope = jnp.exp2(-2.0 * ((row >> 1) + 1).astype(F32))
    q = q_ref[0]
    q2 = [jnp.where(((row >> 1) == h) & ((lane >= A_HALF) == ((row & 1) == 1)), q[:, h * A_DHEAD:(h + 1) * A_DHEAD], 0.0)
          for h in range(A_HEADS)]
    q2b = [x.astype(BF16) for x in q2]
    ss = []
    for g in range(DIFF_PAGES):
        sg = None
        for h in range(A_HEADS):
            kh = page_refs[g][pl.ds(h, PAGE_SIZE, stride=per_key), :].astype(BF16)
            d = lax.dot_general(q2b[h], kh, NT_DIMS, preferred_element_type=F32)
            sg = d if sg is None else sg + d
        ss.append(sg)
    s = jnp.concatenate(ss, axis=1) * scale
    nk = DIFF_PAGES * PAGE_SIZE
    kpos = p * nk + lax.broadcasted_iota(jnp.int32, (1, nk), 1)
    s = s - slope * (past_len - kpos).astype(F32)
    m_old = m_ref[:, 0:1]
    m_new = jnp.maximum(m_old, jnp.max(s, axis=-1, keepdims=True))
    alpha = jnp.exp(m_old - m_new)
    e = jnp.exp(s - m_new)
    eb = e.astype(BF16)
    l_new = alpha * l_ref[:, 0:1] + jnp.sum(e, axis=-1, keepdims=True)
    pv = jnp.zeros((nrow, A_DHEAD), F32)
    for h in range(A_HEADS):
        vh = jnp.concatenate([page_refs[g][pl.ds(A_HEADS + h, PAGE_SIZE, stride=per_key), :].astype(BF16)
                              for g in range(DIFF_PAGES)], axis=0)
        pv = pv + jnp.where((row >> 1) == h, jnp.dot(eb, vh, preferred_element_type=F32), 0.0)
    acc = alpha * acc_ref[...] + pv
    m_ref[...] = jnp.broadcast_to(m_new, m_ref.shape)
    l_ref[...] = jnp.broadcast_to(l_new, l_ref.shape)
    acc_ref[...] = acc

    @pl.when(p == pl.num_programs(1) - 1)
    def _():
        knew = knew_ref[0]
        vnew = vnew_ref[0]
        s_n = jnp.zeros((nrow, 1), F32)
        for h in range(A_HEADS):
            s_n = s_n + jnp.sum(q2[h] * knew[:, h * A_DHEAD:(h + 1) * A_DHEAD], axis=-1, keepdims=True)
        s_n = s_n * scale
        v8 = jnp.concatenate([vnew[:, (r // 2) * A_DHEAD:(r // 2 + 1) * A_DHEAD] for r in range(nrow)], axis=0)
        m_f = jnp.maximum(m_new, s_n)
        a_f = jnp.exp(m_new - m_f)
        e_n = jnp.exp(s_n - m_f)
        o8 = (a_f * acc + e_n * v8) / (a_f * l_new + e_n)
        lam = _diff_lambda(dl_ref[...], lam_init)
        outs = []
        for h in range(A_HEADS):
            oh = o8[2 * h:2 * h + 1, :] - lam * o8[2 * h + 1:2 * h + 2, :]
            ms = jnp.mean(oh * oh, axis=-1, keepdims=True)
            outs.append(oh * lax.rsqrt(ms + NORM_EPS) * g_ref[...] * (1.0 - lam_init))
        o_ref[0] = jnp.concatenate(outs, axis=1)


def _diff_sample_call(page_table, proj3, cache, layer, dl, subln_g, lam_init):
    bs, n_pages = page_table.shape
    past_len = n_pages * PAGE_SIZE
    steps = n_pages // DIFF_PAGES

    def page_spec(g):
        return pl.BlockSpec((None, None, PAGE_SIZE, 2, A_HEADS, A_DHEAD),
                            lambda b, p, pt: (layer, pt[b, p * DIFF_PAGES + g], 0, 0, 0, 0))

    grid_spec = pltpu.PrefetchScalarGridSpec(
        num_scalar_prefetch=1,
        grid=(bs, steps),
        in_specs=[
            pl.BlockSpec((1, 1, A_WIDTH), lambda b, p, pt: (b, 0, COL_AQ // A_WIDTH)),
            pl.BlockSpec((1, 1, A_WIDTH), lambda b, p, pt: (b, 0, COL_AK // A_WIDTH)),
            pl.BlockSpec((1, 1, A_WIDTH), lambda b, p, pt: (b, 0, COL_AV // A_WIDTH)),
            pl.BlockSpec((4, A_HALF), lambda b, p, pt: (0, 0)),
            pl.BlockSpec((1, A_DHEAD), lambda b, p, pt: (0, 0)),
        ] + [page_spec(g) for g in range(DIFF_PAGES)],
        out_specs=pl.BlockSpec((1, 1, A_WIDTH), lambda b, p, pt: (b, 0, 0)),
        scratch_shapes=[pltpu.VMEM((2 * A_HEADS, LANES), F32), pltpu.VMEM((2 * A_HEADS, LANES), F32),
                        pltpu.VMEM((2 * A_HEADS, A_DHEAD), F32)],
    )
    return pl.pallas_call(
        functools.partial(_diff_sample_kernel, past_len=past_len, lam_init=lam_init),
        grid_spec=grid_spec,
        out_shape=jax.ShapeDtypeStruct((bs, 1, A_WIDTH), F32),
        compiler_params=_cparams(("parallel", "arbitrary")),
        name="diff_attn_sample",
    )(page_table, proj3, proj3, proj3, dl, subln_g.reshape(1, A_DHEAD), *([cache] * DIFF_PAGES))


def _diag_rows(vec):
    n = vec.shape[1]
    r = lax.broadcasted_iota(jnp.int32, (n, n), 0)
    c = lax.broadcasted_iota(jnp.int32, (n, n), 1)
    return jnp.where(r == c, vec, 0.0)


def _ssd_sample_kernel(*refs):
    nx, nz = B_CONV_DIM // SSD_COLS, B_WIDTH // SSD_COLS
    xbc_refs, z_refs = refs[:nx], refs[nx:nx + nz]
    (sm_ref, cbuf_ref, h0_ref, cw_ref, cb_ref, dtb_ref, alog_ref, dsk_ref, g_ref,
     o_ref, hout_ref, cout_ref) = refs[nx + nz:]
    hi = lax.Precision.HIGHEST
    new = jnp.concatenate([r[0] for r in xbc_refs], axis=1)
    buf = cbuf_ref[0, 0]
    cw = cw_ref[...]
    conv = cb_ref[...] + cw[CONV_W - 1:CONV_W] * new
    for i in range(CONV_W - 1):
        conv = conv + cw[i:i + 1] * buf[i:i + 1]
    cout_ref[0] = jnp.concatenate([buf[1:CONV_W - 1], new], axis=0)
    xc = _silu(conv)
    xs = xc[:, :B_WIDTH]
    dt = _softplus(sm_ref[0] + dtb_ref[...])
    ea = jnp.exp(dt * (-jnp.exp(alog_ref[...])))
    hr = lax.broadcasted_iota(jnp.int32, (LANES, B_WIDTH), 0)
    hc = lax.broadcasted_iota(jnp.int32, (LANES, B_WIDTH), 1)
    rep = jnp.where((hc // B_HEADDIM) == hr, 1.0, 0.0)
    both = jnp.concatenate([dt, ea, jnp.zeros((6, LANES), F32)], axis=0)
    both_rep = jnp.dot(both, rep, preferred_element_type=F32, precision=hi)
    u = both_rep[0:1] * xs
    ea_rep = both_rep[1:2]
    gn = B_GROUPS * B_STATE
    rows = (B_HEADS // B_GROUPS) * B_HEADDIM
    ys = []
    for g in range(B_GROUPS):
        r0 = g * rows
        bg = xc[:, B_WIDTH + g * B_STATE:B_WIDTH + (g + 1) * B_STATE]
        cg = xc[:, B_WIDTH + gn + g * B_STATE:B_WIDTH + gn + (g + 1) * B_STATE]
        h0 = h0_ref[0, 0, r0:r0 + rows, :]
        hn = jnp.dot(_diag_rows(ea_rep[:, r0:r0 + rows]), h0, preferred_element_type=F32, precision=hi)
        hn = hn + jnp.dot(_diag_rows(u[:, r0:r0 + rows]), jnp.broadcast_to(bg, (rows, B_STATE)),
                          preferred_element_type=F32, precision=hi)
        hout_ref[0, r0:r0 + rows, :] = hn
        c8 = jnp.broadcast_to(cg, (8, B_STATE)).astype(BF16)
        ys.append(lax.dot_general(c8, hn.astype(BF16), NT_DIMS, preferred_element_type=F32)[0:1])
    y = jnp.concatenate(ys, axis=1) + dsk_ref[...] * xs
    y = y * _silu(jnp.concatenate([r[0] for r in z_refs], axis=1))
    ms = jnp.mean(y * y, axis=-1, keepdims=True)
    o_ref[0] = y * lax.rsqrt(ms + NORM_EPS) * g_ref[...]


def _ssd_sample_call(proj3, state_conv, state_ssm4, layer, conv_w, conv_b, dt_bias, a_log, d_skip, norm_g):
    bs = proj3.shape[0]
    const = lambda b: (0, 0)
    nrow = B_HEADS * B_HEADDIM
    return pl.pallas_call(
        _ssd_sample_kernel,
        grid=(bs,),
        in_specs=[
            *[pl.BlockSpec((1, 1, SSD_COLS), functools.partial(lambda k, b: (b, 0, COL_XBC // SSD_COLS + k), k))
              for k in range(B_CONV_DIM // SSD_COLS)],
            *[pl.BlockSpec((1, 1, SSD_COLS), functools.partial(lambda k, b: (b, 0, COL_Z // SSD_COLS + k), k))
              for k in range(B_WIDTH // SSD_COLS)],
            pl.BlockSpec((1, 1, LANES), lambda b: (b, 0, COL_SMALL // LANES)),
            pl.BlockSpec((1, 1, CONV_W - 1, B_CONV_DIM), lambda b: (layer, b, 0, 0)),
            pl.BlockSpec((1, 1, nrow, B_STATE), lambda b: (layer, b, 0, 0)),
            pl.BlockSpec((CONV_W, B_CONV_DIM), const),
            pl.BlockSpec((1, B_CONV_DIM), const),
            pl.BlockSpec((1, LANES), const),
            pl.BlockSpec((1, LANES), const),
            pl.BlockSpec((1, B_WIDTH), const),
            pl.BlockSpec((1, B_WIDTH), const),
        ],
        out_specs=[
            pl.BlockSpec((1, 1, B_WIDTH), lambda b: (b, 0, 0)),
            pl.BlockSpec((1, nrow, B_STATE), lambda b: (b, 0, 0)),
            pl.BlockSpec((1, CONV_W - 1, B_CONV_DIM), lambda b: (b, 0, 0)),
        ],
        out_shape=[
            jax.ShapeDtypeStruct((bs, 1, B_WIDTH), F32),
            jax.ShapeDtypeStruct((bs, nrow, B_STATE), F32),
            jax.ShapeDtypeStruct((bs, CONV_W - 1, B_CONV_DIM), F32),
        ],
        compiler_params=_cparams(("parallel",)),
        name="ssd_sample",
    )(*([proj3] * (B_CONV_DIM // SSD_COLS + B_WIDTH // SSD_COLS + 1)), state_conv, state_ssm4, conv_w,
      conv_b.reshape(1, -1), _pad_lanes(dt_bias),
      _pad_lanes(a_log), jnp.repeat(d_skip, B_HEADDIM).reshape(1, B_WIDTH), norm_g.reshape(1, B_WIDTH))


CMP_PAGES = 32


def _compress_paged_kernel(pt_ref, pe_ref, w1_ref, *rest):
    page_refs = [_row_view(r, PAGE_SIZE * NSA_ROWS) for r in rest[:CMP_PAGES]]
    o_ref = rest[CMP_PAGES]
    per_page = PAGE_SIZE // CMP_STRIDE
    outs = []
    for kv in range(2):
        acc_lo = jnp.zeros((CMP_PAGES * per_page, C_DHEAD), F32)
        acc_hi = jnp.zeros((CMP_PAGES * per_page, C_DHEAD), F32)
        for r in range(CMP_STRIDE):
            x = jnp.concatenate([page_refs[g][pl.ds(NSA_ROWS * r + kv, per_page, stride=NSA_ROWS * CMP_STRIDE), :]
                                 for g in range(CMP_PAGES)], axis=0)
            x_lo = (x + pe_ref[kv, r:r + 1, :]).astype(BF16)
            x_hi = (x + pe_ref[kv, CMP_STRIDE + r:CMP_STRIDE + r + 1, :]).astype(BF16)
            acc_lo = acc_lo + jnp.dot(x_lo, w1_ref[kv, r].astype(BF16), preferred_element_type=F32)
            acc_hi = acc_hi + jnp.dot(x_hi, w1_ref[kv, CMP_STRIDE + r].astype(BF16), preferred_element_type=F32)
        outs += [acc_lo, acc_hi]
    o_ref[0] = jnp.concatenate(outs, axis=1)


def _compress_paged_call(page_table, cache, layer, pe, w1):
    bs, n_pages = page_table.shape
    steps = n_pages // CMP_PAGES
    per_page = PAGE_SIZE // CMP_STRIDE

    def page_spec(g):
        return pl.BlockSpec((None, None, PAGE_SIZE, NSA_ROWS, C_DHEAD),
                            lambda b, p, pt: (layer, pt[b, p * CMP_PAGES + g], 0, 0, 0))

    grid_spec = pltpu.PrefetchScalarGridSpec(
        num_scalar_prefetch=1,
        grid=(bs, steps),
        in_specs=[
            pl.BlockSpec((2, CMP_LEN, C_DHEAD), lambda b, p, pt: (0, 0, 0)),
            pl.BlockSpec((2, CMP_LEN, C_DHEAD, C_DHEAD), lambda b, p, pt: (0, 0, 0, 0)),
        ] + [page_spec(g) for g in range(CMP_PAGES)],
        out_specs=pl.BlockSpec((1, CMP_PAGES * per_page, 4 * C_DHEAD), lambda b, p, pt: (b, p, 0)),
    )
    return pl.pallas_call(
        _compress_paged_kernel,
        grid_spec=grid_spec,
        out_shape=jax.ShapeDtypeStruct((bs, n_pages * per_page, 4 * C_DHEAD), F32),
        compiler_params=_cparams(("parallel", "arbitrary")),
        name="nsa_compress_sample",
    )(page_table, pe, w1.reshape(2, CMP_LEN, C_DHEAD, C_DHEAD), *([cache] * CMP_PAGES))


def _heads_to_rows(q):
    rows = [q[:, h * C_DHEAD:(h + 1) * C_DHEAD] for h in range(C_HEADS)]
    return jnp.concatenate(rows + [jnp.zeros((8 - C_HEADS, C_DHEAD), F32)], axis=0)


SEL_LANES = 384


def _nsa_select_kernel(part_ref, w2_ref, q_ref, ocmp_ref, sel_ref, *, q_pos):
    nchunk = part_ref.shape[1]
    n_cmp = (q_pos + 1 - CMP_LEN) // CMP_STRIDE + 1
    n_sel = -(-(q_pos + 1) // SEL_BLOCK)
    scale = C_DHEAD ** -0.5
    part = part_ref[0]
    kv_cmp = []
    for kv in range(2):
        lo = part[:, (2 * kv) * C_DHEAD:(2 * kv + 1) * C_DHEAD]
        hi = part[:, (2 * kv + 1) * C_DHEAD:(2 * kv + 2) * C_DHEAD]
        hid = lo + pltpu.roll(hi, nchunk - 1, 0)
        kv_cmp.append(jnp.dot(_gelu(hid).astype(BF16), w2_ref[kv].astype(BF16), preferred_element_type=F32).astype(BF16))
    q8 = _heads_to_rows(q_ref[0]).astype(BF16)
    row = lax.broadcasted_iota(jnp.int32, (8, 1), 0)
    slope = jnp.exp2(-2.0 * (row + 1).astype(F32))
    n_i = lax.broadcasted_iota(jnp.int32, (1, nchunk), 1)
    dist_c = q_pos - (n_i * CMP_STRIDE + CMP_LEN - 1)
    ok = (dist_c >= 0) & (n_i < n_cmp)
    s = lax.dot_general(q8, kv_cmp[0], NT_DIMS, preferred_element_type=F32) * scale - slope * dist_c.astype(F32)
    e, den = _masked_softmax(s, ok)
    p = jnp.where(row < C_HEADS, e / jnp.maximum(den, 1e-30), 0.0)
    ocmp_ref[0] = jnp.dot(p.astype(BF16), kv_cmp[1], preferred_element_type=F32)
    psum = jnp.broadcast_to(jnp.sum(p, axis=0, keepdims=True), (8, nchunk))
    c_i = lax.broadcasted_iota(jnp.int32, (nchunk, SEL_LANES), 0)
    j_i = lax.broadcasted_iota(jnp.int32, (nchunk, SEL_LANES), 1)
    lo_ = jnp.maximum(c_i * CMP_STRIDE, j_i * SEL_BLOCK)
    hi_ = jnp.minimum(c_i * CMP_STRIDE + CMP_LEN, (j_i + 1) * SEL_BLOCK)
    ovl = jnp.where((c_i < n_cmp) & (j_i < n_sel), jnp.maximum(hi_ - lo_, 0).astype(F32) * (1.0 / CMP_LEN), 0.0)
    imp = jnp.dot(psum, ovl, preferred_element_type=F32, precision=lax.Precision.HIGHEST)[0:1]
    lane = lax.broadcasted_iota(jnp.int32, (1, SEL_LANES), 1)
    qblk = q_pos // SEL_BLOCK
    forced = (lane == 0) | (lane == qblk) | (lane == qblk - 1)
    score = jnp.where(lane <= qblk, imp + jnp.where(forced, FORCE_SCORE, 0.0), NEG_INF)
    score = jnp.where(lane < n_sel, score, -jnp.inf)
    lane_f = lane.astype(F32)
    out_lane = lax.broadcasted_iota(jnp.int32, (1, LANES), 1)
    sel = jnp.full((1, LANES), -1.0, F32)
    for k in range(min(SEL_TOPK, n_sel)):
        m = jnp.max(score, axis=-1, keepdims=True)
        idx = jnp.min(jnp.where(score == m, lane_f, 1e9), axis=-1, keepdims=True)
        sel = jnp.where(out_lane == k, jnp.where(m > NEG_INF / 2, idx, -1.0), sel)
        score = jnp.where(lane_f == idx, -jnp.inf, score)
    sel_ref[0] = sel.astype(jnp.int32)


def _nsa_select_call(part, w2, q3, q_pos):
    bs, nchunk, _ = part.shape
    return pl.pallas_call(
        functools.partial(_nsa_select_kernel, q_pos=q_pos),
        grid=(bs,),
        in_specs=[
            pl.BlockSpec((1, nchunk, 4 * C_DHEAD), lambda b: (b, 0, 0)),
            pl.BlockSpec((2, C_DHEAD, C_DHEAD), lambda b: (0, 0, 0)),
            pl.BlockSpec((1, 1, C_WIDTH), lambda b: (b, 0, COL_CQ // C_WIDTH)),
        ],
        out_specs=[pl.BlockSpec((1, 8, C_DHEAD), lambda b: (b, 0, 0)), pl.BlockSpec((1, 1, LANES), lambda b: (b, 0, 0))],
        out_shape=[jax.ShapeDtypeStruct((bs, 8, C_DHEAD), F32), jax.ShapeDtypeStruct((bs, 1, LANES), jnp.int32)],
        compiler_params=_cparams(("parallel",)),
        name="nsa_select_sample",
    )(part, w2, q3)


def _nsa_attend_kernel(sel_ref, pt_ref, q_ref, new_ref, wnew_ref, sm_ref, ocmp_ref, win_ref, g_ref, *rest, q_pos):
    k_eff = SEL_TOPK
    blk_refs = [_row_view(r, SEL_BLOCK * NSA_ROWS) for r in rest[:k_eff]]
    o_ref, wout_ref = rest[k_eff:]
    b = pl.program_id(0)
    scale = C_DHEAD ** -0.5
    n_past_blocks = q_pos // SEL_BLOCK
    q8f = _heads_to_rows(q_ref[0])
    q8 = q8f.astype(BF16)
    row = lax.broadcasted_iota(jnp.int32, (8, 1), 0)
    slope = jnp.exp2(-2.0 * (row + 1).astype(F32))
    lane64 = lax.broadcasted_iota(jnp.int32, (1, SEL_BLOCK), 1)

    ss, vs, oks = [], [], []
    new_sel = jnp.zeros((1, 1), jnp.int32)
    for k in range(k_eff):
        j = sel_ref[b, k]
        k_sel = blk_refs[k][pl.ds(2, SEL_BLOCK, stride=NSA_ROWS), :]
        v_sel = blk_refs[k][pl.ds(3, SEL_BLOCK, stride=NSA_ROWS), :]
        s = lax.dot_general(q8, k_sel.astype(BF16), NT_DIMS, preferred_element_type=F32)
        dist = q_pos - (j * SEL_BLOCK + lane64)
        ss.append(s * scale - slope * dist.astype(F32))
        oks.append(lane64 * 0 + jnp.where((j >= 0) & (j < n_past_blocks), 1, 0))
        vs.append(v_sel.astype(BF16))
        new_sel = new_sel + jnp.where(j == n_past_blocks, 1, 0)
    s = jnp.concatenate(ss, axis=1)
    ok = jnp.concatenate(oks, axis=1) > 0
    new = new_ref[0]
    s_n = jnp.sum(q8f * new[:, 2 * C_DHEAD:3 * C_DHEAD], axis=-1, keepdims=True) * scale
    s_n = jnp.where(new_sel > 0, s_n, NEG_INF)
    s = jnp.where(ok, s, NEG_INF)
    m = jnp.maximum(jnp.max(s, axis=-1, keepdims=True), s_n)
    e = jnp.where(ok, jnp.exp(s - m), 0.0)
    e_n = jnp.where(new_sel > 0, jnp.exp(s_n - m), 0.0)
    den = jnp.sum(e, axis=-1, keepdims=True) + e_n
    o_sel = (jnp.dot(e.astype(BF16), jnp.concatenate(vs, axis=0), preferred_element_type=F32)
             + e_n * new[:, 3 * C_DHEAD:]) / den

    lw = win_ref.shape[0]
    win_k = win_ref[:, 0, :]
    win_v = win_ref[:, 1, :]
    wnew = wnew_ref[0]
    wpos = lax.broadcasted_iota(jnp.int32, (1, lw), 1)
    dist_w = lw - wpos
    ok_w = dist_w < WINDOW
    s = lax.dot_general(q8, win_k.astype(BF16), NT_DIMS, preferred_element_type=F32) * scale
    s = jnp.where(ok_w, s - slope * dist_w.astype(F32), NEG_INF)
    s_n = jnp.sum(q8f * wnew[:, :C_DHEAD], axis=-1, keepdims=True) * scale
    m = jnp.maximum(jnp.max(s, axis=-1, keepdims=True), s_n)
    e = jnp.where(ok_w, jnp.exp(s - m), 0.0)
    e_n = jnp.exp(s_n - m)
    den = jnp.sum(e, axis=-1, keepdims=True) + e_n
    o_win = (jnp.dot(e.astype(BF16), win_v.astype(BF16), preferred_element_type=F32)
             + e_n * wnew[:, C_DHEAD:]) / den
    keep = min(WINDOW, lw + 1)
    wout_ref[0:keep - 1, :, :] = win_ref[lw + 1 - keep:lw, :, :]
    wout_ref[keep - 1:keep, 0, :] = wnew[:, :C_DHEAD]
    wout_ref[keep - 1:keep, 1, :] = wnew[:, C_DHEAD:]

    gate = 1.0 / (1.0 + jnp.exp(-sm_ref[0]))
    outs = []
    for h in range(C_HEADS):
        g0 = gate[:, GATE_LANE0 + h:GATE_LANE0 + h + 1]
        g1 = gate[:, GATE_LANE0 + C_HEADS + h:GATE_LANE0 + C_HEADS + h + 1]
        g2 = gate[:, GATE_LANE0 + 2 * C_HEADS + h:GATE_LANE0 + 2 * C_HEADS + h + 1]
        outs.append(g0 * ocmp_ref[0, h:h + 1, :] + g1 * o_sel[h:h + 1, :] + g2 * o_win[h:h + 1, :])
    o = jnp.concatenate(outs, axis=1)
    ms = jnp.mean(o * o, axis=-1, keepdims=True)
    o_ref[0] = o * lax.rsqrt(ms + NORM_EPS) * g_ref[...]


def _nsa_attend_call(sel, page_table, proj3, ocmp, cache, cache_win, layer, out_g, q_pos):
    bs = proj3.shape[0]
    lw = cache_win.shape[2]
    keep = min(WINDOW, lw + 1)
    n_pages = page_table.shape[1]
    halves = PAGE_SIZE // SEL_BLOCK

    def blk_spec(k):
        def imap(b, sel_r, pt_r):
            j = jnp.clip(sel_r[b, k], 0, n_pages * halves - 1)
            return (layer, pt_r[b, j // halves], j % halves, 0, 0)
        return pl.BlockSpec((None, None, SEL_BLOCK, NSA_ROWS, C_DHEAD), imap)

    row3 = lambda w, col: pl.BlockSpec((1, 1, w), lambda b, s_, p_: (b, 0, col))
    grid_spec = pltpu.PrefetchScalarGridSpec(
        num_scalar_prefetch=2,
        grid=(bs,),
        in_specs=[
            row3(C_WIDTH, COL_CQ // C_WIDTH),
            row3(4 * C_DHEAD, COL_CKV // (4 * C_DHEAD)),
            row3(2 * C_DHEAD, COL_WIN // (2 * C_DHEAD)),
            row3(LANES, COL_SMALL // LANES),
            pl.BlockSpec((1, 8, C_DHEAD), lambda b, s_, p_: (b, 0, 0)),
            pl.BlockSpec((None, None, lw, 2, C_DHEAD), lambda b, s_, p_: (layer, b, 0, 0, 0)),
            pl.BlockSpec((1, C_WIDTH), lambda b, s_, p_: (0, 0)),
        ] + [blk_spec(k) for k in range(SEL_TOPK)],
        out_specs=[
            pl.BlockSpec((1, 1, C_WIDTH), lambda b, s_, p_: (b, 0, 0)),
            pl.BlockSpec((None, keep, 2, C_DHEAD), lambda b, s_, p_: (b, 0, 0, 0)),
        ],
    )
    return pl.pallas_call(
        functools.partial(_nsa_attend_kernel, q_pos=q_pos),
        grid_spec=grid_spec,
        out_shape=[jax.ShapeDtypeStruct((bs, 1, C_WIDTH), F32), jax.ShapeDtypeStruct((bs, keep, 2, C_DHEAD), F32)],
        compiler_params=_cparams(("arbitrary",)),
        name="nsa_attend_sample",
    )(sel, page_table, proj3, proj3, proj3, proj3, ocmp, cache_win, out_g.reshape(1, C_WIDTH),
      *([cache] * SEL_TOPK))


def _pack_w_tail(w):
    n_in = w.shape[1]
    dt0, cq0, ckv0, gate0 = COL_MAIN, COL_MAIN + B_HEADS, COL_MAIN + B_HEADS + C_WIDTH, n_in - 3 * C_HEADS
    used = (n_in - COL_MAIN)
    return jnp.concatenate([w[:, cq0:ckv0], w[:, ckv0:gate0], w[:, dt0:cq0], w[:, gate0:],
                            jnp.zeros((D_MODEL, PROJ_W - COL_MAIN - used), w.dtype)], axis=1).astype(BF16)


def _token_mixer_tail(xa, xb, oa, ob, oc, w_out_b, ffn_g, wq_t, subkeys, u_tab, v_tab, tm, tn, et, emit_bf16=False,
                      layer=0):
    h, hn_t = _outproj_call(xa, xb, oa, ob, oc, w_out_b, ffn_g, tm)
    q_t = _peer_q_call(wq_t, hn_t, tm)
    route = _peer_route_call(q_t, subkeys)
    return h, _peer_expert_call(hn_t, route, u_tab, v_tab, tn, et, emit_bf16, layer)


def kernel(x_prompt, x_sample, cache_diff_kv, cache_nsa_kv, cache_nsa_win, state_ssm, state_conv, page_table,
           norm_mix_g, w_in, w_out, diff_lam, diff_subln_g, ssm_conv_w, ssm_conv_b, ssm_dt_bias, ssm_a_log,
           ssm_d, ssm_norm_g, nsa_pe, nsa_cmp_w1, nsa_cmp_w2, nsa_out_g, norm_ffn_g, peer_wq, peer_subkeys,
           peer_u, peer_v, norm_final_g):
    depth = w_in.shape[0]
    bp, t, _ = x_prompt.shape
    bs = x_sample.shape[0]
    past_len = page_table.shape[1] * PAGE_SIZE
    n_p = bp * t
    n_s = LANES
    tm_p = 512

    xa_p, xb_p = x_prompt.reshape(n_p, D_MODEL), None
    xa_s = jnp.pad(x_sample.reshape(bs, D_MODEL), ((0, n_s - bs), (0, 0)))
    xb_s = None
    st_p = [[] for _ in range(5)]
    st_s = [[] for _ in range(5)]
    state_ssm4 = state_ssm.reshape(depth, bs, B_HEADS * B_HEADDIM, B_STATE)
    for l in range(depth):
        lam_init = 0.8 - 0.6 * math.exp(-0.3 * l)
        w_tail = _pack_w_tail(w_in[l])
        w_out_b = w_out[l].astype(BF16)
        wq_t = peer_wq[l].T.astype(BF16)

        proj_s, dkv_s, nkv_s = _proj_call(xa_s, xb_s, norm_mix_g[l], w_in, l, w_tail, n_s)
        proj_s = proj_s[:bs]
        proj_s3 = proj_s.reshape(bs, 1, PROJ_W)
        o_a = _diff_sample_call(page_table, proj_s3, cache_diff_kv, l, diff_lam[l], diff_subln_g[l], lam_init)
        o_b, h_new, conv_new = _ssd_sample_call(proj_s3, state_conv, state_ssm4, l, ssm_conv_w[l], ssm_conv_b[l],
                                                ssm_dt_bias[l], ssm_a_log[l], ssm_d[l], ssm_norm_g[l])
        part = _compress_paged_call(page_table, cache_nsa_kv, l, nsa_pe[l], nsa_cmp_w1[l])
        o_cmp, sel = _nsa_select_call(part, nsa_cmp_w2[l], proj_s3, past_len)
        o_c, win_out = _nsa_attend_call(sel[:, 0, :SEL_TOPK], page_table, proj_s3, o_cmp, cache_nsa_kv, cache_nsa_win, l,
                                        nsa_out_g[l], past_len)
        st_s[0].append(dkv_s[:bs].reshape(bs, 1, 2, A_HEADS, A_DHEAD))
        st_s[1].append(nkv_s[:bs].reshape(bs, 1, NSA_ROWS, C_DHEAD))
        st_s[2].append(win_out)
        st_s[3].append(h_new.reshape(bs, B_HEADS, B_HEADDIM, B_STATE))
        st_s[4].append(conv_new)
        pad = lambda a: jnp.pad(a.reshape(bs, -1), ((0, n_s - bs), (0, 0)))
        xa_s, (xb_s, u_b, vt_b) = _token_mixer_tail(xa_s, xb_s, pad(o_a), pad(o_b), pad(o_c), w_out_b, norm_ffn_g[l],
                                                   wq_t, peer_subkeys[l], peer_u, peer_v, n_s, n_s, 512, True, l)

        proj, dkv_p, nkv_p = _proj_call(xa_p, xb_p, norm_mix_g[l], w_in, l, w_tail, tm_p)
        o_a = _diff_prompt_call(proj, bp, t, diff_lam[l], diff_subln_g[l], lam_init)
        o_b, h_ssm = _ssd_prompt_call(proj, bp, t, ssm_conv_w[l], ssm_conv_b[l], ssm_dt_bias[l], ssm_a_log[l],
                                      ssm_d[l], ssm_norm_g[l])
        kc, vc = _compress_prompt_call(proj, bp, t, nsa_pe[l], nsa_cmp_w1[l], nsa_cmp_w2[l])
        o_c = _nsa_prompt_call(proj, kc, vc, bp, t, nsa_out_g[l])
        proj3 = proj.reshape(bp, t, PROJ_W)
        st_p[0].append(dkv_p.reshape(bp, t, 2, A_HEADS, A_DHEAD))
        st_p[1].append(nkv_p.reshape(bp, t, NSA_ROWS, C_DHEAD))
        keep = min(WINDOW, t)
        st_p[2].append(proj3[:, t - keep:, COL_WIN:COL_WIN + 2 * C_DHEAD].reshape(bp, keep, 2, C_DHEAD))
        st_p[3].append(h_ssm)
        st_p[4].append(proj3[:, t - (CONV_W - 1):, COL_XBC:COL_XBC + B_CONV_DIM])
        xa_p, xb_p = _token_mixer_tail(xa_p, xb_p, o_a, o_b, o_c, w_out_b, norm_ffn_g[l], wq_t, peer_subkeys[l],
                                       u_b, vt_b, tm_p, 512, 1024)

    y_p = _final_norm_call(xa_p, xb_p, norm_final_g, tm_p).reshape(bp, t, D_MODEL)
    y_s = _final_norm_call(xa_s, xb_s, norm_final_g, n_s)[:bs].reshape(bs, 1, D_MODEL)
    return (y_p, y_s) + tuple(jnp.stack(s) for s in st_p) + tuple(jnp.stack(s) for s in st_s)
```

```python
import functools
import math

import jax
import jax.numpy as jnp
from jax import lax
from jax.experimental import pallas as pl
from jax.experimental.pallas import tpu as pltpu

F32 = jnp.float32
BF16 = jnp.bfloat16

D_MODEL = 2048
A_HEADS = 4
A_HALF = 64
A_DHEAD = 128
A_WIDTH = 512
B_WIDTH = 1024
B_HEADDIM = 64
B_HEADS = 16
B_GROUPS = 4
B_STATE = 128
CONV_W = 4
B_CONV_DIM = 2048
SSD_CHUNK = 128
C_HEADS = 4
C_DHEAD = 128
C_WIDTH = 512
CMP_LEN = 32
CMP_STRIDE = 16
SEL_BLOCK = 64
SEL_TOPK = 16
WINDOW = 512
PEER_HEADS = 8
PEER_NKEYS = 128
PEER_TOPK = 16
PEER_DQ = 256
PAGE_SIZE = 128
NSA_ROWS = 4
NORM_EPS = 1e-6
NEG_INF = -1e30
FORCE_SCORE = 1e4

LANES = 128
VMEM_LIMIT = 56 * 1024 * 1024

COL_AQ = 0
COL_AK = 512
COL_AV = 1024
COL_Z = 1536
COL_XBC = 2560
COL_MAIN = 4608
COL_CQ = 4608
COL_CKV = 5120
COL_WIN = 5632
COL_SMALL = 5888
PROJ_W = 6144
GATE_LANE0 = B_HEADS

NT_DIMS = (((1,), (1,)), ((), ()))


def _cparams(sem, vmem=VMEM_LIMIT):
    return pltpu.CompilerParams(dimension_semantics=sem, vmem_limit_bytes=vmem)


def _gelu(x):
    return 0.5 * x * (1.0 + jnp.tanh(math.sqrt(2.0 / math.pi) * (x + 0.044715 * (x * x * x))))


def _silu(x):
    return x * (1.0 / (1.0 + jnp.exp(-x)))


def _softplus(x):
    return jnp.maximum(x, 0.0) + jnp.log(1.0 + jnp.exp(-jnp.abs(x)))


def _row_view(ref, rows):
    return ref.reshape(rows, ref.shape[-1])


def _alibi_slope(h):
    if isinstance(h, int):
        return 2.0 ** (-2.0 * (h + 1))
    return jnp.exp2(jnp.full((1, 1), -2.0, F32) * (h + 1).astype(F32))


def _proj_kernel(*refs, two, tc):
    if two:
        xa_ref, xb_ref, g_ref, wm_ref, wt_ref, o_ref, dkv_ref, nkv_ref, xn_ref = refs
    else:
        xa_ref, g_ref, wm_ref, wt_ref, o_ref, dkv_ref, nkv_ref, xn_ref = refs
    j = pl.program_id(1)
    n_main = COL_MAIN // tc

    @pl.when(j == 0)
    def _():
        x = xa_ref[...]
        if two:
            x = x + xb_ref[...]
        ms = jnp.mean(x * x, axis=-1, keepdims=True)
        xn_ref[...] = (x * lax.rsqrt(ms + NORM_EPS) * g_ref[...]).astype(BF16)

    def store_cache_rows(res, tile):
        for b in range(2 * A_HEADS):
            col = COL_AK + b * LANES
            if col // tc == tile:
                dkv_ref[:, b // A_HEADS, b % A_HEADS, :] = res[:, col % tc:col % tc + LANES]
        for b in range(NSA_ROWS):
            col = COL_CKV + b * LANES
            if col // tc == tile:
                nkv_ref[:, b, :] = res[:, col % tc:col % tc + LANES]

    state_tiles = sorted({(COL_AK + b * LANES) // tc for b in range(2 * A_HEADS)}
                         | {(COL_CKV + b * LANES) // tc for b in range(NSA_ROWS)})

    def finish(res, tiles):
        o_ref[...] = res
        for tile in tiles:
            if tile in state_tiles:
                pl.when(j == tile)(functools.partial(store_cache_rows, res, tile))

    @pl.when(j < n_main)
    def _():
        finish(jnp.dot(xn_ref[...], wm_ref[...], preferred_element_type=F32), range(n_main))

    @pl.when(j >= n_main)
    def _():
        finish(jnp.dot(xn_ref[...], wt_ref[...], preferred_element_type=F32), range(n_main, PROJ_W // tc))


def _cast_kernel(w_ref, o_ref):
    o_ref[...] = w_ref[...].astype(BF16)


def _cast_w_main_call(w_in, layer, tc=768):
    return pl.pallas_call(
        _cast_kernel,
        grid=(COL_MAIN // tc,),
        in_specs=[pl.BlockSpec((None, D_MODEL, tc), lambda j: (layer, 0, j))],
        out_specs=pl.BlockSpec((D_MODEL, tc), lambda j: (0, j)),
        out_shape=jax.ShapeDtypeStruct((D_MODEL, COL_MAIN), BF16),
        compiler_params=_cparams(("parallel",)),
        name="w_in_cast",
    )(w_in)


def _proj_call(xa, xb, g, w_main, w_tail, tm, tc=768):
    n = xa.shape[0]
    two = xb is not None
    n_main = COL_MAIN // tc
    assert COL_MAIN % tc == 0 and w_tail.shape[1] == PROJ_W - COL_MAIN
    xspec = pl.BlockSpec((tm, D_MODEL), lambda i, j: (i, 0))
    ins = [xa] + ([xb] if two else []) + [g.reshape(1, D_MODEL), w_main, w_tail]
    specs = [xspec] + ([xspec] if two else []) + [
        pl.BlockSpec((1, D_MODEL), lambda i, j: (0, 0)),
        pl.BlockSpec((D_MODEL, tc), lambda i, j: (0, jnp.minimum(j, n_main - 1))),
        pl.BlockSpec((D_MODEL, tc), lambda i, j: (0, jnp.maximum(j - n_main, 0))),
    ]
    return pl.pallas_call(
        functools.partial(_proj_kernel, two=two, tc=tc),
        grid=(n // tm, PROJ_W // tc),
        in_specs=specs,
        out_specs=[pl.BlockSpec((tm, tc), lambda i, j: (i, j)),
                   pl.BlockSpec((tm, 2, A_HEADS, A_DHEAD), lambda i, j: (i, 0, 0, 0)),
                   pl.BlockSpec((tm, NSA_ROWS, C_DHEAD), lambda i, j: (i, 0, 0))],
        out_shape=[jax.ShapeDtypeStruct((n, PROJ_W), F32), jax.ShapeDtypeStruct((n, 2, A_HEADS, A_DHEAD), F32),
                   jax.ShapeDtypeStruct((n, NSA_ROWS, C_DHEAD), F32)],
        scratch_shapes=[pltpu.VMEM((tm, D_MODEL), BF16)],
        compiler_params=_cparams(("parallel", "arbitrary")),
        name="in_proj",
    )(*ins)


def _diff_lambda(dl, lam_init):
    a = jnp.sum(dl[0:1] * dl[1:2], axis=-1, keepdims=True)
    b = jnp.sum(dl[2:3] * dl[3:4], axis=-1, keepdims=True)
    return jnp.exp(a) - jnp.exp(b) + lam_init


CAUSAL_LEVELS = 4


def _causal_prefixes(i, nq, tq, body):
    levels = min(CAUSAL_LEVELS, nq)
    per = nq // levels
    for lv in range(levels):
        pl.when(i // per == lv)(functools.partial(body, (lv + 1) * per * tq))


def _diff_prompt_kernel(q_ref, k_ref, v_ref, dl_ref, g_ref, o_ref, *, tq, lam_init):
    h = pl.program_id(1)
    i = pl.program_id(2)
    t = k_ref.shape[0]
    scale = A_HALF ** -0.5

    def body(nk):
        lam = _diff_lambda(dl_ref[...], lam_init)
        q = q_ref[...] * scale
        lane = lax.broadcasted_iota(jnp.int32, (1, A_DHEAD), 1)
        kb = k_ref[0:nk, :].astype(BF16)
        vb = v_ref[0:nk, :].astype(BF16)
        qpos = i * tq + lax.broadcasted_iota(jnp.int32, (tq, 1), 0)
        kpos = lax.broadcasted_iota(jnp.int32, (1, nk), 1)
        ok = qpos >= kpos
        key_bias = _alibi_slope(h) * kpos.astype(F32)

        def half_attention(c):
            qc = jnp.where((lane >= c * A_HALF) & (lane < (c + 1) * A_HALF), q, 0.0).astype(BF16)
            s = lax.dot_general(qc, kb, NT_DIMS, preferred_element_type=F32) + key_bias
            s = jnp.where(ok, s, NEG_INF)
            e = jnp.exp(s - jnp.max(s, axis=-1, keepdims=True))
            pv = jnp.dot(e.astype(BF16), vb, preferred_element_type=F32)
            return pv / jnp.sum(e, axis=-1, keepdims=True)

        o = half_attention(0) - lam * half_attention(1)
        ms = jnp.mean(o * o, axis=-1, keepdims=True)
        o_ref[...] = o * lax.rsqrt(ms + NORM_EPS) * g_ref[...] * (1.0 - lam_init)

    _causal_prefixes(i, t // tq, tq, body)


def _diff_prompt_call(proj, bsz, t, dl, subln_g, lam_init, tq=256):
    nq = t // tq
    cq, ck, cv = COL_AQ // A_DHEAD, COL_AK // A_DHEAD, COL_AV // A_DHEAD
    return pl.pallas_call(
        functools.partial(_diff_prompt_kernel, tq=tq, lam_init=lam_init),
        grid=(bsz, A_HEADS, nq),
        in_specs=[
            pl.BlockSpec((tq, A_DHEAD), lambda b, h, i: (b * nq + i, cq + h)),
            pl.BlockSpec((t, A_DHEAD), lambda b, h, i: (b, ck + h)),
            pl.BlockSpec((t, A_DHEAD), lambda b, h, i: (b, cv + h)),
            pl.BlockSpec((4, A_HALF), lambda b, h, i: (0, 0)),
            pl.BlockSpec((1, A_DHEAD), lambda b, h, i: (0, 0)),
        ],
        out_specs=pl.BlockSpec((tq, A_DHEAD), lambda b, h, i: (b * nq + i, h)),
        out_shape=jax.ShapeDtypeStruct((bsz * t, A_WIDTH), F32),
        compiler_params=_cparams(("parallel", "parallel", "arbitrary")),
        name="diff_attn_prompt",
    )(proj, proj, proj, dl, subln_g.reshape(1, A_DHEAD))


SSD_COLS = 512


def _ssd_prompt_kernel(*refs):
    nx, nz = B_CONV_DIM // SSD_COLS, B_WIDTH // SSD_COLS
    xbc_refs, z_refs = refs[:nx], refs[nx:nx + nz]
    sm_ref, cw_ref, cb_ref, dtb_ref, alog_ref, dsk_ref, g_ref, o_ref, hout_ref, buf_ref, h_ref = refs[nx + nz:]
    c = pl.program_id(1)
    cs = SSD_CHUNK

    @pl.when(c == 0)
    def _():
        buf_ref[0:8, :] = jnp.zeros((8, B_CONV_DIM), F32)
        h_ref[...] = jnp.zeros_like(h_ref)

    xbc = jnp.concatenate([r[...] for r in xbc_refs], axis=1)
    buf_ref[8:8 + cs, :] = xbc
    cw = cw_ref[...]
    conv = cb_ref[...] + cw[3:4] * xbc
    for j in range(1, CONV_W):
        conv = conv + cw[3 - j:4 - j] * buf_ref[8 - j:8 - j + cs, :]
    buf_ref[0:8, :] = xbc[cs - 8:cs, :]
    xc = _silu(conv)
    xs = xc[:, :B_WIDTH]

    dt = _softplus(sm_ref[...] + dtb_ref[...])
    a_neg = -jnp.exp(alog_ref[...])
    dta = dt * a_neg
    row = lax.broadcasted_iota(jnp.int32, (cs, cs), 0)
    col = lax.broadcasted_iota(jnp.int32, (cs, cs), 1)
    causal = row >= col
    acum = jnp.dot(causal.astype(F32), dta, preferred_element_type=F32, precision=lax.Precision.HIGHEST)
    acum_t = acum.T
    dt_t = dt.T
    lane = lax.broadcasted_iota(jnp.int32, (1, LANES), 1)
    lo = lane < B_HEADDIM

    ys = []
    for g in range(B_GROUPS):
        bg = xc[:, B_WIDTH + g * B_STATE:B_WIDTH + (g + 1) * B_STATE]
        cg = xc[:, B_WIDTH + B_GROUPS * B_STATE + g * B_STATE:B_WIDTH + B_GROUPS * B_STATE + (g + 1) * B_STATE]
        bgb = bg.astype(BF16)
        cgb = cg.astype(BF16)
        cb = lax.dot_general(cgb, bgb, NT_DIMS, preferred_element_type=F32)
        for pr in range(2):
            h0 = g * 4 + pr * 2
            xpair = xs[:, h0 * B_HEADDIM:(h0 + 2) * B_HEADDIM]
            xpb = xpair.astype(BF16)
            ydiag = []
            ecol = []
            wcol = []
            elast = []
            for hh in (h0, h0 + 1):
                a_col = acum[:, hh:hh + 1]
                a_row = acum_t[hh:hh + 1, :]
                decay = jnp.exp(jnp.where(causal, a_col - a_row, NEG_INF))
                lm = cb * decay * dt_t[hh:hh + 1, :]
                ydiag.append(jnp.dot(lm.astype(BF16), xpb, preferred_element_type=F32))
                a_last = acum[cs - 1:cs, hh:hh + 1]
                ecol.append(jnp.exp(a_col))
                wcol.append(jnp.exp(a_last - a_col) * dt[:, hh:hh + 1])
                elast.append(jnp.exp(a_last))
            hp = h_ref[h0 * B_HEADDIM:(h0 + 2) * B_HEADDIM, :]
            yoff = lax.dot_general(cgb, hp.astype(BF16), NT_DIMS, preferred_element_type=F32)
            y = jnp.where(lo, ydiag[0], ydiag[1]) + yoff * jnp.where(lo, ecol[0], ecol[1])
            ys.append(y)
            wx = xpair * jnp.where(lo, wcol[0], wcol[1])
            upd = jnp.dot(wx.T.astype(BF16), bgb, preferred_element_type=F32)
            prow = lax.broadcasted_iota(jnp.int32, (LANES, 1), 0) < B_HEADDIM
            h_ref[h0 * B_HEADDIM:(h0 + 2) * B_HEADDIM, :] = jnp.where(prow, elast[0], elast[1]) * hp + upd

    y = jnp.concatenate(ys, axis=1)
    y = y + dsk_ref[...] * xs
    y = y * _silu(jnp.concatenate([r[...] for r in z_refs], axis=1))
    ms = jnp.mean(y * y, axis=-1, keepdims=True)
    o_ref[...] = y * lax.rsqrt(ms + NORM_EPS) * g_ref[...]

    @pl.when(c == pl.num_programs(1) - 1)
    def _():
        hout_ref[0] = h_ref[...]


def _pad_lanes(v, fill=0.0):
    v = v.reshape(1, -1).astype(F32)
    return jnp.pad(v, ((0, 0), (0, LANES - v.shape[1])), constant_values=fill)


def _ssd_prompt_call(proj, bsz, t, conv_w, conv_b, dt_bias, a_log, d_skip, norm_g):
    nc = t // SSD_CHUNK
    cs = SSD_CHUNK
    const = lambda b, c: (0, 0)
    o, hout = pl.pallas_call(
        _ssd_prompt_kernel,
        grid=(bsz, nc),
        in_specs=[
            *[pl.BlockSpec((cs, SSD_COLS), functools.partial(lambda k, b, c: (b * nc + c, COL_XBC // SSD_COLS + k), k))
              for k in range(B_CONV_DIM // SSD_COLS)],
            *[pl.BlockSpec((cs, SSD_COLS), functools.partial(lambda k, b, c: (b * nc + c, COL_Z // SSD_COLS + k), k))
              for k in range(B_WIDTH // SSD_COLS)],
            pl.BlockSpec((cs, LANES), lambda b, c: (b * nc + c, COL_SMALL // LANES)),
            pl.BlockSpec((CONV_W, B_CONV_DIM), const),
            pl.BlockSpec((1, B_CONV_DIM), const),
            pl.BlockSpec((1, LANES), const),
            pl.BlockSpec((1, LANES), const),
            pl.BlockSpec((1, B_WIDTH), const),
            pl.BlockSpec((1, B_WIDTH), const),
        ],
        out_specs=[
            pl.BlockSpec((cs, B_WIDTH), lambda b, c: (b * nc + c, 0)),
            pl.BlockSpec((1, B_HEADS * B_HEADDIM, B_STATE), lambda b, c: (b, 0, 0)),
        ],
        out_shape=[
            jax.ShapeDtypeStruct((bsz * t, B_WIDTH), F32),
            jax.ShapeDtypeStruct((bsz, B_HEADS * B_HEADDIM, B_STATE), F32),
        ],
        scratch_shapes=[pltpu.VMEM((8 + cs, B_CONV_DIM), F32), pltpu.VMEM((B_HEADS * B_HEADDIM, B_STATE), F32)],
        compiler_params=_cparams(("parallel", "arbitrary")),
        name="ssd_prompt",
    )(*([proj] * (B_CONV_DIM // SSD_COLS + B_WIDTH // SSD_COLS + 1)), conv_w, conv_b.reshape(1, -1),
      _pad_lanes(dt_bias), _pad_lanes(a_log), jnp.repeat(d_skip, B_HEADDIM).reshape(1, B_WIDTH),
      norm_g.reshape(1, B_WIDTH))
    return o, hout.reshape(bsz, B_HEADS, B_HEADDIM, B_STATE)


def _compress_kernel(k_ref, v_ref, pe_ref, w1_ref, w2_ref, kc_ref, vc_ref, *, nchunk):
    outs = []
    for kv, rows_ref in enumerate((k_ref, v_ref)):
        acc_lo = jnp.zeros((nchunk, C_DHEAD), F32)
        acc_hi = jnp.zeros((nchunk, C_DHEAD), F32)
        for r in range(CMP_STRIDE):
            x = rows_ref[pl.ds(r, nchunk, stride=CMP_STRIDE), :]
            x_lo = (x + pe_ref[kv, r:r + 1, :]).astype(BF16)
            x_hi = (x + pe_ref[kv, CMP_STRIDE + r:CMP_STRIDE + r + 1, :]).astype(BF16)
            acc_lo = acc_lo + jnp.dot(x_lo, w1_ref[kv, r].astype(BF16), preferred_element_type=F32)
            acc_hi = acc_hi + jnp.dot(x_hi, w1_ref[kv, CMP_STRIDE + r].astype(BF16), preferred_element_type=F32)
        hid = acc_lo + pltpu.roll(acc_hi, nchunk - 1, 0)
        outs.append(jnp.dot(_gelu(hid).astype(BF16), w2_ref[kv].astype(BF16), preferred_element_type=F32))
    kc_ref[0] = outs[0]
    vc_ref[0] = outs[1]


def _compress_prompt_call(proj, bsz, t, pe, w1, w2):
    nchunk = t // CMP_STRIDE
    shp = jax.ShapeDtypeStruct((bsz, nchunk, C_DHEAD), F32)
    return pl.pallas_call(
        functools.partial(_compress_kernel, nchunk=nchunk),
        grid=(bsz,),
        in_specs=[
            pl.BlockSpec((t, C_DHEAD), lambda b: (b, COL_CKV // C_DHEAD)),
            pl.BlockSpec((t, C_DHEAD), lambda b: (b, COL_CKV // C_DHEAD + 1)),
            pl.BlockSpec((2, CMP_LEN, C_DHEAD), lambda b: (0, 0, 0)),
            pl.BlockSpec((2, CMP_LEN, C_DHEAD, C_DHEAD), lambda b: (0, 0, 0, 0)),
            pl.BlockSpec((2, C_DHEAD, C_DHEAD), lambda b: (0, 0, 0)),
        ],
        out_specs=[pl.BlockSpec((1, nchunk, C_DHEAD), lambda b: (b, 0, 0))] * 2,
        out_shape=[shp, shp],
        compiler_params=_cparams(("parallel",)),
        name="nsa_compress_prompt",
    )(proj, proj, pe, w1.reshape(2, CMP_LEN, C_DHEAD, C_DHEAD), w2)


def _masked_softmax(s, ok):
    s = jnp.where(ok, s, NEG_INF)
    m = jnp.max(s, axis=-1, keepdims=True)
    e = jnp.where(ok, jnp.exp(s - m), 0.0)
    return e, jnp.sum(e, axis=-1, keepdims=True)


def _topk_mask_lanes(score, k, n):
    lane = lax.broadcasted_iota(jnp.int32, (1, LANES), 1)
    rank = jnp.zeros(score.shape, F32)
    for i in range(n):
        ci = score[:, i:i + 1]
        beats = (ci > score) | ((ci == score) & (lane > i))
        rank = rank + jnp.where(beats, 1.0, 0.0)
    return (rank < k) & (lane < n)


def _nsa_prompt_kernel(q_ref, ks_ref, vs_ref, kw_ref, vw_ref, kc_ref, vc_ref, sm_ref, g_ref, o_ref, *, tq, n_cmp):
    i = pl.program_id(1)
    t = ks_ref.shape[0]
    n_sel = t // SEL_BLOCK
    scale = C_DHEAD ** -0.5
    sel_shift = SEL_BLOCK.bit_length() - 1
    wlen = min(t, WINDOW + tq)

    def body(nk):
        qpos = i * tq + lax.broadcasted_iota(jnp.int32, (tq, 1), 0)
        lane = lax.broadcasted_iota(jnp.int32, (1, LANES), 1)

        cmp_end = lane * CMP_STRIDE + (CMP_LEN - 1)
        dist_c = qpos - cmp_end
        ok_c = (dist_c >= 0) & (lane < n_cmp)
        dist_cf = dist_c.astype(F32)
        kcb = kc_ref[0].astype(BF16)
        vcb = vc_ref[0].astype(BF16)
        qs = [(q_ref[:, h * C_DHEAD:(h + 1) * C_DHEAD] * scale).astype(BF16) for h in range(C_HEADS)]
        o_cmp = []
        psum = jnp.zeros((tq, LANES), F32)
        for h in range(C_HEADS):
            s = lax.dot_general(qs[h], kcb, NT_DIMS, preferred_element_type=F32)
            s = s - _alibi_slope(h) * dist_cf
            e, den = _masked_softmax(s, ok_c)
            p = e / jnp.maximum(den, 1e-30)
            psum = psum + p
            o_cmp.append(jnp.dot(p.astype(BF16), vcb, preferred_element_type=F32))

        n_i = lax.broadcasted_iota(jnp.int32, (LANES, LANES), 0)
        j_i = lax.broadcasted_iota(jnp.int32, (LANES, LANES), 1)
        lo_ = jnp.maximum(n_i * CMP_STRIDE, j_i * SEL_BLOCK)
        hi_ = jnp.minimum(n_i * CMP_STRIDE + CMP_LEN, (j_i + 1) * SEL_BLOCK)
        ovl = jnp.maximum(hi_ - lo_, 0).astype(F32) * (1.0 / CMP_LEN)
        ovl = jnp.where((n_i < n_cmp) & (j_i < n_sel), ovl, 0.0)
        imp = jnp.dot(psum, ovl, preferred_element_type=F32, precision=lax.Precision.HIGHEST)
        qblk = qpos >> sel_shift
        sel_valid = lane <= qblk
        forced = (lane == 0) | (lane == qblk) | (lane == qblk - 1)
        score = jnp.where(sel_valid, imp + jnp.where(forced, FORCE_SCORE, 0.0), NEG_INF)
        score = jnp.where(lane < n_sel, score, -jnp.inf)
        chosen = _topk_mask_lanes(score, min(SEL_TOPK, n_sel), n_sel) & sel_valid
        e_j = lax.broadcasted_iota(jnp.int32, (LANES, nk), 0)
        e_k = lax.broadcasted_iota(jnp.int32, (LANES, nk), 1)
        expand = jnp.where((e_k >> sel_shift) == e_j, 1.0, 0.0).astype(BF16)
        key_sel = jnp.dot(jnp.where(chosen, 1.0, 0.0).astype(BF16), expand, preferred_element_type=F32) > 0.5
        kpos = lax.broadcasted_iota(jnp.int32, (1, nk), 1)
        kpos_f = kpos.astype(F32)
        ok_s = key_sel & (qpos >= kpos)
        w0 = pl.multiple_of(jnp.clip(i * tq - WINDOW, 0, t - wlen), 8)
        wpos = w0 + lax.broadcasted_iota(jnp.int32, (1, wlen), 1)
        wpos_f = wpos.astype(F32)
        dist_w = qpos - wpos
        ok_w = (dist_w >= 0) & (dist_w < WINDOW)

        def attend(qh, kb, vb, key_bias, ok):
            s = lax.dot_general(qh, kb, NT_DIMS, preferred_element_type=F32) + key_bias
            s = jnp.where(ok, s, NEG_INF)
            e = jnp.exp(s - jnp.max(s, axis=-1, keepdims=True))
            return jnp.dot(e.astype(BF16), vb, preferred_element_type=F32) / jnp.sum(e, axis=-1, keepdims=True)

        ksb = ks_ref[0:nk, :].astype(BF16)
        vsb = vs_ref[0:nk, :].astype(BF16)
        kwb = kw_ref[pl.ds(w0, wlen), :].astype(BF16)
        vwb = vw_ref[pl.ds(w0, wlen), :].astype(BF16)
        gate = 1.0 / (1.0 + jnp.exp(-sm_ref[...]))
        outs = []
        for h in range(C_HEADS):
            o_sel = attend(qs[h], ksb, vsb, _alibi_slope(h) * kpos_f, ok_s)
            o_win = attend(qs[h], kwb, vwb, _alibi_slope(h) * wpos_f, ok_w)
            g0 = gate[:, GATE_LANE0 + h:GATE_LANE0 + h + 1]
            g1 = gate[:, GATE_LANE0 + C_HEADS + h:GATE_LANE0 + C_HEADS + h + 1]
            g2 = gate[:, GATE_LANE0 + 2 * C_HEADS + h:GATE_LANE0 + 2 * C_HEADS + h + 1]
            outs.append(g0 * o_cmp[h] + g1 * o_sel + g2 * o_win)
        o = jnp.concatenate(outs, axis=1)
        ms = jnp.mean(o * o, axis=-1, keepdims=True)
        o_ref[...] = o * lax.rsqrt(ms + NORM_EPS) * g_ref[...]

    body(t)


def _nsa_prompt_call(proj, kc, vc, bsz, t, out_g, tq=256):
    nq = t // tq
    n_cmp = (t - CMP_LEN) // CMP_STRIDE + 1
    c0 = COL_CKV // C_DHEAD
    w0 = COL_WIN // C_DHEAD
    kvspec = lambda col: pl.BlockSpec((t, C_DHEAD), lambda b, i: (b, col))
    return pl.pallas_call(
        functools.partial(_nsa_prompt_kernel, tq=tq, n_cmp=n_cmp),
        grid=(bsz, nq),
        in_specs=[
            pl.BlockSpec((tq, C_WIDTH), lambda b, i: (b * nq + i, COL_CQ // C_WIDTH)),
            kvspec(c0 + 2), kvspec(c0 + 3), kvspec(w0), kvspec(w0 + 1),
            pl.BlockSpec((1, kc.shape[1], C_DHEAD), lambda b, i: (b, 0, 0)),
            pl.BlockSpec((1, kc.shape[1], C_DHEAD), lambda b, i: (b, 0, 0)),
            pl.BlockSpec((tq, LANES), lambda b, i: (b * nq + i, COL_SMALL // LANES)),
            pl.BlockSpec((1, C_WIDTH), lambda b, i: (0, 0)),
        ],
        out_specs=pl.BlockSpec((tq, C_WIDTH), lambda b, i: (b * nq + i, 0)),
        out_shape=jax.ShapeDtypeStruct((bsz * t, C_WIDTH), F32),
        compiler_params=_cparams(("parallel", "arbitrary")),
        name="nsa_attn_prompt",
    )(proj, proj, proj, proj, proj, kc, vc, proj, out_g.reshape(1, C_WIDTH))


def _outproj_kernel(*refs, two):
    if two:
        xa_ref, xb_ref, oa_ref, ob_ref, oc_ref, w_ref, g_ref, h_ref, hnt_ref = refs
    else:
        xa_ref, oa_ref, ob_ref, oc_ref, w_ref, g_ref, h_ref, hnt_ref = refs
    x = xa_ref[...]
    if two:
        x = x + xb_ref[...]
    mixed = jnp.dot(oa_ref[...].astype(BF16), w_ref[0:A_WIDTH, :], preferred_element_type=F32)
    mixed = mixed + jnp.dot(ob_ref[...].astype(BF16), w_ref[A_WIDTH:A_WIDTH + B_WIDTH, :], preferred_element_type=F32)
    mixed = mixed + jnp.dot(oc_ref[...].astype(BF16), w_ref[A_WIDTH + B_WIDTH:, :], preferred_element_type=F32)
    h = x + mixed
    h_ref[...] = h
    ms = jnp.mean(h * h, axis=-1, keepdims=True)
    hnt_ref[...] = (h * lax.rsqrt(ms + NORM_EPS) * g_ref[...]).T.astype(BF16)


def _outproj_call(xa, xb, oa, ob, oc, w_out_b, g, tm):
    n = xa.shape[0]
    two = xb is not None
    row = lambda w: pl.BlockSpec((tm, w), lambda i: (i, 0))
    ins = [xa] + ([xb] if two else []) + [oa, ob, oc, w_out_b, g.reshape(1, D_MODEL)]
    specs = [row(D_MODEL)] + ([row(D_MODEL)] if two else []) + [
        row(A_WIDTH), row(B_WIDTH), row(C_WIDTH),
        pl.BlockSpec((D_MODEL, D_MODEL), lambda i: (0, 0)),
        pl.BlockSpec((1, D_MODEL), lambda i: (0, 0)),
    ]
    return pl.pallas_call(
        functools.partial(_outproj_kernel, two=two),
        grid=(n // tm,),
        in_specs=specs,
        out_specs=[row(D_MODEL), pl.BlockSpec((D_MODEL, tm), lambda i: (0, i))],
        out_shape=[jax.ShapeDtypeStruct((n, D_MODEL), F32), jax.ShapeDtypeStruct((D_MODEL, n), BF16)],
        compiler_params=_cparams(("parallel",)),
        name="out_proj",
    )(*ins)


def _peer_q_kernel(wqt_ref, hnt_ref, qt_ref):
    qt_ref[...] = jnp.dot(wqt_ref[...], hnt_ref[...], preferred_element_type=F32)


def _peer_q_call(wq_t, hn_t, tm):
    n = hn_t.shape[1]
    dq = wq_t.shape[0]
    return pl.pallas_call(
        _peer_q_kernel,
        grid=(n // tm,),
        in_specs=[pl.BlockSpec((dq, D_MODEL), lambda i: (0, 0)), pl.BlockSpec((D_MODEL, tm), lambda i: (0, i))],
        out_specs=pl.BlockSpec((dq, tm), lambda i: (0, i)),
        out_shape=jax.ShapeDtypeStruct((dq, n), F32),
        compiler_params=_cparams(("parallel",)),
        name="peer_query",
    )(wq_t, hn_t)


NOT_RANKED = 99.0


def _top_rows(s, pos, k, want_rank):
    rank = jnp.full(s.shape, NOT_RANKED, F32) if want_rank else None
    vals, picks = [], []
    for j in range(k):
        m = jnp.max(s, axis=0, keepdims=True)
        idx = jnp.min(jnp.where(s == m, pos, 1e9), axis=0, keepdims=True)
        hit = pos == idx
        if want_rank:
            rank = jnp.where(hit, float(j), rank)
        s = jnp.where(hit, -jnp.inf, s)
        vals.append(m)
        picks.append(idx)
    return jnp.concatenate(vals, axis=0), jnp.concatenate(picks, axis=0), rank


PAIR_ROWS = PEER_TOPK + 7 * 8 + 8


def _pair_candidates(v1, v2):
    tn = v1.shape[1]
    parts = [v1[0:1, :] + v2] + [v1[a:a + 1, :] + v2[0:8, :] for a in range(1, 8)] + [v1[8:16, :] + v2[0:1, :]]
    r = lax.broadcasted_iota(jnp.int32, (PAIR_ROWS, tn), 0)
    mid = r - PEER_TOPK
    pos = jnp.where(r < PEER_TOPK, r,
                    jnp.where(r < PEER_TOPK + 56, ((mid >> 3) + 1) * PEER_TOPK + (mid & 7), (r - 64) * PEER_TOPK))
    return jnp.concatenate(parts, axis=0), pos.astype(F32)


def _peer_route_kernel(qt_ref, sk_ref, lim_ref, coef_ref, rank2_ref, e2_ref, *, heads):
    half = PEER_DQ // 2
    row = lax.broadcasted_iota(jnp.int32, (PEER_NKEYS, qt_ref.shape[1]), 0).astype(F32)
    for hh in range(heads):
        q = qt_ref[hh * PEER_DQ:(hh + 1) * PEER_DQ, :]
        s1 = jnp.dot(sk_ref[hh, 0], q[0:half, :], preferred_element_type=F32, precision=lax.Precision.HIGHEST)
        s2 = jnp.dot(sk_ref[hh, 1], q[half:, :], preferred_element_type=F32, precision=lax.Precision.HIGHEST)
        v1, _, rank1 = _top_rows(s1, row, PEER_TOPK, True)
        v2, _, rank2 = _top_rows(s2, row, PEER_TOPK, True)
        cand, cpos = _pair_candidates(v1, v2)
        top, pos, _ = _top_rows(cand, cpos, PEER_TOPK, False)
        z = jnp.sum(jnp.exp(top - top[0:1, :]), axis=0, keepdims=True)
        a_of = jnp.floor(pos * (1.0 / PEER_TOPK))
        lim = jnp.zeros(s1.shape, F32)
        for a in range(PEER_TOPK):
            cnt = jnp.sum(jnp.where(a_of == float(a), 1.0, 0.0), axis=0, keepdims=True)
            lim = jnp.where(rank1 == float(a), cnt, lim)
        lim_ref[hh] = lim
        coef_ref[hh] = jnp.exp(s1 - v1[0:1, :]) / z
        rank2_ref[hh] = rank2.astype(BF16)
        e2_ref[hh] = jnp.exp(s2 - v2[0:1, :]).astype(BF16)


def _peer_route_call(q_t, subkeys, tn=LANES, heads=4):
    n = q_t.shape[1]
    shp = lambda dt: jax.ShapeDtypeStruct((PEER_HEADS, PEER_NKEYS, n), dt)
    ospec = pl.BlockSpec((heads, PEER_NKEYS, tn), lambda j, h: (h, 0, j))
    return pl.pallas_call(
        functools.partial(_peer_route_kernel, heads=heads),
        grid=(n // tn, PEER_HEADS // heads),
        in_specs=[
            pl.BlockSpec((heads * PEER_DQ, tn), lambda j, h: (h, j)),
            pl.BlockSpec((heads, 2, PEER_NKEYS, PEER_DQ // 2), lambda j, h: (h, 0, 0, 0)),
        ],
        out_specs=[ospec] * 4,
        out_shape=[shp(F32), shp(F32), shp(BF16), shp(BF16)],
        compiler_params=_cparams(("parallel", "arbitrary")),
        name="peer_route",
    )(q_t, subkeys)


def _peer_expert_kernel(hnt_ref, lim_ref, coef_ref, rank2_ref, e2_ref, u_ref, v_ref, o_ref, *rest, et, emit):
    acc_ref = rest[-1]
    t = pl.program_id(1)

    @pl.when(t == 0)
    def _():
        acc_ref[...] = jnp.zeros_like(acc_ref)

    if emit:
        ub = u_ref[...].astype(BF16)
        vtb = v_ref[...].T.astype(BF16)
        rest[0][...] = ub
        rest[1][...] = vtb
    else:
        ub = u_ref[...]
        vtb = v_ref[...]
    tn = hnt_ref.shape[1]
    hid = jnp.dot(ub, hnt_ref[...], preferred_element_type=F32)
    acts = []
    for ii in range(et // PEER_NKEYS):
        i1 = t * (et // PEER_NKEYS) + ii
        gate = jnp.zeros((PEER_NKEYS, tn), BF16)
        for h in range(PEER_HEADS):
            lim = lim_ref[h, pl.ds(i1, 1), :].astype(BF16)
            coef = coef_ref[h, pl.ds(i1, 1), :].astype(BF16)
            gate = gate + jnp.where(rank2_ref[h] < lim, e2_ref[h], jnp.zeros((), BF16)) * coef
        acts.append(gate * _gelu(hid[ii * PEER_NKEYS:(ii + 1) * PEER_NKEYS, :]).astype(BF16))
    acc_ref[...] += jnp.dot(vtb, jnp.concatenate(acts, axis=0), preferred_element_type=F32)

    @pl.when(t == pl.num_programs(1) - 1)
    def _():
        o_ref[...] = acc_ref[...].T


def _peer_expert_call(hn_t, route, u_tab, v_tab, tn, et, emit=False, layer=0):
    n = hn_t.shape[1]
    n_exp = u_tab.shape[-2] if emit else u_tab.shape[0]
    rspec = pl.BlockSpec((PEER_HEADS, PEER_NKEYS, tn), lambda j, t: (0, 0, j))
    uspec = pl.BlockSpec((et, D_MODEL), lambda j, t: (t, 0))
    vtspec = pl.BlockSpec((D_MODEL, et), lambda j, t: (0, t))
    ospec = pl.BlockSpec((tn, D_MODEL), lambda j, t: (j, 0))
    oshape = jax.ShapeDtypeStruct((n, D_MODEL), F32)
    if emit:
        assert n == tn
        out_specs = [ospec, uspec, vtspec]
        out_shape = [oshape, jax.ShapeDtypeStruct((n_exp, D_MODEL), BF16), jax.ShapeDtypeStruct((D_MODEL, n_exp), BF16)]
    else:
        out_specs, out_shape = ospec, oshape
    return pl.pallas_call(
        functools.partial(_peer_expert_kernel, et=et, emit=emit),
        grid=(n // tn, n_exp // et),
        in_specs=[pl.BlockSpec((D_MODEL, tn), lambda j, t: (0, j)), rspec, rspec, rspec, rspec]
        + ([pl.BlockSpec((None, et, D_MODEL), lambda j, t: (layer, t, 0))] * 2 if emit else [uspec, vtspec]),
        out_specs=out_specs,
        out_shape=out_shape,
        scratch_shapes=[pltpu.VMEM((D_MODEL, tn), F32)],
        compiler_params=_cparams(("parallel", "arbitrary")),
        name="peer_experts",
    )(hn_t, *route, u_tab, v_tab)


def _final_norm_kernel(xa_ref, xb_ref, g_ref, o_ref):
    x = xa_ref[...] + xb_ref[...]
    ms = jnp.mean(x * x, axis=-1, keepdims=True)
    o_ref[...] = x * lax.rsqrt(ms + NORM_EPS) * g_ref[...]


def _final_norm_call(xa, xb, g, tm):
    n = xa.shape[0]
    row = pl.BlockSpec((tm, D_MODEL), lambda i: (i, 0))
    return pl.pallas_call(
        _final_norm_kernel,
        grid=(n // tm,),
        in_specs=[row, row, pl.BlockSpec((1, D_MODEL), lambda i: (0, 0))],
        out_specs=row,
        out_shape=jax.ShapeDtypeStruct((n, D_MODEL), F32),
        compiler_params=_cparams(("parallel",)),
        name="final_norm",
    )(xa, xb, g.reshape(1, D_MODEL))


DIFF_PAGES = 8


def _diff_sample_kernel(pt_ref, q_ref, knew_ref, vnew_ref, dl_ref, g_ref, *rest, past_len, lam_init):
    page_refs = [_row_view(r, PAGE_SIZE * 2 * A_HEADS) for r in rest[:DIFF_PAGES]]
    o_ref, m_ref, l_ref, acc_ref = rest[DIFF_PAGES:]
    p = pl.program_id(1)
    scale = A_HALF ** -0.5
    nrow = 2 * A_HEADS
    per_key = 2 * A_HEADS

    @pl.when(p == 0)
    def _():
        m_ref[...] = jnp.full(m_ref.shape, NEG_INF, F32)
        l_ref[...] = jnp.zeros(l_ref.shape, F32)
        acc_ref[...] = jnp.zeros(acc_ref.shape, F32)

    row = lax.broadcasted_iota(jnp.int32, (nrow, 1), 0)
    lane = lax.broadcasted_iota(jnp.int32, (1, A_DHEAD), 1)
    slope = jnp.exp2(-2.0 * ((row >> 1) + 1).astype(F32))
    q = q_ref[0]
    q2 = [jnp.where(((row >> 1) == h) & ((lane >= A_HALF) == ((row & 1) == 1)), q[:, h * A_DHEAD:(h + 1) * A_DHEAD], 0.0)
          for h in range(A_HEADS)]
    q2b = [x.astype(BF16) for x in q2]
    ss = []
    for g in range(DIFF_PAGES):
        sg = None
        for h in range(A_HEADS):
            kh = page_refs[g][pl.ds(h, PAGE_SIZE, stride=per_key), :].astype(BF16)
            d = lax.dot_general(q2b[h], kh, NT_DIMS, preferred_element_type=F32)
            sg = d if sg is None else sg + d
        ss.append(sg)
    s = jnp.concatenate(ss, axis=1) * scale
    nk = DIFF_PAGES * PAGE_SIZE
    kpos = p * nk + lax.broadcasted_iota(jnp.int32, (1, nk), 1)
    s = s - slope * (past_len - kpos).astype(F32)
    m_old = m_ref[:, 0:1]
    m_new = jnp.maximum(m_old, jnp.max(s, axis=-1, keepdims=True))
    alpha = jnp.exp(m_old - m_new)
    e = jnp.exp(s - m_new)
    eb = e.astype(BF16)
    l_new = alpha * l_ref[:, 0:1] + jnp.sum(e, axis=-1, keepdims=True)
    pv = jnp.zeros((nrow, A_DHEAD), F32)
    for h in range(A_HEADS):
        vh = jnp.concatenate([page_refs[g][pl.ds(A_HEADS + h, PAGE_SIZE, stride=per_key), :].astype(BF16)
                              for g in range(DIFF_PAGES)], axis=0)
        pv = pv + jnp.where((row >> 1) == h, jnp.dot(eb, vh, preferred_element_type=F32), 0.0)
    acc = alpha * acc_ref[...] + pv
    m_ref[...] = jnp.broadcast_to(m_new, m_ref.shape)
    l_ref[...] = jnp.broadcast_to(l_new, l_ref.shape)
    acc_ref[...] = acc

    @pl.when(p == pl.num_programs(1) - 1)
    def _():
        knew = knew_ref[0]
        vnew = vnew_ref[0]
        s_n = jnp.zeros((nrow, 1), F32)
        for h in range(A_HEADS):
            s_n = s_n + jnp.sum(q2[h] * knew[:, h * A_DHEAD:(h + 1) * A_DHEAD], axis=-1, keepdims=True)
        s_n = s_n * scale
        v8 = jnp.concatenate([vnew[:, (r // 2) * A_DHEAD:(r // 2 + 1) * A_DHEAD] for r in range(nrow)], axis=0)
        m_f = jnp.maximum(m_new, s_n)
        a_f = jnp.exp(m_new - m_f)
        e_n = jnp.exp(s_n - m_f)
        o8 = (a_f * acc + e_n * v8) / (a_f * l_new + e_n)
        lam = _diff_lambda(dl_ref[...], lam_init)
        outs = []
        for h in range(A_HEADS):
            oh = o8[2 * h:2 * h + 1, :] - lam * o8[2 * h + 1:2 * h + 2, :]
            ms = jnp.mean(oh * oh, axis=-1, keepdims=True)
            outs.append(oh * lax.rsqrt(ms + NORM_EPS) * g_ref[...] * (1.0 - lam_init))
        o_ref[0] = jnp.concatenate(outs, axis=1)


def _diff_sample_call(page_table, proj3, cache, layer, dl, subln_g, lam_init):
    bs, n_pages = page_table.shape
    past_len = n_pages * PAGE_SIZE
    steps = n_pages // DIFF_PAGES

    def page_spec(g):
        return pl.BlockSpec((None, None, PAGE_SIZE, 2, A_HEADS, A_DHEAD),
                            lambda b, p, pt: (layer, pt[b, p * DIFF_PAGES + g], 0, 0, 0, 0))

    grid_spec = pltpu.PrefetchScalarGridSpec(
        num_scalar_prefetch=1,
        grid=(bs, steps),
        in_specs=[
            pl.BlockSpec((1, 1, A_WIDTH), lambda b, p, pt: (b, 0, COL_AQ // A_WIDTH)),
            pl.BlockSpec((1, 1, A_WIDTH), lambda b, p, pt: (b, 0, COL_AK // A_WIDTH)),
            pl.BlockSpec((1, 1, A_WIDTH), lambda b, p, pt: (b, 0, COL_AV // A_WIDTH)),
            pl.BlockSpec((4, A_HALF), lambda b, p, pt: (0, 0)),
            pl.BlockSpec((1, A_DHEAD), lambda b, p, pt: (0, 0)),
        ] + [page_spec(g) for g in range(DIFF_PAGES)],
        out_specs=pl.BlockSpec((1, 1, A_WIDTH), lambda b, p, pt: (b, 0, 0)),
        scratch_shapes=[pltpu.VMEM((2 * A_HEADS, LANES), F32), pltpu.VMEM((2 * A_HEADS, LANES), F32),
                        pltpu.VMEM((2 * A_HEADS, A_DHEAD), F32)],
    )
    return pl.pallas_call(
        functools.partial(_diff_sample_kernel, past_len=past_len, lam_init=lam_init),
        grid_spec=grid_spec,
        out_shape=jax.ShapeDtypeStruct((bs, 1, A_WIDTH), F32),
        compiler_params=_cparams(("parallel", "arbitrary")),
        name="diff_attn_sample",
    )(page_table, proj3, proj3, proj3, dl, subln_g.reshape(1, A_DHEAD), *([cache] * DIFF_PAGES))


def _diag_rows(vec):
    n = vec.shape[1]
    r = lax.broadcasted_iota(jnp.int32, (n, n), 0)
    c = lax.broadcasted_iota(jnp.int32, (n, n), 1)
    return jnp.where(r == c, vec, 0.0)


def _ssd_sample_kernel(*refs):
    nx, nz = B_CONV_DIM // SSD_COLS, B_WIDTH // SSD_COLS
    xbc_refs, z_refs = refs[:nx], refs[nx:nx + nz]
    (sm_ref, cbuf_ref, h0_ref, cw_ref, cb_ref, dtb_ref, alog_ref, dsk_ref, g_ref,
     o_ref, hout_ref, cout_ref) = refs[nx + nz:]
    hi = lax.Precision.HIGHEST
    new = jnp.concatenate([r[0] for r in xbc_refs], axis=1)
    buf = cbuf_ref[0, 0]
    cw = cw_ref[...]
    conv = cb_ref[...] + cw[CONV_W - 1:CONV_W] * new
    for i in range(CONV_W - 1):
        conv = conv + cw[i:i + 1] * buf[i:i + 1]
    cout_ref[0] = jnp.concatenate([buf[1:CONV_W - 1], new], axis=0)
    xc = _silu(conv)
    xs = xc[:, :B_WIDTH]
    dt = _softplus(sm_ref[0] + dtb_ref[...])
    ea = jnp.exp(dt * (-jnp.exp(alog_ref[...])))
    hr = lax.broadcasted_iota(jnp.int32, (LANES, B_WIDTH), 0)
    hc = lax.broadcasted_iota(jnp.int32, (LANES, B_WIDTH), 1)
    rep = jnp.where((hc // B_HEADDIM) == hr, 1.0, 0.0)
    both = jnp.concatenate([dt, ea, jnp.zeros((6, LANES), F32)], axis=0)
    both_rep = jnp.dot(both, rep, preferred_element_type=F32, precision=hi)
    u = both_rep[0:1] * xs
    ea_rep = both_rep[1:2]
    gn = B_GROUPS * B_STATE
    rows = (B_HEADS // B_GROUPS) * B_HEADDIM
    ys = []
    for g in range(B_GROUPS):
        r0 = g * rows
        bg = xc[:, B_WIDTH + g * B_STATE:B_WIDTH + (g + 1) * B_STATE]
        cg = xc[:, B_WIDTH + gn + g * B_STATE:B_WIDTH + gn + (g + 1) * B_STATE]
        h0 = h0_ref[0, 0, r0:r0 + rows, :]
        hn = jnp.dot(_diag_rows(ea_rep[:, r0:r0 + rows]), h0, preferred_element_type=F32, precision=hi)
        hn = hn + jnp.dot(_diag_rows(u[:, r0:r0 + rows]), jnp.broadcast_to(bg, (rows, B_STATE)),
                          preferred_element_type=F32, precision=hi)
        hout_ref[0, r0:r0 + rows, :] = hn
        c8 = jnp.broadcast_to(cg, (8, B_STATE)).astype(BF16)
        ys.append(lax.dot_general(c8, hn.astype(BF16), NT_DIMS, preferred_element_type=F32)[0:1])
    y = jnp.concatenate(ys, axis=1) + dsk_ref[...] * xs
    y = y * _silu(jnp.concatenate([r[0] for r in z_refs], axis=1))
    ms = jnp.mean(y * y, axis=-1, keepdims=True)
    o_ref[0] = y * lax.rsqrt(ms + NORM_EPS) * g_ref[...]


def _ssd_sample_call(proj3, state_conv, state_ssm4, layer, conv_w, conv_b, dt_bias, a_log, d_skip, norm_g):
    bs = proj3.shape[0]
    const = lambda b: (0, 0)
    nrow = B_HEADS * B_HEADDIM
    return pl.pallas_call(
        _ssd_sample_kernel,
        grid=(bs,),
        in_specs=[
            *[pl.BlockSpec((1, 1, SSD_COLS), functools.partial(lambda k, b: (b, 0, COL_XBC // SSD_COLS + k), k))
              for k in range(B_CONV_DIM // SSD_COLS)],
            *[pl.BlockSpec((1, 1, SSD_COLS), functools.partial(lambda k, b: (b, 0, COL_Z // SSD_COLS + k), k))
              for k in range(B_WIDTH // SSD_COLS)],
            pl.BlockSpec((1, 1, LANES), lambda b: (b, 0, COL_SMALL // LANES)),
            pl.BlockSpec((1, 1, CONV_W - 1, B_CONV_DIM), lambda b: (layer, b, 0, 0)),
            pl.BlockSpec((1, 1, nrow, B_STATE), lambda b: (layer, b, 0, 0)),
            pl.BlockSpec((CONV_W, B_CONV_DIM), const),
            pl.BlockSpec((1, B_CONV_DIM), const),
            pl.BlockSpec((1, LANES), const),
            pl.BlockSpec((1, LANES), const),
            pl.BlockSpec((1, B_WIDTH), const),
            pl.BlockSpec((1, B_WIDTH), const),
        ],
        out_specs=[
            pl.BlockSpec((1, 1, B_WIDTH), lambda b: (b, 0, 0)),
            pl.BlockSpec((1, nrow, B_STATE), lambda b: (b, 0, 0)),
            pl.BlockSpec((1, CONV_W - 1, B_CONV_DIM), lambda b: (b, 0, 0)),
        ],
        out_shape=[
            jax.ShapeDtypeStruct((bs, 1, B_WIDTH), F32),
            jax.ShapeDtypeStruct((bs, nrow, B_STATE), F32),
            jax.ShapeDtypeStruct((bs, CONV_W - 1, B_CONV_DIM), F32),
        ],
        compiler_params=_cparams(("parallel",)),
        name="ssd_sample",
    )(*([proj3] * (B_CONV_DIM // SSD_COLS + B_WIDTH // SSD_COLS + 1)), state_conv, state_ssm4, conv_w,
      conv_b.reshape(1, -1), _pad_lanes(dt_bias),
      _pad_lanes(a_log), jnp.repeat(d_skip, B_HEADDIM).reshape(1, B_WIDTH), norm_g.reshape(1, B_WIDTH))


CMP_PAGES = 32


def _compress_paged_kernel(pt_ref, pe_ref, w1_ref, *rest):
    page_refs = [_row_view(r, PAGE_SIZE * NSA_ROWS) for r in rest[:CMP_PAGES]]
    o_ref = rest[CMP_PAGES]
    per_page = PAGE_SIZE // CMP_STRIDE
    outs = []
    for kv in range(2):
        acc_lo = jnp.zeros((CMP_PAGES * per_page, C_DHEAD), F32)
        acc_hi = jnp.zeros((CMP_PAGES * per_page, C_DHEAD), F32)
        for r in range(CMP_STRIDE):
            x = jnp.concatenate([page_refs[g][pl.ds(NSA_ROWS * r + kv, per_page, stride=NSA_ROWS * CMP_STRIDE), :]
                                 for g in range(CMP_PAGES)], axis=0)
            x_lo = (x + pe_ref[kv, r:r + 1, :]).astype(BF16)
            x_hi = (x + pe_ref[kv, CMP_STRIDE + r:CMP_STRIDE + r + 1, :]).astype(BF16)
            acc_lo = acc_lo + jnp.dot(x_lo, w1_ref[kv, r].astype(BF16), preferred_element_type=F32)
            acc_hi = acc_hi + jnp.dot(x_hi, w1_ref[kv, CMP_STRIDE + r].astype(BF16), preferred_element_type=F32)
        outs += [acc_lo, acc_hi]
    o_ref[0] = jnp.concatenate(outs, axis=1)


def _compress_paged_call(page_table, cache, layer, pe, w1):
    bs, n_pages = page_table.shape
    steps = n_pages // CMP_PAGES
    per_page = PAGE_SIZE // CMP_STRIDE

    def page_spec(g):
        return pl.BlockSpec((None, None, PAGE_SIZE, NSA_ROWS, C_DHEAD),
                            lambda b, p, pt: (layer, pt[b, p * CMP_PAGES + g], 0, 0, 0))

    grid_spec = pltpu.PrefetchScalarGridSpec(
        num_scalar_prefetch=1,
        grid=(bs, steps),
        in_specs=[
            pl.BlockSpec((2, CMP_LEN, C_DHEAD), lambda b, p, pt: (0, 0, 0)),
            pl.BlockSpec((2, CMP_LEN, C_DHEAD, C_DHEAD), lambda b, p, pt: (0, 0, 0, 0)),
        ] + [page_spec(g) for g in range(CMP_PAGES)],
        out_specs=pl.BlockSpec((1, CMP_PAGES * per_page, 4 * C_DHEAD), lambda b, p, pt: (b, p, 0)),
    )
    return pl.pallas_call(
        _compress_paged_kernel,
        grid_spec=grid_spec,
        out_shape=jax.ShapeDtypeStruct((bs, n_pages * per_page, 4 * C_DHEAD), F32),
        compiler_params=_cparams(("parallel", "arbitrary")),
        name="nsa_compress_sample",
    )(page_table, pe, w1.reshape(2, CMP_LEN, C_DHEAD, C_DHEAD), *([cache] * CMP_PAGES))


def _heads_to_rows(q):
    rows = [q[:, h * C_DHEAD:(h + 1) * C_DHEAD] for h in range(C_HEADS)]
    return jnp.concatenate(rows + [jnp.zeros((8 - C_HEADS, C_DHEAD), F32)], axis=0)


SEL_LANES = 384


def _nsa_select_kernel(part_ref, w2_ref, q_ref, ocmp_ref, sel_ref, *, q_pos):
    nchunk = part_ref.shape[1]
    n_cmp = (q_pos + 1 - CMP_LEN) // CMP_STRIDE + 1
    n_sel = -(-(q_pos + 1) // SEL_BLOCK)
    scale = C_DHEAD ** -0.5
    part = part_ref[0]
    kv_cmp = []
    for kv in range(2):
        lo = part[:, (2 * kv) * C_DHEAD:(2 * kv + 1) * C_DHEAD]
        hi = part[:, (2 * kv + 1) * C_DHEAD:(2 * kv + 2) * C_DHEAD]
        hid = lo + pltpu.roll(hi, nchunk - 1, 0)
        kv_cmp.append(jnp.dot(_gelu(hid).astype(BF16), w2_ref[kv].astype(BF16), preferred_element_type=F32).astype(BF16))
    q8 = _heads_to_rows(q_ref[0]).astype(BF16)
    row = lax.broadcasted_iota(jnp.int32, (8, 1), 0)
    slope = jnp.exp2(-2.0 * (row + 1).astype(F32))
    n_i = lax.broadcasted_iota(jnp.int32, (1, nchunk), 1)
    dist_c = q_pos - (n_i * CMP_STRIDE + CMP_LEN - 1)
    ok = (dist_c >= 0) & (n_i < n_cmp)
    s = lax.dot_general(q8, kv_cmp[0], NT_DIMS, preferred_element_type=F32) * scale - slope * dist_c.astype(F32)
    e, den = _masked_softmax(s, ok)
    p = jnp.where(row < C_HEADS, e / jnp.maximum(den, 1e-30), 0.0)
    ocmp_ref[0] = jnp.dot(p.astype(BF16), kv_cmp[1], preferred_element_type=F32)
    psum = jnp.broadcast_to(jnp.sum(p, axis=0, keepdims=True), (8, nchunk))
    c_i = lax.broadcasted_iota(jnp.int32, (nchunk, SEL_LANES), 0)
    j_i = lax.broadcasted_iota(jnp.int32, (nchunk, SEL_LANES), 1)
    lo_ = jnp.maximum(c_i * CMP_STRIDE, j_i * SEL_BLOCK)
    hi_ = jnp.minimum(c_i * CMP_STRIDE + CMP_LEN, (j_i + 1) * SEL_BLOCK)
    ovl = jnp.where((c_i < n_cmp) & (j_i < n_sel), jnp.maximum(hi_ - lo_, 0).astype(F32) * (1.0 / CMP_LEN), 0.0)
    imp = jnp.dot(psum, ovl, preferred_element_type=F32, precision=lax.Precision.HIGHEST)[0:1]
    lane = lax.broadcasted_iota(jnp.int32, (1, SEL_LANES), 1)
    qblk = q_pos // SEL_BLOCK
    forced = (lane == 0) | (lane == qblk) | (lane == qblk - 1)
    score = jnp.where(lane <= qblk, imp + jnp.where(forced, FORCE_SCORE, 0.0), NEG_INF)
    score = jnp.where(lane < n_sel, score, -jnp.inf)
    lane_f = lane.astype(F32)
    out_lane = lax.broadcasted_iota(jnp.int32, (1, LANES), 1)
    sel = jnp.full((1, LANES), -1.0, F32)
    for k in range(min(SEL_TOPK, n_sel)):
        m = jnp.max(score, axis=-1, keepdims=True)
        idx = jnp.min(jnp.where(score == m, lane_f, 1e9), axis=-1, keepdims=True)
        sel = jnp.where(out_lane == k, jnp.where(m > NEG_INF / 2, idx, -1.0), sel)
        score = jnp.where(lane_f == idx, -jnp.inf, score)
    sel_ref[0] = sel.astype(jnp.int32)


def _nsa_select_call(part, w2, q3, q_pos):
    bs, nchunk, _ = part.shape
    return pl.pallas_call(
        functools.partial(_nsa_select_kernel, q_pos=q_pos),
        grid=(bs,),
        in_specs=[
            pl.BlockSpec((1, nchunk, 4 * C_DHEAD), lambda b: (b, 0, 0)),
            pl.BlockSpec((2, C_DHEAD, C_DHEAD), lambda b: (0, 0, 0)),
            pl.BlockSpec((1, 1, C_WIDTH), lambda b: (b, 0, COL_CQ // C_WIDTH)),
        ],
        out_specs=[pl.BlockSpec((1, 8, C_DHEAD), lambda b: (b, 0, 0)), pl.BlockSpec((1, 1, LANES), lambda b: (b, 0, 0))],
        out_shape=[jax.ShapeDtypeStruct((bs, 8, C_DHEAD), F32), jax.ShapeDtypeStruct((bs, 1, LANES), jnp.int32)],
        compiler_params=_cparams(("parallel",)),
        name="nsa_select_sample",
    )(part, w2, q3)


def _nsa_attend_kernel(sel_ref, pt_ref, q_ref, new_ref, wnew_ref, sm_ref, ocmp_ref, win_ref, g_ref, *rest, q_pos):
    k_eff = SEL_TOPK
    blk_refs = [_row_view(r, SEL_BLOCK * NSA_ROWS) for r in rest[:k_eff]]
    o_ref, wout_ref = rest[k_eff:]
    b = pl.program_id(0)
    scale = C_DHEAD ** -0.5
    n_past_blocks = q_pos // SEL_BLOCK
    q8f = _heads_to_rows(q_ref[0])
    q8 = q8f.astype(BF16)
    row = lax.broadcasted_iota(jnp.int32, (8, 1), 0)
    slope = jnp.exp2(-2.0 * (row + 1).astype(F32))
    lane64 = lax.broadcasted_iota(jnp.int32, (1, SEL_BLOCK), 1)

    ss, vs, oks = [], [], []
    new_sel = jnp.zeros((1, 1), jnp.int32)
    for k in range(k_eff):
        j = sel_ref[b, k]
        k_sel = blk_refs[k][pl.ds(2, SEL_BLOCK, stride=NSA_ROWS), :]
        v_sel = blk_refs[k][pl.ds(3, SEL_BLOCK, stride=NSA_ROWS), :]
        s = lax.dot_general(q8, k_sel.astype(BF16), NT_DIMS, preferred_element_type=F32)
        dist = q_pos - (j * SEL_BLOCK + lane64)
        ss.append(s * scale - slope * dist.astype(F32))
        oks.append(lane64 * 0 + jnp.where((j >= 0) & (j < n_past_blocks), 1, 0))
        vs.append(v_sel.astype(BF16))
        new_sel = new_sel + jnp.where(j == n_past_blocks, 1, 0)
    s = jnp.concatenate(ss, axis=1)
    ok = jnp.concatenate(oks, axis=1) > 0
    new = new_ref[0]
    s_n = jnp.sum(q8f * new[:, 2 * C_DHEAD:3 * C_DHEAD], axis=-1, keepdims=True) * scale
    s_n = jnp.where(new_sel > 0, s_n, NEG_INF)
    s = jnp.where(ok, s, NEG_INF)
    m = jnp.maximum(jnp.max(s, axis=-1, keepdims=True), s_n)
    e = jnp.where(ok, jnp.exp(s - m), 0.0)
    e_n = jnp.where(new_sel > 0, jnp.exp(s_n - m), 0.0)
    den = jnp.sum(e, axis=-1, keepdims=True) + e_n
    o_sel = (jnp.dot(e.astype(BF16), jnp.concatenate(vs, axis=0), preferred_element_type=F32)
             + e_n * new[:, 3 * C_DHEAD:]) / den

    lw = win_ref.shape[0]
    win_k = win_ref[:, 0, :]
    win_v = win_ref[:, 1, :]
    wnew = wnew_ref[0]
    wpos = lax.broadcasted_iota(jnp.int32, (1, lw), 1)
    dist_w = lw - wpos
    ok_w = dist_w < WINDOW
    s = lax.dot_general(q8, win_k.astype(BF16), NT_DIMS, preferred_element_type=F32) * scale
    s = jnp.where(ok_w, s - slope * dist_w.astype(F32), NEG_INF)
    s_n = jnp.sum(q8f * wnew[:, :C_DHEAD], axis=-1, keepdims=True) * scale
    m = jnp.maximum(jnp.max(s, axis=-1, keepdims=True), s_n)
    e = jnp.where(ok_w, jnp.exp(s - m), 0.0)
    e_n = jnp.exp(s_n - m)
    den = jnp.sum(e, axis=-1, keepdims=True) + e_n
    o_win = (jnp.dot(e.astype(BF16), win_v.astype(BF16), preferred_element_type=F32)
             + e_n * wnew[:, C_DHEAD:]) / den
    keep = min(WINDOW, lw + 1)
    wout_ref[0:keep - 1, :, :] = win_ref[lw + 1 - keep:lw, :, :]
    wout_ref[keep - 1:keep, 0, :] = wnew[:, :C_DHEAD]
    wout_ref[keep - 1:keep, 1, :] = wnew[:, C_DHEAD:]

    gate = 1.0 / (1.0 + jnp.exp(-sm_ref[0]))
    outs = []
    for h in range(C_HEADS):
        g0 = gate[:, GATE_LANE0 + h:GATE_LANE0 + h + 1]
        g1 = gate[:, GATE_LANE0 + C_HEADS + h:GATE_LANE0 + C_HEADS + h + 1]
        g2 = gate[:, GATE_LANE0 + 2 * C_HEADS + h:GATE_LANE0 + 2 * C_HEADS + h + 1]
        outs.append(g0 * ocmp_ref[0, h:h + 1, :] + g1 * o_sel[h:h + 1, :] + g2 * o_win[h:h + 1, :])
    o = jnp.concatenate(outs, axis=1)
    ms = jnp.mean(o * o, axis=-1, keepdims=True)
    o_ref[0] = o * lax.rsqrt(ms + NORM_EPS) * g_ref[...]


def _nsa_attend_call(sel, page_table, proj3, ocmp, cache, cache_win, layer, out_g, q_pos):
    bs = proj3.shape[0]
    lw = cache_win.shape[2]
    keep = min(WINDOW, lw + 1)
    n_pages = page_table.shape[1]
    halves = PAGE_SIZE // SEL_BLOCK

    def blk_spec(k):
        def imap(b, sel_r, pt_r):
            j = jnp.clip(sel_r[b, k], 0, n_pages * halves - 1)
            return (layer, pt_r[b, j // halves], j % halves, 0, 0)
        return pl.BlockSpec((None, None, SEL_BLOCK, NSA_ROWS, C_DHEAD), imap)

    row3 = lambda w, col: pl.BlockSpec((1, 1, w), lambda b, s_, p_: (b, 0, col))
    grid_spec = pltpu.PrefetchScalarGridSpec(
        num_scalar_prefetch=2,
        grid=(bs,),
        in_specs=[
            row3(C_WIDTH, COL_CQ // C_WIDTH),
            row3(4 * C_DHEAD, COL_CKV // (4 * C_DHEAD)),
            row3(2 * C_DHEAD, COL_WIN // (2 * C_DHEAD)),
            row3(LANES, COL_SMALL // LANES),
            pl.BlockSpec((1, 8, C_DHEAD), lambda b, s_, p_: (b, 0, 0)),
            pl.BlockSpec((None, None, lw, 2, C_DHEAD), lambda b, s_, p_: (layer, b, 0, 0, 0)),
            pl.BlockSpec((1, C_WIDTH), lambda b, s_, p_: (0, 0)),
        ] + [blk_spec(k) for k in range(SEL_TOPK)],
        out_specs=[
            pl.BlockSpec((1, 1, C_WIDTH), lambda b, s_, p_: (b, 0, 0)),
            pl.BlockSpec((None, keep, 2, C_DHEAD), lambda b, s_, p_: (b, 0, 0, 0)),
        ],
    )
    return pl.pallas_call(
        functools.partial(_nsa_attend_kernel, q_pos=q_pos),
        grid_spec=grid_spec,
        out_shape=[jax.ShapeDtypeStruct((bs, 1, C_WIDTH), F32), jax.ShapeDtypeStruct((bs, keep, 2, C_DHEAD), F32)],
        compiler_params=_cparams(("arbitrary",)),
        name="nsa_attend_sample",
    )(sel, page_table, proj3, proj3, proj3, proj3, ocmp, cache_win, out_g.reshape(1, C_WIDTH),
      *([cache] * SEL_TOPK))


def _pack_w_tail(w):
    n_in = w.shape[1]
    dt0, cq0, ckv0, gate0 = COL_MAIN, COL_MAIN + B_HEADS, COL_MAIN + B_HEADS + C_WIDTH, n_in - 3 * C_HEADS
    used = (n_in - COL_MAIN)
    return jnp.concatenate([w[:, cq0:ckv0], w[:, ckv0:gate0], w[:, dt0:cq0], w[:, gate0:],
                            jnp.zeros((D_MODEL, PROJ_W - COL_MAIN - used), w.dtype)], axis=1).astype(BF16)


def _token_mixer_tail(xa, xb, oa, ob, oc, w_out_b, ffn_g, wq_t, subkeys, u_tab, v_tab, tm, tn, et, emit_bf16=False,
                      layer=0):
    h, hn_t = _outproj_call(xa, xb, oa, ob, oc, w_out_b, ffn_g, tm)
    q_t = _peer_q_call(wq_t, hn_t, tm)
    route = _peer_route_call(q_t, subkeys)
    return h, _peer_expert_call(hn_t, route, u_tab, v_tab, tn, et, emit_bf16, layer)


def kernel(x_prompt, x_sample, cache_diff_kv, cache_nsa_kv, cache_nsa_win, state_ssm, state_conv, page_table,
           norm_mix_g, w_in, w_out, diff_lam, diff_subln_g, ssm_conv_w, ssm_conv_b, ssm_dt_bias, ssm_a_log,
           ssm_d, ssm_norm_g, nsa_pe, nsa_cmp_w1, nsa_cmp_w2, nsa_out_g, norm_ffn_g, peer_wq, peer_subkeys,
           peer_u, peer_v, norm_final_g):
    depth = w_in.shape[0]
    bp, t, _ = x_prompt.shape
    bs = x_sample.shape[0]
    past_len = page_table.shape[1] * PAGE_SIZE
    n_p = bp * t
    n_s = LANES
    tm_p = 512

    xa_p, xb_p = x_prompt.reshape(n_p, D_MODEL), None
    xa_s = jnp.pad(x_sample.reshape(bs, D_MODEL), ((0, n_s - bs), (0, 0)))
    xb_s = None
    st_p = [[] for _ in range(5)]
    st_s = [[] for _ in range(5)]
    state_ssm4 = state_ssm.reshape(depth, bs, B_HEADS * B_HEADDIM, B_STATE)
    for l in range(depth):
        lam_init = 0.8 - 0.6 * math.exp(-0.3 * l)
        w_main = _cast_w_main_call(w_in, l)
        w_tail = _pack_w_tail(w_in[l])
        w_out_b = w_out[l].astype(BF16)
        wq_t = peer_wq[l].T.astype(BF16)

        proj_s, dkv_s, nkv_s = _proj_call(xa_s, xb_s, norm_mix_g[l], w_main, w_tail, n_s)
        proj_s = proj_s[:bs]
        proj_s3 = proj_s.reshape(bs, 1, PROJ_W)
        o_a = _diff_sample_call(page_table, proj_s3, cache_diff_kv, l, diff_lam[l], diff_subln_g[l], lam_init)
        o_b, h_new, conv_new = _ssd_sample_call(proj_s3, state_conv, state_ssm4, l, ssm_conv_w[l], ssm_conv_b[l],
                                                ssm_dt_bias[l], ssm_a_log[l], ssm_d[l], ssm_norm_g[l])
        part = _compress_paged_call(page_table, cache_nsa_kv, l, nsa_pe[l], nsa_cmp_w1[l])
        o_cmp, sel = _nsa_select_call(part, nsa_cmp_w2[l], proj_s3, past_len)
        o_c, win_out = _nsa_attend_call(sel[:, 0, :SEL_TOPK], page_table, proj_s3, o_cmp, cache_nsa_kv, cache_nsa_win, l,
                                        nsa_out_g[l], past_len)
        st_s[0].append(dkv_s[:bs].reshape(bs, 1, 2, A_HEADS, A_DHEAD))
        st_s[1].append(nkv_s[:bs].reshape(bs, 1, NSA_ROWS, C_DHEAD))
        st_s[2].append(win_out)
        st_s[3].append(h_new.reshape(bs, B_HEADS, B_HEADDIM, B_STATE))
        st_s[4].append(conv_new)
        pad = lambda a: jnp.pad(a.reshape(bs, -1), ((0, n_s - bs), (0, 0)))
        xa_s, (xb_s, u_b, vt_b) = _token_mixer_tail(xa_s, xb_s, pad(o_a), pad(o_b), pad(o_c), w_out_b, norm_ffn_g[l],
                                                   wq_t, peer_subkeys[l], peer_u, peer_v, n_s, n_s, 512, True, l)

        proj, dkv_p, nkv_p = _proj_call(xa_p, xb_p, norm_mix_g[l], w_main, w_tail, tm_p)
        o_a = _diff_prompt_call(proj, bp, t, diff_lam[l], diff_subln_g[l], lam_init)
        o_b, h_ssm = _ssd_prompt_call(proj, bp, t, ssm_conv_w[l], ssm_conv_b[l], ssm_dt_bias[l], ssm_a_log[l],
                                      ssm_d[l], ssm_norm_g[l])
        kc, vc = _compress_prompt_call(proj, bp, t, nsa_pe[l], nsa_cmp_w1[l], nsa_cmp_w2[l])
        o_c = _nsa_prompt_call(proj, kc, vc, bp, t, nsa_out_g[l])
        proj3 = proj.reshape(bp, t, PROJ_W)
        st_p[0].append(dkv_p.reshape(bp, t, 2, A_HEADS, A_DHEAD))
        st_p[1].append(nkv_p.reshape(bp, t, NSA_ROWS, C_DHEAD))
        keep = min(WINDOW, t)
        st_p[2].append(proj3[:, t - keep:, COL_WIN:COL_WIN + 2 * C_DHEAD].reshape(bp, keep, 2, C_DHEAD))
        st_p[3].append(h_ssm)
        st_p[4].append(proj3[:, t - (CONV_W - 1):, COL_XBC:COL_XBC + B_CONV_DIM])
        xa_p, xb_p = _token_mixer_tail(xa_p, xb_p, o_a, o_b, o_c, w_out_b, norm_ffn_g[l], wq_t, peer_subkeys[l],
                                       u_b, vt_b, tm_p, 512, 1024)

    y_p = _final_norm_call(xa_p, xb_p, norm_final_g, tm_p).reshape(bp, t, D_MODEL)
    y_s = _final_norm_call(xa_s, xb_s, norm_final_g, n_s)[:bs].reshape(bs, 1, D_MODEL)
    return (y_p, y_s) + tuple(jnp.stack(s) for s in st_p) + tuple(jnp.stack(s) for s in st_s)
```

```python
import functools
import math

import jax
import jax.numpy as jnp
from jax import lax
from jax.experimental import pallas as pl
from jax.experimental.pallas import tpu as pltpu

F32 = jnp.float32
BF16 = jnp.bfloat16

D_MODEL = 2048
A_HEADS = 4
A_HALF = 64
A_DHEAD = 128
A_WIDTH = 512
B_WIDTH = 1024
B_HEADDIM = 64
B_HEADS = 16
B_GROUPS = 4
B_STATE = 128
CONV_W = 4
B_CONV_DIM = 2048
SSD_CHUNK = 128
C_HEADS = 4
C_DHEAD = 128
C_WIDTH = 512
CMP_LEN = 32
CMP_STRIDE = 16
SEL_BLOCK = 64
SEL_TOPK = 16
WINDOW = 512
PEER_HEADS = 8
PEER_NKEYS = 128
PEER_TOPK = 16
PEER_DQ = 256
PAGE_SIZE = 128
NSA_ROWS = 4
NORM_EPS = 1e-6
NEG_INF = -1e30
FORCE_SCORE = 1e4

LANES = 128
VMEM_LIMIT = 56 * 1024 * 1024

COL_AQ = 0
COL_AK = 512
COL_AV = 1024
COL_Z = 1536
COL_XBC = 2560
COL_MAIN = 4608
COL_CQ = 4608
COL_CKV = 5120
COL_WIN = 5632
COL_SMALL = 5888
PROJ_W = 6144
GATE_LANE0 = B_HEADS

NT_DIMS = (((1,), (1,)), ((), ()))


def _cparams(sem, vmem=VMEM_LIMIT):
    return pltpu.CompilerParams(dimension_semantics=sem, vmem_limit_bytes=vmem)


def _gelu(x):
    return 0.5 * x * (1.0 + jnp.tanh(math.sqrt(2.0 / math.pi) * (x + 0.044715 * (x * x * x))))


def _silu(x):
    return x * (1.0 / (1.0 + jnp.exp(-x)))


def _softplus(x):
    return jnp.maximum(x, 0.0) + jnp.log(1.0 + jnp.exp(-jnp.abs(x)))


def _row_view(ref, rows):
    return ref.reshape(rows, ref.shape[-1])


def _alibi_slope(h):
    if isinstance(h, int):
        return 2.0 ** (-2.0 * (h + 1))
    return jnp.exp2(jnp.full((1, 1), -2.0, F32) * (h + 1).astype(F32))


def _proj_kernel(*refs, two, tc):
    if two:
        xa_ref, xb_ref, g_ref, wm_ref, wt_ref, o_ref, dkv_ref, nkv_ref, xn_ref = refs
    else:
        xa_ref, g_ref, wm_ref, wt_ref, o_ref, dkv_ref, nkv_ref, xn_ref = refs
    j = pl.program_id(1)
    n_main = COL_MAIN // tc

    @pl.when(j == 0)
    def _():
        x = xa_ref[...]
        if two:
            x = x + xb_ref[...]
        ms = jnp.mean(x * x, axis=-1, keepdims=True)
        xn_ref[...] = (x * lax.rsqrt(ms + NORM_EPS) * g_ref[...]).astype(BF16)

    def store_cache_rows(res, tile):
        for b in range(2 * A_HEADS):
            col = COL_AK + b * LANES
            if col // tc == tile:
                dkv_ref[:, b // A_HEADS, b % A_HEADS, :] = res[:, col % tc:col % tc + LANES]
        for b in range(NSA_ROWS):
            col = COL_CKV + b * LANES
            if col // tc == tile:
                nkv_ref[:, b, :] = res[:, col % tc:col % tc + LANES]

    state_tiles = sorted({(COL_AK + b * LANES) // tc for b in range(2 * A_HEADS)}
                         | {(COL_CKV + b * LANES) // tc for b in range(NSA_ROWS)})

    def finish(res, tiles):
        o_ref[...] = res
        for tile in tiles:
            if tile in state_tiles:
                pl.when(j == tile)(functools.partial(store_cache_rows, res, tile))

    @pl.when(j < n_main)
    def _():
        finish(jnp.dot(xn_ref[...], wm_ref[...], preferred_element_type=F32), range(n_main))

    @pl.when(j >= n_main)
    def _():
        finish(jnp.dot(xn_ref[...], wt_ref[...], preferred_element_type=F32), range(n_main, PROJ_W // tc))


def _cast_t_kernel(w_ref, o_ref):
    for l in range(w_ref.shape[1]):
        o_ref[l] = w_ref[:, l, :].T.astype(BF16)


def _cast_w_main_call(w_in_t, tr=384):
    depth = w_in_t.shape[1]
    return pl.pallas_call(
        _cast_t_kernel,
        grid=(COL_MAIN // tr,),
        in_specs=[pl.BlockSpec((tr, depth, D_MODEL), lambda j: (j, 0, 0))],
        out_specs=pl.BlockSpec((depth, D_MODEL, tr), lambda j: (0, 0, j)),
        out_shape=jax.ShapeDtypeStruct((depth, D_MODEL, COL_MAIN), BF16),
        compiler_params=_cparams(("parallel",)),
        name="w_in_cast",
    )(w_in_t)


def _pack_tail_kernel(w_ref, o_ref, *, n_in):
    width, depth, tk = w_ref.shape
    dt_w, gate_w = B_HEADS, 3 * C_HEADS
    cq0 = dt_w
    ckv0 = cq0 + C_WIDTH
    gate0 = n_in - COL_MAIN - gate_w
    pad = jnp.zeros((PROJ_W - n_in, tk), F32)
    for l in range(depth):
        x = w_ref[:, l, :]
        packed = jnp.concatenate([x[cq0:ckv0], x[ckv0:gate0], x[0:dt_w], x[gate0:gate0 + gate_w], pad], axis=0)
        o_ref[l] = packed.T.astype(BF16)


def _pack_w_tail_call(w_in_t, tk=512):
    width = PROJ_W - COL_MAIN
    depth = w_in_t.shape[1]
    assert COL_MAIN % width == 0
    return pl.pallas_call(
        functools.partial(_pack_tail_kernel, n_in=w_in_t.shape[0]),
        grid=(D_MODEL // tk,),
        in_specs=[pl.BlockSpec((width, depth, tk), lambda k: (COL_MAIN // width, 0, k))],
        out_specs=pl.BlockSpec((depth, tk, width), lambda k: (0, k, 0)),
        out_shape=jax.ShapeDtypeStruct((depth, D_MODEL, width), BF16),
        compiler_params=_cparams(("parallel",)),
        name="w_in_tail_pack",
    )(w_in_t)


def _proj_call(xa, xb, g, w_main, w_tail, layer, tm):
    n = xa.shape[0]
    two = xb is not None
    tc = 768 if two else 1536
    n_main = COL_MAIN // tc
    assert COL_MAIN % tc == 0 and w_tail.shape[2] == PROJ_W - COL_MAIN
    xspec = pl.BlockSpec((tm, D_MODEL), lambda i, j: (i, 0))
    ins = [xa] + ([xb] if two else []) + [g.reshape(1, D_MODEL), w_main, w_tail]
    specs = [xspec] + ([xspec] if two else []) + [
        pl.BlockSpec((1, D_MODEL), lambda i, j: (0, 0)),
        pl.BlockSpec((None, D_MODEL, tc), lambda i, j: (layer, 0, jnp.minimum(j, n_main - 1))),
        pl.BlockSpec((None, D_MODEL, tc), lambda i, j: (layer, 0, jnp.maximum(j - n_main, 0))),
    ]
    return pl.pallas_call(
        functools.partial(_proj_kernel, two=two, tc=tc),
        grid=(n // tm, PROJ_W // tc),
        in_specs=specs,
        out_specs=[pl.BlockSpec((tm, tc), lambda i, j: (i, j)),
                   pl.BlockSpec((tm, 2, A_HEADS, A_DHEAD), lambda i, j: (i, 0, 0, 0)),
                   pl.BlockSpec((tm, NSA_ROWS, C_DHEAD), lambda i, j: (i, 0, 0))],
        out_shape=[jax.ShapeDtypeStruct((n, PROJ_W), F32), jax.ShapeDtypeStruct((n, 2, A_HEADS, A_DHEAD), F32),
                   jax.ShapeDtypeStruct((n, NSA_ROWS, C_DHEAD), F32)],
        scratch_shapes=[pltpu.VMEM((tm, D_MODEL), BF16)],
        compiler_params=_cparams(("parallel", "arbitrary")),
        name="in_proj",
    )(*ins)


def _diff_lambda(dl, lam_init):
    a = jnp.sum(dl[0:1] * dl[1:2], axis=-1, keepdims=True)
    b = jnp.sum(dl[2:3] * dl[3:4], axis=-1, keepdims=True)
    return jnp.exp(a) - jnp.exp(b) + lam_init


CAUSAL_LEVELS = 4


def _causal_prefixes(i, nq, tq, body):
    levels = min(CAUSAL_LEVELS, nq)
    per = nq // levels
    for lv in range(levels):
        pl.when(i // per == lv)(functools.partial(body, (lv + 1) * per * tq))


def _diff_prompt_kernel(q_ref, k_ref, v_ref, dl_ref, g_ref, o_ref, *, tq, lam_init):
    h = pl.program_id(1)
    i = pl.program_id(2)
    t = k_ref.shape[0]
    scale = A_HALF ** -0.5

    def body(nk):
        lam = _diff_lambda(dl_ref[...], lam_init)
        q = q_ref[...] * scale
        lane = lax.broadcasted_iota(jnp.int32, (1, A_DHEAD), 1)
        kb = k_ref[0:nk, :].astype(BF16)
        vb = v_ref[0:nk, :].astype(BF16)
        qpos = i * tq + lax.broadcasted_iota(jnp.int32, (tq, 1), 0)
        kpos = lax.broadcasted_iota(jnp.int32, (1, nk), 1)
        ok = qpos >= kpos
        key_bias = _alibi_slope(h) * kpos.astype(F32)

        def half_attention(c):
            qc = jnp.where((lane >= c * A_HALF) & (lane < (c + 1) * A_HALF), q, 0.0).astype(BF16)
            s = lax.dot_general(qc, kb, NT_DIMS, preferred_element_type=F32) + key_bias
            s = jnp.where(ok, s, NEG_INF)
            e = jnp.exp(s - jnp.max(s, axis=-1, keepdims=True))
            pv = jnp.dot(e.astype(BF16), vb, preferred_element_type=F32)
            return pv / jnp.sum(e, axis=-1, keepdims=True)

        o = half_attention(0) - lam * half_attention(1)
        ms = jnp.mean(o * o, axis=-1, keepdims=True)
        o_ref[...] = o * lax.rsqrt(ms + NORM_EPS) * g_ref[...] * (1.0 - lam_init)

    _causal_prefixes(i, t // tq, tq, body)


def _diff_prompt_call(proj, bsz, t, dl, subln_g, lam_init, tq=256):
    nq = t // tq
    cq, ck, cv = COL_AQ // A_DHEAD, COL_AK // A_DHEAD, COL_AV // A_DHEAD
    return pl.pallas_call(
        functools.partial(_diff_prompt_kernel, tq=tq, lam_init=lam_init),
        grid=(bsz, A_HEADS, nq),
        in_specs=[
            pl.BlockSpec((tq, A_DHEAD), lambda b, h, i: (b * nq + i, cq + h)),
            pl.BlockSpec((t, A_DHEAD), lambda b, h, i: (b, ck + h)),
            pl.BlockSpec((t, A_DHEAD), lambda b, h, i: (b, cv + h)),
            pl.BlockSpec((4, A_HALF), lambda b, h, i: (0, 0)),
            pl.BlockSpec((1, A_DHEAD), lambda b, h, i: (0, 0)),
        ],
        out_specs=pl.BlockSpec((tq, A_DHEAD), lambda b, h, i: (b * nq + i, h)),
        out_shape=jax.ShapeDtypeStruct((bsz * t, A_WIDTH), F32),
        compiler_params=_cparams(("parallel", "parallel", "arbitrary")),
        name="diff_attn_prompt",
    )(proj, proj, proj, dl, subln_g.reshape(1, A_DHEAD))


SSD_COLS = 512


def _ssd_prompt_kernel(*refs):
    nx, nz = B_CONV_DIM // SSD_COLS, B_WIDTH // SSD_COLS
    xbc_refs, z_refs = refs[:nx], refs[nx:nx + nz]
    sm_ref, cw_ref, cb_ref, dtb_ref, alog_ref, dsk_ref, g_ref, o_ref, hout_ref, buf_ref, h_ref = refs[nx + nz:]
    c = pl.program_id(1)
    cs = SSD_CHUNK

    @pl.when(c == 0)
    def _():
        buf_ref[0:8, :] = jnp.zeros((8, B_CONV_DIM), F32)
        h_ref[...] = jnp.zeros_like(h_ref)

    xbc = jnp.concatenate([r[...] for r in xbc_refs], axis=1)
    buf_ref[8:8 + cs, :] = xbc
    cw = cw_ref[...]
    conv = cb_ref[...] + cw[3:4] * xbc
    for j in range(1, CONV_W):
        conv = conv + cw[3 - j:4 - j] * buf_ref[8 - j:8 - j + cs, :]
    buf_ref[0:8, :] = xbc[cs - 8:cs, :]
    xc = _silu(conv)
    xs = xc[:, :B_WIDTH]

    dt = _softplus(sm_ref[...] + dtb_ref[...])
    a_neg = -jnp.exp(alog_ref[...])
    dta = dt * a_neg
    row = lax.broadcasted_iota(jnp.int32, (cs, cs), 0)
    col = lax.broadcasted_iota(jnp.int32, (cs, cs), 1)
    causal = row >= col
    acum = jnp.dot(causal.astype(F32), dta, preferred_element_type=F32, precision=lax.Precision.HIGHEST)
    acum_t = acum.T
    dt_t = dt.T
    lane = lax.broadcasted_iota(jnp.int32, (1, LANES), 1)
    lo = lane < B_HEADDIM

    ys = []
    for g in range(B_GROUPS):
        bg = xc[:, B_WIDTH + g * B_STATE:B_WIDTH + (g + 1) * B_STATE]
        cg = xc[:, B_WIDTH + B_GROUPS * B_STATE + g * B_STATE:B_WIDTH + B_GROUPS * B_STATE + (g + 1) * B_STATE]
        bgb = bg.astype(BF16)
        cgb = cg.astype(BF16)
        cb = lax.dot_general(cgb, bgb, NT_DIMS, preferred_element_type=F32)
        for pr in range(2):
            h0 = g * 4 + pr * 2
            xpair = xs[:, h0 * B_HEADDIM:(h0 + 2) * B_HEADDIM]
            xpb = xpair.astype(BF16)
            ydiag = []
            ecol = []
            wcol = []
            elast = []
            for hh in (h0, h0 + 1):
                a_col = acum[:, hh:hh + 1]
                a_row = acum_t[hh:hh + 1, :]
                decay = jnp.exp(jnp.where(causal, a_col - a_row, NEG_INF))
                lm = cb * decay * dt_t[hh:hh + 1, :]
                ydiag.append(jnp.dot(lm.astype(BF16), xpb, preferred_element_type=F32))
                a_last = acum[cs - 1:cs, hh:hh + 1]
                ecol.append(jnp.exp(a_col))
                wcol.append(jnp.exp(a_last - a_col) * dt[:, hh:hh + 1])
                elast.append(jnp.exp(a_last))
            hp = h_ref[h0 * B_HEADDIM:(h0 + 2) * B_HEADDIM, :]
            yoff = lax.dot_general(cgb, hp.astype(BF16), NT_DIMS, preferred_element_type=F32)
            y = jnp.where(lo, ydiag[0], ydiag[1]) + yoff * jnp.where(lo, ecol[0], ecol[1])
            ys.append(y)
            wx = xpair * jnp.where(lo, wcol[0], wcol[1])
            upd = jnp.dot(wx.T.astype(BF16), bgb, preferred_element_type=F32)
            prow = lax.broadcasted_iota(jnp.int32, (LANES, 1), 0) < B_HEADDIM
            h_ref[h0 * B_HEADDIM:(h0 + 2) * B_HEADDIM, :] = jnp.where(prow, elast[0], elast[1]) * hp + upd

    y = jnp.concatenate(ys, axis=1)
    y = y + dsk_ref[...] * xs
    y = y * _silu(jnp.concatenate([r[...] for r in z_refs], axis=1))
    ms = jnp.mean(y * y, axis=-1, keepdims=True)
    o_ref[...] = y * lax.rsqrt(ms + NORM_EPS) * g_ref[...]

    @pl.when(c == pl.num_programs(1) - 1)
    def _():
        hout_ref[0] = h_ref[...]


def _pad_lanes(v, fill=0.0):
    v = v.reshape(1, -1).astype(F32)
    return jnp.pad(v, ((0, 0), (0, LANES - v.shape[1])), constant_values=fill)


def _ssd_prompt_call(proj, bsz, t, conv_w, conv_b, dt_bias, a_log, d_skip, norm_g):
    nc = t // SSD_CHUNK
    cs = SSD_CHUNK
    const = lambda b, c: (0, 0)
    o, hout = pl.pallas_call(
        _ssd_prompt_kernel,
        grid=(bsz, nc),
        in_specs=[
            *[pl.BlockSpec((cs, SSD_COLS), functools.partial(lambda k, b, c: (b * nc + c, COL_XBC // SSD_COLS + k), k))
              for k in range(B_CONV_DIM // SSD_COLS)],
            *[pl.BlockSpec((cs, SSD_COLS), functools.partial(lambda k, b, c: (b * nc + c, COL_Z // SSD_COLS + k), k))
              for k in range(B_WIDTH // SSD_COLS)],
            pl.BlockSpec((cs, LANES), lambda b, c: (b * nc + c, COL_SMALL // LANES)),
            pl.BlockSpec((CONV_W, B_CONV_DIM), const),
            pl.BlockSpec((1, B_CONV_DIM), const),
            pl.BlockSpec((1, LANES), const),
            pl.BlockSpec((1, LANES), const),
            pl.BlockSpec((1, B_WIDTH), const),
            pl.BlockSpec((1, B_WIDTH), const),
        ],
        out_specs=[
            pl.BlockSpec((cs, B_WIDTH), lambda b, c: (b * nc + c, 0)),
            pl.BlockSpec((1, B_HEADS * B_HEADDIM, B_STATE), lambda b, c: (b, 0, 0)),
        ],
        out_shape=[
            jax.ShapeDtypeStruct((bsz * t, B_WIDTH), F32),
            jax.ShapeDtypeStruct((bsz, B_HEADS * B_HEADDIM, B_STATE), F32),
        ],
        scratch_shapes=[pltpu.VMEM((8 + cs, B_CONV_DIM), F32), pltpu.VMEM((B_HEADS * B_HEADDIM, B_STATE), F32)],
        compiler_params=_cparams(("parallel", "arbitrary")),
        name="ssd_prompt",
    )(*([proj] * (B_CONV_DIM // SSD_COLS + B_WIDTH // SSD_COLS + 1)), conv_w, conv_b.reshape(1, -1),
      _pad_lanes(dt_bias), _pad_lanes(a_log), jnp.repeat(d_skip, B_HEADDIM).reshape(1, B_WIDTH),
      norm_g.reshape(1, B_WIDTH))
    return o, hout.reshape(bsz, B_HEADS, B_HEADDIM, B_STATE)


def _compress_kernel(k_ref, v_ref, pe_ref, w1_ref, w2_ref, kc_ref, vc_ref, *, nchunk):
    outs = []
    for kv, rows_ref in enumerate((k_ref, v_ref)):
        acc_lo = jnp.zeros((nchunk, C_DHEAD), F32)
        acc_hi = jnp.zeros((nchunk, C_DHEAD), F32)
        for r in range(CMP_STRIDE):
            x = rows_ref[pl.ds(r, nchunk, stride=CMP_STRIDE), :]
            x_lo = (x + pe_ref[kv, r:r + 1, :]).astype(BF16)
            x_hi = (x + pe_ref[kv, CMP_STRIDE + r:CMP_STRIDE + r + 1, :]).astype(BF16)
            acc_lo = acc_lo + jnp.dot(x_lo, w1_ref[kv, r].astype(BF16), preferred_element_type=F32)
            acc_hi = acc_hi + jnp.dot(x_hi, w1_ref[kv, CMP_STRIDE + r].astype(BF16), preferred_element_type=F32)
        hid = acc_lo + pltpu.roll(acc_hi, nchunk - 1, 0)
        outs.append(jnp.dot(_gelu(hid).astype(BF16), w2_ref[kv].astype(BF16), preferred_element_type=F32))
    kc_ref[0] = outs[0]
    vc_ref[0] = outs[1]


def _compress_prompt_call(proj, bsz, t, pe, w1, w2):
    nchunk = t // CMP_STRIDE
    shp = jax.ShapeDtypeStruct((bsz, nchunk, C_DHEAD), F32)
    return pl.pallas_call(
        functools.partial(_compress_kernel, nchunk=nchunk),
        grid=(bsz,),
        in_specs=[
            pl.BlockSpec((t, C_DHEAD), lambda b: (b, COL_CKV // C_DHEAD)),
            pl.BlockSpec((t, C_DHEAD), lambda b: (b, COL_CKV // C_DHEAD + 1)),
            pl.BlockSpec((2, CMP_LEN, C_DHEAD), lambda b: (0, 0, 0)),
            pl.BlockSpec((2, CMP_LEN, C_DHEAD, C_DHEAD), lambda b: (0, 0, 0, 0)),
            pl.BlockSpec((2, C_DHEAD, C_DHEAD), lambda b: (0, 0, 0)),
        ],
        out_specs=[pl.BlockSpec((1, nchunk, C_DHEAD), lambda b: (b, 0, 0))] * 2,
        out_shape=[shp, shp],
        compiler_params=_cparams(("parallel",)),
        name="nsa_compress_prompt",
    )(proj, proj, pe, w1.reshape(2, CMP_LEN, C_DHEAD, C_DHEAD), w2)


def _masked_softmax(s, ok):
    s = jnp.where(ok, s, NEG_INF)
    m = jnp.max(s, axis=-1, keepdims=True)
    e = jnp.where(ok, jnp.exp(s - m), 0.0)
    return e, jnp.sum(e, axis=-1, keepdims=True)


def _topk_mask_lanes(score, k, n):
    lane = lax.broadcasted_iota(jnp.int32, (1, LANES), 1)
    rank = jnp.zeros(score.shape, F32)
    for i in range(n):
        ci = score[:, i:i + 1]
        beats = (ci > score) | ((ci == score) & (lane > i))
        rank = rank + jnp.where(beats, 1.0, 0.0)
    return (rank < k) & (lane < n)


def _nsa_prompt_kernel(q_ref, ks_ref, vs_ref, kw_ref, vw_ref, kc_ref, vc_ref, sm_ref, g_ref, o_ref, *, tq, n_cmp):
    i = pl.program_id(1)
    t = ks_ref.shape[0]
    n_sel = t // SEL_BLOCK
    scale = C_DHEAD ** -0.5
    sel_shift = SEL_BLOCK.bit_length() - 1
    wlen = min(t, WINDOW + tq)

    def body(nk):
        qpos = i * tq + lax.broadcasted_iota(jnp.int32, (tq, 1), 0)
        lane = lax.broadcasted_iota(jnp.int32, (1, LANES), 1)

        cmp_end = lane * CMP_STRIDE + (CMP_LEN - 1)
        dist_c = qpos - cmp_end
        ok_c = (dist_c >= 0) & (lane < n_cmp)
        dist_cf = dist_c.astype(F32)
        kcb = kc_ref[0].astype(BF16)
        vcb = vc_ref[0].astype(BF16)
        qs = [(q_ref[:, h * C_DHEAD:(h + 1) * C_DHEAD] * scale).astype(BF16) for h in range(C_HEADS)]
        o_cmp = []
        psum = jnp.zeros((tq, LANES), F32)
        for h in range(C_HEADS):
            s = lax.dot_general(qs[h], kcb, NT_DIMS, preferred_element_type=F32)
            s = s - _alibi_slope(h) * dist_cf
            e, den = _masked_softmax(s, ok_c)
            p = e / jnp.maximum(den, 1e-30)
            psum = psum + p
            o_cmp.append(jnp.dot(p.astype(BF16), vcb, preferred_element_type=F32))

        n_i = lax.broadcasted_iota(jnp.int32, (LANES, LANES), 0)
        j_i = lax.broadcasted_iota(jnp.int32, (LANES, LANES), 1)
        lo_ = jnp.maximum(n_i * CMP_STRIDE, j_i * SEL_BLOCK)
        hi_ = jnp.minimum(n_i * CMP_STRIDE + CMP_LEN, (j_i + 1) * SEL_BLOCK)
        ovl = jnp.maximum(hi_ - lo_, 0).astype(F32) * (1.0 / CMP_LEN)
        ovl = jnp.where((n_i < n_cmp) & (j_i < n_sel), ovl, 0.0)
        imp = jnp.dot(psum, ovl, preferred_element_type=F32, precision=lax.Precision.HIGHEST)
        qblk = qpos >> sel_shift
        sel_valid = lane <= qblk
        forced = (lane == 0) | (lane == qblk) | (lane == qblk - 1)
        score = jnp.where(sel_valid, imp + jnp.where(forced, FORCE_SCORE, 0.0), NEG_INF)
        score = jnp.where(lane < n_sel, score, -jnp.inf)
        chosen = _topk_mask_lanes(score, min(SEL_TOPK, n_sel), n_sel) & sel_valid
        e_j = lax.broadcasted_iota(jnp.int32, (LANES, nk), 0)
        e_k = lax.broadcasted_iota(jnp.int32, (LANES, nk), 1)
        expand = jnp.where((e_k >> sel_shift) == e_j, 1.0, 0.0).astype(BF16)
        key_sel = jnp.dot(jnp.where(chosen, 1.0, 0.0).astype(BF16), expand, preferred_element_type=F32) > 0.5
        kpos = lax.broadcasted_iota(jnp.int32, (1, nk), 1)
        kpos_f = kpos.astype(F32)
        ok_s = key_sel & (qpos >= kpos)
        w0 = pl.multiple_of(jnp.clip(i * tq - WINDOW, 0, t - wlen), 8)
        wpos = w0 + lax.broadcasted_iota(jnp.int32, (1, wlen), 1)
        wpos_f = wpos.astype(F32)
        dist_w = qpos - wpos
        ok_w = (dist_w >= 0) & (dist_w < WINDOW)

        def attend(qh, kb, vb, key_bias, ok):
            s = lax.dot_general(qh, kb, NT_DIMS, preferred_element_type=F32) + key_bias
            s = jnp.where(ok, s, NEG_INF)
            e = jnp.exp(s - jnp.max(s, axis=-1, keepdims=True))
            return jnp.dot(e.astype(BF16), vb, preferred_element_type=F32) / jnp.sum(e, axis=-1, keepdims=True)

        ksb = ks_ref[0:nk, :].astype(BF16)
        vsb = vs_ref[0:nk, :].astype(BF16)
        kwb = kw_ref[pl.ds(w0, wlen), :].astype(BF16)
        vwb = vw_ref[pl.ds(w0, wlen), :].astype(BF16)
        gate = 1.0 / (1.0 + jnp.exp(-sm_ref[...]))
        outs = []
        for h in range(C_HEADS):
            o_sel = attend(qs[h], ksb, vsb, _alibi_slope(h) * kpos_f, ok_s)
            o_win = attend(qs[h], kwb, vwb, _alibi_slope(h) * wpos_f, ok_w)
            g0 = gate[:, GATE_LANE0 + h:GATE_LANE0 + h + 1]
            g1 = gate[:, GATE_LANE0 + C_HEADS + h:GATE_LANE0 + C_HEADS + h + 1]
            g2 = gate[:, GATE_LANE0 + 2 * C_HEADS + h:GATE_LANE0 + 2 * C_HEADS + h + 1]
            outs.append(g0 * o_cmp[h] + g1 * o_sel + g2 * o_win)
        o = jnp.concatenate(outs, axis=1)
        ms = jnp.mean(o * o, axis=-1, keepdims=True)
        o_ref[...] = o * lax.rsqrt(ms + NORM_EPS) * g_ref[...]

    body(t)


def _nsa_prompt_call(proj, kc, vc, bsz, t, out_g, tq=256):
    nq = t // tq
    n_cmp = (t - CMP_LEN) // CMP_STRIDE + 1
    c0 = COL_CKV // C_DHEAD
    w0 = COL_WIN // C_DHEAD
    kvspec = lambda col: pl.BlockSpec((t, C_DHEAD), lambda b, i: (b, col))
    return pl.pallas_call(
        functools.partial(_nsa_prompt_kernel, tq=tq, n_cmp=n_cmp),
        grid=(bsz, nq),
        in_specs=[
            pl.BlockSpec((tq, C_WIDTH), lambda b, i: (b * nq + i, COL_CQ // C_WIDTH)),
            kvspec(c0 + 2), kvspec(c0 + 3), kvspec(w0), kvspec(w0 + 1),
            pl.BlockSpec((1, kc.shape[1], C_DHEAD), lambda b, i: (b, 0, 0)),
            pl.BlockSpec((1, kc.shape[1], C_DHEAD), lambda b, i: (b, 0, 0)),
            pl.BlockSpec((tq, LANES), lambda b, i: (b * nq + i, COL_SMALL // LANES)),
            pl.BlockSpec((1, C_WIDTH), lambda b, i: (0, 0)),
        ],
        out_specs=pl.BlockSpec((tq, C_WIDTH), lambda b, i: (b * nq + i, 0)),
        out_shape=jax.ShapeDtypeStruct((bsz * t, C_WIDTH), F32),
        compiler_params=_cparams(("parallel", "arbitrary")),
        name="nsa_attn_prompt",
    )(proj, proj, proj, proj, proj, kc, vc, proj, out_g.reshape(1, C_WIDTH))


def _outproj_kernel(*refs, two):
    if two:
        xa_ref, xb_ref, oa_ref, ob_ref, oc_ref, w_ref, g_ref, h_ref, hnt_ref = refs
    else:
        xa_ref, oa_ref, ob_ref, oc_ref, w_ref, g_ref, h_ref, hnt_ref = refs
    x = xa_ref[...]
    if two:
        x = x + xb_ref[...]
    mixed = jnp.dot(oa_ref[...].astype(BF16), w_ref[0:A_WIDTH, :], preferred_element_type=F32)
    mixed = mixed + jnp.dot(ob_ref[...].astype(BF16), w_ref[A_WIDTH:A_WIDTH + B_WIDTH, :], preferred_element_type=F32)
    mixed = mixed + jnp.dot(oc_ref[...].astype(BF16), w_ref[A_WIDTH + B_WIDTH:, :], preferred_element_type=F32)
    h = x + mixed
    h_ref[...] = h
    ms = jnp.mean(h * h, axis=-1, keepdims=True)
    hnt_ref[...] = (h * lax.rsqrt(ms + NORM_EPS) * g_ref[...]).T.astype(BF16)


def _outproj_call(xa, xb, oa, ob, oc, w_out_b, g, tm):
    n = xa.shape[0]
    two = xb is not None
    row = lambda w: pl.BlockSpec((tm, w), lambda i: (i, 0))
    ins = [xa] + ([xb] if two else []) + [oa, ob, oc, w_out_b, g.reshape(1, D_MODEL)]
    specs = [row(D_MODEL)] + ([row(D_MODEL)] if two else []) + [
        row(A_WIDTH), row(B_WIDTH), row(C_WIDTH),
        pl.BlockSpec((D_MODEL, D_MODEL), lambda i: (0, 0)),
        pl.BlockSpec((1, D_MODEL), lambda i: (0, 0)),
    ]
    return pl.pallas_call(
        functools.partial(_outproj_kernel, two=two),
        grid=(n // tm,),
        in_specs=specs,
        out_specs=[row(D_MODEL), pl.BlockSpec((D_MODEL, tm), lambda i: (0, i))],
        out_shape=[jax.ShapeDtypeStruct((n, D_MODEL), F32), jax.ShapeDtypeStruct((D_MODEL, n), BF16)],
        compiler_params=_cparams(("parallel",)),
        name="out_proj",
    )(*ins)


def _peer_q_kernel(wqt_ref, hnt_ref, qt_ref):
    qt_ref[...] = jnp.dot(wqt_ref[...], hnt_ref[...], preferred_element_type=F32)


def _peer_q_call(wq_t, hn_t, tm):
    n = hn_t.shape[1]
    dq = wq_t.shape[0]
    return pl.pallas_call(
        _peer_q_kernel,
        grid=(n // tm,),
        in_specs=[pl.BlockSpec((dq, D_MODEL), lambda i: (0, 0)), pl.BlockSpec((D_MODEL, tm), lambda i: (0, i))],
        out_specs=pl.BlockSpec((dq, tm), lambda i: (0, i)),
        out_shape=jax.ShapeDtypeStruct((dq, n), F32),
        compiler_params=_cparams(("parallel",)),
        name="peer_query",
    )(wq_t, hn_t)


NOT_RANKED = 99.0


def _top_rows(s, pos, k, want_rank):
    rank = jnp.full(s.shape, NOT_RANKED, F32) if want_rank else None
    vals, picks = [], []
    for j in range(k):
        m = jnp.max(s, axis=0, keepdims=True)
        idx = jnp.min(jnp.where(s == m, pos, 1e9), axis=0, keepdims=True)
        hit = pos == idx
        if want_rank:
            rank = jnp.where(hit, float(j), rank)
        s = jnp.where(hit, -jnp.inf, s)
        vals.append(m)
        picks.append(idx)
    return jnp.concatenate(vals, axis=0), jnp.concatenate(picks, axis=0), rank


PAIR_ROWS = PEER_TOPK + 7 * 8 + 8


def _pair_candidates(v1, v2):
    tn = v1.shape[1]
    parts = [v1[0:1, :] + v2] + [v1[a:a + 1, :] + v2[0:8, :] for a in range(1, 8)] + [v1[8:16, :] + v2[0:1, :]]
    r = lax.broadcasted_iota(jnp.int32, (PAIR_ROWS, tn), 0)
    mid = r - PEER_TOPK
    pos = jnp.where(r < PEER_TOPK, r,
                    jnp.where(r < PEER_TOPK + 56, ((mid >> 3) + 1) * PEER_TOPK + (mid & 7), (r - 64) * PEER_TOPK))
    return jnp.concatenate(parts, axis=0), pos.astype(F32)


def _peer_route_kernel(qt_ref, sk_ref, lim_ref, coef_ref, rank2_ref, e2_ref, *, heads):
    half = PEER_DQ // 2
    row = lax.broadcasted_iota(jnp.int32, (PEER_NKEYS, qt_ref.shape[1]), 0).astype(F32)
    for hh in range(heads):
        q = qt_ref[hh * PEER_DQ:(hh + 1) * PEER_DQ, :]
        s1 = jnp.dot(sk_ref[hh, 0], q[0:half, :], preferred_element_type=F32, precision=lax.Precision.HIGHEST)
        s2 = jnp.dot(sk_ref[hh, 1], q[half:, :], preferred_element_type=F32, precision=lax.Precision.HIGHEST)
        v1, _, rank1 = _top_rows(s1, row, PEER_TOPK, True)
        v2, _, rank2 = _top_rows(s2, row, PEER_TOPK, True)
        cand, cpos = _pair_candidates(v1, v2)
        top, pos, _ = _top_rows(cand, cpos, PEER_TOPK, False)
        z = jnp.sum(jnp.exp(top - top[0:1, :]), axis=0, keepdims=True)
        a_of = jnp.floor(pos * (1.0 / PEER_TOPK))
        lim = jnp.zeros(s1.shape, F32)
        for a in range(PEER_TOPK):
            cnt = jnp.sum(jnp.where(a_of == float(a), 1.0, 0.0), axis=0, keepdims=True)
            lim = jnp.where(rank1 == float(a), cnt, lim)
        lim_ref[hh] = lim
        coef_ref[hh] = jnp.exp(s1 - v1[0:1, :]) / z
        rank2_ref[hh] = rank2.astype(BF16)
        e2_ref[hh] = jnp.exp(s2 - v2[0:1, :]).astype(BF16)


def _peer_route_call(q_t, subkeys, tn=LANES, heads=4):
    n = q_t.shape[1]
    shp = lambda dt: jax.ShapeDtypeStruct((PEER_HEADS, PEER_NKEYS, n), dt)
    ospec = pl.BlockSpec((heads, PEER_NKEYS, tn), lambda j, h: (h, 0, j))
    return pl.pallas_call(
        functools.partial(_peer_route_kernel, heads=heads),
        grid=(n // tn, PEER_HEADS // heads),
        in_specs=[
            pl.BlockSpec((heads * PEER_DQ, tn), lambda j, h: (h, j)),
            pl.BlockSpec((heads, 2, PEER_NKEYS, PEER_DQ // 2), lambda j, h: (h, 0, 0, 0)),
        ],
        out_specs=[ospec] * 4,
        out_shape=[shp(F32), shp(F32), shp(BF16), shp(BF16)],
        compiler_params=_cparams(("parallel", "arbitrary")),
        name="peer_route",
    )(q_t, subkeys)


def _peer_expert_kernel(hnt_ref, lim_ref, coef_ref, rank2_ref, e2_ref, u_ref, v_ref, o_ref, *rest, et, emit):
    acc_ref = rest[-1]
    t = pl.program_id(1)

    @pl.when(t == 0)
    def _():
        acc_ref[...] = jnp.zeros_like(acc_ref)

    if emit:
        ub = u_ref[...].astype(BF16)
        vtb = v_ref[...].T.astype(BF16)
        rest[0][...] = ub
        rest[1][...] = vtb
    else:
        ub = u_ref[...]
        vtb = v_ref[...]
    tn = hnt_ref.shape[1]
    hid = jnp.dot(ub, hnt_ref[...], preferred_element_type=F32)
    acts = []
    for ii in range(et // PEER_NKEYS):
        i1 = t * (et // PEER_NKEYS) + ii
        gate = jnp.zeros((PEER_NKEYS, tn), BF16)
        for h in range(PEER_HEADS):
            lim = lim_ref[h, pl.ds(i1, 1), :].astype(BF16)
            coef = coef_ref[h, pl.ds(i1, 1), :].astype(BF16)
            gate = gate + jnp.where(rank2_ref[h] < lim, e2_ref[h], jnp.zeros((), BF16)) * coef
        acts.append(gate * _gelu(hid[ii * PEER_NKEYS:(ii + 1) * PEER_NKEYS, :]).astype(BF16))
    acc_ref[...] += jnp.dot(vtb, jnp.concatenate(acts, axis=0), preferred_element_type=F32)

    @pl.when(t == pl.num_programs(1) - 1)
    def _():
        o_ref[...] = acc_ref[...].T


def _peer_expert_call(hn_t, route, u_tab, v_tab, tn, et, emit=False, layer=0):
    n = hn_t.shape[1]
    n_exp = u_tab.shape[-2] if emit else u_tab.shape[0]
    rspec = pl.BlockSpec((PEER_HEADS, PEER_NKEYS, tn), lambda j, t: (0, 0, j))
    uspec = pl.BlockSpec((et, D_MODEL), lambda j, t: (t, 0))
    vtspec = pl.BlockSpec((D_MODEL, et), lambda j, t: (0, t))
    ospec = pl.BlockSpec((tn, D_MODEL), lambda j, t: (j, 0))
    oshape = jax.ShapeDtypeStruct((n, D_MODEL), F32)
    if emit:
        assert n == tn
        out_specs = [ospec, uspec, vtspec]
        out_shape = [oshape, jax.ShapeDtypeStruct((n_exp, D_MODEL), BF16), jax.ShapeDtypeStruct((D_MODEL, n_exp), BF16)]
    else:
        out_specs, out_shape = ospec, oshape
    return pl.pallas_call(
        functools.partial(_peer_expert_kernel, et=et, emit=emit),
        grid=(n // tn, n_exp // et),
        in_specs=[pl.BlockSpec((D_MODEL, tn), lambda j, t: (0, j)), rspec, rspec, rspec, rspec]
        + ([pl.BlockSpec((None, et, D_MODEL), lambda j, t: (layer, t, 0))] * 2 if emit else [uspec, vtspec]),
        out_specs=out_specs,
        out_shape=out_shape,
        scratch_shapes=[pltpu.VMEM((D_MODEL, tn), F32)],
        compiler_params=_cparams(("parallel", "arbitrary")),
        name="peer_experts",
    )(hn_t, *route, u_tab, v_tab)


def _final_norm_kernel(xa_ref, xb_ref, g_ref, o_ref):
    x = xa_ref[...] + xb_ref[...]
    ms = jnp.mean(x * x, axis=-1, keepdims=True)
    o_ref[...] = x * lax.rsqrt(ms + NORM_EPS) * g_ref[...]


def _final_norm_call(xa, xb, g, tm):
    n = xa.shape[0]
    row = pl.BlockSpec((tm, D_MODEL), lambda i: (i, 0))
    return pl.pallas_call(
        _final_norm_kernel,
        grid=(n // tm,),
        in_specs=[row, row, pl.BlockSpec((1, D_MODEL), lambda i: (0, 0))],
        out_specs=row,
        out_shape=jax.ShapeDtypeStruct((n, D_MODEL), F32),
        compiler_params=_cparams(("parallel",)),
        name="final_norm",
    )(xa, xb, g.reshape(1, D_MODEL))


DIFF_PAGES = 8


def _diff_sample_kernel(pt_ref, q_ref, knew_ref, vnew_ref, dl_ref, g_ref, *rest, past_len, lam_init):
    page_refs = [_row_view(r, PAGE_SIZE * 2 * A_HEADS) for r in rest[:DIFF_PAGES]]
    o_ref, m_ref, l_ref, acc_ref = rest[DIFF_PAGES:]
    p = pl.program_id(1)
    scale = A_HALF ** -0.5
    nrow = 2 * A_HEADS
    per_key = 2 * A_HEADS

    @pl.when(p == 0)
    def _():
        m_ref[...] = jnp.full(m_ref.shape, NEG_INF, F32)
        l_ref[...] = jnp.zeros(l_ref.shape, F32)
        acc_ref[...] = jnp.zeros(acc_ref.shape, F32)

    row = lax.broadcasted_iota(jnp.int32, (nrow, 1), 0)
    lane = lax.broadcasted_iota(jnp.int32, (1, A_DHEAD), 1)
    slope = jnp.exp2(-2.0 * ((row >> 1) + 1).astype(F32))
    q = q_ref[0]
    q2 = [jnp.where(((row >> 1) == h) & ((lane >= A_HALF) == ((row & 1) == 1)), q[:, h * A_DHEAD:(h + 1) * A_DHEAD], 0.0)
          for h in range(A_HEADS)]
    q2b = [x.astype(BF16) for x in q2]
    ss = []
    for g in range(DIFF_PAGES):
        sg = None
        for h in range(A_HEADS):
            kh = page_refs[g][pl.ds(h, PAGE_SIZE, stride=per_key), :].astype(BF16)
            d = lax.dot_general(q2b[h], kh, NT_DIMS, preferred_element_type=F32)
            sg = d if sg is None else sg + d
        ss.append(sg)
    s = jnp.concatenate(ss, axis=1) * scale
    nk = DIFF_PAGES * PAGE_SIZE
    kpos = p * nk + lax.broadcasted_iota(jnp.int32, (1, nk), 1)
    s = s - slope * (past_len - kpos).astype(F32)
    m_old = m_ref[:, 0:1]
    m_new = jnp.maximum(m_old, jnp.max(s, axis=-1, keepdims=True))
    alpha = jnp.exp(m_old - m_new)
    e = jnp.exp(s - m_new)
    eb = e.astype(BF16)
    l_new = alpha * l_ref[:, 0:1] + jnp.sum(e, axis=-1, keepdims=True)
    pv = jnp.zeros((nrow, A_DHEAD), F32)
    for h in range(A_HEADS):
        vh = jnp.concatenate([page_refs[g][pl.ds(A_HEADS + h, PAGE_SIZE, stride=per_key), :].astype(BF16)
                              for g in range(DIFF_PAGES)], axis=0)
        pv = pv + jnp.where((row >> 1) == h, jnp.dot(eb, vh, preferred_element_type=F32), 0.0)
    acc = alpha * acc_ref[...] + pv
    m_ref[...] = jnp.broadcast_to(m_new, m_ref.shape)
    l_ref[...] = jnp.broadcast_to(l_new, l_ref.shape)
    acc_ref[...] = acc

    @pl.when(p == pl.num_programs(1) - 1)
    def _():
        knew = knew_ref[0]
        vnew = vnew_ref[0]
        s_n = jnp.zeros((nrow, 1), F32)
        for h in range(A_HEADS):
            s_n = s_n + jnp.sum(q2[h] * knew[:, h * A_DHEAD:(h + 1) * A_DHEAD], axis=-1, keepdims=True)
        s_n = s_n * scale
        v8 = jnp.concatenate([vnew[:, (r // 2) * A_DHEAD:(r // 2 + 1) * A_DHEAD] for r in range(nrow)], axis=0)
        m_f = jnp.maximum(m_new, s_n)
        a_f = jnp.exp(m_new - m_f)
        e_n = jnp.exp(s_n - m_f)
        o8 = (a_f * acc + e_n * v8) / (a_f * l_new + e_n)
        lam = _diff_lambda(dl_ref[...], lam_init)
        outs = []
        for h in range(A_HEADS):
            oh = o8[2 * h:2 * h + 1, :] - lam * o8[2 * h + 1:2 * h + 2, :]
            ms = jnp.mean(oh * oh, axis=-1, keepdims=True)
            outs.append(oh * lax.rsqrt(ms + NORM_EPS) * g_ref[...] * (1.0 - lam_init))
        o_ref[0] = jnp.concatenate(outs, axis=1)


def _diff_sample_call(page_table, proj3, cache, layer, dl, subln_g, lam_init):
    bs, n_pages = page_table.shape
    past_len = n_pages * PAGE_SIZE
    steps = n_pages // DIFF_PAGES

    def page_spec(g):
        return pl.BlockSpec((None, None, PAGE_SIZE, 2, A_HEADS, A_DHEAD),
                            lambda b, p, pt: (layer, pt[b, p * DIFF_PAGES + g], 0, 0, 0, 0))

    grid_spec = pltpu.PrefetchScalarGridSpec(
        num_scalar_prefetch=1,
        grid=(bs, steps),
        in_specs=[
            pl.BlockSpec((1, 1, A_WIDTH), lambda b, p, pt: (b, 0, COL_AQ // A_WIDTH)),
            pl.BlockSpec((1, 1, A_WIDTH), lambda b, p, pt: (b, 0, COL_AK // A_WIDTH)),
            pl.BlockSpec((1, 1, A_WIDTH), lambda b, p, pt: (b, 0, COL_AV // A_WIDTH)),
            pl.BlockSpec((4, A_HALF), lambda b, p, pt: (0, 0)),
            pl.BlockSpec((1, A_DHEAD), lambda b, p, pt: (0, 0)),
        ] + [page_spec(g) for g in range(DIFF_PAGES)],
        out_specs=pl.BlockSpec((1, 1, A_WIDTH), lambda b, p, pt: (b, 0, 0)),
        scratch_shapes=[pltpu.VMEM((2 * A_HEADS, LANES), F32), pltpu.VMEM((2 * A_HEADS, LANES), F32),
                        pltpu.VMEM((2 * A_HEADS, A_DHEAD), F32)],
    )
    return pl.pallas_call(
        functools.partial(_diff_sample_kernel, past_len=past_len, lam_init=lam_init),
        grid_spec=grid_spec,
        out_shape=jax.ShapeDtypeStruct((bs, 1, A_WIDTH), F32),
        compiler_params=_cparams(("parallel", "arbitrary")),
        name="diff_attn_sample",
    )(page_table, proj3, proj3, proj3, dl, subln_g.reshape(1, A_DHEAD), *([cache] * DIFF_PAGES))


def _diag_rows(vec):
    n = vec.shape[1]
    r = lax.broadcasted_iota(jnp.int32, (n, n), 0)
    c = lax.broadcasted_iota(jnp.int32, (n, n), 1)
    return jnp.where(r == c, vec, 0.0)


def _ssd_sample_kernel(*refs):
    nx, nz = B_CONV_DIM // SSD_COLS, B_WIDTH // SSD_COLS
    xbc_refs, z_refs = refs[:nx], refs[nx:nx + nz]
    (sm_ref, cbuf_ref, h0_ref, cw_ref, cb_ref, dtb_ref, alog_ref, dsk_ref, g_ref,
     o_ref, hout_ref, cout_ref) = refs[nx + nz:]
    hi = lax.Precision.HIGHEST
    new = jnp.concatenate([r[0] for r in xbc_refs], axis=1)
    buf = cbuf_ref[0, 0]
    cw = cw_ref[...]
    conv = cb_ref[...] + cw[CONV_W - 1:CONV_W] * new
    for i in range(CONV_W - 1):
        conv = conv + cw[i:i + 1] * buf[i:i + 1]
    cout_ref[0] = jnp.concatenate([buf[1:CONV_W - 1], new], axis=0)
    xc = _silu(conv)
    xs = xc[:, :B_WIDTH]
    dt = _softplus(sm_ref[0] + dtb_ref[...])
    ea = jnp.exp(dt * (-jnp.exp(alog_ref[...])))
    hr = lax.broadcasted_iota(jnp.int32, (LANES, B_WIDTH), 0)
    hc = lax.broadcasted_iota(jnp.int32, (LANES, B_WIDTH), 1)
    rep = jnp.where((hc // B_HEADDIM) == hr, 1.0, 0.0)
    both = jnp.concatenate([dt, ea, jnp.zeros((6, LANES), F32)], axis=0)
    both_rep = jnp.dot(both, rep, preferred_element_type=F32, precision=hi)
    u = both_rep[0:1] * xs
    ea_rep = both_rep[1:2]
    gn = B_GROUPS * B_STATE
    rows = (B_HEADS // B_GROUPS) * B_HEADDIM
    ys = []
    for g in range(B_GROUPS):
        r0 = g * rows
        bg = xc[:, B_WIDTH + g * B_STATE:B_WIDTH + (g + 1) * B_STATE]
        cg = xc[:, B_WIDTH + gn + g * B_STATE:B_WIDTH + gn + (g + 1) * B_STATE]
        h0 = h0_ref[0, 0, r0:r0 + rows, :]
        hn = jnp.dot(_diag_rows(ea_rep[:, r0:r0 + rows]), h0, preferred_element_type=F32, precision=hi)
        hn = hn + jnp.dot(_diag_rows(u[:, r0:r0 + rows]), jnp.broadcast_to(bg, (rows, B_STATE)),
                          preferred_element_type=F32, precision=hi)
        hout_ref[0, r0:r0 + rows, :] = hn
        c8 = jnp.broadcast_to(cg, (8, B_STATE)).astype(BF16)
        ys.append(lax.dot_general(c8, hn.astype(BF16), NT_DIMS, preferred_element_type=F32)[0:1])
    y = jnp.concatenate(ys, axis=1) + dsk_ref[...] * xs
    y = y * _silu(jnp.concatenate([r[0] for r in z_refs], axis=1))
    ms = jnp.mean(y * y, axis=-1, keepdims=True)
    o_ref[0] = y * lax.rsqrt(ms + NORM_EPS) * g_ref[...]


def _ssd_sample_call(proj3, state_conv, state_ssm4, layer, conv_w, conv_b, dt_bias, a_log, d_skip, norm_g):
    bs = proj3.shape[0]
    const = lambda b: (0, 0)
    nrow = B_HEADS * B_HEADDIM
    return pl.pallas_call(
        _ssd_sample_kernel,
        grid=(bs,),
        in_specs=[
            *[pl.BlockSpec((1, 1, SSD_COLS), functools.partial(lambda k, b: (b, 0, COL_XBC // SSD_COLS + k), k))
              for k in range(B_CONV_DIM // SSD_COLS)],
            *[pl.BlockSpec((1, 1, SSD_COLS), functools.partial(lambda k, b: (b, 0, COL_Z // SSD_COLS + k), k))
              for k in range(B_WIDTH // SSD_COLS)],
            pl.BlockSpec((1, 1, LANES), lambda b: (b, 0, COL_SMALL // LANES)),
            pl.BlockSpec((1, 1, CONV_W - 1, B_CONV_DIM), lambda b: (layer, b, 0, 0)),
            pl.BlockSpec((1, 1, nrow, B_STATE), lambda b: (layer, b, 0, 0)),
            pl.BlockSpec((CONV_W, B_CONV_DIM), const),
            pl.BlockSpec((1, B_CONV_DIM), const),
            pl.BlockSpec((1, LANES), const),
            pl.BlockSpec((1, LANES), const),
            pl.BlockSpec((1, B_WIDTH), const),
            pl.BlockSpec((1, B_WIDTH), const),
        ],
        out_specs=[
            pl.BlockSpec((1, 1, B_WIDTH), lambda b: (b, 0, 0)),
            pl.BlockSpec((1, nrow, B_STATE), lambda b: (b, 0, 0)),
            pl.BlockSpec((1, CONV_W - 1, B_CONV_DIM), lambda b: (b, 0, 0)),
        ],
        out_shape=[
            jax.ShapeDtypeStruct((bs, 1, B_WIDTH), F32),
            jax.ShapeDtypeStruct((bs, nrow, B_STATE), F32),
            jax.ShapeDtypeStruct((bs, CONV_W - 1, B_CONV_DIM), F32),
        ],
        compiler_params=_cparams(("parallel",)),
        name="ssd_sample",
    )(*([proj3] * (B_CONV_DIM // SSD_COLS + B_WIDTH // SSD_COLS + 1)), state_conv, state_ssm4, conv_w,
      conv_b.reshape(1, -1), _pad_lanes(dt_bias),
      _pad_lanes(a_log), jnp.repeat(d_skip, B_HEADDIM).reshape(1, B_WIDTH), norm_g.reshape(1, B_WIDTH))


CMP_PAGES = 32


def _compress_paged_kernel(pt_ref, pe_ref, w1_ref, *rest):
    page_refs = [_row_view(r, PAGE_SIZE * NSA_ROWS) for r in rest[:CMP_PAGES]]
    o_ref = rest[CMP_PAGES]
    per_page = PAGE_SIZE // CMP_STRIDE
    outs = []
    for kv in range(2):
        acc_lo = jnp.zeros((CMP_PAGES * per_page, C_DHEAD), F32)
        acc_hi = jnp.zeros((CMP_PAGES * per_page, C_DHEAD), F32)
        for r in range(CMP_STRIDE):
            x = jnp.concatenate([page_refs[g][pl.ds(NSA_ROWS * r + kv, per_page, stride=NSA_ROWS * CMP_STRIDE), :]
                                 for g in range(CMP_PAGES)], axis=0)
            x_lo = (x + pe_ref[kv, r:r + 1, :]).astype(BF16)
            x_hi = (x + pe_ref[kv, CMP_STRIDE + r:CMP_STRIDE + r + 1, :]).astype(BF16)
            acc_lo = acc_lo + jnp.dot(x_lo, w1_ref[kv, r].astype(BF16), preferred_element_type=F32)
            acc_hi = acc_hi + jnp.dot(x_hi, w1_ref[kv, CMP_STRIDE + r].astype(BF16), preferred_element_type=F32)
        outs += [acc_lo, acc_hi]
    o_ref[0] = jnp.concatenate(outs, axis=1)


def _compress_paged_call(page_table, cache, layer, pe, w1):
    bs, n_pages = page_table.shape
    steps = n_pages // CMP_PAGES
    per_page = PAGE_SIZE // CMP_STRIDE

    def page_spec(g):
        return pl.BlockSpec((None, None, PAGE_SIZE, NSA_ROWS, C_DHEAD),
                            lambda b, p, pt: (layer, pt[b, p * CMP_PAGES + g], 0, 0, 0))

    grid_spec = pltpu.PrefetchScalarGridSpec(
        num_scalar_prefetch=1,
        grid=(bs, steps),
        in_specs=[
            pl.BlockSpec((2, CMP_LEN, C_DHEAD), lambda b, p, pt: (0, 0, 0)),
            pl.BlockSpec((2, CMP_LEN, C_DHEAD, C_DHEAD), lambda b, p, pt: (0, 0, 0, 0)),
        ] + [page_spec(g) for g in range(CMP_PAGES)],
        out_specs=pl.BlockSpec((1, CMP_PAGES * per_page, 4 * C_DHEAD), lambda b, p, pt: (b, p, 0)),
    )
    return pl.pallas_call(
        _compress_paged_kernel,
        grid_spec=grid_spec,
        out_shape=jax.ShapeDtypeStruct((bs, n_pages * per_page, 4 * C_DHEAD), F32),
        compiler_params=_cparams(("parallel", "arbitrary")),
        name="nsa_compress_sample",
    )(page_table, pe, w1.reshape(2, CMP_LEN, C_DHEAD, C_DHEAD), *([cache] * CMP_PAGES))


def _heads_to_rows(q):
    rows = [q[:, h * C_DHEAD:(h + 1) * C_DHEAD] for h in range(C_HEADS)]
    return jnp.concatenate(rows + [jnp.zeros((8 - C_HEADS, C_DHEAD), F32)], axis=0)


SEL_LANES = 384


def _nsa_select_kernel(part_ref, w2_ref, q_ref, ocmp_ref, sel_ref, *, q_pos):
    nchunk = part_ref.shape[1]
    n_cmp = (q_pos + 1 - CMP_LEN) // CMP_STRIDE + 1
    n_sel = -(-(q_pos + 1) // SEL_BLOCK)
    scale = C_DHEAD ** -0.5
    part = part_ref[0]
    kv_cmp = []
    for kv in range(2):
        lo = part[:, (2 * kv) * C_DHEAD:(2 * kv + 1) * C_DHEAD]
        hi = part[:, (2 * kv + 1) * C_DHEAD:(2 * kv + 2) * C_DHEAD]
        hid = lo + pltpu.roll(hi, nchunk - 1, 0)
        kv_cmp.append(jnp.dot(_gelu(hid).astype(BF16), w2_ref[kv].astype(BF16), preferred_element_type=F32).astype(BF16))
    q8 = _heads_to_rows(q_ref[0]).astype(BF16)
    row = lax.broadcasted_iota(jnp.int32, (8, 1), 0)
    slope = jnp.exp2(-2.0 * (row + 1).astype(F32))
    n_i = lax.broadcasted_iota(jnp.int32, (1, nchunk), 1)
    dist_c = q_pos - (n_i * CMP_STRIDE + CMP_LEN - 1)
    ok = (dist_c >= 0) & (n_i < n_cmp)
    s = lax.dot_general(q8, kv_cmp[0], NT_DIMS, preferred_element_type=F32) * scale - slope * dist_c.astype(F32)
    e, den = _masked_softmax(s, ok)
    p = jnp.where(row < C_HEADS, e / jnp.maximum(den, 1e-30), 0.0)
    ocmp_ref[0] = jnp.dot(p.astype(BF16), kv_cmp[1], preferred_element_type=F32)
    psum = jnp.broadcast_to(jnp.sum(p, axis=0, keepdims=True), (8, nchunk))
    c_i = lax.broadcasted_iota(jnp.int32, (nchunk, SEL_LANES), 0)
    j_i = lax.broadcasted_iota(jnp.int32, (nchunk, SEL_LANES), 1)
    lo_ = jnp.maximum(c_i * CMP_STRIDE, j_i * SEL_BLOCK)
    hi_ = jnp.minimum(c_i * CMP_STRIDE + CMP_LEN, (j_i + 1) * SEL_BLOCK)
    ovl = jnp.where((c_i < n_cmp) & (j_i < n_sel), jnp.maximum(hi_ - lo_, 0).astype(F32) * (1.0 / CMP_LEN), 0.0)
    imp = jnp.dot(psum, ovl, preferred_element_type=F32, precision=lax.Precision.HIGHEST)[0:1]
    lane = lax.broadcasted_iota(jnp.int32, (1, SEL_LANES), 1)
    qblk = q_pos // SEL_BLOCK
    forced = (lane == 0) | (lane == qblk) | (lane == qblk - 1)
    score = jnp.where(lane <= qblk, imp + jnp.where(forced, FORCE_SCORE, 0.0), NEG_INF)
    score = jnp.where(lane < n_sel, score, -jnp.inf)
    lane_f = lane.astype(F32)
    out_lane = lax.broadcasted_iota(jnp.int32, (1, LANES), 1)
    sel = jnp.full((1, LANES), -1.0, F32)
    for k in range(min(SEL_TOPK, n_sel)):
        m = jnp.max(score, axis=-1, keepdims=True)
        idx = jnp.min(jnp.where(score == m, lane_f, 1e9), axis=-1, keepdims=True)
        sel = jnp.where(out_lane == k, jnp.where(m > NEG_INF / 2, idx, -1.0), sel)
        score = jnp.where(lane_f == idx, -jnp.inf, score)
    sel_ref[0] = sel.astype(jnp.int32)


def _nsa_select_call(part, w2, q3, q_pos):
    bs, nchunk, _ = part.shape
    return pl.pallas_call(
        functools.partial(_nsa_select_kernel, q_pos=q_pos),
        grid=(bs,),
        in_specs=[
            pl.BlockSpec((1, nchunk, 4 * C_DHEAD), lambda b: (b, 0, 0)),
            pl.BlockSpec((2, C_DHEAD, C_DHEAD), lambda b: (0, 0, 0)),
            pl.BlockSpec((1, 1, C_WIDTH), lambda b: (b, 0, COL_CQ // C_WIDTH)),
        ],
        out_specs=[pl.BlockSpec((1, 8, C_DHEAD), lambda b: (b, 0, 0)), pl.BlockSpec((1, 1, LANES), lambda b: (b, 0, 0))],
        out_shape=[jax.ShapeDtypeStruct((bs, 8, C_DHEAD), F32), jax.ShapeDtypeStruct((bs, 1, LANES), jnp.int32)],
        compiler_params=_cparams(("parallel",)),
        name="nsa_select_sample",
    )(part, w2, q3)


def _nsa_attend_kernel(sel_ref, pt_ref, q_ref, new_ref, wnew_ref, sm_ref, ocmp_ref, win_ref, g_ref, *rest, q_pos):
    k_eff = SEL_TOPK
    blk_refs = [_row_view(r, SEL_BLOCK * NSA_ROWS) for r in rest[:k_eff]]
    o_ref, wout_ref = rest[k_eff:]
    b = pl.program_id(0)
    scale = C_DHEAD ** -0.5
    n_past_blocks = q_pos // SEL_BLOCK
    q8f = _heads_to_rows(q_ref[0])
    q8 = q8f.astype(BF16)
    row = lax.broadcasted_iota(jnp.int32, (8, 1), 0)
    slope = jnp.exp2(-2.0 * (row + 1).astype(F32))
    lane64 = lax.broadcasted_iota(jnp.int32, (1, SEL_BLOCK), 1)

    ss, vs, oks = [], [], []
    new_sel = jnp.zeros((1, 1), jnp.int32)
    for k in range(k_eff):
        j = sel_ref[b, k]
        k_sel = blk_refs[k][pl.ds(2, SEL_BLOCK, stride=NSA_ROWS), :]
        v_sel = blk_refs[k][pl.ds(3, SEL_BLOCK, stride=NSA_ROWS), :]
        s = lax.dot_general(q8, k_sel.astype(BF16), NT_DIMS, preferred_element_type=F32)
        dist = q_pos - (j * SEL_BLOCK + lane64)
        ss.append(s * scale - slope * dist.astype(F32))
        oks.append(lane64 * 0 + jnp.where((j >= 0) & (j < n_past_blocks), 1, 0))
        vs.append(v_sel.astype(BF16))
        new_sel = new_sel + jnp.where(j == n_past_blocks, 1, 0)
    s = jnp.concatenate(ss, axis=1)
    ok = jnp.concatenate(oks, axis=1) > 0
    new = new_ref[0]
    s_n = jnp.sum(q8f * new[:, 2 * C_DHEAD:3 * C_DHEAD], axis=-1, keepdims=True) * scale
    s_n = jnp.where(new_sel > 0, s_n, NEG_INF)
    s = jnp.where(ok, s, NEG_INF)
    m = jnp.maximum(jnp.max(s, axis=-1, keepdims=True), s_n)
    e = jnp.where(ok, jnp.exp(s - m), 0.0)
    e_n = jnp.where(new_sel > 0, jnp.exp(s_n - m), 0.0)
    den = jnp.sum(e, axis=-1, keepdims=True) + e_n
    o_sel = (jnp.dot(e.astype(BF16), jnp.concatenate(vs, axis=0), preferred_element_type=F32)
             + e_n * new[:, 3 * C_DHEAD:]) / den

    lw = win_ref.shape[0]
    win_k = win_ref[:, 0, :]
    win_v = win_ref[:, 1, :]
    wnew = wnew_ref[0]
    wpos = lax.broadcasted_iota(jnp.int32, (1, lw), 1)
    dist_w = lw - wpos
    ok_w = dist_w < WINDOW
    s = lax.dot_general(q8, win_k.astype(BF16), NT_DIMS, preferred_element_type=F32) * scale
    s = jnp.where(ok_w, s - slope * dist_w.astype(F32), NEG_INF)
    s_n = jnp.sum(q8f * wnew[:, :C_DHEAD], axis=-1, keepdims=True) * scale
    m = jnp.maximum(jnp.max(s, axis=-1, keepdims=True), s_n)
    e = jnp.where(ok_w, jnp.exp(s - m), 0.0)
    e_n = jnp.exp(s_n - m)
    den = jnp.sum(e, axis=-1, keepdims=True) + e_n
    o_win = (jnp.dot(e.astype(BF16), win_v.astype(BF16), preferred_element_type=F32)
             + e_n * wnew[:, C_DHEAD:]) / den
    keep = min(WINDOW, lw + 1)
    wout_ref[0:keep - 1, :, :] = win_ref[lw + 1 - keep:lw, :, :]
    wout_ref[keep - 1:keep, 0, :] = wnew[:, :C_DHEAD]
    wout_ref[keep - 1:keep, 1, :] = wnew[:, C_DHEAD:]

    gate = 1.0 / (1.0 + jnp.exp(-sm_ref[0]))
    outs = []
    for h in range(C_HEADS):
        g0 = gate[:, GATE_LANE0 + h:GATE_LANE0 + h + 1]
        g1 = gate[:, GATE_LANE0 + C_HEADS + h:GATE_LANE0 + C_HEADS + h + 1]
        g2 = gate[:, GATE_LANE0 + 2 * C_HEADS + h:GATE_LANE0 + 2 * C_HEADS + h + 1]
        outs.append(g0 * ocmp_ref[0, h:h + 1, :] + g1 * o_sel[h:h + 1, :] + g2 * o_win[h:h + 1, :])
    o = jnp.concatenate(outs, axis=1)
    ms = jnp.mean(o * o, axis=-1, keepdims=True)
    o_ref[0] = o * lax.rsqrt(ms + NORM_EPS) * g_ref[...]


def _nsa_attend_call(sel, page_table, proj3, ocmp, cache, cache_win, layer, out_g, q_pos):
    bs = proj3.shape[0]
    lw = cache_win.shape[2]
    keep = min(WINDOW, lw + 1)
    n_pages = page_table.shape[1]
    halves = PAGE_SIZE // SEL_BLOCK

    def blk_spec(k):
        def imap(b, sel_r, pt_r):
            j = jnp.clip(sel_r[b, k], 0, n_pages * halves - 1)
            return (layer, pt_r[b, j // halves], j % halves, 0, 0)
        return pl.BlockSpec((None, None, SEL_BLOCK, NSA_ROWS, C_DHEAD), imap)

    row3 = lambda w, col: pl.BlockSpec((1, 1, w), lambda b, s_, p_: (b, 0, col))
    grid_spec = pltpu.PrefetchScalarGridSpec(
        num_scalar_prefetch=2,
        grid=(bs,),
        in_specs=[
            row3(C_WIDTH, COL_CQ // C_WIDTH),
            row3(4 * C_DHEAD, COL_CKV // (4 * C_DHEAD)),
            row3(2 * C_DHEAD, COL_WIN // (2 * C_DHEAD)),
            row3(LANES, COL_SMALL // LANES),
            pl.BlockSpec((1, 8, C_DHEAD), lambda b, s_, p_: (b, 0, 0)),
            pl.BlockSpec((None, None, lw, 2, C_DHEAD), lambda b, s_, p_: (layer, b, 0, 0, 0)),
            pl.BlockSpec((1, C_WIDTH), lambda b, s_, p_: (0, 0)),
        ] + [blk_spec(k) for k in range(SEL_TOPK)],
        out_specs=[
            pl.BlockSpec((1, 1, C_WIDTH), lambda b, s_, p_: (b, 0, 0)),
            pl.BlockSpec((None, keep, 2, C_DHEAD), lambda b, s_, p_: (b, 0, 0, 0)),
        ],
    )
    return pl.pallas_call(
        functools.partial(_nsa_attend_kernel, q_pos=q_pos),
        grid_spec=grid_spec,
        out_shape=[jax.ShapeDtypeStruct((bs, 1, C_WIDTH), F32), jax.ShapeDtypeStruct((bs, keep, 2, C_DHEAD), F32)],
        compiler_params=_cparams(("arbitrary",)),
        name="nsa_attend_sample",
    )(sel, page_table, proj3, proj3, proj3, proj3, ocmp, cache_win, out_g.reshape(1, C_WIDTH),
      *([cache] * SEL_TOPK))


def _token_mixer_tail(xa, xb, oa, ob, oc, w_out_b, ffn_g, wq_t, subkeys, u_tab, v_tab, tm, tn, et, emit_bf16=False,
                      layer=0):
    h, hn_t = _outproj_call(xa, xb, oa, ob, oc, w_out_b, ffn_g, tm)
    q_t = _peer_q_call(wq_t, hn_t, tm)
    route = _peer_route_call(q_t, subkeys)
    return h, _peer_expert_call(hn_t, route, u_tab, v_tab, tn, et, emit_bf16, layer)


def kernel(x_prompt, x_sample, cache_diff_kv, cache_nsa_kv, cache_nsa_win, state_ssm, state_conv, page_table,
           norm_mix_g, w_in, w_out, diff_lam, diff_subln_g, ssm_conv_w, ssm_conv_b, ssm_dt_bias, ssm_a_log,
           ssm_d, ssm_norm_g, nsa_pe, nsa_cmp_w1, nsa_cmp_w2, nsa_out_g, norm_ffn_g, peer_wq, peer_subkeys,
           peer_u, peer_v, norm_final_g):
    depth = w_in.shape[0]
    bp, t, _ = x_prompt.shape
    bs = x_sample.shape[0]
    past_len = page_table.shape[1] * PAGE_SIZE
    n_p = bp * t
    n_s = LANES
    tm_p = 512

    xa_p, xb_p = x_prompt.reshape(n_p, D_MODEL), None
    xa_s = jnp.pad(x_sample.reshape(bs, D_MODEL), ((0, n_s - bs), (0, 0)))
    xb_s = None
    st_p = [[] for _ in range(5)]
    st_s = [[] for _ in range(5)]
    state_ssm4 = state_ssm.reshape(depth, bs, B_HEADS * B_HEADDIM, B_STATE)
    w_in_t = jnp.transpose(w_in, (2, 0, 1))
    w_main = _cast_w_main_call(w_in_t)
    w_tail = _pack_w_tail_call(w_in_t)
    for l in range(depth):
        lam_init = 0.8 - 0.6 * math.exp(-0.3 * l)
        w_out_b = w_out[l].astype(BF16)
        wq_t = peer_wq[l].T.astype(BF16)

        proj_s, dkv_s, nkv_s = _proj_call(xa_s, xb_s, norm_mix_g[l], w_main, w_tail, l, n_s)
        proj_s = proj_s[:bs]
        proj_s3 = proj_s.reshape(bs, 1, PROJ_W)
        o_a = _diff_sample_call(page_table, proj_s3, cache_diff_kv, l, diff_lam[l], diff_subln_g[l], lam_init)
        o_b, h_new, conv_new = _ssd_sample_call(proj_s3, state_conv, state_ssm4, l, ssm_conv_w[l], ssm_conv_b[l],
                                                ssm_dt_bias[l], ssm_a_log[l], ssm_d[l], ssm_norm_g[l])
        part = _compress_paged_call(page_table, cache_nsa_kv, l, nsa_pe[l], nsa_cmp_w1[l])
        o_cmp, sel = _nsa_select_call(part, nsa_cmp_w2[l], proj_s3, past_len)
        o_c, win_out = _nsa_attend_call(sel[:, 0, :SEL_TOPK], page_table, proj_s3, o_cmp, cache_nsa_kv, cache_nsa_win, l,
                                        nsa_out_g[l], past_len)
        st_s[0].append(dkv_s[:bs].reshape(bs, 1, 2, A_HEADS, A_DHEAD))
        st_s[1].append(nkv_s[:bs].reshape(bs, 1, NSA_ROWS, C_DHEAD))
        st_s[2].append(win_out)
        st_s[3].append(h_new.reshape(bs, B_HEADS, B_HEADDIM, B_STATE))
        st_s[4].append(conv_new)
        pad = lambda a: jnp.pad(a.reshape(bs, -1), ((0, n_s - bs), (0, 0)))
        xa_s, (xb_s, u_b, vt_b) = _token_mixer_tail(xa_s, xb_s, pad(o_a), pad(o_b), pad(o_c), w_out_b, norm_ffn_g[l],
                                                   wq_t, peer_subkeys[l], peer_u, peer_v, n_s, n_s, 512, True, l)

        proj, dkv_p, nkv_p = _proj_call(xa_p, xb_p, norm_mix_g[l], w_main, w_tail, l, tm_p)
        o_a = _diff_prompt_call(proj, bp, t, diff_lam[l], diff_subln_g[l], lam_init)
        o_b, h_ssm = _ssd_prompt_call(proj, bp, t, ssm_conv_w[l], ssm_conv_b[l], ssm_dt_bias[l], ssm_a_log[l],
                                      ssm_d[l], ssm_norm_g[l])
        kc, vc = _compress_prompt_call(proj, bp, t, nsa_pe[l], nsa_cmp_w1[l], nsa_cmp_w2[l])
        o_c = _nsa_prompt_call(proj, kc, vc, bp, t, nsa_out_g[l])
        proj3 = proj.reshape(bp, t, PROJ_W)
        st_p[0].append(dkv_p.reshape(bp, t, 2, A_HEADS, A_DHEAD))
        st_p[1].append(nkv_p.reshape(bp, t, NSA_ROWS, C_DHEAD))
        keep = min(WINDOW, t)
        st_p[2].append(proj3[:, t - keep:, COL_WIN:COL_WIN + 2 * C_DHEAD].reshape(bp, keep, 2, C_DHEAD))
        st_p[3].append(h_ssm)
        st_p[4].append(proj3[:, t - (CONV_W - 1):, COL_XBC:COL_XBC + B_CONV_DIM])
        xa_p, xb_p = _token_mixer_tail(xa_p, xb_p, o_a, o_b, o_c, w_out_b, norm_ffn_g[l], wq_t, peer_subkeys[l],
                                       u_b, vt_b, tm_p, 512, 1024)

    y_p = _final_norm_call(xa_p, xb_p, norm_final_g, tm_p).reshape(bp, t, D_MODEL)
    y_s = _final_norm_call(xa_s, xb_s, norm_final_g, n_s)[:bs].reshape(bs, 1, D_MODEL)
    return (y_p, y_s) + tuple(jnp.stack(s) for s in st_p) + tuple(jnp.stack(s) for s in st_s)
```

```python
import functools
import math

import jax
import jax.numpy as jnp
from jax import lax
from jax.experimental import pallas as pl
from jax.experimental.pallas import tpu as pltpu

F32 = jnp.float32
BF16 = jnp.bfloat16

D_MODEL = 2048
A_HEADS = 4
A_HALF = 64
A_DHEAD = 128
A_WIDTH = 512
B_WIDTH = 1024
B_HEADDIM = 64
B_HEADS = 16
B_GROUPS = 4
B_STATE = 128
CONV_W = 4
B_CONV_DIM = 2048
SSD_CHUNK = 128
C_HEADS = 4
C_DHEAD = 128
C_WIDTH = 512
CMP_LEN = 32
CMP_STRIDE = 16
SEL_BLOCK = 64
SEL_TOPK = 16
WINDOW = 512
PEER_HEADS = 8
PEER_NKEYS = 128
PEER_TOPK = 16
PEER_DQ = 256
PAGE_SIZE = 128
NSA_ROWS = 4
NORM_EPS = 1e-6
NEG_INF = -1e30
FORCE_SCORE = 1e4

LANES = 128
VMEM_LIMIT = 56 * 1024 * 1024

COL_AQ = 0
COL_AK = 512
COL_AV = 1024
COL_Z = 1536
COL_XBC = 2560
COL_MAIN = 4608
COL_CQ = 4608
COL_CKV = 5120
COL_WIN = 5632
COL_SMALL = 5888
PROJ_W = 6144
GATE_LANE0 = B_HEADS

NT_DIMS = (((1,), (1,)), ((), ()))


def _cparams(sem, vmem=VMEM_LIMIT):
    return pltpu.CompilerParams(dimension_semantics=sem, vmem_limit_bytes=vmem)


def _gelu(x):
    return 0.5 * x * (1.0 + jnp.tanh(math.sqrt(2.0 / math.pi) * (x + 0.044715 * (x * x * x))))


def _silu(x):
    return x * (1.0 / (1.0 + jnp.exp(-x)))


def _softplus(x):
    return jnp.maximum(x, 0.0) + jnp.log(1.0 + jnp.exp(-jnp.abs(x)))


def _row_view(ref, rows):
    return ref.reshape(rows, ref.shape[-1])


def _alibi_slope(h):
    if isinstance(h, int):
        return 2.0 ** (-2.0 * (h + 1))
    return jnp.exp2(jnp.full((1, 1), -2.0, F32) * (h + 1).astype(F32))


def _proj_kernel(*refs, two, tc):
    if two:
        xa_ref, xb_ref, g_ref, wm_ref, wt_ref, o_ref, dkv_ref, nkv_ref, xn_ref = refs
    else:
        xa_ref, g_ref, wm_ref, wt_ref, o_ref, dkv_ref, nkv_ref, xn_ref = refs
    j = pl.program_id(1)
    n_main = COL_MAIN // tc

    @pl.when(j == 0)
    def _():
        x = xa_ref[...]
        if two:
            x = x + xb_ref[...]
        ms = jnp.mean(x * x, axis=-1, keepdims=True)
        xn_ref[...] = (x * lax.rsqrt(ms + NORM_EPS) * g_ref[...]).astype(BF16)

    def store_cache_rows(res, tile):
        for b in range(2 * A_HEADS):
            col = COL_AK + b * LANES
            if col // tc == tile:
                dkv_ref[:, b // A_HEADS, b % A_HEADS, :] = res[:, col % tc:col % tc + LANES]
        for b in range(NSA_ROWS):
            col = COL_CKV + b * LANES
            if col // tc == tile:
                nkv_ref[:, b, :] = res[:, col % tc:col % tc + LANES]

    state_tiles = sorted({(COL_AK + b * LANES) // tc for b in range(2 * A_HEADS)}
                         | {(COL_CKV + b * LANES) // tc for b in range(NSA_ROWS)})

    def finish(res, tiles):
        o_ref[...] = res
        for tile in tiles:
            if tile in state_tiles:
                pl.when(j == tile)(functools.partial(store_cache_rows, res, tile))

    @pl.when(j < n_main)
    def _():
        finish(jnp.dot(xn_ref[...], wm_ref[...], preferred_element_type=F32), range(n_main))

    @pl.when(j >= n_main)
    def _():
        finish(jnp.dot(xn_ref[...], wt_ref[...], preferred_element_type=F32), range(n_main, PROJ_W // tc))


def _cast_t_kernel(w_ref, o_ref):
    for l in range(w_ref.shape[1]):
        o_ref[l] = w_ref[:, l, :].T.astype(BF16)


def _cast_w_main_call(w_in_t, tr=384):
    depth = w_in_t.shape[1]
    return pl.pallas_call(
        _cast_t_kernel,
        grid=(COL_MAIN // tr,),
        in_specs=[pl.BlockSpec((tr, depth, D_MODEL), lambda j: (j, 0, 0))],
        out_specs=pl.BlockSpec((depth, D_MODEL, tr), lambda j: (0, 0, j)),
        out_shape=jax.ShapeDtypeStruct((depth, D_MODEL, COL_MAIN), BF16),
        compiler_params=_cparams(("parallel",)),
        name="w_in_cast",
    )(w_in_t)


def _pack_tail_kernel(w_ref, o_ref, *, n_in):
    width, depth, tk = w_ref.shape
    dt_w, gate_w = B_HEADS, 3 * C_HEADS
    cq0 = dt_w
    ckv0 = cq0 + C_WIDTH
    gate0 = n_in - COL_MAIN - gate_w
    pad = jnp.zeros((PROJ_W - n_in, tk), F32)
    for l in range(depth):
        x = w_ref[:, l, :]
        packed = jnp.concatenate([x[cq0:ckv0], x[ckv0:gate0], x[0:dt_w], x[gate0:gate0 + gate_w], pad], axis=0)
        o_ref[l] = packed.T.astype(BF16)


def _pack_w_tail_call(w_in_t, tk=512):
    width = PROJ_W - COL_MAIN
    depth = w_in_t.shape[1]
    assert COL_MAIN % width == 0
    return pl.pallas_call(
        functools.partial(_pack_tail_kernel, n_in=w_in_t.shape[0]),
        grid=(D_MODEL // tk,),
        in_specs=[pl.BlockSpec((width, depth, tk), lambda k: (COL_MAIN // width, 0, k))],
        out_specs=pl.BlockSpec((depth, tk, width), lambda k: (0, k, 0)),
        out_shape=jax.ShapeDtypeStruct((depth, D_MODEL, width), BF16),
        compiler_params=_cparams(("parallel",)),
        name="w_in_tail_pack",
    )(w_in_t)


def _proj_call(xa, xb, g, w_main, w_tail, layer, tm):
    n = xa.shape[0]
    two = xb is not None
    tc = 768 if two else 1536
    n_main = COL_MAIN // tc
    assert COL_MAIN % tc == 0 and w_tail.shape[2] == PROJ_W - COL_MAIN
    xspec = pl.BlockSpec((tm, D_MODEL), lambda i, j: (i, 0))
    ins = [xa] + ([xb] if two else []) + [g.reshape(1, D_MODEL), w_main, w_tail]
    specs = [xspec] + ([xspec] if two else []) + [
        pl.BlockSpec((1, D_MODEL), lambda i, j: (0, 0)),
        pl.BlockSpec((None, D_MODEL, tc), lambda i, j: (layer, 0, jnp.minimum(j, n_main - 1))),
        pl.BlockSpec((None, D_MODEL, tc), lambda i, j: (layer, 0, jnp.maximum(j - n_main, 0))),
    ]
    return pl.pallas_call(
        functools.partial(_proj_kernel, two=two, tc=tc),
        grid=(n // tm, PROJ_W // tc),
        in_specs=specs,
        out_specs=[pl.BlockSpec((tm, tc), lambda i, j: (i, j)),
                   pl.BlockSpec((tm, 2, A_HEADS, A_DHEAD), lambda i, j: (i, 0, 0, 0)),
                   pl.BlockSpec((tm, NSA_ROWS, C_DHEAD), lambda i, j: (i, 0, 0))],
        out_shape=[jax.ShapeDtypeStruct((n, PROJ_W), F32), jax.ShapeDtypeStruct((n, 2, A_HEADS, A_DHEAD), F32),
                   jax.ShapeDtypeStruct((n, NSA_ROWS, C_DHEAD), F32)],
        scratch_shapes=[pltpu.VMEM((tm, D_MODEL), BF16)],
        compiler_params=_cparams(("parallel", "arbitrary")),
        name="in_proj",
    )(*ins)


def _diff_lambda(dl, lam_init):
    a = jnp.sum(dl[0:1] * dl[1:2], axis=-1, keepdims=True)
    b = jnp.sum(dl[2:3] * dl[3:4], axis=-1, keepdims=True)
    return jnp.exp(a) - jnp.exp(b) + lam_init


CAUSAL_LEVELS = 4


def _causal_prefixes(i, nq, tq, body):
    levels = min(CAUSAL_LEVELS, nq)
    per = nq // levels
    for lv in range(levels):
        pl.when(i // per == lv)(functools.partial(body, (lv + 1) * per * tq))


def _diff_prompt_kernel(q_ref, k_ref, v_ref, dl_ref, g_ref, o_ref, *, tq, lam_init):
    h = pl.program_id(1)
    i = pl.program_id(2)
    t = k_ref.shape[0]
    scale = A_HALF ** -0.5

    def body(nk):
        lam = _diff_lambda(dl_ref[...], lam_init)
        q = q_ref[...] * scale
        lane = lax.broadcasted_iota(jnp.int32, (1, A_DHEAD), 1)
        kb = k_ref[0:nk, :].astype(BF16)
        vb = v_ref[0:nk, :].astype(BF16)
        qpos = i * tq + lax.broadcasted_iota(jnp.int32, (tq, 1), 0)
        kpos = lax.broadcasted_iota(jnp.int32, (1, nk), 1)
        ok = qpos >= kpos
        key_bias = _alibi_slope(h) * kpos.astype(F32)

        def half_attention(c):
            qc = jnp.where((lane >= c * A_HALF) & (lane < (c + 1) * A_HALF), q, 0.0).astype(BF16)
            s = lax.dot_general(qc, kb, NT_DIMS, preferred_element_type=F32) + key_bias
            s = jnp.where(ok, s, NEG_INF)
            e = jnp.exp(s - jnp.max(s, axis=-1, keepdims=True))
            pv = jnp.dot(e.astype(BF16), vb, preferred_element_type=F32)
            return pv / jnp.sum(e, axis=-1, keepdims=True)

        o = half_attention(0) - lam * half_attention(1)
        ms = jnp.mean(o * o, axis=-1, keepdims=True)
        o_ref[...] = o * lax.rsqrt(ms + NORM_EPS) * g_ref[...] * (1.0 - lam_init)

    _causal_prefixes(i, t // tq, tq, body)


def _diff_prompt_call(proj, bsz, t, dl, subln_g, lam_init, tq=256):
    nq = t // tq
    cq, ck, cv = COL_AQ // A_DHEAD, COL_AK // A_DHEAD, COL_AV // A_DHEAD
    return pl.pallas_call(
        functools.partial(_diff_prompt_kernel, tq=tq, lam_init=lam_init),
        grid=(bsz, A_HEADS, nq),
        in_specs=[
            pl.BlockSpec((tq, A_DHEAD), lambda b, h, i: (b * nq + i, cq + h)),
            pl.BlockSpec((t, A_DHEAD), lambda b, h, i: (b, ck + h)),
            pl.BlockSpec((t, A_DHEAD), lambda b, h, i: (b, cv + h)),
            pl.BlockSpec((4, A_HALF), lambda b, h, i: (0, 0)),
            pl.BlockSpec((1, A_DHEAD), lambda b, h, i: (0, 0)),
        ],
        out_specs=pl.BlockSpec((tq, A_DHEAD), lambda b, h, i: (b * nq + i, h)),
        out_shape=jax.ShapeDtypeStruct((bsz * t, A_WIDTH), F32),
        compiler_params=_cparams(("parallel", "parallel", "arbitrary")),
        name="diff_attn_prompt",
    )(proj, proj, proj, dl, subln_g.reshape(1, A_DHEAD))


SSD_COLS = 512


def _ssd_prompt_kernel(*refs):
    nx, nz = B_CONV_DIM // SSD_COLS, B_WIDTH // SSD_COLS
    xbc_refs, z_refs = refs[:nx], refs[nx:nx + nz]
    sm_ref, cw_ref, cb_ref, dtb_ref, alog_ref, dsk_ref, g_ref, o_ref, hout_ref, buf_ref, h_ref = refs[nx + nz:]
    c = pl.program_id(1)
    cs = SSD_CHUNK

    @pl.when(c == 0)
    def _():
        buf_ref[0:8, :] = jnp.zeros((8, B_CONV_DIM), F32)
        h_ref[...] = jnp.zeros_like(h_ref)

    xbc = jnp.concatenate([r[...] for r in xbc_refs], axis=1)
    buf_ref[8:8 + cs, :] = xbc
    cw = cw_ref[...]
    conv = cb_ref[...] + cw[3:4] * xbc
    for j in range(1, CONV_W):
        conv = conv + cw[3 - j:4 - j] * buf_ref[8 - j:8 - j + cs, :]
    buf_ref[0:8, :] = xbc[cs - 8:cs, :]
    xc = _silu(conv)
    xs = xc[:, :B_WIDTH]

    dt = _softplus(sm_ref[...] + dtb_ref[...])
    a_neg = -jnp.exp(alog_ref[...])
    dta = dt * a_neg
    row = lax.broadcasted_iota(jnp.int32, (cs, cs), 0)
    col = lax.broadcasted_iota(jnp.int32, (cs, cs), 1)
    causal = row >= col
    acum = jnp.dot(causal.astype(F32), dta, preferred_element_type=F32, precision=lax.Precision.HIGHEST)
    acum_t = acum.T
    dt_t = dt.T
    lane = lax.broadcasted_iota(jnp.int32, (1, LANES), 1)
    lo = lane < B_HEADDIM

    ys = []
    for g in range(B_GROUPS):
        bg = xc[:, B_WIDTH + g * B_STATE:B_WIDTH + (g + 1) * B_STATE]
        cg = xc[:, B_WIDTH + B_GROUPS * B_STATE + g * B_STATE:B_WIDTH + B_GROUPS * B_STATE + (g + 1) * B_STATE]
        bgb = bg.astype(BF16)
        cgb = cg.astype(BF16)
        cb = lax.dot_general(cgb, bgb, NT_DIMS, preferred_element_type=F32)
        for pr in range(2):
            h0 = g * 4 + pr * 2
            xpair = xs[:, h0 * B_HEADDIM:(h0 + 2) * B_HEADDIM]
            xpb = xpair.astype(BF16)
            ydiag = []
            ecol = []
            wcol = []
            elast = []
            for hh in (h0, h0 + 1):
                a_col = acum[:, hh:hh + 1]
                a_row = acum_t[hh:hh + 1, :]
                decay = jnp.exp(jnp.where(causal, a_col - a_row, NEG_INF))
                lm = cb * decay * dt_t[hh:hh + 1, :]
                ydiag.append(jnp.dot(lm.astype(BF16), xpb, preferred_element_type=F32))
                a_last = acum[cs - 1:cs, hh:hh + 1]
                ecol.append(jnp.exp(a_col))
                wcol.append(jnp.exp(a_last - a_col) * dt[:, hh:hh + 1])
                elast.append(jnp.exp(a_last))
            hp = h_ref[h0 * B_HEADDIM:(h0 + 2) * B_HEADDIM, :]
            yoff = lax.dot_general(cgb, hp.astype(BF16), NT_DIMS, preferred_element_type=F32)
            y = jnp.where(lo, ydiag[0], ydiag[1]) + yoff * jnp.where(lo, ecol[0], ecol[1])
            ys.append(y)
            wx = xpair * jnp.where(lo, wcol[0], wcol[1])
            upd = jnp.dot(wx.T.astype(BF16), bgb, preferred_element_type=F32)
            prow = lax.broadcasted_iota(jnp.int32, (LANES, 1), 0) < B_HEADDIM
            h_ref[h0 * B_HEADDIM:(h0 + 2) * B_HEADDIM, :] = jnp.where(prow, elast[0], elast[1]) * hp + upd

    y = jnp.concatenate(ys, axis=1)
    y = y + dsk_ref[...] * xs
    y = y * _silu(jnp.concatenate([r[...] for r in z_refs], axis=1))
    ms = jnp.mean(y * y, axis=-1, keepdims=True)
    o_ref[...] = y * lax.rsqrt(ms + NORM_EPS) * g_ref[...]

    @pl.when(c == pl.num_programs(1) - 1)
    def _():
        hout_ref[0] = h_ref[...]


def _pad_lanes(v, fill=0.0):
    v = v.reshape(1, -1).astype(F32)
    return jnp.pad(v, ((0, 0), (0, LANES - v.shape[1])), constant_values=fill)


def _ssd_prompt_call(proj, bsz, t, conv_w, conv_b, dt_bias, a_log, d_skip, norm_g):
    nc = t // SSD_CHUNK
    cs = SSD_CHUNK
    const = lambda b, c: (0, 0)
    o, hout = pl.pallas_call(
        _ssd_prompt_kernel,
        grid=(bsz, nc),
        in_specs=[
            *[pl.BlockSpec((cs, SSD_COLS), functools.partial(lambda k, b, c: (b * nc + c, COL_XBC // SSD_COLS + k), k))
              for k in range(B_CONV_DIM // SSD_COLS)],
            *[pl.BlockSpec((cs, SSD_COLS), functools.partial(lambda k, b, c: (b * nc + c, COL_Z // SSD_COLS + k), k))
              for k in range(B_WIDTH // SSD_COLS)],
            pl.BlockSpec((cs, LANES), lambda b, c: (b * nc + c, COL_SMALL // LANES)),
            pl.BlockSpec((CONV_W, B_CONV_DIM), const),
            pl.BlockSpec((1, B_CONV_DIM), const),
            pl.BlockSpec((1, LANES), const),
            pl.BlockSpec((1, LANES), const),
            pl.BlockSpec((1, B_WIDTH), const),
            pl.BlockSpec((1, B_WIDTH), const),
        ],
        out_specs=[
            pl.BlockSpec((cs, B_WIDTH), lambda b, c: (b * nc + c, 0)),
            pl.BlockSpec((1, B_HEADS * B_HEADDIM, B_STATE), lambda b, c: (b, 0, 0)),
        ],
        out_shape=[
            jax.ShapeDtypeStruct((bsz * t, B_WIDTH), F32),
            jax.ShapeDtypeStruct((bsz, B_HEADS * B_HEADDIM, B_STATE), F32),
        ],
        scratch_shapes=[pltpu.VMEM((8 + cs, B_CONV_DIM), F32), pltpu.VMEM((B_HEADS * B_HEADDIM, B_STATE), F32)],
        compiler_params=_cparams(("parallel", "arbitrary")),
        name="ssd_prompt",
    )(*([proj] * (B_CONV_DIM // SSD_COLS + B_WIDTH // SSD_COLS + 1)), conv_w, conv_b.reshape(1, -1),
      _pad_lanes(dt_bias), _pad_lanes(a_log), jnp.repeat(d_skip, B_HEADDIM).reshape(1, B_WIDTH),
      norm_g.reshape(1, B_WIDTH))
    return o, hout.reshape(bsz, B_HEADS, B_HEADDIM, B_STATE)


def _compress_kernel(k_ref, v_ref, pe_ref, w1_ref, w2_ref, kc_ref, vc_ref, *, nchunk):
    outs = []
    for kv, rows_ref in enumerate((k_ref, v_ref)):
        acc_lo = jnp.zeros((nchunk, C_DHEAD), F32)
        acc_hi = jnp.zeros((nchunk, C_DHEAD), F32)
        for r in range(CMP_STRIDE):
            x = rows_ref[pl.ds(r, nchunk, stride=CMP_STRIDE), :]
            x_lo = (x + pe_ref[kv, r:r + 1, :]).astype(BF16)
            x_hi = (x + pe_ref[kv, CMP_STRIDE + r:CMP_STRIDE + r + 1, :]).astype(BF16)
            acc_lo = acc_lo + jnp.dot(x_lo, w1_ref[kv, r].astype(BF16), preferred_element_type=F32)
            acc_hi = acc_hi + jnp.dot(x_hi, w1_ref[kv, CMP_STRIDE + r].astype(BF16), preferred_element_type=F32)
        hid = acc_lo + pltpu.roll(acc_hi, nchunk - 1, 0)
        outs.append(jnp.dot(_gelu(hid).astype(BF16), w2_ref[kv].astype(BF16), preferred_element_type=F32))
    kc_ref[0] = outs[0]
    vc_ref[0] = outs[1]


def _compress_prompt_call(proj, bsz, t, pe, w1, w2):
    nchunk = t // CMP_STRIDE
    shp = jax.ShapeDtypeStruct((bsz, nchunk, C_DHEAD), F32)
    return pl.pallas_call(
        functools.partial(_compress_kernel, nchunk=nchunk),
        grid=(bsz,),
        in_specs=[
            pl.BlockSpec((t, C_DHEAD), lambda b: (b, COL_CKV // C_DHEAD)),
            pl.BlockSpec((t, C_DHEAD), lambda b: (b, COL_CKV // C_DHEAD + 1)),
            pl.BlockSpec((2, CMP_LEN, C_DHEAD), lambda b: (0, 0, 0)),
            pl.BlockSpec((2, CMP_LEN, C_DHEAD, C_DHEAD), lambda b: (0, 0, 0, 0)),
            pl.BlockSpec((2, C_DHEAD, C_DHEAD), lambda b: (0, 0, 0)),
        ],
        out_specs=[pl.BlockSpec((1, nchunk, C_DHEAD), lambda b: (b, 0, 0))] * 2,
        out_shape=[shp, shp],
        compiler_params=_cparams(("parallel",)),
        name="nsa_compress_prompt",
    )(proj, proj, pe, w1.reshape(2, CMP_LEN, C_DHEAD, C_DHEAD), w2)


def _masked_softmax(s, ok):
    s = jnp.where(ok, s, NEG_INF)
    m = jnp.max(s, axis=-1, keepdims=True)
    e = jnp.where(ok, jnp.exp(s - m), 0.0)
    return e, jnp.sum(e, axis=-1, keepdims=True)


def _topk_mask_lanes(score, k, n):
    lane = lax.broadcasted_iota(jnp.int32, (1, LANES), 1)
    rank = jnp.zeros(score.shape, F32)
    for i in range(n):
        ci = score[:, i:i + 1]
        beats = (ci > score) | ((ci == score) & (lane > i))
        rank = rank + jnp.where(beats, 1.0, 0.0)
    return (rank < k) & (lane < n)


def _nsa_prompt_kernel(q_ref, ks_ref, vs_ref, kw_ref, vw_ref, kc_ref, vc_ref, sm_ref, g_ref, o_ref, osel_ref, *, tq, n_cmp):
    i = pl.program_id(1)
    t = ks_ref.shape[0]
    n_sel = t // SEL_BLOCK
    scale = C_DHEAD ** -0.5
    sel_shift = SEL_BLOCK.bit_length() - 1
    wlen = min(t, WINDOW + tq)

    qpos = i * tq + lax.broadcasted_iota(jnp.int32, (tq, 1), 0)
    lane = lax.broadcasted_iota(jnp.int32, (1, LANES), 1)

    cmp_end = lane * CMP_STRIDE + (CMP_LEN - 1)
    dist_c = qpos - cmp_end
    ok_c = (dist_c >= 0) & (lane < n_cmp)
    dist_cf = dist_c.astype(F32)
    kcb = kc_ref[0].astype(BF16)
    vcb = vc_ref[0].astype(BF16)
    qs = [(q_ref[:, h * C_DHEAD:(h + 1) * C_DHEAD] * scale).astype(BF16) for h in range(C_HEADS)]
    o_cmp = []
    psum = jnp.zeros((tq, LANES), F32)
    for h in range(C_HEADS):
        s = lax.dot_general(qs[h], kcb, NT_DIMS, preferred_element_type=F32)
        s = s - _alibi_slope(h) * dist_cf
        e, den = _masked_softmax(s, ok_c)
        p = e / jnp.maximum(den, 1e-30)
        psum = psum + p
        o_cmp.append(jnp.dot(p.astype(BF16), vcb, preferred_element_type=F32))

    n_i = lax.broadcasted_iota(jnp.int32, (LANES, LANES), 0)
    j_i = lax.broadcasted_iota(jnp.int32, (LANES, LANES), 1)
    lo_ = jnp.maximum(n_i * CMP_STRIDE, j_i * SEL_BLOCK)
    hi_ = jnp.minimum(n_i * CMP_STRIDE + CMP_LEN, (j_i + 1) * SEL_BLOCK)
    ovl = jnp.maximum(hi_ - lo_, 0).astype(F32) * (1.0 / CMP_LEN)
    ovl = jnp.where((n_i < n_cmp) & (j_i < n_sel), ovl, 0.0)
    imp = jnp.dot(psum, ovl, preferred_element_type=F32, precision=lax.Precision.HIGHEST)
    qblk = qpos >> sel_shift
    sel_valid = lane <= qblk
    forced = (lane == 0) | (lane == qblk) | (lane == qblk - 1)
    score = jnp.where(sel_valid, imp + jnp.where(forced, FORCE_SCORE, 0.0), NEG_INF)
    score = jnp.where(lane < n_sel, score, -jnp.inf)
    chosen = _topk_mask_lanes(score, min(SEL_TOPK, n_sel), n_sel) & sel_valid
    chosen_b = jnp.where(chosen, 1.0, 0.0).astype(BF16)

    def attend(qh, kb, vb, key_bias, ok):
        s = lax.dot_general(qh, kb, NT_DIMS, preferred_element_type=F32) + key_bias
        s = jnp.where(ok, s, NEG_INF)
        e = jnp.exp(s - jnp.max(s, axis=-1, keepdims=True))
        return jnp.dot(e.astype(BF16), vb, preferred_element_type=F32) / jnp.sum(e, axis=-1, keepdims=True)

    def selected_branch(nk):
        e_j = lax.broadcasted_iota(jnp.int32, (LANES, nk), 0)
        e_k = lax.broadcasted_iota(jnp.int32, (LANES, nk), 1)
        expand = jnp.where((e_k >> sel_shift) == e_j, 1.0, 0.0).astype(BF16)
        key_sel = jnp.dot(chosen_b, expand, preferred_element_type=F32) > 0.5
        kpos = lax.broadcasted_iota(jnp.int32, (1, nk), 1)
        kpos_f = kpos.astype(F32)
        ok_s = key_sel & (qpos >= kpos)
        ksb = ks_ref[0:nk, :].astype(BF16)
        vsb = vs_ref[0:nk, :].astype(BF16)
        for h in range(C_HEADS):
            osel_ref[:, h * C_DHEAD:(h + 1) * C_DHEAD] = attend(qs[h], ksb, vsb, _alibi_slope(h) * kpos_f, ok_s)

    nq = t // tq
    if nq % 2 == 0:
        pl.when(i < nq // 2)(functools.partial(selected_branch, t // 2))
        pl.when(i >= nq // 2)(functools.partial(selected_branch, t))
    else:
        selected_branch(t)

    w0 = pl.multiple_of(jnp.clip(i * tq - WINDOW, 0, t - wlen), 8)
    wpos = w0 + lax.broadcasted_iota(jnp.int32, (1, wlen), 1)
    wpos_f = wpos.astype(F32)
    dist_w = qpos - wpos
    ok_w = (dist_w >= 0) & (dist_w < WINDOW)
    kwb = kw_ref[pl.ds(w0, wlen), :].astype(BF16)
    vwb = vw_ref[pl.ds(w0, wlen), :].astype(BF16)
    gate = 1.0 / (1.0 + jnp.exp(-sm_ref[...]))
    outs = []
    for h in range(C_HEADS):
        o_sel = osel_ref[:, h * C_DHEAD:(h + 1) * C_DHEAD]
        o_win = attend(qs[h], kwb, vwb, _alibi_slope(h) * wpos_f, ok_w)
        g0 = gate[:, GATE_LANE0 + h:GATE_LANE0 + h + 1]
        g1 = gate[:, GATE_LANE0 + C_HEADS + h:GATE_LANE0 + C_HEADS + h + 1]
        g2 = gate[:, GATE_LANE0 + 2 * C_HEADS + h:GATE_LANE0 + 2 * C_HEADS + h + 1]
        outs.append(g0 * o_cmp[h] + g1 * o_sel + g2 * o_win)
    o = jnp.concatenate(outs, axis=1)
    ms = jnp.mean(o * o, axis=-1, keepdims=True)
    o_ref[...] = o * lax.rsqrt(ms + NORM_EPS) * g_ref[...]


def _nsa_prompt_call(proj, kc, vc, bsz, t, out_g, tq=256):
    nq = t // tq
    n_cmp = (t - CMP_LEN) // CMP_STRIDE + 1
    c0 = COL_CKV // C_DHEAD
    w0 = COL_WIN // C_DHEAD
    kvspec = lambda col: pl.BlockSpec((t, C_DHEAD), lambda b, i: (b, col))
    return pl.pallas_call(
        functools.partial(_nsa_prompt_kernel, tq=tq, n_cmp=n_cmp),
        grid=(bsz, nq),
        in_specs=[
            pl.BlockSpec((tq, C_WIDTH), lambda b, i: (b * nq + i, COL_CQ // C_WIDTH)),
            kvspec(c0 + 2), kvspec(c0 + 3), kvspec(w0), kvspec(w0 + 1),
            pl.BlockSpec((1, kc.shape[1], C_DHEAD), lambda b, i: (b, 0, 0)),
            pl.BlockSpec((1, kc.shape[1], C_DHEAD), lambda b, i: (b, 0, 0)),
            pl.BlockSpec((tq, LANES), lambda b, i: (b * nq + i, COL_SMALL // LANES)),
            pl.BlockSpec((1, C_WIDTH), lambda b, i: (0, 0)),
        ],
        out_specs=pl.BlockSpec((tq, C_WIDTH), lambda b, i: (b * nq + i, 0)),
        out_shape=jax.ShapeDtypeStruct((bsz * t, C_WIDTH), F32),
        scratch_shapes=[pltpu.VMEM((tq, C_WIDTH), F32)],
        compiler_params=_cparams(("parallel", "arbitrary")),
        name="nsa_attn_prompt",
    )(proj, proj, proj, proj, proj, kc, vc, proj, out_g.reshape(1, C_WIDTH))


def _outproj_kernel(*refs, two):
    if two:
        xa_ref, xb_ref, oa_ref, ob_ref, oc_ref, w_ref, g_ref, h_ref, hnt_ref = refs
    else:
        xa_ref, oa_ref, ob_ref, oc_ref, w_ref, g_ref, h_ref, hnt_ref = refs
    x = xa_ref[...]
    if two:
        x = x + xb_ref[...]
    mixed = jnp.dot(oa_ref[...].astype(BF16), w_ref[0:A_WIDTH, :], preferred_element_type=F32)
    mixed = mixed + jnp.dot(ob_ref[...].astype(BF16), w_ref[A_WIDTH:A_WIDTH + B_WIDTH, :], preferred_element_type=F32)
    mixed = mixed + jnp.dot(oc_ref[...].astype(BF16), w_ref[A_WIDTH + B_WIDTH:, :], preferred_element_type=F32)
    h = x + mixed
    h_ref[...] = h
    ms = jnp.mean(h * h, axis=-1, keepdims=True)
    hnt_ref[...] = (h * lax.rsqrt(ms + NORM_EPS) * g_ref[...]).T.astype(BF16)


def _outproj_call(xa, xb, oa, ob, oc, w_out_b, g, tm):
    n = xa.shape[0]
    two = xb is not None
    row = lambda w: pl.BlockSpec((tm, w), lambda i: (i, 0))
    ins = [xa] + ([xb] if two else []) + [oa, ob, oc, w_out_b, g.reshape(1, D_MODEL)]
    specs = [row(D_MODEL)] + ([row(D_MODEL)] if two else []) + [
        row(A_WIDTH), row(B_WIDTH), row(C_WIDTH),
        pl.BlockSpec((D_MODEL, D_MODEL), lambda i: (0, 0)),
        pl.BlockSpec((1, D_MODEL), lambda i: (0, 0)),
    ]
    return pl.pallas_call(
        functools.partial(_outproj_kernel, two=two),
        grid=(n // tm,),
        in_specs=specs,
        out_specs=[row(D_MODEL), pl.BlockSpec((D_MODEL, tm), lambda i: (0, i))],
        out_shape=[jax.ShapeDtypeStruct((n, D_MODEL), F32), jax.ShapeDtypeStruct((D_MODEL, n), BF16)],
        compiler_params=_cparams(("parallel",)),
        name="out_proj",
    )(*ins)


def _peer_q_kernel(wqt_ref, hnt_ref, qt_ref):
    qt_ref[...] = jnp.dot(wqt_ref[...], hnt_ref[...], preferred_element_type=F32)


def _peer_q_call(wq_t, hn_t, tm):
    n = hn_t.shape[1]
    dq = wq_t.shape[0]
    return pl.pallas_call(
        _peer_q_kernel,
        grid=(n // tm,),
        in_specs=[pl.BlockSpec((dq, D_MODEL), lambda i: (0, 0)), pl.BlockSpec((D_MODEL, tm), lambda i: (0, i))],
        out_specs=pl.BlockSpec((dq, tm), lambda i: (0, i)),
        out_shape=jax.ShapeDtypeStruct((dq, n), F32),
        compiler_params=_cparams(("parallel",)),
        name="peer_query",
    )(wq_t, hn_t)


NOT_RANKED = 99.0


def _top_rows(s, pos, k, want_rank):
    rank = jnp.full(s.shape, NOT_RANKED, F32) if want_rank else None
    vals, picks = [], []
    for j in range(k):
        m = jnp.max(s, axis=0, keepdims=True)
        idx = jnp.min(jnp.where(s == m, pos, 1e9), axis=0, keepdims=True)
        hit = pos == idx
        if want_rank:
            rank = jnp.where(hit, float(j), rank)
        s = jnp.where(hit, -jnp.inf, s)
        vals.append(m)
        picks.append(idx)
    return jnp.concatenate(vals, axis=0), jnp.concatenate(picks, axis=0), rank


PAIR_ROWS = PEER_TOPK + 7 * 8 + 8


def _pair_candidates(v1, v2):
    tn = v1.shape[1]
    parts = [v1[0:1, :] + v2] + [v1[a:a + 1, :] + v2[0:8, :] for a in range(1, 8)] + [v1[8:16, :] + v2[0:1, :]]
    r = lax.broadcasted_iota(jnp.int32, (PAIR_ROWS, tn), 0)
    mid = r - PEER_TOPK
    pos = jnp.where(r < PEER_TOPK, r,
                    jnp.where(r < PEER_TOPK + 56, ((mid >> 3) + 1) * PEER_TOPK + (mid & 7), (r - 64) * PEER_TOPK))
    return jnp.concatenate(parts, axis=0), pos.astype(F32)


def _peer_route_kernel(qt_ref, sk_ref, lim_ref, coef_ref, rank2_ref, e2_ref, *, heads):
    half = PEER_DQ // 2
    row = lax.broadcasted_iota(jnp.int32, (PEER_NKEYS, qt_ref.shape[1]), 0).astype(F32)
    for hh in range(heads):
        q = qt_ref[hh * PEER_DQ:(hh + 1) * PEER_DQ, :]
        s1 = jnp.dot(sk_ref[hh, 0], q[0:half, :], preferred_element_type=F32, precision=lax.Precision.HIGHEST)
        s2 = jnp.dot(sk_ref[hh, 1], q[half:, :], preferred_element_type=F32, precision=lax.Precision.HIGHEST)
        v1, _, rank1 = _top_rows(s1, row, PEER_TOPK, True)
        v2, _, rank2 = _top_rows(s2, row, PEER_TOPK, True)
        cand, cpos = _pair_candidates(v1, v2)
        top, pos, _ = _top_rows(cand, cpos, PEER_TOPK, False)
        z = jnp.sum(jnp.exp(top - top[0:1, :]), axis=0, keepdims=True)
        a_of = jnp.floor(pos * (1.0 / PEER_TOPK))
        lim = jnp.zeros(s1.shape, F32)
        for a in range(PEER_TOPK):
            cnt = jnp.sum(jnp.where(a_of == float(a), 1.0, 0.0), axis=0, keepdims=True)
            lim = jnp.where(rank1 == float(a), cnt, lim)
        lim_ref[hh] = lim
        coef_ref[hh] = jnp.exp(s1 - v1[0:1, :]) / z
        rank2_ref[hh] = rank2.astype(BF16)
        e2_ref[hh] = jnp.exp(s2 - v2[0:1, :]).astype(BF16)


def _peer_route_call(q_t, subkeys, tn=LANES, heads=4):
    n = q_t.shape[1]
    shp = lambda dt: jax.ShapeDtypeStruct((PEER_HEADS, PEER_NKEYS, n), dt)
    ospec = pl.BlockSpec((heads, PEER_NKEYS, tn), lambda j, h: (h, 0, j))
    return pl.pallas_call(
        functools.partial(_peer_route_kernel, heads=heads),
        grid=(n // tn, PEER_HEADS // heads),
        in_specs=[
            pl.BlockSpec((heads * PEER_DQ, tn), lambda j, h: (h, j)),
            pl.BlockSpec((heads, 2, PEER_NKEYS, PEER_DQ // 2), lambda j, h: (h, 0, 0, 0)),
        ],
        out_specs=[ospec] * 4,
        out_shape=[shp(F32), shp(F32), shp(BF16), shp(BF16)],
        compiler_params=_cparams(("parallel", "arbitrary")),
        name="peer_route",
    )(q_t, subkeys)


def _peer_expert_kernel(hnt_ref, lim_ref, coef_ref, rank2_ref, e2_ref, u_ref, v_ref, o_ref, *rest, et, emit):
    acc_ref = rest[-1]
    t = pl.program_id(1)

    @pl.when(t == 0)
    def _():
        acc_ref[...] = jnp.zeros_like(acc_ref)

    if emit:
        ub = u_ref[...].astype(BF16)
        vtb = v_ref[...].T.astype(BF16)
        rest[0][...] = ub
        rest[1][...] = vtb
    else:
        ub = u_ref[...]
        vtb = v_ref[...]
    tn = hnt_ref.shape[1]
    hid = jnp.dot(ub, hnt_ref[...], preferred_element_type=F32)
    acts = []
    for ii in range(et // PEER_NKEYS):
        i1 = t * (et // PEER_NKEYS) + ii
        gate = jnp.zeros((PEER_NKEYS, tn), BF16)
        for h in range(PEER_HEADS):
            lim = lim_ref[h, pl.ds(i1, 1), :].astype(BF16)
            coef = coef_ref[h, pl.ds(i1, 1), :].astype(BF16)
            gate = gate + jnp.where(rank2_ref[h] < lim, e2_ref[h], jnp.zeros((), BF16)) * coef
        acts.append(gate * _gelu(hid[ii * PEER_NKEYS:(ii + 1) * PEER_NKEYS, :]).astype(BF16))
    acc_ref[...] += jnp.dot(vtb, jnp.concatenate(acts, axis=0), preferred_element_type=F32)

    @pl.when(t == pl.num_programs(1) - 1)
    def _():
        o_ref[...] = acc_ref[...].T


def _peer_expert_call(hn_t, route, u_tab, v_tab, tn, et, emit=False, layer=0):
    n = hn_t.shape[1]
    n_exp = u_tab.shape[-2] if emit else u_tab.shape[0]
    rspec = pl.BlockSpec((PEER_HEADS, PEER_NKEYS, tn), lambda j, t: (0, 0, j))
    uspec = pl.BlockSpec((et, D_MODEL), lambda j, t: (t, 0))
    vtspec = pl.BlockSpec((D_MODEL, et), lambda j, t: (0, t))
    ospec = pl.BlockSpec((tn, D_MODEL), lambda j, t: (j, 0))
    oshape = jax.ShapeDtypeStruct((n, D_MODEL), F32)
    if emit:
        assert n == tn
        out_specs = [ospec, uspec, vtspec]
        out_shape = [oshape, jax.ShapeDtypeStruct((n_exp, D_MODEL), BF16), jax.ShapeDtypeStruct((D_MODEL, n_exp), BF16)]
    else:
        out_specs, out_shape = ospec, oshape
    return pl.pallas_call(
        functools.partial(_peer_expert_kernel, et=et, emit=emit),
        grid=(n // tn, n_exp // et),
        in_specs=[pl.BlockSpec((D_MODEL, tn), lambda j, t: (0, j)), rspec, rspec, rspec, rspec]
        + ([pl.BlockSpec((None, et, D_MODEL), lambda j, t: (layer, t, 0))] * 2 if emit else [uspec, vtspec]),
        out_specs=out_specs,
        out_shape=out_shape,
        scratch_shapes=[pltpu.VMEM((D_MODEL, tn), F32)],
        compiler_params=_cparams(("parallel", "arbitrary")),
        name="peer_experts",
    )(hn_t, *route, u_tab, v_tab)


def _final_norm_kernel(xa_ref, xb_ref, g_ref, o_ref):
    x = xa_ref[...] + xb_ref[...]
    ms = jnp.mean(x * x, axis=-1, keepdims=True)
    o_ref[...] = x * lax.rsqrt(ms + NORM_EPS) * g_ref[...]


def _final_norm_call(xa, xb, g, tm):
    n = xa.shape[0]
    row = pl.BlockSpec((tm, D_MODEL), lambda i: (i, 0))
    return pl.pallas_call(
        _final_norm_kernel,
        grid=(n // tm,),
        in_specs=[row, row, pl.BlockSpec((1, D_MODEL), lambda i: (0, 0))],
        out_specs=row,
        out_shape=jax.ShapeDtypeStruct((n, D_MODEL), F32),
        compiler_params=_cparams(("parallel",)),
        name="final_norm",
    )(xa, xb, g.reshape(1, D_MODEL))


DIFF_PAGES = 16


def _diff_sample_kernel(pt_ref, q_ref, knew_ref, vnew_ref, dl_ref, g_ref, *rest, past_len, lam_init):
    page_refs = [_row_view(r, PAGE_SIZE * 2 * A_HEADS) for r in rest[:DIFF_PAGES]]
    o_ref, m_ref, l_ref, acc_ref = rest[DIFF_PAGES:]
    p = pl.program_id(1)
    scale = A_HALF ** -0.5
    nrow = 2 * A_HEADS
    per_key = 2 * A_HEADS

    @pl.when(p == 0)
    def _():
        m_ref[...] = jnp.full(m_ref.shape, NEG_INF, F32)
        l_ref[...] = jnp.zeros(l_ref.shape, F32)
        acc_ref[...] = jnp.zeros(acc_ref.shape, F32)

    row = lax.broadcasted_iota(jnp.int32, (nrow, 1), 0)
    lane = lax.broadcasted_iota(jnp.int32, (1, A_DHEAD), 1)
    slope = jnp.exp2(-2.0 * ((row >> 1) + 1).astype(F32))
    q = q_ref[0]
    q2 = [jnp.where(((row >> 1) == h) & ((lane >= A_HALF) == ((row & 1) == 1)), q[:, h * A_DHEAD:(h + 1) * A_DHEAD], 0.0)
          for h in range(A_HEADS)]
    q2b = [x.astype(BF16) for x in q2]
    ss = []
    for g in range(DIFF_PAGES):
        sg = None
        for h in range(A_HEADS):
            kh = page_refs[g][pl.ds(h, PAGE_SIZE, stride=per_key), :].astype(BF16)
            d = lax.dot_general(q2b[h], kh, NT_DIMS, preferred_element_type=F32)
            sg = d if sg is None else sg + d
        ss.append(sg)
    s = jnp.concatenate(ss, axis=1) * scale
    nk = DIFF_PAGES * PAGE_SIZE
    kpos = p * nk + lax.broadcasted_iota(jnp.int32, (1, nk), 1)
    s = s - slope * (past_len - kpos).astype(F32)
    m_old = m_ref[:, 0:1]
    m_new = jnp.maximum(m_old, jnp.max(s, axis=-1, keepdims=True))
    alpha = jnp.exp(m_old - m_new)
    e = jnp.exp(s - m_new)
    eb = e.astype(BF16)
    l_new = alpha * l_ref[:, 0:1] + jnp.sum(e, axis=-1, keepdims=True)
    pv = jnp.zeros((nrow, A_DHEAD), F32)
    for h in range(A_HEADS):
        vh = jnp.concatenate([page_refs[g][pl.ds(A_HEADS + h, PAGE_SIZE, stride=per_key), :].astype(BF16)
                              for g in range(DIFF_PAGES)], axis=0)
        pv = pv + jnp.where((row >> 1) == h, jnp.dot(eb, vh, preferred_element_type=F32), 0.0)
    acc = alpha * acc_ref[...] + pv
    m_ref[...] = jnp.broadcast_to(m_new, m_ref.shape)
    l_ref[...] = jnp.broadcast_to(l_new, l_ref.shape)
    acc_ref[...] = acc

    @pl.when(p == pl.num_programs(1) - 1)
    def _():
        knew = knew_ref[0]
        vnew = vnew_ref[0]
        s_n = jnp.zeros((nrow, 1), F32)
        for h in range(A_HEADS):
            s_n = s_n + jnp.sum(q2[h] * knew[:, h * A_DHEAD:(h + 1) * A_DHEAD], axis=-1, keepdims=True)
        s_n = s_n * scale
        v8 = jnp.concatenate([vnew[:, (r // 2) * A_DHEAD:(r // 2 + 1) * A_DHEAD] for r in range(nrow)], axis=0)
        m_f = jnp.maximum(m_new, s_n)
        a_f = jnp.exp(m_new - m_f)
        e_n = jnp.exp(s_n - m_f)
        o8 = (a_f * acc + e_n * v8) / (a_f * l_new + e_n)
        lam = _diff_lambda(dl_ref[...], lam_init)
        outs = []
        for h in range(A_HEADS):
            oh = o8[2 * h:2 * h + 1, :] - lam * o8[2 * h + 1:2 * h + 2, :]
            ms = jnp.mean(oh * oh, axis=-1, keepdims=True)
            outs.append(oh * lax.rsqrt(ms + NORM_EPS) * g_ref[...] * (1.0 - lam_init))
        o_ref[0] = jnp.concatenate(outs, axis=1)


def _diff_sample_call(page_table, proj3, cache, layer, dl, subln_g, lam_init):
    bs, n_pages = page_table.shape
    past_len = n_pages * PAGE_SIZE
    steps = n_pages // DIFF_PAGES

    def page_spec(g):
        return pl.BlockSpec((None, None, PAGE_SIZE, 2, A_HEADS, A_DHEAD),
                            lambda b, p, pt: (layer, pt[b, p * DIFF_PAGES + g], 0, 0, 0, 0))

    grid_spec = pltpu.PrefetchScalarGridSpec(
        num_scalar_prefetch=1,
        grid=(bs, steps),
        in_specs=[
            pl.BlockSpec((1, 1, A_WIDTH), lambda b, p, pt: (b, 0, COL_AQ // A_WIDTH)),
            pl.BlockSpec((1, 1, A_WIDTH), lambda b, p, pt: (b, 0, COL_AK // A_WIDTH)),
            pl.BlockSpec((1, 1, A_WIDTH), lambda b, p, pt: (b, 0, COL_AV // A_WIDTH)),
            pl.BlockSpec((4, A_HALF), lambda b, p, pt: (0, 0)),
            pl.BlockSpec((1, A_DHEAD), lambda b, p, pt: (0, 0)),
        ] + [page_spec(g) for g in range(DIFF_PAGES)],
        out_specs=pl.BlockSpec((1, 1, A_WIDTH), lambda b, p, pt: (b, 0, 0)),
        scratch_shapes=[pltpu.VMEM((2 * A_HEADS, LANES), F32), pltpu.VMEM((2 * A_HEADS, LANES), F32),
                        pltpu.VMEM((2 * A_HEADS, A_DHEAD), F32)],
    )
    return pl.pallas_call(
        functools.partial(_diff_sample_kernel, past_len=past_len, lam_init=lam_init),
        grid_spec=grid_spec,
        out_shape=jax.ShapeDtypeStruct((bs, 1, A_WIDTH), F32),
        compiler_params=_cparams(("parallel", "arbitrary")),
        name="diff_attn_sample",
    )(page_table, proj3, proj3, proj3, dl, subln_g.reshape(1, A_DHEAD), *([cache] * DIFF_PAGES))


def _diag_rows(vec):
    n = vec.shape[1]
    r = lax.broadcasted_iota(jnp.int32, (n, n), 0)
    c = lax.broadcasted_iota(jnp.int32, (n, n), 1)
    return jnp.where(r == c, vec, 0.0)


def _ssd_sample_kernel(*refs):
    nx, nz = B_CONV_DIM // SSD_COLS, B_WIDTH // SSD_COLS
    xbc_refs, z_refs = refs[:nx], refs[nx:nx + nz]
    (sm_ref, cbuf_ref, h0_ref, cw_ref, cb_ref, dtb_ref, alog_ref, dsk_ref, g_ref,
     o_ref, hout_ref, cout_ref) = refs[nx + nz:]
    hi = lax.Precision.HIGHEST
    new = jnp.concatenate([r[0] for r in xbc_refs], axis=1)
    buf = cbuf_ref[0, 0]
    cw = cw_ref[...]
    conv = cb_ref[...] + cw[CONV_W - 1:CONV_W] * new
    for i in range(CONV_W - 1):
        conv = conv + cw[i:i + 1] * buf[i:i + 1]
    cout_ref[0] = jnp.concatenate([buf[1:CONV_W - 1], new], axis=0)
    xc = _silu(conv)
    xs = xc[:, :B_WIDTH]
    dt = _softplus(sm_ref[0] + dtb_ref[...])
    ea = jnp.exp(dt * (-jnp.exp(alog_ref[...])))
    hr = lax.broadcasted_iota(jnp.int32, (LANES, B_WIDTH), 0)
    hc = lax.broadcasted_iota(jnp.int32, (LANES, B_WIDTH), 1)
    rep = jnp.where((hc // B_HEADDIM) == hr, 1.0, 0.0)
    both = jnp.concatenate([dt, ea, jnp.zeros((6, LANES), F32)], axis=0)
    both_rep = jnp.dot(both, rep, preferred_element_type=F32, precision=hi)
    u = both_rep[0:1] * xs
    ea_rep = both_rep[1:2]
    gn = B_GROUPS * B_STATE
    rows = (B_HEADS // B_GROUPS) * B_HEADDIM
    ys = []
    for g in range(B_GROUPS):
        r0 = g * rows
        bg = xc[:, B_WIDTH + g * B_STATE:B_WIDTH + (g + 1) * B_STATE]
        cg = xc[:, B_WIDTH + gn + g * B_STATE:B_WIDTH + gn + (g + 1) * B_STATE]
        h0 = h0_ref[0, 0, r0:r0 + rows, :]
        hn = jnp.dot(_diag_rows(ea_rep[:, r0:r0 + rows]), h0, preferred_element_type=F32, precision=hi)
        hn = hn + jnp.dot(_diag_rows(u[:, r0:r0 + rows]), jnp.broadcast_to(bg, (rows, B_STATE)),
                          preferred_element_type=F32, precision=hi)
        hout_ref[0, r0:r0 + rows, :] = hn
        c8 = jnp.broadcast_to(cg, (8, B_STATE)).astype(BF16)
        ys.append(lax.dot_general(c8, hn.astype(BF16), NT_DIMS, preferred_element_type=F32)[0:1])
    y = jnp.concatenate(ys, axis=1) + dsk_ref[...] * xs
    y = y * _silu(jnp.concatenate([r[0] for r in z_refs], axis=1))
    ms = jnp.mean(y * y, axis=-1, keepdims=True)
    o_ref[0] = y * lax.rsqrt(ms + NORM_EPS) * g_ref[...]


def _ssd_sample_call(proj3, state_conv, state_ssm4, layer, conv_w, conv_b, dt_bias, a_log, d_skip, norm_g):
    bs = proj3.shape[0]
    const = lambda b: (0, 0)
    nrow = B_HEADS * B_HEADDIM
    return pl.pallas_call(
        _ssd_sample_kernel,
        grid=(bs,),
        in_specs=[
            *[pl.BlockSpec((1, 1, SSD_COLS), functools.partial(lambda k, b: (b, 0, COL_XBC // SSD_COLS + k), k))
              for k in range(B_CONV_DIM // SSD_COLS)],
            *[pl.BlockSpec((1, 1, SSD_COLS), functools.partial(lambda k, b: (b, 0, COL_Z // SSD_COLS + k), k))
              for k in range(B_WIDTH // SSD_COLS)],
            pl.BlockSpec((1, 1, LANES), lambda b: (b, 0, COL_SMALL // LANES)),
            pl.BlockSpec((1, 1, CONV_W - 1, B_CONV_DIM), lambda b: (layer, b, 0, 0)),
            pl.BlockSpec((1, 1, nrow, B_STATE), lambda b: (layer, b, 0, 0)),
            pl.BlockSpec((CONV_W, B_CONV_DIM), const),
            pl.BlockSpec((1, B_CONV_DIM), const),
            pl.BlockSpec((1, LANES), const),
            pl.BlockSpec((1, LANES), const),
            pl.BlockSpec((1, B_WIDTH), const),
            pl.BlockSpec((1, B_WIDTH), const),
        ],
        out_specs=[
            pl.BlockSpec((1, 1, B_WIDTH), lambda b: (b, 0, 0)),
            pl.BlockSpec((1, nrow, B_STATE), lambda b: (b, 0, 0)),
            pl.BlockSpec((1, CONV_W - 1, B_CONV_DIM), lambda b: (b, 0, 0)),
        ],
        out_shape=[
            jax.ShapeDtypeStruct((bs, 1, B_WIDTH), F32),
            jax.ShapeDtypeStruct((bs, nrow, B_STATE), F32),
            jax.ShapeDtypeStruct((bs, CONV_W - 1, B_CONV_DIM), F32),
        ],
        compiler_params=_cparams(("parallel",)),
        name="ssd_sample",
    )(*([proj3] * (B_CONV_DIM // SSD_COLS + B_WIDTH // SSD_COLS + 1)), state_conv, state_ssm4, conv_w,
      conv_b.reshape(1, -1), _pad_lanes(dt_bias),
      _pad_lanes(a_log), jnp.repeat(d_skip, B_HEADDIM).reshape(1, B_WIDTH), norm_g.reshape(1, B_WIDTH))


CMP_PAGES = 32


def _compress_paged_kernel(pt_ref, pe_ref, w1_ref, *rest):
    page_refs = [_row_view(r, PAGE_SIZE * NSA_ROWS) for r in rest[:CMP_PAGES]]
    o_ref = rest[CMP_PAGES]
    per_page = PAGE_SIZE // CMP_STRIDE
    outs = []
    for kv in range(2):
        acc_lo = jnp.zeros((CMP_PAGES * per_page, C_DHEAD), F32)
        acc_hi = jnp.zeros((CMP_PAGES * per_page, C_DHEAD), F32)
        for r in range(CMP_STRIDE):
            x = jnp.concatenate([page_refs[g][pl.ds(NSA_ROWS * r + kv, per_page, stride=NSA_ROWS * CMP_STRIDE), :]
                                 for g in range(CMP_PAGES)], axis=0)
            x_lo = (x + pe_ref[kv, r:r + 1, :]).astype(BF16)
            x_hi = (x + pe_ref[kv, CMP_STRIDE + r:CMP_STRIDE + r + 1, :]).astype(BF16)
            acc_lo = acc_lo + jnp.dot(x_lo, w1_ref[kv, r].astype(BF16), preferred_element_type=F32)
            acc_hi = acc_hi + jnp.dot(x_hi, w1_ref[kv, CMP_STRIDE + r].astype(BF16), preferred_element_type=F32)
        outs += [acc_lo, acc_hi]
    o_ref[0] = jnp.concatenate(outs, axis=1)


def _compress_paged_call(page_table, cache, layer, pe, w1):
    bs, n_pages = page_table.shape
    steps = n_pages // CMP_PAGES
    per_page = PAGE_SIZE // CMP_STRIDE

    def page_spec(g):
        return pl.BlockSpec((None, None, PAGE_SIZE, NSA_ROWS, C_DHEAD),
                            lambda b, p, pt: (layer, pt[b, p * CMP_PAGES + g], 0, 0, 0))

    grid_spec = pltpu.PrefetchScalarGridSpec(
        num_scalar_prefetch=1,
        grid=(bs, steps),
        in_specs=[
            pl.BlockSpec((2, CMP_LEN, C_DHEAD), lambda b, p, pt: (0, 0, 0)),
            pl.BlockSpec((2, CMP_LEN, C_DHEAD, C_DHEAD), lambda b, p, pt: (0, 0, 0, 0)),
        ] + [page_spec(g) for g in range(CMP_PAGES)],
        out_specs=pl.BlockSpec((1, CMP_PAGES * per_page, 4 * C_DHEAD), lambda b, p, pt: (b, p, 0)),
    )
    return pl.pallas_call(
        _compress_paged_kernel,
        grid_spec=grid_spec,
        out_shape=jax.ShapeDtypeStruct((bs, n_pages * per_page, 4 * C_DHEAD), F32),
        compiler_params=_cparams(("parallel", "arbitrary")),
        name="nsa_compress_sample",
    )(page_table, pe, w1.reshape(2, CMP_LEN, C_DHEAD, C_DHEAD), *([cache] * CMP_PAGES))


def _heads_to_rows(q):
    rows = [q[:, h * C_DHEAD:(h + 1) * C_DHEAD] for h in range(C_HEADS)]
    return jnp.concatenate(rows + [jnp.zeros((8 - C_HEADS, C_DHEAD), F32)], axis=0)


SEL_LANES = 384


def _nsa_select_kernel(part_ref, w2_ref, q_ref, ocmp_ref, sel_ref, *, q_pos):
    nchunk = part_ref.shape[1]
    n_cmp = (q_pos + 1 - CMP_LEN) // CMP_STRIDE + 1
    n_sel = -(-(q_pos + 1) // SEL_BLOCK)
    scale = C_DHEAD ** -0.5
    part = part_ref[0]
    kv_cmp = []
    for kv in range(2):
        lo = part[:, (2 * kv) * C_DHEAD:(2 * kv + 1) * C_DHEAD]
        hi = part[:, (2 * kv + 1) * C_DHEAD:(2 * kv + 2) * C_DHEAD]
        hid = lo + pltpu.roll(hi, nchunk - 1, 0)
        kv_cmp.append(jnp.dot(_gelu(hid).astype(BF16), w2_ref[kv].astype(BF16), preferred_element_type=F32).astype(BF16))
    q8 = _heads_to_rows(q_ref[0]).astype(BF16)
    row = lax.broadcasted_iota(jnp.int32, (8, 1), 0)
    slope = jnp.exp2(-2.0 * (row + 1).astype(F32))
    n_i = lax.broadcasted_iota(jnp.int32, (1, nchunk), 1)
    dist_c = q_pos - (n_i * CMP_STRIDE + CMP_LEN - 1)
    ok = (dist_c >= 0) & (n_i < n_cmp)
    s = lax.dot_general(q8, kv_cmp[0], NT_DIMS, preferred_element_type=F32) * scale - slope * dist_c.astype(F32)
    e, den = _masked_softmax(s, ok)
    p = jnp.where(row < C_HEADS, e / jnp.maximum(den, 1e-30), 0.0)
    ocmp_ref[0] = jnp.dot(p.astype(BF16), kv_cmp[1], preferred_element_type=F32)
    psum = jnp.broadcast_to(jnp.sum(p, axis=0, keepdims=True), (8, nchunk))
    c_i = lax.broadcasted_iota(jnp.int32, (nchunk, SEL_LANES), 0)
    j_i = lax.broadcasted_iota(jnp.int32, (nchunk, SEL_LANES), 1)
    lo_ = jnp.maximum(c_i * CMP_STRIDE, j_i * SEL_BLOCK)
    hi_ = jnp.minimum(c_i * CMP_STRIDE + CMP_LEN, (j_i + 1) * SEL_BLOCK)
    ovl = jnp.where((c_i < n_cmp) & (j_i < n_sel), jnp.maximum(hi_ - lo_, 0).astype(F32) * (1.0 / CMP_LEN), 0.0)
    imp = jnp.dot(psum, ovl, preferred_element_type=F32, precision=lax.Precision.HIGHEST)[0:1]
    lane = lax.broadcasted_iota(jnp.int32, (1, SEL_LANES), 1)
    qblk = q_pos // SEL_BLOCK
    forced = (lane == 0) | (lane == qblk) | (lane == qblk - 1)
    score = jnp.where(lane <= qblk, imp + jnp.where(forced, FORCE_SCORE, 0.0), NEG_INF)
    score = jnp.where(lane < n_sel, score, -jnp.inf)
    lane_f = lane.astype(F32)
    out_lane = lax.broadcasted_iota(jnp.int32, (1, LANES), 1)
    sel = jnp.full((1, LANES), -1.0, F32)
    for k in range(min(SEL_TOPK, n_sel)):
        m = jnp.max(score, axis=-1, keepdims=True)
        idx = jnp.min(jnp.where(score == m, lane_f, 1e9), axis=-1, keepdims=True)
        sel = jnp.where(out_lane == k, jnp.where(m > NEG_INF / 2, idx, -1.0), sel)
        score = jnp.where(lane_f == idx, -jnp.inf, score)
    sel_ref[0] = sel.astype(jnp.int32)


def _nsa_select_call(part, w2, q3, q_pos):
    bs, nchunk, _ = part.shape
    return pl.pallas_call(
        functools.partial(_nsa_select_kernel, q_pos=q_pos),
        grid=(bs,),
        in_specs=[
            pl.BlockSpec((1, nchunk, 4 * C_DHEAD), lambda b: (b, 0, 0)),
            pl.BlockSpec((2, C_DHEAD, C_DHEAD), lambda b: (0, 0, 0)),
            pl.BlockSpec((1, 1, C_WIDTH), lambda b: (b, 0, COL_CQ // C_WIDTH)),
        ],
        out_specs=[pl.BlockSpec((1, 8, C_DHEAD), lambda b: (b, 0, 0)), pl.BlockSpec((1, 1, LANES), lambda b: (b, 0, 0))],
        out_shape=[jax.ShapeDtypeStruct((bs, 8, C_DHEAD), F32), jax.ShapeDtypeStruct((bs, 1, LANES), jnp.int32)],
        compiler_params=_cparams(("parallel",)),
        name="nsa_select_sample",
    )(part, w2, q3)


def _nsa_attend_kernel(sel_ref, pt_ref, q_ref, new_ref, wnew_ref, sm_ref, ocmp_ref, win_ref, g_ref, *rest, q_pos):
    k_eff = SEL_TOPK
    blk_refs = [_row_view(r, SEL_BLOCK * NSA_ROWS) for r in rest[:k_eff]]
    o_ref, wout_ref = rest[k_eff:]
    b = pl.program_id(0)
    scale = C_DHEAD ** -0.5
    n_past_blocks = q_pos // SEL_BLOCK
    q8f = _heads_to_rows(q_ref[0])
    q8 = q8f.astype(BF16)
    row = lax.broadcasted_iota(jnp.int32, (8, 1), 0)
    slope = jnp.exp2(-2.0 * (row + 1).astype(F32))
    lane64 = lax.broadcasted_iota(jnp.int32, (1, SEL_BLOCK), 1)

    ss, vs, oks = [], [], []
    new_sel = jnp.zeros((1, 1), jnp.int32)
    for k in range(k_eff):
        j = sel_ref[b, k]
        k_sel = blk_refs[k][pl.ds(2, SEL_BLOCK, stride=NSA_ROWS), :]
        v_sel = blk_refs[k][pl.ds(3, SEL_BLOCK, stride=NSA_ROWS), :]
        s = lax.dot_general(q8, k_sel.astype(BF16), NT_DIMS, preferred_element_type=F32)
        dist = q_pos - (j * SEL_BLOCK + lane64)
        ss.append(s * scale - slope * dist.astype(F32))
        oks.append(lane64 * 0 + jnp.where((j >= 0) & (j < n_past_blocks), 1, 0))
        vs.append(v_sel.astype(BF16))
        new_sel = new_sel + jnp.where(j == n_past_blocks, 1, 0)
    s = jnp.concatenate(ss, axis=1)
    ok = jnp.concatenate(oks, axis=1) > 0
    new = new_ref[0]
    s_n = jnp.sum(q8f * new[:, 2 * C_DHEAD:3 * C_DHEAD], axis=-1, keepdims=True) * scale
    s_n = jnp.where(new_sel > 0, s_n, NEG_INF)
    s = jnp.where(ok, s, NEG_INF)
    m = jnp.maximum(jnp.max(s, axis=-1, keepdims=True), s_n)
    e = jnp.where(ok, jnp.exp(s - m), 0.0)
    e_n = jnp.where(new_sel > 0, jnp.exp(s_n - m), 0.0)
    den = jnp.sum(e, axis=-1, keepdims=True) + e_n
    o_sel = (jnp.dot(e.astype(BF16), jnp.concatenate(vs, axis=0), preferred_element_type=F32)
             + e_n * new[:, 3 * C_DHEAD:]) / den

    lw = win_ref.shape[0]
    win_k = win_ref[:, 0, :]
    win_v = win_ref[:, 1, :]
    wnew = wnew_ref[0]
    wpos = lax.broadcasted_iota(jnp.int32, (1, lw), 1)
    dist_w = lw - wpos
    ok_w = dist_w < WINDOW
    s = lax.dot_general(q8, win_k.astype(BF16), NT_DIMS, preferred_element_type=F32) * scale
    s = jnp.where(ok_w, s - slope * dist_w.astype(F32), NEG_INF)
    s_n = jnp.sum(q8f * wnew[:, :C_DHEAD], axis=-1, keepdims=True) * scale
    m = jnp.maximum(jnp.max(s, axis=-1, keepdims=True), s_n)
    e = jnp.where(ok_w, jnp.exp(s - m), 0.0)
    e_n = jnp.exp(s_n - m)
    den = jnp.sum(e, axis=-1, keepdims=True) + e_n
    o_win = (jnp.dot(e.astype(BF16), win_v.astype(BF16), preferred_element_type=F32)
             + e_n * wnew[:, C_DHEAD:]) / den
    keep = min(WINDOW, lw + 1)
    wout_ref[0:keep - 1, :, :] = win_ref[lw + 1 - keep:lw, :, :]
    wout_ref[keep - 1:keep, 0, :] = wnew[:, :C_DHEAD]
    wout_ref[keep - 1:keep, 1, :] = wnew[:, C_DHEAD:]

    gate = 1.0 / (1.0 + jnp.exp(-sm_ref[0]))
    outs = []
    for h in range(C_HEADS):
        g0 = gate[:, GATE_LANE0 + h:GATE_LANE0 + h + 1]
        g1 = gate[:, GATE_LANE0 + C_HEADS + h:GATE_LANE0 + C_HEADS + h + 1]
        g2 = gate[:, GATE_LANE0 + 2 * C_HEADS + h:GATE_LANE0 + 2 * C_HEADS + h + 1]
        outs.append(g0 * ocmp_ref[0, h:h + 1, :] + g1 * o_sel[h:h + 1, :] + g2 * o_win[h:h + 1, :])
    o = jnp.concatenate(outs, axis=1)
    ms = jnp.mean(o * o, axis=-1, keepdims=True)
    o_ref[0] = o * lax.rsqrt(ms + NORM_EPS) * g_ref[...]


def _nsa_attend_call(sel, page_table, proj3, ocmp, cache, cache_win, layer, out_g, q_pos):
    bs = proj3.shape[0]
    lw = cache_win.shape[2]
    keep = min(WINDOW, lw + 1)
    n_pages = page_table.shape[1]
    halves = PAGE_SIZE // SEL_BLOCK

    def blk_spec(k):
        def imap(b, sel_r, pt_r):
            j = jnp.clip(sel_r[b, k], 0, n_pages * halves - 1)
            return (layer, pt_r[b, j // halves], j % halves, 0, 0)
        return pl.BlockSpec((None, None, SEL_BLOCK, NSA_ROWS, C_DHEAD), imap)

    row3 = lambda w, col: pl.BlockSpec((1, 1, w), lambda b, s_, p_: (b, 0, col))
    grid_spec = pltpu.PrefetchScalarGridSpec(
        num_scalar_prefetch=2,
        grid=(bs,),
        in_specs=[
            row3(C_WIDTH, COL_CQ // C_WIDTH),
            row3(4 * C_DHEAD, COL_CKV // (4 * C_DHEAD)),
            row3(2 * C_DHEAD, COL_WIN // (2 * C_DHEAD)),
            row3(LANES, COL_SMALL // LANES),
            pl.BlockSpec((1, 8, C_DHEAD), lambda b, s_, p_: (b, 0, 0)),
            pl.BlockSpec((None, None, lw, 2, C_DHEAD), lambda b, s_, p_: (layer, b, 0, 0, 0)),
            pl.BlockSpec((1, C_WIDTH), lambda b, s_, p_: (0, 0)),
        ] + [blk_spec(k) for k in range(SEL_TOPK)],
        out_specs=[
            pl.BlockSpec((1, 1, C_WIDTH), lambda b, s_, p_: (b, 0, 0)),
            pl.BlockSpec((None, keep, 2, C_DHEAD), lambda b, s_, p_: (b, 0, 0, 0)),
        ],
    )
    return pl.pallas_call(
        functools.partial(_nsa_attend_kernel, q_pos=q_pos),
        grid_spec=grid_spec,
        out_shape=[jax.ShapeDtypeStruct((bs, 1, C_WIDTH), F32), jax.ShapeDtypeStruct((bs, keep, 2, C_DHEAD), F32)],
        compiler_params=_cparams(("arbitrary",)),
        name="nsa_attend_sample",
    )(sel, page_table, proj3, proj3, proj3, proj3, ocmp, cache_win, out_g.reshape(1, C_WIDTH),
      *([cache] * SEL_TOPK))


def _token_mixer_tail(xa, xb, oa, ob, oc, w_out_b, ffn_g, wq_t, subkeys, u_tab, v_tab, tm, tn, et, emit_bf16=False,
                      layer=0):
    h, hn_t = _outproj_call(xa, xb, oa, ob, oc, w_out_b, ffn_g, tm)
    q_t = _peer_q_call(wq_t, hn_t, tm)
    route = _peer_route_call(q_t, subkeys)
    return h, _peer_expert_call(hn_t, route, u_tab, v_tab, tn, et, emit_bf16, layer)


def kernel(x_prompt, x_sample, cache_diff_kv, cache_nsa_kv, cache_nsa_win, state_ssm, state_conv, page_table,
           norm_mix_g, w_in, w_out, diff_lam, diff_subln_g, ssm_conv_w, ssm_conv_b, ssm_dt_bias, ssm_a_log,
           ssm_d, ssm_norm_g, nsa_pe, nsa_cmp_w1, nsa_cmp_w2, nsa_out_g, norm_ffn_g, peer_wq, peer_subkeys,
           peer_u, peer_v, norm_final_g):
    depth = w_in.shape[0]
    bp, t, _ = x_prompt.shape
    bs = x_sample.shape[0]
    past_len = page_table.shape[1] * PAGE_SIZE
    n_p = bp * t
    n_s = LANES
    tm_p = 512

    xa_p, xb_p = x_prompt.reshape(n_p, D_MODEL), None
    xa_s = jnp.pad(x_sample.reshape(bs, D_MODEL), ((0, n_s - bs), (0, 0)))
    xb_s = None
    st_p = [[] for _ in range(5)]
    st_s = [[] for _ in range(5)]
    state_ssm4 = state_ssm.reshape(depth, bs, B_HEADS * B_HEADDIM, B_STATE)
    w_in_t = jnp.transpose(w_in, (2, 0, 1))
    w_main = _cast_w_main_call(w_in_t)
    w_tail = _pack_w_tail_call(w_in_t)
    for l in range(depth):
        lam_init = 0.8 - 0.6 * math.exp(-0.3 * l)
        w_out_b = w_out[l].astype(BF16)
        wq_t = peer_wq[l].T.astype(BF16)

        proj_s, dkv_s, nkv_s = _proj_call(xa_s, xb_s, norm_mix_g[l], w_main, w_tail, l, n_s)
        proj_s = proj_s[:bs]
        proj_s3 = proj_s.reshape(bs, 1, PROJ_W)
        o_a = _diff_sample_call(page_table, proj_s3, cache_diff_kv, l, diff_lam[l], diff_subln_g[l], lam_init)
        o_b, h_new, conv_new = _ssd_sample_call(proj_s3, state_conv, state_ssm4, l, ssm_conv_w[l], ssm_conv_b[l],
                                                ssm_dt_bias[l], ssm_a_log[l], ssm_d[l], ssm_norm_g[l])
        part = _compress_paged_call(page_table, cache_nsa_kv, l, nsa_pe[l], nsa_cmp_w1[l])
        o_cmp, sel = _nsa_select_call(part, nsa_cmp_w2[l], proj_s3, past_len)
        o_c, win_out = _nsa_attend_call(sel[:, 0, :SEL_TOPK], page_table, proj_s3, o_cmp, cache_nsa_kv, cache_nsa_win, l,
                                        nsa_out_g[l], past_len)
        st_s[0].append(dkv_s[:bs].reshape(bs, 1, 2, A_HEADS, A_DHEAD))
        st_s[1].append(nkv_s[:bs].reshape(bs, 1, NSA_ROWS, C_DHEAD))
        st_s[2].append(win_out)
        st_s[3].append(h_new.reshape(bs, B_HEADS, B_HEADDIM, B_STATE))
        st_s[4].append(conv_new)
        pad = lambda a: jnp.pad(a.reshape(bs, -1), ((0, n_s - bs), (0, 0)))
        xa_s, (xb_s, u_b, vt_b) = _token_mixer_tail(xa_s, xb_s, pad(o_a), pad(o_b), pad(o_c), w_out_b, norm_ffn_g[l],
                                                   wq_t, peer_subkeys[l], peer_u, peer_v, n_s, n_s, 512, True, l)

        proj, dkv_p, nkv_p = _proj_call(xa_p, xb_p, norm_mix_g[l], w_main, w_tail, l, tm_p)
        o_a = _diff_prompt_call(proj, bp, t, diff_lam[l], diff_subln_g[l], lam_init)
        o_b, h_ssm = _ssd_prompt_call(proj, bp, t, ssm_conv_w[l], ssm_conv_b[l], ssm_dt_bias[l], ssm_a_log[l],
                                      ssm_d[l], ssm_norm_g[l])
        kc, vc = _compress_prompt_call(proj, bp, t, nsa_pe[l], nsa_cmp_w1[l], nsa_cmp_w2[l])
        o_c = _nsa_prompt_call(proj, kc, vc, bp, t, nsa_out_g[l])
        proj3 = proj.reshape(bp, t, PROJ_W)
        st_p[0].append(dkv_p.reshape(bp, t, 2, A_HEADS, A_DHEAD))
        st_p[1].append(nkv_p.reshape(bp, t, NSA_ROWS, C_DHEAD))
        keep = min(WINDOW, t)
        st_p[2].append(proj3[:, t - keep:, COL_WIN:COL_WIN + 2 * C_DHEAD].reshape(bp, keep, 2, C_DHEAD))
        st_p[3].append(h_ssm)
        st_p[4].append(proj3[:, t - (CONV_W - 1):, COL_XBC:COL_XBC + B_CONV_DIM])
        xa_p, xb_p = _token_mixer_tail(xa_p, xb_p, o_a, o_b, o_c, w_out_b, norm_ffn_g[l], wq_t, peer_subkeys[l],
                                       u_b, vt_b, tm_p, 512, 1024)

    y_p = _final_norm_call(xa_p, xb_p, norm_final_g, tm_p).reshape(bp, t, D_MODEL)
    y_s = _final_norm_call(xa_s, xb_s, norm_final_g, n_s)[:bs].reshape(bs, 1, D_MODEL)
    return (y_p, y_s) + tuple(jnp.stack(s) for s in st_p) + tuple(jnp.stack(s) for s in st_s)
```

```python
import functools
import math

import jax
import jax.numpy as jnp
from jax import lax
from jax.experimental import pallas as pl
from jax.experimental.pallas import tpu as pltpu

F32 = jnp.float32
BF16 = jnp.bfloat16

D_MODEL = 2048
A_HEADS = 4
A_HALF = 64
A_DHEAD = 128
A_WIDTH = 512
B_WIDTH = 1024
B_HEADDIM = 64
B_HEADS = 16
B_GROUPS = 4
B_STATE = 128
CONV_W = 4
B_CONV_DIM = 2048
SSD_CHUNK = 128
C_HEADS = 4
C_DHEAD = 128
C_WIDTH = 512
CMP_LEN = 32
CMP_STRIDE = 16
SEL_BLOCK = 64
SEL_TOPK = 16
WINDOW = 512
PEER_HEADS = 8
PEER_NKEYS = 128
PEER_TOPK = 16
PEER_DQ = 256
PAGE_SIZE = 128
NSA_ROWS = 4
NORM_EPS = 1e-6
NEG_INF = -1e30
FORCE_SCORE = 1e4

LANES = 128
VMEM_LIMIT = 56 * 1024 * 1024

COL_AQ = 0
COL_AK = 512
COL_AV = 1024
COL_Z = 1536
COL_XBC = 2560
COL_MAIN = 4608
COL_CQ = 4608
COL_CKV = 5120
COL_WIN = 5632
COL_SMALL = 5888
PROJ_W = 6144
GATE_LANE0 = B_HEADS

NT_DIMS = (((1,), (1,)), ((), ()))


def _cparams(sem, vmem=VMEM_LIMIT):
    return pltpu.CompilerParams(dimension_semantics=sem, vmem_limit_bytes=vmem)


def _gelu(x):
    return 0.5 * x * (1.0 + jnp.tanh(math.sqrt(2.0 / math.pi) * (x + 0.044715 * (x * x * x))))


def _silu(x):
    return x * (1.0 / (1.0 + jnp.exp(-x)))


def _softplus(x):
    return jnp.maximum(x, 0.0) + jnp.log(1.0 + jnp.exp(-jnp.abs(x)))


def _row_view(ref, rows):
    return ref.reshape(rows, ref.shape[-1])


def _alibi_slope(h):
    if isinstance(h, int):
        return 2.0 ** (-2.0 * (h + 1))
    return jnp.exp2(jnp.full((1, 1), -2.0, F32) * (h + 1).astype(F32))


def _proj_kernel(*refs, two, tc):
    if two:
        xa_ref, xb_ref, g_ref, wm_ref, wt_ref, o_ref, dkv_ref, nkv_ref, xn_ref = refs
    else:
        xa_ref, g_ref, wm_ref, wt_ref, o_ref, dkv_ref, nkv_ref, xn_ref = refs
    j = pl.program_id(1)
    n_main = COL_MAIN // tc

    @pl.when(j == 0)
    def _():
        x = xa_ref[...]
        if two:
            x = x + xb_ref[...]
        ms = jnp.mean(x * x, axis=-1, keepdims=True)
        xn_ref[...] = (x * lax.rsqrt(ms + NORM_EPS) * g_ref[...]).astype(BF16)

    def store_cache_rows(res, tile):
        for b in range(2 * A_HEADS):
            col = COL_AK + b * LANES
            if col // tc == tile:
                dkv_ref[:, b // A_HEADS, b % A_HEADS, :] = res[:, col % tc:col % tc + LANES]
        for b in range(NSA_ROWS):
            col = COL_CKV + b * LANES
            if col // tc == tile:
                nkv_ref[:, b, :] = res[:, col % tc:col % tc + LANES]

    state_tiles = sorted({(COL_AK + b * LANES) // tc for b in range(2 * A_HEADS)}
                         | {(COL_CKV + b * LANES) // tc for b in range(NSA_ROWS)})

    def finish(res, tiles):
        o_ref[...] = res
        for tile in tiles:
            if tile in state_tiles:
                pl.when(j == tile)(functools.partial(store_cache_rows, res, tile))

    @pl.when(j < n_main)
    def _():
        finish(jnp.dot(xn_ref[...], wm_ref[...], preferred_element_type=F32), range(n_main))

    @pl.when(j >= n_main)
    def _():
        finish(jnp.dot(xn_ref[...], wt_ref[...], preferred_element_type=F32), range(n_main, PROJ_W // tc))


def _cast_t_kernel(w_ref, o_ref):
    for l in range(w_ref.shape[1]):
        o_ref[l] = w_ref[:, l, :].T.astype(BF16)


def _cast_w_main_call(w_in_t, tr=384):
    depth = w_in_t.shape[1]
    return pl.pallas_call(
        _cast_t_kernel,
        grid=(COL_MAIN // tr,),
        in_specs=[pl.BlockSpec((tr, depth, D_MODEL), lambda j: (j, 0, 0))],
        out_specs=pl.BlockSpec((depth, D_MODEL, tr), lambda j: (0, 0, j)),
        out_shape=jax.ShapeDtypeStruct((depth, D_MODEL, COL_MAIN), BF16),
        compiler_params=_cparams(("parallel",)),
        name="w_in_cast",
    )(w_in_t)


def _pack_tail_kernel(w_ref, o_ref, *, n_in):
    width, depth, tk = w_ref.shape
    dt_w, gate_w = B_HEADS, 3 * C_HEADS
    cq0 = dt_w
    ckv0 = cq0 + C_WIDTH
    gate0 = n_in - COL_MAIN - gate_w
    pad = jnp.zeros((PROJ_W - n_in, tk), F32)
    for l in range(depth):
        x = w_ref[:, l, :]
        packed = jnp.concatenate([x[cq0:ckv0], x[ckv0:gate0], x[0:dt_w], x[gate0:gate0 + gate_w], pad], axis=0)
        o_ref[l] = packed.T.astype(BF16)


def _pack_w_tail_call(w_in_t, tk=512):
    width = PROJ_W - COL_MAIN
    depth = w_in_t.shape[1]
    assert COL_MAIN % width == 0
    return pl.pallas_call(
        functools.partial(_pack_tail_kernel, n_in=w_in_t.shape[0]),
        grid=(D_MODEL // tk,),
        in_specs=[pl.BlockSpec((width, depth, tk), lambda k: (COL_MAIN // width, 0, k))],
        out_specs=pl.BlockSpec((depth, tk, width), lambda k: (0, k, 0)),
        out_shape=jax.ShapeDtypeStruct((depth, D_MODEL, width), BF16),
        compiler_params=_cparams(("parallel",)),
        name="w_in_tail_pack",
    )(w_in_t)


def _proj_call(xa, xb, g, w_main, w_tail, layer, tm):
    n = xa.shape[0]
    two = xb is not None
    tc = 768 if two else 1536
    n_main = COL_MAIN // tc
    assert COL_MAIN % tc == 0 and w_tail.shape[2] == PROJ_W - COL_MAIN
    xspec = pl.BlockSpec((tm, D_MODEL), lambda i, j: (i, 0))
    ins = [xa] + ([xb] if two else []) + [g.reshape(1, D_MODEL), w_main, w_tail]
    specs = [xspec] + ([xspec] if two else []) + [
        pl.BlockSpec((1, D_MODEL), lambda i, j: (0, 0)),
        pl.BlockSpec((None, D_MODEL, tc), lambda i, j: (layer, 0, jnp.minimum(j, n_main - 1))),
        pl.BlockSpec((None, D_MODEL, tc), lambda i, j: (layer, 0, jnp.maximum(j - n_main, 0))),
    ]
    return pl.pallas_call(
        functools.partial(_proj_kernel, two=two, tc=tc),
        grid=(n // tm, PROJ_W // tc),
        in_specs=specs,
        out_specs=[pl.BlockSpec((tm, tc), lambda i, j: (i, j)),
                   pl.BlockSpec((tm, 2, A_HEADS, A_DHEAD), lambda i, j: (i, 0, 0, 0)),
                   pl.BlockSpec((tm, NSA_ROWS, C_DHEAD), lambda i, j: (i, 0, 0))],
        out_shape=[jax.ShapeDtypeStruct((n, PROJ_W), F32), jax.ShapeDtypeStruct((n, 2, A_HEADS, A_DHEAD), F32),
                   jax.ShapeDtypeStruct((n, NSA_ROWS, C_DHEAD), F32)],
        scratch_shapes=[pltpu.VMEM((tm, D_MODEL), BF16)],
        compiler_params=_cparams(("parallel", "arbitrary")),
        name="in_proj",
    )(*ins)


def _diff_lambda(dl, lam_init):
    a = jnp.sum(dl[0:1] * dl[1:2], axis=-1, keepdims=True)
    b = jnp.sum(dl[2:3] * dl[3:4], axis=-1, keepdims=True)
    return jnp.exp(a) - jnp.exp(b) + lam_init


CAUSAL_LEVELS = 4


def _causal_prefixes(i, nq, tq, body):
    levels = min(CAUSAL_LEVELS, nq)
    per = nq // levels
    for lv in range(levels):
        pl.when(i // per == lv)(functools.partial(body, (lv + 1) * per * tq))


def _diff_prompt_kernel(q_ref, k_ref, v_ref, dl_ref, g_ref, o_ref, *, tq, lam_init):
    h = pl.program_id(1)
    i = pl.program_id(2)
    t = k_ref.shape[0]
    scale = A_HALF ** -0.5

    def body(nk):
        lam = _diff_lambda(dl_ref[...], lam_init)
        q = q_ref[...] * scale
        lane = lax.broadcasted_iota(jnp.int32, (1, A_DHEAD), 1)
        kb = k_ref[0:nk, :].astype(BF16)
        vb = v_ref[0:nk, :].astype(BF16)
        qpos = i * tq + lax.broadcasted_iota(jnp.int32, (tq, 1), 0)
        kpos = lax.broadcasted_iota(jnp.int32, (1, nk), 1)
        ok = qpos >= kpos
        key_bias = _alibi_slope(h) * kpos.astype(F32)

        def half_attention(c):
            qc = jnp.where((lane >= c * A_HALF) & (lane < (c + 1) * A_HALF), q, 0.0).astype(BF16)
            s = lax.dot_general(qc, kb, NT_DIMS, preferred_element_type=F32) + key_bias
            s = jnp.where(ok, s, NEG_INF)
            e = jnp.exp(s - jnp.max(s, axis=-1, keepdims=True))
            pv = jnp.dot(e.astype(BF16), vb, preferred_element_type=F32)
            return pv / jnp.sum(e, axis=-1, keepdims=True)

        o = half_attention(0) - lam * half_attention(1)
        ms = jnp.mean(o * o, axis=-1, keepdims=True)
        o_ref[...] = o * lax.rsqrt(ms + NORM_EPS) * g_ref[...] * (1.0 - lam_init)

    _causal_prefixes(i, t // tq, tq, body)


def _diff_prompt_call(proj, bsz, t, dl, subln_g, lam_init, tq=256):
    nq = t // tq
    cq, ck, cv = COL_AQ // A_DHEAD, COL_AK // A_DHEAD, COL_AV // A_DHEAD
    return pl.pallas_call(
        functools.partial(_diff_prompt_kernel, tq=tq, lam_init=lam_init),
        grid=(bsz, A_HEADS, nq),
        in_specs=[
            pl.BlockSpec((tq, A_DHEAD), lambda b, h, i: (b * nq + i, cq + h)),
            pl.BlockSpec((t, A_DHEAD), lambda b, h, i: (b, ck + h)),
            pl.BlockSpec((t, A_DHEAD), lambda b, h, i: (b, cv + h)),
            pl.BlockSpec((4, A_HALF), lambda b, h, i: (0, 0)),
            pl.BlockSpec((1, A_DHEAD), lambda b, h, i: (0, 0)),
        ],
        out_specs=pl.BlockSpec((tq, A_DHEAD), lambda b, h, i: (b * nq + i, h)),
        out_shape=jax.ShapeDtypeStruct((bsz * t, A_WIDTH), F32),
        compiler_params=_cparams(("parallel", "parallel", "arbitrary")),
        name="diff_attn_prompt",
    )(proj, proj, proj, dl, subln_g.reshape(1, A_DHEAD))


SSD_COLS = 512


def _ssd_prompt_kernel(*refs):
    nx, nz = B_CONV_DIM // SSD_COLS, B_WIDTH // SSD_COLS
    xbc_refs, z_refs = refs[:nx], refs[nx:nx + nz]
    sm_ref, cw_ref, cb_ref, dtb_ref, alog_ref, dsk_ref, g_ref, o_ref, hout_ref, buf_ref, h_ref = refs[nx + nz:]
    c = pl.program_id(1)
    cs = SSD_CHUNK

    @pl.when(c == 0)
    def _():
        buf_ref[0:8, :] = jnp.zeros((8, B_CONV_DIM), F32)
        h_ref[...] = jnp.zeros_like(h_ref)

    xbc = jnp.concatenate([r[...] for r in xbc_refs], axis=1)
    buf_ref[8:8 + cs, :] = xbc
    cw = cw_ref[...]
    conv = cb_ref[...] + cw[3:4] * xbc
    for j in range(1, CONV_W):
        conv = conv + cw[3 - j:4 - j] * buf_ref[8 - j:8 - j + cs, :]
    buf_ref[0:8, :] = xbc[cs - 8:cs, :]
    xc = _silu(conv)
    xs = xc[:, :B_WIDTH]

    dt = _softplus(sm_ref[...] + dtb_ref[...])
    a_neg = -jnp.exp(alog_ref[...])
    dta = dt * a_neg
    row = lax.broadcasted_iota(jnp.int32, (cs, cs), 0)
    col = lax.broadcasted_iota(jnp.int32, (cs, cs), 1)
    causal = row >= col
    acum = jnp.dot(causal.astype(F32), dta, preferred_element_type=F32, precision=lax.Precision.HIGHEST)
    acum_t = acum.T
    dt_t = dt.T
    lane = lax.broadcasted_iota(jnp.int32, (1, LANES), 1)
    lo = lane < B_HEADDIM

    ys = []
    for g in range(B_GROUPS):
        bg = xc[:, B_WIDTH + g * B_STATE:B_WIDTH + (g + 1) * B_STATE]
        cg = xc[:, B_WIDTH + B_GROUPS * B_STATE + g * B_STATE:B_WIDTH + B_GROUPS * B_STATE + (g + 1) * B_STATE]
        bgb = bg.astype(BF16)
        cgb = cg.astype(BF16)
        cb = lax.dot_general(cgb, bgb, NT_DIMS, preferred_element_type=F32)
        for pr in range(2):
            h0 = g * 4 + pr * 2
            xpair = xs[:, h0 * B_HEADDIM:(h0 + 2) * B_HEADDIM]
            xpb = xpair.astype(BF16)
            ydiag = []
            ecol = []
            wcol = []
            elast = []
            for hh in (h0, h0 + 1):
                a_col = acum[:, hh:hh + 1]
                a_row = acum_t[hh:hh + 1, :]
                decay = jnp.exp(jnp.where(causal, a_col - a_row, NEG_INF))
                lm = cb * decay * dt_t[hh:hh + 1, :]
                ydiag.append(jnp.dot(lm.astype(BF16), xpb, preferred_element_type=F32))
                a_last = acum[cs - 1:cs, hh:hh + 1]
                ecol.append(jnp.exp(a_col))
                wcol.append(jnp.exp(a_last - a_col) * dt[:, hh:hh + 1])
                elast.append(jnp.exp(a_last))
            hp = h_ref[h0 * B_HEADDIM:(h0 + 2) * B_HEADDIM, :]
            yoff = lax.dot_general(cgb, hp.astype(BF16), NT_DIMS, preferred_element_type=F32)
            y = jnp.where(lo, ydiag[0], ydiag[1]) + yoff * jnp.where(lo, ecol[0], ecol[1])
            ys.append(y)
            wx = xpair * jnp.where(lo, wcol[0], wcol[1])
            upd = jnp.dot(wx.T.astype(BF16), bgb, preferred_element_type=F32)
            prow = lax.broadcasted_iota(jnp.int32, (LANES, 1), 0) < B_HEADDIM
            h_ref[h0 * B_HEADDIM:(h0 + 2) * B_HEADDIM, :] = jnp.where(prow, elast[0], elast[1]) * hp + upd

    y = jnp.concatenate(ys, axis=1)
    y = y + dsk_ref[...] * xs
    y = y * _silu(jnp.concatenate([r[...] for r in z_refs], axis=1))
    ms = jnp.mean(y * y, axis=-1, keepdims=True)
    o_ref[...] = y * lax.rsqrt(ms + NORM_EPS) * g_ref[...]

    @pl.when(c == pl.num_programs(1) - 1)
    def _():
        hout_ref[0] = h_ref[...]


def _pad_lanes(v, fill=0.0):
    v = v.reshape(1, -1).astype(F32)
    return jnp.pad(v, ((0, 0), (0, LANES - v.shape[1])), constant_values=fill)


def _ssd_prompt_call(proj, bsz, t, conv_w, conv_b, dt_bias, a_log, d_skip, norm_g):
    nc = t // SSD_CHUNK
    cs = SSD_CHUNK
    const = lambda b, c: (0, 0)
    o, hout = pl.pallas_call(
        _ssd_prompt_kernel,
        grid=(bsz, nc),
        in_specs=[
            *[pl.BlockSpec((cs, SSD_COLS), functools.partial(lambda k, b, c: (b * nc + c, COL_XBC // SSD_COLS + k), k))
              for k in range(B_CONV_DIM // SSD_COLS)],
            *[pl.BlockSpec((cs, SSD_COLS), functools.partial(lambda k, b, c: (b * nc + c, COL_Z // SSD_COLS + k), k))
              for k in range(B_WIDTH // SSD_COLS)],
            pl.BlockSpec((cs, LANES), lambda b, c: (b * nc + c, COL_SMALL // LANES)),
            pl.BlockSpec((CONV_W, B_CONV_DIM), const),
            pl.BlockSpec((1, B_CONV_DIM), const),
            pl.BlockSpec((1, LANES), const),
            pl.BlockSpec((1, LANES), const),
            pl.BlockSpec((1, B_WIDTH), const),
            pl.BlockSpec((1, B_WIDTH), const),
        ],
        out_specs=[
            pl.BlockSpec((cs, B_WIDTH), lambda b, c: (b * nc + c, 0)),
            pl.BlockSpec((1, B_HEADS * B_HEADDIM, B_STATE), lambda b, c: (b, 0, 0)),
        ],
        out_shape=[
            jax.ShapeDtypeStruct((bsz * t, B_WIDTH), F32),
            jax.ShapeDtypeStruct((bsz, B_HEADS * B_HEADDIM, B_STATE), F32),
        ],
        scratch_shapes=[pltpu.VMEM((8 + cs, B_CONV_DIM), F32), pltpu.VMEM((B_HEADS * B_HEADDIM, B_STATE), F32)],
        compiler_params=_cparams(("parallel", "arbitrary")),
        name="ssd_prompt",
    )(*([proj] * (B_CONV_DIM // SSD_COLS + B_WIDTH // SSD_COLS + 1)), conv_w, conv_b.reshape(1, -1),
      _pad_lanes(dt_bias), _pad_lanes(a_log), jnp.repeat(d_skip, B_HEADDIM).reshape(1, B_WIDTH),
      norm_g.reshape(1, B_WIDTH))
    return o, hout.reshape(bsz, B_HEADS, B_HEADDIM, B_STATE)


def _compress_partials(load_rows, pe_ref, w1_ref, kv, nchunk):
    acc = jnp.zeros((nchunk + 8, 2 * C_DHEAD), F32)
    tail = jnp.zeros((6, C_DHEAD), F32)
    for r in range(CMP_STRIDE):
        x = jnp.concatenate([load_rows(r), pe_ref[kv, r:r + 1, :], pe_ref[kv, CMP_STRIDE + r:CMP_STRIDE + r + 1, :], tail],
                            axis=0).astype(BF16)
        w = jnp.concatenate([w1_ref[kv, r], w1_ref[kv, CMP_STRIDE + r]], axis=1).astype(BF16)
        acc = acc + jnp.dot(x, w, preferred_element_type=F32)
    lo = acc[0:nchunk, :C_DHEAD] + acc[nchunk:nchunk + 1, :C_DHEAD]
    hi = acc[0:nchunk, C_DHEAD:] + acc[nchunk + 1:nchunk + 2, C_DHEAD:]
    return lo, hi


def _compress_kernel(k_ref, v_ref, pe_ref, w1_ref, w2_ref, kc_ref, vc_ref, *, nchunk):
    outs = []
    for kv, rows_ref in enumerate((k_ref, v_ref)):
        acc_lo, acc_hi = _compress_partials(lambda r: rows_ref[pl.ds(r, nchunk, stride=CMP_STRIDE), :],
                                            pe_ref, w1_ref, kv, nchunk)
        hid = acc_lo + pltpu.roll(acc_hi, nchunk - 1, 0)
        outs.append(jnp.dot(_gelu(hid).astype(BF16), w2_ref[kv].astype(BF16), preferred_element_type=F32))
    kc_ref[0] = outs[0]
    vc_ref[0] = outs[1]


def _compress_prompt_call(proj, bsz, t, pe, w1, w2):
    nchunk = t // CMP_STRIDE
    shp = jax.ShapeDtypeStruct((bsz, nchunk, C_DHEAD), F32)
    return pl.pallas_call(
        functools.partial(_compress_kernel, nchunk=nchunk),
        grid=(bsz,),
        in_specs=[
            pl.BlockSpec((t, C_DHEAD), lambda b: (b, COL_CKV // C_DHEAD)),
            pl.BlockSpec((t, C_DHEAD), lambda b: (b, COL_CKV // C_DHEAD + 1)),
            pl.BlockSpec((2, CMP_LEN, C_DHEAD), lambda b: (0, 0, 0)),
            pl.BlockSpec((2, CMP_LEN, C_DHEAD, C_DHEAD), lambda b: (0, 0, 0, 0)),
            pl.BlockSpec((2, C_DHEAD, C_DHEAD), lambda b: (0, 0, 0)),
        ],
        out_specs=[pl.BlockSpec((1, nchunk, C_DHEAD), lambda b: (b, 0, 0))] * 2,
        out_shape=[shp, shp],
        compiler_params=_cparams(("parallel",)),
        name="nsa_compress_prompt",
    )(proj, proj, pe, w1.reshape(2, CMP_LEN, C_DHEAD, C_DHEAD), w2)


def _masked_softmax(s, ok):
    s = jnp.where(ok, s, NEG_INF)
    m = jnp.max(s, axis=-1, keepdims=True)
    e = jnp.where(ok, jnp.exp(s - m), 0.0)
    return e, jnp.sum(e, axis=-1, keepdims=True)


def _topk_mask_lanes(score, k, n):
    lane = lax.broadcasted_iota(jnp.int32, (1, LANES), 1)
    rank = jnp.zeros(score.shape, F32)
    for i in range(n):
        ci = score[:, i:i + 1]
        beats = (ci > score) | ((ci == score) & (lane > i))
        rank = rank + jnp.where(beats, 1.0, 0.0)
    return (rank < k) & (lane < n)


def _nsa_prompt_kernel(q_ref, ks_ref, vs_ref, kw_ref, vw_ref, kc_ref, vc_ref, sm_ref, g_ref, o_ref, osel_ref, *, tq, n_cmp):
    i = pl.program_id(1)
    t = ks_ref.shape[0]
    n_sel = t // SEL_BLOCK
    scale = C_DHEAD ** -0.5
    sel_shift = SEL_BLOCK.bit_length() - 1
    wlen = min(t, WINDOW + tq)

    qpos = i * tq + lax.broadcasted_iota(jnp.int32, (tq, 1), 0)
    lane = lax.broadcasted_iota(jnp.int32, (1, LANES), 1)

    cmp_end = lane * CMP_STRIDE + (CMP_LEN - 1)
    dist_c = qpos - cmp_end
    ok_c = (dist_c >= 0) & (lane < n_cmp)
    dist_cf = dist_c.astype(F32)
    kcb = kc_ref[0].astype(BF16)
    vcb = vc_ref[0].astype(BF16)
    qs = [(q_ref[:, h * C_DHEAD:(h + 1) * C_DHEAD] * scale).astype(BF16) for h in range(C_HEADS)]
    o_cmp = []
    psum = jnp.zeros((tq, LANES), F32)
    for h in range(C_HEADS):
        s = lax.dot_general(qs[h], kcb, NT_DIMS, preferred_element_type=F32)
        s = s - _alibi_slope(h) * dist_cf
        e, den = _masked_softmax(s, ok_c)
        p = e / jnp.maximum(den, 1e-30)
        psum = psum + p
        o_cmp.append(jnp.dot(p.astype(BF16), vcb, preferred_element_type=F32))

    n_i = lax.broadcasted_iota(jnp.int32, (LANES, LANES), 0)
    j_i = lax.broadcasted_iota(jnp.int32, (LANES, LANES), 1)
    lo_ = jnp.maximum(n_i * CMP_STRIDE, j_i * SEL_BLOCK)
    hi_ = jnp.minimum(n_i * CMP_STRIDE + CMP_LEN, (j_i + 1) * SEL_BLOCK)
    ovl = jnp.maximum(hi_ - lo_, 0).astype(F32) * (1.0 / CMP_LEN)
    ovl = jnp.where((n_i < n_cmp) & (j_i < n_sel), ovl, 0.0)
    imp = jnp.dot(psum, ovl, preferred_element_type=F32, precision=lax.Precision.HIGHEST)
    qblk = qpos >> sel_shift
    sel_valid = lane <= qblk
    forced = (lane == 0) | (lane == qblk) | (lane == qblk - 1)
    score = jnp.where(sel_valid, imp + jnp.where(forced, FORCE_SCORE, 0.0), NEG_INF)
    score = jnp.where(lane < n_sel, score, -jnp.inf)
    chosen = _topk_mask_lanes(score, min(SEL_TOPK, n_sel), n_sel) & sel_valid
    chosen_b = jnp.where(chosen, 1.0, 0.0).astype(BF16)

    def attend(qh, kb, vb, key_bias, ok):
        s = lax.dot_general(qh, kb, NT_DIMS, preferred_element_type=F32) + key_bias
        s = jnp.where(ok, s, NEG_INF)
        e = jnp.exp(s - jnp.max(s, axis=-1, keepdims=True))
        return jnp.dot(e.astype(BF16), vb, preferred_element_type=F32) / jnp.sum(e, axis=-1, keepdims=True)

    def selected_branch(nk):
        e_j = lax.broadcasted_iota(jnp.int32, (LANES, nk), 0)
        e_k = lax.broadcasted_iota(jnp.int32, (LANES, nk), 1)
        expand = jnp.where((e_k >> sel_shift) == e_j, 1.0, 0.0).astype(BF16)
        key_sel = jnp.dot(chosen_b, expand, preferred_element_type=F32) > 0.5
        kpos = lax.broadcasted_iota(jnp.int32, (1, nk), 1)
        kpos_f = kpos.astype(F32)
        ok_s = key_sel & (qpos >= kpos)
        ksb = ks_ref[0:nk, :].astype(BF16)
        vsb = vs_ref[0:nk, :].astype(BF16)
        for h in range(C_HEADS):
            osel_ref[:, h * C_DHEAD:(h + 1) * C_DHEAD] = attend(qs[h], ksb, vsb, _alibi_slope(h) * kpos_f, ok_s)

    nq = t // tq
    if nq % 2 == 0:
        pl.when(i < nq // 2)(functools.partial(selected_branch, t // 2))
        pl.when(i >= nq // 2)(functools.partial(selected_branch, t))
    else:
        selected_branch(t)

    w0 = pl.multiple_of(jnp.clip(i * tq - WINDOW, 0, t - wlen), 8)
    wpos = w0 + lax.broadcasted_iota(jnp.int32, (1, wlen), 1)
    wpos_f = wpos.astype(F32)
    dist_w = qpos - wpos
    ok_w = (dist_w >= 0) & (dist_w < WINDOW)
    kwb = kw_ref[pl.ds(w0, wlen), :].astype(BF16)
    vwb = vw_ref[pl.ds(w0, wlen), :].astype(BF16)
    gate = 1.0 / (1.0 + jnp.exp(-sm_ref[...]))
    outs = []
    for h in range(C_HEADS):
        o_sel = osel_ref[:, h * C_DHEAD:(h + 1) * C_DHEAD]
        o_win = attend(qs[h], kwb, vwb, _alibi_slope(h) * wpos_f, ok_w)
        g0 = gate[:, GATE_LANE0 + h:GATE_LANE0 + h + 1]
        g1 = gate[:, GATE_LANE0 + C_HEADS + h:GATE_LANE0 + C_HEADS + h + 1]
        g2 = gate[:, GATE_LANE0 + 2 * C_HEADS + h:GATE_LANE0 + 2 * C_HEADS + h + 1]
        outs.append(g0 * o_cmp[h] + g1 * o_sel + g2 * o_win)
    o = jnp.concatenate(outs, axis=1)
    ms = jnp.mean(o * o, axis=-1, keepdims=True)
    o_ref[...] = o * lax.rsqrt(ms + NORM_EPS) * g_ref[...]


def _nsa_prompt_call(proj, kc, vc, bsz, t, out_g, tq=256):
    nq = t // tq
    n_cmp = (t - CMP_LEN) // CMP_STRIDE + 1
    c0 = COL_CKV // C_DHEAD
    w0 = COL_WIN // C_DHEAD
    kvspec = lambda col: pl.BlockSpec((t, C_DHEAD), lambda b, i: (b, col))
    return pl.pallas_call(
        functools.partial(_nsa_prompt_kernel, tq=tq, n_cmp=n_cmp),
        grid=(bsz, nq),
        in_specs=[
            pl.BlockSpec((tq, C_WIDTH), lambda b, i: (b * nq + i, COL_CQ // C_WIDTH)),
            kvspec(c0 + 2), kvspec(c0 + 3), kvspec(w0), kvspec(w0 + 1),
            pl.BlockSpec((1, kc.shape[1], C_DHEAD), lambda b, i: (b, 0, 0)),
            pl.BlockSpec((1, kc.shape[1], C_DHEAD), lambda b, i: (b, 0, 0)),
            pl.BlockSpec((tq, LANES), lambda b, i: (b * nq + i, COL_SMALL // LANES)),
            pl.BlockSpec((1, C_WIDTH), lambda b, i: (0, 0)),
        ],
        out_specs=pl.BlockSpec((tq, C_WIDTH), lambda b, i: (b * nq + i, 0)),
        out_shape=jax.ShapeDtypeStruct((bsz * t, C_WIDTH), F32),
        scratch_shapes=[pltpu.VMEM((tq, C_WIDTH), F32)],
        compiler_params=_cparams(("parallel", "arbitrary")),
        name="nsa_attn_prompt",
    )(proj, proj, proj, proj, proj, kc, vc, proj, out_g.reshape(1, C_WIDTH))


def _outproj_kernel(*refs, two):
    if two:
        xa_ref, xb_ref, oa_ref, ob_ref, oc_ref, w_ref, g_ref, h_ref, hnt_ref = refs
    else:
        xa_ref, oa_ref, ob_ref, oc_ref, w_ref, g_ref, h_ref, hnt_ref = refs
    x = xa_ref[...]
    if two:
        x = x + xb_ref[...]
    mixed = jnp.dot(oa_ref[...].astype(BF16), w_ref[0:A_WIDTH, :], preferred_element_type=F32)
    mixed = mixed + jnp.dot(ob_ref[...].astype(BF16), w_ref[A_WIDTH:A_WIDTH + B_WIDTH, :], preferred_element_type=F32)
    mixed = mixed + jnp.dot(oc_ref[...].astype(BF16), w_ref[A_WIDTH + B_WIDTH:, :], preferred_element_type=F32)
    h = x + mixed
    h_ref[...] = h
    ms = jnp.mean(h * h, axis=-1, keepdims=True)
    hnt_ref[...] = (h * lax.rsqrt(ms + NORM_EPS) * g_ref[...]).T.astype(BF16)


def _outproj_call(xa, xb, oa, ob, oc, w_out_b, g, tm):
    n = xa.shape[0]
    two = xb is not None
    row = lambda w: pl.BlockSpec((tm, w), lambda i: (i, 0))
    ins = [xa] + ([xb] if two else []) + [oa, ob, oc, w_out_b, g.reshape(1, D_MODEL)]
    specs = [row(D_MODEL)] + ([row(D_MODEL)] if two else []) + [
        row(A_WIDTH), row(B_WIDTH), row(C_WIDTH),
        pl.BlockSpec((D_MODEL, D_MODEL), lambda i: (0, 0)),
        pl.BlockSpec((1, D_MODEL), lambda i: (0, 0)),
    ]
    return pl.pallas_call(
        functools.partial(_outproj_kernel, two=two),
        grid=(n // tm,),
        in_specs=specs,
        out_specs=[row(D_MODEL), pl.BlockSpec((D_MODEL, tm), lambda i: (0, i))],
        out_shape=[jax.ShapeDtypeStruct((n, D_MODEL), F32), jax.ShapeDtypeStruct((D_MODEL, n), BF16)],
        compiler_params=_cparams(("parallel",)),
        name="out_proj",
    )(*ins)


def _peer_q_kernel(wqt_ref, hnt_ref, qt_ref):
    qt_ref[...] = jnp.dot(wqt_ref[...], hnt_ref[...], preferred_element_type=F32)


def _peer_q_call(wq_t, hn_t, tm):
    n = hn_t.shape[1]
    dq = wq_t.shape[0]
    return pl.pallas_call(
        _peer_q_kernel,
        grid=(n // tm,),
        in_specs=[pl.BlockSpec((dq, D_MODEL), lambda i: (0, 0)), pl.BlockSpec((D_MODEL, tm), lambda i: (0, i))],
        out_specs=pl.BlockSpec((dq, tm), lambda i: (0, i)),
        out_shape=jax.ShapeDtypeStruct((dq, n), F32),
        compiler_params=_cparams(("parallel",)),
        name="peer_query",
    )(wq_t, hn_t)


NOT_RANKED = 99.0


def _top_rows(s, pos, k, want_rank):
    rank = jnp.full(s.shape, NOT_RANKED, F32) if want_rank else None
    vals, picks = [], []
    for j in range(k):
        m = jnp.max(s, axis=0, keepdims=True)
        idx = jnp.min(jnp.where(s == m, pos, 1e9), axis=0, keepdims=True)
        hit = pos == idx
        if want_rank:
            rank = jnp.where(hit, float(j), rank)
        s = jnp.where(hit, -jnp.inf, s)
        vals.append(m)
        picks.append(idx)
    return jnp.concatenate(vals, axis=0), jnp.concatenate(picks, axis=0), rank


PAIR_ROWS = PEER_TOPK + 7 * 8 + 8


def _pair_candidates(v1, v2):
    tn = v1.shape[1]
    parts = [v1[0:1, :] + v2] + [v1[a:a + 1, :] + v2[0:8, :] for a in range(1, 8)] + [v1[8:16, :] + v2[0:1, :]]
    r = lax.broadcasted_iota(jnp.int32, (PAIR_ROWS, tn), 0)
    mid = r - PEER_TOPK
    pos = jnp.where(r < PEER_TOPK, r,
                    jnp.where(r < PEER_TOPK + 56, ((mid >> 3) + 1) * PEER_TOPK + (mid & 7), (r - 64) * PEER_TOPK))
    return jnp.concatenate(parts, axis=0), pos.astype(F32)


def _peer_route_kernel(qt_ref, sk_ref, lim_ref, coef_ref, rank2_ref, e2_ref, *, heads):
    half = PEER_DQ // 2
    row = lax.broadcasted_iota(jnp.int32, (PEER_NKEYS, qt_ref.shape[1]), 0).astype(F32)
    for hh in range(heads):
        q = qt_ref[hh * PEER_DQ:(hh + 1) * PEER_DQ, :]
        s1 = jnp.dot(sk_ref[hh, 0], q[0:half, :], preferred_element_type=F32, precision=lax.Precision.HIGHEST)
        s2 = jnp.dot(sk_ref[hh, 1], q[half:, :], preferred_element_type=F32, precision=lax.Precision.HIGHEST)
        v1, _, rank1 = _top_rows(s1, row, PEER_TOPK, True)
        v2, _, rank2 = _top_rows(s2, row, PEER_TOPK, True)
        cand, cpos = _pair_candidates(v1, v2)
        top, pos, _ = _top_rows(cand, cpos, PEER_TOPK, False)
        z = jnp.sum(jnp.exp(top - top[0:1, :]), axis=0, keepdims=True)
        a_of = jnp.floor(pos * (1.0 / PEER_TOPK))
        lim = jnp.zeros(s1.shape, F32)
        for a in range(PEER_TOPK):
            cnt = jnp.sum(jnp.where(a_of == float(a), 1.0, 0.0), axis=0, keepdims=True)
            lim = jnp.where(rank1 == float(a), cnt, lim)
        lim_ref[hh] = lim
        coef_ref[hh] = jnp.exp(s1 - v1[0:1, :]) / z
        rank2_ref[hh] = rank2.astype(BF16)
        e2_ref[hh] = jnp.exp(s2 - v2[0:1, :]).astype(BF16)


def _peer_route_call(q_t, subkeys, tn=LANES, heads=4):
    n = q_t.shape[1]
    shp = lambda dt: jax.ShapeDtypeStruct((PEER_HEADS, PEER_NKEYS, n), dt)
    ospec = pl.BlockSpec((heads, PEER_NKEYS, tn), lambda j, h: (h, 0, j))
    return pl.pallas_call(
        functools.partial(_peer_route_kernel, heads=heads),
        grid=(n // tn, PEER_HEADS // heads),
        in_specs=[
            pl.BlockSpec((heads * PEER_DQ, tn), lambda j, h: (h, j)),
            pl.BlockSpec((heads, 2, PEER_NKEYS, PEER_DQ // 2), lambda j, h: (h, 0, 0, 0)),
        ],
        out_specs=[ospec] * 4,
        out_shape=[shp(F32), shp(F32), shp(BF16), shp(BF16)],
        compiler_params=_cparams(("parallel", "arbitrary")),
        name="peer_route",
    )(q_t, subkeys)


def _peer_expert_kernel(hnt_ref, lim_ref, coef_ref, rank2_ref, e2_ref, u_ref, v_ref, o_ref, *rest, et, emit):
    acc_ref = rest[-1]
    t = pl.program_id(1)

    @pl.when(t == 0)
    def _():
        acc_ref[...] = jnp.zeros_like(acc_ref)

    if emit:
        ub = u_ref[...].astype(BF16)
        vtb = v_ref[...].T.astype(BF16)
        rest[0][...] = ub
        rest[1][...] = vtb
    else:
        ub = u_ref[...]
        vtb = v_ref[...]
    tn = hnt_ref.shape[1]
    hid = jnp.dot(ub, hnt_ref[...], preferred_element_type=F32)
    acts = []
    for ii in range(et // PEER_NKEYS):
        i1 = t * (et // PEER_NKEYS) + ii
        gate = jnp.zeros((PEER_NKEYS, tn), BF16)
        for h in range(PEER_HEADS):
            lim = lim_ref[h, pl.ds(i1, 1), :].astype(BF16)
            coef = coef_ref[h, pl.ds(i1, 1), :].astype(BF16)
            gate = gate + jnp.where(rank2_ref[h] < lim, e2_ref[h], jnp.zeros((), BF16)) * coef
        acts.append(gate * _gelu(hid[ii * PEER_NKEYS:(ii + 1) * PEER_NKEYS, :]).astype(BF16))
    acc_ref[...] += jnp.dot(vtb, jnp.concatenate(acts, axis=0), preferred_element_type=F32)

    @pl.when(t == pl.num_programs(1) - 1)
    def _():
        o_ref[...] = acc_ref[...].T


def _peer_expert_call(hn_t, route, u_tab, v_tab, tn, et, emit=False, layer=0):
    n = hn_t.shape[1]
    n_exp = u_tab.shape[-2] if emit else u_tab.shape[0]
    rspec = pl.BlockSpec((PEER_HEADS, PEER_NKEYS, tn), lambda j, t: (0, 0, j))
    uspec = pl.BlockSpec((et, D_MODEL), lambda j, t: (t, 0))
    vtspec = pl.BlockSpec((D_MODEL, et), lambda j, t: (0, t))
    ospec = pl.BlockSpec((tn, D_MODEL), lambda j, t: (j, 0))
    oshape = jax.ShapeDtypeStruct((n, D_MODEL), F32)
    if emit:
        assert n == tn
        out_specs = [ospec, uspec, vtspec]
        out_shape = [oshape, jax.ShapeDtypeStruct((n_exp, D_MODEL), BF16), jax.ShapeDtypeStruct((D_MODEL, n_exp), BF16)]
    else:
        out_specs, out_shape = ospec, oshape
    return pl.pallas_call(
        functools.partial(_peer_expert_kernel, et=et, emit=emit),
        grid=(n // tn, n_exp // et),
        in_specs=[pl.BlockSpec((D_MODEL, tn), lambda j, t: (0, j)), rspec, rspec, rspec, rspec]
        + ([pl.BlockSpec((None, et, D_MODEL), lambda j, t: (layer, t, 0))] * 2 if emit else [uspec, vtspec]),
        out_specs=out_specs,
        out_shape=out_shape,
        scratch_shapes=[pltpu.VMEM((D_MODEL, tn), F32)],
        compiler_params=_cparams(("parallel", "arbitrary")),
        name="peer_experts",
    )(hn_t, *route, u_tab, v_tab)


def _final_norm_kernel(xa_ref, xb_ref, g_ref, o_ref):
    x = xa_ref[...] + xb_ref[...]
    ms = jnp.mean(x * x, axis=-1, keepdims=True)
    o_ref[...] = x * lax.rsqrt(ms + NORM_EPS) * g_ref[...]


def _final_norm_call(xa, xb, g, tm):
    n = xa.shape[0]
    row = pl.BlockSpec((tm, D_MODEL), lambda i: (i, 0))
    return pl.pallas_call(
        _final_norm_kernel,
        grid=(n // tm,),
        in_specs=[row, row, pl.BlockSpec((1, D_MODEL), lambda i: (0, 0))],
        out_specs=row,
        out_shape=jax.ShapeDtypeStruct((n, D_MODEL), F32),
        compiler_params=_cparams(("parallel",)),
        name="final_norm",
    )(xa, xb, g.reshape(1, D_MODEL))


DIFF_PAGES = 32


def _diff_sample_kernel(pt_ref, q_ref, knew_ref, vnew_ref, dl_ref, g_ref, *rest, past_len, lam_init):
    page_refs = [_row_view(r, PAGE_SIZE * 2 * A_HEADS) for r in rest[:DIFF_PAGES]]
    o_ref, m_ref, l_ref, acc_ref = rest[DIFF_PAGES:]
    p = pl.program_id(1)
    scale = A_HALF ** -0.5
    nrow = 2 * A_HEADS
    per_key = 2 * A_HEADS

    @pl.when(p == 0)
    def _():
        m_ref[...] = jnp.full(m_ref.shape, NEG_INF, F32)
        l_ref[...] = jnp.zeros(l_ref.shape, F32)
        acc_ref[...] = jnp.zeros(acc_ref.shape, F32)

    row = lax.broadcasted_iota(jnp.int32, (nrow, 1), 0)
    lane = lax.broadcasted_iota(jnp.int32, (1, A_DHEAD), 1)
    slope = jnp.exp2(-2.0 * ((row >> 1) + 1).astype(F32))
    q = q_ref[0]
    q2 = [jnp.where(((row >> 1) == h) & ((lane >= A_HALF) == ((row & 1) == 1)), q[:, h * A_DHEAD:(h + 1) * A_DHEAD], 0.0)
          for h in range(A_HEADS)]
    q2b = [x.astype(BF16) for x in q2]
    ss = []
    for g in range(DIFF_PAGES):
        sg = None
        for h in range(A_HEADS):
            kh = page_refs[g][pl.ds(h, PAGE_SIZE, stride=per_key), :].astype(BF16)
            d = lax.dot_general(q2b[h], kh, NT_DIMS, preferred_element_type=F32)
            sg = d if sg is None else sg + d
        ss.append(sg)
    s = jnp.concatenate(ss, axis=1) * scale
    nk = DIFF_PAGES * PAGE_SIZE
    kpos = p * nk + lax.broadcasted_iota(jnp.int32, (1, nk), 1)
    s = s - slope * (past_len - kpos).astype(F32)
    m_old = m_ref[:, 0:1]
    m_new = jnp.maximum(m_old, jnp.max(s, axis=-1, keepdims=True))
    alpha = jnp.exp(m_old - m_new)
    e = jnp.exp(s - m_new)
    eb = e.astype(BF16)
    l_new = alpha * l_ref[:, 0:1] + jnp.sum(e, axis=-1, keepdims=True)
    pv = jnp.zeros((nrow, A_DHEAD), F32)
    for h in range(A_HEADS):
        vh = jnp.concatenate([page_refs[g][pl.ds(A_HEADS + h, PAGE_SIZE, stride=per_key), :].astype(BF16)
                              for g in range(DIFF_PAGES)], axis=0)
        pv = pv + jnp.where((row >> 1) == h, jnp.dot(eb, vh, preferred_element_type=F32), 0.0)
    acc = alpha * acc_ref[...] + pv
    m_ref[...] = jnp.broadcast_to(m_new, m_ref.shape)
    l_ref[...] = jnp.broadcast_to(l_new, l_ref.shape)
    acc_ref[...] = acc

    @pl.when(p == pl.num_programs(1) - 1)
    def _():
        knew = knew_ref[0]
        vnew = vnew_ref[0]
        s_n = jnp.zeros((nrow, 1), F32)
        for h in range(A_HEADS):
            s_n = s_n + jnp.sum(q2[h] * knew[:, h * A_DHEAD:(h + 1) * A_DHEAD], axis=-1, keepdims=True)
        s_n = s_n * scale
        v8 = jnp.concatenate([vnew[:, (r // 2) * A_DHEAD:(r // 2 + 1) * A_DHEAD] for r in range(nrow)], axis=0)
        m_f = jnp.maximum(m_new, s_n)
        a_f = jnp.exp(m_new - m_f)
        e_n = jnp.exp(s_n - m_f)
        o8 = (a_f * acc + e_n * v8) / (a_f * l_new + e_n)
        lam = _diff_lambda(dl_ref[...], lam_init)
        outs = []
        for h in range(A_HEADS):
            oh = o8[2 * h:2 * h + 1, :] - lam * o8[2 * h + 1:2 * h + 2, :]
            ms = jnp.mean(oh * oh, axis=-1, keepdims=True)
            outs.append(oh * lax.rsqrt(ms + NORM_EPS) * g_ref[...] * (1.0 - lam_init))
        o_ref[0] = jnp.concatenate(outs, axis=1)


def _diff_sample_call(page_table, proj3, cache, layer, dl, subln_g, lam_init):
    bs, n_pages = page_table.shape
    past_len = n_pages * PAGE_SIZE
    steps = n_pages // DIFF_PAGES

    def page_spec(g):
        return pl.BlockSpec((None, None, PAGE_SIZE, 2, A_HEADS, A_DHEAD),
                            lambda b, p, pt: (layer, pt[b, p * DIFF_PAGES + g], 0, 0, 0, 0))

    grid_spec = pltpu.PrefetchScalarGridSpec(
        num_scalar_prefetch=1,
        grid=(bs, steps),
        in_specs=[
            pl.BlockSpec((1, 1, A_WIDTH), lambda b, p, pt: (b, 0, COL_AQ // A_WIDTH)),
            pl.BlockSpec((1, 1, A_WIDTH), lambda b, p, pt: (b, 0, COL_AK // A_WIDTH)),
            pl.BlockSpec((1, 1, A_WIDTH), lambda b, p, pt: (b, 0, COL_AV // A_WIDTH)),
            pl.BlockSpec((4, A_HALF), lambda b, p, pt: (0, 0)),
            pl.BlockSpec((1, A_DHEAD), lambda b, p, pt: (0, 0)),
        ] + [page_spec(g) for g in range(DIFF_PAGES)],
        out_specs=pl.BlockSpec((1, 1, A_WIDTH), lambda b, p, pt: (b, 0, 0)),
        scratch_shapes=[pltpu.VMEM((2 * A_HEADS, LANES), F32), pltpu.VMEM((2 * A_HEADS, LANES), F32),
                        pltpu.VMEM((2 * A_HEADS, A_DHEAD), F32)],
    )
    return pl.pallas_call(
        functools.partial(_diff_sample_kernel, past_len=past_len, lam_init=lam_init),
        grid_spec=grid_spec,
        out_shape=jax.ShapeDtypeStruct((bs, 1, A_WIDTH), F32),
        compiler_params=_cparams(("parallel", "arbitrary")),
        name="diff_attn_sample",
    )(page_table, proj3, proj3, proj3, dl, subln_g.reshape(1, A_DHEAD), *([cache] * DIFF_PAGES))


def _diag_rows(vec):
    n = vec.shape[1]
    r = lax.broadcasted_iota(jnp.int32, (n, n), 0)
    c = lax.broadcasted_iota(jnp.int32, (n, n), 1)
    return jnp.where(r == c, vec, 0.0)


def _ssd_sample_kernel(*refs):
    nx, nz = B_CONV_DIM // SSD_COLS, B_WIDTH // SSD_COLS
    xbc_refs, z_refs = refs[:nx], refs[nx:nx + nz]
    (sm_ref, cbuf_ref, h0_ref, cw_ref, cb_ref, dtb_ref, alog_ref, dsk_ref, g_ref,
     o_ref, hout_ref, cout_ref) = refs[nx + nz:]
    hi = lax.Precision.HIGHEST
    new = jnp.concatenate([r[0] for r in xbc_refs], axis=1)
    buf = cbuf_ref[0, 0]
    cw = cw_ref[...]
    conv = cb_ref[...] + cw[CONV_W - 1:CONV_W] * new
    for i in range(CONV_W - 1):
        conv = conv + cw[i:i + 1] * buf[i:i + 1]
    cout_ref[0] = jnp.concatenate([buf[1:CONV_W - 1], new], axis=0)
    xc = _silu(conv)
    xs = xc[:, :B_WIDTH]
    dt = _softplus(sm_ref[0] + dtb_ref[...])
    ea = jnp.exp(dt * (-jnp.exp(alog_ref[...])))
    hr = lax.broadcasted_iota(jnp.int32, (LANES, B_WIDTH), 0)
    hc = lax.broadcasted_iota(jnp.int32, (LANES, B_WIDTH), 1)
    rep = jnp.where((hc // B_HEADDIM) == hr, 1.0, 0.0)
    both = jnp.concatenate([dt, ea, jnp.zeros((6, LANES), F32)], axis=0)
    both_rep = jnp.dot(both, rep, preferred_element_type=F32, precision=hi)
    u = both_rep[0:1] * xs
    ea_rep = both_rep[1:2]
    gn = B_GROUPS * B_STATE
    rows = (B_HEADS // B_GROUPS) * B_HEADDIM
    ys = []
    for g in range(B_GROUPS):
        r0 = g * rows
        bg = xc[:, B_WIDTH + g * B_STATE:B_WIDTH + (g + 1) * B_STATE]
        cg = xc[:, B_WIDTH + gn + g * B_STATE:B_WIDTH + gn + (g + 1) * B_STATE]
        h0 = h0_ref[0, 0, r0:r0 + rows, :]
        hn = jnp.dot(_diag_rows(ea_rep[:, r0:r0 + rows]), h0, preferred_element_type=F32, precision=hi)
        hn = hn + jnp.dot(_diag_rows(u[:, r0:r0 + rows]), jnp.broadcast_to(bg, (rows, B_STATE)),
                          preferred_element_type=F32, precision=hi)
        hout_ref[0, r0:r0 + rows, :] = hn
        c8 = jnp.broadcast_to(cg, (8, B_STATE)).astype(BF16)
        ys.append(lax.dot_general(c8, hn.astype(BF16), NT_DIMS, preferred_element_type=F32)[0:1])
    y = jnp.concatenate(ys, axis=1) + dsk_ref[...] * xs
    y = y * _silu(jnp.concatenate([r[0] for r in z_refs], axis=1))
    ms = jnp.mean(y * y, axis=-1, keepdims=True)
    o_ref[0] = y * lax.rsqrt(ms + NORM_EPS) * g_ref[...]


def _ssd_sample_call(proj3, state_conv, state_ssm4, layer, conv_w, conv_b, dt_bias, a_log, d_skip, norm_g):
    bs = proj3.shape[0]
    const = lambda b: (0, 0)
    nrow = B_HEADS * B_HEADDIM
    return pl.pallas_call(
        _ssd_sample_kernel,
        grid=(bs,),
        in_specs=[
            *[pl.BlockSpec((1, 1, SSD_COLS), functools.partial(lambda k, b: (b, 0, COL_XBC // SSD_COLS + k), k))
              for k in range(B_CONV_DIM // SSD_COLS)],
            *[pl.BlockSpec((1, 1, SSD_COLS), functools.partial(lambda k, b: (b, 0, COL_Z // SSD_COLS + k), k))
              for k in range(B_WIDTH // SSD_COLS)],
            pl.BlockSpec((1, 1, LANES), lambda b: (b, 0, COL_SMALL // LANES)),
            pl.BlockSpec((1, 1, CONV_W - 1, B_CONV_DIM), lambda b: (layer, b, 0, 0)),
            pl.BlockSpec((1, 1, nrow, B_STATE), lambda b: (layer, b, 0, 0)),
            pl.BlockSpec((CONV_W, B_CONV_DIM), const),
            pl.BlockSpec((1, B_CONV_DIM), const),
            pl.BlockSpec((1, LANES), const),
            pl.BlockSpec((1, LANES), const),
            pl.BlockSpec((1, B_WIDTH), const),
            pl.BlockSpec((1, B_WIDTH), const),
        ],
        out_specs=[
            pl.BlockSpec((1, 1, B_WIDTH), lambda b: (b, 0, 0)),
            pl.BlockSpec((1, nrow, B_STATE), lambda b: (b, 0, 0)),
            pl.BlockSpec((1, CONV_W - 1, B_CONV_DIM), lambda b: (b, 0, 0)),
        ],
        out_shape=[
            jax.ShapeDtypeStruct((bs, 1, B_WIDTH), F32),
            jax.ShapeDtypeStruct((bs, nrow, B_STATE), F32),
            jax.ShapeDtypeStruct((bs, CONV_W - 1, B_CONV_DIM), F32),
        ],
        compiler_params=_cparams(("parallel",)),
        name="ssd_sample",
    )(*([proj3] * (B_CONV_DIM // SSD_COLS + B_WIDTH // SSD_COLS + 1)), state_conv, state_ssm4, conv_w,
      conv_b.reshape(1, -1), _pad_lanes(dt_bias),
      _pad_lanes(a_log), jnp.repeat(d_skip, B_HEADDIM).reshape(1, B_WIDTH), norm_g.reshape(1, B_WIDTH))


CMP_PAGES = 32


def _compress_paged_kernel(pt_ref, pe_ref, w1_ref, *rest):
    page_refs = [_row_view(r, PAGE_SIZE * NSA_ROWS) for r in rest[:CMP_PAGES]]
    o_ref = rest[CMP_PAGES]
    per_page = PAGE_SIZE // CMP_STRIDE
    outs = []
    for kv in range(2):
        def load_rows(r, kv=kv):
            return jnp.concatenate([page_refs[g][pl.ds(NSA_ROWS * r + kv, per_page, stride=NSA_ROWS * CMP_STRIDE), :]
                                    for g in range(CMP_PAGES)], axis=0)
        outs += list(_compress_partials(load_rows, pe_ref, w1_ref, kv, CMP_PAGES * per_page))
    o_ref[0] = jnp.concatenate(outs, axis=1)


def _compress_paged_call(page_table, cache, layer, pe, w1):
    bs, n_pages = page_table.shape
    steps = n_pages // CMP_PAGES
    per_page = PAGE_SIZE // CMP_STRIDE

    def page_spec(g):
        return pl.BlockSpec((None, None, PAGE_SIZE, NSA_ROWS, C_DHEAD),
                            lambda b, p, pt: (layer, pt[b, p * CMP_PAGES + g], 0, 0, 0))

    grid_spec = pltpu.PrefetchScalarGridSpec(
        num_scalar_prefetch=1,
        grid=(bs, steps),
        in_specs=[
            pl.BlockSpec((2, CMP_LEN, C_DHEAD), lambda b, p, pt: (0, 0, 0)),
            pl.BlockSpec((2, CMP_LEN, C_DHEAD, C_DHEAD), lambda b, p, pt: (0, 0, 0, 0)),
        ] + [page_spec(g) for g in range(CMP_PAGES)],
        out_specs=pl.BlockSpec((1, CMP_PAGES * per_page, 4 * C_DHEAD), lambda b, p, pt: (b, p, 0)),
    )
    return pl.pallas_call(
        _compress_paged_kernel,
        grid_spec=grid_spec,
        out_shape=jax.ShapeDtypeStruct((bs, n_pages * per_page, 4 * C_DHEAD), F32),
        compiler_params=_cparams(("parallel", "arbitrary")),
        name="nsa_compress_sample",
    )(page_table, pe, w1.reshape(2, CMP_LEN, C_DHEAD, C_DHEAD), *([cache] * CMP_PAGES))


def _heads_to_rows(q):
    rows = [q[:, h * C_DHEAD:(h + 1) * C_DHEAD] for h in range(C_HEADS)]
    return jnp.concatenate(rows + [jnp.zeros((8 - C_HEADS, C_DHEAD), F32)], axis=0)


SEL_LANES = 384


def _nsa_select_kernel(part_ref, w2_ref, q_ref, ocmp_ref, sel_ref, *, q_pos):
    nchunk = part_ref.shape[1]
    n_cmp = (q_pos + 1 - CMP_LEN) // CMP_STRIDE + 1
    n_sel = -(-(q_pos + 1) // SEL_BLOCK)
    scale = C_DHEAD ** -0.5
    part = part_ref[0]
    kv_cmp = []
    for kv in range(2):
        lo = part[:, (2 * kv) * C_DHEAD:(2 * kv + 1) * C_DHEAD]
        hi = part[:, (2 * kv + 1) * C_DHEAD:(2 * kv + 2) * C_DHEAD]
        hid = lo + pltpu.roll(hi, nchunk - 1, 0)
        kv_cmp.append(jnp.dot(_gelu(hid).astype(BF16), w2_ref[kv].astype(BF16), preferred_element_type=F32).astype(BF16))
    q8 = _heads_to_rows(q_ref[0]).astype(BF16)
    row = lax.broadcasted_iota(jnp.int32, (8, 1), 0)
    slope = jnp.exp2(-2.0 * (row + 1).astype(F32))
    n_i = lax.broadcasted_iota(jnp.int32, (1, nchunk), 1)
    dist_c = q_pos - (n_i * CMP_STRIDE + CMP_LEN - 1)
    ok = (dist_c >= 0) & (n_i < n_cmp)
    s = lax.dot_general(q8, kv_cmp[0], NT_DIMS, preferred_element_type=F32) * scale - slope * dist_c.astype(F32)
    e, den = _masked_softmax(s, ok)
    p = jnp.where(row < C_HEADS, e / jnp.maximum(den, 1e-30), 0.0)
    ocmp_ref[0] = jnp.dot(p.astype(BF16), kv_cmp[1], preferred_element_type=F32)
    psum = jnp.broadcast_to(jnp.sum(p, axis=0, keepdims=True), (8, nchunk))
    c_i = lax.broadcasted_iota(jnp.int32, (nchunk, SEL_LANES), 0)
    j_i = lax.broadcasted_iota(jnp.int32, (nchunk, SEL_LANES), 1)
    lo_ = jnp.maximum(c_i * CMP_STRIDE, j_i * SEL_BLOCK)
    hi_ = jnp.minimum(c_i * CMP_STRIDE + CMP_LEN, (j_i + 1) * SEL_BLOCK)
    ovl = jnp.where((c_i < n_cmp) & (j_i < n_sel), jnp.maximum(hi_ - lo_, 0).astype(F32) * (1.0 / CMP_LEN), 0.0)
    imp = jnp.dot(psum, ovl, preferred_element_type=F32, precision=lax.Precision.HIGHEST)[0:1]
    lane = lax.broadcasted_iota(jnp.int32, (1, SEL_LANES), 1)
    qblk = q_pos // SEL_BLOCK
    forced = (lane == 0) | (lane == qblk) | (lane == qblk - 1)
    score = jnp.where(lane <= qblk, imp + jnp.where(forced, FORCE_SCORE, 0.0), NEG_INF)
    score = jnp.where(lane < n_sel, score, -jnp.inf)
    lane_f = lane.astype(F32)
    out_lane = lax.broadcasted_iota(jnp.int32, (1, LANES), 1)
    sel = jnp.full((1, LANES), -1.0, F32)
    for k in range(min(SEL_TOPK, n_sel)):
        m = jnp.max(score, axis=-1, keepdims=True)
        idx = jnp.min(jnp.where(score == m, lane_f, 1e9), axis=-1, keepdims=True)
        sel = jnp.where(out_lane == k, jnp.where(m > NEG_INF / 2, idx, -1.0), sel)
        score = jnp.where(lane_f == idx, -jnp.inf, score)
    sel_ref[0] = sel.astype(jnp.int32)


def _nsa_select_call(part, w2, q3, q_pos):
    bs, nchunk, _ = part.shape
    return pl.pallas_call(
        functools.partial(_nsa_select_kernel, q_pos=q_pos),
        grid=(bs,),
        in_specs=[
            pl.BlockSpec((1, nchunk, 4 * C_DHEAD), lambda b: (b, 0, 0)),
            pl.BlockSpec((2, C_DHEAD, C_DHEAD), lambda b: (0, 0, 0)),
            pl.BlockSpec((1, 1, C_WIDTH), lambda b: (b, 0, COL_CQ // C_WIDTH)),
        ],
        out_specs=[pl.BlockSpec((1, 8, C_DHEAD), lambda b: (b, 0, 0)), pl.BlockSpec((1, 1, LANES), lambda b: (b, 0, 0))],
        out_shape=[jax.ShapeDtypeStruct((bs, 8, C_DHEAD), F32), jax.ShapeDtypeStruct((bs, 1, LANES), jnp.int32)],
        compiler_params=_cparams(("parallel",)),
        name="nsa_select_sample",
    )(part, w2, q3)


def _nsa_attend_kernel(sel_ref, pt_ref, q_ref, new_ref, wnew_ref, sm_ref, ocmp_ref, win_ref, g_ref, *rest, q_pos):
    k_eff = SEL_TOPK
    blk_refs = [_row_view(r, SEL_BLOCK * NSA_ROWS) for r in rest[:k_eff]]
    o_ref, wout_ref = rest[k_eff:]
    b = pl.program_id(0)
    scale = C_DHEAD ** -0.5
    n_past_blocks = q_pos // SEL_BLOCK
    q8f = _heads_to_rows(q_ref[0])
    q8 = q8f.astype(BF16)
    row = lax.broadcasted_iota(jnp.int32, (8, 1), 0)
    slope = jnp.exp2(-2.0 * (row + 1).astype(F32))
    lane64 = lax.broadcasted_iota(jnp.int32, (1, SEL_BLOCK), 1)

    ss, vs, oks = [], [], []
    new_sel = jnp.zeros((1, 1), jnp.int32)
    for k in range(k_eff):
        j = sel_ref[b, k]
        k_sel = blk_refs[k][pl.ds(2, SEL_BLOCK, stride=NSA_ROWS), :]
        v_sel = blk_refs[k][pl.ds(3, SEL_BLOCK, stride=NSA_ROWS), :]
        s = lax.dot_general(q8, k_sel.astype(BF16), NT_DIMS, preferred_element_type=F32)
        dist = q_pos - (j * SEL_BLOCK + lane64)
        ss.append(s * scale - slope * dist.astype(F32))
        oks.append(lane64 * 0 + jnp.where((j >= 0) & (j < n_past_blocks), 1, 0))
        vs.append(v_sel.astype(BF16))
        new_sel = new_sel + jnp.where(j == n_past_blocks, 1, 0)
    s = jnp.concatenate(ss, axis=1)
    ok = jnp.concatenate(oks, axis=1) > 0
    new = new_ref[0]
    s_n = jnp.sum(q8f * new[:, 2 * C_DHEAD:3 * C_DHEAD], axis=-1, keepdims=True) * scale
    s_n = jnp.where(new_sel > 0, s_n, NEG_INF)
    s = jnp.where(ok, s, NEG_INF)
    m = jnp.maximum(jnp.max(s, axis=-1, keepdims=True), s_n)
    e = jnp.where(ok, jnp.exp(s - m), 0.0)
    e_n = jnp.where(new_sel > 0, jnp.exp(s_n - m), 0.0)
    den = jnp.sum(e, axis=-1, keepdims=True) + e_n
    o_sel = (jnp.dot(e.astype(BF16), jnp.concatenate(vs, axis=0), preferred_element_type=F32)
             + e_n * new[:, 3 * C_DHEAD:]) / den

    lw = win_ref.shape[0]
    win_k = win_ref[:, 0, :]
    win_v = win_ref[:, 1, :]
    wnew = wnew_ref[0]
    wpos = lax.broadcasted_iota(jnp.int32, (1, lw), 1)
    dist_w = lw - wpos
    ok_w = dist_w < WINDOW
    s = lax.dot_general(q8, win_k.astype(BF16), NT_DIMS, preferred_element_type=F32) * scale
    s = jnp.where(ok_w, s - slope * dist_w.astype(F32), NEG_INF)
    s_n = jnp.sum(q8f * wnew[:, :C_DHEAD], axis=-1, keepdims=True) * scale
    m = jnp.maximum(jnp.max(s, axis=-1, keepdims=True), s_n)
    e = jnp.where(ok_w, jnp.exp(s - m), 0.0)
    e_n = jnp.exp(s_n - m)
    den = jnp.sum(e, axis=-1, keepdims=True) + e_n
    o_win = (jnp.dot(e.astype(BF16), win_v.astype(BF16), preferred_element_type=F32)
             + e_n * wnew[:, C_DHEAD:]) / den
    keep = min(WINDOW, lw + 1)
    wout_ref[0:keep - 1, :, :] = win_ref[lw + 1 - keep:lw, :, :]
    wout_ref[keep - 1:keep, 0, :] = wnew[:, :C_DHEAD]
    wout_ref[keep - 1:keep, 1, :] = wnew[:, C_DHEAD:]

    gate = 1.0 / (1.0 + jnp.exp(-sm_ref[0]))
    outs = []
    for h in range(C_HEADS):
        g0 = gate[:, GATE_LANE0 + h:GATE_LANE0 + h + 1]
        g1 = gate[:, GATE_LANE0 + C_HEADS + h:GATE_LANE0 + C_HEADS + h + 1]
        g2 = gate[:, GATE_LANE0 + 2 * C_HEADS + h:GATE_LANE0 + 2 * C_HEADS + h + 1]
        outs.append(g0 * ocmp_ref[0, h:h + 1, :] + g1 * o_sel[h:h + 1, :] + g2 * o_win[h:h + 1, :])
    o = jnp.concatenate(outs, axis=1)
    ms = jnp.mean(o * o, axis=-1, keepdims=True)
    o_ref[0] = o * lax.rsqrt(ms + NORM_EPS) * g_ref[...]


def _nsa_attend_call(sel, page_table, proj3, ocmp, cache, cache_win, layer, out_g, q_pos):
    bs = proj3.shape[0]
    lw = cache_win.shape[2]
    keep = min(WINDOW, lw + 1)
    n_pages = page_table.shape[1]
    halves = PAGE_SIZE // SEL_BLOCK

    def blk_spec(k):
        def imap(b, sel_r, pt_r):
            j = jnp.clip(sel_r[b, k], 0, n_pages * halves - 1)
            return (layer, pt_r[b, j // halves], j % halves, 0, 0)
        return pl.BlockSpec((None, None, SEL_BLOCK, NSA_ROWS, C_DHEAD), imap)

    row3 = lambda w, col: pl.BlockSpec((1, 1, w), lambda b, s_, p_: (b, 0, col))
    grid_spec = pltpu.PrefetchScalarGridSpec(
        num_scalar_prefetch=2,
        grid=(bs,),
        in_specs=[
            row3(C_WIDTH, COL_CQ // C_WIDTH),
            row3(4 * C_DHEAD, COL_CKV // (4 * C_DHEAD)),
            row3(2 * C_DHEAD, COL_WIN // (2 * C_DHEAD)),
            row3(LANES, COL_SMALL // LANES),
            pl.BlockSpec((1, 8, C_DHEAD), lambda b, s_, p_: (b, 0, 0)),
            pl.BlockSpec((None, None, lw, 2, C_DHEAD), lambda b, s_, p_: (layer, b, 0, 0, 0)),
            pl.BlockSpec((1, C_WIDTH), lambda b, s_, p_: (0, 0)),
        ] + [blk_spec(k) for k in range(SEL_TOPK)],
        out_specs=[
            pl.BlockSpec((1, 1, C_WIDTH), lambda b, s_, p_: (b, 0, 0)),
            pl.BlockSpec((None, keep, 2, C_DHEAD), lambda b, s_, p_: (b, 0, 0, 0)),
        ],
    )
    return pl.pallas_call(
        functools.partial(_nsa_attend_kernel, q_pos=q_pos),
        grid_spec=grid_spec,
        out_shape=[jax.ShapeDtypeStruct((bs, 1, C_WIDTH), F32), jax.ShapeDtypeStruct((bs, keep, 2, C_DHEAD), F32)],
        compiler_params=_cparams(("arbitrary",)),
        name="nsa_attend_sample",
    )(sel, page_table, proj3, proj3, proj3, proj3, ocmp, cache_win, out_g.reshape(1, C_WIDTH),
      *([cache] * SEL_TOPK))


def _token_mixer_tail(xa, xb, oa, ob, oc, w_out_b, ffn_g, wq_t, subkeys, u_tab, v_tab, tm, tn, et, emit_bf16=False,
                      layer=0):
    h, hn_t = _outproj_call(xa, xb, oa, ob, oc, w_out_b, ffn_g, tm)
    q_t = _peer_q_call(wq_t, hn_t, tm)
    route = _peer_route_call(q_t, subkeys)
    return h, _peer_expert_call(hn_t, route, u_tab, v_tab, tn, et, emit_bf16, layer)


def kernel(x_prompt, x_sample, cache_diff_kv, cache_nsa_kv, cache_nsa_win, state_ssm, state_conv, page_table,
           norm_mix_g, w_in, w_out, diff_lam, diff_subln_g, ssm_conv_w, ssm_conv_b, ssm_dt_bias, ssm_a_log,
           ssm_d, ssm_norm_g, nsa_pe, nsa_cmp_w1, nsa_cmp_w2, nsa_out_g, norm_ffn_g, peer_wq, peer_subkeys,
           peer_u, peer_v, norm_final_g):
    depth = w_in.shape[0]
    bp, t, _ = x_prompt.shape
    bs = x_sample.shape[0]
    past_len = page_table.shape[1] * PAGE_SIZE
    n_p = bp * t
    n_s = LANES
    tm_p = 512

    xa_p, xb_p = x_prompt.reshape(n_p, D_MODEL), None
    xa_s = jnp.pad(x_sample.reshape(bs, D_MODEL), ((0, n_s - bs), (0, 0)))
    xb_s = None
    st_p = [[] for _ in range(5)]
    st_s = [[] for _ in range(5)]
    state_ssm4 = state_ssm.reshape(depth, bs, B_HEADS * B_HEADDIM, B_STATE)
    w_in_t = jnp.transpose(w_in, (2, 0, 1))
    w_main = _cast_w_main_call(w_in_t)
    w_tail = _pack_w_tail_call(w_in_t)
    for l in range(depth):
        lam_init = 0.8 - 0.6 * math.exp(-0.3 * l)
        w_out_b = w_out[l].astype(BF16)
        wq_t = peer_wq[l].T.astype(BF16)

        proj_s, dkv_s, nkv_s = _proj_call(xa_s, xb_s, norm_mix_g[l], w_main, w_tail, l, n_s)
        proj_s = proj_s[:bs]
        proj_s3 = proj_s.reshape(bs, 1, PROJ_W)
        o_a = _diff_sample_call(page_table, proj_s3, cache_diff_kv, l, diff_lam[l], diff_subln_g[l], lam_init)
        o_b, h_new, conv_new = _ssd_sample_call(proj_s3, state_conv, state_ssm4, l, ssm_conv_w[l], ssm_conv_b[l],
                                                ssm_dt_bias[l], ssm_a_log[l], ssm_d[l], ssm_norm_g[l])
        part = _compress_paged_call(page_table, cache_nsa_kv, l, nsa_pe[l], nsa_cmp_w1[l])
        o_cmp, sel = _nsa_select_call(part, nsa_cmp_w2[l], proj_s3, past_len)
        o_c, win_out = _nsa_attend_call(sel[:, 0, :SEL_TOPK], page_table, proj_s3, o_cmp, cache_nsa_kv, cache_nsa_win, l,
                                        nsa_out_g[l], past_len)
        st_s[0].append(dkv_s[:bs].reshape(bs, 1, 2, A_HEADS, A_DHEAD))
        st_s[1].append(nkv_s[:bs].reshape(bs, 1, NSA_ROWS, C_DHEAD))
        st_s[2].append(win_out)
        st_s[3].append(h_new.reshape(bs, B_HEADS, B_HEADDIM, B_STATE))
        st_s[4].append(conv_new)
        pad = lambda a: jnp.pad(a.reshape(bs, -1), ((0, n_s - bs), (0, 0)))
        xa_s, (xb_s, u_b, vt_b) = _token_mixer_tail(xa_s, xb_s, pad(o_a), pad(o_b), pad(o_c), w_out_b, norm_ffn_g[l],
                                                   wq_t, peer_subkeys[l], peer_u, peer_v, n_s, n_s, 512, True, l)

        proj, dkv_p, nkv_p = _proj_call(xa_p, xb_p, norm_mix_g[l], w_main, w_tail, l, tm_p)
        o_a = _diff_prompt_call(proj, bp, t, diff_lam[l], diff_subln_g[l], lam_init)
        o_b, h_ssm = _ssd_prompt_call(proj, bp, t, ssm_conv_w[l], ssm_conv_b[l], ssm_dt_bias[l], ssm_a_log[l],
                                      ssm_d[l], ssm_norm_g[l])
        kc, vc = _compress_prompt_call(proj, bp, t, nsa_pe[l], nsa_cmp_w1[l], nsa_cmp_w2[l])
        o_c = _nsa_prompt_call(proj, kc, vc, bp, t, nsa_out_g[l])
        proj3 = proj.reshape(bp, t, PROJ_W)
        st_p[0].append(dkv_p.reshape(bp, t, 2, A_HEADS, A_DHEAD))
        st_p[1].append(nkv_p.reshape(bp, t, NSA_ROWS, C_DHEAD))
        keep = min(WINDOW, t)
        st_p[2].append(proj3[:, t - keep:, COL_WIN:COL_WIN + 2 * C_DHEAD].reshape(bp, keep, 2, C_DHEAD))
        st_p[3].append(h_ssm)
        st_p[4].append(proj3[:, t - (CONV_W - 1):, COL_XBC:COL_XBC + B_CONV_DIM])
        xa_p, xb_p = _token_mixer_tail(xa_p, xb_p, o_a, o_b, o_c, w_out_b, norm_ffn_g[l], wq_t, peer_subkeys[l],
                                       u_b, vt_b, tm_p, 512, 1024)

    y_p = _final_norm_call(xa_p, xb_p, norm_final_g, tm_p).reshape(bp, t, D_MODEL)
    y_s = _final_norm_call(xa_s, xb_s, norm_final_g, n_s)[:bs].reshape(bs, 1, D_MODEL)
    return (y_p, y_s) + tuple(jnp.stack(s) for s in st_p) + tuple(jnp.stack(s) for s in st_s)
```

```python
import functools
import math

import jax
import jax.numpy as jnp
from jax import lax
from jax.experimental import pallas as pl
from jax.experimental.pallas import tpu as pltpu

F32 = jnp.float32
BF16 = jnp.bfloat16

D_MODEL = 2048
A_HEADS = 4
A_HALF = 64
A_DHEAD = 128
A_WIDTH = 512
B_WIDTH = 1024
B_HEADDIM = 64
B_HEADS = 16
B_GROUPS = 4
B_STATE = 128
CONV_W = 4
B_CONV_DIM = 2048
SSD_CHUNK = 128
C_HEADS = 4
C_DHEAD = 128
C_WIDTH = 512
CMP_LEN = 32
CMP_STRIDE = 16
SEL_BLOCK = 64
SEL_TOPK = 16
WINDOW = 512
PEER_HEADS = 8
PEER_NKEYS = 128
PEER_TOPK = 16
PEER_DQ = 256
PAGE_SIZE = 128
NSA_ROWS = 4
NORM_EPS = 1e-6
NEG_INF = -1e30
FORCE_SCORE = 1e4

LANES = 128
VMEM_LIMIT = 56 * 1024 * 1024

COL_AQ = 0
COL_AK = 512
COL_AV = 1024
COL_Z = 1536
COL_XBC = 2560
COL_MAIN = 4608
COL_CQ = 4608
COL_CKV = 5120
COL_WIN = 5632
COL_SMALL = 5888
PROJ_W = 6144
GATE_LANE0 = B_HEADS

NT_DIMS = (((1,), (1,)), ((), ()))


def _cparams(sem, vmem=VMEM_LIMIT):
    return pltpu.CompilerParams(dimension_semantics=sem, vmem_limit_bytes=vmem)


def _gelu(x):
    return 0.5 * x * (1.0 + jnp.tanh(math.sqrt(2.0 / math.pi) * (x + 0.044715 * (x * x * x))))


def _silu(x):
    return x * (1.0 / (1.0 + jnp.exp(-x)))


def _softplus(x):
    return jnp.maximum(x, 0.0) + jnp.log(1.0 + jnp.exp(-jnp.abs(x)))


def _row_view(ref, rows):
    return ref.reshape(rows, ref.shape[-1])


def _alibi_slope(h):
    if isinstance(h, int):
        return 2.0 ** (-2.0 * (h + 1))
    return jnp.exp2(jnp.full((1, 1), -2.0, F32) * (h + 1).astype(F32))


def _proj_kernel(*refs, two, tc):
    if two:
        xa_ref, xb_ref, g_ref, wm_ref, wt_ref, o_ref, dkv_ref, nkv_ref, xn_ref = refs
    else:
        xa_ref, g_ref, wm_ref, wt_ref, o_ref, dkv_ref, nkv_ref, xn_ref = refs
    j = pl.program_id(1)
    n_main = COL_MAIN // tc

    @pl.when(j == 0)
    def _():
        x = xa_ref[...]
        if two:
            x = x + xb_ref[...]
        ms = jnp.mean(x * x, axis=-1, keepdims=True)
        xn_ref[...] = (x * lax.rsqrt(ms + NORM_EPS) * g_ref[...]).astype(BF16)

    def store_cache_rows(res, tile):
        for b in range(2 * A_HEADS):
            col = COL_AK + b * LANES
            if col // tc == tile:
                dkv_ref[:, b // A_HEADS, b % A_HEADS, :] = res[:, col % tc:col % tc + LANES]
        for b in range(NSA_ROWS):
            col = COL_CKV + b * LANES
            if col // tc == tile:
                nkv_ref[:, b, :] = res[:, col % tc:col % tc + LANES]

    state_tiles = sorted({(COL_AK + b * LANES) // tc for b in range(2 * A_HEADS)}
                         | {(COL_CKV + b * LANES) // tc for b in range(NSA_ROWS)})

    def finish(res, tiles):
        o_ref[...] = res
        for tile in tiles:
            if tile in state_tiles:
                pl.when(j == tile)(functools.partial(store_cache_rows, res, tile))

    @pl.when(j < n_main)
    def _():
        finish(jnp.dot(xn_ref[...], wm_ref[...], preferred_element_type=F32), range(n_main))

    @pl.when(j >= n_main)
    def _():
        finish(jnp.dot(xn_ref[...], wt_ref[...], preferred_element_type=F32), range(n_main, PROJ_W // tc))


def _cast_t_kernel(w_ref, o_ref):
    for l in range(w_ref.shape[1]):
        o_ref[l] = w_ref[:, l, :].T.astype(BF16)


def _cast_w_main_call(w_in_t, tr=384):
    depth = w_in_t.shape[1]
    return pl.pallas_call(
        _cast_t_kernel,
        grid=(COL_MAIN // tr,),
        in_specs=[pl.BlockSpec((tr, depth, D_MODEL), lambda j: (j, 0, 0))],
        out_specs=pl.BlockSpec((depth, D_MODEL, tr), lambda j: (0, 0, j)),
        out_shape=jax.ShapeDtypeStruct((depth, D_MODEL, COL_MAIN), BF16),
        compiler_params=_cparams(("parallel",)),
        name="w_in_cast",
    )(w_in_t)


def _pack_tail_kernel(w_ref, o_ref, *, n_in):
    width, depth, tk = w_ref.shape
    dt_w, gate_w = B_HEADS, 3 * C_HEADS
    cq0 = dt_w
    ckv0 = cq0 + C_WIDTH
    gate0 = n_in - COL_MAIN - gate_w
    pad = jnp.zeros((PROJ_W - n_in, tk), F32)
    for l in range(depth):
        x = w_ref[:, l, :]
        packed = jnp.concatenate([x[cq0:ckv0], x[ckv0:gate0], x[0:dt_w], x[gate0:gate0 + gate_w], pad], axis=0)
        o_ref[l] = packed.T.astype(BF16)


def _pack_w_tail_call(w_in_t, tk=512):
    width = PROJ_W - COL_MAIN
    depth = w_in_t.shape[1]
    assert COL_MAIN % width == 0
    return pl.pallas_call(
        functools.partial(_pack_tail_kernel, n_in=w_in_t.shape[0]),
        grid=(D_MODEL // tk,),
        in_specs=[pl.BlockSpec((width, depth, tk), lambda k: (COL_MAIN // width, 0, k))],
        out_specs=pl.BlockSpec((depth, tk, width), lambda k: (0, k, 0)),
        out_shape=jax.ShapeDtypeStruct((depth, D_MODEL, width), BF16),
        compiler_params=_cparams(("parallel",)),
        name="w_in_tail_pack",
    )(w_in_t)


def _proj_call(xa, xb, g, w_main, w_tail, layer, tm):
    n = xa.shape[0]
    two = xb is not None
    tc = 1536
    n_main = COL_MAIN // tc
    assert COL_MAIN % tc == 0 and w_tail.shape[2] == PROJ_W - COL_MAIN == tc
    xspec = pl.BlockSpec((tm, D_MODEL), lambda i, j: (i, 0))
    ins = [xa] + ([xb] if two else []) + [g.reshape(1, D_MODEL), w_main, w_tail]
    specs = [xspec] + ([xspec] if two else []) + [
        pl.BlockSpec((1, D_MODEL), lambda i, j: (0, 0)),
        pl.BlockSpec((None, D_MODEL, tc), lambda i, j: (layer, 0, jnp.minimum(j, n_main - 1))),
        pl.BlockSpec((None, D_MODEL, tc), lambda i, j: (layer, 0, jnp.maximum(j - n_main, 0)),
                     pipeline_mode=pl.Buffered(1)),
    ]
    return pl.pallas_call(
        functools.partial(_proj_kernel, two=two, tc=tc),
        grid=(n // tm, PROJ_W // tc),
        in_specs=specs,
        out_specs=[pl.BlockSpec((tm, tc), lambda i, j: (i, j)),
                   pl.BlockSpec((tm, 2, A_HEADS, A_DHEAD), lambda i, j: (i, 0, 0, 0)),
                   pl.BlockSpec((tm, NSA_ROWS, C_DHEAD), lambda i, j: (i, 0, 0))],
        out_shape=[jax.ShapeDtypeStruct((n, PROJ_W), F32), jax.ShapeDtypeStruct((n, 2, A_HEADS, A_DHEAD), F32),
                   jax.ShapeDtypeStruct((n, NSA_ROWS, C_DHEAD), F32)],
        scratch_shapes=[pltpu.VMEM((tm, D_MODEL), BF16)],
        compiler_params=_cparams(("parallel", "arbitrary")),
        name="in_proj",
    )(*ins)


def _diff_lambda(dl, lam_init):
    a = jnp.sum(dl[0:1] * dl[1:2], axis=-1, keepdims=True)
    b = jnp.sum(dl[2:3] * dl[3:4], axis=-1, keepdims=True)
    return jnp.exp(a) - jnp.exp(b) + lam_init


CAUSAL_LEVELS = 4


def _causal_prefixes(i, nq, tq, body):
    levels = min(CAUSAL_LEVELS, nq)
    per = nq // levels
    for lv in range(levels):
        pl.when(i // per == lv)(functools.partial(body, (lv + 1) * per * tq))


def _diff_prompt_kernel(q_ref, k_ref, v_ref, dl_ref, g_ref, o_ref, *, tq, lam_init):
    h = pl.program_id(1)
    i = pl.program_id(2)
    t = k_ref.shape[0]
    scale = A_HALF ** -0.5

    def body(nk):
        lam = _diff_lambda(dl_ref[...], lam_init)
        q = q_ref[...] * scale
        lane = lax.broadcasted_iota(jnp.int32, (1, A_DHEAD), 1)
        kb = k_ref[0:nk, :].astype(BF16)
        vb = v_ref[0:nk, :].astype(BF16)
        qpos = i * tq + lax.broadcasted_iota(jnp.int32, (tq, 1), 0)
        kpos = lax.broadcasted_iota(jnp.int32, (1, nk), 1)
        ok = qpos >= kpos
        key_bias = _alibi_slope(h) * kpos.astype(F32)

        def half_attention(c):
            qc = jnp.where((lane >= c * A_HALF) & (lane < (c + 1) * A_HALF), q, 0.0).astype(BF16)
            s = lax.dot_general(qc, kb, NT_DIMS, preferred_element_type=F32) + key_bias
            s = jnp.where(ok, s, NEG_INF)
            e = jnp.exp(s - jnp.max(s, axis=-1, keepdims=True))
            pv = jnp.dot(e.astype(BF16), vb, preferred_element_type=F32)
            return pv / jnp.sum(e, axis=-1, keepdims=True)

        o = half_attention(0) - lam * half_attention(1)
        ms = jnp.mean(o * o, axis=-1, keepdims=True)
        o_ref[...] = o * lax.rsqrt(ms + NORM_EPS) * g_ref[...] * (1.0 - lam_init)

    _causal_prefixes(i, t // tq, tq, body)


def _diff_prompt_call(proj, bsz, t, dl, subln_g, lam_init, tq=256):
    nq = t // tq
    cq, ck, cv = COL_AQ // A_DHEAD, COL_AK // A_DHEAD, COL_AV // A_DHEAD
    return pl.pallas_call(
        functools.partial(_diff_prompt_kernel, tq=tq, lam_init=lam_init),
        grid=(bsz, A_HEADS, nq),
        in_specs=[
            pl.BlockSpec((tq, A_DHEAD), lambda b, h, i: (b * nq + i, cq + h)),
            pl.BlockSpec((t, A_DHEAD), lambda b, h, i: (b, ck + h)),
            pl.BlockSpec((t, A_DHEAD), lambda b, h, i: (b, cv + h)),
            pl.BlockSpec((4, A_HALF), lambda b, h, i: (0, 0)),
            pl.BlockSpec((1, A_DHEAD), lambda b, h, i: (0, 0)),
        ],
        out_specs=pl.BlockSpec((tq, A_DHEAD), lambda b, h, i: (b * nq + i, h)),
        out_shape=jax.ShapeDtypeStruct((bsz * t, A_WIDTH), F32),
        compiler_params=_cparams(("parallel", "parallel", "arbitrary")),
        name="diff_attn_prompt",
    )(proj, proj, proj, dl, subln_g.reshape(1, A_DHEAD))


SSD_COLS = 512


def _ssd_prompt_kernel(*refs):
    nx, nz = B_CONV_DIM // SSD_COLS, B_WIDTH // SSD_COLS
    xbc_refs, z_refs = refs[:nx], refs[nx:nx + nz]
    sm_ref, cw_ref, cb_ref, dtb_ref, alog_ref, dsk_ref, g_ref, o_ref, hout_ref, buf_ref, h_ref = refs[nx + nz:]
    c = pl.program_id(1)
    cs = SSD_CHUNK

    @pl.when(c == 0)
    def _():
        buf_ref[0:8, :] = jnp.zeros((8, B_CONV_DIM), F32)
        h_ref[...] = jnp.zeros_like(h_ref)

    xbc = jnp.concatenate([r[...] for r in xbc_refs], axis=1)
    buf_ref[8:8 + cs, :] = xbc
    cw = cw_ref[...]
    conv = cb_ref[...] + cw[3:4] * xbc
    for j in range(1, CONV_W):
        conv = conv + cw[3 - j:4 - j] * buf_ref[8 - j:8 - j + cs, :]
    buf_ref[0:8, :] = xbc[cs - 8:cs, :]
    xc = _silu(conv)
    xs = xc[:, :B_WIDTH]

    dt = _softplus(sm_ref[...] + dtb_ref[...])
    a_neg = -jnp.exp(alog_ref[...])
    dta = dt * a_neg
    row = lax.broadcasted_iota(jnp.int32, (cs, cs), 0)
    col = lax.broadcasted_iota(jnp.int32, (cs, cs), 1)
    causal = row >= col
    acum = jnp.dot(causal.astype(F32), dta, preferred_element_type=F32, precision=lax.Precision.HIGHEST)
    acum_t = acum.T
    dt_t = dt.T
    lane = lax.broadcasted_iota(jnp.int32, (1, LANES), 1)
    lo = lane < B_HEADDIM

    ys = []
    for g in range(B_GROUPS):
        bg = xc[:, B_WIDTH + g * B_STATE:B_WIDTH + (g + 1) * B_STATE]
        cg = xc[:, B_WIDTH + B_GROUPS * B_STATE + g * B_STATE:B_WIDTH + B_GROUPS * B_STATE + (g + 1) * B_STATE]
        bgb = bg.astype(BF16)
        cgb = cg.astype(BF16)
        cb = lax.dot_general(cgb, bgb, NT_DIMS, preferred_element_type=F32)
        for pr in range(2):
            h0 = g * 4 + pr * 2
            xpair = xs[:, h0 * B_HEADDIM:(h0 + 2) * B_HEADDIM]
            xpb = xpair.astype(BF16)
            ydiag = []
            ecol = []
            wcol = []
            elast = []
            for hh in (h0, h0 + 1):
                a_col = acum[:, hh:hh + 1]
                a_row = acum_t[hh:hh + 1, :]
                decay = jnp.exp(jnp.where(causal, a_col - a_row, NEG_INF))
                lm = cb * decay * dt_t[hh:hh + 1, :]
                ydiag.append(jnp.dot(lm.astype(BF16), xpb, preferred_element_type=F32))
                a_last = acum[cs - 1:cs, hh:hh + 1]
                ecol.append(jnp.exp(a_col))
                wcol.append(jnp.exp(a_last - a_col) * dt[:, hh:hh + 1])
                elast.append(jnp.exp(a_last))
            hp = h_ref[h0 * B_HEADDIM:(h0 + 2) * B_HEADDIM, :]
            yoff = lax.dot_general(cgb, hp.astype(BF16), NT_DIMS, preferred_element_type=F32)
            y = jnp.where(lo, ydiag[0], ydiag[1]) + yoff * jnp.where(lo, ecol[0], ecol[1])
            ys.append(y)
            wx = xpair * jnp.where(lo, wcol[0], wcol[1])
            upd = jnp.dot(wx.T.astype(BF16), bgb, preferred_element_type=F32)
            prow = lax.broadcasted_iota(jnp.int32, (LANES, 1), 0) < B_HEADDIM
            h_ref[h0 * B_HEADDIM:(h0 + 2) * B_HEADDIM, :] = jnp.where(prow, elast[0], elast[1]) * hp + upd

    y = jnp.concatenate(ys, axis=1)
    y = y + dsk_ref[...] * xs
    y = y * _silu(jnp.concatenate([r[...] for r in z_refs], axis=1))
    ms = jnp.mean(y * y, axis=-1, keepdims=True)
    o_ref[...] = y * lax.rsqrt(ms + NORM_EPS) * g_ref[...]

    @pl.when(c == pl.num_programs(1) - 1)
    def _():
        hout_ref[0] = h_ref[...]


def _pad_lanes(v, fill=0.0):
    v = v.reshape(1, -1).astype(F32)
    return jnp.pad(v, ((0, 0), (0, LANES - v.shape[1])), constant_values=fill)


def _ssd_prompt_call(proj, bsz, t, conv_w, conv_b, dt_bias, a_log, d_skip, norm_g):
    nc = t // SSD_CHUNK
    cs = SSD_CHUNK
    const = lambda b, c: (0, 0)
    o, hout = pl.pallas_call(
        _ssd_prompt_kernel,
        grid=(bsz, nc),
        in_specs=[
            *[pl.BlockSpec((cs, SSD_COLS), functools.partial(lambda k, b, c: (b * nc + c, COL_XBC // SSD_COLS + k), k))
              for k in range(B_CONV_DIM // SSD_COLS)],
            *[pl.BlockSpec((cs, SSD_COLS), functools.partial(lambda k, b, c: (b * nc + c, COL_Z // SSD_COLS + k), k))
              for k in range(B_WIDTH // SSD_COLS)],
            pl.BlockSpec((cs, LANES), lambda b, c: (b * nc + c, COL_SMALL // LANES)),
            pl.BlockSpec((CONV_W, B_CONV_DIM), const),
            pl.BlockSpec((1, B_CONV_DIM), const),
            pl.BlockSpec((1, LANES), const),
            pl.BlockSpec((1, LANES), const),
            pl.BlockSpec((1, B_WIDTH), const),
            pl.BlockSpec((1, B_WIDTH), const),
        ],
        out_specs=[
            pl.BlockSpec((cs, B_WIDTH), lambda b, c: (b * nc + c, 0)),
            pl.BlockSpec((1, B_HEADS * B_HEADDIM, B_STATE), lambda b, c: (b, 0, 0)),
        ],
        out_shape=[
            jax.ShapeDtypeStruct((bsz * t, B_WIDTH), F32),
            jax.ShapeDtypeStruct((bsz, B_HEADS * B_HEADDIM, B_STATE), F32),
        ],
        scratch_shapes=[pltpu.VMEM((8 + cs, B_CONV_DIM), F32), pltpu.VMEM((B_HEADS * B_HEADDIM, B_STATE), F32)],
        compiler_params=_cparams(("parallel", "arbitrary")),
        name="ssd_prompt",
    )(*([proj] * (B_CONV_DIM // SSD_COLS + B_WIDTH // SSD_COLS + 1)), conv_w, conv_b.reshape(1, -1),
      _pad_lanes(dt_bias), _pad_lanes(a_log), jnp.repeat(d_skip, B_HEADDIM).reshape(1, B_WIDTH),
      norm_g.reshape(1, B_WIDTH))
    return o, hout.reshape(bsz, B_HEADS, B_HEADDIM, B_STATE)


def _compress_partials(load_rows, pe_ref, w1_ref, kv, nchunk):
    acc = jnp.zeros((nchunk + 8, 2 * C_DHEAD), F32)
    tail = jnp.zeros((6, C_DHEAD), F32)
    for r in range(CMP_STRIDE):
        x = jnp.concatenate([load_rows(r), pe_ref[kv, r:r + 1, :], pe_ref[kv, CMP_STRIDE + r:CMP_STRIDE + r + 1, :], tail],
                            axis=0).astype(BF16)
        w = jnp.concatenate([w1_ref[kv, r], w1_ref[kv, CMP_STRIDE + r]], axis=1).astype(BF16)
        acc = acc + jnp.dot(x, w, preferred_element_type=F32)
    lo = acc[0:nchunk, :C_DHEAD] + acc[nchunk:nchunk + 1, :C_DHEAD]
    hi = acc[0:nchunk, C_DHEAD:] + acc[nchunk + 1:nchunk + 2, C_DHEAD:]
    return lo, hi


def _compress_kernel(k_ref, v_ref, pe_ref, w1_ref, w2_ref, kc_ref, vc_ref, *, nchunk):
    outs = []
    for kv, rows_ref in enumerate((k_ref, v_ref)):
        acc_lo, acc_hi = _compress_partials(lambda r: rows_ref[pl.ds(r, nchunk, stride=CMP_STRIDE), :],
                                            pe_ref, w1_ref, kv, nchunk)
        hid = acc_lo + pltpu.roll(acc_hi, nchunk - 1, 0)
        outs.append(jnp.dot(_gelu(hid).astype(BF16), w2_ref[kv].astype(BF16), preferred_element_type=F32))
    kc_ref[0] = outs[0]
    vc_ref[0] = outs[1]


def _compress_prompt_call(proj, bsz, t, pe, w1, w2):
    nchunk = t // CMP_STRIDE
    shp = jax.ShapeDtypeStruct((bsz, nchunk, C_DHEAD), F32)
    return pl.pallas_call(
        functools.partial(_compress_kernel, nchunk=nchunk),
        grid=(bsz,),
        in_specs=[
            pl.BlockSpec((t, C_DHEAD), lambda b: (b, COL_CKV // C_DHEAD)),
            pl.BlockSpec((t, C_DHEAD), lambda b: (b, COL_CKV // C_DHEAD + 1)),
            pl.BlockSpec((2, CMP_LEN, C_DHEAD), lambda b: (0, 0, 0)),
            pl.BlockSpec((2, CMP_LEN, C_DHEAD, C_DHEAD), lambda b: (0, 0, 0, 0)),
            pl.BlockSpec((2, C_DHEAD, C_DHEAD), lambda b: (0, 0, 0)),
        ],
        out_specs=[pl.BlockSpec((1, nchunk, C_DHEAD), lambda b: (b, 0, 0))] * 2,
        out_shape=[shp, shp],
        compiler_params=_cparams(("parallel",)),
        name="nsa_compress_prompt",
    )(proj, proj, pe, w1.reshape(2, CMP_LEN, C_DHEAD, C_DHEAD), w2)


def _masked_softmax(s, ok):
    s = jnp.where(ok, s, NEG_INF)
    m = jnp.max(s, axis=-1, keepdims=True)
    e = jnp.where(ok, jnp.exp(s - m), 0.0)
    return e, jnp.sum(e, axis=-1, keepdims=True)


def _topk_mask_lanes(score, k, n):
    lane = lax.broadcasted_iota(jnp.int32, (1, LANES), 1)
    rank = jnp.zeros(score.shape, F32)
    for i in range(n):
        ci = score[:, i:i + 1]
        beats = (ci > score) | ((ci == score) & (lane > i))
        rank = rank + jnp.where(beats, 1.0, 0.0)
    return (rank < k) & (lane < n)


def _nsa_prompt_kernel(q_ref, ks_ref, vs_ref, kw_ref, vw_ref, kc_ref, vc_ref, sm_ref, g_ref, o_ref, osel_ref, *, tq, n_cmp):
    i = pl.program_id(1)
    t = ks_ref.shape[0]
    n_sel = t // SEL_BLOCK
    scale = C_DHEAD ** -0.5
    sel_shift = SEL_BLOCK.bit_length() - 1
    wlen = min(t, WINDOW + tq)

    qpos = i * tq + lax.broadcasted_iota(jnp.int32, (tq, 1), 0)
    lane = lax.broadcasted_iota(jnp.int32, (1, LANES), 1)

    cmp_end = lane * CMP_STRIDE + (CMP_LEN - 1)
    dist_c = qpos - cmp_end
    ok_c = (dist_c >= 0) & (lane < n_cmp)
    dist_cf = dist_c.astype(F32)
    kcb = kc_ref[0].astype(BF16)
    vcb = vc_ref[0].astype(BF16)
    qs = [(q_ref[:, h * C_DHEAD:(h + 1) * C_DHEAD] * scale).astype(BF16) for h in range(C_HEADS)]
    o_cmp = []
    psum = jnp.zeros((tq, LANES), F32)
    for h in range(C_HEADS):
        s = lax.dot_general(qs[h], kcb, NT_DIMS, preferred_element_type=F32)
        s = s - _alibi_slope(h) * dist_cf
        e, den = _masked_softmax(s, ok_c)
        p = e / jnp.maximum(den, 1e-30)
        psum = psum + p
        o_cmp.append(jnp.dot(p.astype(BF16), vcb, preferred_element_type=F32))

    n_i = lax.broadcasted_iota(jnp.int32, (LANES, LANES), 0)
    j_i = lax.broadcasted_iota(jnp.int32, (LANES, LANES), 1)
    lo_ = jnp.maximum(n_i * CMP_STRIDE, j_i * SEL_BLOCK)
    hi_ = jnp.minimum(n_i * CMP_STRIDE + CMP_LEN, (j_i + 1) * SEL_BLOCK)
    ovl = jnp.maximum(hi_ - lo_, 0).astype(F32) * (1.0 / CMP_LEN)
    ovl = jnp.where((n_i < n_cmp) & (j_i < n_sel), ovl, 0.0)
    imp = jnp.dot(psum, ovl, preferred_element_type=F32, precision=lax.Precision.HIGHEST)
    qblk = qpos >> sel_shift
    sel_valid = lane <= qblk
    forced = (lane == 0) | (lane == qblk) | (lane == qblk - 1)
    score = jnp.where(sel_valid, imp + jnp.where(forced, FORCE_SCORE, 0.0), NEG_INF)
    score = jnp.where(lane < n_sel, score, -jnp.inf)
    chosen = _topk_mask_lanes(score, min(SEL_TOPK, n_sel), n_sel) & sel_valid
    chosen_b = jnp.where(chosen, 1.0, 0.0).astype(BF16)

    def attend(qh, kb, vb, key_bias, ok):
        s = lax.dot_general(qh, kb, NT_DIMS, preferred_element_type=F32) + key_bias
        s = jnp.where(ok, s, NEG_INF)
        e = jnp.exp(s - jnp.max(s, axis=-1, keepdims=True))
        return jnp.dot(e.astype(BF16), vb, preferred_element_type=F32) / jnp.sum(e, axis=-1, keepdims=True)

    def selected_branch(nk):
        e_j = lax.broadcasted_iota(jnp.int32, (LANES, nk), 0)
        e_k = lax.broadcasted_iota(jnp.int32, (LANES, nk), 1)
        expand = jnp.where((e_k >> sel_shift) == e_j, 1.0, 0.0).astype(BF16)
        key_sel = jnp.dot(chosen_b, expand, preferred_element_type=F32) > 0.5
        kpos = lax.broadcasted_iota(jnp.int32, (1, nk), 1)
        kpos_f = kpos.astype(F32)
        ok_s = key_sel & (qpos >= kpos)
        ksb = ks_ref[0:nk, :].astype(BF16)
        vsb = vs_ref[0:nk, :].astype(BF16)
        for h in range(C_HEADS):
            osel_ref[:, h * C_DHEAD:(h + 1) * C_DHEAD] = attend(qs[h], ksb, vsb, _alibi_slope(h) * kpos_f, ok_s)

    nq = t // tq
    if nq % 2 == 0:
        pl.when(i < nq // 2)(functools.partial(selected_branch, t // 2))
        pl.when(i >= nq // 2)(functools.partial(selected_branch, t))
    else:
        selected_branch(t)

    w0 = pl.multiple_of(jnp.clip(i * tq - WINDOW, 0, t - wlen), 8)
    wpos = w0 + lax.broadcasted_iota(jnp.int32, (1, wlen), 1)
    wpos_f = wpos.astype(F32)
    dist_w = qpos - wpos
    ok_w = (dist_w >= 0) & (dist_w < WINDOW)
    kwb = kw_ref[pl.ds(w0, wlen), :].astype(BF16)
    vwb = vw_ref[pl.ds(w0, wlen), :].astype(BF16)
    gate = 1.0 / (1.0 + jnp.exp(-sm_ref[...]))
    outs = []
    for h in range(C_HEADS):
        o_sel = osel_ref[:, h * C_DHEAD:(h + 1) * C_DHEAD]
        o_win = attend(qs[h], kwb, vwb, _alibi_slope(h) * wpos_f, ok_w)
        g0 = gate[:, GATE_LANE0 + h:GATE_LANE0 + h + 1]
        g1 = gate[:, GATE_LANE0 + C_HEADS + h:GATE_LANE0 + C_HEADS + h + 1]
        g2 = gate[:, GATE_LANE0 + 2 * C_HEADS + h:GATE_LANE0 + 2 * C_HEADS + h + 1]
        outs.append(g0 * o_cmp[h] + g1 * o_sel + g2 * o_win)
    o = jnp.concatenate(outs, axis=1)
    ms = jnp.mean(o * o, axis=-1, keepdims=True)
    o_ref[...] = o * lax.rsqrt(ms + NORM_EPS) * g_ref[...]


def _nsa_prompt_call(proj, kc, vc, bsz, t, out_g, tq=256):
    nq = t // tq
    n_cmp = (t - CMP_LEN) // CMP_STRIDE + 1
    c0 = COL_CKV // C_DHEAD
    w0 = COL_WIN // C_DHEAD
    kvspec = lambda col: pl.BlockSpec((t, C_DHEAD), lambda b, i: (b, col))
    return pl.pallas_call(
        functools.partial(_nsa_prompt_kernel, tq=tq, n_cmp=n_cmp),
        grid=(bsz, nq),
        in_specs=[
            pl.BlockSpec((tq, C_WIDTH), lambda b, i: (b * nq + i, COL_CQ // C_WIDTH)),
            kvspec(c0 + 2), kvspec(c0 + 3), kvspec(w0), kvspec(w0 + 1),
            pl.BlockSpec((1, kc.shape[1], C_DHEAD), lambda b, i: (b, 0, 0)),
            pl.BlockSpec((1, kc.shape[1], C_DHEAD), lambda b, i: (b, 0, 0)),
            pl.BlockSpec((tq, LANES), lambda b, i: (b * nq + i, COL_SMALL // LANES)),
            pl.BlockSpec((1, C_WIDTH), lambda b, i: (0, 0)),
        ],
        out_specs=pl.BlockSpec((tq, C_WIDTH), lambda b, i: (b * nq + i, 0)),
        out_shape=jax.ShapeDtypeStruct((bsz * t, C_WIDTH), F32),
        scratch_shapes=[pltpu.VMEM((tq, C_WIDTH), F32)],
        compiler_params=_cparams(("parallel", "arbitrary")),
        name="nsa_attn_prompt",
    )(proj, proj, proj, proj, proj, kc, vc, proj, out_g.reshape(1, C_WIDTH))


def _outproj_kernel(*refs, two):
    if two:
        xa_ref, xb_ref, oa_ref, ob_ref, oc_ref, w_ref, g_ref, h_ref, hnt_ref = refs
    else:
        xa_ref, oa_ref, ob_ref, oc_ref, w_ref, g_ref, h_ref, hnt_ref = refs
    x = xa_ref[...]
    if two:
        x = x + xb_ref[...]
    mixed = jnp.dot(oa_ref[...].astype(BF16), w_ref[0:A_WIDTH, :], preferred_element_type=F32)
    mixed = mixed + jnp.dot(ob_ref[...].astype(BF16), w_ref[A_WIDTH:A_WIDTH + B_WIDTH, :], preferred_element_type=F32)
    mixed = mixed + jnp.dot(oc_ref[...].astype(BF16), w_ref[A_WIDTH + B_WIDTH:, :], preferred_element_type=F32)
    h = x + mixed
    h_ref[...] = h
    ms = jnp.mean(h * h, axis=-1, keepdims=True)
    hnt_ref[...] = (h * lax.rsqrt(ms + NORM_EPS) * g_ref[...]).T.astype(BF16)


def _outproj_call(xa, xb, oa, ob, oc, w_out_b, g, tm):
    n = xa.shape[0]
    two = xb is not None
    row = lambda w: pl.BlockSpec((tm, w), lambda i: (i, 0))
    ins = [xa] + ([xb] if two else []) + [oa, ob, oc, w_out_b, g.reshape(1, D_MODEL)]
    specs = [row(D_MODEL)] + ([row(D_MODEL)] if two else []) + [
        row(A_WIDTH), row(B_WIDTH), row(C_WIDTH),
        pl.BlockSpec((D_MODEL, D_MODEL), lambda i: (0, 0)),
        pl.BlockSpec((1, D_MODEL), lambda i: (0, 0)),
    ]
    return pl.pallas_call(
        functools.partial(_outproj_kernel, two=two),
        grid=(n // tm,),
        in_specs=specs,
        out_specs=[row(D_MODEL), pl.BlockSpec((D_MODEL, tm), lambda i: (0, i))],
        out_shape=[jax.ShapeDtypeStruct((n, D_MODEL), F32), jax.ShapeDtypeStruct((D_MODEL, n), BF16)],
        compiler_params=_cparams(("parallel",)),
        name="out_proj",
    )(*ins)


def _peer_q_kernel(wqt_ref, hnt_ref, qt_ref):
    qt_ref[...] = jnp.dot(wqt_ref[...], hnt_ref[...], preferred_element_type=F32)


def _peer_q_call(wq_t, hn_t, tm):
    n = hn_t.shape[1]
    dq = wq_t.shape[0]
    return pl.pallas_call(
        _peer_q_kernel,
        grid=(n // tm,),
        in_specs=[pl.BlockSpec((dq, D_MODEL), lambda i: (0, 0)), pl.BlockSpec((D_MODEL, tm), lambda i: (0, i))],
        out_specs=pl.BlockSpec((dq, tm), lambda i: (0, i)),
        out_shape=jax.ShapeDtypeStruct((dq, n), F32),
        compiler_params=_cparams(("parallel",)),
        name="peer_query",
    )(wq_t, hn_t)


NOT_RANKED = 99.0


def _top_rows(s, pos, k, want_rank):
    rank = jnp.full(s.shape, NOT_RANKED, F32) if want_rank else None
    vals, picks = [], []
    for j in range(k):
        m = jnp.max(s, axis=0, keepdims=True)
        idx = jnp.min(jnp.where(s == m, pos, 1e9), axis=0, keepdims=True)
        hit = pos == idx
        if want_rank:
            rank = jnp.where(hit, float(j), rank)
        s = jnp.where(hit, -jnp.inf, s)
        vals.append(m)
        picks.append(idx)
    return jnp.concatenate(vals, axis=0), jnp.concatenate(picks, axis=0), rank


PAIR_ROWS = PEER_TOPK + 7 * 8 + 8


def _pair_candidates(v1, v2):
    tn = v1.shape[1]
    parts = [v1[0:1, :] + v2] + [v1[a:a + 1, :] + v2[0:8, :] for a in range(1, 8)] + [v1[8:16, :] + v2[0:1, :]]
    r = lax.broadcasted_iota(jnp.int32, (PAIR_ROWS, tn), 0)
    mid = r - PEER_TOPK
    pos = jnp.where(r < PEER_TOPK, r,
                    jnp.where(r < PEER_TOPK + 56, ((mid >> 3) + 1) * PEER_TOPK + (mid & 7), (r - 64) * PEER_TOPK))
    return jnp.concatenate(parts, axis=0), pos.astype(F32)


def _peer_route_kernel(qt_ref, sk_ref, lim_ref, coef_ref, rank2_ref, e2_ref, *, heads):
    half = PEER_DQ // 2
    row = lax.broadcasted_iota(jnp.int32, (PEER_NKEYS, qt_ref.shape[1]), 0).astype(F32)
    for hh in range(heads):
        q = qt_ref[hh * PEER_DQ:(hh + 1) * PEER_DQ, :]
        s1 = jnp.dot(sk_ref[hh, 0], q[0:half, :], preferred_element_type=F32, precision=lax.Precision.HIGHEST)
        s2 = jnp.dot(sk_ref[hh, 1], q[half:, :], preferred_element_type=F32, precision=lax.Precision.HIGHEST)
        v1, _, rank1 = _top_rows(s1, row, PEER_TOPK, True)
        v2, _, rank2 = _top_rows(s2, row, PEER_TOPK, True)
        cand, cpos = _pair_candidates(v1, v2)
        top, pos, _ = _top_rows(cand, cpos, PEER_TOPK, False)
        z = jnp.sum(jnp.exp(top - top[0:1, :]), axis=0, keepdims=True)
        a_of = jnp.floor(pos * (1.0 / PEER_TOPK))
        lim = jnp.zeros(s1.shape, F32)
        for a in range(PEER_TOPK):
            cnt = jnp.sum(jnp.where(a_of == float(a), 1.0, 0.0), axis=0, keepdims=True)
            lim = jnp.where(rank1 == float(a), cnt, lim)
        lim_ref[hh] = lim
        coef_ref[hh] = jnp.exp(s1 - v1[0:1, :]) / z
        rank2_ref[hh] = rank2.astype(BF16)
        e2_ref[hh] = jnp.exp(s2 - v2[0:1, :]).astype(BF16)


def _peer_route_call(q_t, subkeys, tn=LANES, heads=4):
    n = q_t.shape[1]
    shp = lambda dt: jax.ShapeDtypeStruct((PEER_HEADS, PEER_NKEYS, n), dt)
    ospec = pl.BlockSpec((heads, PEER_NKEYS, tn), lambda j, h: (h, 0, j))
    return pl.pallas_call(
        functools.partial(_peer_route_kernel, heads=heads),
        grid=(n // tn, PEER_HEADS // heads),
        in_specs=[
            pl.BlockSpec((heads * PEER_DQ, tn), lambda j, h: (h, j)),
            pl.BlockSpec((heads, 2, PEER_NKEYS, PEER_DQ // 2), lambda j, h: (h, 0, 0, 0)),
        ],
        out_specs=[ospec] * 4,
        out_shape=[shp(F32), shp(F32), shp(BF16), shp(BF16)],
        compiler_params=_cparams(("parallel", "arbitrary")),
        name="peer_route",
    )(q_t, subkeys)


def _peer_expert_kernel(hnt_ref, lim_ref, coef_ref, rank2_ref, e2_ref, u_ref, v_ref, o_ref, *rest, et, emit):
    acc_ref = rest[-1]
    t = pl.program_id(1)

    @pl.when(t == 0)
    def _():
        acc_ref[...] = jnp.zeros_like(acc_ref)

    if emit:
        ub = u_ref[...].astype(BF16)
        vtb = v_ref[...].T.astype(BF16)
        rest[0][...] = ub
        rest[1][...] = vtb
    else:
        ub = u_ref[...]
        vtb = v_ref[...]
    tn = hnt_ref.shape[1]
    hid = jnp.dot(ub, hnt_ref[...], preferred_element_type=F32)
    acts = []
    for ii in range(et // PEER_NKEYS):
        i1 = t * (et // PEER_NKEYS) + ii
        gate = jnp.zeros((PEER_NKEYS, tn), BF16)
        for h in range(PEER_HEADS):
            lim = lim_ref[h, pl.ds(i1, 1), :].astype(BF16)
            coef = coef_ref[h, pl.ds(i1, 1), :].astype(BF16)
            gate = gate + jnp.where(rank2_ref[h] < lim, e2_ref[h], jnp.zeros((), BF16)) * coef
        acts.append(gate * _gelu(hid[ii * PEER_NKEYS:(ii + 1) * PEER_NKEYS, :]).astype(BF16))
    acc_ref[...] += jnp.dot(vtb, jnp.concatenate(acts, axis=0), preferred_element_type=F32)

    @pl.when(t == pl.num_programs(1) - 1)
    def _():
        o_ref[...] = acc_ref[...].T


def _peer_expert_call(hn_t, route, u_tab, v_tab, tn, et, emit=False, layer=0):
    n = hn_t.shape[1]
    n_exp = u_tab.shape[-2] if emit else u_tab.shape[0]
    rspec = pl.BlockSpec((PEER_HEADS, PEER_NKEYS, tn), lambda j, t: (0, 0, j))
    uspec = pl.BlockSpec((et, D_MODEL), lambda j, t: (t, 0))
    vtspec = pl.BlockSpec((D_MODEL, et), lambda j, t: (0, t))
    ospec = pl.BlockSpec((tn, D_MODEL), lambda j, t: (j, 0))
    oshape = jax.ShapeDtypeStruct((n, D_MODEL), F32)
    if emit:
        assert n == tn
        out_specs = [ospec, uspec, vtspec]
        out_shape = [oshape, jax.ShapeDtypeStruct((n_exp, D_MODEL), BF16), jax.ShapeDtypeStruct((D_MODEL, n_exp), BF16)]
    else:
        out_specs, out_shape = ospec, oshape
    return pl.pallas_call(
        functools.partial(_peer_expert_kernel, et=et, emit=emit),
        grid=(n // tn, n_exp // et),
        in_specs=[pl.BlockSpec((D_MODEL, tn), lambda j, t: (0, j)), rspec, rspec, rspec, rspec]
        + ([pl.BlockSpec((None, et, D_MODEL), lambda j, t: (layer, t, 0))] * 2 if emit else [uspec, vtspec]),
        out_specs=out_specs,
        out_shape=out_shape,
        scratch_shapes=[pltpu.VMEM((D_MODEL, tn), F32)],
        compiler_params=_cparams(("parallel", "arbitrary")),
        name="peer_experts",
    )(hn_t, *route, u_tab, v_tab)


def _final_norm_kernel(xa_ref, xb_ref, g_ref, o_ref):
    x = xa_ref[...] + xb_ref[...]
    ms = jnp.mean(x * x, axis=-1, keepdims=True)
    o_ref[...] = x * lax.rsqrt(ms + NORM_EPS) * g_ref[...]


def _final_norm_call(xa, xb, g, tm):
    n = xa.shape[0]
    row = pl.BlockSpec((tm, D_MODEL), lambda i: (i, 0))
    return pl.pallas_call(
        _final_norm_kernel,
        grid=(n // tm,),
        in_specs=[row, row, pl.BlockSpec((1, D_MODEL), lambda i: (0, 0))],
        out_specs=row,
        out_shape=jax.ShapeDtypeStruct((n, D_MODEL), F32),
        compiler_params=_cparams(("parallel",)),
        name="final_norm",
    )(xa, xb, g.reshape(1, D_MODEL))


DIFF_PAGES = 32


def _diff_sample_kernel(pt_ref, q_ref, knew_ref, vnew_ref, dl_ref, g_ref, *rest, past_len, lam_init):
    page_refs = [_row_view(r, PAGE_SIZE * 2 * A_HEADS) for r in rest[:DIFF_PAGES]]
    o_ref, m_ref, l_ref, acc_ref = rest[DIFF_PAGES:]
    p = pl.program_id(1)
    scale = A_HALF ** -0.5
    nrow = 2 * A_HEADS
    per_key = 2 * A_HEADS

    @pl.when(p == 0)
    def _():
        m_ref[...] = jnp.full(m_ref.shape, NEG_INF, F32)
        l_ref[...] = jnp.zeros(l_ref.shape, F32)
        acc_ref[...] = jnp.zeros(acc_ref.shape, F32)

    row = lax.broadcasted_iota(jnp.int32, (nrow, 1), 0)
    lane = lax.broadcasted_iota(jnp.int32, (1, A_DHEAD), 1)
    slope = jnp.exp2(-2.0 * ((row >> 1) + 1).astype(F32))
    q = q_ref[0]
    q2 = [jnp.where(((row >> 1) == h) & ((lane >= A_HALF) == ((row & 1) == 1)), q[:, h * A_DHEAD:(h + 1) * A_DHEAD], 0.0)
          for h in range(A_HEADS)]
    q2b = [x.astype(BF16) for x in q2]
    ss = []
    for g in range(DIFF_PAGES):
        sg = None
        for h in range(A_HEADS):
            kh = page_refs[g][pl.ds(h, PAGE_SIZE, stride=per_key), :].astype(BF16)
            d = lax.dot_general(q2b[h], kh, NT_DIMS, preferred_element_type=F32)
            sg = d if sg is None else sg + d
        ss.append(sg)
    s = jnp.concatenate(ss, axis=1) * scale
    nk = DIFF_PAGES * PAGE_SIZE
    kpos = p * nk + lax.broadcasted_iota(jnp.int32, (1, nk), 1)
    s = s - slope * (past_len - kpos).astype(F32)
    m_old = m_ref[:, 0:1]
    m_new = jnp.maximum(m_old, jnp.max(s, axis=-1, keepdims=True))
    alpha = jnp.exp(m_old - m_new)
    e = jnp.exp(s - m_new)
    eb = e.astype(BF16)
    l_new = alpha * l_ref[:, 0:1] + jnp.sum(e, axis=-1, keepdims=True)
    pv = jnp.zeros((nrow, A_DHEAD), F32)
    for h in range(A_HEADS):
        vh = jnp.concatenate([page_refs[g][pl.ds(A_HEADS + h, PAGE_SIZE, stride=per_key), :].astype(BF16)
                              for g in range(DIFF_PAGES)], axis=0)
        pv = pv + jnp.where((row >> 1) == h, jnp.dot(eb, vh, preferred_element_type=F32), 0.0)
    acc = alpha * acc_ref[...] + pv
    m_ref[...] = jnp.broadcast_to(m_new, m_ref.shape)
    l_ref[...] = jnp.broadcast_to(l_new, l_ref.shape)
    acc_ref[...] = acc

    @pl.when(p == pl.num_programs(1) - 1)
    def _():
        knew = knew_ref[0]
        vnew = vnew_ref[0]
        s_n = jnp.zeros((nrow, 1), F32)
        for h in range(A_HEADS):
            s_n = s_n + jnp.sum(q2[h] * knew[:, h * A_DHEAD:(h + 1) * A_DHEAD], axis=-1, keepdims=True)
        s_n = s_n * scale
        v8 = jnp.concatenate([vnew[:, (r // 2) * A_DHEAD:(r // 2 + 1) * A_DHEAD] for r in range(nrow)], axis=0)
        m_f = jnp.maximum(m_new, s_n)
        a_f = jnp.exp(m_new - m_f)
        e_n = jnp.exp(s_n - m_f)
        o8 = (a_f * acc + e_n * v8) / (a_f * l_new + e_n)
        lam = _diff_lambda(dl_ref[...], lam_init)
        outs = []
        for h in range(A_HEADS):
            oh = o8[2 * h:2 * h + 1, :] - lam * o8[2 * h + 1:2 * h + 2, :]
            ms = jnp.mean(oh * oh, axis=-1, keepdims=True)
            outs.append(oh * lax.rsqrt(ms + NORM_EPS) * g_ref[...] * (1.0 - lam_init))
        o_ref[0] = jnp.concatenate(outs, axis=1)


def _diff_sample_call(page_table, proj3, cache, layer, dl, subln_g, lam_init):
    bs, n_pages = page_table.shape
    past_len = n_pages * PAGE_SIZE
    steps = n_pages // DIFF_PAGES

    def page_spec(g):
        return pl.BlockSpec((None, None, PAGE_SIZE, 2, A_HEADS, A_DHEAD),
                            lambda b, p, pt: (layer, pt[b, p * DIFF_PAGES + g], 0, 0, 0, 0))

    grid_spec = pltpu.PrefetchScalarGridSpec(
        num_scalar_prefetch=1,
        grid=(bs, steps),
        in_specs=[
            pl.BlockSpec((1, 1, A_WIDTH), lambda b, p, pt: (b, 0, COL_AQ // A_WIDTH)),
            pl.BlockSpec((1, 1, A_WIDTH), lambda b, p, pt: (b, 0, COL_AK // A_WIDTH)),
            pl.BlockSpec((1, 1, A_WIDTH), lambda b, p, pt: (b, 0, COL_AV // A_WIDTH)),
            pl.BlockSpec((4, A_HALF), lambda b, p, pt: (0, 0)),
            pl.BlockSpec((1, A_DHEAD), lambda b, p, pt: (0, 0)),
        ] + [page_spec(g) for g in range(DIFF_PAGES)],
        out_specs=pl.BlockSpec((1, 1, A_WIDTH), lambda b, p, pt: (b, 0, 0)),
        scratch_shapes=[pltpu.VMEM((2 * A_HEADS, LANES), F32), pltpu.VMEM((2 * A_HEADS, LANES), F32),
                        pltpu.VMEM((2 * A_HEADS, A_DHEAD), F32)],
    )
    return pl.pallas_call(
        functools.partial(_diff_sample_kernel, past_len=past_len, lam_init=lam_init),
        grid_spec=grid_spec,
        out_shape=jax.ShapeDtypeStruct((bs, 1, A_WIDTH), F32),
        compiler_params=_cparams(("parallel", "arbitrary")),
        name="diff_attn_sample",
    )(page_table, proj3, proj3, proj3, dl, subln_g.reshape(1, A_DHEAD), *([cache] * DIFF_PAGES))


def _diag_rows(vec):
    n = vec.shape[1]
    r = lax.broadcasted_iota(jnp.int32, (n, n), 0)
    c = lax.broadcasted_iota(jnp.int32, (n, n), 1)
    return jnp.where(r == c, vec, 0.0)


def _ssd_sample_kernel(*refs):
    nx, nz = B_CONV_DIM // SSD_COLS, B_WIDTH // SSD_COLS
    xbc_refs, z_refs = refs[:nx], refs[nx:nx + nz]
    (sm_ref, cbuf_ref, h0_ref, cw_ref, cb_ref, dtb_ref, alog_ref, dsk_ref, g_ref,
     o_ref, hout_ref, cout_ref) = refs[nx + nz:]
    hi = lax.Precision.HIGHEST
    new = jnp.concatenate([r[0] for r in xbc_refs], axis=1)
    buf = cbuf_ref[0, 0]
    cw = cw_ref[...]
    conv = cb_ref[...] + cw[CONV_W - 1:CONV_W] * new
    for i in range(CONV_W - 1):
        conv = conv + cw[i:i + 1] * buf[i:i + 1]
    cout_ref[0] = jnp.concatenate([buf[1:CONV_W - 1], new], axis=0)
    xc = _silu(conv)
    xs = xc[:, :B_WIDTH]
    dt = _softplus(sm_ref[0] + dtb_ref[...])
    ea = jnp.exp(dt * (-jnp.exp(alog_ref[...])))
    hr = lax.broadcasted_iota(jnp.int32, (LANES, B_WIDTH), 0)
    hc = lax.broadcasted_iota(jnp.int32, (LANES, B_WIDTH), 1)
    rep = jnp.where((hc // B_HEADDIM) == hr, 1.0, 0.0)
    both = jnp.concatenate([dt, ea, jnp.zeros((6, LANES), F32)], axis=0)
    both_rep = jnp.dot(both, rep, preferred_element_type=F32, precision=hi)
    u = both_rep[0:1] * xs
    ea_rep = both_rep[1:2]
    gn = B_GROUPS * B_STATE
    rows = (B_HEADS // B_GROUPS) * B_HEADDIM
    ys = []
    for g in range(B_GROUPS):
        r0 = g * rows
        bg = xc[:, B_WIDTH + g * B_STATE:B_WIDTH + (g + 1) * B_STATE]
        cg = xc[:, B_WIDTH + gn + g * B_STATE:B_WIDTH + gn + (g + 1) * B_STATE]
        h0 = h0_ref[0, 0, r0:r0 + rows, :]
        hn = jnp.dot(_diag_rows(ea_rep[:, r0:r0 + rows]), h0, preferred_element_type=F32, precision=hi)
        hn = hn + jnp.dot(_diag_rows(u[:, r0:r0 + rows]), jnp.broadcast_to(bg, (rows, B_STATE)),
                          preferred_element_type=F32, precision=hi)
        hout_ref[0, r0:r0 + rows, :] = hn
        c8 = jnp.broadcast_to(cg, (8, B_STATE)).astype(BF16)
        ys.append(lax.dot_general(c8, hn.astype(BF16), NT_DIMS, preferred_element_type=F32)[0:1])
    y = jnp.concatenate(ys, axis=1) + dsk_ref[...] * xs
    y = y * _silu(jnp.concatenate([r[0] for r in z_refs], axis=1))
    ms = jnp.mean(y * y, axis=-1, keepdims=True)
    o_ref[0] = y * lax.rsqrt(ms + NORM_EPS) * g_ref[...]


def _ssd_sample_call(proj3, state_conv, state_ssm4, layer, conv_w, conv_b, dt_bias, a_log, d_skip, norm_g):
    bs = proj3.shape[0]
    const = lambda b: (0, 0)
    nrow = B_HEADS * B_HEADDIM
    return pl.pallas_call(
        _ssd_sample_kernel,
        grid=(bs,),
        in_specs=[
            *[pl.BlockSpec((1, 1, SSD_COLS), functools.partial(lambda k, b: (b, 0, COL_XBC // SSD_COLS + k), k))
              for k in range(B_CONV_DIM // SSD_COLS)],
            *[pl.BlockSpec((1, 1, SSD_COLS), functools.partial(lambda k, b: (b, 0, COL_Z // SSD_COLS + k), k))
              for k in range(B_WIDTH // SSD_COLS)],
            pl.BlockSpec((1, 1, LANES), lambda b: (b, 0, COL_SMALL // LANES)),
            pl.BlockSpec((1, 1, CONV_W - 1, B_CONV_DIM), lambda b: (layer, b, 0, 0)),
            pl.BlockSpec((1, 1, nrow, B_STATE), lambda b: (layer, b, 0, 0)),
            pl.BlockSpec((CONV_W, B_CONV_DIM), const),
            pl.BlockSpec((1, B_CONV_DIM), const),
            pl.BlockSpec((1, LANES), const),
            pl.BlockSpec((1, LANES), const),
            pl.BlockSpec((1, B_WIDTH), const),
            pl.BlockSpec((1, B_WIDTH), const),
        ],
        out_specs=[
            pl.BlockSpec((1, 1, B_WIDTH), lambda b: (b, 0, 0)),
            pl.BlockSpec((1, nrow, B_STATE), lambda b: (b, 0, 0)),
            pl.BlockSpec((1, CONV_W - 1, B_CONV_DIM), lambda b: (b, 0, 0)),
        ],
        out_shape=[
            jax.ShapeDtypeStruct((bs, 1, B_WIDTH), F32),
            jax.ShapeDtypeStruct((bs, nrow, B_STATE), F32),
            jax.ShapeDtypeStruct((bs, CONV_W - 1, B_CONV_DIM), F32),
        ],
        compiler_params=_cparams(("parallel",)),
        name="ssd_sample",
    )(*([proj3] * (B_CONV_DIM // SSD_COLS + B_WIDTH // SSD_COLS + 1)), state_conv, state_ssm4, conv_w,
      conv_b.reshape(1, -1), _pad_lanes(dt_bias),
      _pad_lanes(a_log), jnp.repeat(d_skip, B_HEADDIM).reshape(1, B_WIDTH), norm_g.reshape(1, B_WIDTH))


CMP_PAGES = 32


def _compress_paged_kernel(pt_ref, pe_ref, w1_ref, *rest):
    page_refs = [_row_view(r, PAGE_SIZE * NSA_ROWS) for r in rest[:CMP_PAGES]]
    o_ref = rest[CMP_PAGES]
    per_page = PAGE_SIZE // CMP_STRIDE
    outs = []
    for kv in range(2):
        def load_rows(r, kv=kv):
            return jnp.concatenate([page_refs[g][pl.ds(NSA_ROWS * r + kv, per_page, stride=NSA_ROWS * CMP_STRIDE), :]
                                    for g in range(CMP_PAGES)], axis=0)
        outs += list(_compress_partials(load_rows, pe_ref, w1_ref, kv, CMP_PAGES * per_page))
    o_ref[0] = jnp.concatenate(outs, axis=1)


def _compress_paged_call(page_table, cache, layer, pe, w1):
    bs, n_pages = page_table.shape
    steps = n_pages // CMP_PAGES
    per_page = PAGE_SIZE // CMP_STRIDE

    def page_spec(g):
        return pl.BlockSpec((None, None, PAGE_SIZE, NSA_ROWS, C_DHEAD),
                            lambda b, p, pt: (layer, pt[b, p * CMP_PAGES + g], 0, 0, 0))

    grid_spec = pltpu.PrefetchScalarGridSpec(
        num_scalar_prefetch=1,
        grid=(bs, steps),
        in_specs=[
            pl.BlockSpec((2, CMP_LEN, C_DHEAD), lambda b, p, pt: (0, 0, 0)),
            pl.BlockSpec((2, CMP_LEN, C_DHEAD, C_DHEAD), lambda b, p, pt: (0, 0, 0, 0)),
        ] + [page_spec(g) for g in range(CMP_PAGES)],
        out_specs=pl.BlockSpec((1, CMP_PAGES * per_page, 4 * C_DHEAD), lambda b, p, pt: (b, p, 0)),
    )
    return pl.pallas_call(
        _compress_paged_kernel,
        grid_spec=grid_spec,
        out_shape=jax.ShapeDtypeStruct((bs, n_pages * per_page, 4 * C_DHEAD), F32),
        compiler_params=_cparams(("parallel", "arbitrary")),
        name="nsa_compress_sample",
    )(page_table, pe, w1.reshape(2, CMP_LEN, C_DHEAD, C_DHEAD), *([cache] * CMP_PAGES))


def _heads_to_rows(q):
    rows = [q[:, h * C_DHEAD:(h + 1) * C_DHEAD] for h in range(C_HEADS)]
    return jnp.concatenate(rows + [jnp.zeros((8 - C_HEADS, C_DHEAD), F32)], axis=0)


SEL_LANES = 384


def _nsa_select_kernel(part_ref, w2_ref, q_ref, ocmp_ref, sel_ref, *, q_pos):
    nchunk = part_ref.shape[1]
    n_cmp = (q_pos + 1 - CMP_LEN) // CMP_STRIDE + 1
    n_sel = -(-(q_pos + 1) // SEL_BLOCK)
    scale = C_DHEAD ** -0.5
    part = part_ref[0]
    kv_cmp = []
    for kv in range(2):
        lo = part[:, (2 * kv) * C_DHEAD:(2 * kv + 1) * C_DHEAD]
        hi = part[:, (2 * kv + 1) * C_DHEAD:(2 * kv + 2) * C_DHEAD]
        hid = lo + pltpu.roll(hi, nchunk - 1, 0)
        kv_cmp.append(jnp.dot(_gelu(hid).astype(BF16), w2_ref[kv].astype(BF16), preferred_element_type=F32).astype(BF16))
    q8 = _heads_to_rows(q_ref[0]).astype(BF16)
    row = lax.broadcasted_iota(jnp.int32, (8, 1), 0)
    slope = jnp.exp2(-2.0 * (row + 1).astype(F32))
    n_i = lax.broadcasted_iota(jnp.int32, (1, nchunk), 1)
    dist_c = q_pos - (n_i * CMP_STRIDE + CMP_LEN - 1)
    ok = (dist_c >= 0) & (n_i < n_cmp)
    s = lax.dot_general(q8, kv_cmp[0], NT_DIMS, preferred_element_type=F32) * scale - slope * dist_c.astype(F32)
    e, den = _masked_softmax(s, ok)
    p = jnp.where(row < C_HEADS, e / jnp.maximum(den, 1e-30), 0.0)
    ocmp_ref[0] = jnp.dot(p.astype(BF16), kv_cmp[1], preferred_element_type=F32)
    psum = jnp.broadcast_to(jnp.sum(p, axis=0, keepdims=True), (8, nchunk))
    c_i = lax.broadcasted_iota(jnp.int32, (nchunk, SEL_LANES), 0)
    j_i = lax.broadcasted_iota(jnp.int32, (nchunk, SEL_LANES), 1)
    lo_ = jnp.maximum(c_i * CMP_STRIDE, j_i * SEL_BLOCK)
    hi_ = jnp.minimum(c_i * CMP_STRIDE + CMP_LEN, (j_i + 1) * SEL_BLOCK)
    ovl = jnp.where((c_i < n_cmp) & (j_i < n_sel), jnp.maximum(hi_ - lo_, 0).astype(F32) * (1.0 / CMP_LEN), 0.0)
    imp = jnp.dot(psum, ovl, preferred_element_type=F32, precision=lax.Precision.HIGHEST)[0:1]
    lane = lax.broadcasted_iota(jnp.int32, (1, SEL_LANES), 1)
    qblk = q_pos // SEL_BLOCK
    forced = (lane == 0) | (lane == qblk) | (lane == qblk - 1)
    score = jnp.where(lane <= qblk, imp + jnp.where(forced, FORCE_SCORE, 0.0), NEG_INF)
    score = jnp.where(lane < n_sel, score, -jnp.inf)
    lane_f = lane.astype(F32)
    out_lane = lax.broadcasted_iota(jnp.int32, (1, LANES), 1)
    sel = jnp.full((1, LANES), -1.0, F32)
    for k in range(min(SEL_TOPK, n_sel)):
        m = jnp.max(score, axis=-1, keepdims=True)
        idx = jnp.min(jnp.where(score == m, lane_f, 1e9), axis=-1, keepdims=True)
        sel = jnp.where(out_lane == k, jnp.where(m > NEG_INF / 2, idx, -1.0), sel)
        score = jnp.where(lane_f == idx, -jnp.inf, score)
    sel_ref[0] = sel.astype(jnp.int32)


def _nsa_select_call(part, w2, q3, q_pos):
    bs, nchunk, _ = part.shape
    return pl.pallas_call(
        functools.partial(_nsa_select_kernel, q_pos=q_pos),
        grid=(bs,),
        in_specs=[
            pl.BlockSpec((1, nchunk, 4 * C_DHEAD), lambda b: (b, 0, 0)),
            pl.BlockSpec((2, C_DHEAD, C_DHEAD), lambda b: (0, 0, 0)),
            pl.BlockSpec((1, 1, C_WIDTH), lambda b: (b, 0, COL_CQ // C_WIDTH)),
        ],
        out_specs=[pl.BlockSpec((1, 8, C_DHEAD), lambda b: (b, 0, 0)), pl.BlockSpec((1, 1, LANES), lambda b: (b, 0, 0))],
        out_shape=[jax.ShapeDtypeStruct((bs, 8, C_DHEAD), F32), jax.ShapeDtypeStruct((bs, 1, LANES), jnp.int32)],
        compiler_params=_cparams(("parallel",)),
        name="nsa_select_sample",
    )(part, w2, q3)


def _nsa_attend_kernel(sel_ref, pt_ref, q_ref, new_ref, wnew_ref, sm_ref, ocmp_ref, win_ref, g_ref, *rest, q_pos):
    k_eff = SEL_TOPK
    blk_refs = [_row_view(r, SEL_BLOCK * NSA_ROWS) for r in rest[:k_eff]]
    o_ref, wout_ref = rest[k_eff:]
    b = pl.program_id(0)
    scale = C_DHEAD ** -0.5
    n_past_blocks = q_pos // SEL_BLOCK
    q8f = _heads_to_rows(q_ref[0])
    q8 = q8f.astype(BF16)
    row = lax.broadcasted_iota(jnp.int32, (8, 1), 0)
    slope = jnp.exp2(-2.0 * (row + 1).astype(F32))
    lane64 = lax.broadcasted_iota(jnp.int32, (1, SEL_BLOCK), 1)

    ss, vs, oks = [], [], []
    new_sel = jnp.zeros((1, 1), jnp.int32)
    for k in range(k_eff):
        j = sel_ref[b, k]
        k_sel = blk_refs[k][pl.ds(2, SEL_BLOCK, stride=NSA_ROWS), :]
        v_sel = blk_refs[k][pl.ds(3, SEL_BLOCK, stride=NSA_ROWS), :]
        s = lax.dot_general(q8, k_sel.astype(BF16), NT_DIMS, preferred_element_type=F32)
        dist = q_pos - (j * SEL_BLOCK + lane64)
        ss.append(s * scale - slope * dist.astype(F32))
        oks.append(lane64 * 0 + jnp.where((j >= 0) & (j < n_past_blocks), 1, 0))
        vs.append(v_sel.astype(BF16))
        new_sel = new_sel + jnp.where(j == n_past_blocks, 1, 0)
    s = jnp.concatenate(ss, axis=1)
    ok = jnp.concatenate(oks, axis=1) > 0
    new = new_ref[0]
    s_n = jnp.sum(q8f * new[:, 2 * C_DHEAD:3 * C_DHEAD], axis=-1, keepdims=True) * scale
    s_n = jnp.where(new_sel > 0, s_n, NEG_INF)
    s = jnp.where(ok, s, NEG_INF)
    m = jnp.maximum(jnp.max(s, axis=-1, keepdims=True), s_n)
    e = jnp.where(ok, jnp.exp(s - m), 0.0)
    e_n = jnp.where(new_sel > 0, jnp.exp(s_n - m), 0.0)
    den = jnp.sum(e, axis=-1, keepdims=True) + e_n
    o_sel = (jnp.dot(e.astype(BF16), jnp.concatenate(vs, axis=0), preferred_element_type=F32)
             + e_n * new[:, 3 * C_DHEAD:]) / den

    lw = win_ref.shape[0]
    win_k = win_ref[:, 0, :]
    win_v = win_ref[:, 1, :]
    wnew = wnew_ref[0]
    wpos = lax.broadcasted_iota(jnp.int32, (1, lw), 1)
    dist_w = lw - wpos
    ok_w = dist_w < WINDOW
    s = lax.dot_general(q8, win_k.astype(BF16), NT_DIMS, preferred_element_type=F32) * scale
    s = jnp.where(ok_w, s - slope * dist_w.astype(F32), NEG_INF)
    s_n = jnp.sum(q8f * wnew[:, :C_DHEAD], axis=-1, keepdims=True) * scale
    m = jnp.maximum(jnp.max(s, axis=-1, keepdims=True), s_n)
    e = jnp.where(ok_w, jnp.exp(s - m), 0.0)
    e_n = jnp.exp(s_n - m)
    den = jnp.sum(e, axis=-1, keepdims=True) + e_n
    o_win = (jnp.dot(e.astype(BF16), win_v.astype(BF16), preferred_element_type=F32)
             + e_n * wnew[:, C_DHEAD:]) / den
    keep = min(WINDOW, lw + 1)
    wout_ref[0:keep - 1, :, :] = win_ref[lw + 1 - keep:lw, :, :]
    wout_ref[keep - 1:keep, 0, :] = wnew[:, :C_DHEAD]
    wout_ref[keep - 1:keep, 1, :] = wnew[:, C_DHEAD:]

    gate = 1.0 / (1.0 + jnp.exp(-sm_ref[0]))
    outs = []
    for h in range(C_HEADS):
        g0 = gate[:, GATE_LANE0 + h:GATE_LANE0 + h + 1]
        g1 = gate[:, GATE_LANE0 + C_HEADS + h:GATE_LANE0 + C_HEADS + h + 1]
        g2 = gate[:, GATE_LANE0 + 2 * C_HEADS + h:GATE_LANE0 + 2 * C_HEADS + h + 1]
        outs.append(g0 * ocmp_ref[0, h:h + 1, :] + g1 * o_sel[h:h + 1, :] + g2 * o_win[h:h + 1, :])
    o = jnp.concatenate(outs, axis=1)
    ms = jnp.mean(o * o, axis=-1, keepdims=True)
    o_ref[0] = o * lax.rsqrt(ms + NORM_EPS) * g_ref[...]


def _nsa_attend_call(sel, page_table, proj3, ocmp, cache, cache_win, layer, out_g, q_pos):
    bs = proj3.shape[0]
    lw = cache_win.shape[2]
    keep = min(WINDOW, lw + 1)
    n_pages = page_table.shape[1]
    halves = PAGE_SIZE // SEL_BLOCK

    def blk_spec(k):
        def imap(b, sel_r, pt_r):
            j = jnp.clip(sel_r[b, k], 0, n_pages * halves - 1)
            return (layer, pt_r[b, j // halves], j % halves, 0, 0)
        return pl.BlockSpec((None, None, SEL_BLOCK, NSA_ROWS, C_DHEAD), imap)

    row3 = lambda w, col: pl.BlockSpec((1, 1, w), lambda b, s_, p_: (b, 0, col))
    grid_spec = pltpu.PrefetchScalarGridSpec(
        num_scalar_prefetch=2,
        grid=(bs,),
        in_specs=[
            row3(C_WIDTH, COL_CQ // C_WIDTH),
            row3(4 * C_DHEAD, COL_CKV // (4 * C_DHEAD)),
            row3(2 * C_DHEAD, COL_WIN // (2 * C_DHEAD)),
            row3(LANES, COL_SMALL // LANES),
            pl.BlockSpec((1, 8, C_DHEAD), lambda b, s_, p_: (b, 0, 0)),
            pl.BlockSpec((None, None, lw, 2, C_DHEAD), lambda b, s_, p_: (layer, b, 0, 0, 0)),
            pl.BlockSpec((1, C_WIDTH), lambda b, s_, p_: (0, 0)),
        ] + [blk_spec(k) for k in range(SEL_TOPK)],
        out_specs=[
            pl.BlockSpec((1, 1, C_WIDTH), lambda b, s_, p_: (b, 0, 0)),
            pl.BlockSpec((None, keep, 2, C_DHEAD), lambda b, s_, p_: (b, 0, 0, 0)),
        ],
    )
    return pl.pallas_call(
        functools.partial(_nsa_attend_kernel, q_pos=q_pos),
        grid_spec=grid_spec,
        out_shape=[jax.ShapeDtypeStruct((bs, 1, C_WIDTH), F32), jax.ShapeDtypeStruct((bs, keep, 2, C_DHEAD), F32)],
        compiler_params=_cparams(("arbitrary",)),
        name="nsa_attend_sample",
    )(sel, page_table, proj3, proj3, proj3, proj3, ocmp, cache_win, out_g.reshape(1, C_WIDTH),
      *([cache] * SEL_TOPK))


def _token_mixer_tail(xa, xb, oa, ob, oc, w_out_b, ffn_g, wq_t, subkeys, u_tab, v_tab, tm, tn, et, emit_bf16=False,
                      layer=0):
    h, hn_t = _outproj_call(xa, xb, oa, ob, oc, w_out_b, ffn_g, tm)
    q_t = _peer_q_call(wq_t, hn_t, tm)
    route = _peer_route_call(q_t, subkeys)
    return h, _peer_expert_call(hn_t, route, u_tab, v_tab, tn, et, emit_bf16, layer)


def kernel(x_prompt, x_sample, cache_diff_kv, cache_nsa_kv, cache_nsa_win, state_ssm, state_conv, page_table,
           norm_mix_g, w_in, w_out, diff_lam, diff_subln_g, ssm_conv_w, ssm_conv_b, ssm_dt_bias, ssm_a_log,
           ssm_d, ssm_norm_g, nsa_pe, nsa_cmp_w1, nsa_cmp_w2, nsa_out_g, norm_ffn_g, peer_wq, peer_subkeys,
           peer_u, peer_v, norm_final_g):
    depth = w_in.shape[0]
    bp, t, _ = x_prompt.shape
    bs = x_sample.shape[0]
    past_len = page_table.shape[1] * PAGE_SIZE
    n_p = bp * t
    n_s = LANES
    tm_p = 512

    xa_p, xb_p = x_prompt.reshape(n_p, D_MODEL), None
    xa_s = jnp.pad(x_sample.reshape(bs, D_MODEL), ((0, n_s - bs), (0, 0)))
    xb_s = None
    st_p = [[] for _ in range(5)]
    st_s = [[] for _ in range(5)]
    state_ssm4 = state_ssm.reshape(depth, bs, B_HEADS * B_HEADDIM, B_STATE)
    w_in_t = jnp.transpose(w_in, (2, 0, 1))
    w_main = _cast_w_main_call(w_in_t)
    w_tail = _pack_w_tail_call(w_in_t)
    for l in range(depth):
        lam_init = 0.8 - 0.6 * math.exp(-0.3 * l)
        w_out_b = w_out[l].astype(BF16)
        wq_t = peer_wq[l].T.astype(BF16)

        proj_s, dkv_s, nkv_s = _proj_call(xa_s, xb_s, norm_mix_g[l], w_main, w_tail, l, n_s)
        proj_s = proj_s[:bs]
        proj_s3 = proj_s.reshape(bs, 1, PROJ_W)
        o_a = _diff_sample_call(page_table, proj_s3, cache_diff_kv, l, diff_lam[l], diff_subln_g[l], lam_init)
        o_b, h_new, conv_new = _ssd_sample_call(proj_s3, state_conv, state_ssm4, l, ssm_conv_w[l], ssm_conv_b[l],
                                                ssm_dt_bias[l], ssm_a_log[l], ssm_d[l], ssm_norm_g[l])
        part = _compress_paged_call(page_table, cache_nsa_kv, l, nsa_pe[l], nsa_cmp_w1[l])
        o_cmp, sel = _nsa_select_call(part, nsa_cmp_w2[l], proj_s3, past_len)
        o_c, win_out = _nsa_attend_call(sel[:, 0, :SEL_TOPK], page_table, proj_s3, o_cmp, cache_nsa_kv, cache_nsa_win, l,
                                        nsa_out_g[l], past_len)
        st_s[0].append(dkv_s[:bs].reshape(bs, 1, 2, A_HEADS, A_DHEAD))
        st_s[1].append(nkv_s[:bs].reshape(bs, 1, NSA_ROWS, C_DHEAD))
        st_s[2].append(win_out)
        st_s[3].append(h_new.reshape(bs, B_HEADS, B_HEADDIM, B_STATE))
        st_s[4].append(conv_new)
        pad = lambda a: jnp.pad(a.reshape(bs, -1), ((0, n_s - bs), (0, 0)))
        xa_s, (xb_s, u_b, vt_b) = _token_mixer_tail(xa_s, xb_s, pad(o_a), pad(o_b), pad(o_c), w_out_b, norm_ffn_g[l],
                                                   wq_t, peer_subkeys[l], peer_u, peer_v, n_s, n_s, 512, True, l)

        proj, dkv_p, nkv_p = _proj_call(xa_p, xb_p, norm_mix_g[l], w_main, w_tail, l, tm_p)
        o_a = _diff_prompt_call(proj, bp, t, diff_lam[l], diff_subln_g[l], lam_init)
        o_b, h_ssm = _ssd_prompt_call(proj, bp, t, ssm_conv_w[l], ssm_conv_b[l], ssm_dt_bias[l], ssm_a_log[l],
                                      ssm_d[l], ssm_norm_g[l])
        kc, vc = _compress_prompt_call(proj, bp, t, nsa_pe[l], nsa_cmp_w1[l], nsa_cmp_w2[l])
        o_c = _nsa_prompt_call(proj, kc, vc, bp, t, nsa_out_g[l])
        proj3 = proj.reshape(bp, t, PROJ_W)
        st_p[0].append(dkv_p.reshape(bp, t, 2, A_HEADS, A_DHEAD))
        st_p[1].append(nkv_p.reshape(bp, t, NSA_ROWS, C_DHEAD))
        keep = min(WINDOW, t)
        st_p[2].append(proj3[:, t - keep:, COL_WIN:COL_WIN + 2 * C_DHEAD].reshape(bp, keep, 2, C_DHEAD))
        st_p[3].append(h_ssm)
        st_p[4].append(proj3[:, t - (CONV_W - 1):, COL_XBC:COL_XBC + B_CONV_DIM])
        xa_p, xb_p = _token_mixer_tail(xa_p, xb_p, o_a, o_b, o_c, w_out_b, norm_ffn_g[l], wq_t, peer_subkeys[l],
                                       u_b, vt_b, tm_p, 512, 1024)

    y_p = _final_norm_call(xa_p, xb_p, norm_final_g, tm_p).reshape(bp, t, D_MODEL)
    y_s = _final_norm_call(xa_s, xb_s, norm_final_g, n_s)[:bs].reshape(bs, 1, D_MODEL)
    return (y_p, y_s) + tuple(jnp.stack(s) for s in st_p) + tuple(jnp.stack(s) for s in st_s)
```

```python
import functools
import math

import jax
import jax.numpy as jnp
from jax import lax
from jax.experimental import pallas as pl
from jax.experimental.pallas import tpu as pltpu

F32 = jnp.float32
BF16 = jnp.bfloat16

D_MODEL = 2048
A_HEADS = 4
A_HALF = 64
A_DHEAD = 128
A_WIDTH = 512
B_WIDTH = 1024
B_HEADDIM = 64
B_HEADS = 16
B_GROUPS = 4
B_STATE = 128
CONV_W = 4
B_CONV_DIM = 2048
SSD_CHUNK = 128
C_HEADS = 4
C_DHEAD = 128
C_WIDTH = 512
CMP_LEN = 32
CMP_STRIDE = 16
SEL_BLOCK = 64
SEL_TOPK = 16
WINDOW = 512
PEER_HEADS = 8
PEER_NKEYS = 128
PEER_TOPK = 16
PEER_DQ = 256
PAGE_SIZE = 128
NSA_ROWS = 4
NORM_EPS = 1e-6
NEG_INF = -1e30
FORCE_SCORE = 1e4

LANES = 128
VMEM_LIMIT = 56 * 1024 * 1024

COL_AQ = 0
COL_AK = 512
COL_AV = 1024
COL_Z = 1536
COL_XBC = 2560
COL_MAIN = 4608
COL_CQ = 4608
COL_CKV = 5120
COL_WIN = 5632
COL_SMALL = 5888
PROJ_W = 6144
GATE_LANE0 = B_HEADS

NT_DIMS = (((1,), (1,)), ((), ()))


def _cparams(sem, vmem=VMEM_LIMIT):
    return pltpu.CompilerParams(dimension_semantics=sem, vmem_limit_bytes=vmem)


def _gelu(x):
    return 0.5 * x * (1.0 + jnp.tanh(math.sqrt(2.0 / math.pi) * (x + 0.044715 * (x * x * x))))


def _silu(x):
    return x * (1.0 / (1.0 + jnp.exp(-x)))


def _softplus(x):
    return jnp.maximum(x, 0.0) + jnp.log(1.0 + jnp.exp(-jnp.abs(x)))


def _row_view(ref, rows):
    return ref.reshape(rows, ref.shape[-1])


def _alibi_slope(h):
    if isinstance(h, int):
        return 2.0 ** (-2.0 * (h + 1))
    return jnp.exp2(jnp.full((1, 1), -2.0, F32) * (h + 1).astype(F32))


def _proj_kernel(*refs, two, tc):
    if two:
        xa_ref, xb_ref, g_ref, wm_ref, wt_ref, o_ref, dkv_ref, nkv_ref, xn_ref = refs
    else:
        xa_ref, g_ref, wm_ref, wt_ref, o_ref, dkv_ref, nkv_ref, xn_ref = refs
    j = pl.program_id(1)
    n_main = COL_MAIN // tc

    @pl.when(j == 0)
    def _():
        x = xa_ref[...]
        if two:
            x = x + xb_ref[...]
        ms = jnp.mean(x * x, axis=-1, keepdims=True)
        xn_ref[...] = (x * lax.rsqrt(ms + NORM_EPS) * g_ref[...]).astype(BF16)

    def store_cache_rows(res, tile):
        for b in range(2 * A_HEADS):
            col = COL_AK + b * LANES
            if col // tc == tile:
                dkv_ref[:, b // A_HEADS, b % A_HEADS, :] = res[:, col % tc:col % tc + LANES]
        for b in range(NSA_ROWS):
            col = COL_CKV + b * LANES
            if col // tc == tile:
                nkv_ref[:, b, :] = res[:, col % tc:col % tc + LANES]

    state_tiles = sorted({(COL_AK + b * LANES) // tc for b in range(2 * A_HEADS)}
                         | {(COL_CKV + b * LANES) // tc for b in range(NSA_ROWS)})

    def finish(res, tiles):
        o_ref[...] = res
        for tile in tiles:
            if tile in state_tiles:
                pl.when(j == tile)(functools.partial(store_cache_rows, res, tile))

    @pl.when(j < n_main)
    def _():
        finish(jnp.dot(xn_ref[...], wm_ref[...], preferred_element_type=F32), range(n_main))

    @pl.when(j >= n_main)
    def _():
        finish(jnp.dot(xn_ref[...], wt_ref[...], preferred_element_type=F32), range(n_main, PROJ_W // tc))


def _cast_t_kernel(w_ref, o_ref):
    for l in range(w_ref.shape[1]):
        o_ref[l] = w_ref[:, l, :].T.astype(BF16)


def _cast_w_main_call(w_in_t, tr=384):
    depth = w_in_t.shape[1]
    return pl.pallas_call(
        _cast_t_kernel,
        grid=(COL_MAIN // tr,),
        in_specs=[pl.BlockSpec((tr, depth, D_MODEL), lambda j: (j, 0, 0))],
        out_specs=pl.BlockSpec((depth, D_MODEL, tr), lambda j: (0, 0, j)),
        out_shape=jax.ShapeDtypeStruct((depth, D_MODEL, COL_MAIN), BF16),
        compiler_params=_cparams(("parallel",)),
        name="w_in_cast",
    )(w_in_t)


def _pack_tail_kernel(w_ref, o_ref, *, n_in):
    width, depth, tk = w_ref.shape
    dt_w, gate_w = B_HEADS, 3 * C_HEADS
    cq0 = dt_w
    ckv0 = cq0 + C_WIDTH
    gate0 = n_in - COL_MAIN - gate_w
    pad = jnp.zeros((PROJ_W - n_in, tk), F32)
    for l in range(depth):
        x = w_ref[:, l, :]
        packed = jnp.concatenate([x[cq0:ckv0], x[ckv0:gate0], x[0:dt_w], x[gate0:gate0 + gate_w], pad], axis=0)
        o_ref[l] = packed.T.astype(BF16)


def _pack_w_tail_call(w_in_t, tk=512):
    width = PROJ_W - COL_MAIN
    depth = w_in_t.shape[1]
    assert COL_MAIN % width == 0
    return pl.pallas_call(
        functools.partial(_pack_tail_kernel, n_in=w_in_t.shape[0]),
        grid=(D_MODEL // tk,),
        in_specs=[pl.BlockSpec((width, depth, tk), lambda k: (COL_MAIN // width, 0, k))],
        out_specs=pl.BlockSpec((depth, tk, width), lambda k: (0, k, 0)),
        out_shape=jax.ShapeDtypeStruct((depth, D_MODEL, width), BF16),
        compiler_params=_cparams(("parallel",)),
        name="w_in_tail_pack",
    )(w_in_t)


def _proj_call(xa, xb, g, w_main, w_tail, layer, tm):
    n = xa.shape[0]
    two = xb is not None
    tc = 1536
    n_main = COL_MAIN // tc
    assert COL_MAIN % tc == 0 and w_tail.shape[2] == PROJ_W - COL_MAIN == tc
    xspec = pl.BlockSpec((tm, D_MODEL), lambda i, j: (i, 0))
    ins = [xa] + ([xb] if two else []) + [g.reshape(1, D_MODEL), w_main, w_tail]
    specs = [xspec] + ([xspec] if two else []) + [
        pl.BlockSpec((1, D_MODEL), lambda i, j: (0, 0)),
        pl.BlockSpec((None, D_MODEL, tc), lambda i, j: (layer, 0, jnp.minimum(j, n_main - 1))),
        pl.BlockSpec((None, D_MODEL, tc), lambda i, j: (layer, 0, jnp.maximum(j - n_main, 0)),
                     pipeline_mode=pl.Buffered(1)),
    ]
    return pl.pallas_call(
        functools.partial(_proj_kernel, two=two, tc=tc),
        grid=(n // tm, PROJ_W // tc),
        in_specs=specs,
        out_specs=[pl.BlockSpec((tm, tc), lambda i, j: (i, j)),
                   pl.BlockSpec((tm, 2, A_HEADS, A_DHEAD), lambda i, j: (i, 0, 0, 0)),
                   pl.BlockSpec((tm, NSA_ROWS, C_DHEAD), lambda i, j: (i, 0, 0))],
        out_shape=[jax.ShapeDtypeStruct((n, PROJ_W), F32), jax.ShapeDtypeStruct((n, 2, A_HEADS, A_DHEAD), F32),
                   jax.ShapeDtypeStruct((n, NSA_ROWS, C_DHEAD), F32)],
        scratch_shapes=[pltpu.VMEM((tm, D_MODEL), BF16)],
        compiler_params=_cparams(("parallel", "arbitrary")),
        name="in_proj",
    )(*ins)


def _diff_lambda(dl, lam_init):
    a = jnp.sum(dl[0:1] * dl[1:2], axis=-1, keepdims=True)
    b = jnp.sum(dl[2:3] * dl[3:4], axis=-1, keepdims=True)
    return jnp.exp(a) - jnp.exp(b) + lam_init


CAUSAL_LEVELS = 4


def _causal_prefixes(i, nq, tq, body):
    levels = min(CAUSAL_LEVELS, nq)
    per = nq // levels
    for lv in range(levels):
        pl.when(i // per == lv)(functools.partial(body, (lv + 1) * per * tq))


def _diff_prompt_kernel(q_ref, k_ref, v_ref, dl_ref, g_ref, o_ref, *, tq, lam_init):
    h = pl.program_id(1)
    i = pl.program_id(2)
    t = k_ref.shape[0]
    scale = A_HALF ** -0.5

    def body(nk):
        lam = _diff_lambda(dl_ref[...], lam_init)
        q = q_ref[...] * scale
        lane = lax.broadcasted_iota(jnp.int32, (1, A_DHEAD), 1)
        kb = k_ref[0:nk, :].astype(BF16)
        vb = v_ref[0:nk, :].astype(BF16)
        qpos = i * tq + lax.broadcasted_iota(jnp.int32, (tq, 1), 0)
        kpos = lax.broadcasted_iota(jnp.int32, (1, nk), 1)
        ok = qpos >= kpos
        key_bias = _alibi_slope(h) * kpos.astype(F32)

        def half_attention(c):
            qc = jnp.where((lane >= c * A_HALF) & (lane < (c + 1) * A_HALF), q, 0.0).astype(BF16)
            s = lax.dot_general(qc, kb, NT_DIMS, preferred_element_type=F32) + key_bias
            s = jnp.where(ok, s, NEG_INF)
            e = jnp.exp(s - jnp.max(s, axis=-1, keepdims=True))
            pv = jnp.dot(e.astype(BF16), vb, preferred_element_type=F32)
            return pv / jnp.sum(e, axis=-1, keepdims=True)

        o = half_attention(0) - lam * half_attention(1)
        ms = jnp.mean(o * o, axis=-1, keepdims=True)
        o_ref[...] = o * lax.rsqrt(ms + NORM_EPS) * g_ref[...] * (1.0 - lam_init)

    _causal_prefixes(i, t // tq, tq, body)


def _diff_prompt_call(proj, bsz, t, dl, subln_g, lam_init, tq=256):
    nq = t // tq
    cq, ck, cv = COL_AQ // A_DHEAD, COL_AK // A_DHEAD, COL_AV // A_DHEAD
    return pl.pallas_call(
        functools.partial(_diff_prompt_kernel, tq=tq, lam_init=lam_init),
        grid=(bsz, A_HEADS, nq),
        in_specs=[
            pl.BlockSpec((tq, A_DHEAD), lambda b, h, i: (b * nq + i, cq + h)),
            pl.BlockSpec((t, A_DHEAD), lambda b, h, i: (b, ck + h)),
            pl.BlockSpec((t, A_DHEAD), lambda b, h, i: (b, cv + h)),
            pl.BlockSpec((4, A_HALF), lambda b, h, i: (0, 0)),
            pl.BlockSpec((1, A_DHEAD), lambda b, h, i: (0, 0)),
        ],
        out_specs=pl.BlockSpec((tq, A_DHEAD), lambda b, h, i: (b * nq + i, h)),
        out_shape=jax.ShapeDtypeStruct((bsz * t, A_WIDTH), F32),
        compiler_params=_cparams(("parallel", "parallel", "arbitrary")),
        name="diff_attn_prompt",
    )(proj, proj, proj, dl, subln_g.reshape(1, A_DHEAD))


SSD_COLS = 512


def _ssd_prompt_kernel(*refs):
    nx, nz = B_CONV_DIM // SSD_COLS, B_WIDTH // SSD_COLS
    xbc_refs, z_refs = refs[:nx], refs[nx:nx + nz]
    sm_ref, cw_ref, cb_ref, dtb_ref, alog_ref, dsk_ref, g_ref, o_ref, hout_ref, buf_ref, h_ref = refs[nx + nz:]
    c = pl.program_id(1)
    cs = SSD_CHUNK

    @pl.when(c == 0)
    def _():
        buf_ref[0:8, :] = jnp.zeros((8, B_CONV_DIM), F32)
        h_ref[...] = jnp.zeros_like(h_ref)

    xbc = jnp.concatenate([r[...] for r in xbc_refs], axis=1)
    buf_ref[8:8 + cs, :] = xbc
    cw = cw_ref[...]
    conv = cb_ref[...] + cw[3:4] * xbc
    for j in range(1, CONV_W):
        conv = conv + cw[3 - j:4 - j] * buf_ref[8 - j:8 - j + cs, :]
    buf_ref[0:8, :] = xbc[cs - 8:cs, :]
    xc = _silu(conv)
    xs = xc[:, :B_WIDTH]

    dt = _softplus(sm_ref[...] + dtb_ref[...])
    a_neg = -jnp.exp(alog_ref[...])
    dta = dt * a_neg
    row = lax.broadcasted_iota(jnp.int32, (cs, cs), 0)
    col = lax.broadcasted_iota(jnp.int32, (cs, cs), 1)
    causal = row >= col
    acum = jnp.dot(causal.astype(F32), dta, preferred_element_type=F32, precision=lax.Precision.HIGHEST)
    acum_t = acum.T
    dt_t = dt.T
    lane = lax.broadcasted_iota(jnp.int32, (1, LANES), 1)
    lo = lane < B_HEADDIM

    ys = []
    for g in range(B_GROUPS):
        bg = xc[:, B_WIDTH + g * B_STATE:B_WIDTH + (g + 1) * B_STATE]
        cg = xc[:, B_WIDTH + B_GROUPS * B_STATE + g * B_STATE:B_WIDTH + B_GROUPS * B_STATE + (g + 1) * B_STATE]
        bgb = bg.astype(BF16)
        cgb = cg.astype(BF16)
        cb = lax.dot_general(cgb, bgb, NT_DIMS, preferred_element_type=F32)
        for pr in range(2):
            h0 = g * 4 + pr * 2
            xpair = xs[:, h0 * B_HEADDIM:(h0 + 2) * B_HEADDIM]
            xpb = xpair.astype(BF16)
            ydiag = []
            ecol = []
            wcol = []
            elast = []
            for hh in (h0, h0 + 1):
                a_col = acum[:, hh:hh + 1]
                a_row = acum_t[hh:hh + 1, :]
                decay = jnp.exp(jnp.where(causal, a_col - a_row, NEG_INF))
                lm = cb * decay * dt_t[hh:hh + 1, :]
                ydiag.append(jnp.dot(lm.astype(BF16), xpb, preferred_element_type=F32))
                a_last = acum[cs - 1:cs, hh:hh + 1]
                ecol.append(jnp.exp(a_col))
                wcol.append(jnp.exp(a_last - a_col) * dt[:, hh:hh + 1])
                elast.append(jnp.exp(a_last))
            hp = h_ref[h0 * B_HEADDIM:(h0 + 2) * B_HEADDIM, :]
            yoff = lax.dot_general(cgb, hp.astype(BF16), NT_DIMS, preferred_element_type=F32)
            y = jnp.where(lo, ydiag[0], ydiag[1]) + yoff * jnp.where(lo, ecol[0], ecol[1])
            ys.append(y)
            wx = xpair * jnp.where(lo, wcol[0], wcol[1])
            upd = jnp.dot(wx.T.astype(BF16), bgb, preferred_element_type=F32)
            prow = lax.broadcasted_iota(jnp.int32, (LANES, 1), 0) < B_HEADDIM
            h_ref[h0 * B_HEADDIM:(h0 + 2) * B_HEADDIM, :] = jnp.where(prow, elast[0], elast[1]) * hp + upd

    y = jnp.concatenate(ys, axis=1)
    y = y + dsk_ref[...] * xs
    y = y * _silu(jnp.concatenate([r[...] for r in z_refs], axis=1))
    ms = jnp.mean(y * y, axis=-1, keepdims=True)
    o_ref[...] = y * lax.rsqrt(ms + NORM_EPS) * g_ref[...]

    @pl.when(c == pl.num_programs(1) - 1)
    def _():
        hout_ref[0] = h_ref[...]


def _pad_lanes(v, fill=0.0):
    v = v.reshape(1, -1).astype(F32)
    return jnp.pad(v, ((0, 0), (0, LANES - v.shape[1])), constant_values=fill)


def _ssd_prompt_call(proj, bsz, t, conv_w, conv_b, dt_bias, a_log, d_skip, norm_g):
    nc = t // SSD_CHUNK
    cs = SSD_CHUNK
    const = lambda b, c: (0, 0)
    o, hout = pl.pallas_call(
        _ssd_prompt_kernel,
        grid=(bsz, nc),
        in_specs=[
            *[pl.BlockSpec((cs, SSD_COLS), functools.partial(lambda k, b, c: (b * nc + c, COL_XBC // SSD_COLS + k), k))
              for k in range(B_CONV_DIM // SSD_COLS)],
            *[pl.BlockSpec((cs, SSD_COLS), functools.partial(lambda k, b, c: (b * nc + c, COL_Z // SSD_COLS + k), k))
              for k in range(B_WIDTH // SSD_COLS)],
            pl.BlockSpec((cs, LANES), lambda b, c: (b * nc + c, COL_SMALL // LANES)),
            pl.BlockSpec((CONV_W, B_CONV_DIM), const),
            pl.BlockSpec((1, B_CONV_DIM), const),
            pl.BlockSpec((1, LANES), const),
            pl.BlockSpec((1, LANES), const),
            pl.BlockSpec((1, B_WIDTH), const),
            pl.BlockSpec((1, B_WIDTH), const),
        ],
        out_specs=[
            pl.BlockSpec((cs, B_WIDTH), lambda b, c: (b * nc + c, 0)),
            pl.BlockSpec((1, B_HEADS * B_HEADDIM, B_STATE), lambda b, c: (b, 0, 0)),
        ],
        out_shape=[
            jax.ShapeDtypeStruct((bsz * t, B_WIDTH), F32),
            jax.ShapeDtypeStruct((bsz, B_HEADS * B_HEADDIM, B_STATE), F32),
        ],
        scratch_shapes=[pltpu.VMEM((8 + cs, B_CONV_DIM), F32), pltpu.VMEM((B_HEADS * B_HEADDIM, B_STATE), F32)],
        compiler_params=_cparams(("parallel", "arbitrary")),
        name="ssd_prompt",
    )(*([proj] * (B_CONV_DIM // SSD_COLS + B_WIDTH // SSD_COLS + 1)), conv_w, conv_b.reshape(1, -1),
      _pad_lanes(dt_bias), _pad_lanes(a_log), jnp.repeat(d_skip, B_HEADDIM).reshape(1, B_WIDTH),
      norm_g.reshape(1, B_WIDTH))
    return o, hout.reshape(bsz, B_HEADS, B_HEADDIM, B_STATE)


def _compress_partials(load_rows, pe_ref, w1_ref, kv, nchunk):
    acc = jnp.zeros((nchunk + 8, 2 * C_DHEAD), F32)
    tail = jnp.zeros((6, C_DHEAD), F32)
    for r in range(CMP_STRIDE):
        x = jnp.concatenate([load_rows(r), pe_ref[kv, r:r + 1, :], pe_ref[kv, CMP_STRIDE + r:CMP_STRIDE + r + 1, :], tail],
                            axis=0).astype(BF16)
        w = jnp.concatenate([w1_ref[kv, r], w1_ref[kv, CMP_STRIDE + r]], axis=1).astype(BF16)
        acc = acc + jnp.dot(x, w, preferred_element_type=F32)
    lo = acc[0:nchunk, :C_DHEAD] + acc[nchunk:nchunk + 1, :C_DHEAD]
    hi = acc[0:nchunk, C_DHEAD:] + acc[nchunk + 1:nchunk + 2, C_DHEAD:]
    return lo, hi


def _compress_kernel(k_ref, v_ref, pe_ref, w1_ref, w2_ref, kc_ref, vc_ref, *, nchunk):
    outs = []
    for kv, rows_ref in enumerate((k_ref, v_ref)):
        acc_lo, acc_hi = _compress_partials(lambda r: rows_ref[pl.ds(r, nchunk, stride=CMP_STRIDE), :],
                                            pe_ref, w1_ref, kv, nchunk)
        hid = acc_lo + pltpu.roll(acc_hi, nchunk - 1, 0)
        outs.append(jnp.dot(_gelu(hid).astype(BF16), w2_ref[kv].astype(BF16), preferred_element_type=F32))
    kc_ref[0] = outs[0]
    vc_ref[0] = outs[1]


def _compress_prompt_call(proj, bsz, t, pe, w1, w2):
    nchunk = t // CMP_STRIDE
    shp = jax.ShapeDtypeStruct((bsz, nchunk, C_DHEAD), F32)
    return pl.pallas_call(
        functools.partial(_compress_kernel, nchunk=nchunk),
        grid=(bsz,),
        in_specs=[
            pl.BlockSpec((t, C_DHEAD), lambda b: (b, COL_CKV // C_DHEAD)),
            pl.BlockSpec((t, C_DHEAD), lambda b: (b, COL_CKV // C_DHEAD + 1)),
            pl.BlockSpec((2, CMP_LEN, C_DHEAD), lambda b: (0, 0, 0)),
            pl.BlockSpec((2, CMP_LEN, C_DHEAD, C_DHEAD), lambda b: (0, 0, 0, 0)),
            pl.BlockSpec((2, C_DHEAD, C_DHEAD), lambda b: (0, 0, 0)),
        ],
        out_specs=[pl.BlockSpec((1, nchunk, C_DHEAD), lambda b: (b, 0, 0))] * 2,
        out_shape=[shp, shp],
        compiler_params=_cparams(("parallel",)),
        name="nsa_compress_prompt",
    )(proj, proj, pe, w1.reshape(2, CMP_LEN, C_DHEAD, C_DHEAD), w2)


def _masked_softmax(s, ok):
    s = jnp.where(ok, s, NEG_INF)
    m = jnp.max(s, axis=-1, keepdims=True)
    e = jnp.where(ok, jnp.exp(s - m), 0.0)
    return e, jnp.sum(e, axis=-1, keepdims=True)


def _topk_mask_lanes(score, k, n):
    lane = lax.broadcasted_iota(jnp.int32, (1, LANES), 1)
    rank = jnp.zeros(score.shape, F32)
    for i in range(n):
        ci = score[:, i:i + 1]
        beats = (ci > score) | ((ci == score) & (lane > i))
        rank = rank + jnp.where(beats, 1.0, 0.0)
    return (rank < k) & (lane < n)


def _nsa_prompt_kernel(q_ref, ks_ref, vs_ref, kw_ref, vw_ref, kc_ref, vc_ref, sm_ref, g_ref, o_ref, osel_ref, *, tq, n_cmp):
    i = pl.program_id(1)
    t = ks_ref.shape[0]
    n_sel = t // SEL_BLOCK
    scale = C_DHEAD ** -0.5
    sel_shift = SEL_BLOCK.bit_length() - 1
    wlen = min(t, WINDOW + tq)

    qpos = i * tq + lax.broadcasted_iota(jnp.int32, (tq, 1), 0)
    lane = lax.broadcasted_iota(jnp.int32, (1, LANES), 1)

    cmp_end = lane * CMP_STRIDE + (CMP_LEN - 1)
    dist_c = qpos - cmp_end
    ok_c = (dist_c >= 0) & (lane < n_cmp)
    dist_cf = dist_c.astype(F32)
    kcb = kc_ref[0].astype(BF16)
    vcb = vc_ref[0].astype(BF16)
    qs = [(q_ref[:, h * C_DHEAD:(h + 1) * C_DHEAD] * scale).astype(BF16) for h in range(C_HEADS)]
    o_cmp = []
    psum = jnp.zeros((tq, LANES), F32)
    for h in range(C_HEADS):
        s = lax.dot_general(qs[h], kcb, NT_DIMS, preferred_element_type=F32)
        s = s - _alibi_slope(h) * dist_cf
        e, den = _masked_softmax(s, ok_c)
        p = e / jnp.maximum(den, 1e-30)
        psum = psum + p
        o_cmp.append(jnp.dot(p.astype(BF16), vcb, preferred_element_type=F32))

    n_i = lax.broadcasted_iota(jnp.int32, (LANES, LANES), 0)
    j_i = lax.broadcasted_iota(jnp.int32, (LANES, LANES), 1)
    lo_ = jnp.maximum(n_i * CMP_STRIDE, j_i * SEL_BLOCK)
    hi_ = jnp.minimum(n_i * CMP_STRIDE + CMP_LEN, (j_i + 1) * SEL_BLOCK)
    ovl = jnp.maximum(hi_ - lo_, 0).astype(F32) * (1.0 / CMP_LEN)
    ovl = jnp.where((n_i < n_cmp) & (j_i < n_sel), ovl, 0.0)
    imp = jnp.dot(psum, ovl, preferred_element_type=F32, precision=lax.Precision.HIGHEST)
    qblk = qpos >> sel_shift
    sel_valid = lane <= qblk
    forced = (lane == 0) | (lane == qblk) | (lane == qblk - 1)
    score = jnp.where(sel_valid, imp + jnp.where(forced, FORCE_SCORE, 0.0), NEG_INF)
    score = jnp.where(lane < n_sel, score, -jnp.inf)
    chosen = _topk_mask_lanes(score, min(SEL_TOPK, n_sel), n_sel) & sel_valid
    chosen_b = jnp.where(chosen, 1.0, 0.0).astype(BF16)

    def attend(qh, kb, vb, key_bias, ok):
        s = lax.dot_general(qh, kb, NT_DIMS, preferred_element_type=F32) + key_bias
        s = jnp.where(ok, s, NEG_INF)
        e = jnp.exp(s - jnp.max(s, axis=-1, keepdims=True))
        return jnp.dot(e.astype(BF16), vb, preferred_element_type=F32) / jnp.sum(e, axis=-1, keepdims=True)

    def selected_branch(nk):
        e_j = lax.broadcasted_iota(jnp.int32, (LANES, nk), 0)
        e_k = lax.broadcasted_iota(jnp.int32, (LANES, nk), 1)
        expand = jnp.where((e_k >> sel_shift) == e_j, 1.0, 0.0).astype(BF16)
        key_sel = jnp.dot(chosen_b, expand, preferred_element_type=F32) > 0.5
        kpos = lax.broadcasted_iota(jnp.int32, (1, nk), 1)
        kpos_f = kpos.astype(F32)
        ok_s = key_sel & (qpos >= kpos)
        ksb = ks_ref[0:nk, :].astype(BF16)
        vsb = vs_ref[0:nk, :].astype(BF16)
        for h in range(C_HEADS):
            osel_ref[:, h * C_DHEAD:(h + 1) * C_DHEAD] = attend(qs[h], ksb, vsb, _alibi_slope(h) * kpos_f, ok_s)

    nq = t // tq
    if nq % 2 == 0:
        pl.when(i < nq // 2)(functools.partial(selected_branch, t // 2))
        pl.when(i >= nq // 2)(functools.partial(selected_branch, t))
    else:
        selected_branch(t)

    w0 = pl.multiple_of(jnp.clip(i * tq - WINDOW, 0, t - wlen), 8)
    wpos = w0 + lax.broadcasted_iota(jnp.int32, (1, wlen), 1)
    wpos_f = wpos.astype(F32)
    dist_w = qpos - wpos
    ok_w = (dist_w >= 0) & (dist_w < WINDOW)
    kwb = kw_ref[pl.ds(w0, wlen), :].astype(BF16)
    vwb = vw_ref[pl.ds(w0, wlen), :].astype(BF16)
    gate = 1.0 / (1.0 + jnp.exp(-sm_ref[...]))
    outs = []
    for h in range(C_HEADS):
        o_sel = osel_ref[:, h * C_DHEAD:(h + 1) * C_DHEAD]
        o_win = attend(qs[h], kwb, vwb, _alibi_slope(h) * wpos_f, ok_w)
        g0 = gate[:, GATE_LANE0 + h:GATE_LANE0 + h + 1]
        g1 = gate[:, GATE_LANE0 + C_HEADS + h:GATE_LANE0 + C_HEADS + h + 1]
        g2 = gate[:, GATE_LANE0 + 2 * C_HEADS + h:GATE_LANE0 + 2 * C_HEADS + h + 1]
        outs.append(g0 * o_cmp[h] + g1 * o_sel + g2 * o_win)
    o = jnp.concatenate(outs, axis=1)
    ms = jnp.mean(o * o, axis=-1, keepdims=True)
    o_ref[...] = o * lax.rsqrt(ms + NORM_EPS) * g_ref[...]


def _nsa_prompt_call(proj, kc, vc, bsz, t, out_g, tq=256):
    nq = t // tq
    n_cmp = (t - CMP_LEN) // CMP_STRIDE + 1
    c0 = COL_CKV // C_DHEAD
    w0 = COL_WIN // C_DHEAD
    kvspec = lambda col: pl.BlockSpec((t, C_DHEAD), lambda b, i: (b, col))
    return pl.pallas_call(
        functools.partial(_nsa_prompt_kernel, tq=tq, n_cmp=n_cmp),
        grid=(bsz, nq),
        in_specs=[
            pl.BlockSpec((tq, C_WIDTH), lambda b, i: (b * nq + i, COL_CQ // C_WIDTH)),
            kvspec(c0 + 2), kvspec(c0 + 3), kvspec(w0), kvspec(w0 + 1),
            pl.BlockSpec((1, kc.shape[1], C_DHEAD), lambda b, i: (b, 0, 0)),
            pl.BlockSpec((1, kc.shape[1], C_DHEAD), lambda b, i: (b, 0, 0)),
            pl.BlockSpec((tq, LANES), lambda b, i: (b * nq + i, COL_SMALL // LANES)),
            pl.BlockSpec((1, C_WIDTH), lambda b, i: (0, 0)),
        ],
        out_specs=pl.BlockSpec((tq, C_WIDTH), lambda b, i: (b * nq + i, 0)),
        out_shape=jax.ShapeDtypeStruct((bsz * t, C_WIDTH), F32),
        scratch_shapes=[pltpu.VMEM((tq, C_WIDTH), F32)],
        compiler_params=_cparams(("parallel", "arbitrary")),
        name="nsa_attn_prompt",
    )(proj, proj, proj, proj, proj, kc, vc, proj, out_g.reshape(1, C_WIDTH))


def _outproj_kernel(*refs, two):
    if two:
        xa_ref, xb_ref, oa_ref, ob_ref, oc_ref, w_ref, g_ref, h_ref, hnt_ref = refs
    else:
        xa_ref, oa_ref, ob_ref, oc_ref, w_ref, g_ref, h_ref, hnt_ref = refs
    x = xa_ref[...]
    if two:
        x = x + xb_ref[...]
    mixed = jnp.dot(oa_ref[...].astype(BF16), w_ref[0:A_WIDTH, :], preferred_element_type=F32)
    mixed = mixed + jnp.dot(ob_ref[...].astype(BF16), w_ref[A_WIDTH:A_WIDTH + B_WIDTH, :], preferred_element_type=F32)
    mixed = mixed + jnp.dot(oc_ref[...].astype(BF16), w_ref[A_WIDTH + B_WIDTH:, :], preferred_element_type=F32)
    h = x + mixed
    h_ref[...] = h
    ms = jnp.mean(h * h, axis=-1, keepdims=True)
    hnt_ref[...] = (h * lax.rsqrt(ms + NORM_EPS) * g_ref[...]).T.astype(BF16)


def _outproj_call(xa, xb, oa, ob, oc, w_out_b, g, tm):
    n = xa.shape[0]
    two = xb is not None
    row = lambda w: pl.BlockSpec((tm, w), lambda i: (i, 0))
    ins = [xa] + ([xb] if two else []) + [oa, ob, oc, w_out_b, g.reshape(1, D_MODEL)]
    specs = [row(D_MODEL)] + ([row(D_MODEL)] if two else []) + [
        row(A_WIDTH), row(B_WIDTH), row(C_WIDTH),
        pl.BlockSpec((D_MODEL, D_MODEL), lambda i: (0, 0)),
        pl.BlockSpec((1, D_MODEL), lambda i: (0, 0)),
    ]
    return pl.pallas_call(
        functools.partial(_outproj_kernel, two=two),
        grid=(n // tm,),
        in_specs=specs,
        out_specs=[row(D_MODEL), pl.BlockSpec((D_MODEL, tm), lambda i: (0, i))],
        out_shape=[jax.ShapeDtypeStruct((n, D_MODEL), F32), jax.ShapeDtypeStruct((D_MODEL, n), BF16)],
        compiler_params=_cparams(("parallel",)),
        name="out_proj",
    )(*ins)


def _peer_q_kernel(wqt_ref, hnt_ref, qt_ref):
    qt_ref[...] = jnp.dot(wqt_ref[...], hnt_ref[...], preferred_element_type=F32)


def _peer_q_call(wq_t, hn_t, tm):
    n = hn_t.shape[1]
    dq = wq_t.shape[0]
    return pl.pallas_call(
        _peer_q_kernel,
        grid=(n // tm,),
        in_specs=[pl.BlockSpec((dq, D_MODEL), lambda i: (0, 0)), pl.BlockSpec((D_MODEL, tm), lambda i: (0, i))],
        out_specs=pl.BlockSpec((dq, tm), lambda i: (0, i)),
        out_shape=jax.ShapeDtypeStruct((dq, n), F32),
        compiler_params=_cparams(("parallel",)),
        name="peer_query",
    )(wq_t, hn_t)


NOT_RANKED = 99.0


def _top_rows(s, pos, k, want_rank):
    rank = jnp.full(s.shape, NOT_RANKED, F32) if want_rank else None
    vals, picks = [], []
    for j in range(k):
        m = jnp.max(s, axis=0, keepdims=True)
        idx = jnp.min(jnp.where(s == m, pos, 1e9), axis=0, keepdims=True)
        hit = pos == idx
        if want_rank:
            rank = jnp.where(hit, float(j), rank)
        s = jnp.where(hit, -jnp.inf, s)
        vals.append(m)
        picks.append(idx)
    return jnp.concatenate(vals, axis=0), jnp.concatenate(picks, axis=0), rank


PAIR_ROWS = PEER_TOPK + 7 * 8 + 8


def _pair_candidates(v1, v2):
    tn = v1.shape[1]
    parts = [v1[0:1, :] + v2] + [v1[a:a + 1, :] + v2[0:8, :] for a in range(1, 8)] + [v1[8:16, :] + v2[0:1, :]]
    r = lax.broadcasted_iota(jnp.int32, (PAIR_ROWS, tn), 0)
    mid = r - PEER_TOPK
    pos = jnp.where(r < PEER_TOPK, r,
                    jnp.where(r < PEER_TOPK + 56, ((mid >> 3) + 1) * PEER_TOPK + (mid & 7), (r - 64) * PEER_TOPK))
    return jnp.concatenate(parts, axis=0), pos.astype(F32)


def _peer_route_kernel(qt_ref, sk_ref, lim_ref, coef_ref, rank2_ref, e2_ref, *, heads):
    half = PEER_DQ // 2
    row = lax.broadcasted_iota(jnp.int32, (PEER_NKEYS, LANES), 0).astype(F32)
    for hh, tb in [(hh, tb) for hh in range(heads) for tb in range(qt_ref.shape[1] // LANES)]:
        cols = slice(tb * LANES, (tb + 1) * LANES)
        q = qt_ref[hh * PEER_DQ:(hh + 1) * PEER_DQ, cols]
        s1 = jnp.dot(sk_ref[hh, 0], q[0:half, :], preferred_element_type=F32, precision=lax.Precision.HIGHEST)
        s2 = jnp.dot(sk_ref[hh, 1], q[half:, :], preferred_element_type=F32, precision=lax.Precision.HIGHEST)
        v1, _, rank1 = _top_rows(s1, row, PEER_TOPK, True)
        v2, _, rank2 = _top_rows(s2, row, PEER_TOPK, True)
        cand, cpos = _pair_candidates(v1, v2)
        top, pos, _ = _top_rows(cand, cpos, PEER_TOPK, False)
        z = jnp.sum(jnp.exp(top - top[0:1, :]), axis=0, keepdims=True)
        a_of = jnp.floor(pos * (1.0 / PEER_TOPK))
        lim = jnp.zeros(s1.shape, F32)
        for a in range(PEER_TOPK):
            cnt = jnp.sum(jnp.where(a_of == float(a), 1.0, 0.0), axis=0, keepdims=True)
            lim = jnp.where(rank1 == float(a), cnt, lim)
        lim_ref[hh, :, cols] = lim
        coef_ref[hh, :, cols] = jnp.exp(s1 - v1[0:1, :]) / z
        rank2_ref[hh, :, cols] = rank2.astype(BF16)
        e2_ref[hh, :, cols] = jnp.exp(s2 - v2[0:1, :]).astype(BF16)


def _peer_route_call(q_t, subkeys, tn=2 * LANES, heads=2):
    n = q_t.shape[1]
    tn = min(tn, n)
    shp = lambda dt: jax.ShapeDtypeStruct((PEER_HEADS, PEER_NKEYS, n), dt)
    ospec = pl.BlockSpec((heads, PEER_NKEYS, tn), lambda j, h: (h, 0, j))
    return pl.pallas_call(
        functools.partial(_peer_route_kernel, heads=heads),
        grid=(n // tn, PEER_HEADS // heads),
        in_specs=[
            pl.BlockSpec((heads * PEER_DQ, tn), lambda j, h: (h, j)),
            pl.BlockSpec((heads, 2, PEER_NKEYS, PEER_DQ // 2), lambda j, h: (h, 0, 0, 0)),
        ],
        out_specs=[ospec] * 4,
        out_shape=[shp(F32), shp(F32), shp(BF16), shp(BF16)],
        compiler_params=_cparams(("parallel", "arbitrary")),
        name="peer_route",
    )(q_t, subkeys)


def _peer_expert_kernel(hnt_ref, lim_ref, coef_ref, rank2_ref, e2_ref, u_ref, v_ref, o_ref, *rest, et, emit):
    acc_ref = rest[-1]
    t = pl.program_id(1)

    @pl.when(t == 0)
    def _():
        acc_ref[...] = jnp.zeros_like(acc_ref)

    if emit:
        ub = u_ref[...].astype(BF16)
        vtb = v_ref[...].T.astype(BF16)
        rest[0][...] = ub
        rest[1][...] = vtb
    else:
        ub = u_ref[...]
        vtb = v_ref[...]
    tn = hnt_ref.shape[1]
    hid = jnp.dot(ub, hnt_ref[...], preferred_element_type=F32)
    acts = []
    for ii in range(et // PEER_NKEYS):
        i1 = t * (et // PEER_NKEYS) + ii
        gate = jnp.zeros((PEER_NKEYS, tn), BF16)
        for h in range(PEER_HEADS):
            lim = lim_ref[h, pl.ds(i1, 1), :].astype(BF16)
            coef = coef_ref[h, pl.ds(i1, 1), :].astype(BF16)
            gate = gate + jnp.where(rank2_ref[h] < lim, e2_ref[h], jnp.zeros((), BF16)) * coef
        acts.append(gate * _gelu(hid[ii * PEER_NKEYS:(ii + 1) * PEER_NKEYS, :]).astype(BF16))
    acc_ref[...] += jnp.dot(vtb, jnp.concatenate(acts, axis=0), preferred_element_type=F32)

    @pl.when(t == pl.num_programs(1) - 1)
    def _():
        o_ref[...] = acc_ref[...].T


def _peer_expert_call(hn_t, route, u_tab, v_tab, tn, et, emit=False, layer=0):
    n = hn_t.shape[1]
    n_exp = u_tab.shape[-2] if emit else u_tab.shape[0]
    rspec = pl.BlockSpec((PEER_HEADS, PEER_NKEYS, tn), lambda j, t: (0, 0, j))
    uspec = pl.BlockSpec((et, D_MODEL), lambda j, t: (t, 0))
    vtspec = pl.BlockSpec((D_MODEL, et), lambda j, t: (0, t))
    ospec = pl.BlockSpec((tn, D_MODEL), lambda j, t: (j, 0))
    oshape = jax.ShapeDtypeStruct((n, D_MODEL), F32)
    if emit:
        assert n == tn
        out_specs = [ospec, uspec, vtspec]
        out_shape = [oshape, jax.ShapeDtypeStruct((n_exp, D_MODEL), BF16), jax.ShapeDtypeStruct((D_MODEL, n_exp), BF16)]
    else:
        out_specs, out_shape = ospec, oshape
    return pl.pallas_call(
        functools.partial(_peer_expert_kernel, et=et, emit=emit),
        grid=(n // tn, n_exp // et),
        in_specs=[pl.BlockSpec((D_MODEL, tn), lambda j, t: (0, j)), rspec, rspec, rspec, rspec]
        + ([pl.BlockSpec((None, et, D_MODEL), lambda j, t: (layer, t, 0))] * 2 if emit else [uspec, vtspec]),
        out_specs=out_specs,
        out_shape=out_shape,
        scratch_shapes=[pltpu.VMEM((D_MODEL, tn), F32)],
        compiler_params=_cparams(("parallel", "arbitrary")),
        name="peer_experts",
    )(hn_t, *route, u_tab, v_tab)


def _final_norm_kernel(xa_ref, xb_ref, g_ref, o_ref):
    x = xa_ref[...] + xb_ref[...]
    ms = jnp.mean(x * x, axis=-1, keepdims=True)
    o_ref[...] = x * lax.rsqrt(ms + NORM_EPS) * g_ref[...]


def _final_norm_call(xa, xb, g, tm):
    n = xa.shape[0]
    row = pl.BlockSpec((tm, D_MODEL), lambda i: (i, 0))
    return pl.pallas_call(
        _final_norm_kernel,
        grid=(n // tm,),
        in_specs=[row, row, pl.BlockSpec((1, D_MODEL), lambda i: (0, 0))],
        out_specs=row,
        out_shape=jax.ShapeDtypeStruct((n, D_MODEL), F32),
        compiler_params=_cparams(("parallel",)),
        name="final_norm",
    )(xa, xb, g.reshape(1, D_MODEL))


DIFF_PAGES = 32


def _diff_sample_kernel(pt_ref, q_ref, knew_ref, vnew_ref, dl_ref, g_ref, *rest, past_len, lam_init):
    page_refs = [_row_view(r, PAGE_SIZE * 2 * A_HEADS) for r in rest[:DIFF_PAGES]]
    o_ref, m_ref, l_ref, acc_ref = rest[DIFF_PAGES:]
    p = pl.program_id(1)
    scale = A_HALF ** -0.5
    nrow = 2 * A_HEADS
    per_key = 2 * A_HEADS

    @pl.when(p == 0)
    def _():
        m_ref[...] = jnp.full(m_ref.shape, NEG_INF, F32)
        l_ref[...] = jnp.zeros(l_ref.shape, F32)
        acc_ref[...] = jnp.zeros(acc_ref.shape, F32)

    row = lax.broadcasted_iota(jnp.int32, (nrow, 1), 0)
    lane = lax.broadcasted_iota(jnp.int32, (1, A_DHEAD), 1)
    slope = jnp.exp2(-2.0 * ((row >> 1) + 1).astype(F32))
    q = q_ref[0]
    q2 = [jnp.where(((row >> 1) == h) & ((lane >= A_HALF) == ((row & 1) == 1)), q[:, h * A_DHEAD:(h + 1) * A_DHEAD], 0.0)
          for h in range(A_HEADS)]
    q2b = [x.astype(BF16) for x in q2]
    ss = []
    for g in range(DIFF_PAGES):
        sg = None
        for h in range(A_HEADS):
            kh = page_refs[g][pl.ds(h, PAGE_SIZE, stride=per_key), :].astype(BF16)
            d = lax.dot_general(q2b[h], kh, NT_DIMS, preferred_element_type=F32)
            sg = d if sg is None else sg + d
        ss.append(sg)
    s = jnp.concatenate(ss, axis=1) * scale
    nk = DIFF_PAGES * PAGE_SIZE
    kpos = p * nk + lax.broadcasted_iota(jnp.int32, (1, nk), 1)
    s = s - slope * (past_len - kpos).astype(F32)
    m_old = m_ref[:, 0:1]
    m_new = jnp.maximum(m_old, jnp.max(s, axis=-1, keepdims=True))
    alpha = jnp.exp(m_old - m_new)
    e = jnp.exp(s - m_new)
    eb = e.astype(BF16)
    l_new = alpha * l_ref[:, 0:1] + jnp.sum(e, axis=-1, keepdims=True)
    pv = jnp.zeros((nrow, A_DHEAD), F32)
    for h in range(A_HEADS):
        vh = jnp.concatenate([page_refs[g][pl.ds(A_HEADS + h, PAGE_SIZE, stride=per_key), :].astype(BF16)
                              for g in range(DIFF_PAGES)], axis=0)
        pv = pv + jnp.where((row >> 1) == h, jnp.dot(eb, vh, preferred_element_type=F32), 0.0)
    acc = alpha * acc_ref[...] + pv
    m_ref[...] = jnp.broadcast_to(m_new, m_ref.shape)
    l_ref[...] = jnp.broadcast_to(l_new, l_ref.shape)
    acc_ref[...] = acc

    @pl.when(p == pl.num_programs(1) - 1)
    def _():
        knew = knew_ref[0]
        vnew = vnew_ref[0]
        s_n = jnp.zeros((nrow, 1), F32)
        for h in range(A_HEADS):
            s_n = s_n + jnp.sum(q2[h] * knew[:, h * A_DHEAD:(h + 1) * A_DHEAD], axis=-1, keepdims=True)
        s_n = s_n * scale
        v8 = jnp.concatenate([vnew[:, (r // 2) * A_DHEAD:(r // 2 + 1) * A_DHEAD] for r in range(nrow)], axis=0)
        m_f = jnp.maximum(m_new, s_n)
        a_f = jnp.exp(m_new - m_f)
        e_n = jnp.exp(s_n - m_f)
        o8 = (a_f * acc + e_n * v8) / (a_f * l_new + e_n)
        lam = _diff_lambda(dl_ref[...], lam_init)
        outs = []
        for h in range(A_HEADS):
            oh = o8[2 * h:2 * h + 1, :] - lam * o8[2 * h + 1:2 * h + 2, :]
            ms = jnp.mean(oh * oh, axis=-1, keepdims=True)
            outs.append(oh * lax.rsqrt(ms + NORM_EPS) * g_ref[...] * (1.0 - lam_init))
        o_ref[0] = jnp.concatenate(outs, axis=1)


def _diff_sample_call(page_table, proj3, cache, layer, dl, subln_g, lam_init):
    bs, n_pages = page_table.shape
    past_len = n_pages * PAGE_SIZE
    steps = n_pages // DIFF_PAGES

    def page_spec(g):
        return pl.BlockSpec((None, None, PAGE_SIZE, 2, A_HEADS, A_DHEAD),
                            lambda b, p, pt: (layer, pt[b, p * DIFF_PAGES + g], 0, 0, 0, 0))

    grid_spec = pltpu.PrefetchScalarGridSpec(
        num_scalar_prefetch=1,
        grid=(bs, steps),
        in_specs=[
            pl.BlockSpec((1, 1, A_WIDTH), lambda b, p, pt: (b, 0, COL_AQ // A_WIDTH)),
            pl.BlockSpec((1, 1, A_WIDTH), lambda b, p, pt: (b, 0, COL_AK // A_WIDTH)),
            pl.BlockSpec((1, 1, A_WIDTH), lambda b, p, pt: (b, 0, COL_AV // A_WIDTH)),
            pl.BlockSpec((4, A_HALF), lambda b, p, pt: (0, 0)),
            pl.BlockSpec((1, A_DHEAD), lambda b, p, pt: (0, 0)),
        ] + [page_spec(g) for g in range(DIFF_PAGES)],
        out_specs=pl.BlockSpec((1, 1, A_WIDTH), lambda b, p, pt: (b, 0, 0)),
        scratch_shapes=[pltpu.VMEM((2 * A_HEADS, LANES), F32), pltpu.VMEM((2 * A_HEADS, LANES), F32),
                        pltpu.VMEM((2 * A_HEADS, A_DHEAD), F32)],
    )
    return pl.pallas_call(
        functools.partial(_diff_sample_kernel, past_len=past_len, lam_init=lam_init),
        grid_spec=grid_spec,
        out_shape=jax.ShapeDtypeStruct((bs, 1, A_WIDTH), F32),
        compiler_params=_cparams(("parallel", "arbitrary")),
        name="diff_attn_sample",
    )(page_table, proj3, proj3, proj3, dl, subln_g.reshape(1, A_DHEAD), *([cache] * DIFF_PAGES))


def _diag_rows(vec):
    n = vec.shape[1]
    r = lax.broadcasted_iota(jnp.int32, (n, n), 0)
    c = lax.broadcasted_iota(jnp.int32, (n, n), 1)
    return jnp.where(r == c, vec, 0.0)


def _ssd_sample_kernel(*refs):
    nx, nz = B_CONV_DIM // SSD_COLS, B_WIDTH // SSD_COLS
    xbc_refs, z_refs = refs[:nx], refs[nx:nx + nz]
    (sm_ref, cbuf_ref, h0_ref, cw_ref, cb_ref, dtb_ref, alog_ref, dsk_ref, g_ref,
     o_ref, hout_ref, cout_ref) = refs[nx + nz:]
    hi = lax.Precision.HIGHEST
    new = jnp.concatenate([r[0] for r in xbc_refs], axis=1)
    buf = cbuf_ref[0, 0]
    cw = cw_ref[...]
    conv = cb_ref[...] + cw[CONV_W - 1:CONV_W] * new
    for i in range(CONV_W - 1):
        conv = conv + cw[i:i + 1] * buf[i:i + 1]
    cout_ref[0] = jnp.concatenate([buf[1:CONV_W - 1], new], axis=0)
    xc = _silu(conv)
    xs = xc[:, :B_WIDTH]
    dt = _softplus(sm_ref[0] + dtb_ref[...])
    ea = jnp.exp(dt * (-jnp.exp(alog_ref[...])))
    hr = lax.broadcasted_iota(jnp.int32, (LANES, B_WIDTH), 0)
    hc = lax.broadcasted_iota(jnp.int32, (LANES, B_WIDTH), 1)
    rep = jnp.where((hc // B_HEADDIM) == hr, 1.0, 0.0)
    both = jnp.concatenate([dt, ea, jnp.zeros((6, LANES), F32)], axis=0)
    both_rep = jnp.dot(both, rep, preferred_element_type=F32, precision=hi)
    u = both_rep[0:1] * xs
    ea_rep = both_rep[1:2]
    gn = B_GROUPS * B_STATE
    rows = (B_HEADS // B_GROUPS) * B_HEADDIM
    ys = []
    for g in range(B_GROUPS):
        r0 = g * rows
        bg = xc[:, B_WIDTH + g * B_STATE:B_WIDTH + (g + 1) * B_STATE]
        cg = xc[:, B_WIDTH + gn + g * B_STATE:B_WIDTH + gn + (g + 1) * B_STATE]
        h0 = h0_ref[0, 0, r0:r0 + rows, :]
        hn = jnp.dot(_diag_rows(ea_rep[:, r0:r0 + rows]), h0, preferred_element_type=F32, precision=hi)
        hn = hn + jnp.dot(_diag_rows(u[:, r0:r0 + rows]), jnp.broadcast_to(bg, (rows, B_STATE)),
                          preferred_element_type=F32, precision=hi)
        hout_ref[0, r0:r0 + rows, :] = hn
        c8 = jnp.broadcast_to(cg, (8, B_STATE)).astype(BF16)
        ys.append(lax.dot_general(c8, hn.astype(BF16), NT_DIMS, preferred_element_type=F32)[0:1])
    y = jnp.concatenate(ys, axis=1) + dsk_ref[...] * xs
    y = y * _silu(jnp.concatenate([r[0] for r in z_refs], axis=1))
    ms = jnp.mean(y * y, axis=-1, keepdims=True)
    o_ref[0] = y * lax.rsqrt(ms + NORM_EPS) * g_ref[...]


def _ssd_sample_call(proj3, state_conv, state_ssm4, layer, conv_w, conv_b, dt_bias, a_log, d_skip, norm_g):
    bs = proj3.shape[0]
    const = lambda b: (0, 0)
    nrow = B_HEADS * B_HEADDIM
    return pl.pallas_call(
        _ssd_sample_kernel,
        grid=(bs,),
        in_specs=[
            *[pl.BlockSpec((1, 1, SSD_COLS), functools.partial(lambda k, b: (b, 0, COL_XBC // SSD_COLS + k), k))
              for k in range(B_CONV_DIM // SSD_COLS)],
            *[pl.BlockSpec((1, 1, SSD_COLS), functools.partial(lambda k, b: (b, 0, COL_Z // SSD_COLS + k), k))
              for k in range(B_WIDTH // SSD_COLS)],
            pl.BlockSpec((1, 1, LANES), lambda b: (b, 0, COL_SMALL // LANES)),
            pl.BlockSpec((1, 1, CONV_W - 1, B_CONV_DIM), lambda b: (layer, b, 0, 0)),
            pl.BlockSpec((1, 1, nrow, B_STATE), lambda b: (layer, b, 0, 0)),
            pl.BlockSpec((CONV_W, B_CONV_DIM), const),
            pl.BlockSpec((1, B_CONV_DIM), const),
            pl.BlockSpec((1, LANES), const),
            pl.BlockSpec((1, LANES), const),
            pl.BlockSpec((1, B_WIDTH), const),
            pl.BlockSpec((1, B_WIDTH), const),
        ],
        out_specs=[
            pl.BlockSpec((1, 1, B_WIDTH), lambda b: (b, 0, 0)),
            pl.BlockSpec((1, nrow, B_STATE), lambda b: (b, 0, 0)),
            pl.BlockSpec((1, CONV_W - 1, B_CONV_DIM), lambda b: (b, 0, 0)),
        ],
        out_shape=[
            jax.ShapeDtypeStruct((bs, 1, B_WIDTH), F32),
            jax.ShapeDtypeStruct((bs, nrow, B_STATE), F32),
            jax.ShapeDtypeStruct((bs, CONV_W - 1, B_CONV_DIM), F32),
        ],
        compiler_params=_cparams(("parallel",)),
        name="ssd_sample",
    )(*([proj3] * (B_CONV_DIM // SSD_COLS + B_WIDTH // SSD_COLS + 1)), state_conv, state_ssm4, conv_w,
      conv_b.reshape(1, -1), _pad_lanes(dt_bias),
      _pad_lanes(a_log), jnp.repeat(d_skip, B_HEADDIM).reshape(1, B_WIDTH), norm_g.reshape(1, B_WIDTH))


CMP_PAGES = 32


def _compress_paged_kernel(pt_ref, pe_ref, w1_ref, *rest):
    page_refs = [_row_view(r, PAGE_SIZE * NSA_ROWS) for r in rest[:CMP_PAGES]]
    o_ref = rest[CMP_PAGES]
    per_page = PAGE_SIZE // CMP_STRIDE
    outs = []
    for kv in range(2):
        def load_rows(r, kv=kv):
            return jnp.concatenate([page_refs[g][pl.ds(NSA_ROWS * r + kv, per_page, stride=NSA_ROWS * CMP_STRIDE), :]
                                    for g in range(CMP_PAGES)], axis=0)
        outs += list(_compress_partials(load_rows, pe_ref, w1_ref, kv, CMP_PAGES * per_page))
    o_ref[0] = jnp.concatenate(outs, axis=1)


def _compress_paged_call(page_table, cache, layer, pe, w1):
    bs, n_pages = page_table.shape
    steps = n_pages // CMP_PAGES
    per_page = PAGE_SIZE // CMP_STRIDE

    def page_spec(g):
        return pl.BlockSpec((None, None, PAGE_SIZE, NSA_ROWS, C_DHEAD),
                            lambda b, p, pt: (layer, pt[b, p * CMP_PAGES + g], 0, 0, 0))

    grid_spec = pltpu.PrefetchScalarGridSpec(
        num_scalar_prefetch=1,
        grid=(bs, steps),
        in_specs=[
            pl.BlockSpec((2, CMP_LEN, C_DHEAD), lambda b, p, pt: (0, 0, 0)),
            pl.BlockSpec((2, CMP_LEN, C_DHEAD, C_DHEAD), lambda b, p, pt: (0, 0, 0, 0)),
        ] + [page_spec(g) for g in range(CMP_PAGES)],
        out_specs=pl.BlockSpec((1, CMP_PAGES * per_page, 4 * C_DHEAD), lambda b, p, pt: (b, p, 0)),
    )
    return pl.pallas_call(
        _compress_paged_kernel,
        grid_spec=grid_spec,
        out_shape=jax.ShapeDtypeStruct((bs, n_pages * per_page, 4 * C_DHEAD), F32),
        compiler_params=_cparams(("parallel", "arbitrary")),
        name="nsa_compress_sample",
    )(page_table, pe, w1.reshape(2, CMP_LEN, C_DHEAD, C_DHEAD), *([cache] * CMP_PAGES))


def _heads_to_rows(q):
    rows = [q[:, h * C_DHEAD:(h + 1) * C_DHEAD] for h in range(C_HEADS)]
    return jnp.concatenate(rows + [jnp.zeros((8 - C_HEADS, C_DHEAD), F32)], axis=0)


SEL_LANES = 384


def _nsa_select_kernel(part_ref, w2_ref, q_ref, ocmp_ref, sel_ref, *, q_pos):
    nchunk = part_ref.shape[1]
    n_cmp = (q_pos + 1 - CMP_LEN) // CMP_STRIDE + 1
    n_sel = -(-(q_pos + 1) // SEL_BLOCK)
    scale = C_DHEAD ** -0.5
    part = part_ref[0]
    kv_cmp = []
    for kv in range(2):
        lo = part[:, (2 * kv) * C_DHEAD:(2 * kv + 1) * C_DHEAD]
        hi = part[:, (2 * kv + 1) * C_DHEAD:(2 * kv + 2) * C_DHEAD]
        hid = lo + pltpu.roll(hi, nchunk - 1, 0)
        kv_cmp.append(jnp.dot(_gelu(hid).astype(BF16), w2_ref[kv].astype(BF16), preferred_element_type=F32).astype(BF16))
    q8 = _heads_to_rows(q_ref[0]).astype(BF16)
    row = lax.broadcasted_iota(jnp.int32, (8, 1), 0)
    slope = jnp.exp2(-2.0 * (row + 1).astype(F32))
    n_i = lax.broadcasted_iota(jnp.int32, (1, nchunk), 1)
    dist_c = q_pos - (n_i * CMP_STRIDE + CMP_LEN - 1)
    ok = (dist_c >= 0) & (n_i < n_cmp)
    s = lax.dot_general(q8, kv_cmp[0], NT_DIMS, preferred_element_type=F32) * scale - slope * dist_c.astype(F32)
    e, den = _masked_softmax(s, ok)
    p = jnp.where(row < C_HEADS, e / jnp.maximum(den, 1e-30), 0.0)
    ocmp_ref[0] = jnp.dot(p.astype(BF16), kv_cmp[1], preferred_element_type=F32)
    psum = jnp.broadcast_to(jnp.sum(p, axis=0, keepdims=True), (8, nchunk))
    c_i = lax.broadcasted_iota(jnp.int32, (nchunk, SEL_LANES), 0)
    j_i = lax.broadcasted_iota(jnp.int32, (nchunk, SEL_LANES), 1)
    lo_ = jnp.maximum(c_i * CMP_STRIDE, j_i * SEL_BLOCK)
    hi_ = jnp.minimum(c_i * CMP_STRIDE + CMP_LEN, (j_i + 1) * SEL_BLOCK)
    ovl = jnp.where((c_i < n_cmp) & (j_i < n_sel), jnp.maximum(hi_ - lo_, 0).astype(F32) * (1.0 / CMP_LEN), 0.0)
    imp = jnp.dot(psum, ovl, preferred_element_type=F32, precision=lax.Precision.HIGHEST)[0:1]
    lane = lax.broadcasted_iota(jnp.int32, (1, SEL_LANES), 1)
    qblk = q_pos // SEL_BLOCK
    forced = (lane == 0) | (lane == qblk) | (lane == qblk - 1)
    score = jnp.where(lane <= qblk, imp + jnp.where(forced, FORCE_SCORE, 0.0), NEG_INF)
    score = jnp.where(lane < n_sel, score, -jnp.inf)
    lane_f = lane.astype(F32)
    out_lane = lax.broadcasted_iota(jnp.int32, (1, LANES), 1)
    sel = jnp.full((1, LANES), -1.0, F32)
    for k in range(min(SEL_TOPK, n_sel)):
        m = jnp.max(score, axis=-1, keepdims=True)
        idx = jnp.min(jnp.where(score == m, lane_f, 1e9), axis=-1, keepdims=True)
        sel = jnp.where(out_lane == k, jnp.where(m > NEG_INF / 2, idx, -1.0), sel)
        score = jnp.where(lane_f == idx, -jnp.inf, score)
    sel_ref[0] = sel.astype(jnp.int32)


def _nsa_select_call(part, w2, q3, q_pos):
    bs, nchunk, _ = part.shape
    return pl.pallas_call(
        functools.partial(_nsa_select_kernel, q_pos=q_pos),
        grid=(bs,),
        in_specs=[
            pl.BlockSpec((1, nchunk, 4 * C_DHEAD), lambda b: (b, 0, 0)),
            pl.BlockSpec((2, C_DHEAD, C_DHEAD), lambda b: (0, 0, 0)),
            pl.BlockSpec((1, 1, C_WIDTH), lambda b: (b, 0, COL_CQ // C_WIDTH)),
        ],
        out_specs=[pl.BlockSpec((1, 8, C_DHEAD), lambda b: (b, 0, 0)), pl.BlockSpec((1, 1, LANES), lambda b: (b, 0, 0))],
        out_shape=[jax.ShapeDtypeStruct((bs, 8, C_DHEAD), F32), jax.ShapeDtypeStruct((bs, 1, LANES), jnp.int32)],
        compiler_params=_cparams(("parallel",)),
        name="nsa_select_sample",
    )(part, w2, q3)


def _nsa_attend_kernel(sel_ref, pt_ref, q_ref, new_ref, wnew_ref, sm_ref, ocmp_ref, win_ref, g_ref, *rest, q_pos):
    k_eff = SEL_TOPK
    blk_refs = [_row_view(r, SEL_BLOCK * NSA_ROWS) for r in rest[:k_eff]]
    o_ref, wout_ref = rest[k_eff:]
    b = pl.program_id(0)
    scale = C_DHEAD ** -0.5
    n_past_blocks = q_pos // SEL_BLOCK
    q8f = _heads_to_rows(q_ref[0])
    q8 = q8f.astype(BF16)
    row = lax.broadcasted_iota(jnp.int32, (8, 1), 0)
    slope = jnp.exp2(-2.0 * (row + 1).astype(F32))
    lane64 = lax.broadcasted_iota(jnp.int32, (1, SEL_BLOCK), 1)

    ss, vs, oks = [], [], []
    new_sel = jnp.zeros((1, 1), jnp.int32)
    for k in range(k_eff):
        j = sel_ref[b, k]
        k_sel = blk_refs[k][pl.ds(2, SEL_BLOCK, stride=NSA_ROWS), :]
        v_sel = blk_refs[k][pl.ds(3, SEL_BLOCK, stride=NSA_ROWS), :]
        s = lax.dot_general(q8, k_sel.astype(BF16), NT_DIMS, preferred_element_type=F32)
        dist = q_pos - (j * SEL_BLOCK + lane64)
        ss.append(s * scale - slope * dist.astype(F32))
        oks.append(lane64 * 0 + jnp.where((j >= 0) & (j < n_past_blocks), 1, 0))
        vs.append(v_sel.astype(BF16))
        new_sel = new_sel + jnp.where(j == n_past_blocks, 1, 0)
    s = jnp.concatenate(ss, axis=1)
    ok = jnp.concatenate(oks, axis=1) > 0
    new = new_ref[0]
    s_n = jnp.sum(q8f * new[:, 2 * C_DHEAD:3 * C_DHEAD], axis=-1, keepdims=True) * scale
    s_n = jnp.where(new_sel > 0, s_n, NEG_INF)
    s = jnp.where(ok, s, NEG_INF)
    m = jnp.maximum(jnp.max(s, axis=-1, keepdims=True), s_n)
    e = jnp.where(ok, jnp.exp(s - m), 0.0)
    e_n = jnp.where(new_sel > 0, jnp.exp(s_n - m), 0.0)
    den = jnp.sum(e, axis=-1, keepdims=True) + e_n
    o_sel = (jnp.dot(e.astype(BF16), jnp.concatenate(vs, axis=0), preferred_element_type=F32)
             + e_n * new[:, 3 * C_DHEAD:]) / den

    lw = win_ref.shape[0]
    win_k = win_ref[:, 0, :]
    win_v = win_ref[:, 1, :]
    wnew = wnew_ref[0]
    wpos = lax.broadcasted_iota(jnp.int32, (1, lw), 1)
    dist_w = lw - wpos
    ok_w = dist_w < WINDOW
    s = lax.dot_general(q8, win_k.astype(BF16), NT_DIMS, preferred_element_type=F32) * scale
    s = jnp.where(ok_w, s - slope * dist_w.astype(F32), NEG_INF)
    s_n = jnp.sum(q8f * wnew[:, :C_DHEAD], axis=-1, keepdims=True) * scale
    m = jnp.maximum(jnp.max(s, axis=-1, keepdims=True), s_n)
    e = jnp.where(ok_w, jnp.exp(s - m), 0.0)
    e_n = jnp.exp(s_n - m)
    den = jnp.sum(e, axis=-1, keepdims=True) + e_n
    o_win = (jnp.dot(e.astype(BF16), win_v.astype(BF16), preferred_element_type=F32)
             + e_n * wnew[:, C_DHEAD:]) / den
    keep = min(WINDOW, lw + 1)
    wout_ref[0:keep - 1, :, :] = win_ref[lw + 1 - keep:lw, :, :]
    wout_ref[keep - 1:keep, 0, :] = wnew[:, :C_DHEAD]
    wout_ref[keep - 1:keep, 1, :] = wnew[:, C_DHEAD:]

    gate = 1.0 / (1.0 + jnp.exp(-sm_ref[0]))
    outs = []
    for h in range(C_HEADS):
        g0 = gate[:, GATE_LANE0 + h:GATE_LANE0 + h + 1]
        g1 = gate[:, GATE_LANE0 + C_HEADS + h:GATE_LANE0 + C_HEADS + h + 1]
        g2 = gate[:, GATE_LANE0 + 2 * C_HEADS + h:GATE_LANE0 + 2 * C_HEADS + h + 1]
        outs.append(g0 * ocmp_ref[0, h:h + 1, :] + g1 * o_sel[h:h + 1, :] + g2 * o_win[h:h + 1, :])
    o = jnp.concatenate(outs, axis=1)
    ms = jnp.mean(o * o, axis=-1, keepdims=True)
    o_ref[0] = o * lax.rsqrt(ms + NORM_EPS) * g_ref[...]


def _nsa_attend_call(sel, page_table, proj3, ocmp, cache, cache_win, layer, out_g, q_pos):
    bs = proj3.shape[0]
    lw = cache_win.shape[2]
    keep = min(WINDOW, lw + 1)
    n_pages = page_table.shape[1]
    halves = PAGE_SIZE // SEL_BLOCK

    def blk_spec(k):
        def imap(b, sel_r, pt_r):
            j = jnp.clip(sel_r[b, k], 0, n_pages * halves - 1)
            return (layer, pt_r[b, j // halves], j % halves, 0, 0)
        return pl.BlockSpec((None, None, SEL_BLOCK, NSA_ROWS, C_DHEAD), imap)

    row3 = lambda w, col: pl.BlockSpec((1, 1, w), lambda b, s_, p_: (b, 0, col))
    grid_spec = pltpu.PrefetchScalarGridSpec(
        num_scalar_prefetch=2,
        grid=(bs,),
        in_specs=[
            row3(C_WIDTH, COL_CQ // C_WIDTH),
            row3(4 * C_DHEAD, COL_CKV // (4 * C_DHEAD)),
            row3(2 * C_DHEAD, COL_WIN // (2 * C_DHEAD)),
            row3(LANES, COL_SMALL // LANES),
            pl.BlockSpec((1, 8, C_DHEAD), lambda b, s_, p_: (b, 0, 0)),
            pl.BlockSpec((None, None, lw, 2, C_DHEAD), lambda b, s_, p_: (layer, b, 0, 0, 0)),
            pl.BlockSpec((1, C_WIDTH), lambda b, s_, p_: (0, 0)),
        ] + [blk_spec(k) for k in range(SEL_TOPK)],
        out_specs=[
            pl.BlockSpec((1, 1, C_WIDTH), lambda b, s_, p_: (b, 0, 0)),
            pl.BlockSpec((None, keep, 2, C_DHEAD), lambda b, s_, p_: (b, 0, 0, 0)),
        ],
    )
    return pl.pallas_call(
        functools.partial(_nsa_attend_kernel, q_pos=q_pos),
        grid_spec=grid_spec,
        out_shape=[jax.ShapeDtypeStruct((bs, 1, C_WIDTH), F32), jax.ShapeDtypeStruct((bs, keep, 2, C_DHEAD), F32)],
        compiler_params=_cparams(("arbitrary",)),
        name="nsa_attend_sample",
    )(sel, page_table, proj3, proj3, proj3, proj3, ocmp, cache_win, out_g.reshape(1, C_WIDTH),
      *([cache] * SEL_TOPK))


def _token_mixer_tail(xa, xb, oa, ob, oc, w_out_b, ffn_g, wq_t, subkeys, u_tab, v_tab, tm, tn, et, emit_bf16=False,
                      layer=0):
    h, hn_t = _outproj_call(xa, xb, oa, ob, oc, w_out_b, ffn_g, tm)
    q_t = _peer_q_call(wq_t, hn_t, tm)
    route = _peer_route_call(q_t, subkeys)
    return h, _peer_expert_call(hn_t, route, u_tab, v_tab, tn, et, emit_bf16, layer)


def kernel(x_prompt, x_sample, cache_diff_kv, cache_nsa_kv, cache_nsa_win, state_ssm, state_conv, page_table,
           norm_mix_g, w_in, w_out, diff_lam, diff_subln_g, ssm_conv_w, ssm_conv_b, ssm_dt_bias, ssm_a_log,
           ssm_d, ssm_norm_g, nsa_pe, nsa_cmp_w1, nsa_cmp_w2, nsa_out_g, norm_ffn_g, peer_wq, peer_subkeys,
           peer_u, peer_v, norm_final_g):
    depth = w_in.shape[0]
    bp, t, _ = x_prompt.shape
    bs = x_sample.shape[0]
    past_len = page_table.shape[1] * PAGE_SIZE
    n_p = bp * t
    n_s = LANES
    tm_p = 512

    xa_p, xb_p = x_prompt.reshape(n_p, D_MODEL), None
    xa_s = jnp.pad(x_sample.reshape(bs, D_MODEL), ((0, n_s - bs), (0, 0)))
    xb_s = None
    st_p = [[] for _ in range(5)]
    st_s = [[] for _ in range(5)]
    state_ssm4 = state_ssm.reshape(depth, bs, B_HEADS * B_HEADDIM, B_STATE)
    w_in_t = jnp.transpose(w_in, (2, 0, 1))
    w_main = _cast_w_main_call(w_in_t)
    w_tail = _pack_w_tail_call(w_in_t)
    for l in range(depth):
        lam_init = 0.8 - 0.6 * math.exp(-0.3 * l)
        w_out_b = w_out[l].astype(BF16)
        wq_t = peer_wq[l].T.astype(BF16)

        proj_s, dkv_s, nkv_s = _proj_call(xa_s, xb_s, norm_mix_g[l], w_main, w_tail, l, n_s)
        proj_s = proj_s[:bs]
        proj_s3 = proj_s.reshape(bs, 1, PROJ_W)
        o_a = _diff_sample_call(page_table, proj_s3, cache_diff_kv, l, diff_lam[l], diff_subln_g[l], lam_init)
        o_b, h_new, conv_new = _ssd_sample_call(proj_s3, state_conv, state_ssm4, l, ssm_conv_w[l], ssm_conv_b[l],
                                                ssm_dt_bias[l], ssm_a_log[l], ssm_d[l], ssm_norm_g[l])
        part = _compress_paged_call(page_table, cache_nsa_kv, l, nsa_pe[l], nsa_cmp_w1[l])
        o_cmp, sel = _nsa_select_call(part, nsa_cmp_w2[l], proj_s3, past_len)
        o_c, win_out = _nsa_attend_call(sel[:, 0, :SEL_TOPK], page_table, proj_s3, o_cmp, cache_nsa_kv, cache_nsa_win, l,
                                        nsa_out_g[l], past_len)
        st_s[0].append(dkv_s[:bs].reshape(bs, 1, 2, A_HEADS, A_DHEAD))
        st_s[1].append(nkv_s[:bs].reshape(bs, 1, NSA_ROWS, C_DHEAD))
        st_s[2].append(win_out)
        st_s[3].append(h_new.reshape(bs, B_HEADS, B_HEADDIM, B_STATE))
        st_s[4].append(conv_new)
        pad = lambda a: jnp.pad(a.reshape(bs, -1), ((0, n_s - bs), (0, 0)))
        xa_s, (xb_s, u_b, vt_b) = _token_mixer_tail(xa_s, xb_s, pad(o_a), pad(o_b), pad(o_c), w_out_b, norm_ffn_g[l],
                                                   wq_t, peer_subkeys[l], peer_u, peer_v, n_s, n_s, 512, True, l)

        proj, dkv_p, nkv_p = _proj_call(xa_p, xb_p, norm_mix_g[l], w_main, w_tail, l, tm_p)
        o_a = _diff_prompt_call(proj, bp, t, diff_lam[l], diff_subln_g[l], lam_init)
        o_b, h_ssm = _ssd_prompt_call(proj, bp, t, ssm_conv_w[l], ssm_conv_b[l], ssm_dt_bias[l], ssm_a_log[l],
                                      ssm_d[l], ssm_norm_g[l])
        kc, vc = _compress_prompt_call(proj, bp, t, nsa_pe[l], nsa_cmp_w1[l], nsa_cmp_w2[l])
        o_c = _nsa_prompt_call(proj, kc, vc, bp, t, nsa_out_g[l])
        proj3 = proj.reshape(bp, t, PROJ_W)
        st_p[0].append(dkv_p.reshape(bp, t, 2, A_HEADS, A_DHEAD))
        st_p[1].append(nkv_p.reshape(bp, t, NSA_ROWS, C_DHEAD))
        keep = min(WINDOW, t)
        st_p[2].append(proj3[:, t - keep:, COL_WIN:COL_WIN + 2 * C_DHEAD].reshape(bp, keep, 2, C_DHEAD))
        st_p[3].append(h_ssm)
        st_p[4].append(proj3[:, t - (CONV_W - 1):, COL_XBC:COL_XBC + B_CONV_DIM])
        xa_p, xb_p = _token_mixer_tail(xa_p, xb_p, o_a, o_b, o_c, w_out_b, norm_ffn_g[l], wq_t, peer_subkeys[l],
                                       u_b, vt_b, tm_p, 512, 1024)

    y_p = _final_norm_call(xa_p, xb_p, norm_final_g, tm_p).reshape(bp, t, D_MODEL)
    y_s = _final_norm_call(xa_s, xb_s, norm_final_g, n_s)[:bs].reshape(bs, 1, D_MODEL)
    return (y_p, y_s) + tuple(jnp.stack(s) for s in st_p) + tuple(jnp.stack(s) for s in st_s)
```

```python
import functools
import math

import jax
import jax.numpy as jnp
from jax import lax
from jax.experimental import pallas as pl
from jax.experimental.pallas import tpu as pltpu

F32 = jnp.float32
BF16 = jnp.bfloat16

D_MODEL = 2048
A_HEADS = 4
A_HALF = 64
A_DHEAD = 128
A_WIDTH = 512
B_WIDTH = 1024
B_HEADDIM = 64
B_HEADS = 16
B_GROUPS = 4
B_STATE = 128
CONV_W = 4
B_CONV_DIM = 2048
SSD_CHUNK = 128
C_HEADS = 4
C_DHEAD = 128
C_WIDTH = 512
CMP_LEN = 32
CMP_STRIDE = 16
SEL_BLOCK = 64
SEL_TOPK = 16
WINDOW = 512
PEER_HEADS = 8
PEER_NKEYS = 128
PEER_TOPK = 16
PEER_DQ = 256
PAGE_SIZE = 128
NSA_ROWS = 4
NORM_EPS = 1e-6
NEG_INF = -1e30
FORCE_SCORE = 1e4

LANES = 128
VMEM_LIMIT = 56 * 1024 * 1024

COL_AQ = 0
COL_AK = 512
COL_AV = 1024
COL_Z = 1536
COL_XBC = 2560
COL_MAIN = 4608
COL_CQ = 4608
COL_CKV = 5120
COL_WIN = 5632
COL_SMALL = 5888
PROJ_W = 6144
GATE_LANE0 = B_HEADS

NT_DIMS = (((1,), (1,)), ((), ()))


def _cparams(sem, vmem=VMEM_LIMIT):
    return pltpu.CompilerParams(dimension_semantics=sem, vmem_limit_bytes=vmem)


def _gelu(x):
    return 0.5 * x * (1.0 + jnp.tanh(math.sqrt(2.0 / math.pi) * (x + 0.044715 * (x * x * x))))


def _silu(x):
    return x * (1.0 / (1.0 + jnp.exp(-x)))


def _softplus(x):
    return jnp.maximum(x, 0.0) + jnp.log(1.0 + jnp.exp(-jnp.abs(x)))


def _row_view(ref, rows):
    return ref.reshape(rows, ref.shape[-1])


def _alibi_slope(h):
    if isinstance(h, int):
        return 2.0 ** (-2.0 * (h + 1))
    return jnp.exp2(jnp.full((1, 1), -2.0, F32) * (h + 1).astype(F32))


def _proj_kernel(*refs, two, tc):
    if two:
        xa_ref, xb_ref, g_ref, wm_ref, wt_ref, o_ref, dkv_ref, nkv_ref, xn_ref = refs
    else:
        xa_ref, g_ref, wm_ref, wt_ref, o_ref, dkv_ref, nkv_ref, xn_ref = refs
    j = pl.program_id(1)
    n_main = COL_MAIN // tc

    @pl.when(j == 0)
    def _():
        x = xa_ref[...]
        if two:
            x = x + xb_ref[...]
        ms = jnp.mean(x * x, axis=-1, keepdims=True)
        xn_ref[...] = (x * lax.rsqrt(ms + NORM_EPS) * g_ref[...]).astype(BF16)

    def store_cache_rows(res, tile):
        for b in range(2 * A_HEADS):
            col = COL_AK + b * LANES
            if col // tc == tile:
                dkv_ref[:, b // A_HEADS, b % A_HEADS, :] = res[:, col % tc:col % tc + LANES]
        for b in range(NSA_ROWS):
            col = COL_CKV + b * LANES
            if col // tc == tile:
                nkv_ref[:, b, :] = res[:, col % tc:col % tc + LANES]

    state_tiles = sorted({(COL_AK + b * LANES) // tc for b in range(2 * A_HEADS)}
                         | {(COL_CKV + b * LANES) // tc for b in range(NSA_ROWS)})

    def finish(res, tiles):
        o_ref[...] = res
        for tile in tiles:
            if tile in state_tiles:
                pl.when(j == tile)(functools.partial(store_cache_rows, res, tile))

    @pl.when(j < n_main)
    def _():
        finish(jnp.dot(xn_ref[...], wm_ref[...], preferred_element_type=F32), range(n_main))

    @pl.when(j >= n_main)
    def _():
        finish(jnp.dot(xn_ref[...], wt_ref[...], preferred_element_type=F32), range(n_main, PROJ_W // tc))


def _cast_t_kernel(w_ref, o_ref):
    for l in range(w_ref.shape[1]):
        o_ref[l] = w_ref[:, l, :].T.astype(BF16)


def _cast_w_main_call(w_in_t, tr=384):
    depth = w_in_t.shape[1]
    return pl.pallas_call(
        _cast_t_kernel,
        grid=(COL_MAIN // tr,),
        in_specs=[pl.BlockSpec((tr, depth, D_MODEL), lambda j: (j, 0, 0))],
        out_specs=pl.BlockSpec((depth, D_MODEL, tr), lambda j: (0, 0, j)),
        out_shape=jax.ShapeDtypeStruct((depth, D_MODEL, COL_MAIN), BF16),
        compiler_params=_cparams(("parallel",)),
        name="w_in_cast",
    )(w_in_t)


def _pack_tail_kernel(w_ref, o_ref, *, n_in):
    width, depth, tk = w_ref.shape
    dt_w, gate_w = B_HEADS, 3 * C_HEADS
    cq0 = dt_w
    ckv0 = cq0 + C_WIDTH
    gate0 = n_in - COL_MAIN - gate_w
    pad = jnp.zeros((PROJ_W - n_in, tk), F32)
    for l in range(depth):
        x = w_ref[:, l, :]
        packed = jnp.concatenate([x[cq0:ckv0], x[ckv0:gate0], x[0:dt_w], x[gate0:gate0 + gate_w], pad], axis=0)
        o_ref[l] = packed.T.astype(BF16)


def _pack_w_tail_call(w_in_t, tk=512):
    width = PROJ_W - COL_MAIN
    depth = w_in_t.shape[1]
    assert COL_MAIN % width == 0
    return pl.pallas_call(
        functools.partial(_pack_tail_kernel, n_in=w_in_t.shape[0]),
        grid=(D_MODEL // tk,),
        in_specs=[pl.BlockSpec((width, depth, tk), lambda k: (COL_MAIN // width, 0, k))],
        out_specs=pl.BlockSpec((depth, tk, width), lambda k: (0, k, 0)),
        out_shape=jax.ShapeDtypeStruct((depth, D_MODEL, width), BF16),
        compiler_params=_cparams(("parallel",)),
        name="w_in_tail_pack",
    )(w_in_t)


def _proj_call(xa, xb, g, w_main, w_tail, layer, tm):
    n = xa.shape[0]
    two = xb is not None
    tc = 1536
    n_main = COL_MAIN // tc
    assert COL_MAIN % tc == 0 and w_tail.shape[2] == PROJ_W - COL_MAIN == tc
    xspec = pl.BlockSpec((tm, D_MODEL), lambda i, j: (i, 0))
    ins = [xa] + ([xb] if two else []) + [g.reshape(1, D_MODEL), w_main, w_tail]
    specs = [xspec] + ([xspec] if two else []) + [
        pl.BlockSpec((1, D_MODEL), lambda i, j: (0, 0)),
        pl.BlockSpec((None, D_MODEL, tc), lambda i, j: (layer, 0, jnp.minimum(j, n_main - 1))),
        pl.BlockSpec((None, D_MODEL, tc), lambda i, j: (layer, 0, jnp.maximum(j - n_main, 0)),
                     pipeline_mode=pl.Buffered(1)),
    ]
    return pl.pallas_call(
        functools.partial(_proj_kernel, two=two, tc=tc),
        grid=(n // tm, PROJ_W // tc),
        in_specs=specs,
        out_specs=[pl.BlockSpec((tm, tc), lambda i, j: (i, j)),
                   pl.BlockSpec((tm, 2, A_HEADS, A_DHEAD), lambda i, j: (i, 0, 0, 0)),
                   pl.BlockSpec((tm, NSA_ROWS, C_DHEAD), lambda i, j: (i, 0, 0))],
        out_shape=[jax.ShapeDtypeStruct((n, PROJ_W), F32), jax.ShapeDtypeStruct((n, 2, A_HEADS, A_DHEAD), F32),
                   jax.ShapeDtypeStruct((n, NSA_ROWS, C_DHEAD), F32)],
        scratch_shapes=[pltpu.VMEM((tm, D_MODEL), BF16)],
        compiler_params=_cparams(("parallel", "arbitrary")),
        name="in_proj",
    )(*ins)


def _diff_lambda(dl, lam_init):
    a = jnp.sum(dl[0:1] * dl[1:2], axis=-1, keepdims=True)
    b = jnp.sum(dl[2:3] * dl[3:4], axis=-1, keepdims=True)
    return jnp.exp(a) - jnp.exp(b) + lam_init


CAUSAL_LEVELS = 4


def _causal_prefixes(i, nq, tq, body):
    levels = min(CAUSAL_LEVELS, nq)
    per = nq // levels
    for lv in range(levels):
        pl.when(i // per == lv)(functools.partial(body, (lv + 1) * per * tq))


def _diff_prompt_kernel(q_ref, k_ref, v_ref, dl_ref, g_ref, o_ref, *, tq, lam_init):
    h = pl.program_id(1)
    i = pl.program_id(2)
    t = k_ref.shape[0]
    scale = A_HALF ** -0.5

    def body(nk):
        lam = _diff_lambda(dl_ref[...], lam_init)
        q = q_ref[...] * scale
        lane = lax.broadcasted_iota(jnp.int32, (1, A_DHEAD), 1)
        kb = k_ref[0:nk, :].astype(BF16)
        vb = v_ref[0:nk, :].astype(BF16)
        qpos = i * tq + lax.broadcasted_iota(jnp.int32, (tq, 1), 0)
        kpos = lax.broadcasted_iota(jnp.int32, (1, nk), 1)
        ok = qpos >= kpos
        key_bias = _alibi_slope(h) * kpos.astype(F32)

        def half_attention(c):
            qc = jnp.where((lane >= c * A_HALF) & (lane < (c + 1) * A_HALF), q, 0.0).astype(BF16)
            s = lax.dot_general(qc, kb, NT_DIMS, preferred_element_type=F32) + key_bias
            s = jnp.where(ok, s, NEG_INF)
            e = jnp.exp(s - jnp.max(s, axis=-1, keepdims=True))
            pv = jnp.dot(e.astype(BF16), vb, preferred_element_type=F32)
            return pv / jnp.sum(e, axis=-1, keepdims=True)

        o = half_attention(0) - lam * half_attention(1)
        ms = jnp.mean(o * o, axis=-1, keepdims=True)
        o_ref[...] = o * lax.rsqrt(ms + NORM_EPS) * g_ref[...] * (1.0 - lam_init)

    _causal_prefixes(i, t // tq, tq, body)


def _diff_prompt_call(proj, bsz, t, dl, subln_g, lam_init, tq=256):
    nq = t // tq
    cq, ck, cv = COL_AQ // A_DHEAD, COL_AK // A_DHEAD, COL_AV // A_DHEAD
    return pl.pallas_call(
        functools.partial(_diff_prompt_kernel, tq=tq, lam_init=lam_init),
        grid=(bsz, A_HEADS, nq),
        in_specs=[
            pl.BlockSpec((tq, A_DHEAD), lambda b, h, i: (b * nq + i, cq + h)),
            pl.BlockSpec((t, A_DHEAD), lambda b, h, i: (b, ck + h)),
            pl.BlockSpec((t, A_DHEAD), lambda b, h, i: (b, cv + h)),
            pl.BlockSpec((4, A_HALF), lambda b, h, i: (0, 0)),
            pl.BlockSpec((1, A_DHEAD), lambda b, h, i: (0, 0)),
        ],
        out_specs=pl.BlockSpec((tq, A_DHEAD), lambda b, h, i: (b * nq + i, h)),
        out_shape=jax.ShapeDtypeStruct((bsz * t, A_WIDTH), F32),
        compiler_params=_cparams(("parallel", "parallel", "arbitrary")),
        name="diff_attn_prompt",
    )(proj, proj, proj, dl, subln_g.reshape(1, A_DHEAD))


SSD_COLS = 512


def _ssd_prompt_kernel(*refs):
    nx, nz = B_CONV_DIM // SSD_COLS, B_WIDTH // SSD_COLS
    xbc_refs, z_refs = refs[:nx], refs[nx:nx + nz]
    sm_ref, cw_ref, cb_ref, dtb_ref, alog_ref, dsk_ref, g_ref, o_ref, hout_ref, buf_ref, h_ref = refs[nx + nz:]
    c = pl.program_id(1)
    cs = SSD_CHUNK

    @pl.when(c == 0)
    def _():
        buf_ref[0:8, :] = jnp.zeros((8, B_CONV_DIM), F32)
        h_ref[...] = jnp.zeros_like(h_ref)

    xbc = jnp.concatenate([r[...] for r in xbc_refs], axis=1)
    buf_ref[8:8 + cs, :] = xbc
    cw = cw_ref[...]
    conv = cb_ref[...] + cw[3:4] * xbc
    for j in range(1, CONV_W):
        conv = conv + cw[3 - j:4 - j] * buf_ref[8 - j:8 - j + cs, :]
    buf_ref[0:8, :] = xbc[cs - 8:cs, :]
    xc = _silu(conv)
    xs = xc[:, :B_WIDTH]

    dt = _softplus(sm_ref[...] + dtb_ref[...])
    a_neg = -jnp.exp(alog_ref[...])
    dta = dt * a_neg
    row = lax.broadcasted_iota(jnp.int32, (cs, cs), 0)
    col = lax.broadcasted_iota(jnp.int32, (cs, cs), 1)
    causal = row >= col
    acum = jnp.dot(causal.astype(F32), dta, preferred_element_type=F32, precision=lax.Precision.HIGHEST)
    acum_t = acum.T
    dt_t = dt.T
    lane = lax.broadcasted_iota(jnp.int32, (1, LANES), 1)
    lo = lane < B_HEADDIM

    ys = []
    for g in range(B_GROUPS):
        bg = xc[:, B_WIDTH + g * B_STATE:B_WIDTH + (g + 1) * B_STATE]
        cg = xc[:, B_WIDTH + B_GROUPS * B_STATE + g * B_STATE:B_WIDTH + B_GROUPS * B_STATE + (g + 1) * B_STATE]
        bgb = bg.astype(BF16)
        cgb = cg.astype(BF16)
        cb = lax.dot_general(cgb, bgb, NT_DIMS, preferred_element_type=F32)
        for pr in range(2):
            h0 = g * 4 + pr * 2
            xpair = xs[:, h0 * B_HEADDIM:(h0 + 2) * B_HEADDIM]
            xpb = xpair.astype(BF16)
            ydiag = []
            ecol = []
            wcol = []
            elast = []
            for hh in (h0, h0 + 1):
                a_col = acum[:, hh:hh + 1]
                a_row = acum_t[hh:hh + 1, :]
                decay = jnp.exp(jnp.where(causal, a_col - a_row, NEG_INF))
                lm = cb * decay * dt_t[hh:hh + 1, :]
                ydiag.append(jnp.dot(lm.astype(BF16), xpb, preferred_element_type=F32))
                a_last = acum[cs - 1:cs, hh:hh + 1]
                ecol.append(jnp.exp(a_col))
                wcol.append(jnp.exp(a_last - a_col) * dt[:, hh:hh + 1])
                elast.append(jnp.exp(a_last))
            hp = h_ref[h0 * B_HEADDIM:(h0 + 2) * B_HEADDIM, :]
            yoff = lax.dot_general(cgb, hp.astype(BF16), NT_DIMS, preferred_element_type=F32)
            y = jnp.where(lo, ydiag[0], ydiag[1]) + yoff * jnp.where(lo, ecol[0], ecol[1])
            ys.append(y)
            wx = xpair * jnp.where(lo, wcol[0], wcol[1])
            upd = jnp.dot(wx.T.astype(BF16), bgb, preferred_element_type=F32)
            prow = lax.broadcasted_iota(jnp.int32, (LANES, 1), 0) < B_HEADDIM
            h_ref[h0 * B_HEADDIM:(h0 + 2) * B_HEADDIM, :] = jnp.where(prow, elast[0], elast[1]) * hp + upd

    y = jnp.concatenate(ys, axis=1)
    y = y + dsk_ref[...] * xs
    y = y * _silu(jnp.concatenate([r[...] for r in z_refs], axis=1))
    ms = jnp.mean(y * y, axis=-1, keepdims=True)
    o_ref[...] = y * lax.rsqrt(ms + NORM_EPS) * g_ref[...]

    @pl.when(c == pl.num_programs(1) - 1)
    def _():
        hout_ref[0] = h_ref[...]


def _pad_lanes(v, fill=0.0):
    v = v.reshape(1, -1).astype(F32)
    return jnp.pad(v, ((0, 0), (0, LANES - v.shape[1])), constant_values=fill)


def _ssd_prompt_call(proj, bsz, t, conv_w, conv_b, dt_bias, a_log, d_skip, norm_g):
    nc = t // SSD_CHUNK
    cs = SSD_CHUNK
    const = lambda b, c: (0, 0)
    o, hout = pl.pallas_call(
        _ssd_prompt_kernel,
        grid=(bsz, nc),
        in_specs=[
            *[pl.BlockSpec((cs, SSD_COLS), functools.partial(lambda k, b, c: (b * nc + c, COL_XBC // SSD_COLS + k), k))
              for k in range(B_CONV_DIM // SSD_COLS)],
            *[pl.BlockSpec((cs, SSD_COLS), functools.partial(lambda k, b, c: (b * nc + c, COL_Z // SSD_COLS + k), k))
              for k in range(B_WIDTH // SSD_COLS)],
            pl.BlockSpec((cs, LANES), lambda b, c: (b * nc + c, COL_SMALL // LANES)),
            pl.BlockSpec((CONV_W, B_CONV_DIM), const),
            pl.BlockSpec((1, B_CONV_DIM), const),
            pl.BlockSpec((1, LANES), const),
            pl.BlockSpec((1, LANES), const),
            pl.BlockSpec((1, B_WIDTH), const),
            pl.BlockSpec((1, B_WIDTH), const),
        ],
        out_specs=[
            pl.BlockSpec((cs, B_WIDTH), lambda b, c: (b * nc + c, 0)),
            pl.BlockSpec((1, B_HEADS * B_HEADDIM, B_STATE), lambda b, c: (b, 0, 0)),
        ],
        out_shape=[
            jax.ShapeDtypeStruct((bsz * t, B_WIDTH), F32),
            jax.ShapeDtypeStruct((bsz, B_HEADS * B_HEADDIM, B_STATE), F32),
        ],
        scratch_shapes=[pltpu.VMEM((8 + cs, B_CONV_DIM), F32), pltpu.VMEM((B_HEADS * B_HEADDIM, B_STATE), F32)],
        compiler_params=_cparams(("parallel", "arbitrary")),
        name="ssd_prompt",
    )(*([proj] * (B_CONV_DIM // SSD_COLS + B_WIDTH // SSD_COLS + 1)), conv_w, conv_b.reshape(1, -1),
      _pad_lanes(dt_bias), _pad_lanes(a_log), jnp.repeat(d_skip, B_HEADDIM).reshape(1, B_WIDTH),
      norm_g.reshape(1, B_WIDTH))
    return o, hout.reshape(bsz, B_HEADS, B_HEADDIM, B_STATE)


def _compress_partials(load_rows, pe_ref, w1_ref, kv, nchunk):
    acc = jnp.zeros((nchunk + 8, 2 * C_DHEAD), F32)
    tail = jnp.zeros((6, C_DHEAD), F32)
    for r in range(CMP_STRIDE):
        x = jnp.concatenate([load_rows(r), pe_ref[kv, r:r + 1, :], pe_ref[kv, CMP_STRIDE + r:CMP_STRIDE + r + 1, :], tail],
                            axis=0).astype(BF16)
        w = jnp.concatenate([w1_ref[kv, r], w1_ref[kv, CMP_STRIDE + r]], axis=1).astype(BF16)
        acc = acc + jnp.dot(x, w, preferred_element_type=F32)
    lo = acc[0:nchunk, :C_DHEAD] + acc[nchunk:nchunk + 1, :C_DHEAD]
    hi = acc[0:nchunk, C_DHEAD:] + acc[nchunk + 1:nchunk + 2, C_DHEAD:]
    return lo, hi


def _compress_kernel(k_ref, v_ref, pe_ref, w1_ref, w2_ref, kc_ref, vc_ref, *, nchunk):
    outs = []
    for kv, rows_ref in enumerate((k_ref, v_ref)):
        acc_lo, acc_hi = _compress_partials(lambda r: rows_ref[pl.ds(r, nchunk, stride=CMP_STRIDE), :],
                                            pe_ref, w1_ref, kv, nchunk)
        hid = acc_lo + pltpu.roll(acc_hi, nchunk - 1, 0)
        outs.append(jnp.dot(_gelu(hid).astype(BF16), w2_ref[kv].astype(BF16), preferred_element_type=F32))
    kc_ref[0] = outs[0]
    vc_ref[0] = outs[1]


def _compress_prompt_call(proj, bsz, t, pe, w1, w2):
    nchunk = t // CMP_STRIDE
    shp = jax.ShapeDtypeStruct((bsz, nchunk, C_DHEAD), F32)
    return pl.pallas_call(
        functools.partial(_compress_kernel, nchunk=nchunk),
        grid=(bsz,),
        in_specs=[
            pl.BlockSpec((t, C_DHEAD), lambda b: (b, COL_CKV // C_DHEAD)),
            pl.BlockSpec((t, C_DHEAD), lambda b: (b, COL_CKV // C_DHEAD + 1)),
            pl.BlockSpec((2, CMP_LEN, C_DHEAD), lambda b: (0, 0, 0)),
            pl.BlockSpec((2, CMP_LEN, C_DHEAD, C_DHEAD), lambda b: (0, 0, 0, 0)),
            pl.BlockSpec((2, C_DHEAD, C_DHEAD), lambda b: (0, 0, 0)),
        ],
        out_specs=[pl.BlockSpec((1, nchunk, C_DHEAD), lambda b: (b, 0, 0))] * 2,
        out_shape=[shp, shp],
        compiler_params=_cparams(("parallel",)),
        name="nsa_compress_prompt",
    )(proj, proj, pe, w1.reshape(2, CMP_LEN, C_DHEAD, C_DHEAD), w2)


def _masked_softmax(s, ok):
    s = jnp.where(ok, s, NEG_INF)
    m = jnp.max(s, axis=-1, keepdims=True)
    e = jnp.where(ok, jnp.exp(s - m), 0.0)
    return e, jnp.sum(e, axis=-1, keepdims=True)


def _topk_mask_lanes(score, k, n):
    lane = lax.broadcasted_iota(jnp.int32, (1, LANES), 1)
    rank = jnp.zeros(score.shape, F32)
    for i in range(n):
        ci = score[:, i:i + 1]
        beats = (ci > score) | ((ci == score) & (lane > i))
        rank = rank + jnp.where(beats, 1.0, 0.0)
    return (rank < k) & (lane < n)


def _nsa_prompt_kernel(q_ref, ks_ref, vs_ref, kw_ref, vw_ref, kc_ref, vc_ref, sm_ref, g_ref, o_ref, osel_ref, *, tq, n_cmp):
    i = pl.program_id(1)
    t = ks_ref.shape[0]
    n_sel = t // SEL_BLOCK
    scale = C_DHEAD ** -0.5
    sel_shift = SEL_BLOCK.bit_length() - 1
    wlen = min(t, WINDOW + tq)

    qpos = i * tq + lax.broadcasted_iota(jnp.int32, (tq, 1), 0)
    lane = lax.broadcasted_iota(jnp.int32, (1, LANES), 1)

    cmp_end = lane * CMP_STRIDE + (CMP_LEN - 1)
    dist_c = qpos - cmp_end
    ok_c = (dist_c >= 0) & (lane < n_cmp)
    dist_cf = dist_c.astype(F32)
    kcb = kc_ref[0].astype(BF16)
    vcb = vc_ref[0].astype(BF16)
    qs = [(q_ref[:, h * C_DHEAD:(h + 1) * C_DHEAD] * scale).astype(BF16) for h in range(C_HEADS)]
    o_cmp = []
    psum = jnp.zeros((tq, LANES), F32)
    for h in range(C_HEADS):
        s = lax.dot_general(qs[h], kcb, NT_DIMS, preferred_element_type=F32)
        s = s - _alibi_slope(h) * dist_cf
        e, den = _masked_softmax(s, ok_c)
        p = e / jnp.maximum(den, 1e-30)
        psum = psum + p
        o_cmp.append(jnp.dot(p.astype(BF16), vcb, preferred_element_type=F32))

    n_i = lax.broadcasted_iota(jnp.int32, (LANES, LANES), 0)
    j_i = lax.broadcasted_iota(jnp.int32, (LANES, LANES), 1)
    lo_ = jnp.maximum(n_i * CMP_STRIDE, j_i * SEL_BLOCK)
    hi_ = jnp.minimum(n_i * CMP_STRIDE + CMP_LEN, (j_i + 1) * SEL_BLOCK)
    ovl = jnp.maximum(hi_ - lo_, 0).astype(F32) * (1.0 / CMP_LEN)
    ovl = jnp.where((n_i < n_cmp) & (j_i < n_sel), ovl, 0.0)
    imp = jnp.dot(psum, ovl, preferred_element_type=F32, precision=lax.Precision.HIGHEST)
    qblk = qpos >> sel_shift
    sel_valid = lane <= qblk
    forced = (lane == 0) | (lane == qblk) | (lane == qblk - 1)
    score = jnp.where(sel_valid, imp + jnp.where(forced, FORCE_SCORE, 0.0), NEG_INF)
    score = jnp.where(lane < n_sel, score, -jnp.inf)
    chosen = _topk_mask_lanes(score, min(SEL_TOPK, n_sel), n_sel) & sel_valid
    chosen_b = jnp.where(chosen, 1.0, 0.0).astype(BF16)

    def attend(qh, kb, vb, key_bias, ok):
        s = lax.dot_general(qh, kb, NT_DIMS, preferred_element_type=F32) + key_bias
        s = jnp.where(ok, s, NEG_INF)
        e = jnp.exp(s - jnp.max(s, axis=-1, keepdims=True))
        return jnp.dot(e.astype(BF16), vb, preferred_element_type=F32) / jnp.sum(e, axis=-1, keepdims=True)

    def selected_branch(nk):
        e_j = lax.broadcasted_iota(jnp.int32, (LANES, nk), 0)
        e_k = lax.broadcasted_iota(jnp.int32, (LANES, nk), 1)
        expand = jnp.where((e_k >> sel_shift) == e_j, 1.0, 0.0).astype(BF16)
        key_sel = jnp.dot(chosen_b, expand, preferred_element_type=F32) > 0.5
        kpos = lax.broadcasted_iota(jnp.int32, (1, nk), 1)
        kpos_f = kpos.astype(F32)
        ok_s = key_sel & (qpos >= kpos)
        ksb = ks_ref[0:nk, :].astype(BF16)
        vsb = vs_ref[0:nk, :].astype(BF16)
        for h in range(C_HEADS):
            osel_ref[:, h * C_DHEAD:(h + 1) * C_DHEAD] = attend(qs[h], ksb, vsb, _alibi_slope(h) * kpos_f, ok_s)

    nq = t // tq
    if nq % 2 == 0:
        pl.when(i < nq // 2)(functools.partial(selected_branch, t // 2))
        pl.when(i >= nq // 2)(functools.partial(selected_branch, t))
    else:
        selected_branch(t)

    w0 = pl.multiple_of(jnp.clip(i * tq - WINDOW, 0, t - wlen), 8)
    wpos = w0 + lax.broadcasted_iota(jnp.int32, (1, wlen), 1)
    wpos_f = wpos.astype(F32)
    dist_w = qpos - wpos
    ok_w = (dist_w >= 0) & (dist_w < WINDOW)
    kwb = kw_ref[pl.ds(w0, wlen), :].astype(BF16)
    vwb = vw_ref[pl.ds(w0, wlen), :].astype(BF16)
    gate = 1.0 / (1.0 + jnp.exp(-sm_ref[...]))
    outs = []
    for h in range(C_HEADS):
        o_sel = osel_ref[:, h * C_DHEAD:(h + 1) * C_DHEAD]
        o_win = attend(qs[h], kwb, vwb, _alibi_slope(h) * wpos_f, ok_w)
        g0 = gate[:, GATE_LANE0 + h:GATE_LANE0 + h + 1]
        g1 = gate[:, GATE_LANE0 + C_HEADS + h:GATE_LANE0 + C_HEADS + h + 1]
        g2 = gate[:, GATE_LANE0 + 2 * C_HEADS + h:GATE_LANE0 + 2 * C_HEADS + h + 1]
        outs.append(g0 * o_cmp[h] + g1 * o_sel + g2 * o_win)
    o = jnp.concatenate(outs, axis=1)
    ms = jnp.mean(o * o, axis=-1, keepdims=True)
    o_ref[...] = o * lax.rsqrt(ms + NORM_EPS) * g_ref[...]


def _nsa_prompt_call(proj, kc, vc, bsz, t, out_g, tq=256):
    nq = t // tq
    n_cmp = (t - CMP_LEN) // CMP_STRIDE + 1
    c0 = COL_CKV // C_DHEAD
    w0 = COL_WIN // C_DHEAD
    kvspec = lambda col: pl.BlockSpec((t, C_DHEAD), lambda b, i: (b, col))
    return pl.pallas_call(
        functools.partial(_nsa_prompt_kernel, tq=tq, n_cmp=n_cmp),
        grid=(bsz, nq),
        in_specs=[
            pl.BlockSpec((tq, C_WIDTH), lambda b, i: (b * nq + i, COL_CQ // C_WIDTH)),
            kvspec(c0 + 2), kvspec(c0 + 3), kvspec(w0), kvspec(w0 + 1),
            pl.BlockSpec((1, kc.shape[1], C_DHEAD), lambda b, i: (b, 0, 0)),
            pl.BlockSpec((1, kc.shape[1], C_DHEAD), lambda b, i: (b, 0, 0)),
            pl.BlockSpec((tq, LANES), lambda b, i: (b * nq + i, COL_SMALL // LANES)),
            pl.BlockSpec((1, C_WIDTH), lambda b, i: (0, 0)),
        ],
        out_specs=pl.BlockSpec((tq, C_WIDTH), lambda b, i: (b * nq + i, 0)),
        out_shape=jax.ShapeDtypeStruct((bsz * t, C_WIDTH), F32),
        scratch_shapes=[pltpu.VMEM((tq, C_WIDTH), F32)],
        compiler_params=_cparams(("parallel", "arbitrary")),
        name="nsa_attn_prompt",
    )(proj, proj, proj, proj, proj, kc, vc, proj, out_g.reshape(1, C_WIDTH))


def _outproj_kernel(*refs, two):
    if two:
        xa_ref, xb_ref, oa_ref, ob_ref, oc_ref, w_ref, g_ref, h_ref, hnt_ref = refs
    else:
        xa_ref, oa_ref, ob_ref, oc_ref, w_ref, g_ref, h_ref, hnt_ref = refs
    x = xa_ref[...]
    if two:
        x = x + xb_ref[...]
    mixed = jnp.dot(oa_ref[...].astype(BF16), w_ref[0:A_WIDTH, :], preferred_element_type=F32)
    mixed = mixed + jnp.dot(ob_ref[...].astype(BF16), w_ref[A_WIDTH:A_WIDTH + B_WIDTH, :], preferred_element_type=F32)
    mixed = mixed + jnp.dot(oc_ref[...].astype(BF16), w_ref[A_WIDTH + B_WIDTH:, :], preferred_element_type=F32)
    h = x + mixed
    h_ref[...] = h
    ms = jnp.mean(h * h, axis=-1, keepdims=True)
    hnt_ref[...] = (h * lax.rsqrt(ms + NORM_EPS) * g_ref[...]).T.astype(BF16)


def _outproj_call(xa, xb, oa, ob, oc, w_out_b, g, tm):
    n = xa.shape[0]
    two = xb is not None
    row = lambda w: pl.BlockSpec((tm, w), lambda i: (i, 0))
    ins = [xa] + ([xb] if two else []) + [oa, ob, oc, w_out_b, g.reshape(1, D_MODEL)]
    specs = [row(D_MODEL)] + ([row(D_MODEL)] if two else []) + [
        row(A_WIDTH), row(B_WIDTH), row(C_WIDTH),
        pl.BlockSpec((D_MODEL, D_MODEL), lambda i: (0, 0)),
        pl.BlockSpec((1, D_MODEL), lambda i: (0, 0)),
    ]
    return pl.pallas_call(
        functools.partial(_outproj_kernel, two=two),
        grid=(n // tm,),
        in_specs=specs,
        out_specs=[row(D_MODEL), pl.BlockSpec((D_MODEL, tm), lambda i: (0, i))],
        out_shape=[jax.ShapeDtypeStruct((n, D_MODEL), F32), jax.ShapeDtypeStruct((D_MODEL, n), BF16)],
        compiler_params=_cparams(("parallel",)),
        name="out_proj",
    )(*ins)


def _peer_q_kernel(wqt_ref, hnt_ref, qt_ref):
    qt_ref[...] = jnp.dot(wqt_ref[...], hnt_ref[...], preferred_element_type=F32)


def _peer_q_call(wq_t, hn_t, tm):
    n = hn_t.shape[1]
    dq = wq_t.shape[0]
    return pl.pallas_call(
        _peer_q_kernel,
        grid=(n // tm,),
        in_specs=[pl.BlockSpec((dq, D_MODEL), lambda i: (0, 0)), pl.BlockSpec((D_MODEL, tm), lambda i: (0, i))],
        out_specs=pl.BlockSpec((dq, tm), lambda i: (0, i)),
        out_shape=jax.ShapeDtypeStruct((dq, n), F32),
        compiler_params=_cparams(("parallel",)),
        name="peer_query",
    )(wq_t, hn_t)


NOT_RANKED = 99.0


def _top_rows(s, pos, k, want_rank):
    rank = jnp.full(s.shape, NOT_RANKED, F32) if want_rank else None
    vals, picks = [], []
    for j in range(k):
        m = jnp.max(s, axis=0, keepdims=True)
        idx = jnp.min(jnp.where(s == m, pos, 1e9), axis=0, keepdims=True)
        hit = pos == idx
        if want_rank:
            rank = jnp.where(hit, float(j), rank)
        s = jnp.where(hit, -jnp.inf, s)
        vals.append(m)
        picks.append(idx)
    return jnp.concatenate(vals, axis=0), jnp.concatenate(picks, axis=0), rank


PAIR_ROWS = PEER_TOPK + 7 * 8 + 8


def _pair_candidates(v1, v2):
    tn = v1.shape[1]
    parts = [v1[0:1, :] + v2] + [v1[a:a + 1, :] + v2[0:8, :] for a in range(1, 8)] + [v1[8:16, :] + v2[0:1, :]]
    r = lax.broadcasted_iota(jnp.int32, (PAIR_ROWS, tn), 0)
    mid = r - PEER_TOPK
    pos = jnp.where(r < PEER_TOPK, r,
                    jnp.where(r < PEER_TOPK + 56, ((mid >> 3) + 1) * PEER_TOPK + (mid & 7), (r - 64) * PEER_TOPK))
    return jnp.concatenate(parts, axis=0), pos.astype(F32)


def _peer_route_kernel(qt_ref, sk_ref, lim_ref, coef_ref, rank2_ref, e2_ref, *, heads):
    half = PEER_DQ // 2
    row = lax.broadcasted_iota(jnp.int32, (PEER_NKEYS, LANES), 0).astype(F32)
    for hh, tb in [(hh, tb) for hh in range(heads) for tb in range(qt_ref.shape[1] // LANES)]:
        cols = slice(tb * LANES, (tb + 1) * LANES)
        q = qt_ref[hh * PEER_DQ:(hh + 1) * PEER_DQ, cols]
        s1 = jnp.dot(sk_ref[hh, 0], q[0:half, :], preferred_element_type=F32, precision=lax.Precision.HIGHEST)
        s2 = jnp.dot(sk_ref[hh, 1], q[half:, :], preferred_element_type=F32, precision=lax.Precision.HIGHEST)
        v1, _, rank1 = _top_rows(s1, row, PEER_TOPK, True)
        v2, _, rank2 = _top_rows(s2, row, PEER_TOPK, True)
        cand, cpos = _pair_candidates(v1, v2)
        top, pos, _ = _top_rows(cand, cpos, PEER_TOPK, False)
        z = jnp.sum(jnp.exp(top - top[0:1, :]), axis=0, keepdims=True)
        a_of = jnp.floor(pos * (1.0 / PEER_TOPK))
        lim = jnp.zeros(s1.shape, F32)
        for a in range(PEER_TOPK):
            cnt = jnp.sum(jnp.where(a_of == float(a), 1.0, 0.0), axis=0, keepdims=True)
            lim = jnp.where(rank1 == float(a), cnt, lim)
        lim_ref[hh, :, cols] = lim
        coef_ref[hh, :, cols] = jnp.exp(s1 - v1[0:1, :]) / z
        rank2_ref[hh, :, cols] = rank2.astype(BF16)
        e2_ref[hh, :, cols] = jnp.exp(s2 - v2[0:1, :]).astype(BF16)


def _peer_route_call(q_t, subkeys, tn=2 * LANES, heads=4):
    n = q_t.shape[1]
    tn = min(tn, n)
    shp = lambda dt: jax.ShapeDtypeStruct((PEER_HEADS, PEER_NKEYS, n), dt)
    ospec = pl.BlockSpec((heads, PEER_NKEYS, tn), lambda j, h: (h, 0, j))
    return pl.pallas_call(
        functools.partial(_peer_route_kernel, heads=heads),
        grid=(n // tn, PEER_HEADS // heads),
        in_specs=[
            pl.BlockSpec((heads * PEER_DQ, tn), lambda j, h: (h, j)),
            pl.BlockSpec((heads, 2, PEER_NKEYS, PEER_DQ // 2), lambda j, h: (h, 0, 0, 0)),
        ],
        out_specs=[ospec] * 4,
        out_shape=[shp(F32), shp(F32), shp(BF16), shp(BF16)],
        compiler_params=_cparams(("parallel", "arbitrary")),
        name="peer_route",
    )(q_t, subkeys)


def _peer_expert_kernel(hnt_ref, lim_ref, coef_ref, rank2_ref, e2_ref, u_ref, v_ref, o_ref, *rest, et, emit):
    acc_ref = rest[-1]
    t = pl.program_id(1)

    @pl.when(t == 0)
    def _():
        acc_ref[...] = jnp.zeros_like(acc_ref)

    if emit:
        ub = u_ref[...].astype(BF16)
        vtb = v_ref[...].T.astype(BF16)
        rest[0][...] = ub
        rest[1][...] = vtb
    else:
        ub = u_ref[...]
        vtb = v_ref[...]
    tn = hnt_ref.shape[1]
    hid = jnp.dot(ub, hnt_ref[...], preferred_element_type=F32)
    acts = []
    for ii in range(et // PEER_NKEYS):
        i1 = t * (et // PEER_NKEYS) + ii
        gate = jnp.zeros((PEER_NKEYS, tn), BF16)
        for h in range(PEER_HEADS):
            lim = lim_ref[h, pl.ds(i1, 1), :].astype(BF16)
            coef = coef_ref[h, pl.ds(i1, 1), :].astype(BF16)
            gate = gate + jnp.where(rank2_ref[h] < lim, e2_ref[h], jnp.zeros((), BF16)) * coef
        acts.append(gate * _gelu(hid[ii * PEER_NKEYS:(ii + 1) * PEER_NKEYS, :]).astype(BF16))
    acc_ref[...] += jnp.dot(vtb, jnp.concatenate(acts, axis=0), preferred_element_type=F32)

    @pl.when(t == pl.num_programs(1) - 1)
    def _():
        o_ref[...] = acc_ref[...].T


def _peer_expert_call(hn_t, route, u_tab, v_tab, tn, et, emit=False, layer=0):
    n = hn_t.shape[1]
    n_exp = u_tab.shape[-2] if emit else u_tab.shape[0]
    rspec = pl.BlockSpec((PEER_HEADS, PEER_NKEYS, tn), lambda j, t: (0, 0, j))
    uspec = pl.BlockSpec((et, D_MODEL), lambda j, t: (t, 0))
    vtspec = pl.BlockSpec((D_MODEL, et), lambda j, t: (0, t))
    ospec = pl.BlockSpec((tn, D_MODEL), lambda j, t: (j, 0))
    oshape = jax.ShapeDtypeStruct((n, D_MODEL), F32)
    if emit:
        assert n == tn
        out_specs = [ospec, uspec, vtspec]
        out_shape = [oshape, jax.ShapeDtypeStruct((n_exp, D_MODEL), BF16), jax.ShapeDtypeStruct((D_MODEL, n_exp), BF16)]
    else:
        out_specs, out_shape = ospec, oshape
    return pl.pallas_call(
        functools.partial(_peer_expert_kernel, et=et, emit=emit),
        grid=(n // tn, n_exp // et),
        in_specs=[pl.BlockSpec((D_MODEL, tn), lambda j, t: (0, j)), rspec, rspec, rspec, rspec]
        + ([pl.BlockSpec((None, et, D_MODEL), lambda j, t: (layer, t, 0))] * 2 if emit else [uspec, vtspec]),
        out_specs=out_specs,
        out_shape=out_shape,
        scratch_shapes=[pltpu.VMEM((D_MODEL, tn), F32)],
        compiler_params=_cparams(("parallel", "arbitrary")),
        name="peer_experts",
    )(hn_t, *route, u_tab, v_tab)


def _final_norm_kernel(xa_ref, xb_ref, g_ref, o_ref):
    x = xa_ref[...] + xb_ref[...]
    ms = jnp.mean(x * x, axis=-1, keepdims=True)
    o_ref[...] = x * lax.rsqrt(ms + NORM_EPS) * g_ref[...]


def _final_norm_call(xa, xb, g, tm):
    n = xa.shape[0]
    row = pl.BlockSpec((tm, D_MODEL), lambda i: (i, 0))
    return pl.pallas_call(
        _final_norm_kernel,
        grid=(n // tm,),
        in_specs=[row, row, pl.BlockSpec((1, D_MODEL), lambda i: (0, 0))],
        out_specs=row,
        out_shape=jax.ShapeDtypeStruct((n, D_MODEL), F32),
        compiler_params=_cparams(("parallel",)),
        name="final_norm",
    )(xa, xb, g.reshape(1, D_MODEL))


DIFF_PAGES = 32


def _diff_sample_kernel(pt_ref, q_ref, knew_ref, vnew_ref, dl_ref, g_ref, *rest, past_len, lam_init):
    page_refs = [_row_view(r, PAGE_SIZE * 2 * A_HEADS) for r in rest[:DIFF_PAGES]]
    o_ref, m_ref, l_ref, acc_ref = rest[DIFF_PAGES:]
    p = pl.program_id(1)
    scale = A_HALF ** -0.5
    nrow = 2 * A_HEADS
    per_key = 2 * A_HEADS

    @pl.when(p == 0)
    def _():
        m_ref[...] = jnp.full(m_ref.shape, NEG_INF, F32)
        l_ref[...] = jnp.zeros(l_ref.shape, F32)
        acc_ref[...] = jnp.zeros(acc_ref.shape, F32)

    row = lax.broadcasted_iota(jnp.int32, (nrow, 1), 0)
    lane = lax.broadcasted_iota(jnp.int32, (1, A_DHEAD), 1)
    slope = jnp.exp2(-2.0 * ((row >> 1) + 1).astype(F32))
    q = q_ref[0]
    q2 = [jnp.where(((row >> 1) == h) & ((lane >= A_HALF) == ((row & 1) == 1)), q[:, h * A_DHEAD:(h + 1) * A_DHEAD], 0.0)
          for h in range(A_HEADS)]
    q2b = [x.astype(BF16) for x in q2]
    ss = []
    for g in range(DIFF_PAGES):
        sg = None
        for h in range(A_HEADS):
            kh = page_refs[g][pl.ds(h, PAGE_SIZE, stride=per_key), :].astype(BF16)
            d = lax.dot_general(q2b[h], kh, NT_DIMS, preferred_element_type=F32)
            sg = d if sg is None else sg + d
        ss.append(sg)
    s = jnp.concatenate(ss, axis=1) * scale
    nk = DIFF_PAGES * PAGE_SIZE
    kpos = p * nk + lax.broadcasted_iota(jnp.int32, (1, nk), 1)
    s = s - slope * (past_len - kpos).astype(F32)
    m_old = m_ref[:, 0:1]
    m_new = jnp.maximum(m_old, jnp.max(s, axis=-1, keepdims=True))
    alpha = jnp.exp(m_old - m_new)
    e = jnp.exp(s - m_new)
    eb = e.astype(BF16)
    l_new = alpha * l_ref[:, 0:1] + jnp.sum(e, axis=-1, keepdims=True)
    pv = jnp.zeros((nrow, A_DHEAD), F32)
    for h in range(A_HEADS):
        vh = jnp.concatenate([page_refs[g][pl.ds(A_HEADS + h, PAGE_SIZE, stride=per_key), :].astype(BF16)
                              for g in range(DIFF_PAGES)], axis=0)
        pv = pv + jnp.where((row >> 1) == h, jnp.dot(eb, vh, preferred_element_type=F32), 0.0)
    acc = alpha * acc_ref[...] + pv
    m_ref[...] = jnp.broadcast_to(m_new, m_ref.shape)
    l_ref[...] = jnp.broadcast_to(l_new, l_ref.shape)
    acc_ref[...] = acc

    @pl.when(p == pl.num_programs(1) - 1)
    def _():
        knew = knew_ref[0]
        vnew = vnew_ref[0]
        s_n = jnp.zeros((nrow, 1), F32)
        for h in range(A_HEADS):
            s_n = s_n + jnp.sum(q2[h] * knew[:, h * A_DHEAD:(h + 1) * A_DHEAD], axis=-1, keepdims=True)
        s_n = s_n * scale
        v8 = jnp.concatenate([vnew[:, (r // 2) * A_DHEAD:(r // 2 + 1) * A_DHEAD] for r in range(nrow)], axis=0)
        m_f = jnp.maximum(m_new, s_n)
        a_f = jnp.exp(m_new - m_f)
        e_n = jnp.exp(s_n - m_f)
        o8 = (a_f * acc + e_n * v8) / (a_f * l_new + e_n)
        lam = _diff_lambda(dl_ref[...], lam_init)
        outs = []
        for h in range(A_HEADS):
            oh = o8[2 * h:2 * h + 1, :] - lam * o8[2 * h + 1:2 * h + 2, :]
            ms = jnp.mean(oh * oh, axis=-1, keepdims=True)
            outs.append(oh * lax.rsqrt(ms + NORM_EPS) * g_ref[...] * (1.0 - lam_init))
        o_ref[0] = jnp.concatenate(outs, axis=1)


def _diff_sample_call(page_table, proj3, cache, layer, dl, subln_g, lam_init):
    bs, n_pages = page_table.shape
    past_len = n_pages * PAGE_SIZE
    steps = n_pages // DIFF_PAGES

    def page_spec(g):
        return pl.BlockSpec((None, None, PAGE_SIZE, 2, A_HEADS, A_DHEAD),
                            lambda b, p, pt: (layer, pt[b, p * DIFF_PAGES + g], 0, 0, 0, 0))

    grid_spec = pltpu.PrefetchScalarGridSpec(
        num_scalar_prefetch=1,
        grid=(bs, steps),
        in_specs=[
            pl.BlockSpec((1, 1, A_WIDTH), lambda b, p, pt: (b, 0, COL_AQ // A_WIDTH)),
            pl.BlockSpec((1, 1, A_WIDTH), lambda b, p, pt: (b, 0, COL_AK // A_WIDTH)),
            pl.BlockSpec((1, 1, A_WIDTH), lambda b, p, pt: (b, 0, COL_AV // A_WIDTH)),
            pl.BlockSpec((4, A_HALF), lambda b, p, pt: (0, 0)),
            pl.BlockSpec((1, A_DHEAD), lambda b, p, pt: (0, 0)),
        ] + [page_spec(g) for g in range(DIFF_PAGES)],
        out_specs=pl.BlockSpec((1, 1, A_WIDTH), lambda b, p, pt: (b, 0, 0)),
        scratch_shapes=[pltpu.VMEM((2 * A_HEADS, LANES), F32), pltpu.VMEM((2 * A_HEADS, LANES), F32),
                        pltpu.VMEM((2 * A_HEADS, A_DHEAD), F32)],
    )
    return pl.pallas_call(
        functools.partial(_diff_sample_kernel, past_len=past_len, lam_init=lam_init),
        grid_spec=grid_spec,
        out_shape=jax.ShapeDtypeStruct((bs, 1, A_WIDTH), F32),
        compiler_params=_cparams(("parallel", "arbitrary")),
        name="diff_attn_sample",
    )(page_table, proj3, proj3, proj3, dl, subln_g.reshape(1, A_DHEAD), *([cache] * DIFF_PAGES))


def _diag_rows(vec):
    n = vec.shape[1]
    r = lax.broadcasted_iota(jnp.int32, (n, n), 0)
    c = lax.broadcasted_iota(jnp.int32, (n, n), 1)
    return jnp.where(r == c, vec, 0.0)


def _ssd_sample_kernel(*refs):
    nx, nz = B_CONV_DIM // SSD_COLS, B_WIDTH // SSD_COLS
    xbc_refs, z_refs = refs[:nx], refs[nx:nx + nz]
    (sm_ref, cbuf_ref, h0_ref, cw_ref, cb_ref, dtb_ref, alog_ref, dsk_ref, g_ref,
     o_ref, hout_ref, cout_ref) = refs[nx + nz:]
    hi = lax.Precision.HIGHEST
    new = jnp.concatenate([r[0] for r in xbc_refs], axis=1)
    buf = cbuf_ref[0, 0]
    cw = cw_ref[...]
    conv = cb_ref[...] + cw[CONV_W - 1:CONV_W] * new
    for i in range(CONV_W - 1):
        conv = conv + cw[i:i + 1] * buf[i:i + 1]
    cout_ref[0] = jnp.concatenate([buf[1:CONV_W - 1], new], axis=0)
    xc = _silu(conv)
    xs = xc[:, :B_WIDTH]
    dt = _softplus(sm_ref[0] + dtb_ref[...])
    ea = jnp.exp(dt * (-jnp.exp(alog_ref[...])))
    hr = lax.broadcasted_iota(jnp.int32, (LANES, B_WIDTH), 0)
    hc = lax.broadcasted_iota(jnp.int32, (LANES, B_WIDTH), 1)
    rep = jnp.where((hc // B_HEADDIM) == hr, 1.0, 0.0)
    both = jnp.concatenate([dt, ea, jnp.zeros((6, LANES), F32)], axis=0)
    both_rep = jnp.dot(both, rep, preferred_element_type=F32, precision=hi)
    u = both_rep[0:1] * xs
    ea_rep = both_rep[1:2]
    gn = B_GROUPS * B_STATE
    rows = (B_HEADS // B_GROUPS) * B_HEADDIM
    ys = []
    for g in range(B_GROUPS):
        r0 = g * rows
        bg = xc[:, B_WIDTH + g * B_STATE:B_WIDTH + (g + 1) * B_STATE]
        cg = xc[:, B_WIDTH + gn + g * B_STATE:B_WIDTH + gn + (g + 1) * B_STATE]
        h0 = h0_ref[0, 0, r0:r0 + rows, :]
        hn = jnp.dot(_diag_rows(ea_rep[:, r0:r0 + rows]), h0, preferred_element_type=F32, precision=hi)
        hn = hn + jnp.dot(_diag_rows(u[:, r0:r0 + rows]), jnp.broadcast_to(bg, (rows, B_STATE)),
                          preferred_element_type=F32, precision=hi)
        hout_ref[0, r0:r0 + rows, :] = hn
        c8 = jnp.broadcast_to(cg, (8, B_STATE)).astype(BF16)
        ys.append(lax.dot_general(c8, hn.astype(BF16), NT_DIMS, preferred_element_type=F32)[0:1])
    y = jnp.concatenate(ys, axis=1) + dsk_ref[...] * xs
    y = y * _silu(jnp.concatenate([r[0] for r in z_refs], axis=1))
    ms = jnp.mean(y * y, axis=-1, keepdims=True)
    o_ref[0] = y * lax.rsqrt(ms + NORM_EPS) * g_ref[...]


def _ssd_sample_call(proj3, state_conv, state_ssm4, layer, conv_w, conv_b, dt_bias, a_log, d_skip, norm_g):
    bs = proj3.shape[0]
    const = lambda b: (0, 0)
    nrow = B_HEADS * B_HEADDIM
    return pl.pallas_call(
        _ssd_sample_kernel,
        grid=(bs,),
        in_specs=[
            *[pl.BlockSpec((1, 1, SSD_COLS), functools.partial(lambda k, b: (b, 0, COL_XBC // SSD_COLS + k), k))
              for k in range(B_CONV_DIM // SSD_COLS)],
            *[pl.BlockSpec((1, 1, SSD_COLS), functools.partial(lambda k, b: (b, 0, COL_Z // SSD_COLS + k), k))
              for k in range(B_WIDTH // SSD_COLS)],
            pl.BlockSpec((1, 1, LANES), lambda b: (b, 0, COL_SMALL // LANES)),
            pl.BlockSpec((1, 1, CONV_W - 1, B_CONV_DIM), lambda b: (layer, b, 0, 0)),
            pl.BlockSpec((1, 1, nrow, B_STATE), lambda b: (layer, b, 0, 0)),
            pl.BlockSpec((CONV_W, B_CONV_DIM), const),
            pl.BlockSpec((1, B_CONV_DIM), const),
            pl.BlockSpec((1, LANES), const),
            pl.BlockSpec((1, LANES), const),
            pl.BlockSpec((1, B_WIDTH), const),
            pl.BlockSpec((1, B_WIDTH), const),
        ],
        out_specs=[
            pl.BlockSpec((1, 1, B_WIDTH), lambda b: (b, 0, 0)),
            pl.BlockSpec((1, nrow, B_STATE), lambda b: (b, 0, 0)),
            pl.BlockSpec((1, CONV_W - 1, B_CONV_DIM), lambda b: (b, 0, 0)),
        ],
        out_shape=[
            jax.ShapeDtypeStruct((bs, 1, B_WIDTH), F32),
            jax.ShapeDtypeStruct((bs, nrow, B_STATE), F32),
            jax.ShapeDtypeStruct((bs, CONV_W - 1, B_CONV_DIM), F32),
        ],
        compiler_params=_cparams(("parallel",)),
        name="ssd_sample",
    )(*([proj3] * (B_CONV_DIM // SSD_COLS + B_WIDTH // SSD_COLS + 1)), state_conv, state_ssm4, conv_w,
      conv_b.reshape(1, -1), _pad_lanes(dt_bias),
      _pad_lanes(a_log), jnp.repeat(d_skip, B_HEADDIM).reshape(1, B_WIDTH), norm_g.reshape(1, B_WIDTH))


CMP_PAGES = 32


def _compress_paged_kernel(pt_ref, pe_ref, w1_ref, *rest):
    page_refs = [_row_view(r, PAGE_SIZE * NSA_ROWS) for r in rest[:CMP_PAGES]]
    o_ref = rest[CMP_PAGES]
    per_page = PAGE_SIZE // CMP_STRIDE
    outs = []
    for kv in range(2):
        def load_rows(r, kv=kv):
            return jnp.concatenate([page_refs[g][pl.ds(NSA_ROWS * r + kv, per_page, stride=NSA_ROWS * CMP_STRIDE), :]
                                    for g in range(CMP_PAGES)], axis=0)
        outs += list(_compress_partials(load_rows, pe_ref, w1_ref, kv, CMP_PAGES * per_page))
    o_ref[0] = jnp.concatenate(outs, axis=1)


def _compress_paged_call(page_table, cache, layer, pe, w1):
    bs, n_pages = page_table.shape
    steps = n_pages // CMP_PAGES
    per_page = PAGE_SIZE // CMP_STRIDE

    def page_spec(g):
        return pl.BlockSpec((None, None, PAGE_SIZE, NSA_ROWS, C_DHEAD),
                            lambda b, p, pt: (layer, pt[b, p * CMP_PAGES + g], 0, 0, 0))

    grid_spec = pltpu.PrefetchScalarGridSpec(
        num_scalar_prefetch=1,
        grid=(bs, steps),
        in_specs=[
            pl.BlockSpec((2, CMP_LEN, C_DHEAD), lambda b, p, pt: (0, 0, 0)),
            pl.BlockSpec((2, CMP_LEN, C_DHEAD, C_DHEAD), lambda b, p, pt: (0, 0, 0, 0)),
        ] + [page_spec(g) for g in range(CMP_PAGES)],
        out_specs=pl.BlockSpec((1, CMP_PAGES * per_page, 4 * C_DHEAD), lambda b, p, pt: (b, p, 0)),
    )
    return pl.pallas_call(
        _compress_paged_kernel,
        grid_spec=grid_spec,
        out_shape=jax.ShapeDtypeStruct((bs, n_pages * per_page, 4 * C_DHEAD), F32),
        compiler_params=_cparams(("parallel", "arbitrary")),
        name="nsa_compress_sample",
    )(page_table, pe, w1.reshape(2, CMP_LEN, C_DHEAD, C_DHEAD), *([cache] * CMP_PAGES))


def _heads_to_rows(q):
    rows = [q[:, h * C_DHEAD:(h + 1) * C_DHEAD] for h in range(C_HEADS)]
    return jnp.concatenate(rows + [jnp.zeros((8 - C_HEADS, C_DHEAD), F32)], axis=0)


SEL_LANES = 384


def _nsa_select_kernel(part_ref, w2_ref, q_ref, ocmp_ref, sel_ref, *, q_pos):
    nchunk = part_ref.shape[1]
    n_cmp = (q_pos + 1 - CMP_LEN) // CMP_STRIDE + 1
    n_sel = -(-(q_pos + 1) // SEL_BLOCK)
    scale = C_DHEAD ** -0.5
    part = part_ref[0]
    kv_cmp = []
    for kv in range(2):
        lo = part[:, (2 * kv) * C_DHEAD:(2 * kv + 1) * C_DHEAD]
        hi = part[:, (2 * kv + 1) * C_DHEAD:(2 * kv + 2) * C_DHEAD]
        hid = lo + pltpu.roll(hi, nchunk - 1, 0)
        kv_cmp.append(jnp.dot(_gelu(hid).astype(BF16), w2_ref[kv].astype(BF16), preferred_element_type=F32).astype(BF16))
    q8 = _heads_to_rows(q_ref[0]).astype(BF16)
    row = lax.broadcasted_iota(jnp.int32, (8, 1), 0)
    slope = jnp.exp2(-2.0 * (row + 1).astype(F32))
    n_i = lax.broadcasted_iota(jnp.int32, (1, nchunk), 1)
    dist_c = q_pos - (n_i * CMP_STRIDE + CMP_LEN - 1)
    ok = (dist_c >= 0) & (n_i < n_cmp)
    s = lax.dot_general(q8, kv_cmp[0], NT_DIMS, preferred_element_type=F32) * scale - slope * dist_c.astype(F32)
    e, den = _masked_softmax(s, ok)
    p = jnp.where(row < C_HEADS, e / jnp.maximum(den, 1e-30), 0.0)
    ocmp_ref[0] = jnp.dot(p.astype(BF16), kv_cmp[1], preferred_element_type=F32)
    psum = jnp.broadcast_to(jnp.sum(p, axis=0, keepdims=True), (8, nchunk))
    c_i = lax.broadcasted_iota(jnp.int32, (nchunk, SEL_LANES), 0)
    j_i = lax.broadcasted_iota(jnp.int32, (nchunk, SEL_LANES), 1)
    lo_ = jnp.maximum(c_i * CMP_STRIDE, j_i * SEL_BLOCK)
    hi_ = jnp.minimum(c_i * CMP_STRIDE + CMP_LEN, (j_i + 1) * SEL_BLOCK)
    ovl = jnp.where((c_i < n_cmp) & (j_i < n_sel), jnp.maximum(hi_ - lo_, 0).astype(F32) * (1.0 / CMP_LEN), 0.0)
    imp = jnp.dot(psum, ovl, preferred_element_type=F32, precision=lax.Precision.HIGHEST)[0:1]
    lane = lax.broadcasted_iota(jnp.int32, (1, SEL_LANES), 1)
    qblk = q_pos // SEL_BLOCK
    forced = (lane == 0) | (lane == qblk) | (lane == qblk - 1)
    score = jnp.where(lane <= qblk, imp + jnp.where(forced, FORCE_SCORE, 0.0), NEG_INF)
    score = jnp.where(lane < n_sel, score, -jnp.inf)
    lane_f = lane.astype(F32)
    out_lane = lax.broadcasted_iota(jnp.int32, (1, LANES), 1)
    sel = jnp.full((1, LANES), -1.0, F32)
    for k in range(min(SEL_TOPK, n_sel)):
        m = jnp.max(score, axis=-1, keepdims=True)
        idx = jnp.min(jnp.where(score == m, lane_f, 1e9), axis=-1, keepdims=True)
        sel = jnp.where(out_lane == k, jnp.where(m > NEG_INF / 2, idx, -1.0), sel)
        score = jnp.where(lane_f == idx, -jnp.inf, score)
    sel_ref[0] = sel.astype(jnp.int32)


def _nsa_select_call(part, w2, q3, q_pos):
    bs, nchunk, _ = part.shape
    return pl.pallas_call(
        functools.partial(_nsa_select_kernel, q_pos=q_pos),
        grid=(bs,),
        in_specs=[
            pl.BlockSpec((1, nchunk, 4 * C_DHEAD), lambda b: (b, 0, 0)),
            pl.BlockSpec((2, C_DHEAD, C_DHEAD), lambda b: (0, 0, 0)),
            pl.BlockSpec((1, 1, C_WIDTH), lambda b: (b, 0, COL_CQ // C_WIDTH)),
        ],
        out_specs=[pl.BlockSpec((1, 8, C_DHEAD), lambda b: (b, 0, 0)), pl.BlockSpec((1, 1, LANES), lambda b: (b, 0, 0))],
        out_shape=[jax.ShapeDtypeStruct((bs, 8, C_DHEAD), F32), jax.ShapeDtypeStruct((bs, 1, LANES), jnp.int32)],
        compiler_params=_cparams(("parallel",)),
        name="nsa_select_sample",
    )(part, w2, q3)


def _nsa_attend_kernel(sel_ref, pt_ref, q_ref, new_ref, wnew_ref, sm_ref, ocmp_ref, win_ref, g_ref, *rest, q_pos):
    k_eff = SEL_TOPK
    blk_refs = [_row_view(r, SEL_BLOCK * NSA_ROWS) for r in rest[:k_eff]]
    o_ref, wout_ref = rest[k_eff:]
    b = pl.program_id(0)
    scale = C_DHEAD ** -0.5
    n_past_blocks = q_pos // SEL_BLOCK
    q8f = _heads_to_rows(q_ref[0])
    q8 = q8f.astype(BF16)
    row = lax.broadcasted_iota(jnp.int32, (8, 1), 0)
    slope = jnp.exp2(-2.0 * (row + 1).astype(F32))
    lane64 = lax.broadcasted_iota(jnp.int32, (1, SEL_BLOCK), 1)

    ss, vs, oks = [], [], []
    new_sel = jnp.zeros((1, 1), jnp.int32)
    for k in range(k_eff):
        j = sel_ref[b, k]
        k_sel = blk_refs[k][pl.ds(2, SEL_BLOCK, stride=NSA_ROWS), :]
        v_sel = blk_refs[k][pl.ds(3, SEL_BLOCK, stride=NSA_ROWS), :]
        s = lax.dot_general(q8, k_sel.astype(BF16), NT_DIMS, preferred_element_type=F32)
        dist = q_pos - (j * SEL_BLOCK + lane64)
        ss.append(s * scale - slope * dist.astype(F32))
        oks.append(lane64 * 0 + jnp.where((j >= 0) & (j < n_past_blocks), 1, 0))
        vs.append(v_sel.astype(BF16))
        new_sel = new_sel + jnp.where(j == n_past_blocks, 1, 0)
    s = jnp.concatenate(ss, axis=1)
    ok = jnp.concatenate(oks, axis=1) > 0
    new = new_ref[0]
    s_n = jnp.sum(q8f * new[:, 2 * C_DHEAD:3 * C_DHEAD], axis=-1, keepdims=True) * scale
    s_n = jnp.where(new_sel > 0, s_n, NEG_INF)
    s = jnp.where(ok, s, NEG_INF)
    m = jnp.maximum(jnp.max(s, axis=-1, keepdims=True), s_n)
    e = jnp.where(ok, jnp.exp(s - m), 0.0)
    e_n = jnp.where(new_sel > 0, jnp.exp(s_n - m), 0.0)
    den = jnp.sum(e, axis=-1, keepdims=True) + e_n
    o_sel = (jnp.dot(e.astype(BF16), jnp.concatenate(vs, axis=0), preferred_element_type=F32)
             + e_n * new[:, 3 * C_DHEAD:]) / den

    lw = win_ref.shape[0]
    win_k = win_ref[:, 0, :]
    win_v = win_ref[:, 1, :]
    wnew = wnew_ref[0]
    wpos = lax.broadcasted_iota(jnp.int32, (1, lw), 1)
    dist_w = lw - wpos
    ok_w = dist_w < WINDOW
    s = lax.dot_general(q8, win_k.astype(BF16), NT_DIMS, preferred_element_type=F32) * scale
    s = jnp.where(ok_w, s - slope * dist_w.astype(F32), NEG_INF)
    s_n = jnp.sum(q8f * wnew[:, :C_DHEAD], axis=-1, keepdims=True) * scale
    m = jnp.maximum(jnp.max(s, axis=-1, keepdims=True), s_n)
    e = jnp.where(ok_w, jnp.exp(s - m), 0.0)
    e_n = jnp.exp(s_n - m)
    den = jnp.sum(e, axis=-1, keepdims=True) + e_n
    o_win = (jnp.dot(e.astype(BF16), win_v.astype(BF16), preferred_element_type=F32)
             + e_n * wnew[:, C_DHEAD:]) / den
    keep = min(WINDOW, lw + 1)
    wout_ref[0:keep - 1, :, :] = win_ref[lw + 1 - keep:lw, :, :]
    wout_ref[keep - 1:keep, 0, :] = wnew[:, :C_DHEAD]
    wout_ref[keep - 1:keep, 1, :] = wnew[:, C_DHEAD:]

    gate = 1.0 / (1.0 + jnp.exp(-sm_ref[0]))
    outs = []
    for h in range(C_HEADS):
        g0 = gate[:, GATE_LANE0 + h:GATE_LANE0 + h + 1]
        g1 = gate[:, GATE_LANE0 + C_HEADS + h:GATE_LANE0 + C_HEADS + h + 1]
        g2 = gate[:, GATE_LANE0 + 2 * C_HEADS + h:GATE_LANE0 + 2 * C_HEADS + h + 1]
        outs.append(g0 * ocmp_ref[0, h:h + 1, :] + g1 * o_sel[h:h + 1, :] + g2 * o_win[h:h + 1, :])
    o = jnp.concatenate(outs, axis=1)
    ms = jnp.mean(o * o, axis=-1, keepdims=True)
    o_ref[0] = o * lax.rsqrt(ms + NORM_EPS) * g_ref[...]


def _nsa_attend_call(sel, page_table, proj3, ocmp, cache, cache_win, layer, out_g, q_pos):
    bs = proj3.shape[0]
    lw = cache_win.shape[2]
    keep = min(WINDOW, lw + 1)
    n_pages = page_table.shape[1]
    halves = PAGE_SIZE // SEL_BLOCK

    def blk_spec(k):
        def imap(b, sel_r, pt_r):
            j = jnp.clip(sel_r[b, k], 0, n_pages * halves - 1)
            return (layer, pt_r[b, j // halves], j % halves, 0, 0)
        return pl.BlockSpec((None, None, SEL_BLOCK, NSA_ROWS, C_DHEAD), imap)

    row3 = lambda w, col: pl.BlockSpec((1, 1, w), lambda b, s_, p_: (b, 0, col))
    grid_spec = pltpu.PrefetchScalarGridSpec(
        num_scalar_prefetch=2,
        grid=(bs,),
        in_specs=[
            row3(C_WIDTH, COL_CQ // C_WIDTH),
            row3(4 * C_DHEAD, COL_CKV // (4 * C_DHEAD)),
            row3(2 * C_DHEAD, COL_WIN // (2 * C_DHEAD)),
            row3(LANES, COL_SMALL // LANES),
            pl.BlockSpec((1, 8, C_DHEAD), lambda b, s_, p_: (b, 0, 0)),
            pl.BlockSpec((None, None, lw, 2, C_DHEAD), lambda b, s_, p_: (layer, b, 0, 0, 0)),
            pl.BlockSpec((1, C_WIDTH), lambda b, s_, p_: (0, 0)),
        ] + [blk_spec(k) for k in range(SEL_TOPK)],
        out_specs=[
            pl.BlockSpec((1, 1, C_WIDTH), lambda b, s_, p_: (b, 0, 0)),
            pl.BlockSpec((None, keep, 2, C_DHEAD), lambda b, s_, p_: (b, 0, 0, 0)),
        ],
    )
    return pl.pallas_call(
        functools.partial(_nsa_attend_kernel, q_pos=q_pos),
        grid_spec=grid_spec,
        out_shape=[jax.ShapeDtypeStruct((bs, 1, C_WIDTH), F32), jax.ShapeDtypeStruct((bs, keep, 2, C_DHEAD), F32)],
        compiler_params=_cparams(("arbitrary",)),
        name="nsa_attend_sample",
    )(sel, page_table, proj3, proj3, proj3, proj3, ocmp, cache_win, out_g.reshape(1, C_WIDTH),
      *([cache] * SEL_TOPK))


def _token_mixer_tail(xa, xb, oa, ob, oc, w_out_b, ffn_g, wq_t, subkeys, u_tab, v_tab, tm, tn, et, emit_bf16=False,
                      layer=0):
    h, hn_t = _outproj_call(xa, xb, oa, ob, oc, w_out_b, ffn_g, tm)
    q_t = _peer_q_call(wq_t, hn_t, tm)
    route = _peer_route_call(q_t, subkeys)
    return h, _peer_expert_call(hn_t, route, u_tab, v_tab, tn, et, emit_bf16, layer)


def kernel(x_prompt, x_sample, cache_diff_kv, cache_nsa_kv, cache_nsa_win, state_ssm, state_conv, page_table,
           norm_mix_g, w_in, w_out, diff_lam, diff_subln_g, ssm_conv_w, ssm_conv_b, ssm_dt_bias, ssm_a_log,
           ssm_d, ssm_norm_g, nsa_pe, nsa_cmp_w1, nsa_cmp_w2, nsa_out_g, norm_ffn_g, peer_wq, peer_subkeys,
           peer_u, peer_v, norm_final_g):
    depth = w_in.shape[0]
    bp, t, _ = x_prompt.shape
    bs = x_sample.shape[0]
    past_len = page_table.shape[1] * PAGE_SIZE
    n_p = bp * t
    n_s = LANES
    tm_p = 512

    xa_p, xb_p = x_prompt.reshape(n_p, D_MODEL), None
    xa_s = jnp.pad(x_sample.reshape(bs, D_MODEL), ((0, n_s - bs), (0, 0)))
    xb_s = None
    st_p = [[] for _ in range(5)]
    st_s = [[] for _ in range(5)]
    state_ssm4 = state_ssm.reshape(depth, bs, B_HEADS * B_HEADDIM, B_STATE)
    w_in_t = jnp.transpose(w_in, (2, 0, 1))
    w_main = _cast_w_main_call(w_in_t)
    w_tail = _pack_w_tail_call(w_in_t)
    for l in range(depth):
        lam_init = 0.8 - 0.6 * math.exp(-0.3 * l)
        w_out_b = w_out[l].astype(BF16)
        wq_t = peer_wq[l].T.astype(BF16)

        proj_s, dkv_s, nkv_s = _proj_call(xa_s, xb_s, norm_mix_g[l], w_main, w_tail, l, n_s)
        proj_s = proj_s[:bs]
        proj_s3 = proj_s.reshape(bs, 1, PROJ_W)
        o_a = _diff_sample_call(page_table, proj_s3, cache_diff_kv, l, diff_lam[l], diff_subln_g[l], lam_init)
        o_b, h_new, conv_new = _ssd_sample_call(proj_s3, state_conv, state_ssm4, l, ssm_conv_w[l], ssm_conv_b[l],
                                                ssm_dt_bias[l], ssm_a_log[l], ssm_d[l], ssm_norm_g[l])
        part = _compress_paged_call(page_table, cache_nsa_kv, l, nsa_pe[l], nsa_cmp_w1[l])
        o_cmp, sel = _nsa_select_call(part, nsa_cmp_w2[l], proj_s3, past_len)
        o_c, win_out = _nsa_attend_call(sel[:, 0, :SEL_TOPK], page_table, proj_s3, o_cmp, cache_nsa_kv, cache_nsa_win, l,
                                        nsa_out_g[l], past_len)
        st_s[0].append(dkv_s[:bs].reshape(bs, 1, 2, A_HEADS, A_DHEAD))
        st_s[1].append(nkv_s[:bs].reshape(bs, 1, NSA_ROWS, C_DHEAD))
        st_s[2].append(win_out)
        st_s[3].append(h_new.reshape(bs, B_HEADS, B_HEADDIM, B_STATE))
        st_s[4].append(conv_new)
        pad = lambda a: jnp.pad(a.reshape(bs, -1), ((0, n_s - bs), (0, 0)))
        xa_s, (xb_s, u_b, vt_b) = _token_mixer_tail(xa_s, xb_s, pad(o_a), pad(o_b), pad(o_c), w_out_b, norm_ffn_g[l],
                                                   wq_t, peer_subkeys[l], peer_u, peer_v, n_s, n_s, 512, True, l)

        proj, dkv_p, nkv_p = _proj_call(xa_p, xb_p, norm_mix_g[l], w_main, w_tail, l, tm_p)
        o_a = _diff_prompt_call(proj, bp, t, diff_lam[l], diff_subln_g[l], lam_init)
        o_b, h_ssm = _ssd_prompt_call(proj, bp, t, ssm_conv_w[l], ssm_conv_b[l], ssm_dt_bias[l], ssm_a_log[l],
                                      ssm_d[l], ssm_norm_g[l])
        kc, vc = _compress_prompt_call(proj, bp, t, nsa_pe[l], nsa_cmp_w1[l], nsa_cmp_w2[l])
        o_c = _nsa_prompt_call(proj, kc, vc, bp, t, nsa_out_g[l])
        proj3 = proj.reshape(bp, t, PROJ_W)
        st_p[0].append(dkv_p.reshape(bp, t, 2, A_HEADS, A_DHEAD))
        st_p[1].append(nkv_p.reshape(bp, t, NSA_ROWS, C_DHEAD))
        keep = min(WINDOW, t)
        st_p[2].append(proj3[:, t - keep:, COL_WIN:COL_WIN + 2 * C_DHEAD].reshape(bp, keep, 2, C_DHEAD))
        st_p[3].append(h_ssm)
        st_p[4].append(proj3[:, t - (CONV_W - 1):, COL_XBC:COL_XBC + B_CONV_DIM])
        xa_p, xb_p = _token_mixer_tail(xa_p, xb_p, o_a, o_b, o_c, w_out_b, norm_ffn_g[l], wq_t, peer_subkeys[l],
                                       u_b, vt_b, tm_p, 512, 1024)

    y_p = _final_norm_call(xa_p, xb_p, norm_final_g, tm_p).reshape(bp, t, D_MODEL)
    y_s = _final_norm_call(xa_s, xb_s, norm_final_g, n_s)[:bs].reshape(bs, 1, D_MODEL)
    return (y_p, y_s) + tuple(jnp.stack(s) for s in st_p) + tuple(jnp.stack(s) for s in st_s)
```
